```python
import math
import jax, jax.numpy as jnp
from jax import lax
import numpy as np

D_MODEL = 1024
BATCH = 16
SEQ = 2048
DEPTH = 2

N_A = max(1, DEPTH // 2)
N_B = DEPTH - N_A
N_HEADS = 16
HEAD_DIM = D_MODEL // N_HEADS
D_FF = 2816
CONV_WIDTH = 3
Q_BLOCK = 128
RMS_EPS = 1e-6

kernel_name = "yoco_shortconv_fox_macaron_sandwich"


def rms_norm(x, g):
    xf = x.astype(jnp.float32)
    y = xf * lax.rsqrt(jnp.mean(xf * xf, axis=-1, keepdims=True) + RMS_EPS)
    return (y * g.astype(jnp.float32)).astype(x.dtype)


def swiglu(x, w_in, w_out):
    gate, up = jnp.split(x @ w_in, 2, axis=-1)
    return (jax.nn.silu(gate) * up) @ w_out


def short_conv_mixer(x, w_in, conv_k, w_out):
    b_gate, c_gate, h = jnp.split(x @ w_in, 3, axis=-1)
    u = c_gate * h
    y = lax.conv_general_dilated(
        u, conv_k[:, None, :].astype(u.dtype),
        window_strides=(1,), padding=[(CONV_WIDTH - 1, 0)],
        dimension_numbers=("NWC", "WIO", "NWC"),
        feature_group_count=D_MODEL)
    return (b_gate * y) @ w_out


def shared_kv(x, kv_g, kv_w, forget_b):
    bsz, seq, _ = x.shape
    p = rms_norm(x, kv_g) @ kv_w
    k = p[..., :D_MODEL].reshape(bsz, seq, N_HEADS, HEAD_DIM).transpose(0, 2, 1, 3)
    v = p[..., D_MODEL:2 * D_MODEL].reshape(bsz, seq, N_HEADS, HEAD_DIM).transpose(0, 2, 1, 3)
    f_logit = (p[..., 2 * D_MODEL:] + forget_b).astype(jnp.float32)
    log_f = jax.nn.log_sigmoid(f_logit)
    c = jnp.cumsum(log_f, axis=1).transpose(0, 2, 1)
    return k, v, c


def forgetting_attention(x, k, v, c, w_qg, w_o):
    bsz, seq, _ = x.shape
    n_blk = seq // Q_BLOCK
    q, gate = jnp.split(x @ w_qg, 2, axis=-1)
    q = q.reshape(bsz, seq, N_HEADS, HEAD_DIM).transpose(0, 2, 1, 3)
    q_blocks = q.reshape(bsz, N_HEADS, n_blk, Q_BLOCK, HEAD_DIM).transpose(2, 0, 1, 3, 4)
    c_blocks = c.reshape(bsz, N_HEADS, n_blk, Q_BLOCK).transpose(2, 0, 1, 3)
    k_pos = jnp.arange(seq)
    scale = 1.0 / math.sqrt(HEAD_DIM)

    def attend_block(args):
        qb, cb, i = args
        s = jnp.einsum("bhqd,bhkd->bhqk", qb, k, preferred_element_type=jnp.float32) * scale
        s = s + cb[..., None] - c[:, :, None, :]
        q_pos = i * Q_BLOCK + jnp.arange(Q_BLOCK)
        s = jnp.where(k_pos[None, :] <= q_pos[:, None], s, -jnp.inf)
        p = jax.nn.softmax(s, axis=-1)
        return jnp.einsum("bhqk,bhkd->bhqd", p.astype(v.dtype), v)

    o = lax.map(attend_block, (q_blocks, c_blocks, jnp.arange(n_blk)))
    o = o.transpose(1, 0, 3, 2, 4).reshape(bsz, seq, D_MODEL)
    return (jax.nn.sigmoid(gate) * o) @ w_o


def _fwd_setup_inputs(seed: int = 0) -> dict:
    key = jax.random.key(seed)
    ks = jax.random.split(key, 24)
    f32 = jnp.float32

    def w(k, shape, fan_in):
        return jax.random.normal(k, shape, f32) * fan_in ** -0.5

    def gain(k, shape):
        return 1.0 + 0.05 * jax.random.normal(k, shape, f32)

    return {
        "x": jax.random.normal(ks[0], (BATCH, SEQ, D_MODEL), f32),
        "ffn1_pre_g": gain(ks[1], (DEPTH, D_MODEL)),
        "ffn1_post_g": gain(ks[2], (DEPTH, D_MODEL)),
        "ffn1_w_in": w(ks[3], (DEPTH, D_MODEL, 2 * D_FF), D_MODEL),
        "ffn1_w_out": w(ks[4], (DEPTH, D_FF, D_MODEL), D_FF),
        "mix_pre_g": gain(ks[5], (DEPTH, D_MODEL)),
        "mix_post_g": gain(ks[6], (DEPTH, D_MODEL)),
        "ffn2_pre_g": gain(ks[7], (DEPTH, D_MODEL)),
        "ffn2_post_g": gain(ks[8], (DEPTH, D_MODEL)),
        "ffn2_w_in": w(ks[9], (DEPTH, D_MODEL, 2 * D_FF), D_MODEL),
        "ffn2_w_out": w(ks[10], (DEPTH, D_FF, D_MODEL), D_FF),
        "conv_w_in": w(ks[11], (N_A, D_MODEL, 3 * D_MODEL), D_MODEL),
        "conv_k": w(ks[12], (N_A, CONV_WIDTH, D_MODEL), CONV_WIDTH),
        "conv_w_out": w(ks[13], (N_A, D_MODEL, D_MODEL), D_MODEL),
        "kv_g": gain(ks[14], (D_MODEL,)),
        "kv_w": w(ks[15], (D_MODEL, 2 * D_MODEL + N_HEADS), D_MODEL),
        "forget_b": jax.random.uniform(ks[16], (N_HEADS,), f32, 1.0, 3.0),
        "attn_w_qg": w(ks[17], (N_B, D_MODEL, 2 * D_MODEL), D_MODEL),
        "attn_w_o": w(ks[18], (N_B, D_MODEL, D_MODEL), D_MODEL),
    }


def _fwd_reference(x, ffn1_pre_g, ffn1_post_g, ffn1_w_in, ffn1_w_out, mix_pre_g, mix_post_g,
              ffn2_pre_g, ffn2_post_g, ffn2_w_in, ffn2_w_out, conv_w_in, conv_k, conv_w_out,
              kv_g, kv_w, forget_b, attn_w_qg, attn_w_o):
    k = v = c = None
    for l in range(DEPTH):
        if l == N_A:
            k, v, c = shared_kv(x, kv_g, kv_w, forget_b)
        h = swiglu(rms_norm(x, ffn1_pre_g[l]), ffn1_w_in[l], ffn1_w_out[l])
        x = x + 0.5 * rms_norm(h, ffn1_post_g[l])
        xn = rms_norm(x, mix_pre_g[l])
        if l < N_A:
            m = short_conv_mixer(xn, conv_w_in[l], conv_k[l], conv_w_out[l])
        else:
            j = l - N_A
            m = forgetting_attention(xn, k, v, c, attn_w_qg[j], attn_w_o[j])
        x = x + rms_norm(m, mix_post_g[l])
        h = swiglu(rms_norm(x, ffn2_pre_g[l]), ffn2_w_in[l], ffn2_w_out[l])
        x = x + 0.5 * rms_norm(h, ffn2_post_g[l])
    return x


import jax as _jax
import jax.numpy as _jnp

TWIN_FORMAT = 'train_step'
FWD_PARAMS = ['x', 'ffn1_pre_g', 'ffn1_post_g', 'ffn1_w_in', 'ffn1_w_out', 'mix_pre_g', 'mix_post_g', 'ffn2_pre_g', 'ffn2_post_g', 'ffn2_w_in', 'ffn2_w_out', 'conv_w_in', 'conv_k', 'conv_w_out', 'kv_g', 'kv_w', 'forget_b', 'attn_w_qg', 'attn_w_o']
TWIN_WEIGHTS = ['ffn1_pre_g', 'ffn1_post_g', 'ffn1_w_in', 'ffn1_w_out', 'mix_pre_g', 'mix_post_g', 'ffn2_pre_g', 'ffn2_post_g', 'ffn2_w_in', 'ffn2_w_out', 'conv_w_in', 'conv_k', 'conv_w_out', 'kv_g', 'kv_w', 'forget_b', 'attn_w_qg', 'attn_w_o']
TWIN_DIFF_INPUT = 'x'
TWIN_INPUTS = ['x', 'ffn1_pre_g', 'ffn1_post_g', 'ffn1_w_in', 'ffn1_w_out', 'mix_pre_g', 'mix_post_g', 'ffn2_pre_g', 'ffn2_post_g', 'ffn2_w_in', 'ffn2_w_out', 'conv_w_in', 'conv_k', 'conv_w_out', 'kv_g', 'kv_w', 'forget_b', 'attn_w_qg', 'attn_w_o', 'loss_target', 'm_ffn1_pre_g', 'm_ffn1_post_g', 'm_ffn1_w_in', 'm_ffn1_w_out', 'm_mix_pre_g', 'm_mix_post_g', 'm_ffn2_pre_g', 'm_ffn2_post_g', 'm_ffn2_w_in', 'm_ffn2_w_out', 'm_conv_w_in', 'm_conv_k', 'm_conv_w_out', 'm_kv_g', 'm_kv_w', 'm_forget_b', 'm_attn_w_qg', 'm_attn_w_o', 'v_ffn1_pre_g', 'v_ffn1_post_g', 'v_ffn1_w_in', 'v_ffn1_w_out', 'v_mix_pre_g', 'v_mix_post_g', 'v_ffn2_pre_g', 'v_ffn2_post_g', 'v_ffn2_w_in', 'v_ffn2_w_out', 'v_conv_w_in', 'v_conv_k', 'v_conv_w_out', 'v_kv_g', 'v_kv_w', 'v_forget_b', 'v_attn_w_qg', 'v_attn_w_o']
TWIN_OUTPUTS = ['loss', 'grad_x', 'grad_ffn1_pre_g', 'grad_ffn1_post_g', 'grad_ffn1_w_in', 'grad_ffn1_w_out', 'grad_mix_pre_g', 'grad_mix_post_g', 'grad_ffn2_pre_g', 'grad_ffn2_post_g', 'grad_ffn2_w_in', 'grad_ffn2_w_out', 'grad_conv_w_in', 'grad_conv_k', 'grad_conv_w_out', 'grad_kv_g', 'grad_kv_w', 'grad_forget_b', 'grad_attn_w_qg', 'grad_attn_w_o', 'delta_ffn1_pre_g', 'delta_ffn1_post_g', 'delta_ffn1_w_in', 'delta_ffn1_w_out', 'delta_mix_pre_g', 'delta_mix_post_g', 'delta_ffn2_pre_g', 'delta_ffn2_post_g', 'delta_ffn2_w_in', 'delta_ffn2_w_out', 'delta_conv_w_in', 'delta_conv_k', 'delta_conv_w_out', 'delta_kv_g', 'delta_kv_w', 'delta_forget_b', 'delta_attn_w_qg', 'delta_attn_w_o', 'new_m_ffn1_pre_g', 'new_m_ffn1_post_g', 'new_m_ffn1_w_in', 'new_m_ffn1_w_out', 'new_m_mix_pre_g', 'new_m_mix_post_g', 'new_m_ffn2_pre_g', 'new_m_ffn2_post_g', 'new_m_ffn2_w_in', 'new_m_ffn2_w_out', 'new_m_conv_w_in', 'new_m_conv_k', 'new_m_conv_w_out', 'new_m_kv_g', 'new_m_kv_w', 'new_m_forget_b', 'new_m_attn_w_qg', 'new_m_attn_w_o', 'new_v_ffn1_pre_g', 'new_v_ffn1_post_g', 'new_v_ffn1_w_in', 'new_v_ffn1_w_out', 'new_v_mix_pre_g', 'new_v_mix_post_g', 'new_v_ffn2_pre_g', 'new_v_ffn2_post_g', 'new_v_ffn2_w_in', 'new_v_ffn2_w_out', 'new_v_conv_w_in', 'new_v_conv_k', 'new_v_conv_w_out', 'new_v_kv_g', 'new_v_kv_w', 'new_v_forget_b', 'new_v_attn_w_qg', 'new_v_attn_w_o']
TWIN_LEAF_KINDS = {'loss': 'loss', 'grad_x': 'grad_x', 'grad_ffn1_pre_g': 'grad_w', 'grad_ffn1_post_g': 'grad_w', 'grad_ffn1_w_in': 'grad_w', 'grad_ffn1_w_out': 'grad_w', 'grad_mix_pre_g': 'grad_w', 'grad_mix_post_g': 'grad_w', 'grad_ffn2_pre_g': 'grad_w', 'grad_ffn2_post_g': 'grad_w', 'grad_ffn2_w_in': 'grad_w', 'grad_ffn2_w_out': 'grad_w', 'grad_conv_w_in': 'grad_w', 'grad_conv_k': 'grad_w', 'grad_conv_w_out': 'grad_w', 'grad_kv_g': 'grad_w', 'grad_kv_w': 'grad_w', 'grad_forget_b': 'grad_w', 'grad_attn_w_qg': 'grad_w', 'grad_attn_w_o': 'grad_w', 'delta_ffn1_pre_g': 'delta_w', 'delta_ffn1_post_g': 'delta_w', 'delta_ffn1_w_in': 'delta_w', 'delta_ffn1_w_out': 'delta_w', 'delta_mix_pre_g': 'delta_w', 'delta_mix_post_g': 'delta_w', 'delta_ffn2_pre_g': 'delta_w', 'delta_ffn2_post_g': 'delta_w', 'delta_ffn2_w_in': 'delta_w', 'delta_ffn2_w_out': 'delta_w', 'delta_conv_w_in': 'delta_w', 'delta_conv_k': 'delta_w', 'delta_conv_w_out': 'delta_w', 'delta_kv_g': 'delta_w', 'delta_kv_w': 'delta_w', 'delta_forget_b': 'delta_w', 'delta_attn_w_qg': 'delta_w', 'delta_attn_w_o': 'delta_w', 'new_m_ffn1_pre_g': 'new_m', 'new_m_ffn1_post_g': 'new_m', 'new_m_ffn1_w_in': 'new_m', 'new_m_ffn1_w_out': 'new_m', 'new_m_mix_pre_g': 'new_m', 'new_m_mix_post_g': 'new_m', 'new_m_ffn2_pre_g': 'new_m', 'new_m_ffn2_post_g': 'new_m', 'new_m_ffn2_w_in': 'new_m', 'new_m_ffn2_w_out': 'new_m', 'new_m_conv_w_in': 'new_m', 'new_m_conv_k': 'new_m', 'new_m_conv_w_out': 'new_m', 'new_m_kv_g': 'new_m', 'new_m_kv_w': 'new_m', 'new_m_forget_b': 'new_m', 'new_m_attn_w_qg': 'new_m', 'new_m_attn_w_o': 'new_m', 'new_v_ffn1_pre_g': 'new_v', 'new_v_ffn1_post_g': 'new_v', 'new_v_ffn1_w_in': 'new_v', 'new_v_ffn1_w_out': 'new_v', 'new_v_mix_pre_g': 'new_v', 'new_v_mix_post_g': 'new_v', 'new_v_ffn2_pre_g': 'new_v', 'new_v_ffn2_post_g': 'new_v', 'new_v_ffn2_w_in': 'new_v', 'new_v_ffn2_w_out': 'new_v', 'new_v_conv_w_in': 'new_v', 'new_v_conv_k': 'new_v', 'new_v_conv_w_out': 'new_v', 'new_v_kv_g': 'new_v', 'new_v_kv_w': 'new_v', 'new_v_forget_b': 'new_v', 'new_v_attn_w_qg': 'new_v', 'new_v_attn_w_o': 'new_v'}


def _forward(args):
    return _fwd_reference(*[args[k] for k in FWD_PARAMS])


def _output_shape():
    out = _jax.eval_shape(lambda: _forward(_fwd_setup_inputs(0)))
    return out.shape, out.dtype

N_MICROBATCH = 1
ADAM_LR = 0.001
ADAM_B1 = 0.9
ADAM_B2 = 0.999
ADAM_EPS = 1e-08
ADAM_WD = 0.01
ADAM_STEP = 10
PER_EXAMPLE_BATCH_AXIS = {'x': 0, 'loss_target': 0}
SHARED_INPUTS = []
_WEIGHT_DTYPES = {'ffn1_pre_g': _jnp.float32, 'ffn1_post_g': _jnp.float32, 'ffn1_w_in': _jnp.float32, 'ffn1_w_out': _jnp.float32, 'mix_pre_g': _jnp.float32, 'mix_post_g': _jnp.float32, 'ffn2_pre_g': _jnp.float32, 'ffn2_post_g': _jnp.float32, 'ffn2_w_in': _jnp.float32, 'ffn2_w_out': _jnp.float32, 'conv_w_in': _jnp.float32, 'conv_k': _jnp.float32, 'conv_w_out': _jnp.float32, 'kv_g': _jnp.float32, 'kv_w': _jnp.float32, 'forget_b': _jnp.float32, 'attn_w_qg': _jnp.float32, 'attn_w_o': _jnp.float32}
MOMENT_SCALE = {'ffn1_pre_g': 7.106312e-01, 'ffn1_post_g': 7.721366e+00, 'ffn1_w_in': 2.840035e-01, 'ffn1_w_out': 4.726127e-01, 'mix_pre_g': 8.240872e-01, 'mix_post_g': 3.198986e+01, 'ffn2_pre_g': 4.382752e-01, 'ffn2_post_g': 7.892493e+00, 'ffn2_w_in': 1.751395e-01, 'ffn2_w_out': 3.215936e-01, 'conv_w_in': 6.427400e-01, 'conv_k': 6.899735e-01, 'conv_w_out': 7.141220e-01, 'kv_g': 6.422178e-01, 'kv_w': 4.506594e-01, 'forget_b': 3.576714e+00, 'attn_w_qg': 2.373755e-01, 'attn_w_o': 5.774143e-01}


def _to_microbatches(a, axis):
    t = _jnp.moveaxis(a, axis, 0)
    t = t.reshape((N_MICROBATCH, t.shape[0] // N_MICROBATCH) + t.shape[1:])
    return _jnp.moveaxis(t, 1, axis + 1)


def setup_inputs(seed: int = 0) -> dict:
    inp = _fwd_setup_inputs(seed)
    key = _jax.random.fold_in(_jax.random.key(seed), 7919)
    shape, _ = _output_shape()
    out = dict(inp)
    out["loss_target"] = _jax.random.normal(_jax.random.fold_in(key, 0), shape, _jnp.float32)
    for i, name in enumerate(TWIN_WEIGHTS):
        w = inp[name].astype(_jnp.float32)
        if MOMENT_SCALE is None:
            s = _jnp.sqrt(_jnp.mean(_jnp.square(w)) + 1e-30)
        else:
            s = MOMENT_SCALE[name]
        km, kv = _jax.random.split(_jax.random.fold_in(key, i + 1))
        out[name] = w
        out["m_" + name] = s * _jax.random.normal(km, w.shape, _jnp.float32)
        out["v_" + name] = (s * s) * _jax.random.uniform(kv, w.shape, _jnp.float32, 0.5, 1.5)
    if N_MICROBATCH > 1:
        for name, axis in PER_EXAMPLE_BATCH_AXIS.items():
            out[name] = _to_microbatches(out[name], axis)
    return {'x': out['x'], 'ffn1_pre_g': out['ffn1_pre_g'], 'ffn1_post_g': out['ffn1_post_g'], 'ffn1_w_in': out['ffn1_w_in'], 'ffn1_w_out': out['ffn1_w_out'], 'mix_pre_g': out['mix_pre_g'], 'mix_post_g': out['mix_post_g'], 'ffn2_pre_g': out['ffn2_pre_g'], 'ffn2_post_g': out['ffn2_post_g'], 'ffn2_w_in': out['ffn2_w_in'], 'ffn2_w_out': out['ffn2_w_out'], 'conv_w_in': out['conv_w_in'], 'conv_k': out['conv_k'], 'conv_w_out': out['conv_w_out'], 'kv_g': out['kv_g'], 'kv_w': out['kv_w'], 'forget_b': out['forget_b'], 'attn_w_qg': out['attn_w_qg'], 'attn_w_o': out['attn_w_o'], 'loss_target': out['loss_target'], 'm_ffn1_pre_g': out['m_ffn1_pre_g'], 'm_ffn1_post_g': out['m_ffn1_post_g'], 'm_ffn1_w_in': out['m_ffn1_w_in'], 'm_ffn1_w_out': out['m_ffn1_w_out'], 'm_mix_pre_g': out['m_mix_pre_g'], 'm_mix_post_g': out['m_mix_post_g'], 'm_ffn2_pre_g': out['m_ffn2_pre_g'], 'm_ffn2_post_g': out['m_ffn2_post_g'], 'm_ffn2_w_in': out['m_ffn2_w_in'], 'm_ffn2_w_out': out['m_ffn2_w_out'], 'm_conv_w_in': out['m_conv_w_in'], 'm_conv_k': out['m_conv_k'], 'm_conv_w_out': out['m_conv_w_out'], 'm_kv_g': out['m_kv_g'], 'm_kv_w': out['m_kv_w'], 'm_forget_b': out['m_forget_b'], 'm_attn_w_qg': out['m_attn_w_qg'], 'm_attn_w_o': out['m_attn_w_o'], 'v_ffn1_pre_g': out['v_ffn1_pre_g'], 'v_ffn1_post_g': out['v_ffn1_post_g'], 'v_ffn1_w_in': out['v_ffn1_w_in'], 'v_ffn1_w_out': out['v_ffn1_w_out'], 'v_mix_pre_g': out['v_mix_pre_g'], 'v_mix_post_g': out['v_mix_post_g'], 'v_ffn2_pre_g': out['v_ffn2_pre_g'], 'v_ffn2_post_g': out['v_ffn2_post_g'], 'v_ffn2_w_in': out['v_ffn2_w_in'], 'v_ffn2_w_out': out['v_ffn2_w_out'], 'v_conv_w_in': out['v_conv_w_in'], 'v_conv_k': out['v_conv_k'], 'v_conv_w_out': out['v_conv_w_out'], 'v_kv_g': out['v_kv_g'], 'v_kv_w': out['v_kv_w'], 'v_forget_b': out['v_forget_b'], 'v_attn_w_qg': out['v_attn_w_qg'], 'v_attn_w_o': out['v_attn_w_o']}


def _loss(weights, diff, rest, loss_target):
    with _jax.named_scope("forward"):
        args = {**rest, TWIN_DIFF_INPUT: diff, **{k: w.astype(_WEIGHT_DTYPES[k]) for k, w in weights.items()}}
        y = _forward(args)
    with _jax.named_scope("loss_head"):
        err = _jnp.square(y.astype(_jnp.float32) - loss_target)
        return 0.5 * _jnp.sum(_jnp.mean(err, axis=-1)) if err.ndim else 0.5 * err


def _adamw(w, g, m, v):
    m = ADAM_B1 * m + (1.0 - ADAM_B1) * g
    v = ADAM_B2 * v + (1.0 - ADAM_B2) * _jnp.square(g)
    m_hat = m / (1.0 - ADAM_B1 ** ADAM_STEP)
    v_hat = v / (1.0 - ADAM_B2 ** ADAM_STEP)
    delta = -ADAM_LR * (m_hat / (_jnp.sqrt(v_hat) + ADAM_EPS) + ADAM_WD * w)
    return delta, m, v


def reference(x, ffn1_pre_g, ffn1_post_g, ffn1_w_in, ffn1_w_out, mix_pre_g, mix_post_g, ffn2_pre_g, ffn2_post_g, ffn2_w_in, ffn2_w_out, conv_w_in, conv_k, conv_w_out, kv_g, kv_w, forget_b, attn_w_qg, attn_w_o, loss_target, m_ffn1_pre_g, m_ffn1_post_g, m_ffn1_w_in, m_ffn1_w_out, m_mix_pre_g, m_mix_post_g, m_ffn2_pre_g, m_ffn2_post_g, m_ffn2_w_in, m_ffn2_w_out, m_conv_w_in, m_conv_k, m_conv_w_out, m_kv_g, m_kv_w, m_forget_b, m_attn_w_qg, m_attn_w_o, v_ffn1_pre_g, v_ffn1_post_g, v_ffn1_w_in, v_ffn1_w_out, v_mix_pre_g, v_mix_post_g, v_ffn2_pre_g, v_ffn2_post_g, v_ffn2_w_in, v_ffn2_w_out, v_conv_w_in, v_conv_k, v_conv_w_out, v_kv_g, v_kv_w, v_forget_b, v_attn_w_qg, v_attn_w_o):
    given = dict(x=x, ffn1_pre_g=ffn1_pre_g, ffn1_post_g=ffn1_post_g, ffn1_w_in=ffn1_w_in, ffn1_w_out=ffn1_w_out, mix_pre_g=mix_pre_g, mix_post_g=mix_post_g, ffn2_pre_g=ffn2_pre_g, ffn2_post_g=ffn2_post_g, ffn2_w_in=ffn2_w_in, ffn2_w_out=ffn2_w_out, conv_w_in=conv_w_in, conv_k=conv_k, conv_w_out=conv_w_out, kv_g=kv_g, kv_w=kv_w, forget_b=forget_b, attn_w_qg=attn_w_qg, attn_w_o=attn_w_o, loss_target=loss_target, m_ffn1_pre_g=m_ffn1_pre_g, m_ffn1_post_g=m_ffn1_post_g, m_ffn1_w_in=m_ffn1_w_in, m_ffn1_w_out=m_ffn1_w_out, m_mix_pre_g=m_mix_pre_g, m_mix_post_g=m_mix_post_g, m_ffn2_pre_g=m_ffn2_pre_g, m_ffn2_post_g=m_ffn2_post_g, m_ffn2_w_in=m_ffn2_w_in, m_ffn2_w_out=m_ffn2_w_out, m_conv_w_in=m_conv_w_in, m_conv_k=m_conv_k, m_conv_w_out=m_conv_w_out, m_kv_g=m_kv_g, m_kv_w=m_kv_w, m_forget_b=m_forget_b, m_attn_w_qg=m_attn_w_qg, m_attn_w_o=m_attn_w_o, v_ffn1_pre_g=v_ffn1_pre_g, v_ffn1_post_g=v_ffn1_post_g, v_ffn1_w_in=v_ffn1_w_in, v_ffn1_w_out=v_ffn1_w_out, v_mix_pre_g=v_mix_pre_g, v_mix_post_g=v_mix_post_g, v_ffn2_pre_g=v_ffn2_pre_g, v_ffn2_post_g=v_ffn2_post_g, v_ffn2_w_in=v_ffn2_w_in, v_ffn2_w_out=v_ffn2_w_out, v_conv_w_in=v_conv_w_in, v_conv_k=v_conv_k, v_conv_w_out=v_conv_w_out, v_kv_g=v_kv_g, v_kv_w=v_kv_w, v_forget_b=v_forget_b, v_attn_w_qg=v_attn_w_qg, v_attn_w_o=v_attn_w_o)
    weights = {n: given[n] for n in TWIN_WEIGHTS}
    shared = {n: given[n] for n in SHARED_INPUTS}
    per_example = {n: given[n] for n in ['x']}
    grad_fn = _jax.value_and_grad(_loss, argnums=(0, 1))

    def one_microbatch(ex, loss_target):
        ex = dict(ex)
        diff = ex.pop(TWIN_DIFF_INPUT)
        return grad_fn(weights, diff, {**shared, **ex}, loss_target)

    if N_MICROBATCH == 1:
        loss, (grad_w, grad_x) = one_microbatch(per_example, given["loss_target"])
    else:
        def body(carry, xs):
            loss_sum, grad_sum = carry
            l_k, (gw_k, gx_k) = one_microbatch(xs[0], xs[1])
            with _jax.named_scope("update"):
                return (loss_sum + l_k, _jax.tree.map(_jnp.add, grad_sum, gw_k)), gx_k

        init = (_jnp.zeros((), _jnp.float32), _jax.tree.map(_jnp.zeros_like, weights))
        (loss, grad_w), grad_x = _jax.lax.scan(body, init, (per_example, given["loss_target"]))
    with _jax.named_scope("update"):
        delta_w, new_m, new_v = {}, {}, {}
        for n in TWIN_WEIGHTS:
            delta_w[n], new_m[n], new_v[n] = _adamw(weights[n], grad_w[n], given["m_" + n], given["v_" + n])
    return (loss, grad_x, *[grad_w[n] for n in TWIN_WEIGHTS], *[delta_w[n] for n in TWIN_WEIGHTS],
            *[new_m[n] for n in TWIN_WEIGHTS], *[new_v[n] for n in TWIN_WEIGHTS])
```

```python
import functools
import math

import jax
import jax.numpy as jnp
from jax import lax
from jax.experimental import pallas as pl
from jax.experimental.pallas import tpu as pltpu

F32, BF16 = jnp.float32, jnp.bfloat16
D = 1024
N_HEADS = 16
HEAD_DIM = 64
N_DEV = 8
RMS_EPS = 1e-6
ATT_SCALE = 1.0 / math.sqrt(HEAD_DIM)
LANES = 128
HALO = 8
TM = 512
TQ = 256
VMEM_LIMIT = 48 * 1024 * 1024
MESH = pl.DeviceIdType.MESH

ADAM_LR, ADAM_B1, ADAM_B2, ADAM_EPS, ADAM_WD, ADAM_STEP = 0.001, 0.9, 0.999, 1e-08, 0.01, 10

NT = (((1,), (1,)), ((), ()))
TN = (((0,), (0,)), ((), ()))


def _params(n_axes):
    return pltpu.CompilerParams(dimension_semantics=("arbitrary",) * n_axes, vmem_limit_bytes=VMEM_LIMIT)


def _tile(n, cap, mult):
    best = None
    for t in range(mult, min(n, cap) + 1, mult):
        if n % t == 0:
            best = t
    assert best is not None, (n, cap, mult)
    return best


def _rms_rstd(x):
    return lax.rsqrt(jnp.mean(x * x, axis=-1, keepdims=True) + RMS_EPS)


def _rms_fwd(x, g):
    return x * _rms_rstd(x) * g


def _rms_bwd(x, g, dy):
    xh = x * _rms_rstd(x)
    dyg = dy * g
    dx = _rms_rstd(x) * (dyg - xh * jnp.mean(dyg * xh, axis=-1, keepdims=True))
    return dx, jnp.sum(dy * xh, axis=0, keepdims=True)


def _accumulate(ref, first, value):
    @pl.when(first)
    def _():
        ref[...] = value

    @pl.when(jnp.logical_not(first))
    def _():
        ref[...] += value


def rms_proj(x, g, ws, out_dtypes, name):
    T = x.shape[0]
    tm = _tile(T, TM, 16)
    na = len(ws)
    nbs = [w.shape[0] for w, _ in ws]
    offs = [sum(nbs[:a]) for a in range(na)]
    nb_total = sum(nbs)

    def body(x_ref, g_ref, *refs):
        w_refs, xn_ref, o_refs, xn_s = refs[:na], refs[na], refs[na + 1:2 * na + 1], refs[2 * na + 1]
        n = pl.program_id(1)

        @pl.when(n == 0)
        def _():
            xn = _rms_fwd(x_ref[...], g_ref[...]).astype(BF16)
            xn_s[...] = xn
            xn_ref[...] = xn

        for a in range(na):
            @pl.when((n >= offs[a]) & (n < offs[a] + nbs[a]))
            def _(a=a):
                y = jnp.dot(xn_s[...], w_refs[a][0, 0], preferred_element_type=F32)
                o_refs[a][0] = y.astype(o_refs[a].dtype)

    def blk(off, nb):
        return lambda n: jnp.clip(n - off, 0, nb - 1)

    in_specs = [pl.BlockSpec((tm, D), lambda i, n: (i, 0)), pl.BlockSpec((1, D), lambda i, n: (0, 0))]
    out_specs = [pl.BlockSpec((tm, D), lambda i, n: (i, 0))]
    out_shape = [jax.ShapeDtypeStruct((T, D), BF16)]
    for (w, l), off, nb, dt in zip(ws, offs, nbs, out_dtypes):
        wb = w.shape[3]
        in_specs.append(pl.BlockSpec((1, 1, D, wb), lambda i, n, f=blk(off, nb), l=l: (f(n), l, 0, 0)))
        out_specs.append(pl.BlockSpec((1, tm, wb), lambda i, n, f=blk(off, nb): (f(n), i, 0)))
        out_shape.append(jax.ShapeDtypeStruct((nb, T, wb), dt))
    res = pl.pallas_call(
        body, name=name, grid=(T // tm, nb_total), in_specs=in_specs, out_specs=out_specs, out_shape=out_shape,
        scratch_shapes=[pltpu.VMEM((tm, D), BF16)], compiler_params=_params(2),
    )(x, g, *[w for w, _ in ws])
    return res[0], res[1:]


def mix_out(pro, pro_inputs, pro_specs, w, res, g_post, alpha, nk, name, tm=None):
    w4, l = w
    T = res.shape[0]
    tm = _tile(T, TM, 16) if tm is None else tm
    dpb = N_DEV // nk
    rows = w4.shape[2]
    kb = dpb * rows
    npi = len(pro_inputs)

    def body(*refs):
        pro_refs = refs[:npi]
        w_ref, res_ref, g_ref, z_ref, m_ref, y_ref, acc = refs[npi:]
        i, k = pl.program_id(0), pl.program_id(1)
        z = pro(i, k, *pro_refs).astype(BF16)
        z_ref[0] = z
        part = jnp.dot(z, w_ref[:, 0].reshape(kb, D), preferred_element_type=F32)
        _accumulate(acc, k == 0, part)

        @pl.when(k == nk - 1)
        def _():
            m = acc[...]
            m_ref[...] = m
            y_ref[...] = res_ref[...] + alpha * _rms_fwd(m, g_ref[...])

    row = pl.BlockSpec((tm, D), lambda i, k: (i, 0))
    in_specs = list(pro_specs) + [
        pl.BlockSpec((dpb, 1, rows, D), lambda i, k: (k, l, 0, 0)), row, pl.BlockSpec((1, D), lambda i, k: (0, 0))]
    z, m, y = pl.pallas_call(
        body, name=name, grid=(T // tm, nk), in_specs=in_specs,
        out_specs=[pl.BlockSpec((1, tm, kb), lambda i, k: (k, i, 0)), row, row],
        out_shape=[jax.ShapeDtypeStruct((nk, T, kb), BF16), jax.ShapeDtypeStruct((T, D), F32),
                   jax.ShapeDtypeStruct((T, D), F32)],
        scratch_shapes=[pltpu.VMEM((tm, D), F32)], compiler_params=_params(2),
    )(*pro_inputs, w4, res, g_post)
    return z, m, y


def post_bwd(dy, m, g_post, alpha, w, nk, name, gu=None):
    w4, l = w
    T = dy.shape[0]
    tm = _tile(T, TM, 16)
    dpb = N_DEV // nk
    rows = w4.shape[2]
    kb = dpb * rows
    swiglu = gu is not None

    def body(dy_ref, m_ref, g_ref, w_ref, *refs):
        if swiglu:
            gate_ref, up_ref, dm_ref, dg_ref, dgate_ref, dup_ref, dm_s = refs
        else:
            dm_ref, dg_ref, dz_ref, dm_s = refs
        i, k = pl.program_id(0), pl.program_id(1)

        @pl.when(k == 0)
        def _():
            dm, dg = _rms_bwd(m_ref[...], g_ref[...], alpha * dy_ref[...])
            dm_s[...] = dm.astype(BF16)
            dm_ref[...] = dm.astype(BF16)
            _accumulate(dg_ref, i == 0, dg)

        dz = lax.dot_general(dm_s[...], w_ref[:, 0].reshape(kb, D), NT, preferred_element_type=F32)
        if swiglu:
            gate, up = gate_ref[0].astype(F32), up_ref[0].astype(F32)
            sig = jax.nn.sigmoid(gate)
            dgate_ref[0] = (dz * up * sig * (1.0 + gate * (1.0 - sig))).astype(BF16)
            dup_ref[0] = (dz * gate * sig).astype(BF16)
        else:
            dz_ref[0] = dz.astype(BF16)

    row = pl.BlockSpec((tm, D), lambda i, k: (i, 0))
    vec = pl.BlockSpec((1, D), lambda i, k: (0, 0))
    blkk = pl.BlockSpec((1, tm, kb), lambda i, k: (k, i, 0))
    in_specs = [row, row, vec, pl.BlockSpec((dpb, 1, rows, D), lambda i, k: (k, l, 0, 0))]
    inputs = [dy, m, g_post, w4]
    out_specs = [row, vec, blkk]
    out_shape = [jax.ShapeDtypeStruct((T, D), BF16), jax.ShapeDtypeStruct((1, D), F32),
                 jax.ShapeDtypeStruct((nk, T, kb), BF16)]
    if swiglu:
        in_specs += [blkk, pl.BlockSpec((1, tm, kb), lambda i, k: (k + nk, i, 0))]
        inputs += [gu, gu]
        out_specs.append(blkk)
        out_shape.append(jax.ShapeDtypeStruct((nk, T, kb), BF16))
    return pl.pallas_call(
        body, name=name, grid=(T // tm, nk), in_specs=in_specs, out_specs=out_specs, out_shape=out_shape,
        scratch_shapes=[pltpu.VMEM((tm, D), BF16)], compiler_params=_params(2),
    )(*inputs)


def pre_bwd(pieces, x, g_pre, dres, name):
    T = x.shape[0]
    tm = _tile(T, TM, 16)
    na = len(pieces)
    nbs = [p[0].shape[0] for p in pieces]
    offs = [sum(nbs[:a]) for a in range(na)]
    nb_total = sum(nbs)

    def body(*refs):
        dz_refs, w_refs = refs[:na], refs[na:2 * na]
        x_ref, g_ref, dres_ref, dx_ref, dg_ref, acc = refs[2 * na:]
        i, n = pl.program_id(0), pl.program_id(1)
        for a in range(na):
            @pl.when((n >= offs[a]) & (n < offs[a] + nbs[a]))
            def _(a=a):
                part = lax.dot_general(dz_refs[a][0].astype(BF16), w_refs[a][0, 0], NT, preferred_element_type=F32)
                _accumulate(acc, n == 0, part)

        @pl.when(n == nb_total - 1)
        def _():
            dx, dg = _rms_bwd(x_ref[...], g_ref[...], acc[...])
            dx_ref[...] = dres_ref[...] + dx
            _accumulate(dg_ref, i == 0, dg)

    def blk(off, nb):
        return lambda n: jnp.clip(n - off, 0, nb - 1)

    row = pl.BlockSpec((tm, D), lambda i, n: (i, 0))
    vec = pl.BlockSpec((1, D), lambda i, n: (0, 0))
    dz_specs, w_specs = [], []
    for (dz, (w4, l), w_off), off, nb in zip(pieces, offs, nbs):
        wb = dz.shape[2]
        dz_specs.append(pl.BlockSpec((1, tm, wb), lambda i, n, f=blk(off, nb): (f(n), i, 0)))
        w_specs.append(pl.BlockSpec((1, 1, D, wb), lambda i, n, f=blk(off, nb), l=l, o=w_off: (o + f(n), l, 0, 0)))
    return pl.pallas_call(
        body, name=name, grid=(T // tm, nb_total), in_specs=dz_specs + w_specs + [row, vec, row],
        out_specs=[row, vec], out_shape=[jax.ShapeDtypeStruct((T, D), F32), jax.ShapeDtypeStruct((1, D), F32)],
        scratch_shapes=[pltpu.VMEM((tm, D), F32)], compiler_params=_params(2),
    )(*[p[0] for p in pieces], *[p[1][0] for p in pieces], x, g_pre, dres)


def wgrad(a_list, b_list, nout, out4_shape, l, name, into=None):
    T = a_list[0].shape[1]
    tm = _tile(T, TM, 16)
    dpb = out4_shape[0] // nout
    _, _, R, C = out4_shape
    na, nb = len(a_list), len(b_list)

    def spans(arrs):
        ns = [a.shape[0] for a in arrs]
        return ns, [sum(ns[:k]) for k in range(len(ns))], sum(ns)

    a_ns, a_offs, a_tot = spans(a_list)
    b_ns, b_offs, b_tot = spans(b_list)
    assert a_tot in (1, nout) and b_tot in (1, nout)

    def body(*refs):
        a_refs, b_refs = refs[:na], refs[na:na + nb]
        out_ref = refs[-1]
        p, t = pl.program_id(0), pl.program_id(1)
        for ia in range(na):
            for ib in range(nb):
                conds = []
                if a_tot > 1:
                    conds += [p >= a_offs[ia], p < a_offs[ia] + a_ns[ia]]
                if b_tot > 1:
                    conds += [p >= b_offs[ib], p < b_offs[ib] + b_ns[ib]]

                def work(ia=ia, ib=ib):
                    part = lax.dot_general(a_refs[ia][0], b_refs[ib][0], TN, preferred_element_type=F32)
                    _accumulate(out_ref, t == 0, part.reshape(dpb, 1, R, C))

                if conds:
                    pl.when(functools.reduce(jnp.logical_and, conds))(work)
                else:
                    work()

    def blk(off, n, tot):
        if tot == 1:
            return lambda p: 0
        return lambda p: jnp.clip(p - off, 0, n - 1)

    in_specs = []
    for arrs, ns, offs, tot in ((a_list, a_ns, a_offs, a_tot), (b_list, b_ns, b_offs, b_tot)):
        for arr, n, off in zip(arrs, ns, offs):
            in_specs.append(pl.BlockSpec((1, tm, arr.shape[2]), lambda p, t, f=blk(off, n, tot): (f(p), t, 0)))
    inputs = list(a_list) + list(b_list)
    aliases = {}
    if into is not None:
        in_specs.append(pl.BlockSpec(memory_space=pl.ANY))
        aliases = {len(inputs): 0}
        inputs.append(into)
    return pl.pallas_call(
        body, name=name, grid=(nout, T // tm), in_specs=in_specs,
        out_specs=pl.BlockSpec((dpb, 1, R, C), lambda p, t: (p, l, 0, 0)),
        out_shape=jax.ShapeDtypeStruct(out4_shape, F32), input_output_aliases=aliases,
        compiler_params=_params(2),
    )(*inputs)


def _swiglu_pro(i, k, gate_ref, up_ref):
    gate, up = gate_ref[0].astype(F32), up_ref[0].astype(F32)
    return gate * jax.nn.sigmoid(gate) * up


def _attn_gate_pro(i, k, gate_ref, o_ref):
    return jax.nn.sigmoid(gate_ref[0].astype(F32)) * o_ref[...].astype(F32)


def _shift_rows(u, halo, d):
    rolled = pltpu.roll(u, d, 0)
    row = lax.broadcasted_iota(jnp.int32, u.shape, 0)
    for r in range(d):
        rolled = jnp.where(row == r, halo[HALO - d + r:HALO - d + r + 1, :], rolled)
    return rolled


def _advance_rows(u, halo, d):
    n = u.shape[0]
    rolled = pltpu.roll(u, n - d, 0)
    row = lax.broadcasted_iota(jnp.int32, u.shape, 0)
    for r in range(d):
        rolled = jnp.where(row == n - d + r, halo[r:r + 1, :], rolled)
    return rolled


def _make_conv_pro(tiles_per_seq):
    def pro(i, k, b_ref, c_ref, h_ref, ch_ref, hh_ref, ck_ref):
        u = c_ref[0].astype(F32) * h_ref[0].astype(F32)
        first = (i % tiles_per_seq) == 0
        halo = jnp.where(first, 0.0, ch_ref[0].astype(F32) * hh_ref[0].astype(F32))
        ck = ck_ref[...]
        conv = ck[2:3, :] * u + ck[1:2, :] * _shift_rows(u, halo, 1) + ck[0:1, :] * _shift_rows(u, halo, 2)
        return b_ref[0].astype(F32) * conv
    return pro


def conv_bwd_mix(dz, bch, conv_k, seq_len, name):
    T = dz.shape[1]
    tm = _tile(seq_len, TM, 16)
    tps = seq_len // tm
    nt = T // tm
    hb = tm // HALO

    def body(dz_ref, b_ref, c_ref, h_ref, cp_ref, hp_ref, dzn_ref, bn_ref, ck_ref, dbch_ref, dk_ref):
        i = pl.program_id(0)
        first = (i % tps) == 0
        last = (i % tps) == tps - 1
        b, c, h = b_ref[0].astype(F32), c_ref[0].astype(F32), h_ref[0].astype(F32)
        dzt = dz_ref[0].astype(F32)
        u = c * h
        prev = jnp.where(first, 0.0, cp_ref[0].astype(F32) * hp_ref[0].astype(F32))
        u1, u2 = _shift_rows(u, prev, 1), _shift_rows(u, prev, 2)
        ck = ck_ref[...]
        conv = ck[2:3, :] * u + ck[1:2, :] * u1 + ck[0:1, :] * u2
        dconv = dzt * b
        nxt = jnp.where(last, 0.0, dzn_ref[0].astype(F32) * bn_ref[0].astype(F32))
        du = ck[2:3, :] * dconv + ck[1:2, :] * _advance_rows(dconv, nxt, 1) + ck[0:1, :] * _advance_rows(dconv, nxt, 2)
        dbch_ref[0] = (dzt * conv).astype(BF16)
        dbch_ref[1] = (du * h).astype(BF16)
        dbch_ref[2] = (du * c).astype(BF16)
        tap = lax.broadcasted_iota(jnp.int32, (3, D), 0)
        dk = jnp.where(tap == 0, jnp.sum(dconv * u2, axis=0, keepdims=True),
                       jnp.where(tap == 1, jnp.sum(dconv * u1, axis=0, keepdims=True),
                                 jnp.sum(dconv * u, axis=0, keepdims=True)))
        _accumulate(dk_ref, i == 0, dk)

    def piece(p):
        return pl.BlockSpec((1, tm, D), lambda i, p=p: (p, i, 0))

    def prev(p):
        return pl.BlockSpec((1, HALO, D), lambda i, p=p: (p, jnp.maximum(i * hb - 1, 0), 0))

    def nxt(p):
        return pl.BlockSpec((1, HALO, D), lambda i, p=p: (p, jnp.minimum((i + 1) * hb, nt * hb - 1), 0))

    return pl.pallas_call(
        body, name=name, grid=(nt,),
        in_specs=[piece(0), piece(0), piece(1), piece(2), prev(1), prev(2), nxt(0), nxt(0),
                  pl.BlockSpec((3, D), lambda i: (0, 0))],
        out_specs=[pl.BlockSpec((3, tm, D), lambda i: (0, i, 0)), pl.BlockSpec((3, D), lambda i: (0, 0))],
        out_shape=[jax.ShapeDtypeStruct((3, T, D), BF16), jax.ShapeDtypeStruct((3, D), F32)],
        compiler_params=_params(1),
    )(dz, bch, bch, bch, bch, bch, dz, bch, conv_k)


def _log_sigmoid(x):
    return jnp.minimum(x, 0.0) - jnp.log(1.0 + jnp.exp(-jnp.abs(x)))


def forget_fwd(fl, fb, seq_len, name):
    T = fl.shape[1]

    def body(fl_ref, fb_ref, c_ref):
        c = _log_sigmoid(fl_ref[0] + fb_ref[...])
        row = lax.broadcasted_iota(jnp.int32, c.shape, 0)
        k = 1
        while k < seq_len:
            c = c + jnp.where(row >= k, pltpu.roll(c, k, 0), 0.0)
            k *= 2
        c_ref[...] = c

    return pl.pallas_call(
        body, name=name, grid=(T // seq_len,),
        in_specs=[pl.BlockSpec((1, seq_len, LANES), lambda b: (0, b, 0)), pl.BlockSpec((1, LANES), lambda b: (0, 0))],
        out_specs=pl.BlockSpec((seq_len, LANES), lambda b: (b, 0)),
        out_shape=jax.ShapeDtypeStruct((T, LANES), F32), compiler_params=_params(1),
    )(fl, fb)


def forget_bwd(dc, fl, fb, seq_len, name):
    T = dc.shape[0]

    def body(dc_ref, fl_ref, fb_ref, dfl_ref, dfb_ref):
        b = pl.program_id(0)
        r = dc_ref[...]
        row = lax.broadcasted_iota(jnp.int32, r.shape, 0)
        k = 1
        while k < seq_len:
            r = r + jnp.where(row < seq_len - k, pltpu.roll(r, seq_len - k, 0), 0.0)
            k *= 2
        dfl = r * jax.nn.sigmoid(-(fl_ref[0] + fb_ref[...]))
        dfl_ref[0] = dfl
        _accumulate(dfb_ref, b == 0, jnp.sum(dfl, axis=0, keepdims=True))

    return pl.pallas_call(
        body, name=name, grid=(T // seq_len,),
        in_specs=[pl.BlockSpec((seq_len, LANES), lambda b: (b, 0)), pl.BlockSpec((1, seq_len, LANES), lambda b: (0, b, 0)),
                  pl.BlockSpec((1, LANES), lambda b: (0, 0))],
        out_specs=[pl.BlockSpec((1, seq_len, LANES), lambda b: (0, b, 0)), pl.BlockSpec((1, LANES), lambda b: (0, 0))],
        out_shape=[jax.ShapeDtypeStruct((1, T, LANES), F32), jax.ShapeDtypeStruct((1, LANES), F32)],
        compiler_params=_params(1),
    )(dc, fl, fb)


def attn_fwd(qg, kv, cb, crow, n_seq, seq_len, name):
    T = n_seq * seq_len
    tq = _tile(seq_len, TQ, LANES)
    nq = seq_len // tq

    def body(q_ref, k_ref, v_ref, cb_ref, cr_ref, o_ref, lse_ref):
        i = pl.program_id(2)
        row = i * tq + lax.broadcasted_iota(jnp.int32, (tq, tq), 0)
        col0 = lax.broadcasted_iota(jnp.int32, (tq, tq), 1)
        outs, lses = [], []
        for hh in range(2):
            sl = slice(HEAD_DIM * hh, HEAD_DIM * (hh + 1))
            q = q_ref[0, :, sl]
            cq = cb_ref[:, HEAD_DIM * hh:HEAD_DIM * hh + 1]

            def step(j, carry, sl=sl, q=q, cq=cq, hh=hh):
                m, l, acc = carry
                rows = pl.ds(pl.multiple_of(j * tq, tq), tq)
                k, v = k_ref[0, rows, sl], v_ref[0, rows, sl]
                ck = cr_ref[0, 0, j, hh:hh + 1, :]
                s = lax.dot_general(q, k, NT, preferred_element_type=F32) * ATT_SCALE + cq - ck
                s = jnp.where(j * tq + col0 <= row, s, -1e30)
                m_new = jnp.maximum(m, jnp.max(s, axis=-1, keepdims=True))
                a = jnp.exp(m - m_new)
                p = jnp.exp(s - m_new)
                l = a * l + jnp.sum(p, axis=-1, keepdims=True)
                acc = a * acc + jnp.dot(p.astype(BF16), v, preferred_element_type=F32)
                return m_new, l, acc

            init = (jnp.full((tq, 1), -1e30, F32), jnp.zeros((tq, 1), F32), jnp.zeros((tq, HEAD_DIM), F32))
            m, l, acc = lax.fori_loop(0, i + 1, step, init)
            outs.append(acc / l)
            lses.append(jnp.broadcast_to(m + jnp.log(l), (tq, HEAD_DIM)))
        o_ref[...] = jnp.concatenate(outs, axis=-1)
        lse_ref[...] = jnp.concatenate(lses, axis=-1)

    qspec = pl.BlockSpec((tq, LANES), lambda b, hp, i: (b * nq + i, hp))
    return pl.pallas_call(
        body, name=name, grid=(n_seq, N_HEADS // 2, nq),
        in_specs=[pl.BlockSpec((1, tq, LANES), lambda b, hp, i: (0, b * nq + i, hp)),
                  pl.BlockSpec((1, seq_len, LANES), lambda b, hp, i: (0, b, hp)),
                  pl.BlockSpec((1, seq_len, LANES), lambda b, hp, i: (1, b, hp)),
                  qspec, pl.BlockSpec((1, 1, nq, 2, tq), lambda b, hp, i: (b, hp, 0, 0, 0))],
        out_specs=[qspec, qspec],
        out_shape=[jax.ShapeDtypeStruct((T, D), F32), jax.ShapeDtypeStruct((T, D), F32)],
        compiler_params=_params(3),
    )(qg, kv, kv, cb, crow)


def attn_bwd(dz, qg, kv, o, lse, cb, crow, n_seq, seq_len, name):
    T = n_seq * seq_len
    tq = _tile(seq_len, TQ, LANES)
    nq = seq_len // tq

    def body(dz_ref, q_ref, gate_ref, o_ref, lse_ref, cb_ref, cr_ref, k_ref, v_ref,
             dq_ref, dgate_ref, dk_ref, dv_ref, dcr_ref, p_s, dp_s, dk_s, dv_s):
        i = pl.program_id(2)

        @pl.when(i == 0)
        def _():
            dk_s[...] = jnp.zeros_like(dk_s)
            dv_s[...] = jnp.zeros_like(dv_s)
            dcr_ref[...] = jnp.zeros_like(dcr_ref)

        dzf = dz_ref[...].astype(F32)
        sig = jax.nn.sigmoid(gate_ref[0].astype(F32))
        dob = (dzf * sig).astype(BF16)
        dgate_ref[0] = (dzf * o_ref[...] * sig * (1.0 - sig)).astype(BF16)
        row = i * tq + lax.broadcasted_iota(jnp.int32, (tq, tq), 0)
        col0 = lax.broadcasted_iota(jnp.int32, (tq, tq), 1)
        dqs = []
        for hh in range(2):
            sl = slice(HEAD_DIM * hh, HEAD_DIM * (hh + 1))
            q, do = q_ref[0, :, sl], dob[:, sl]
            lse_i = lse_ref[:, HEAD_DIM * hh:HEAD_DIM * hh + 1]
            cq = cb_ref[:, HEAD_DIM * hh:HEAD_DIM * hh + 1]

            def probs(j, dsum, sl=sl, q=q, do=do, lse_i=lse_i, cq=cq, hh=hh):
                rows = pl.ds(pl.multiple_of(j * tq, tq), tq)
                s = lax.dot_general(q, k_ref[0, rows, sl], NT, preferred_element_type=F32) * ATT_SCALE
                s = s + cq - cr_ref[0, 0, j, hh:hh + 1, :]
                p = jnp.where(j * tq + col0 <= row, jnp.exp(s - lse_i), 0.0)
                dp = lax.dot_general(do, v_ref[0, rows, sl], NT, preferred_element_type=F32)
                p_s[j] = p
                dp_s[j] = dp
                return dsum + jnp.sum(p * dp, axis=-1, keepdims=True)

            dsum = lax.fori_loop(0, i + 1, probs, jnp.zeros((tq, 1), F32))

            def grads(j, dq, sl=sl, q=q, do=do, dsum=dsum, hh=hh):
                rows = pl.ds(pl.multiple_of(j * tq, tq), tq)
                p = p_s[j]
                ds = p * (dp_s[j] - dsum)
                dcr_ref[0, 0, j, hh:hh + 1, :] -= jnp.sum(ds, axis=0, keepdims=True)
                dsb = ds.astype(BF16)
                dk_s[rows, sl] += lax.dot_general(dsb, q, TN, preferred_element_type=F32) * ATT_SCALE
                dv_s[rows, sl] += lax.dot_general(p.astype(BF16), do, TN, preferred_element_type=F32)
                return dq + jnp.dot(dsb, k_ref[0, rows, sl], preferred_element_type=F32) * ATT_SCALE

            dqs.append(lax.fori_loop(0, i + 1, grads, jnp.zeros((tq, HEAD_DIM), F32)))
        dq_ref[0] = jnp.concatenate(dqs, axis=-1).astype(BF16)

        @pl.when(i == nq - 1)
        def _():
            dk_ref[0] = dk_s[...].astype(BF16)
            dv_ref[0] = dv_s[...].astype(BF16)

    qry2 = pl.BlockSpec((tq, LANES), lambda b, hp, i: (b * nq + i, hp))

    def qry3(p):
        return pl.BlockSpec((1, tq, LANES), lambda b, hp, i, p=p: (p, b * nq + i, hp))

    def seq3(p):
        return pl.BlockSpec((1, seq_len, LANES), lambda b, hp, i, p=p: (p, b, hp))

    crs = pl.BlockSpec((1, 1, nq, 2, tq), lambda b, hp, i: (b, hp, 0, 0, 0))
    act = jax.ShapeDtypeStruct((1, T, D), BF16)
    return pl.pallas_call(
        body, name=name, grid=(n_seq, N_HEADS // 2, nq),
        in_specs=[qry2, qry3(0), qry3(1), qry2, qry2, qry2, crs, seq3(0), seq3(1)],
        out_specs=[qry3(0), qry3(0), seq3(0), seq3(0), crs],
        out_shape=[act, act, act, act, jax.ShapeDtypeStruct((n_seq, N_HEADS // 2, nq, 2, tq), F32)],
        scratch_shapes=[pltpu.VMEM((nq, tq, tq), F32), pltpu.VMEM((nq, tq, tq), F32),
                        pltpu.VMEM((seq_len, LANES), F32), pltpu.VMEM((seq_len, LANES), F32)],
        compiler_params=_params(3),
    )(dz, qg, qg, o, lse, cb, crow, kv, kv)


def loss_head(y, target, name):
    T = y.shape[0]
    tm = _tile(T, TM, 8)

    def body(y_ref, t_ref, dy_ref, loss_ref):
        err = y_ref[...] - t_ref[...]
        dy_ref[...] = err * (1.0 / D)
        part = 0.5 * jnp.sum(jnp.mean(err * err, axis=-1, keepdims=True), axis=0, keepdims=True)
        _accumulate(loss_ref, pl.program_id(0) == 0, jnp.broadcast_to(part, (1, LANES)))

    row = pl.BlockSpec((tm, D), lambda i: (i, 0))
    return pl.pallas_call(
        body, name=name, grid=(T // tm,), in_specs=[row, row],
        out_specs=[row, pl.BlockSpec((1, LANES), lambda i: (0, 0))],
        out_shape=[jax.ShapeDtypeStruct((T, D), F32), jax.ShapeDtypeStruct((1, LANES), F32)],
        compiler_params=_params(1),
    )(y, target)


def _adamw(w, g, m, v):
    m = ADAM_B1 * m + (1.0 - ADAM_B1) * g
    v = ADAM_B2 * v + (1.0 - ADAM_B2) * (g * g)
    m_hat = m / (1.0 - ADAM_B1 ** ADAM_STEP)
    v_hat = v / (1.0 - ADAM_B2 ** ADAM_STEP)
    delta = -ADAM_LR * (m_hat / (jnp.sqrt(v_hat) + ADAM_EPS) + ADAM_WD * w)
    return delta, m, v


def adam_sharded(w, m, v, p_own, recv, name):
    R, C = w.shape
    tr = _tile(R, 256, 16)

    def body(w_ref, m_ref, v_ref, p_ref, r0_ref, r1_ref, r2_ref, g_ref, d_ref, nm_ref, nv_ref):
        g = p_ref[...] + r0_ref[0].astype(F32) + r1_ref[0].astype(F32) + r2_ref[0].astype(F32)
        delta, nm, nv = _adamw(w_ref[...], g, m_ref[...], v_ref[...])
        g_ref[...] = g
        d_ref[...] = delta
        nm_ref[...] = nm
        nv_ref[...] = nv

    row = pl.BlockSpec((tr, C), lambda i: (i, 0))

    def rspec(k):
        return pl.BlockSpec((1, tr, C), lambda i, k=k: (k, i, 0))

    shp = jax.ShapeDtypeStruct((R, C), F32)
    return pl.pallas_call(
        body, name=name, grid=(R // tr,), in_specs=[row, row, row, row, rspec(0), rspec(1), rspec(2)],
        out_specs=[row] * 4, out_shape=[shp] * 4, compiler_params=_params(1),
    )(w, m, v, p_own, recv, recv, recv)


def adam_small(params, total, extra_grads, name):
    n, ne = len(params), len(extra_grads)

    def body(*refs):
        total_ref, extra_refs = refs[0], refs[1:1 + ne]
        ins, outs = refs[1 + ne:1 + ne + 3 * n], refs[1 + ne + 3 * n:]
        for k, (_, _, _, where) in enumerate(params):
            if isinstance(where, int):
                g = extra_refs[where][...]
            else:
                row, rows, width = where
                g = total_ref[row:row + rows, 0:width]
            delta, nm, nv = _adamw(ins[3 * k][...], g, ins[3 * k + 1][...], ins[3 * k + 2][...])
            outs[4 * k][...] = g
            outs[4 * k + 1][...] = delta
            outs[4 * k + 2][...] = nm
            outs[4 * k + 3][...] = nv

    flat = [a for w, m, v, _ in params for a in (w, m, v)]
    out_shape = [jax.ShapeDtypeStruct(w.shape, F32) for w, _, _, _ in params for _ in range(4)]
    res = pl.pallas_call(body, name=name, out_shape=out_shape)(total, *extra_grads, *flat)
    return [res[4 * k:4 * k + 4] for k in range(n)]


def pair_sum(g4, recv, sel, name):
    _, R, C = g4.shape
    tr = _tile(R, 256, 16)

    def body(sel_ref, g_ref, r_ref, go_ref, ro_ref, pb_ref, own_ref):
        pb_ref[0] = (g_ref[0] + r_ref[0]).astype(BF16)
        own_ref[...] = go_ref[0] + ro_ref[0]

    grid_spec = pltpu.PrefetchScalarGridSpec(
        num_scalar_prefetch=1, grid=(R // tr, 4),
        in_specs=[pl.BlockSpec((1, tr, C), lambda i, q, s: (2 * q + s[0], i, 0)),
                  pl.BlockSpec((1, tr, C), lambda i, q, s: (q, i, 0)),
                  pl.BlockSpec((1, tr, C), lambda i, q, s: (2 * s[1] + s[0], i, 0)),
                  pl.BlockSpec((1, tr, C), lambda i, q, s: (s[1], i, 0))],
        out_specs=[pl.BlockSpec((1, tr, C), lambda i, q, s: (q, i, 0)), pl.BlockSpec((tr, C), lambda i, q, s: (i, 0))])
    return pl.pallas_call(
        body, name=name, grid_spec=grid_spec,
        out_shape=[jax.ShapeDtypeStruct((4, R, C), BF16), jax.ShapeDtypeStruct((R, C), F32)],
        compiler_params=_params(2),
    )(sel, g4, recv, g4, recv)


def _place():
    return lax.axis_index("x"), lax.axis_index("y"), lax.axis_index("c")


ANY = pl.BlockSpec(memory_space=pl.ANY)


def all_gather(shards, name):
    na = len(shards)

    def body(*refs):
        x_refs, out_refs = refs[:na], refs[na:2 * na]
        send_sems, recv_sems, local_sems = refs[2 * na:]
        x, y, c = _place()
        me, sibling = (x, y, c), (x, y, 1 - c)
        chips = [(1 - x, y), (x, 1 - y), (1 - x, 1 - y)]

        def copy(a, k, block, to, src=None):
            dst = out_refs[a].at[4 * block[0] + 2 * block[1] + block[2]]
            return pltpu.make_async_remote_copy(
                src_ref=dst if src is None else src, dst_ref=dst, send_sem=send_sems.at[7 * a + k],
                recv_sem=recv_sems.at[7 * a + k], device_id=to, device_id_type=MESH)

        mine = [pltpu.make_async_copy(x_refs[a], out_refs[a].at[4 * x + 2 * y + c], local_sems.at[a]) for a in range(na)]
        for cp in mine:
            cp.start()
        first = []
        for a in range(na):
            first.append(copy(a, 0, me, sibling, src=x_refs[a]))
            first += [copy(a, 1 + j, me, (*chip, c), src=x_refs[a]) for j, chip in enumerate(chips)]
        for cp in first:
            cp.start()
        passed = []
        for a in range(na):
            for j, chip in enumerate(chips):
                copy(a, 1 + j, (*chip, c), me).wait_recv()
                fwd = copy(a, 4 + j, (*chip, c), sibling)
                fwd.start()
                passed.append(fwd)
        for a in range(na):
            copy(a, 0, sibling, me).wait_recv()
            for j, chip in enumerate(chips):
                copy(a, 4 + j, (*chip, 1 - c), me).wait_recv()
        for cp in first + passed:
            cp.wait_send()
        for cp in mine:
            cp.wait()

    return pl.pallas_call(
        body, name=name, in_specs=[ANY] * na, out_specs=[ANY] * na,
        out_shape=[jax.ShapeDtypeStruct((N_DEV,) + s.shape, s.dtype) for s in shards],
        scratch_shapes=[pltpu.SemaphoreType.DMA((7 * na,)), pltpu.SemaphoreType.DMA((7 * na,)),
                        pltpu.SemaphoreType.DMA((na,))],
    )(*shards)


def exchange_pair(grads, name):
    na = len(grads)

    def body(*refs):
        g_refs, out_refs = refs[:na], refs[na:2 * na]
        send_sems, recv_sems = refs[2 * na:]
        x, y, c = _place()
        copies = []
        for a in range(na):
            for q in range(4):
                copies.append(pltpu.make_async_remote_copy(
                    src_ref=g_refs[a].at[2 * q + (1 - c)], dst_ref=out_refs[a].at[q],
                    send_sem=send_sems.at[4 * a + q], recv_sem=recv_sems.at[4 * a + q],
                    device_id=(x, y, 1 - c), device_id_type=MESH))
        for cp in copies:
            cp.start()
        for cp in copies:
            cp.wait()

    return pl.pallas_call(
        body, name=name, in_specs=[ANY] * na, out_specs=[ANY] * na,
        out_shape=[jax.ShapeDtypeStruct((4,) + g.shape[1:], g.dtype) for g in grads],
        scratch_shapes=[pltpu.SemaphoreType.DMA((4 * na,)), pltpu.SemaphoreType.DMA((4 * na,))],
    )(*grads)


def exchange_chips(sums, name):
    na = len(sums)

    def body(*refs):
        s_refs, out_refs = refs[:na], refs[na:2 * na]
        send_sems, recv_sems = refs[2 * na:]
        x, y, c = _place()
        chips = [(1 - x, y), (x, 1 - y), (1 - x, 1 - y)]
        copies = []
        for a in range(na):
            for k, (px, py) in enumerate(chips):
                copies.append(pltpu.make_async_remote_copy(
                    src_ref=s_refs[a].at[2 * px + py], dst_ref=out_refs[a].at[k],
                    send_sem=send_sems.at[3 * a + k], recv_sem=recv_sems.at[3 * a + k],
                    device_id=(px, py, c), device_id_type=MESH))
        for cp in copies:
            cp.start()
        for cp in copies:
            cp.wait()

    return pl.pallas_call(
        body, name=name, in_specs=[ANY] * na, out_specs=[ANY] * na,
        out_shape=[jax.ShapeDtypeStruct((3,) + s.shape[1:], s.dtype) for s in sums],
        scratch_shapes=[pltpu.SemaphoreType.DMA((3 * na,)), pltpu.SemaphoreType.DMA((3 * na,))],
    )(*sums)


def all_reduce_small(parts, n_rows, name):
    R = n_rows
    n_parts = len(parts)

    def body(*refs):
        part_refs = refs[:n_parts]
        out_ref, buf, send_sems, recv_sems = refs[n_parts:]
        x, y, c = _place()
        me = 4 * x + 2 * y + c
        own = buf.at[me]
        own[...] = jnp.zeros((R, D), F32)
        for ref, (arr, row) in zip(part_refs, parts):
            own[row:row + arr.shape[0], 0:arr.shape[1]] = ref[...]
        copies = []
        for k in range(1, N_DEV):
            peer = (x ^ (k >> 2), y ^ ((k >> 1) & 1), c ^ (k & 1))
            copies.append(pltpu.make_async_remote_copy(
                src_ref=own, dst_ref=own, send_sem=send_sems.at[k - 1], recv_sem=recv_sems.at[k - 1],
                device_id=peer, device_id_type=MESH))
        for cp in copies:
            cp.start()
        for cp in copies:
            cp.wait()
        total = buf[0]
        for d in range(1, N_DEV):
            total = total + buf[d]
        out_ref[...] = total

    vm = pl.BlockSpec(memory_space=pltpu.VMEM)
    return pl.pallas_call(
        body, name=name, in_specs=[vm] * n_parts, out_specs=vm, out_shape=jax.ShapeDtypeStruct((R, D), F32),
        scratch_shapes=[pltpu.VMEM((N_DEV, R, D), F32), pltpu.SemaphoreType.DMA((N_DEV - 1,)),
                        pltpu.SemaphoreType.DMA((N_DEV - 1,))],
    )(*[arr for arr, _ in parts])


def _col_blocks(gathered, n_blocks):
    n, d, w = gathered.shape
    whole = gathered.transpose(1, 0, 2).reshape(d, n * w)
    return whole.reshape(d, n_blocks, n * w // n_blocks).transpose(1, 0, 2)[:, None]


def _col_shards(blocks):
    n, d, w = blocks.shape
    whole = blocks.transpose(1, 0, 2).reshape(d, n * w)
    return whole.reshape(d, N_DEV, n * w // N_DEV).transpose(1, 0, 2)


def kernel(x, ffn1_pre_g, ffn1_post_g, ffn1_w_in, ffn1_w_out, mix_pre_g, mix_post_g, ffn2_pre_g, ffn2_post_g, ffn2_w_in, ffn2_w_out, conv_w_in, conv_k, conv_w_out, kv_g, kv_w, forget_b, attn_w_qg, attn_w_o, loss_target, m_ffn1_pre_g, m_ffn1_post_g, m_ffn1_w_in, m_ffn1_w_out, m_mix_pre_g, m_mix_post_g, m_ffn2_pre_g, m_ffn2_post_g, m_ffn2_w_in, m_ffn2_w_out, m_conv_w_in, m_conv_k, m_conv_w_out, m_kv_g, m_kv_w, m_forget_b, m_attn_w_qg, m_attn_w_o, v_ffn1_pre_g, v_ffn1_post_g, v_ffn1_w_in, v_ffn1_w_out, v_mix_pre_g, v_mix_post_g, v_ffn2_pre_g, v_ffn2_post_g, v_ffn2_w_in, v_ffn2_w_out, v_conv_w_in, v_conv_k, v_conv_w_out, v_kv_g, v_kv_w, v_forget_b, v_attn_w_qg, v_attn_w_o):
    n_seq, seq_len, _ = x.shape
    T = n_seq * seq_len
    xi, yi, ci = _place()
    dev = 4 * xi + 2 * yi + ci
    x0 = x.reshape(T, D)
    target = loss_target.reshape(T, D)

    big = [ffn1_w_in, ffn1_w_out, ffn2_w_in, ffn2_w_out, conv_w_in, conv_w_out, kv_w, attn_w_qg, attn_w_o]
    gathered = all_gather([w.astype(BF16) for w in big] + [conv_k], "all_gather_weights")
    w1_in, w1_out, w2_in, w2_out, cw_in_g, cw_out, kvw_g, qgw_g, ow, ck_g = gathered
    cw_in = _col_blocks(cw_in_g[:, 0], 3)
    qg_w = _col_blocks(qgw_g[:, 0], 2)
    kv_whole = kvw_g.transpose(1, 0, 2).reshape(D, 2 * D + N_HEADS)
    kv_wb = kv_whole[:, :2 * D].reshape(D, 2, D).transpose(1, 0, 2)[:, None]
    f_w = jnp.pad(kv_whole[:, 2 * D:], ((0, 0), (0, LANES - N_HEADS)))[None, None]
    ck = ck_g[:, 0].transpose(1, 0, 2).reshape(3, D)
    fb = jnp.pad(forget_b, (0, LANES - N_HEADS))[None]

    def vec(g, l):
        return g[l:l + 1]

    grads_small = {}

    def ffn_fwd(xin, g_pre, g_post, w_in, w_out, l, tag):
        xn, (gu,) = rms_proj(xin, g_pre, [(w_in, l)], [BF16], f"{tag}_in")
        tm = _tile(T, TM, 16)
        fbw = gu.shape[2]
        specs = [pl.BlockSpec((1, tm, fbw), lambda i, k: (k, i, 0)), pl.BlockSpec((1, tm, fbw), lambda i, k: (k + 4, i, 0))]
        a, h, y = mix_out(_swiglu_pro, [gu, gu], specs, (w_out, l), xin, g_post, 0.5, 4, f"{tag}_out")
        return y, (xin, xn, gu, a, h)

    def ffn_bwd(dy, saved, g_pre, g_post, w_in, w_out, l, tag, into):
        xin, xn, gu, a, h = saved
        dh, dg_post, dgate, dup = post_bwd(dy, h, g_post, 0.5, (w_out, l), 4, f"{tag}_bwd_out", gu=gu)
        dx, dg_pre = pre_bwd([(dgate, (w_in, l), 0), (dup, (w_in, l), 4)], xin, g_pre, dy, f"{tag}_bwd_in")
        dw_in = wgrad([xn[None]], [dgate, dup], 8, (8,) + w_in.shape[1:], l, f"{tag}_dw_in", into=into[0])
        dw_out = wgrad([a], [dh[None]], 4, (8,) + w_out.shape[1:], l, f"{tag}_dw_out", into=into[1])
        return dx, dg_pre, dg_post, dw_in, dw_out

    x1, s_f1a = ffn_fwd(x0, vec(ffn1_pre_g, 0), vec(ffn1_post_g, 0), w1_in, w1_out, 0, "l0_ffn1")
    xn_c, (bch,) = rms_proj(x1, vec(mix_pre_g, 0), [(cw_in, 0)], [BF16], "conv_in")
    tmc = _tile(seq_len, TM, 16)
    hb = tmc // HALO

    def cpiece(p):
        return pl.BlockSpec((1, tmc, D), lambda i, k, p=p: (p, i, 0))

    def chalo(p):
        return pl.BlockSpec((1, HALO, D), lambda i, k, p=p: (p, jnp.maximum(i * hb - 1, 0), 0))

    conv_specs = [cpiece(0), cpiece(1), cpiece(2), chalo(1), chalo(2), pl.BlockSpec((3, D), lambda i, k: (0, 0))]
    z_c, m_c, x2 = mix_out(_make_conv_pro(seq_len // tmc), [bch, bch, bch, bch, bch, ck], conv_specs, (cw_out, 0),
                           x1, vec(mix_post_g, 0), 1.0, 1, "conv_out", tm=tmc)
    x3, s_f2a = ffn_fwd(x2, vec(ffn2_pre_g, 0), vec(ffn2_post_g, 0), w2_in, w2_out, 0, "l0_ffn2")

    xn_kv, (kv, fl) = rms_proj(x3, kv_g[None], [(kv_wb, 0), (f_w, 0)], [BF16, F32], "kv_in")
    c128 = forget_fwd(fl, fb, seq_len, "forget_fwd")
    c16 = c128[:, :N_HEADS]
    cb = jnp.repeat(c16, HEAD_DIM, axis=1)
    tq = _tile(seq_len, TQ, LANES)
    crow = c16.reshape(n_seq, seq_len // tq, tq, N_HEADS // 2, 2).transpose(0, 3, 1, 4, 2)

    x4, s_f1b = ffn_fwd(x3, vec(ffn1_pre_g, 1), vec(ffn1_post_g, 1), w1_in, w1_out, 1, "l1_ffn1")
    xn_a, (qg,) = rms_proj(x4, vec(mix_pre_g, 1), [(qg_w, 0)], [BF16], "attn_in")
    o, lse = attn_fwd(qg, kv, cb, crow, n_seq, seq_len, "attn_fwd")
    tm = _tile(T, TM, 16)
    gate_specs = [pl.BlockSpec((1, tm, D), lambda i, k: (1, i, 0)), pl.BlockSpec((tm, D), lambda i, k: (i, 0))]
    z_a, m_a, x5 = mix_out(_attn_gate_pro, [qg, o], gate_specs, (ow, 0), x4, vec(mix_post_g, 1), 1.0, 1, "attn_out")
    x6, s_f2b = ffn_fwd(x5, vec(ffn2_pre_g, 1), vec(ffn2_post_g, 1), w2_in, w2_out, 1, "l1_ffn2")

    dy, loss_part = loss_head(x6, target, "loss_head")

    dx5, dg, dgp, dw2_in, dw2_out = ffn_bwd(dy, s_f2b, vec(ffn2_pre_g, 1), vec(ffn2_post_g, 1), w2_in, w2_out, 1,
                                            "l1_ffn2", (None, None))
    grads_small["ffn2_pre", 1], grads_small["ffn2_post", 1] = dg, dgp
    dm_a, dgp, dz_a = post_bwd(dx5, m_a, vec(mix_post_g, 1), 1.0, (ow, 0), 1, "attn_bwd_out")
    grads_small["mix_post", 1] = dgp
    dq, dgate, dk, dv, dcrow = attn_bwd(dz_a[0], qg, kv, o, lse, cb, crow, n_seq, seq_len, "attn_bwd")
    dx4, dg = pre_bwd([(dq, (qg_w, 0), 0), (dgate, (qg_w, 0), 1)], x4, vec(mix_pre_g, 1), dx5, "attn_bwd_in")
    grads_small["mix_pre", 1] = dg
    d_ow = wgrad([z_a], [dm_a[None]], 1, (8,) + ow.shape[1:], 0, "attn_dw_o")
    d_qgw = wgrad([xn_a[None]], [dq, dgate], 2, (2, 1, D, D), 0, "attn_dw_qg")
    dx3, dg, dgp, dw1_in, dw1_out = ffn_bwd(dx4, s_f1b, vec(ffn1_pre_g, 1), vec(ffn1_post_g, 1), w1_in, w1_out, 1,
                                            "l1_ffn1", (None, None))
    grads_small["ffn1_pre", 1], grads_small["ffn1_post", 1] = dg, dgp

    dc16 = dcrow.transpose(0, 2, 4, 1, 3).reshape(T, N_HEADS)
    dfl, dfb = forget_bwd(jnp.pad(dc16, ((0, 0), (0, LANES - N_HEADS))), fl, fb, seq_len, "forget_bwd")
    dx3, dg_kv = pre_bwd([(dk, (kv_wb, 0), 0), (dv, (kv_wb, 0), 1), (dfl, (f_w, 0), 0)], x3, kv_g[None], dx3, "kv_bwd_in")
    d_kvw = wgrad([xn_kv[None]], [dk, dv], 2, (2, 1, D, D), 0, "kv_dw")
    d_fw = wgrad([xn_kv[None]], [dfl.astype(BF16)], 1, (1, 1, D, LANES), 0, "forget_dw")

    dx2, dg, dgp, dw2_in, dw2_out = ffn_bwd(dx3, s_f2a, vec(ffn2_pre_g, 0), vec(ffn2_post_g, 0), w2_in, w2_out, 0,
                                            "l0_ffn2", (dw2_in, dw2_out))
    grads_small["ffn2_pre", 0], grads_small["ffn2_post", 0] = dg, dgp
    dm_c, dgp, dz_c = post_bwd(dx2, m_c, vec(mix_post_g, 0), 1.0, (cw_out, 0), 1, "conv_bwd_out")
    grads_small["mix_post", 0] = dgp
    dbch, d_ck = conv_bwd_mix(dz_c, bch, ck, seq_len, "conv_bwd_mix")
    dx1, dg = pre_bwd([(dbch, (cw_in, 0), 0)], x1, vec(mix_pre_g, 0), dx2, "conv_bwd_in")
    grads_small["mix_pre", 0] = dg
    d_cw_out = wgrad([z_c], [dm_c[None]], 1, (8,) + cw_out.shape[1:], 0, "conv_dw_out")
    d_cw_in = wgrad([xn_c[None]], [dbch], 3, (3, 1, D, D), 0, "conv_dw_in")
    dx0, dg, dgp, dw1_in, dw1_out = ffn_bwd(dx1, s_f1a, vec(ffn1_pre_g, 0), vec(ffn1_post_g, 0), w1_in, w1_out, 0,
                                            "l0_ffn1", (dw1_in, dw1_out))
    grads_small["ffn1_pre", 0], grads_small["ffn1_post", 0] = dg, dgp

    d_kv_whole = jnp.concatenate([d_kvw[0, 0], d_kvw[1, 0], d_fw[0, 0, :, :N_HEADS]], axis=1)
    wshard = D * 2 + N_HEADS
    blocked = [
        dw1_in.reshape(8, -1, dw1_in.shape[3]), dw1_out.reshape(8, -1, D), dw2_in.reshape(8, -1, dw2_in.shape[3]),
        dw2_out.reshape(8, -1, D), _col_shards(d_cw_in[:, 0]), d_cw_out.reshape(8, -1, D),
        d_kv_whole.reshape(D, N_DEV, wshard // N_DEV).transpose(1, 0, 2), _col_shards(d_qgw[:, 0]), d_ow.reshape(8, -1, D)]
    from_sibling = exchange_pair(blocked, "grad_exchange_pair")
    sel = jnp.stack([ci, 2 * xi + yi]).astype(jnp.int32)
    sums, owns = [], []
    names = ["ffn1_w_in", "ffn1_w_out", "ffn2_w_in", "ffn2_w_out", "conv_w_in", "conv_w_out", "kv_w", "attn_w_qg", "attn_w_o"]
    for g4, r4, nm in zip(blocked, from_sibling, names):
        s, own = pair_sum(g4, r4, sel, f"pair_sum_{nm}")
        sums.append(s)
        owns.append(own)
    from_chips = exchange_chips(sums, "grad_exchange_chips")

    moments = {"ffn1_w_in": (m_ffn1_w_in, v_ffn1_w_in), "ffn1_w_out": (m_ffn1_w_out, v_ffn1_w_out),
               "ffn2_w_in": (m_ffn2_w_in, v_ffn2_w_in), "ffn2_w_out": (m_ffn2_w_out, v_ffn2_w_out),
               "conv_w_in": (m_conv_w_in, v_conv_w_in), "conv_w_out": (m_conv_w_out, v_conv_w_out),
               "kv_w": (m_kv_w, v_kv_w), "attn_w_qg": (m_attn_w_qg, v_attn_w_qg), "attn_w_o": (m_attn_w_o, v_attn_w_o)}
    out = {}
    for w, own, recv, nm in zip(big, owns, from_chips, names):
        shape2 = own.shape
        mm, vv = moments[nm]
        res = adam_sharded(w.reshape(shape2), mm.reshape(shape2), vv.reshape(shape2), own, recv, f"adam_{nm}")
        out[nm] = [r.reshape(w.shape) for r in res]

    small_names = ["ffn1_pre", "ffn1_post", "mix_pre", "mix_post", "ffn2_pre", "ffn2_post"]
    parts = [(grads_small[n, l], 2 * k + l) for k, n in enumerate(small_names) for l in range(2)]
    parts += [(dg_kv, 12), (dfb, 13), (d_ck, 14), (loss_part, 17)]
    total = all_reduce_small(parts, 24, "all_reduce_small")
    loss = total[17, 0]
    d_ck_mine = lax.dynamic_slice(total, (14, dev * LANES), (3, LANES))
    gains = [(ffn1_pre_g, m_ffn1_pre_g, v_ffn1_pre_g), (ffn1_post_g, m_ffn1_post_g, v_ffn1_post_g),
             (mix_pre_g, m_mix_pre_g, v_mix_pre_g), (mix_post_g, m_mix_post_g, v_mix_post_g),
             (ffn2_pre_g, m_ffn2_pre_g, v_ffn2_pre_g), (ffn2_post_g, m_ffn2_post_g, v_ffn2_post_g)]
    small_params = [(w, m, v, (2 * k, 2, D)) for k, (w, m, v) in enumerate(gains)]
    small_params += [(kv_g[None], m_kv_g[None], v_kv_g[None], (12, 1, D)),
                     (forget_b[None], m_forget_b[None], v_forget_b[None], (13, 1, N_HEADS)),
                     (conv_k[0], m_conv_k[0], v_conv_k[0], 0)]
    small_res = adam_small(small_params, total, [d_ck_mine], "adam_small")
    small_keys = [n + "_g" for n in small_names] + ["kv_g", "forget_b", "conv_k"]
    shapes = {"kv_g": kv_g.shape, "forget_b": forget_b.shape, "conv_k": conv_k.shape}
    small = [{key: res[kind].reshape(shapes.get(key, res[kind].shape)) for key, res in zip(small_keys, small_res)}
             for kind in range(4)]
    order = ["ffn1_pre_g", "ffn1_post_g", "ffn1_w_in", "ffn1_w_out", "mix_pre_g", "mix_post_g", "ffn2_pre_g", "ffn2_post_g",
             "ffn2_w_in", "ffn2_w_out", "conv_w_in", "conv_k", "conv_w_out", "kv_g", "kv_w", "forget_b", "attn_w_qg",
             "attn_w_o"]
    results = [loss, dx0.reshape(x.shape)]
    for kind in range(4):
        for nm in order:
            results.append(out[nm][kind] if nm in out else small[kind][nm])
    return tuple(results)
```

```python
import functools
import math

import jax
import jax.numpy as jnp
from jax import lax
from jax.experimental import pallas as pl
from jax.experimental.pallas import tpu as pltpu

F32, BF16 = jnp.float32, jnp.bfloat16
D = 1024
N_HEADS = 16
HEAD_DIM = 64
N_DEV = 8
RMS_EPS = 1e-6
ATT_SCALE = 1.0 / math.sqrt(HEAD_DIM)
LANES = 128
HALO = 8
TM = 512
TQ = 256
VMEM_LIMIT = 48 * 1024 * 1024
MESH = pl.DeviceIdType.MESH

ADAM_LR, ADAM_B1, ADAM_B2, ADAM_EPS, ADAM_WD, ADAM_STEP = 0.001, 0.9, 0.999, 1e-08, 0.01, 10

NT = (((1,), (1,)), ((), ()))
TN = (((0,), (0,)), ((), ()))


def _params(n_axes):
    return pltpu.CompilerParams(dimension_semantics=("arbitrary",) * n_axes, vmem_limit_bytes=VMEM_LIMIT)


def _tile(n, cap, mult):
    best = None
    for t in range(mult, min(n, cap) + 1, mult):
        if n % t == 0:
            best = t
    assert best is not None, (n, cap, mult)
    return best


def _rms_rstd(x):
    return lax.rsqrt(jnp.mean(x * x, axis=-1, keepdims=True) + RMS_EPS)


def _rms_fwd(x, g):
    return x * _rms_rstd(x) * g


def _rms_bwd(x, g, dy):
    xh = x * _rms_rstd(x)
    dyg = dy * g
    dx = _rms_rstd(x) * (dyg - xh * jnp.mean(dyg * xh, axis=-1, keepdims=True))
    return dx, jnp.sum(dy * xh, axis=0, keepdims=True)


def _accumulate(ref, first, value):
    @pl.when(first)
    def _():
        ref[...] = value

    @pl.when(jnp.logical_not(first))
    def _():
        ref[...] += value


def rms_proj(x, g, ws, out_dtypes, name):
    T = x.shape[0]
    tm = _tile(T, TM, 16)
    na = len(ws)
    nbs = [w.shape[0] for w, _ in ws]
    offs = [sum(nbs[:a]) for a in range(na)]
    nb_total = sum(nbs)

    def body(x_ref, g_ref, *refs):
        w_refs, xn_ref, o_refs, xn_s = refs[:na], refs[na], refs[na + 1:2 * na + 1], refs[2 * na + 1]
        n = pl.program_id(1)

        @pl.when(n == 0)
        def _():
            xn = _rms_fwd(x_ref[...], g_ref[...]).astype(BF16)
            xn_s[...] = xn
            xn_ref[...] = xn

        for a in range(na):
            @pl.when((n >= offs[a]) & (n < offs[a] + nbs[a]))
            def _(a=a):
                y = jnp.dot(xn_s[...], w_refs[a][0, 0], preferred_element_type=F32)
                o_refs[a][0] = y.astype(o_refs[a].dtype)

    def blk(off, nb):
        return lambda n: jnp.clip(n - off, 0, nb - 1)

    in_specs = [pl.BlockSpec((tm, D), lambda i, n: (i, 0)), pl.BlockSpec((1, D), lambda i, n: (0, 0))]
    out_specs = [pl.BlockSpec((tm, D), lambda i, n: (i, 0))]
    out_shape = [jax.ShapeDtypeStruct((T, D), BF16)]
    for (w, l), off, nb, dt in zip(ws, offs, nbs, out_dtypes):
        wb = w.shape[3]
        in_specs.append(pl.BlockSpec((1, 1, D, wb), lambda i, n, f=blk(off, nb), l=l: (f(n), l, 0, 0)))
        out_specs.append(pl.BlockSpec((1, tm, wb), lambda i, n, f=blk(off, nb): (f(n), i, 0)))
        out_shape.append(jax.ShapeDtypeStruct((nb, T, wb), dt))
    res = pl.pallas_call(
        body, name=name, grid=(T // tm, nb_total), in_specs=in_specs, out_specs=out_specs, out_shape=out_shape,
        scratch_shapes=[pltpu.VMEM((tm, D), BF16)], compiler_params=_params(2),
    )(x, g, *[w for w, _ in ws])
    return res[0], res[1:]


def mix_out(pro, pro_inputs, pro_specs, w, res, g_post, alpha, nk, name, tm=None):
    w4, l = w
    T = res.shape[0]
    tm = _tile(T, TM, 16) if tm is None else tm
    dpb = N_DEV // nk
    rows = w4.shape[2]
    kb = dpb * rows
    npi = len(pro_inputs)

    def body(*refs):
        pro_refs = refs[:npi]
        w_ref, res_ref, g_ref, z_ref, m_ref, y_ref, acc = refs[npi:]
        i, k = pl.program_id(0), pl.program_id(1)
        z = pro(i, k, *pro_refs).astype(BF16)
        z_ref[0] = z
        part = jnp.dot(z, w_ref[:, 0].reshape(kb, D), preferred_element_type=F32)
        _accumulate(acc, k == 0, part)

        @pl.when(k == nk - 1)
        def _():
            m = acc[...]
            m_ref[...] = m
            y_ref[...] = res_ref[...] + alpha * _rms_fwd(m, g_ref[...])

    row = pl.BlockSpec((tm, D), lambda i, k: (i, 0))
    in_specs = list(pro_specs) + [
        pl.BlockSpec((dpb, 1, rows, D), lambda i, k: (k, l, 0, 0)), row, pl.BlockSpec((1, D), lambda i, k: (0, 0))]
    z, m, y = pl.pallas_call(
        body, name=name, grid=(T // tm, nk), in_specs=in_specs,
        out_specs=[pl.BlockSpec((1, tm, kb), lambda i, k: (k, i, 0)), row, row],
        out_shape=[jax.ShapeDtypeStruct((nk, T, kb), BF16), jax.ShapeDtypeStruct((T, D), F32),
                   jax.ShapeDtypeStruct((T, D), F32)],
        scratch_shapes=[pltpu.VMEM((tm, D), F32)], compiler_params=_params(2),
    )(*pro_inputs, w4, res, g_post)
    return z, m, y


def post_bwd(dy, m, g_post, alpha, w, nk, name, gu=None):
    w4, l = w
    T = dy.shape[0]
    tm = _tile(T, TM, 16)
    dpb = N_DEV // nk
    rows = w4.shape[2]
    kb = dpb * rows
    swiglu = gu is not None

    def body(dy_ref, m_ref, g_ref, w_ref, *refs):
        if swiglu:
            gate_ref, up_ref, dm_ref, dg_ref, dgate_ref, dup_ref, dm_s = refs
        else:
            dm_ref, dg_ref, dz_ref, dm_s = refs
        i, k = pl.program_id(0), pl.program_id(1)

        @pl.when(k == 0)
        def _():
            dm, dg = _rms_bwd(m_ref[...], g_ref[...], alpha * dy_ref[...])
            dm_s[...] = dm.astype(BF16)
            dm_ref[...] = dm.astype(BF16)
            _accumulate(dg_ref, i == 0, dg)

        dz = lax.dot_general(dm_s[...], w_ref[:, 0].reshape(kb, D), NT, preferred_element_type=F32)
        if swiglu:
            gate, up = gate_ref[0].astype(F32), up_ref[0].astype(F32)
            sig = jax.nn.sigmoid(gate)
            dgate_ref[0] = (dz * up * sig * (1.0 + gate * (1.0 - sig))).astype(BF16)
            dup_ref[0] = (dz * gate * sig).astype(BF16)
        else:
            dz_ref[0] = dz.astype(BF16)

    row = pl.BlockSpec((tm, D), lambda i, k: (i, 0))
    vec = pl.BlockSpec((1, D), lambda i, k: (0, 0))
    blkk = pl.BlockSpec((1, tm, kb), lambda i, k: (k, i, 0))
    in_specs = [row, row, vec, pl.BlockSpec((dpb, 1, rows, D), lambda i, k: (k, l, 0, 0))]
    inputs = [dy, m, g_post, w4]
    out_specs = [row, vec, blkk]
    out_shape = [jax.ShapeDtypeStruct((T, D), BF16), jax.ShapeDtypeStruct((1, D), F32),
                 jax.ShapeDtypeStruct((nk, T, kb), BF16)]
    if swiglu:
        in_specs += [blkk, pl.BlockSpec((1, tm, kb), lambda i, k: (k + nk, i, 0))]
        inputs += [gu, gu]
        out_specs.append(blkk)
        out_shape.append(jax.ShapeDtypeStruct((nk, T, kb), BF16))
    return pl.pallas_call(
        body, name=name, grid=(T // tm, nk), in_specs=in_specs, out_specs=out_specs, out_shape=out_shape,
        scratch_shapes=[pltpu.VMEM((tm, D), BF16)], compiler_params=_params(2),
    )(*inputs)


def pre_bwd(pieces, x, g_pre, dres, name):
    T = x.shape[0]
    tm = _tile(T, TM, 16)
    na = len(pieces)
    nbs = [p[0].shape[0] for p in pieces]
    offs = [sum(nbs[:a]) for a in range(na)]
    nb_total = sum(nbs)

    def body(*refs):
        dz_refs, w_refs = refs[:na], refs[na:2 * na]
        x_ref, g_ref, dres_ref, dx_ref, dg_ref, acc = refs[2 * na:]
        i, n = pl.program_id(0), pl.program_id(1)
        for a in range(na):
            @pl.when((n >= offs[a]) & (n < offs[a] + nbs[a]))
            def _(a=a):
                part = lax.dot_general(dz_refs[a][0].astype(BF16), w_refs[a][0, 0], NT, preferred_element_type=F32)
                _accumulate(acc, n == 0, part)

        @pl.when(n == nb_total - 1)
        def _():
            dx, dg = _rms_bwd(x_ref[...], g_ref[...], acc[...])
            dx_ref[...] = dres_ref[...] + dx
            _accumulate(dg_ref, i == 0, dg)

    def blk(off, nb):
        return lambda n: jnp.clip(n - off, 0, nb - 1)

    row = pl.BlockSpec((tm, D), lambda i, n: (i, 0))
    vec = pl.BlockSpec((1, D), lambda i, n: (0, 0))
    dz_specs, w_specs = [], []
    for (dz, (w4, l), w_off), off, nb in zip(pieces, offs, nbs):
        wb = dz.shape[2]
        dz_specs.append(pl.BlockSpec((1, tm, wb), lambda i, n, f=blk(off, nb): (f(n), i, 0)))
        w_specs.append(pl.BlockSpec((1, 1, D, wb), lambda i, n, f=blk(off, nb), l=l, o=w_off: (o + f(n), l, 0, 0)))
    return pl.pallas_call(
        body, name=name, grid=(T // tm, nb_total), in_specs=dz_specs + w_specs + [row, vec, row],
        out_specs=[row, vec], out_shape=[jax.ShapeDtypeStruct((T, D), F32), jax.ShapeDtypeStruct((1, D), F32)],
        scratch_shapes=[pltpu.VMEM((tm, D), F32)], compiler_params=_params(2),
    )(*[p[0] for p in pieces], *[p[1][0] for p in pieces], x, g_pre, dres)


def wgrad(a_list, b_list, nout, out4_shape, name):
    T = a_list[0].shape[1]
    tm = _tile(T, TM, 16)
    nt = T // tm
    dpb = out4_shape[0] // nout
    _, _, R, C = out4_shape
    na, nb = len(a_list), len(b_list)

    def spans(arrs):
        ns = [a.shape[0] for a in arrs]
        return ns, [sum(ns[:k]) for k in range(len(ns))], sum(ns)

    a_ns, a_offs, a_tot = spans(a_list)
    b_ns, b_offs, b_tot = spans(b_list)
    assert a_tot in (1, nout) and b_tot in (1, nout)

    def body(*refs):
        a_refs, b_refs = refs[:na], refs[na:na + nb]
        out_ref, acc = refs[-2:]
        p, t = pl.program_id(0), pl.program_id(1)
        for ia in range(na):
            for ib in range(nb):
                conds = []
                if a_tot > 1:
                    conds += [p >= a_offs[ia], p < a_offs[ia] + a_ns[ia]]
                if b_tot > 1:
                    conds += [p >= b_offs[ib], p < b_offs[ib] + b_ns[ib]]

                def work(ia=ia, ib=ib):
                    part = lax.dot_general(a_refs[ia][0], b_refs[ib][0], TN, preferred_element_type=F32)
                    _accumulate(acc, t == 0, part)

                if conds:
                    pl.when(functools.reduce(jnp.logical_and, conds))(work)
                else:
                    work()

        @pl.when(t == nt - 1)
        def _():
            out_ref[...] = acc[...].astype(BF16).reshape(dpb, 1, R, C)

    def blk(off, n, tot):
        if tot == 1:
            return lambda p: 0
        return lambda p: jnp.clip(p - off, 0, n - 1)

    in_specs = []
    for arrs, ns, offs, tot in ((a_list, a_ns, a_offs, a_tot), (b_list, b_ns, b_offs, b_tot)):
        for arr, n, off in zip(arrs, ns, offs):
            in_specs.append(pl.BlockSpec((1, tm, arr.shape[2]), lambda p, t, f=blk(off, n, tot): (f(p), t, 0)))
    return pl.pallas_call(
        body, name=name, grid=(nout, nt), in_specs=in_specs,
        out_specs=pl.BlockSpec((dpb, 1, R, C), lambda p, t: (p, 0, 0, 0)),
        out_shape=jax.ShapeDtypeStruct(out4_shape, BF16),
        scratch_shapes=[pltpu.VMEM((dpb * R, C), F32)], compiler_params=_params(2),
    )(*a_list, *b_list)


def _swiglu_pro(i, k, gate_ref, up_ref):
    gate, up = gate_ref[0].astype(F32), up_ref[0].astype(F32)
    return gate * jax.nn.sigmoid(gate) * up


def _attn_gate_pro(i, k, gate_ref, o_ref):
    return jax.nn.sigmoid(gate_ref[0].astype(F32)) * o_ref[...].astype(F32)


def _shift_rows(u, halo, d):
    rolled = pltpu.roll(u, d, 0)
    row = lax.broadcasted_iota(jnp.int32, u.shape, 0)
    for r in range(d):
        rolled = jnp.where(row == r, halo[HALO - d + r:HALO - d + r + 1, :], rolled)
    return rolled


def _advance_rows(u, halo, d):
    n = u.shape[0]
    rolled = pltpu.roll(u, n - d, 0)
    row = lax.broadcasted_iota(jnp.int32, u.shape, 0)
    for r in range(d):
        rolled = jnp.where(row == n - d + r, halo[r:r + 1, :], rolled)
    return rolled


def _make_conv_pro(tiles_per_seq):
    def pro(i, k, b_ref, c_ref, h_ref, ch_ref, hh_ref, ck_ref):
        u = c_ref[0].astype(F32) * h_ref[0].astype(F32)
        first = (i % tiles_per_seq) == 0
        halo = jnp.where(first, 0.0, ch_ref[0].astype(F32) * hh_ref[0].astype(F32))
        ck = ck_ref[...]
        conv = ck[2:3, :] * u + ck[1:2, :] * _shift_rows(u, halo, 1) + ck[0:1, :] * _shift_rows(u, halo, 2)
        return b_ref[0].astype(F32) * conv
    return pro


def conv_bwd_mix(dz, bch, conv_k, seq_len, name):
    T = dz.shape[1]
    tm = _tile(seq_len, TM, 16)
    tps = seq_len // tm
    nt = T // tm
    hb = tm // HALO

    def body(dz_ref, b_ref, c_ref, h_ref, cp_ref, hp_ref, dzn_ref, bn_ref, ck_ref, dbch_ref, dk_ref):
        i = pl.program_id(0)
        first = (i % tps) == 0
        last = (i % tps) == tps - 1
        b, c, h = b_ref[0].astype(F32), c_ref[0].astype(F32), h_ref[0].astype(F32)
        dzt = dz_ref[0].astype(F32)
        u = c * h
        prev = jnp.where(first, 0.0, cp_ref[0].astype(F32) * hp_ref[0].astype(F32))
        u1, u2 = _shift_rows(u, prev, 1), _shift_rows(u, prev, 2)
        ck = ck_ref[...]
        conv = ck[2:3, :] * u + ck[1:2, :] * u1 + ck[0:1, :] * u2
        dconv = dzt * b
        nxt = jnp.where(last, 0.0, dzn_ref[0].astype(F32) * bn_ref[0].astype(F32))
        du = ck[2:3, :] * dconv + ck[1:2, :] * _advance_rows(dconv, nxt, 1) + ck[0:1, :] * _advance_rows(dconv, nxt, 2)
        dbch_ref[0] = (dzt * conv).astype(BF16)
        dbch_ref[1] = (du * h).astype(BF16)
        dbch_ref[2] = (du * c).astype(BF16)
        tap = lax.broadcasted_iota(jnp.int32, (3, D), 0)
        dk = jnp.where(tap == 0, jnp.sum(dconv * u2, axis=0, keepdims=True),
                       jnp.where(tap == 1, jnp.sum(dconv * u1, axis=0, keepdims=True),
                                 jnp.sum(dconv * u, axis=0, keepdims=True)))
        _accumulate(dk_ref, i == 0, dk)

    def piece(p):
        return pl.BlockSpec((1, tm, D), lambda i, p=p: (p, i, 0))

    def prev(p):
        return pl.BlockSpec((1, HALO, D), lambda i, p=p: (p, jnp.maximum(i * hb - 1, 0), 0))

    def nxt(p):
        return pl.BlockSpec((1, HALO, D), lambda i, p=p: (p, jnp.minimum((i + 1) * hb, nt * hb - 1), 0))

    return pl.pallas_call(
        body, name=name, grid=(nt,),
        in_specs=[piece(0), piece(0), piece(1), piece(2), prev(1), prev(2), nxt(0), nxt(0),
                  pl.BlockSpec((3, D), lambda i: (0, 0))],
        out_specs=[pl.BlockSpec((3, tm, D), lambda i: (0, i, 0)), pl.BlockSpec((3, D), lambda i: (0, 0))],
        out_shape=[jax.ShapeDtypeStruct((3, T, D), BF16), jax.ShapeDtypeStruct((3, D), F32)],
        compiler_params=_params(1),
    )(dz, bch, bch, bch, bch, bch, dz, bch, conv_k)


def _log_sigmoid(x):
    return jnp.minimum(x, 0.0) - jnp.log(1.0 + jnp.exp(-jnp.abs(x)))


def forget_fwd(fl, fb, seq_len, name):
    T = fl.shape[1]

    def body(fl_ref, fb_ref, c_ref):
        c = _log_sigmoid(fl_ref[0] + fb_ref[...])
        row = lax.broadcasted_iota(jnp.int32, c.shape, 0)
        k = 1
        while k < seq_len:
            c = c + jnp.where(row >= k, pltpu.roll(c, k, 0), 0.0)
            k *= 2
        c_ref[...] = c

    return pl.pallas_call(
        body, name=name, grid=(T // seq_len,),
        in_specs=[pl.BlockSpec((1, seq_len, LANES), lambda b: (0, b, 0)), pl.BlockSpec((1, LANES), lambda b: (0, 0))],
        out_specs=pl.BlockSpec((seq_len, LANES), lambda b: (b, 0)),
        out_shape=jax.ShapeDtypeStruct((T, LANES), F32), compiler_params=_params(1),
    )(fl, fb)


def forget_bwd(dc, fl, fb, seq_len, name):
    T = dc.shape[0]

    def body(dc_ref, fl_ref, fb_ref, dfl_ref, dfb_ref):
        b = pl.program_id(0)
        r = dc_ref[...]
        row = lax.broadcasted_iota(jnp.int32, r.shape, 0)
        k = 1
        while k < seq_len:
            r = r + jnp.where(row < seq_len - k, pltpu.roll(r, seq_len - k, 0), 0.0)
            k *= 2
        dfl = r * jax.nn.sigmoid(-(fl_ref[0] + fb_ref[...]))
        dfl_ref[0] = dfl
        _accumulate(dfb_ref, b == 0, jnp.sum(dfl, axis=0, keepdims=True))

    return pl.pallas_call(
        body, name=name, grid=(T // seq_len,),
        in_specs=[pl.BlockSpec((seq_len, LANES), lambda b: (b, 0)), pl.BlockSpec((1, seq_len, LANES), lambda b: (0, b, 0)),
                  pl.BlockSpec((1, LANES), lambda b: (0, 0))],
        out_specs=[pl.BlockSpec((1, seq_len, LANES), lambda b: (0, b, 0)), pl.BlockSpec((1, LANES), lambda b: (0, 0))],
        out_shape=[jax.ShapeDtypeStruct((1, T, LANES), F32), jax.ShapeDtypeStruct((1, LANES), F32)],
        compiler_params=_params(1),
    )(dc, fl, fb)


def attn_fwd(qg, kv, cb, crow, n_seq, seq_len, name):
    T = n_seq * seq_len
    tq = _tile(seq_len, TQ, LANES)
    nq = seq_len // tq

    def body(q_ref, k_ref, v_ref, cb_ref, cr_ref, o_ref, lse_ref):
        i = pl.program_id(2)
        row = i * tq + lax.broadcasted_iota(jnp.int32, (tq, tq), 0)
        col0 = lax.broadcasted_iota(jnp.int32, (tq, tq), 1)
        outs, lses = [], []
        for hh in range(2):
            sl = slice(HEAD_DIM * hh, HEAD_DIM * (hh + 1))
            q = q_ref[0, :, sl]
            cq = cb_ref[:, HEAD_DIM * hh:HEAD_DIM * hh + 1]

            def step(j, carry, sl=sl, q=q, cq=cq, hh=hh):
                m, l, acc = carry
                rows = pl.ds(pl.multiple_of(j * tq, tq), tq)
                k, v = k_ref[0, rows, sl], v_ref[0, rows, sl]
                ck = cr_ref[0, 0, j, hh:hh + 1, :]
                s = lax.dot_general(q, k, NT, preferred_element_type=F32) * ATT_SCALE + cq - ck
                s = jnp.where(j * tq + col0 <= row, s, -1e30)
                m_new = jnp.maximum(m, jnp.max(s, axis=-1, keepdims=True))
                a = jnp.exp(m - m_new)
                p = jnp.exp(s - m_new)
                l = a * l + jnp.sum(p, axis=-1, keepdims=True)
                acc = a * acc + jnp.dot(p.astype(BF16), v, preferred_element_type=F32)
                return m_new, l, acc

            init = (jnp.full((tq, 1), -1e30, F32), jnp.zeros((tq, 1), F32), jnp.zeros((tq, HEAD_DIM), F32))
            m, l, acc = lax.fori_loop(0, i + 1, step, init)
            outs.append(acc / l)
            lses.append(jnp.broadcast_to(m + jnp.log(l), (tq, HEAD_DIM)))
        o_ref[...] = jnp.concatenate(outs, axis=-1)
        lse_ref[...] = jnp.concatenate(lses, axis=-1)

    qspec = pl.BlockSpec((tq, LANES), lambda b, hp, i: (b * nq + i, hp))
    return pl.pallas_call(
        body, name=name, grid=(n_seq, N_HEADS // 2, nq),
        in_specs=[pl.BlockSpec((1, tq, LANES), lambda b, hp, i: (0, b * nq + i, hp)),
                  pl.BlockSpec((1, seq_len, LANES), lambda b, hp, i: (0, b, hp)),
                  pl.BlockSpec((1, seq_len, LANES), lambda b, hp, i: (1, b, hp)),
                  qspec, pl.BlockSpec((1, 1, nq, 2, tq), lambda b, hp, i: (b, hp, 0, 0, 0))],
        out_specs=[qspec, qspec],
        out_shape=[jax.ShapeDtypeStruct((T, D), F32), jax.ShapeDtypeStruct((T, D), F32)],
        compiler_params=_params(3),
    )(qg, kv, kv, cb, crow)


def attn_bwd(dz, qg, kv, o, lse, cb, crow, n_seq, seq_len, name):
    T = n_seq * seq_len
    tq = _tile(seq_len, TQ, LANES)
    nq = seq_len // tq

    def body(dz_ref, q_ref, gate_ref, o_ref, lse_ref, cb_ref, cr_ref, k_ref, v_ref,
             dq_ref, dgate_ref, dk_ref, dv_ref, dcr_ref, p_s, dp_s, dk_s, dv_s):
        i = pl.program_id(2)

        @pl.when(i == 0)
        def _():
            dk_s[...] = jnp.zeros_like(dk_s)
            dv_s[...] = jnp.zeros_like(dv_s)
            dcr_ref[...] = jnp.zeros_like(dcr_ref)

        dzf = dz_ref[...].astype(F32)
        sig = jax.nn.sigmoid(gate_ref[0].astype(F32))
        dob = (dzf * sig).astype(BF16)
        dgate_ref[0] = (dzf * o_ref[...] * sig * (1.0 - sig)).astype(BF16)
        row = i * tq + lax.broadcasted_iota(jnp.int32, (tq, tq), 0)
        col0 = lax.broadcasted_iota(jnp.int32, (tq, tq), 1)
        dqs = []
        for hh in range(2):
            sl = slice(HEAD_DIM * hh, HEAD_DIM * (hh + 1))
            q, do = q_ref[0, :, sl], dob[:, sl]
            lse_i = lse_ref[:, HEAD_DIM * hh:HEAD_DIM * hh + 1]
            cq = cb_ref[:, HEAD_DIM * hh:HEAD_DIM * hh + 1]

            def probs(j, dsum, sl=sl, q=q, do=do, lse_i=lse_i, cq=cq, hh=hh):
                rows = pl.ds(pl.multiple_of(j * tq, tq), tq)
                s = lax.dot_general(q, k_ref[0, rows, sl], NT, preferred_element_type=F32) * ATT_SCALE
                s = s + cq - cr_ref[0, 0, j, hh:hh + 1, :]
                p = jnp.where(j * tq + col0 <= row, jnp.exp(s - lse_i), 0.0)
                dp = lax.dot_general(do, v_ref[0, rows, sl], NT, preferred_element_type=F32)
                p_s[j] = p
                dp_s[j] = dp
                return dsum + jnp.sum(p * dp, axis=-1, keepdims=True)

            dsum = lax.fori_loop(0, i + 1, probs, jnp.zeros((tq, 1), F32))

            def grads(j, dq, sl=sl, q=q, do=do, dsum=dsum, hh=hh):
                rows = pl.ds(pl.multiple_of(j * tq, tq), tq)
                p = p_s[j]
                ds = p * (dp_s[j] - dsum)
                dcr_ref[0, 0, j, hh:hh + 1, :] -= jnp.sum(ds, axis=0, keepdims=True)
                dsb = ds.astype(BF16)
                dk_s[rows, sl] += lax.dot_general(dsb, q, TN, preferred_element_type=F32) * ATT_SCALE
                dv_s[rows, sl] += lax.dot_general(p.astype(BF16), do, TN, preferred_element_type=F32)
                return dq + jnp.dot(dsb, k_ref[0, rows, sl], preferred_element_type=F32) * ATT_SCALE

            dqs.append(lax.fori_loop(0, i + 1, grads, jnp.zeros((tq, HEAD_DIM), F32)))
        dq_ref[0] = jnp.concatenate(dqs, axis=-1).astype(BF16)

        @pl.when(i == nq - 1)
        def _():
            dk_ref[0] = dk_s[...].astype(BF16)
            dv_ref[0] = dv_s[...].astype(BF16)

    qry2 = pl.BlockSpec((tq, LANES), lambda b, hp, i: (b * nq + i, hp))

    def qry3(p):
        return pl.BlockSpec((1, tq, LANES), lambda b, hp, i, p=p: (p, b * nq + i, hp))

    def seq3(p):
        return pl.BlockSpec((1, seq_len, LANES), lambda b, hp, i, p=p: (p, b, hp))

    crs = pl.BlockSpec((1, 1, nq, 2, tq), lambda b, hp, i: (b, hp, 0, 0, 0))
    act = jax.ShapeDtypeStruct((1, T, D), BF16)
    return pl.pallas_call(
        body, name=name, grid=(n_seq, N_HEADS // 2, nq),
        in_specs=[qry2, qry3(0), qry3(1), qry2, qry2, qry2, crs, seq3(0), seq3(1)],
        out_specs=[qry3(0), qry3(0), seq3(0), seq3(0), crs],
        out_shape=[act, act, act, act, jax.ShapeDtypeStruct((n_seq, N_HEADS // 2, nq, 2, tq), F32)],
        scratch_shapes=[pltpu.VMEM((nq, tq, tq), F32), pltpu.VMEM((nq, tq, tq), F32),
                        pltpu.VMEM((seq_len, LANES), F32), pltpu.VMEM((seq_len, LANES), F32)],
        compiler_params=_params(3),
    )(dz, qg, qg, o, lse, cb, crow, kv, kv)


def loss_head(y, target, name):
    T = y.shape[0]
    tm = _tile(T, TM, 8)

    def body(y_ref, t_ref, dy_ref, loss_ref):
        err = y_ref[...] - t_ref[...]
        dy_ref[...] = err * (1.0 / D)
        part = 0.5 * jnp.sum(jnp.mean(err * err, axis=-1, keepdims=True), axis=0, keepdims=True)
        _accumulate(loss_ref, pl.program_id(0) == 0, jnp.broadcast_to(part, (1, LANES)))

    row = pl.BlockSpec((tm, D), lambda i: (i, 0))
    return pl.pallas_call(
        body, name=name, grid=(T // tm,), in_specs=[row, row],
        out_specs=[row, pl.BlockSpec((1, LANES), lambda i: (0, 0))],
        out_shape=[jax.ShapeDtypeStruct((T, D), F32), jax.ShapeDtypeStruct((1, LANES), F32)],
        compiler_params=_params(1),
    )(y, target)


def _adamw(w, g, m, v):
    m = ADAM_B1 * m + (1.0 - ADAM_B1) * g
    v = ADAM_B2 * v + (1.0 - ADAM_B2) * (g * g)
    m_hat = m / (1.0 - ADAM_B1 ** ADAM_STEP)
    v_hat = v / (1.0 - ADAM_B2 ** ADAM_STEP)
    delta = -ADAM_LR * (m_hat / (jnp.sqrt(v_hat) + ADAM_EPS) + ADAM_WD * w)
    return delta, m, v


def adam_sharded(w, m, v, contribs, name):
    L, R, C = w.shape
    tr = _tile(R, 256, 16)

    def body(w_ref, m_ref, v_ref, *refs):
        c_refs, (g_ref, d_ref, nm_ref, nv_ref) = refs[:2 * L], refs[2 * L:]
        l = pl.program_id(0)
        for j in range(L):
            @pl.when(l == j)
            def _(j=j):
                own_ref, recv_ref = c_refs[2 * j], c_refs[2 * j + 1]
                g = own_ref[...].astype(F32)
                for k in range(N_DEV - 1):
                    g = g + recv_ref[k].astype(F32)
                delta, nm, nv = _adamw(w_ref[0], g, m_ref[0], v_ref[0])
                g_ref[0] = g
                d_ref[0] = delta
                nm_ref[0] = nm
                nv_ref[0] = nv

    blk = pl.BlockSpec((1, tr, C), lambda l, i: (l, i, 0))
    in_specs = [blk, blk, blk]
    inputs = [w, m, v]
    for j, (own, recv) in enumerate(contribs):
        in_specs.append(pl.BlockSpec((tr, C), lambda l, i, j=j: (jnp.where(l == j, i, 0), 0)))
        in_specs.append(pl.BlockSpec((N_DEV - 1, tr, C), lambda l, i, j=j: (0, jnp.where(l == j, i, 0), 0)))
        inputs += [own, recv]
    shp = jax.ShapeDtypeStruct((L, R, C), F32)
    return pl.pallas_call(
        body, name=name, grid=(L, R // tr), in_specs=in_specs, out_specs=[blk] * 4, out_shape=[shp] * 4,
        compiler_params=_params(2),
    )(*inputs)


def adam_small(params, total, extra_grads, name):
    n, ne = len(params), len(extra_grads)

    def body(*refs):
        total_ref, extra_refs = refs[0], refs[1:1 + ne]
        ins, outs = refs[1 + ne:1 + ne + 3 * n], refs[1 + ne + 3 * n:]
        for k, (_, _, _, where) in enumerate(params):
            if isinstance(where, int):
                g = extra_refs[where][...]
            else:
                row, rows, width = where
                g = total_ref[row:row + rows, 0:width]
            delta, nm, nv = _adamw(ins[3 * k][...], g, ins[3 * k + 1][...], ins[3 * k + 2][...])
            outs[4 * k][...] = g
            outs[4 * k + 1][...] = delta
            outs[4 * k + 2][...] = nm
            outs[4 * k + 3][...] = nv

    flat = [a for w, m, v, _ in params for a in (w, m, v)]
    out_shape = [jax.ShapeDtypeStruct(w.shape, F32) for w, _, _, _ in params for _ in range(4)]
    res = pl.pallas_call(body, name=name, out_shape=out_shape)(total, *extra_grads, *flat)
    return [res[4 * k:4 * k + 4] for k in range(n)]


def _place():
    return lax.axis_index("x"), lax.axis_index("y"), lax.axis_index("c")


def _peer(place, k):
    x, y, c = place
    return x ^ (k >> 2), y ^ ((k >> 1) & 1), c ^ (k & 1)


ANY = pl.BlockSpec(memory_space=pl.ANY)
HBM = pl.BlockSpec(memory_space=pltpu.HBM)
SEM = pl.BlockSpec(memory_space=pltpu.SEMAPHORE)
EFFECT = pltpu.SideEffectType.DATAFLOW_SIDE_EFFECTING


def _in_hbm(a):
    return pltpu.with_memory_space_constraint(a, pltpu.HBM)


def place_own(shards, name):
    na = len(shards)

    def body(*refs):
        x_refs, out_refs, sems = refs[:na], refs[na:2 * na], refs[2 * na]
        x, y, c = _place()
        copies = [pltpu.make_async_copy(x_refs[a], out_refs[a].at[4 * x + 2 * y + c], sems.at[a]) for a in range(na)]
        for cp in copies:
            cp.start()
        for cp in copies:
            cp.wait()

    return pl.pallas_call(
        body, name=name, in_specs=[ANY] * na, out_specs=[ANY] * na,
        out_shape=[jax.ShapeDtypeStruct((N_DEV,) + s.shape, s.dtype) for s in shards],
        scratch_shapes=[pltpu.SemaphoreType.DMA((na,))],
    )(*shards)


def exchange_start(groups, gather, name):
    sizes = [len(g) for g in groups]
    flat = [pair for g in groups for pair in g]
    na, ng = len(flat), len(groups)

    def body(*refs):
        src_refs, land_refs = refs[:na], refs[na:2 * na]
        sem_refs = refs[2 * na:2 * na + 2 * ng]
        token = refs[-1]
        place = _place()
        me = 4 * place[0] + 2 * place[1] + place[2]
        a = 0
        for gi, n in enumerate(sizes):
            for idx in range(n):
                for k in range(1, N_DEV):
                    px, py, pc = _peer(place, k)
                    if gather:
                        src, dst = src_refs[a], land_refs[a].at[me]
                    else:
                        src, dst = src_refs[a].at[4 * px + 2 * py + pc], land_refs[a].at[k - 1]
                    pltpu.make_async_remote_copy(
                        src_ref=src, dst_ref=dst, send_sem=sem_refs[2 * gi].at[idx], recv_sem=sem_refs[2 * gi + 1].at[idx],
                        device_id=(px, py, pc), device_id_type=MESH).start()
                a += 1
        token[...] = jnp.zeros_like(token)

    srcs = [_in_hbm(s) for s, _ in flat]
    lands = [_in_hbm(l) for _, l in flat]
    out_shape = [pltpu.SemaphoreType.DMA((n,)) for n in sizes for _ in range(2)]
    out_shape += [pltpu.HBM(a.shape, a.dtype) for a in srcs + lands]
    out_shape.append(jax.ShapeDtypeStruct((8, LANES), F32))
    res = pl.pallas_call(
        body, name=name, in_specs=[HBM] * (2 * na), out_shape=out_shape,
        out_specs=[SEM] * (2 * ng) + [HBM] * (2 * na) + [pl.BlockSpec(memory_space=pltpu.VMEM)],
        input_output_aliases={i: 2 * ng + i for i in range(2 * na)},
        compiler_params=pltpu.CompilerParams(has_side_effects=EFFECT),
    )(*srcs, *lands)
    sems, thru, token = res[:2 * ng], res[2 * ng:-1], res[-1]
    started, a = [], 0
    for gi, n in enumerate(sizes):
        started.append((sems[2 * gi], sems[2 * gi + 1], thru[a:a + n], thru[na + a:na + a + n]))
        a += n
    return started, token


def exchange_wait(started, after, name):
    sizes = [len(g[2]) for g in started]
    na, ng = sum(sizes), len(started)

    def body(*refs):
        land_refs = refs[na:2 * na]
        sem_refs = refs[2 * na:2 * na + 2 * ng]
        place = _place()
        a = 0
        for gi, n in enumerate(sizes):
            for idx in range(n):
                zone = land_refs[a].at[pl.ds(0, N_DEV - 1)]
                copy = pltpu.make_async_remote_copy(
                    src_ref=zone, dst_ref=zone, send_sem=sem_refs[2 * gi].at[idx], recv_sem=sem_refs[2 * gi + 1].at[idx],
                    device_id=_peer(place, 1), device_id_type=MESH)
                copy.wait_send()
                copy.wait_recv()
                a += 1

    srcs = [s for g in started for s in g[2]]
    lands = [l for g in started for l in g[3]]
    sems = [s for g in started for s in g[:2]]
    res = pl.pallas_call(
        body, name=name, in_specs=[HBM] * (2 * na) + [SEM] * (2 * ng) + [ANY],
        out_shape=[pltpu.HBM(a.shape, a.dtype) for a in srcs + lands], out_specs=[HBM] * (2 * na),
        input_output_aliases={i: i for i in range(2 * na)},
        compiler_params=pltpu.CompilerParams(has_side_effects=EFFECT),
    )(*srcs, *lands, *sems, after)
    out, a = [], 0
    for n in sizes:
        out.append((res[a:a + n], res[na + a:na + a + n]))
        a += n
    return out


def all_reduce_small(parts, n_rows, name):
    R = n_rows
    n_parts = len(parts)

    def body(*refs):
        part_refs = refs[:n_parts]
        out_ref, buf, send_sems, recv_sems = refs[n_parts:]
        x, y, c = _place()
        me = 4 * x + 2 * y + c
        own = buf.at[me]
        own[...] = jnp.zeros((R, D), F32)
        for ref, (arr, row) in zip(part_refs, parts):
            own[row:row + arr.shape[0], 0:arr.shape[1]] = ref[...]
        copies = []
        for k in range(1, N_DEV):
            peer = (x ^ (k >> 2), y ^ ((k >> 1) & 1), c ^ (k & 1))
            copies.append(pltpu.make_async_remote_copy(
                src_ref=own, dst_ref=own, send_sem=send_sems.at[k - 1], recv_sem=recv_sems.at[k - 1],
                device_id=peer, device_id_type=MESH))
        for cp in copies:
            cp.start()
        for cp in copies:
            cp.wait()
        total = buf[0]
        for d in range(1, N_DEV):
            total = total + buf[d]
        out_ref[...] = total

    vm = pl.BlockSpec(memory_space=pltpu.VMEM)
    return pl.pallas_call(
        body, name=name, in_specs=[vm] * n_parts, out_specs=vm, out_shape=jax.ShapeDtypeStruct((R, D), F32),
        scratch_shapes=[pltpu.VMEM((N_DEV, R, D), F32), pltpu.SemaphoreType.DMA((N_DEV - 1,)),
                        pltpu.SemaphoreType.DMA((N_DEV - 1,))],
    )(*[arr for arr, _ in parts])


def _col_blocks(gathered, n_blocks):
    n, d, w = gathered.shape
    whole = gathered.transpose(1, 0, 2).reshape(d, n * w)
    return whole.reshape(d, n_blocks, n * w // n_blocks).transpose(1, 0, 2)[:, None]


def _col_shards(blocks):
    n, d, w = blocks.shape
    whole = blocks.transpose(1, 0, 2).reshape(d, n * w)
    return whole.reshape(d, N_DEV, n * w // N_DEV).transpose(1, 0, 2)


def kernel(x, ffn1_pre_g, ffn1_post_g, ffn1_w_in, ffn1_w_out, mix_pre_g, mix_post_g, ffn2_pre_g, ffn2_post_g, ffn2_w_in, ffn2_w_out, conv_w_in, conv_k, conv_w_out, kv_g, kv_w, forget_b, attn_w_qg, attn_w_o, loss_target, m_ffn1_pre_g, m_ffn1_post_g, m_ffn1_w_in, m_ffn1_w_out, m_mix_pre_g, m_mix_post_g, m_ffn2_pre_g, m_ffn2_post_g, m_ffn2_w_in, m_ffn2_w_out, m_conv_w_in, m_conv_k, m_conv_w_out, m_kv_g, m_kv_w, m_forget_b, m_attn_w_qg, m_attn_w_o, v_ffn1_pre_g, v_ffn1_post_g, v_ffn1_w_in, v_ffn1_w_out, v_mix_pre_g, v_mix_post_g, v_ffn2_pre_g, v_ffn2_post_g, v_ffn2_w_in, v_ffn2_w_out, v_conv_w_in, v_conv_k, v_conv_w_out, v_kv_g, v_kv_w, v_forget_b, v_attn_w_qg, v_attn_w_o):
    n_seq, seq_len, _ = x.shape
    T = n_seq * seq_len
    xi, yi, ci = _place()
    dev = 4 * xi + 2 * yi + ci
    x0 = x.reshape(T, D)
    target = loss_target.reshape(T, D)

    def layer(w, l):
        return w[l:l + 1].astype(BF16)

    shard_groups = [
        [layer(ffn1_w_in, 0), layer(ffn1_w_out, 0)],
        [conv_w_in.astype(BF16), conv_w_out.astype(BF16), conv_k],
        [layer(ffn2_w_in, 0), layer(ffn2_w_out, 0)],
        [kv_w.astype(BF16), layer(ffn1_w_in, 1), layer(ffn1_w_out, 1), attn_w_qg.astype(BF16), attn_w_o.astype(BF16)],
        [layer(ffn2_w_in, 1), layer(ffn2_w_out, 1)]]
    zones = place_own([s for g in shard_groups for s in g], "gather_place_own")
    zone_groups, n_done = [], 0
    for g in shard_groups:
        zone_groups.append(list(zip(g, zones[n_done:n_done + len(g)])))
        n_done += len(g)
    gathers, token = exchange_start(zone_groups, True, "gather_start")

    def gathered(k, after):
        return exchange_wait([gathers[k]], after, f"gather_wait{k}")[0][1]

    fb = jnp.pad(forget_b, (0, LANES - N_HEADS))[None]

    def vec(g, l):
        return g[l:l + 1]

    def behind(g, tok):
        return g + tok[:1, :1]

    grads_small = {}

    def ffn_fwd(xin, g_pre, g_post, w_in, w_out, tag):
        xn, (gu,) = rms_proj(xin, g_pre, [(w_in, 0)], [BF16], f"{tag}_in")
        tm = _tile(T, TM, 16)
        fbw = gu.shape[2]
        specs = [pl.BlockSpec((1, tm, fbw), lambda i, k: (k, i, 0)), pl.BlockSpec((1, tm, fbw), lambda i, k: (k + 4, i, 0))]
        a, h, y = mix_out(_swiglu_pro, [gu, gu], specs, (w_out, 0), xin, g_post, 0.5, 4, f"{tag}_out")
        return y, (xin, xn, gu, a, h)

    def ffn_bwd(dy, saved, g_pre, g_post, w_in, w_out, tag):
        xin, xn, gu, a, h = saved
        dh, dg_post, dgate, dup = post_bwd(dy, h, g_post, 0.5, (w_out, 0), 4, f"{tag}_bwd_out", gu=gu)
        dw_out = wgrad([a], [dh[None]], 4, w_out.shape, f"{tag}_dw_out")
        dx, dg_pre = pre_bwd([(dgate, (w_in, 0), 0), (dup, (w_in, 0), 4)], xin, g_pre, dy, f"{tag}_bwd_in")
        dw_in = wgrad([xn[None]], [dgate, dup], 8, w_in.shape, f"{tag}_dw_in")
        started, tok = scatter_start([dw_in.reshape(8, -1, dw_in.shape[3]), dw_out.reshape(8, -1, D)], f"{tag}_scatter_start")
        return dx, dg_pre, dg_post, tok, started

    def scatter_start(blocked, name):
        pairs = [(g, lax.empty((N_DEV - 1,) + g.shape[1:], g.dtype)) for g in blocked]
        started, tok = exchange_start([pairs], False, name)
        return started[0], tok

    w1_in, w1_out = gathered(0, token)
    x1, s_f1a = ffn_fwd(x0, vec(ffn1_pre_g, 0), vec(ffn1_post_g, 0), w1_in, w1_out, "l0_ffn1")
    cw_in_g, cw_out, ck_g = gathered(1, x1)
    cw_in = _col_blocks(cw_in_g[:, 0], 3)
    ck = ck_g[:, 0].transpose(1, 0, 2).reshape(3, D)
    xn_c, (bch,) = rms_proj(x1, vec(mix_pre_g, 0), [(cw_in, 0)], [BF16], "conv_in")
    tmc = _tile(seq_len, TM, 16)
    hb = tmc // HALO

    def cpiece(p):
        return pl.BlockSpec((1, tmc, D), lambda i, k, p=p: (p, i, 0))

    def chalo(p):
        return pl.BlockSpec((1, HALO, D), lambda i, k, p=p: (p, jnp.maximum(i * hb - 1, 0), 0))

    conv_specs = [cpiece(0), cpiece(1), cpiece(2), chalo(1), chalo(2), pl.BlockSpec((3, D), lambda i, k: (0, 0))]
    z_c, m_c, x2 = mix_out(_make_conv_pro(seq_len // tmc), [bch, bch, bch, bch, bch, ck], conv_specs, (cw_out, 0),
                           x1, vec(mix_post_g, 0), 1.0, 1, "conv_out", tm=tmc)
    w2_in, w2_out = gathered(2, x2)
    x3, s_f2a = ffn_fwd(x2, vec(ffn2_pre_g, 0), vec(ffn2_post_g, 0), w2_in, w2_out, "l0_ffn2")

    kvw_g, w1_in_b, w1_out_b, qgw_g, ow = gathered(3, x3)
    qg_w = _col_blocks(qgw_g[:, 0], 2)
    kv_whole = kvw_g.transpose(1, 0, 2).reshape(D, 2 * D + N_HEADS)
    kv_wb = kv_whole[:, :2 * D].reshape(D, 2, D).transpose(1, 0, 2)[:, None]
    f_w = jnp.pad(kv_whole[:, 2 * D:], ((0, 0), (0, LANES - N_HEADS)))[None, None]
    xn_kv, (kv, fl) = rms_proj(x3, kv_g[None], [(kv_wb, 0), (f_w, 0)], [BF16, F32], "kv_in")
    c128 = forget_fwd(fl, fb, seq_len, "forget_fwd")
    c16 = c128[:, :N_HEADS]
    cb = jnp.repeat(c16, HEAD_DIM, axis=1)
    tq = _tile(seq_len, TQ, LANES)
    crow = c16.reshape(n_seq, seq_len // tq, tq, N_HEADS // 2, 2).transpose(0, 3, 1, 4, 2)

    x4, s_f1b = ffn_fwd(x3, vec(ffn1_pre_g, 1), vec(ffn1_post_g, 1), w1_in_b, w1_out_b, "l1_ffn1")
    xn_a, (qg,) = rms_proj(x4, vec(mix_pre_g, 1), [(qg_w, 0)], [BF16], "attn_in")
    o, lse = attn_fwd(qg, kv, cb, crow, n_seq, seq_len, "attn_fwd")
    tm = _tile(T, TM, 16)
    gate_specs = [pl.BlockSpec((1, tm, D), lambda i, k: (1, i, 0)), pl.BlockSpec((tm, D), lambda i, k: (i, 0))]
    z_a, m_a, x5 = mix_out(_attn_gate_pro, [qg, o], gate_specs, (ow, 0), x4, vec(mix_post_g, 1), 1.0, 1, "attn_out")
    w2_in_b, w2_out_b = gathered(4, x5)
    x6, s_f2b = ffn_fwd(x5, vec(ffn2_pre_g, 1), vec(ffn2_post_g, 1), w2_in_b, w2_out_b, "l1_ffn2")

    dy, loss_part = loss_head(x6, target, "loss_head")

    scatters = {}
    dx5, dg, dgp, tok, scatters["ffn2", 1] = ffn_bwd(dy, s_f2b, vec(ffn2_pre_g, 1), vec(ffn2_post_g, 1), w2_in_b, w2_out_b,
                                                     "l1_ffn2")
    grads_small["ffn2_pre", 1], grads_small["ffn2_post", 1] = dg, dgp
    dm_a, dgp, dz_a = post_bwd(dx5, m_a, behind(vec(mix_post_g, 1), tok), 1.0, (ow, 0), 1, "attn_bwd_out")
    grads_small["mix_post", 1] = dgp
    dq, dgate, dk, dv, dcrow = attn_bwd(dz_a[0], qg, kv, o, lse, cb, crow, n_seq, seq_len, "attn_bwd")
    dx4, dg = pre_bwd([(dq, (qg_w, 0), 0), (dgate, (qg_w, 0), 1)], x4, vec(mix_pre_g, 1), dx5, "attn_bwd_in")
    grads_small["mix_pre", 1] = dg
    d_ow = wgrad([z_a], [dm_a[None]], 1, ow.shape, "attn_dw_o")
    d_qgw = wgrad([xn_a[None]], [dq, dgate], 2, (2, 1, D, D), "attn_dw_qg")
    scatters["attn"], tok = scatter_start([_col_shards(d_qgw[:, 0]), d_ow.reshape(8, -1, D)], "attn_scatter_start")
    dx3, dg, dgp, tok, scatters["ffn1", 1] = ffn_bwd(dx4, s_f1b, vec(ffn1_pre_g, 1), behind(vec(ffn1_post_g, 1), tok),
                                                     w1_in_b, w1_out_b, "l1_ffn1")
    grads_small["ffn1_pre", 1], grads_small["ffn1_post", 1] = dg, dgp

    dc16 = dcrow.transpose(0, 2, 4, 1, 3).reshape(T, N_HEADS)
    dfl, dfb = forget_bwd(jnp.pad(dc16, ((0, 0), (0, LANES - N_HEADS))), fl, fb, seq_len, "forget_bwd")
    dx3, dg_kv = pre_bwd([(dk, (kv_wb, 0), 0), (dv, (kv_wb, 0), 1), (dfl, (f_w, 0), 0)], x3, behind(kv_g[None], tok), dx3,
                         "kv_bwd_in")
    d_kvw = wgrad([xn_kv[None]], [dk, dv], 2, (2, 1, D, D), "kv_dw")
    d_fw = wgrad([xn_kv[None]], [dfl.astype(BF16)], 1, (1, 1, D, LANES), "forget_dw")
    d_kv_whole = jnp.concatenate([d_kvw[0, 0], d_kvw[1, 0], d_fw[0, 0, :, :N_HEADS]], axis=1)
    wshard = D * 2 + N_HEADS
    scatters["kv"], tok = scatter_start([d_kv_whole.reshape(D, N_DEV, wshard // N_DEV).transpose(1, 0, 2)], "kv_scatter_start")

    dx2, dg, dgp, tok, scatters["ffn2", 0] = ffn_bwd(dx3, s_f2a, vec(ffn2_pre_g, 0), behind(vec(ffn2_post_g, 0), tok),
                                                     w2_in, w2_out, "l0_ffn2")
    grads_small["ffn2_pre", 0], grads_small["ffn2_post", 0] = dg, dgp
    dm_c, dgp, dz_c = post_bwd(dx2, m_c, behind(vec(mix_post_g, 0), tok), 1.0, (cw_out, 0), 1, "conv_bwd_out")
    grads_small["mix_post", 0] = dgp
    dbch, d_ck = conv_bwd_mix(dz_c, bch, ck, seq_len, "conv_bwd_mix")
    dx1, dg = pre_bwd([(dbch, (cw_in, 0), 0)], x1, vec(mix_pre_g, 0), dx2, "conv_bwd_in")
    grads_small["mix_pre", 0] = dg
    d_cw_out = wgrad([z_c], [dm_c[None]], 1, cw_out.shape, "conv_dw_out")
    d_cw_in = wgrad([xn_c[None]], [dbch], 3, (3, 1, D, D), "conv_dw_in")
    scatters["conv"], tok = scatter_start([_col_shards(d_cw_in[:, 0]), d_cw_out.reshape(8, -1, D)], "conv_scatter_start")
    dx0, dg, dgp, tok, scatters["ffn1", 0] = ffn_bwd(dx1, s_f1a, vec(ffn1_pre_g, 0), behind(vec(ffn1_post_g, 0), tok),
                                                     w1_in, w1_out, "l0_ffn1")
    grads_small["ffn1_pre", 0], grads_small["ffn1_post", 0] = dg, dgp

    keys = [("ffn2", 1), "attn", ("ffn1", 1), "kv", ("ffn2", 0), "conv", ("ffn1", 0)]
    waited = exchange_wait([scatters[k] for k in keys], dx0, "scatter_wait")
    parts_of = {}
    for k, (sent, recv) in zip(keys, waited):
        parts_of[k] = [(lax.dynamic_index_in_dim(g, dev, 0, keepdims=False), r) for g, r in zip(sent, recv)]
    grads_of = {
        "ffn1_w_in": [parts_of["ffn1", 0][0], parts_of["ffn1", 1][0]], "ffn1_w_out": [parts_of["ffn1", 0][1], parts_of["ffn1", 1][1]],
        "ffn2_w_in": [parts_of["ffn2", 0][0], parts_of["ffn2", 1][0]], "ffn2_w_out": [parts_of["ffn2", 0][1], parts_of["ffn2", 1][1]],
        "conv_w_in": [parts_of["conv"][0]], "conv_w_out": [parts_of["conv"][1]], "kv_w": [parts_of["kv"][0]],
        "attn_w_qg": [parts_of["attn"][0]], "attn_w_o": [parts_of["attn"][1]]}
    sharded = {"ffn1_w_in": (ffn1_w_in, m_ffn1_w_in, v_ffn1_w_in), "ffn1_w_out": (ffn1_w_out, m_ffn1_w_out, v_ffn1_w_out),
               "ffn2_w_in": (ffn2_w_in, m_ffn2_w_in, v_ffn2_w_in), "ffn2_w_out": (ffn2_w_out, m_ffn2_w_out, v_ffn2_w_out),
               "conv_w_in": (conv_w_in, m_conv_w_in, v_conv_w_in), "conv_w_out": (conv_w_out, m_conv_w_out, v_conv_w_out),
               "kv_w": (kv_w, m_kv_w, v_kv_w), "attn_w_qg": (attn_w_qg, m_attn_w_qg, v_attn_w_qg),
               "attn_w_o": (attn_w_o, m_attn_w_o, v_attn_w_o)}
    out = {}
    for nm, (w, mm, vv) in sharded.items():
        contribs = grads_of[nm]
        shape3 = (len(contribs),) + contribs[0][0].shape
        res = adam_sharded(w.reshape(shape3), mm.reshape(shape3), vv.reshape(shape3), contribs, f"adam_{nm}")
        out[nm] = [r.reshape(w.shape) for r in res]

    small_names = ["ffn1_pre", "ffn1_post", "mix_pre", "mix_post", "ffn2_pre", "ffn2_post"]
    parts = [(grads_small[n, l], 2 * k + l) for k, n in enumerate(small_names) for l in range(2)]
    parts += [(dg_kv, 12), (dfb, 13), (d_ck, 14), (loss_part, 17)]
    total = all_reduce_small(parts, 24, "all_reduce_small")
    loss = total[17, 0]
    d_ck_mine = lax.dynamic_slice(total, (14, dev * LANES), (3, LANES))
    gains = [(ffn1_pre_g, m_ffn1_pre_g, v_ffn1_pre_g), (ffn1_post_g, m_ffn1_post_g, v_ffn1_post_g),
             (mix_pre_g, m_mix_pre_g, v_mix_pre_g), (mix_post_g, m_mix_post_g, v_mix_post_g),
             (ffn2_pre_g, m_ffn2_pre_g, v_ffn2_pre_g), (ffn2_post_g, m_ffn2_post_g, v_ffn2_post_g)]
    small_params = [(w, m, v, (2 * k, 2, D)) for k, (w, m, v) in enumerate(gains)]
    small_params += [(kv_g[None], m_kv_g[None], v_kv_g[None], (12, 1, D)),
                     (forget_b[None], m_forget_b[None], v_forget_b[None], (13, 1, N_HEADS)),
                     (conv_k[0], m_conv_k[0], v_conv_k[0], 0)]
    small_res = adam_small(small_params, total, [d_ck_mine], "adam_small")
    small_keys = [n + "_g" for n in small_names] + ["kv_g", "forget_b", "conv_k"]
    shapes = {"kv_g": kv_g.shape, "forget_b": forget_b.shape, "conv_k": conv_k.shape}
    small = [{key: res[kind].reshape(shapes.get(key, res[kind].shape)) for key, res in zip(small_keys, small_res)}
             for kind in range(4)]
    order = ["ffn1_pre_g", "ffn1_post_g", "ffn1_w_in", "ffn1_w_out", "mix_pre_g", "mix_post_g", "ffn2_pre_g", "ffn2_post_g",
             "ffn2_w_in", "ffn2_w_out", "conv_w_in", "conv_k", "conv_w_out", "kv_g", "kv_w", "forget_b", "attn_w_qg",
             "attn_w_o"]
    results = [loss, dx0.reshape(x.shape)]
    for kind in range(4):
        for nm in order:
            results.append(out[nm][kind] if nm in out else small[kind][nm])
    return tuple(results)
```

```python
import functools
import math

import jax
import jax.numpy as jnp
from jax import lax
from jax.experimental import pallas as pl
from jax.experimental.pallas import tpu as pltpu

F32, BF16 = jnp.float32, jnp.bfloat16
D = 1024
N_HEADS = 16
HEAD_DIM = 64
N_DEV = 8
RMS_EPS = 1e-6
ATT_SCALE = 1.0 / math.sqrt(HEAD_DIM)
LANES = 128
HALO = 8
TM = 512
TQ = 256
VMEM_LIMIT = 48 * 1024 * 1024
MESH = pl.DeviceIdType.MESH

ADAM_LR, ADAM_B1, ADAM_B2, ADAM_EPS, ADAM_WD, ADAM_STEP = 0.001, 0.9, 0.999, 1e-08, 0.01, 10

NT = (((1,), (1,)), ((), ()))
TN = (((0,), (0,)), ((), ()))


def _params(n_axes):
    return pltpu.CompilerParams(dimension_semantics=("arbitrary",) * n_axes, vmem_limit_bytes=VMEM_LIMIT)


def _tile(n, cap, mult):
    best = None
    for t in range(mult, min(n, cap) + 1, mult):
        if n % t == 0:
            best = t
    assert best is not None, (n, cap, mult)
    return best


def _rms_rstd(x):
    return lax.rsqrt(jnp.mean(x * x, axis=-1, keepdims=True) + RMS_EPS)


def _rms_fwd(x, g):
    return x * _rms_rstd(x) * g


def _rms_bwd(x, g, dy):
    xh = x * _rms_rstd(x)
    dyg = dy * g
    dx = _rms_rstd(x) * (dyg - xh * jnp.mean(dyg * xh, axis=-1, keepdims=True))
    return dx, jnp.sum(dy * xh, axis=0, keepdims=True)


def _accumulate(ref, first, value):
    @pl.when(first)
    def _():
        ref[...] = value

    @pl.when(jnp.logical_not(first))
    def _():
        ref[...] += value


def rms_proj(x, g, ws, out_dtypes, name):
    T = x.shape[0]
    tm = _tile(T, TM, 16)
    na = len(ws)
    nbs = [w.shape[0] for w, _ in ws]
    offs = [sum(nbs[:a]) for a in range(na)]
    nb_total = sum(nbs)

    def body(x_ref, g_ref, *refs):
        w_refs, xn_ref, o_refs, xn_s = refs[:na], refs[na], refs[na + 1:2 * na + 1], refs[2 * na + 1]
        n = pl.program_id(1)

        @pl.when(n == 0)
        def _():
            xn = _rms_fwd(x_ref[...], g_ref[...]).astype(BF16)
            xn_s[...] = xn
            xn_ref[...] = xn

        for a in range(na):
            @pl.when((n >= offs[a]) & (n < offs[a] + nbs[a]))
            def _(a=a):
                y = jnp.dot(xn_s[...], w_refs[a][0, 0], preferred_element_type=F32)
                o_refs[a][0] = y.astype(o_refs[a].dtype)

    def blk(off, nb):
        return lambda n: jnp.clip(n - off, 0, nb - 1)

    in_specs = [pl.BlockSpec((tm, D), lambda i, n: (i, 0)), pl.BlockSpec((1, D), lambda i, n: (0, 0))]
    out_specs = [pl.BlockSpec((tm, D), lambda i, n: (i, 0))]
    out_shape = [jax.ShapeDtypeStruct((T, D), BF16)]
    for (w, l), off, nb, dt in zip(ws, offs, nbs, out_dtypes):
        wb = w.shape[3]
        in_specs.append(pl.BlockSpec((1, 1, D, wb), lambda i, n, f=blk(off, nb), l=l: (f(n), l, 0, 0)))
        out_specs.append(pl.BlockSpec((1, tm, wb), lambda i, n, f=blk(off, nb): (f(n), i, 0)))
        out_shape.append(jax.ShapeDtypeStruct((nb, T, wb), dt))
    res = pl.pallas_call(
        body, name=name, grid=(T // tm, nb_total), in_specs=in_specs, out_specs=out_specs, out_shape=out_shape,
        scratch_shapes=[pltpu.VMEM((tm, D), BF16)], compiler_params=_params(2),
    )(x, g, *[w for w, _ in ws])
    return res[0], res[1:]


def mix_out(pro, pro_inputs, pro_specs, w, res, g_post, alpha, nk, name, tm=None):
    w4, l = w
    T = res.shape[0]
    tm = _tile(T, TM, 16) if tm is None else tm
    dpb = N_DEV // nk
    rows = w4.shape[2]
    kb = dpb * rows
    npi = len(pro_inputs)

    def body(*refs):
        pro_refs = refs[:npi]
        w_ref, res_ref, g_ref, z_ref, m_ref, y_ref, acc = refs[npi:]
        i, k = pl.program_id(0), pl.program_id(1)
        z = pro(i, k, *pro_refs).astype(BF16)
        z_ref[0] = z
        part = jnp.dot(z, w_ref[:, 0].reshape(kb, D), preferred_element_type=F32)
        _accumulate(acc, k == 0, part)

        @pl.when(k == nk - 1)
        def _():
            m = acc[...]
            m_ref[...] = m
            y_ref[...] = res_ref[...] + alpha * _rms_fwd(m, g_ref[...])

    row = pl.BlockSpec((tm, D), lambda i, k: (i, 0))
    in_specs = list(pro_specs) + [
        pl.BlockSpec((dpb, 1, rows, D), lambda i, k: (k, l, 0, 0)), row, pl.BlockSpec((1, D), lambda i, k: (0, 0))]
    z, m, y = pl.pallas_call(
        body, name=name, grid=(T // tm, nk), in_specs=in_specs,
        out_specs=[pl.BlockSpec((1, tm, kb), lambda i, k: (k, i, 0)), row, row],
        out_shape=[jax.ShapeDtypeStruct((nk, T, kb), BF16), jax.ShapeDtypeStruct((T, D), F32),
                   jax.ShapeDtypeStruct((T, D), F32)],
        scratch_shapes=[pltpu.VMEM((tm, D), F32)], compiler_params=_params(2),
    )(*pro_inputs, w4, res, g_post)
    return z, m, y


def post_bwd(dy, m, g_post, alpha, w, nk, name, gu=None):
    w4, l = w
    T = dy.shape[0]
    tm = _tile(T, TM, 16)
    dpb = N_DEV // nk
    rows = w4.shape[2]
    kb = dpb * rows
    swiglu = gu is not None

    def body(dy_ref, m_ref, g_ref, w_ref, *refs):
        if swiglu:
            gate_ref, up_ref, dm_ref, dg_ref, dgate_ref, dup_ref, dm_s = refs
        else:
            dm_ref, dg_ref, dz_ref, dm_s = refs
        i, k = pl.program_id(0), pl.program_id(1)

        @pl.when(k == 0)
        def _():
            dm, dg = _rms_bwd(m_ref[...], g_ref[...], alpha * dy_ref[...])
            dm_s[...] = dm.astype(BF16)
            dm_ref[...] = dm.astype(BF16)
            _accumulate(dg_ref, i == 0, dg)

        dz = lax.dot_general(dm_s[...], w_ref[:, 0].reshape(kb, D), NT, preferred_element_type=F32)
        if swiglu:
            gate, up = gate_ref[0].astype(F32), up_ref[0].astype(F32)
            sig = jax.nn.sigmoid(gate)
            dgate_ref[0] = (dz * up * sig * (1.0 + gate * (1.0 - sig))).astype(BF16)
            dup_ref[0] = (dz * gate * sig).astype(BF16)
        else:
            dz_ref[0] = dz.astype(BF16)

    row = pl.BlockSpec((tm, D), lambda i, k: (i, 0))
    vec = pl.BlockSpec((1, D), lambda i, k: (0, 0))
    blkk = pl.BlockSpec((1, tm, kb), lambda i, k: (k, i, 0))
    in_specs = [row, row, vec, pl.BlockSpec((dpb, 1, rows, D), lambda i, k: (k, l, 0, 0))]
    inputs = [dy, m, g_post, w4]
    out_specs = [row, vec, blkk]
    out_shape = [jax.ShapeDtypeStruct((T, D), BF16), jax.ShapeDtypeStruct((1, D), F32),
                 jax.ShapeDtypeStruct((nk, T, kb), BF16)]
    if swiglu:
        in_specs += [blkk, pl.BlockSpec((1, tm, kb), lambda i, k: (k + nk, i, 0))]
        inputs += [gu, gu]
        out_specs.append(blkk)
        out_shape.append(jax.ShapeDtypeStruct((nk, T, kb), BF16))
    return pl.pallas_call(
        body, name=name, grid=(T // tm, nk), in_specs=in_specs, out_specs=out_specs, out_shape=out_shape,
        scratch_shapes=[pltpu.VMEM((tm, D), BF16)], compiler_params=_params(2),
    )(*inputs)


def pre_bwd(pieces, x, g_pre, dres, name):
    T = x.shape[0]
    tm = _tile(T, TM, 16)
    na = len(pieces)
    nbs = [p[0].shape[0] for p in pieces]
    offs = [sum(nbs[:a]) for a in range(na)]
    nb_total = sum(nbs)

    def body(*refs):
        dz_refs, w_refs = refs[:na], refs[na:2 * na]
        x_ref, g_ref, dres_ref, dx_ref, dg_ref, acc = refs[2 * na:]
        i, n = pl.program_id(0), pl.program_id(1)
        for a in range(na):
            @pl.when((n >= offs[a]) & (n < offs[a] + nbs[a]))
            def _(a=a):
                part = lax.dot_general(dz_refs[a][0].astype(BF16), w_refs[a][0, 0], NT, preferred_element_type=F32)
                _accumulate(acc, n == 0, part)

        @pl.when(n == nb_total - 1)
        def _():
            dx, dg = _rms_bwd(x_ref[...], g_ref[...], acc[...])
            dx_ref[...] = dres_ref[...] + dx
            _accumulate(dg_ref, i == 0, dg)

    def blk(off, nb):
        return lambda n: jnp.clip(n - off, 0, nb - 1)

    row = pl.BlockSpec((tm, D), lambda i, n: (i, 0))
    vec = pl.BlockSpec((1, D), lambda i, n: (0, 0))
    dz_specs, w_specs = [], []
    for (dz, (w4, l), w_off), off, nb in zip(pieces, offs, nbs):
        wb = dz.shape[2]
        dz_specs.append(pl.BlockSpec((1, tm, wb), lambda i, n, f=blk(off, nb): (f(n), i, 0)))
        w_specs.append(pl.BlockSpec((1, 1, D, wb), lambda i, n, f=blk(off, nb), l=l, o=w_off: (o + f(n), l, 0, 0)))
    return pl.pallas_call(
        body, name=name, grid=(T // tm, nb_total), in_specs=dz_specs + w_specs + [row, vec, row],
        out_specs=[row, vec], out_shape=[jax.ShapeDtypeStruct((T, D), F32), jax.ShapeDtypeStruct((1, D), F32)],
        scratch_shapes=[pltpu.VMEM((tm, D), F32)], compiler_params=_params(2),
    )(*[p[0] for p in pieces], *[p[1][0] for p in pieces], x, g_pre, dres)


def wgrad(a_list, b_list, nout, out4_shape, name):
    T = a_list[0].shape[1]
    tm = _tile(T, TM, 16)
    nt = T // tm
    dpb = out4_shape[0] // nout
    _, _, R, C = out4_shape
    na, nb = len(a_list), len(b_list)

    def spans(arrs):
        ns = [a.shape[0] for a in arrs]
        return ns, [sum(ns[:k]) for k in range(len(ns))], sum(ns)

    a_ns, a_offs, a_tot = spans(a_list)
    b_ns, b_offs, b_tot = spans(b_list)
    assert a_tot in (1, nout) and b_tot in (1, nout)

    def body(*refs):
        a_refs, b_refs = refs[:na], refs[na:na + nb]
        out_ref, acc = refs[-2:]
        p, t = pl.program_id(0), pl.program_id(1)
        for ia in range(na):
            for ib in range(nb):
                conds = []
                if a_tot > 1:
                    conds += [p >= a_offs[ia], p < a_offs[ia] + a_ns[ia]]
                if b_tot > 1:
                    conds += [p >= b_offs[ib], p < b_offs[ib] + b_ns[ib]]

                def work(ia=ia, ib=ib):
                    part = lax.dot_general(a_refs[ia][0], b_refs[ib][0], TN, preferred_element_type=F32)
                    _accumulate(acc, t == 0, part)

                if conds:
                    pl.when(functools.reduce(jnp.logical_and, conds))(work)
                else:
                    work()

        @pl.when(t == nt - 1)
        def _():
            out_ref[...] = acc[...].astype(BF16).reshape(dpb, 1, R, C)

    def blk(off, n, tot):
        if tot == 1:
            return lambda p: 0
        return lambda p: jnp.clip(p - off, 0, n - 1)

    in_specs = []
    for arrs, ns, offs, tot in ((a_list, a_ns, a_offs, a_tot), (b_list, b_ns, b_offs, b_tot)):
        for arr, n, off in zip(arrs, ns, offs):
            in_specs.append(pl.BlockSpec((1, tm, arr.shape[2]), lambda p, t, f=blk(off, n, tot): (f(p), t, 0)))
    return pl.pallas_call(
        body, name=name, grid=(nout, nt), in_specs=in_specs,
        out_specs=pl.BlockSpec((dpb, 1, R, C), lambda p, t: (p, 0, 0, 0)),
        out_shape=jax.ShapeDtypeStruct(out4_shape, BF16),
        scratch_shapes=[pltpu.VMEM((dpb * R, C), F32)], compiler_params=_params(2),
    )(*a_list, *b_list)


def _swiglu_pro(i, k, gate_ref, up_ref):
    gate, up = gate_ref[0].astype(F32), up_ref[0].astype(F32)
    return gate * jax.nn.sigmoid(gate) * up


def _attn_gate_pro(i, k, gate_ref, o_ref):
    return jax.nn.sigmoid(gate_ref[0].astype(F32)) * o_ref[...].astype(F32)


def _shift_rows(u, halo, d):
    rolled = pltpu.roll(u, d, 0)
    row = lax.broadcasted_iota(jnp.int32, u.shape, 0)
    for r in range(d):
        rolled = jnp.where(row == r, halo[HALO - d + r:HALO - d + r + 1, :], rolled)
    return rolled


def _advance_rows(u, halo, d):
    n = u.shape[0]
    rolled = pltpu.roll(u, n - d, 0)
    row = lax.broadcasted_iota(jnp.int32, u.shape, 0)
    for r in range(d):
        rolled = jnp.where(row == n - d + r, halo[r:r + 1, :], rolled)
    return rolled


def _make_conv_pro(tiles_per_seq):
    def pro(i, k, b_ref, c_ref, h_ref, ch_ref, hh_ref, ck_ref):
        u = c_ref[0].astype(F32) * h_ref[0].astype(F32)
        first = (i % tiles_per_seq) == 0
        halo = jnp.where(first, 0.0, ch_ref[0].astype(F32) * hh_ref[0].astype(F32))
        ck = ck_ref[...]
        conv = ck[2:3, :] * u + ck[1:2, :] * _shift_rows(u, halo, 1) + ck[0:1, :] * _shift_rows(u, halo, 2)
        return b_ref[0].astype(F32) * conv
    return pro


def conv_bwd_mix(dz, bch, conv_k, seq_len, name):
    T = dz.shape[1]
    tm = _tile(seq_len, TM, 16)
    tps = seq_len // tm
    nt = T // tm
    hb = tm // HALO

    def body(dz_ref, b_ref, c_ref, h_ref, cp_ref, hp_ref, dzn_ref, bn_ref, ck_ref, dbch_ref, dk_ref):
        i = pl.program_id(0)
        first = (i % tps) == 0
        last = (i % tps) == tps - 1
        b, c, h = b_ref[0].astype(F32), c_ref[0].astype(F32), h_ref[0].astype(F32)
        dzt = dz_ref[0].astype(F32)
        u = c * h
        prev = jnp.where(first, 0.0, cp_ref[0].astype(F32) * hp_ref[0].astype(F32))
        u1, u2 = _shift_rows(u, prev, 1), _shift_rows(u, prev, 2)
        ck = ck_ref[...]
        conv = ck[2:3, :] * u + ck[1:2, :] * u1 + ck[0:1, :] * u2
        dconv = dzt * b
        nxt = jnp.where(last, 0.0, dzn_ref[0].astype(F32) * bn_ref[0].astype(F32))
        du = ck[2:3, :] * dconv + ck[1:2, :] * _advance_rows(dconv, nxt, 1) + ck[0:1, :] * _advance_rows(dconv, nxt, 2)
        dbch_ref[0] = (dzt * conv).astype(BF16)
        dbch_ref[1] = (du * h).astype(BF16)
        dbch_ref[2] = (du * c).astype(BF16)
        tap = lax.broadcasted_iota(jnp.int32, (3, D), 0)
        dk = jnp.where(tap == 0, jnp.sum(dconv * u2, axis=0, keepdims=True),
                       jnp.where(tap == 1, jnp.sum(dconv * u1, axis=0, keepdims=True),
                                 jnp.sum(dconv * u, axis=0, keepdims=True)))
        _accumulate(dk_ref, i == 0, dk)

    def piece(p):
        return pl.BlockSpec((1, tm, D), lambda i, p=p: (p, i, 0))

    def prev(p):
        return pl.BlockSpec((1, HALO, D), lambda i, p=p: (p, jnp.maximum(i * hb - 1, 0), 0))

    def nxt(p):
        return pl.BlockSpec((1, HALO, D), lambda i, p=p: (p, jnp.minimum((i + 1) * hb, nt * hb - 1), 0))

    return pl.pallas_call(
        body, name=name, grid=(nt,),
        in_specs=[piece(0), piece(0), piece(1), piece(2), prev(1), prev(2), nxt(0), nxt(0),
                  pl.BlockSpec((3, D), lambda i: (0, 0))],
        out_specs=[pl.BlockSpec((3, tm, D), lambda i: (0, i, 0)), pl.BlockSpec((3, D), lambda i: (0, 0))],
        out_shape=[jax.ShapeDtypeStruct((3, T, D), BF16), jax.ShapeDtypeStruct((3, D), F32)],
        compiler_params=_params(1),
    )(dz, bch, bch, bch, bch, bch, dz, bch, conv_k)


def _log_sigmoid(x):
    return jnp.minimum(x, 0.0) - jnp.log(1.0 + jnp.exp(-jnp.abs(x)))


def forget_fwd(fl, fb, seq_len, name):
    T = fl.shape[1]

    def body(fl_ref, fb_ref, c_ref):
        c = _log_sigmoid(fl_ref[0] + fb_ref[...])
        row = lax.broadcasted_iota(jnp.int32, c.shape, 0)
        k = 1
        while k < seq_len:
            c = c + jnp.where(row >= k, pltpu.roll(c, k, 0), 0.0)
            k *= 2
        c_ref[...] = c

    return pl.pallas_call(
        body, name=name, grid=(T // seq_len,),
        in_specs=[pl.BlockSpec((1, seq_len, LANES), lambda b: (0, b, 0)), pl.BlockSpec((1, LANES), lambda b: (0, 0))],
        out_specs=pl.BlockSpec((seq_len, LANES), lambda b: (b, 0)),
        out_shape=jax.ShapeDtypeStruct((T, LANES), F32), compiler_params=_params(1),
    )(fl, fb)


def forget_bwd(dc, fl, fb, seq_len, name):
    T = dc.shape[0]

    def body(dc_ref, fl_ref, fb_ref, dfl_ref, dfb_ref):
        b = pl.program_id(0)
        r = dc_ref[...]
        row = lax.broadcasted_iota(jnp.int32, r.shape, 0)
        k = 1
        while k < seq_len:
            r = r + jnp.where(row < seq_len - k, pltpu.roll(r, seq_len - k, 0), 0.0)
            k *= 2
        dfl = r * jax.nn.sigmoid(-(fl_ref[0] + fb_ref[...]))
        dfl_ref[0] = dfl
        _accumulate(dfb_ref, b == 0, jnp.sum(dfl, axis=0, keepdims=True))

    return pl.pallas_call(
        body, name=name, grid=(T // seq_len,),
        in_specs=[pl.BlockSpec((seq_len, LANES), lambda b: (b, 0)), pl.BlockSpec((1, seq_len, LANES), lambda b: (0, b, 0)),
                  pl.BlockSpec((1, LANES), lambda b: (0, 0))],
        out_specs=[pl.BlockSpec((1, seq_len, LANES), lambda b: (0, b, 0)), pl.BlockSpec((1, LANES), lambda b: (0, 0))],
        out_shape=[jax.ShapeDtypeStruct((1, T, LANES), F32), jax.ShapeDtypeStruct((1, LANES), F32)],
        compiler_params=_params(1),
    )(dc, fl, fb)


def attn_fwd(qg, kv, cb, crow, n_seq, seq_len, name):
    T = n_seq * seq_len
    tq = _tile(seq_len, TQ, LANES)
    nq = seq_len // tq

    def body(q_ref, k_ref, v_ref, cb_ref, cr_ref, o_ref, lse_ref):
        i = pl.program_id(2)
        row = i * tq + lax.broadcasted_iota(jnp.int32, (tq, tq), 0)
        col0 = lax.broadcasted_iota(jnp.int32, (tq, tq), 1)
        outs, lses = [], []
        for hh in range(2):
            sl = slice(HEAD_DIM * hh, HEAD_DIM * (hh + 1))
            q = q_ref[0, :, sl]
            cq = cb_ref[:, HEAD_DIM * hh:HEAD_DIM * hh + 1]

            def step(j, carry, sl=sl, q=q, cq=cq, hh=hh):
                m, l, acc = carry
                rows = pl.ds(pl.multiple_of(j * tq, tq), tq)
                k, v = k_ref[0, rows, sl], v_ref[0, rows, sl]
                ck = cr_ref[0, 0, j, hh:hh + 1, :]
                s = lax.dot_general(q, k, NT, preferred_element_type=F32) * ATT_SCALE + cq - ck
                s = jnp.where(j * tq + col0 <= row, s, -1e30)
                m_new = jnp.maximum(m, jnp.max(s, axis=-1, keepdims=True))
                a = jnp.exp(m - m_new)
                p = jnp.exp(s - m_new)
                l = a * l + jnp.sum(p, axis=-1, keepdims=True)
                acc = a * acc + jnp.dot(p.astype(BF16), v, preferred_element_type=F32)
                return m_new, l, acc

            init = (jnp.full((tq, 1), -1e30, F32), jnp.zeros((tq, 1), F32), jnp.zeros((tq, HEAD_DIM), F32))
            m, l, acc = lax.fori_loop(0, i + 1, step, init)
            outs.append(acc / l)
            lses.append(jnp.broadcast_to(m + jnp.log(l), (tq, HEAD_DIM)))
        o_ref[...] = jnp.concatenate(outs, axis=-1)
        lse_ref[...] = jnp.concatenate(lses, axis=-1)

    qspec = pl.BlockSpec((tq, LANES), lambda b, hp, i: (b * nq + i, hp))
    return pl.pallas_call(
        body, name=name, grid=(n_seq, N_HEADS // 2, nq),
        in_specs=[pl.BlockSpec((1, tq, LANES), lambda b, hp, i: (0, b * nq + i, hp)),
                  pl.BlockSpec((1, seq_len, LANES), lambda b, hp, i: (0, b, hp)),
                  pl.BlockSpec((1, seq_len, LANES), lambda b, hp, i: (1, b, hp)),
                  qspec, pl.BlockSpec((1, 1, nq, 2, tq), lambda b, hp, i: (b, hp, 0, 0, 0))],
        out_specs=[qspec, qspec],
        out_shape=[jax.ShapeDtypeStruct((T, D), F32), jax.ShapeDtypeStruct((T, D), F32)],
        compiler_params=_params(3),
    )(qg, kv, kv, cb, crow)


def attn_bwd(dz, qg, kv, o, lse, cb, crow, n_seq, seq_len, name):
    T = n_seq * seq_len
    tq = _tile(seq_len, TQ, LANES)
    nq = seq_len // tq

    def body(dz_ref, q_ref, gate_ref, o_ref, lse_ref, cb_ref, cr_ref, k_ref, v_ref,
             dq_ref, dgate_ref, dk_ref, dv_ref, dcr_ref, p_s, dp_s, dk_s, dv_s):
        i = pl.program_id(2)

        @pl.when(i == 0)
        def _():
            dk_s[...] = jnp.zeros_like(dk_s)
            dv_s[...] = jnp.zeros_like(dv_s)
            dcr_ref[...] = jnp.zeros_like(dcr_ref)

        dzf = dz_ref[...].astype(F32)
        sig = jax.nn.sigmoid(gate_ref[0].astype(F32))
        dob = (dzf * sig).astype(BF16)
        dgate_ref[0] = (dzf * o_ref[...] * sig * (1.0 - sig)).astype(BF16)
        row = i * tq + lax.broadcasted_iota(jnp.int32, (tq, tq), 0)
        col0 = lax.broadcasted_iota(jnp.int32, (tq, tq), 1)
        dqs = []
        for hh in range(2):
            sl = slice(HEAD_DIM * hh, HEAD_DIM * (hh + 1))
            q, do = q_ref[0, :, sl], dob[:, sl]
            lse_i = lse_ref[:, HEAD_DIM * hh:HEAD_DIM * hh + 1]
            cq = cb_ref[:, HEAD_DIM * hh:HEAD_DIM * hh + 1]

            def probs(j, dsum, sl=sl, q=q, do=do, lse_i=lse_i, cq=cq, hh=hh):
                rows = pl.ds(pl.multiple_of(j * tq, tq), tq)
                s = lax.dot_general(q, k_ref[0, rows, sl], NT, preferred_element_type=F32) * ATT_SCALE
                s = s + cq - cr_ref[0, 0, j, hh:hh + 1, :]
                p = jnp.where(j * tq + col0 <= row, jnp.exp(s - lse_i), 0.0)
                dp = lax.dot_general(do, v_ref[0, rows, sl], NT, preferred_element_type=F32)
                p_s[j] = p
                dp_s[j] = dp
                return dsum + jnp.sum(p * dp, axis=-1, keepdims=True)

            dsum = lax.fori_loop(0, i + 1, probs, jnp.zeros((tq, 1), F32))

            def grads(j, dq, sl=sl, q=q, do=do, dsum=dsum, hh=hh):
                rows = pl.ds(pl.multiple_of(j * tq, tq), tq)
                p = p_s[j]
                ds = p * (dp_s[j] - dsum)
                dcr_ref[0, 0, j, hh:hh + 1, :] -= jnp.sum(ds, axis=0, keepdims=True)
                dsb = ds.astype(BF16)
                dk_s[rows, sl] += lax.dot_general(dsb, q, TN, preferred_element_type=F32) * ATT_SCALE
                dv_s[rows, sl] += lax.dot_general(p.astype(BF16), do, TN, preferred_element_type=F32)
                return dq + jnp.dot(dsb, k_ref[0, rows, sl], preferred_element_type=F32) * ATT_SCALE

            dqs.append(lax.fori_loop(0, i + 1, grads, jnp.zeros((tq, HEAD_DIM), F32)))
        dq_ref[0] = jnp.concatenate(dqs, axis=-1).astype(BF16)

        @pl.when(i == nq - 1)
        def _():
            dk_ref[0] = dk_s[...].astype(BF16)
            dv_ref[0] = dv_s[...].astype(BF16)

    qry2 = pl.BlockSpec((tq, LANES), lambda b, hp, i: (b * nq + i, hp))

    def qry3(p):
        return pl.BlockSpec((1, tq, LANES), lambda b, hp, i, p=p: (p, b * nq + i, hp))

    def seq3(p):
        return pl.BlockSpec((1, seq_len, LANES), lambda b, hp, i, p=p: (p, b, hp))

    crs = pl.BlockSpec((1, 1, nq, 2, tq), lambda b, hp, i: (b, hp, 0, 0, 0))
    act = jax.ShapeDtypeStruct((1, T, D), BF16)
    return pl.pallas_call(
        body, name=name, grid=(n_seq, N_HEADS // 2, nq),
        in_specs=[qry2, qry3(0), qry3(1), qry2, qry2, qry2, crs, seq3(0), seq3(1)],
        out_specs=[qry3(0), qry3(0), seq3(0), seq3(0), crs],
        out_shape=[act, act, act, act, jax.ShapeDtypeStruct((n_seq, N_HEADS // 2, nq, 2, tq), F32)],
        scratch_shapes=[pltpu.VMEM((nq, tq, tq), F32), pltpu.VMEM((nq, tq, tq), F32),
                        pltpu.VMEM((seq_len, LANES), F32), pltpu.VMEM((seq_len, LANES), F32)],
        compiler_params=_params(3),
    )(dz, qg, qg, o, lse, cb, crow, kv, kv)


def loss_head(y, target, name):
    T = y.shape[0]
    tm = _tile(T, TM, 8)

    def body(y_ref, t_ref, dy_ref, loss_ref):
        err = y_ref[...] - t_ref[...]
        dy_ref[...] = err * (1.0 / D)
        part = 0.5 * jnp.sum(jnp.mean(err * err, axis=-1, keepdims=True), axis=0, keepdims=True)
        _accumulate(loss_ref, pl.program_id(0) == 0, jnp.broadcast_to(part, (1, LANES)))

    row = pl.BlockSpec((tm, D), lambda i: (i, 0))
    return pl.pallas_call(
        body, name=name, grid=(T // tm,), in_specs=[row, row],
        out_specs=[row, pl.BlockSpec((1, LANES), lambda i: (0, 0))],
        out_shape=[jax.ShapeDtypeStruct((T, D), F32), jax.ShapeDtypeStruct((1, LANES), F32)],
        compiler_params=_params(1),
    )(y, target)


def _adamw(w, g, m, v):
    m = ADAM_B1 * m + (1.0 - ADAM_B1) * g
    v = ADAM_B2 * v + (1.0 - ADAM_B2) * (g * g)
    m_hat = m / (1.0 - ADAM_B1 ** ADAM_STEP)
    v_hat = v / (1.0 - ADAM_B2 ** ADAM_STEP)
    delta = -ADAM_LR * (m_hat / (jnp.sqrt(v_hat) + ADAM_EPS) + ADAM_WD * w)
    return delta, m, v


def adam_sharded(w, m, v, contribs, name):
    L, R, C = w.shape
    tr = _tile(R, 256, 16)

    def body(w_ref, m_ref, v_ref, *refs):
        c_refs, (g_ref, d_ref, nm_ref, nv_ref) = refs[:2 * L], refs[2 * L:]
        l = pl.program_id(0)
        for j in range(L):
            @pl.when(l == j)
            def _(j=j):
                own_ref, recv_ref = c_refs[2 * j], c_refs[2 * j + 1]
                g = own_ref[...].astype(F32)
                for k in range(N_DEV - 1):
                    g = g + recv_ref[k].astype(F32)
                delta, nm, nv = _adamw(w_ref[0], g, m_ref[0], v_ref[0])
                g_ref[0] = g
                d_ref[0] = delta
                nm_ref[0] = nm
                nv_ref[0] = nv

    blk = pl.BlockSpec((1, tr, C), lambda l, i: (l, i, 0))
    in_specs = [blk, blk, blk]
    inputs = [w, m, v]
    for j, (own, recv) in enumerate(contribs):
        in_specs.append(pl.BlockSpec((tr, C), lambda l, i, j=j: (jnp.where(l == j, i, 0), 0)))
        in_specs.append(pl.BlockSpec((N_DEV - 1, tr, C), lambda l, i, j=j: (0, jnp.where(l == j, i, 0), 0)))
        inputs += [own, recv]
    shp = jax.ShapeDtypeStruct((L, R, C), F32)
    return pl.pallas_call(
        body, name=name, grid=(L, R // tr), in_specs=in_specs, out_specs=[blk] * 4, out_shape=[shp] * 4,
        compiler_params=_params(2),
    )(*inputs)


def adam_small(params, total, extra_grads, name):
    n, ne = len(params), len(extra_grads)

    def body(*refs):
        total_ref, extra_refs = refs[0], refs[1:1 + ne]
        ins, outs = refs[1 + ne:1 + ne + 3 * n], refs[1 + ne + 3 * n:]
        for k, (_, _, _, where) in enumerate(params):
            if isinstance(where, int):
                g = extra_refs[where][...]
            else:
                row, rows, width = where
                g = total_ref[row:row + rows, 0:width]
            delta, nm, nv = _adamw(ins[3 * k][...], g, ins[3 * k + 1][...], ins[3 * k + 2][...])
            outs[4 * k][...] = g
            outs[4 * k + 1][...] = delta
            outs[4 * k + 2][...] = nm
            outs[4 * k + 3][...] = nv

    flat = [a for w, m, v, _ in params for a in (w, m, v)]
    out_shape = [jax.ShapeDtypeStruct(w.shape, F32) for w, _, _, _ in params for _ in range(4)]
    res = pl.pallas_call(body, name=name, out_shape=out_shape)(total, *extra_grads, *flat)
    return [res[4 * k:4 * k + 4] for k in range(n)]


def _place():
    return lax.axis_index("x"), lax.axis_index("y"), lax.axis_index("c")


def _peer(place, k):
    x, y, c = place
    return x ^ (k >> 2), y ^ ((k >> 1) & 1), c ^ (k & 1)


ANY = pl.BlockSpec(memory_space=pl.ANY)
HBM = pl.BlockSpec(memory_space=pltpu.HBM)
SEM = pl.BlockSpec(memory_space=pltpu.SEMAPHORE)
EFFECT = pltpu.SideEffectType.DATAFLOW_SIDE_EFFECTING


def _in_hbm(a):
    return pltpu.with_memory_space_constraint(a, pltpu.HBM)


def _number(place):
    return 4 * place[0] + 2 * place[1] + place[2]


SLOTS = {"gather_like": 4, "gather_pass": 3, "scatter": 7}


def _plan(mode, src_ref, land_ref, place):
    if mode == "gather_like":
        return [(src_ref, land_ref.at[_number(place)], _peer(place, k)) for k in (1, 2, 4, 6)]
    if mode == "gather_pass":
        slots = [land_ref.at[_number(_peer(place, k))] for k in (2, 4, 6)]
        return [(slot, slot, _peer(place, 1)) for slot in slots]
    return [(src_ref.at[_number(_peer(place, k))], land_ref.at[k - 1], _peer(place, k)) for k in range(1, N_DEV)]


def _exchange(name, groups, start, afters):
    sizes = [len(g[2]) for g in groups]
    na, ng = sum(sizes), len(groups)
    waits = groups[0][0] is not None
    n_in_sems = 2 * ng if waits else 0
    n_out_sems = 2 * ng if start else 0

    def body(*refs):
        src_refs, land_refs = refs[:na], refs[na:2 * na]
        in_sems = refs[2 * na:2 * na + n_in_sems]
        outs = refs[2 * na + n_in_sems + len(afters):]
        place = _place()
        a = 0
        for gi, n in enumerate(sizes):
            for idx in range(n):
                if waits:
                    zone = land_refs[a].at[pl.ds(0, groups[gi][4])]
                    copy = pltpu.make_async_remote_copy(
                        src_ref=zone, dst_ref=zone, send_sem=in_sems[2 * gi].at[idx], recv_sem=in_sems[2 * gi + 1].at[idx],
                        device_id=_peer(place, 1), device_id_type=MESH)
                    copy.wait_send()
                    copy.wait_recv()
                if start:
                    for src, dst, peer in _plan(start, src_refs[a], land_refs[a], place):
                        pltpu.make_async_remote_copy(
                            src_ref=src, dst_ref=dst, send_sem=outs[2 * gi].at[idx], recv_sem=outs[2 * gi + 1].at[idx],
                            device_id=peer, device_id_type=MESH).start()
                a += 1
        if start:
            outs[-1][...] = jnp.zeros_like(outs[-1])

    srcs = [_in_hbm(s) for g in groups for s in g[2]]
    lands = [_in_hbm(l) for g in groups for l in g[3]]
    sems = [s for g in groups for s in g[:2]] if waits else []
    out_shape = [pltpu.SemaphoreType.DMA((n,)) for n in sizes for _ in range(2)] if start else []
    out_shape += [pltpu.HBM(a.shape, a.dtype) for a in srcs + lands]
    out_specs = [SEM] * n_out_sems + [HBM] * (2 * na)
    if start:
        out_shape.append(jax.ShapeDtypeStruct((8, LANES), F32))
        out_specs.append(pl.BlockSpec(memory_space=pltpu.VMEM))
    res = pl.pallas_call(
        body, name=name, in_specs=[HBM] * (2 * na) + [SEM] * n_in_sems + [ANY] * len(afters),
        out_shape=out_shape, out_specs=out_specs,
        input_output_aliases={i: n_out_sems + i for i in range(2 * na)},
        compiler_params=pltpu.CompilerParams(has_side_effects=EFFECT),
    )(*srcs, *lands, *sems, *afters)
    new_sems, thru = res[:n_out_sems], res[n_out_sems:n_out_sems + 2 * na]
    out, a = [], 0
    for gi, n in enumerate(sizes):
        pair = (new_sems[2 * gi], new_sems[2 * gi + 1]) if start else (None, None)
        out.append(pair + (thru[a:a + n], thru[na + a:na + a + n], SLOTS.get(start, 0)))
        a += n
    return out, (res[-1] if start else None)


def exchange_start(pair_groups, mode, name):
    groups = [(None, None, [s for s, _ in g], [l for _, l in g], 0) for g in pair_groups]
    return _exchange(name, groups, mode, ())


def exchange_relay(groups, mode, afters, name):
    return _exchange(name, groups, mode, afters)


def exchange_wait(groups, afters, name):
    done, _ = _exchange(name, groups, None, afters)
    return [(g[2], g[3]) for g in done]


def all_reduce_small(parts, n_rows, name):
    R = n_rows
    n_parts = len(parts)

    def body(*refs):
        part_refs = refs[:n_parts]
        out_ref, buf, send_sems, recv_sems = refs[n_parts:]
        x, y, c = _place()
        me = 4 * x + 2 * y + c
        own = buf.at[me]
        own[...] = jnp.zeros((R, D), F32)
        for ref, (arr, row) in zip(part_refs, parts):
            own[row:row + arr.shape[0], 0:arr.shape[1]] = ref[...]
        copies = []
        for k in range(1, N_DEV):
            peer = (x ^ (k >> 2), y ^ ((k >> 1) & 1), c ^ (k & 1))
            copies.append(pltpu.make_async_remote_copy(
                src_ref=own, dst_ref=own, send_sem=send_sems.at[k - 1], recv_sem=recv_sems.at[k - 1],
                device_id=peer, device_id_type=MESH))
        for cp in copies:
            cp.start()
        for cp in copies:
            cp.wait()
        total = buf[0]
        for d in range(1, N_DEV):
            total = total + buf[d]
        out_ref[...] = total

    vm = pl.BlockSpec(memory_space=pltpu.VMEM)
    return pl.pallas_call(
        body, name=name, in_specs=[vm] * n_parts, out_specs=vm, out_shape=jax.ShapeDtypeStruct((R, D), F32),
        scratch_shapes=[pltpu.VMEM((N_DEV, R, D), F32), pltpu.SemaphoreType.DMA((N_DEV - 1,)),
                        pltpu.SemaphoreType.DMA((N_DEV - 1,))],
    )(*[arr for arr, _ in parts])


def _col_blocks(gathered, n_blocks):
    n, d, w = gathered.shape
    whole = gathered.transpose(1, 0, 2).reshape(d, n * w)
    return whole.reshape(d, n_blocks, n * w // n_blocks).transpose(1, 0, 2)[:, None]


def _col_shards(blocks):
    n, d, w = blocks.shape
    whole = blocks.transpose(1, 0, 2).reshape(d, n * w)
    return whole.reshape(d, N_DEV, n * w // N_DEV).transpose(1, 0, 2)


def kernel(x, ffn1_pre_g, ffn1_post_g, ffn1_w_in, ffn1_w_out, mix_pre_g, mix_post_g, ffn2_pre_g, ffn2_post_g, ffn2_w_in, ffn2_w_out, conv_w_in, conv_k, conv_w_out, kv_g, kv_w, forget_b, attn_w_qg, attn_w_o, loss_target, m_ffn1_pre_g, m_ffn1_post_g, m_ffn1_w_in, m_ffn1_w_out, m_mix_pre_g, m_mix_post_g, m_ffn2_pre_g, m_ffn2_post_g, m_ffn2_w_in, m_ffn2_w_out, m_conv_w_in, m_conv_k, m_conv_w_out, m_kv_g, m_kv_w, m_forget_b, m_attn_w_qg, m_attn_w_o, v_ffn1_pre_g, v_ffn1_post_g, v_ffn1_w_in, v_ffn1_w_out, v_mix_pre_g, v_mix_post_g, v_ffn2_pre_g, v_ffn2_post_g, v_ffn2_w_in, v_ffn2_w_out, v_conv_w_in, v_conv_k, v_conv_w_out, v_kv_g, v_kv_w, v_forget_b, v_attn_w_qg, v_attn_w_o):
    n_seq, seq_len, _ = x.shape
    T = n_seq * seq_len
    xi, yi, ci = _place()
    dev = 4 * xi + 2 * yi + ci
    x0 = x.reshape(T, D)
    target = loss_target.reshape(T, D)

    def layer(w, l):
        return w[l:l + 1].astype(BF16)

    shard_groups = [
        [layer(ffn1_w_in, 0), layer(ffn1_w_out, 0)],
        [conv_w_in.astype(BF16), conv_w_out.astype(BF16), conv_k],
        [layer(ffn2_w_in, 0), layer(ffn2_w_out, 0)],
        [kv_w.astype(BF16), layer(ffn1_w_in, 1), layer(ffn1_w_out, 1), attn_w_qg.astype(BF16), attn_w_o.astype(BF16)],
        [layer(ffn2_w_in, 1), layer(ffn2_w_out, 1)]]
    def zone(s):
        return lax.dynamic_update_slice(lax.empty((N_DEV,) + s.shape, s.dtype), s[None], (dev,) + (0,) * s.ndim)

    gathers, token = exchange_start([[(s, zone(s)) for s in g] for g in shard_groups], "gather_like", "gather_start")

    def gathered(k, after):
        passed, _ = exchange_relay([gathers[k]], "gather_pass", [after], f"gather_pass{k}")
        return exchange_wait(passed, [after], f"gather_wait{k}")[0][1]

    fb = jnp.pad(forget_b, (0, LANES - N_HEADS))[None]

    def vec(g, l):
        return g[l:l + 1]

    def behind(g, tok):
        return g + tok[:1, :1]

    grads_small = {}

    def ffn_fwd(xin, g_pre, g_post, w_in, w_out, tag):
        xn, (gu,) = rms_proj(xin, g_pre, [(w_in, 0)], [BF16], f"{tag}_in")
        tm = _tile(T, TM, 16)
        fbw = gu.shape[2]
        specs = [pl.BlockSpec((1, tm, fbw), lambda i, k: (k, i, 0)), pl.BlockSpec((1, tm, fbw), lambda i, k: (k + 4, i, 0))]
        a, h, y = mix_out(_swiglu_pro, [gu, gu], specs, (w_out, 0), xin, g_post, 0.5, 4, f"{tag}_out")
        return y, (xin, xn, gu, a, h)

    def ffn_bwd(dy, saved, g_pre, g_post, w_in, w_out, tag):
        xin, xn, gu, a, h = saved
        dh, dg_post, dgate, dup = post_bwd(dy, h, g_post, 0.5, (w_out, 0), 4, f"{tag}_bwd_out", gu=gu)
        dw_out = wgrad([a], [dh[None]], 4, w_out.shape, f"{tag}_dw_out")
        dw_in = wgrad([xn[None]], [dgate, dup], 8, w_in.shape, f"{tag}_dw_in")
        started, tok = scatter_start([dw_in.reshape(8, -1, dw_in.shape[3]), dw_out.reshape(8, -1, D)], f"{tag}_scatter_start")
        dx, dg_pre = pre_bwd([(dgate, (w_in, 0), 0), (dup, (w_in, 0), 4)], xin, behind(g_pre, tok), dy, f"{tag}_bwd_in")
        return dx, dg_pre, dg_post, started

    def scatter_start(blocked, name):
        pairs = [(g, lax.empty((N_DEV - 1,) + g.shape[1:], g.dtype)) for g in blocked]
        started, tok = exchange_start([pairs], "scatter", name)
        return started[0], tok

    w1_in, w1_out = gathered(0, token)
    x1, s_f1a = ffn_fwd(x0, vec(ffn1_pre_g, 0), vec(ffn1_post_g, 0), w1_in, w1_out, "l0_ffn1")
    cw_in_g, cw_out, ck_g = gathered(1, x1)
    cw_in = _col_blocks(cw_in_g[:, 0], 3)
    ck = ck_g[:, 0].transpose(1, 0, 2).reshape(3, D)
    xn_c, (bch,) = rms_proj(x1, vec(mix_pre_g, 0), [(cw_in, 0)], [BF16], "conv_in")
    tmc = _tile(seq_len, TM, 16)
    hb = tmc // HALO

    def cpiece(p):
        return pl.BlockSpec((1, tmc, D), lambda i, k, p=p: (p, i, 0))

    def chalo(p):
        return pl.BlockSpec((1, HALO, D), lambda i, k, p=p: (p, jnp.maximum(i * hb - 1, 0), 0))

    conv_specs = [cpiece(0), cpiece(1), cpiece(2), chalo(1), chalo(2), pl.BlockSpec((3, D), lambda i, k: (0, 0))]
    z_c, m_c, x2 = mix_out(_make_conv_pro(seq_len // tmc), [bch, bch, bch, bch, bch, ck], conv_specs, (cw_out, 0),
                           x1, vec(mix_post_g, 0), 1.0, 1, "conv_out", tm=tmc)
    w2_in, w2_out = gathered(2, x2)
    x3, s_f2a = ffn_fwd(x2, vec(ffn2_pre_g, 0), vec(ffn2_post_g, 0), w2_in, w2_out, "l0_ffn2")

    kvw_g, w1_in_b, w1_out_b, qgw_g, ow = gathered(3, x3)
    qg_w = _col_blocks(qgw_g[:, 0], 2)
    kv_whole = kvw_g.transpose(1, 0, 2).reshape(D, 2 * D + N_HEADS)
    kv_wb = kv_whole[:, :2 * D].reshape(D, 2, D).transpose(1, 0, 2)[:, None]
    f_w = jnp.pad(kv_whole[:, 2 * D:], ((0, 0), (0, LANES - N_HEADS)))[None, None]
    xn_kv, (kv, fl) = rms_proj(x3, kv_g[None], [(kv_wb, 0), (f_w, 0)], [BF16, F32], "kv_in")
    c128 = forget_fwd(fl, fb, seq_len, "forget_fwd")
    c16 = c128[:, :N_HEADS]
    cb = jnp.repeat(c16, HEAD_DIM, axis=1)
    tq = _tile(seq_len, TQ, LANES)
    crow = c16.reshape(n_seq, seq_len // tq, tq, N_HEADS // 2, 2).transpose(0, 3, 1, 4, 2)

    x4, s_f1b = ffn_fwd(x3, vec(ffn1_pre_g, 1), vec(ffn1_post_g, 1), w1_in_b, w1_out_b, "l1_ffn1")
    xn_a, (qg,) = rms_proj(x4, vec(mix_pre_g, 1), [(qg_w, 0)], [BF16], "attn_in")
    o, lse = attn_fwd(qg, kv, cb, crow, n_seq, seq_len, "attn_fwd")
    tm = _tile(T, TM, 16)
    gate_specs = [pl.BlockSpec((1, tm, D), lambda i, k: (1, i, 0)), pl.BlockSpec((tm, D), lambda i, k: (i, 0))]
    z_a, m_a, x5 = mix_out(_attn_gate_pro, [qg, o], gate_specs, (ow, 0), x4, vec(mix_post_g, 1), 1.0, 1, "attn_out")
    w2_in_b, w2_out_b = gathered(4, x5)
    x6, s_f2b = ffn_fwd(x5, vec(ffn2_pre_g, 1), vec(ffn2_post_g, 1), w2_in_b, w2_out_b, "l1_ffn2")

    dy, loss_part = loss_head(x6, target, "loss_head")

    scatters = {}
    dx5, dg, dgp, scatters["ffn2", 1] = ffn_bwd(dy, s_f2b, vec(ffn2_pre_g, 1), vec(ffn2_post_g, 1), w2_in_b, w2_out_b, "l1_ffn2")
    grads_small["ffn2_pre", 1], grads_small["ffn2_post", 1] = dg, dgp
    dm_a, dgp, dz_a = post_bwd(dx5, m_a, vec(mix_post_g, 1), 1.0, (ow, 0), 1, "attn_bwd_out")
    grads_small["mix_post", 1] = dgp
    dq, dgate, dk, dv, dcrow = attn_bwd(dz_a[0], qg, kv, o, lse, cb, crow, n_seq, seq_len, "attn_bwd")
    dx4, dg = pre_bwd([(dq, (qg_w, 0), 0), (dgate, (qg_w, 0), 1)], x4, vec(mix_pre_g, 1), dx5, "attn_bwd_in")
    grads_small["mix_pre", 1] = dg
    d_ow = wgrad([z_a], [dm_a[None]], 1, ow.shape, "attn_dw_o")
    d_qgw = wgrad([xn_a[None]], [dq, dgate], 2, (2, 1, D, D), "attn_dw_qg")
    scatters["attn"], tok = scatter_start([_col_shards(d_qgw[:, 0]), d_ow.reshape(8, -1, D)], "attn_scatter_start")
    dx3, dg, dgp, scatters["ffn1", 1] = ffn_bwd(dx4, s_f1b, vec(ffn1_pre_g, 1), behind(vec(ffn1_post_g, 1), tok),
                                                w1_in_b, w1_out_b, "l1_ffn1")
    grads_small["ffn1_pre", 1], grads_small["ffn1_post", 1] = dg, dgp

    dc16 = dcrow.transpose(0, 2, 4, 1, 3).reshape(T, N_HEADS)
    dfl, dfb = forget_bwd(jnp.pad(dc16, ((0, 0), (0, LANES - N_HEADS))), fl, fb, seq_len, "forget_bwd")
    dx3, dg_kv = pre_bwd([(dk, (kv_wb, 0), 0), (dv, (kv_wb, 0), 1), (dfl, (f_w, 0), 0)], x3, kv_g[None], dx3, "kv_bwd_in")
    d_kvw = wgrad([xn_kv[None]], [dk, dv], 2, (2, 1, D, D), "kv_dw")
    d_fw = wgrad([xn_kv[None]], [dfl.astype(BF16)], 1, (1, 1, D, LANES), "forget_dw")
    d_kv_whole = jnp.concatenate([d_kvw[0, 0], d_kvw[1, 0], d_fw[0, 0, :, :N_HEADS]], axis=1)
    wshard = D * 2 + N_HEADS
    scatters["kv"], tok = scatter_start([d_kv_whole.reshape(D, N_DEV, wshard // N_DEV).transpose(1, 0, 2)], "kv_scatter_start")

    dx2, dg, dgp, scatters["ffn2", 0] = ffn_bwd(dx3, s_f2a, vec(ffn2_pre_g, 0), behind(vec(ffn2_post_g, 0), tok),
                                                w2_in, w2_out, "l0_ffn2")
    grads_small["ffn2_pre", 0], grads_small["ffn2_post", 0] = dg, dgp
    dm_c, dgp, dz_c = post_bwd(dx2, m_c, vec(mix_post_g, 0), 1.0, (cw_out, 0), 1, "conv_bwd_out")
    grads_small["mix_post", 0] = dgp
    dbch, d_ck = conv_bwd_mix(dz_c, bch, ck, seq_len, "conv_bwd_mix")
    dx1, dg = pre_bwd([(dbch, (cw_in, 0), 0)], x1, vec(mix_pre_g, 0), dx2, "conv_bwd_in")
    grads_small["mix_pre", 0] = dg
    d_cw_out = wgrad([z_c], [dm_c[None]], 1, cw_out.shape, "conv_dw_out")
    d_cw_in = wgrad([xn_c[None]], [dbch], 3, (3, 1, D, D), "conv_dw_in")
    scatters["conv"], tok = scatter_start([_col_shards(d_cw_in[:, 0]), d_cw_out.reshape(8, -1, D)], "conv_scatter_start")
    dx0, dg, dgp, scatters["ffn1", 0] = ffn_bwd(dx1, s_f1a, vec(ffn1_pre_g, 0), behind(vec(ffn1_post_g, 0), tok),
                                                w1_in, w1_out, "l0_ffn1")
    grads_small["ffn1_pre", 0], grads_small["ffn1_post", 0] = dg, dgp

    parts_of = {}

    def scatter_end(keys, afters, name):
        for k, (sent, recv) in zip(keys, exchange_wait([scatters[k] for k in keys], afters, name)):
            parts_of[k] = [(lax.dynamic_index_in_dim(g, dev, 0, keepdims=False), r) for g, r in zip(sent, recv)]

    scatter_end([("ffn2", 1), "attn", ("ffn1", 1), "kv", ("ffn2", 0), "conv"], [dx0], "scatter_wait")
    sharded = {"ffn2_w_in": (ffn2_w_in, m_ffn2_w_in, v_ffn2_w_in), "ffn2_w_out": (ffn2_w_out, m_ffn2_w_out, v_ffn2_w_out),
               "conv_w_in": (conv_w_in, m_conv_w_in, v_conv_w_in), "conv_w_out": (conv_w_out, m_conv_w_out, v_conv_w_out),
               "kv_w": (kv_w, m_kv_w, v_kv_w), "attn_w_qg": (attn_w_qg, m_attn_w_qg, v_attn_w_qg),
               "attn_w_o": (attn_w_o, m_attn_w_o, v_attn_w_o),
               "ffn1_w_in": (ffn1_w_in, m_ffn1_w_in, v_ffn1_w_in), "ffn1_w_out": (ffn1_w_out, m_ffn1_w_out, v_ffn1_w_out)}
    out = {}
    for nm, (w, mm, vv) in sharded.items():
        if nm == "ffn1_w_in":
            scatter_end([("ffn1", 0)], [res[0] for res in out.values()], "scatter_wait_last")
        contribs = {
            "ffn1_w_in": lambda: [parts_of["ffn1", 0][0], parts_of["ffn1", 1][0]],
            "ffn1_w_out": lambda: [parts_of["ffn1", 0][1], parts_of["ffn1", 1][1]],
            "ffn2_w_in": lambda: [parts_of["ffn2", 0][0], parts_of["ffn2", 1][0]],
            "ffn2_w_out": lambda: [parts_of["ffn2", 0][1], parts_of["ffn2", 1][1]],
            "conv_w_in": lambda: [parts_of["conv"][0]], "conv_w_out": lambda: [parts_of["conv"][1]],
            "kv_w": lambda: [parts_of["kv"][0]], "attn_w_qg": lambda: [parts_of["attn"][0]],
            "attn_w_o": lambda: [parts_of["attn"][1]]}[nm]()
        shape3 = (len(contribs),) + contribs[0][0].shape
        res = adam_sharded(w.reshape(shape3), mm.reshape(shape3), vv.reshape(shape3), contribs, f"adam_{nm}")
        out[nm] = [r.reshape(w.shape) for r in res]

    small_names = ["ffn1_pre", "ffn1_post", "mix_pre", "mix_post", "ffn2_pre", "ffn2_post"]
    parts = [(grads_small[n, l], 2 * k + l) for k, n in enumerate(small_names) for l in range(2)]
    parts += [(dg_kv, 12), (dfb, 13), (d_ck, 14), (loss_part, 17)]
    total = all_reduce_small(parts, 24, "all_reduce_small")
    loss = total[17, 0]
    d_ck_mine = lax.dynamic_slice(total, (14, dev * LANES), (3, LANES))
    gains = [(ffn1_pre_g, m_ffn1_pre_g, v_ffn1_pre_g), (ffn1_post_g, m_ffn1_post_g, v_ffn1_post_g),
             (mix_pre_g, m_mix_pre_g, v_mix_pre_g), (mix_post_g, m_mix_post_g, v_mix_post_g),
             (ffn2_pre_g, m_ffn2_pre_g, v_ffn2_pre_g), (ffn2_post_g, m_ffn2_post_g, v_ffn2_post_g)]
    small_params = [(w, m, v, (2 * k, 2, D)) for k, (w, m, v) in enumerate(gains)]
    small_params += [(kv_g[None], m_kv_g[None], v_kv_g[None], (12, 1, D)),
                     (forget_b[None], m_forget_b[None], v_forget_b[None], (13, 1, N_HEADS)),
                     (conv_k[0], m_conv_k[0], v_conv_k[0], 0)]
    small_res = adam_small(small_params, total, [d_ck_mine], "adam_small")
    small_keys = [n + "_g" for n in small_names] + ["kv_g", "forget_b", "conv_k"]
    shapes = {"kv_g": kv_g.shape, "forget_b": forget_b.shape, "conv_k": conv_k.shape}
    small = [{key: res[kind].reshape(shapes.get(key, res[kind].shape)) for key, res in zip(small_keys, small_res)}
             for kind in range(4)]
    order = ["ffn1_pre_g", "ffn1_post_g", "ffn1_w_in", "ffn1_w_out", "mix_pre_g", "mix_post_g", "ffn2_pre_g", "ffn2_post_g",
             "ffn2_w_in", "ffn2_w_out", "conv_w_in", "conv_k", "conv_w_out", "kv_g", "kv_w", "forget_b", "attn_w_qg",
             "attn_w_o"]
    results = [loss, dx0.reshape(x.shape)]
    for kind in range(4):
        for nm in order:
            results.append(out[nm][kind] if nm in out else small[kind][nm])
    return tuple(results)
```

```python
import functools
import math

import jax
import jax.numpy as jnp
from jax import lax
from jax.experimental import pallas as pl
from jax.experimental.pallas import tpu as pltpu

F32, BF16 = jnp.float32, jnp.bfloat16
D = 1024
N_HEADS = 16
HEAD_DIM = 64
N_DEV = 8
RMS_EPS = 1e-6
ATT_SCALE = 1.0 / math.sqrt(HEAD_DIM)
LANES = 128
HALO = 8
TM = 512
TQ = 256
VMEM_LIMIT = 48 * 1024 * 1024
MESH = pl.DeviceIdType.MESH

ADAM_LR, ADAM_B1, ADAM_B2, ADAM_EPS, ADAM_WD, ADAM_STEP = 0.001, 0.9, 0.999, 1e-08, 0.01, 10

NT = (((1,), (1,)), ((), ()))
TN = (((0,), (0,)), ((), ()))


def _params(n_axes):
    return pltpu.CompilerParams(dimension_semantics=("arbitrary",) * n_axes, vmem_limit_bytes=VMEM_LIMIT)


def _tile(n, cap, mult):
    best = None
    for t in range(mult, min(n, cap) + 1, mult):
        if n % t == 0:
            best = t
    assert best is not None, (n, cap, mult)
    return best


def _rms_rstd(x):
    return lax.rsqrt(jnp.mean(x * x, axis=-1, keepdims=True) + RMS_EPS)


def _rms_fwd(x, g):
    return x * _rms_rstd(x) * g


def _rms_bwd(x, g, dy):
    xh = x * _rms_rstd(x)
    dyg = dy * g
    dx = _rms_rstd(x) * (dyg - xh * jnp.mean(dyg * xh, axis=-1, keepdims=True))
    return dx, jnp.sum(dy * xh, axis=0, keepdims=True)


def _accumulate(ref, first, value):
    @pl.when(first)
    def _():
        ref[...] = value

    @pl.when(jnp.logical_not(first))
    def _():
        ref[...] += value


def rms_proj(x, g, ws, out_dtypes, name):
    T = x.shape[0]
    tm = _tile(T, TM, 16)
    na = len(ws)
    nbs = [w.shape[0] for w, _ in ws]
    offs = [sum(nbs[:a]) for a in range(na)]
    nb_total = sum(nbs)

    def body(x_ref, g_ref, *refs):
        w_refs, xn_ref, o_refs, xn_s = refs[:na], refs[na], refs[na + 1:2 * na + 1], refs[2 * na + 1]
        n = pl.program_id(1)

        @pl.when(n == 0)
        def _():
            xn = _rms_fwd(x_ref[...], g_ref[...]).astype(BF16)
            xn_s[...] = xn
            xn_ref[...] = xn

        for a in range(na):
            @pl.when((n >= offs[a]) & (n < offs[a] + nbs[a]))
            def _(a=a):
                y = jnp.dot(xn_s[...], w_refs[a][0, 0], preferred_element_type=F32)
                o_refs[a][0] = y.astype(o_refs[a].dtype)

    def blk(off, nb):
        return lambda n: jnp.clip(n - off, 0, nb - 1)

    in_specs = [pl.BlockSpec((tm, D), lambda i, n: (i, 0)), pl.BlockSpec((1, D), lambda i, n: (0, 0))]
    out_specs = [pl.BlockSpec((tm, D), lambda i, n: (i, 0))]
    out_shape = [jax.ShapeDtypeStruct((T, D), BF16)]
    for (w, l), off, nb, dt in zip(ws, offs, nbs, out_dtypes):
        wb = w.shape[3]
        in_specs.append(pl.BlockSpec((1, 1, D, wb), lambda i, n, f=blk(off, nb), l=l: (f(n), l, 0, 0)))
        out_specs.append(pl.BlockSpec((1, tm, wb), lambda i, n, f=blk(off, nb): (f(n), i, 0)))
        out_shape.append(jax.ShapeDtypeStruct((nb, T, wb), dt))
    res = pl.pallas_call(
        body, name=name, grid=(T // tm, nb_total), in_specs=in_specs, out_specs=out_specs, out_shape=out_shape,
        scratch_shapes=[pltpu.VMEM((tm, D), BF16)], compiler_params=_params(2),
    )(x, g, *[w for w, _ in ws])
    return res[0], res[1:]


def mix_out(pro, pro_inputs, pro_specs, w, res, g_post, alpha, nk, name, tm=None):
    w4, l = w
    T = res.shape[0]
    tm = _tile(T, TM, 16) if tm is None else tm
    dpb = N_DEV // nk
    rows = w4.shape[2]
    kb = dpb * rows
    npi = len(pro_inputs)

    def body(*refs):
        pro_refs = refs[:npi]
        w_ref, res_ref, g_ref, z_ref, m_ref, y_ref, acc = refs[npi:]
        i, k = pl.program_id(0), pl.program_id(1)
        z = pro(i, k, *pro_refs).astype(BF16)
        z_ref[0] = z
        part = jnp.dot(z, w_ref[:, 0].reshape(kb, D), preferred_element_type=F32)
        _accumulate(acc, k == 0, part)

        @pl.when(k == nk - 1)
        def _():
            m = acc[...]
            m_ref[...] = m
            y_ref[...] = res_ref[...] + alpha * _rms_fwd(m, g_ref[...])

    row = pl.BlockSpec((tm, D), lambda i, k: (i, 0))
    in_specs = list(pro_specs) + [
        pl.BlockSpec((dpb, 1, rows, D), lambda i, k: (k, l, 0, 0)), row, pl.BlockSpec((1, D), lambda i, k: (0, 0))]
    z, m, y = pl.pallas_call(
        body, name=name, grid=(T // tm, nk), in_specs=in_specs,
        out_specs=[pl.BlockSpec((1, tm, kb), lambda i, k: (k, i, 0)), row, row],
        out_shape=[jax.ShapeDtypeStruct((nk, T, kb), BF16), jax.ShapeDtypeStruct((T, D), F32),
                   jax.ShapeDtypeStruct((T, D), F32)],
        scratch_shapes=[pltpu.VMEM((tm, D), F32)], compiler_params=_params(2),
    )(*pro_inputs, w4, res, g_post)
    return z, m, y


def post_bwd(dy, m, g_post, alpha, w, nk, name, gu=None):
    w4, l = w
    T = dy.shape[0]
    tm = _tile(T, TM, 16)
    dpb = N_DEV // nk
    rows = w4.shape[2]
    kb = dpb * rows
    swiglu = gu is not None

    def body(dy_ref, m_ref, g_ref, w_ref, *refs):
        if swiglu:
            gate_ref, up_ref, dm_ref, dg_ref, dgate_ref, dup_ref, dm_s = refs
        else:
            dm_ref, dg_ref, dz_ref, dm_s = refs
        i, k = pl.program_id(0), pl.program_id(1)

        @pl.when(k == 0)
        def _():
            dm, dg = _rms_bwd(m_ref[...], g_ref[...], alpha * dy_ref[...])
            dm_s[...] = dm.astype(BF16)
            dm_ref[...] = dm.astype(BF16)
            _accumulate(dg_ref, i == 0, dg)

        dz = lax.dot_general(dm_s[...], w_ref[:, 0].reshape(kb, D), NT, preferred_element_type=F32)
        if swiglu:
            gate, up = gate_ref[0].astype(F32), up_ref[0].astype(F32)
            sig = jax.nn.sigmoid(gate)
            dgate_ref[0] = (dz * up * sig * (1.0 + gate * (1.0 - sig))).astype(BF16)
            dup_ref[0] = (dz * gate * sig).astype(BF16)
        else:
            dz_ref[0] = dz.astype(BF16)

    row = pl.BlockSpec((tm, D), lambda i, k: (i, 0))
    vec = pl.BlockSpec((1, D), lambda i, k: (0, 0))
    blkk = pl.BlockSpec((1, tm, kb), lambda i, k: (k, i, 0))
    in_specs = [row, row, vec, pl.BlockSpec((dpb, 1, rows, D), lambda i, k: (k, l, 0, 0))]
    inputs = [dy, m, g_post, w4]
    out_specs = [row, vec, blkk]
    out_shape = [jax.ShapeDtypeStruct((T, D), BF16), jax.ShapeDtypeStruct((1, D), F32),
                 jax.ShapeDtypeStruct((nk, T, kb), BF16)]
    if swiglu:
        in_specs += [blkk, pl.BlockSpec((1, tm, kb), lambda i, k: (k + nk, i, 0))]
        inputs += [gu, gu]
        out_specs.append(blkk)
        out_shape.append(jax.ShapeDtypeStruct((nk, T, kb), BF16))
    return pl.pallas_call(
        body, name=name, grid=(T // tm, nk), in_specs=in_specs, out_specs=out_specs, out_shape=out_shape,
        scratch_shapes=[pltpu.VMEM((tm, D), BF16)], compiler_params=_params(2),
    )(*inputs)


def pre_bwd(pieces, x, g_pre, dres, name):
    T = x.shape[0]
    tm = _tile(T, TM, 16)
    na = len(pieces)
    nbs = [p[0].shape[0] for p in pieces]
    offs = [sum(nbs[:a]) for a in range(na)]
    nb_total = sum(nbs)

    def body(*refs):
        dz_refs, w_refs = refs[:na], refs[na:2 * na]
        x_ref, g_ref, dres_ref, dx_ref, dg_ref, acc = refs[2 * na:]
        i, n = pl.program_id(0), pl.program_id(1)
        for a in range(na):
            @pl.when((n >= offs[a]) & (n < offs[a] + nbs[a]))
            def _(a=a):
                part = lax.dot_general(dz_refs[a][0].astype(BF16), w_refs[a][0, 0], NT, preferred_element_type=F32)
                _accumulate(acc, n == 0, part)

        @pl.when(n == nb_total - 1)
        def _():
            dx, dg = _rms_bwd(x_ref[...], g_ref[...], acc[...])
            dx_ref[...] = dres_ref[...] + dx
            _accumulate(dg_ref, i == 0, dg)

    def blk(off, nb):
        return lambda n: jnp.clip(n - off, 0, nb - 1)

    row = pl.BlockSpec((tm, D), lambda i, n: (i, 0))
    vec = pl.BlockSpec((1, D), lambda i, n: (0, 0))
    dz_specs, w_specs = [], []
    for (dz, (w4, l), w_off), off, nb in zip(pieces, offs, nbs):
        wb = dz.shape[2]
        dz_specs.append(pl.BlockSpec((1, tm, wb), lambda i, n, f=blk(off, nb): (f(n), i, 0)))
        w_specs.append(pl.BlockSpec((1, 1, D, wb), lambda i, n, f=blk(off, nb), l=l, o=w_off: (o + f(n), l, 0, 0)))
    return pl.pallas_call(
        body, name=name, grid=(T // tm, nb_total), in_specs=dz_specs + w_specs + [row, vec, row],
        out_specs=[row, vec], out_shape=[jax.ShapeDtypeStruct((T, D), F32), jax.ShapeDtypeStruct((1, D), F32)],
        scratch_shapes=[pltpu.VMEM((tm, D), F32)], compiler_params=_params(2),
    )(*[p[0] for p in pieces], *[p[1][0] for p in pieces], x, g_pre, dres)


def wgrad(a_list, b_list, nout, out4_shape, name):
    T = a_list[0].shape[1]
    tm = _tile(T, TM, 16)
    nt = T // tm
    dpb = out4_shape[0] // nout
    _, _, R, C = out4_shape
    na, nb = len(a_list), len(b_list)

    def spans(arrs):
        ns = [a.shape[0] for a in arrs]
        return ns, [sum(ns[:k]) for k in range(len(ns))], sum(ns)

    a_ns, a_offs, a_tot = spans(a_list)
    b_ns, b_offs, b_tot = spans(b_list)
    assert a_tot in (1, nout) and b_tot in (1, nout)

    def body(*refs):
        a_refs, b_refs = refs[:na], refs[na:na + nb]
        out_ref, acc = refs[-2:]
        p, t = pl.program_id(0), pl.program_id(1)
        for ia in range(na):
            for ib in range(nb):
                conds = []
                if a_tot > 1:
                    conds += [p >= a_offs[ia], p < a_offs[ia] + a_ns[ia]]
                if b_tot > 1:
                    conds += [p >= b_offs[ib], p < b_offs[ib] + b_ns[ib]]

                def work(ia=ia, ib=ib):
                    part = lax.dot_general(a_refs[ia][0], b_refs[ib][0], TN, preferred_element_type=F32)
                    _accumulate(acc, t == 0, part)

                if conds:
                    pl.when(functools.reduce(jnp.logical_and, conds))(work)
                else:
                    work()

        @pl.when(t == nt - 1)
        def _():
            out_ref[...] = acc[...].astype(BF16).reshape(dpb, 1, R, C)

    def blk(off, n, tot):
        if tot == 1:
            return lambda p: 0
        return lambda p: jnp.clip(p - off, 0, n - 1)

    in_specs = []
    for arrs, ns, offs, tot in ((a_list, a_ns, a_offs, a_tot), (b_list, b_ns, b_offs, b_tot)):
        for arr, n, off in zip(arrs, ns, offs):
            in_specs.append(pl.BlockSpec((1, tm, arr.shape[2]), lambda p, t, f=blk(off, n, tot): (f(p), t, 0)))
    return pl.pallas_call(
        body, name=name, grid=(nout, nt), in_specs=in_specs,
        out_specs=pl.BlockSpec((dpb, 1, R, C), lambda p, t: (p, 0, 0, 0)),
        out_shape=jax.ShapeDtypeStruct(out4_shape, BF16),
        scratch_shapes=[pltpu.VMEM((dpb * R, C), F32)], compiler_params=_params(2),
    )(*a_list, *b_list)


def _swiglu_pro(i, k, gate_ref, up_ref):
    gate, up = gate_ref[0].astype(F32), up_ref[0].astype(F32)
    return gate * jax.nn.sigmoid(gate) * up


def _attn_gate_pro(i, k, gate_ref, o_ref):
    return jax.nn.sigmoid(gate_ref[0].astype(F32)) * o_ref[...].astype(F32)


def _shift_rows(u, halo, d):
    rolled = pltpu.roll(u, d, 0)
    row = lax.broadcasted_iota(jnp.int32, u.shape, 0)
    for r in range(d):
        rolled = jnp.where(row == r, halo[HALO - d + r:HALO - d + r + 1, :], rolled)
    return rolled


def _advance_rows(u, halo, d):
    n = u.shape[0]
    rolled = pltpu.roll(u, n - d, 0)
    row = lax.broadcasted_iota(jnp.int32, u.shape, 0)
    for r in range(d):
        rolled = jnp.where(row == n - d + r, halo[r:r + 1, :], rolled)
    return rolled


def _make_conv_pro(tiles_per_seq):
    def pro(i, k, b_ref, c_ref, h_ref, ch_ref, hh_ref, ck_ref):
        u = c_ref[0].astype(F32) * h_ref[0].astype(F32)
        first = (i % tiles_per_seq) == 0
        halo = jnp.where(first, 0.0, ch_ref[0].astype(F32) * hh_ref[0].astype(F32))
        ck = ck_ref[...]
        conv = ck[2:3, :] * u + ck[1:2, :] * _shift_rows(u, halo, 1) + ck[0:1, :] * _shift_rows(u, halo, 2)
        return b_ref[0].astype(F32) * conv
    return pro


def conv_bwd_mix(dz, bch, conv_k, seq_len, name):
    T = dz.shape[1]
    tm = _tile(seq_len, TM, 16)
    tps = seq_len // tm
    nt = T // tm
    hb = tm // HALO

    def body(dz_ref, b_ref, c_ref, h_ref, cp_ref, hp_ref, dzn_ref, bn_ref, ck_ref, dbch_ref, dk_ref):
        i = pl.program_id(0)
        first = (i % tps) == 0
        last = (i % tps) == tps - 1
        b, c, h = b_ref[0].astype(F32), c_ref[0].astype(F32), h_ref[0].astype(F32)
        dzt = dz_ref[0].astype(F32)
        u = c * h
        prev = jnp.where(first, 0.0, cp_ref[0].astype(F32) * hp_ref[0].astype(F32))
        u1, u2 = _shift_rows(u, prev, 1), _shift_rows(u, prev, 2)
        ck = ck_ref[...]
        conv = ck[2:3, :] * u + ck[1:2, :] * u1 + ck[0:1, :] * u2
        dconv = dzt * b
        nxt = jnp.where(last, 0.0, dzn_ref[0].astype(F32) * bn_ref[0].astype(F32))
        du = ck[2:3, :] * dconv + ck[1:2, :] * _advance_rows(dconv, nxt, 1) + ck[0:1, :] * _advance_rows(dconv, nxt, 2)
        dbch_ref[0] = (dzt * conv).astype(BF16)
        dbch_ref[1] = (du * h).astype(BF16)
        dbch_ref[2] = (du * c).astype(BF16)
        tap = lax.broadcasted_iota(jnp.int32, (3, D), 0)
        dk = jnp.where(tap == 0, jnp.sum(dconv * u2, axis=0, keepdims=True),
                       jnp.where(tap == 1, jnp.sum(dconv * u1, axis=0, keepdims=True),
                                 jnp.sum(dconv * u, axis=0, keepdims=True)))
        _accumulate(dk_ref, i == 0, dk)

    def piece(p):
        return pl.BlockSpec((1, tm, D), lambda i, p=p: (p, i, 0))

    def prev(p):
        return pl.BlockSpec((1, HALO, D), lambda i, p=p: (p, jnp.maximum(i * hb - 1, 0), 0))

    def nxt(p):
        return pl.BlockSpec((1, HALO, D), lambda i, p=p: (p, jnp.minimum((i + 1) * hb, nt * hb - 1), 0))

    return pl.pallas_call(
        body, name=name, grid=(nt,),
        in_specs=[piece(0), piece(0), piece(1), piece(2), prev(1), prev(2), nxt(0), nxt(0),
                  pl.BlockSpec((3, D), lambda i: (0, 0))],
        out_specs=[pl.BlockSpec((3, tm, D), lambda i: (0, i, 0)), pl.BlockSpec((3, D), lambda i: (0, 0))],
        out_shape=[jax.ShapeDtypeStruct((3, T, D), BF16), jax.ShapeDtypeStruct((3, D), F32)],
        compiler_params=_params(1),
    )(dz, bch, bch, bch, bch, bch, dz, bch, conv_k)


def _log_sigmoid(x):
    return jnp.minimum(x, 0.0) - jnp.log(1.0 + jnp.exp(-jnp.abs(x)))


def forget_fwd(fl, fb, seq_len, name):
    T = fl.shape[1]

    def body(fl_ref, fb_ref, c_ref):
        c = _log_sigmoid(fl_ref[0] + fb_ref[...])
        row = lax.broadcasted_iota(jnp.int32, c.shape, 0)
        k = 1
        while k < seq_len:
            c = c + jnp.where(row >= k, pltpu.roll(c, k, 0), 0.0)
            k *= 2
        c_ref[...] = c

    return pl.pallas_call(
        body, name=name, grid=(T // seq_len,),
        in_specs=[pl.BlockSpec((1, seq_len, LANES), lambda b: (0, b, 0)), pl.BlockSpec((1, LANES), lambda b: (0, 0))],
        out_specs=pl.BlockSpec((seq_len, LANES), lambda b: (b, 0)),
        out_shape=jax.ShapeDtypeStruct((T, LANES), F32), compiler_params=_params(1),
    )(fl, fb)


def forget_bwd(dc, fl, fb, seq_len, name):
    T = dc.shape[0]

    def body(dc_ref, fl_ref, fb_ref, dfl_ref, dfb_ref):
        b = pl.program_id(0)
        r = dc_ref[...]
        row = lax.broadcasted_iota(jnp.int32, r.shape, 0)
        k = 1
        while k < seq_len:
            r = r + jnp.where(row < seq_len - k, pltpu.roll(r, seq_len - k, 0), 0.0)
            k *= 2
        dfl = r * jax.nn.sigmoid(-(fl_ref[0] + fb_ref[...]))
        dfl_ref[0] = dfl
        _accumulate(dfb_ref, b == 0, jnp.sum(dfl, axis=0, keepdims=True))

    return pl.pallas_call(
        body, name=name, grid=(T // seq_len,),
        in_specs=[pl.BlockSpec((seq_len, LANES), lambda b: (b, 0)), pl.BlockSpec((1, seq_len, LANES), lambda b: (0, b, 0)),
                  pl.BlockSpec((1, LANES), lambda b: (0, 0))],
        out_specs=[pl.BlockSpec((1, seq_len, LANES), lambda b: (0, b, 0)), pl.BlockSpec((1, LANES), lambda b: (0, 0))],
        out_shape=[jax.ShapeDtypeStruct((1, T, LANES), F32), jax.ShapeDtypeStruct((1, LANES), F32)],
        compiler_params=_params(1),
    )(dc, fl, fb)


HEADS = (slice(0, HEAD_DIM), slice(HEAD_DIM, 2 * HEAD_DIM))


def attn_fwd(qg, kv, crow, n_seq, seq_len, name):
    T = n_seq * seq_len
    tq = _tile(seq_len, TQ, LANES)
    nq = seq_len // tq

    def body(q_ref, k_ref, v_ref, cr_ref, o_ref, lse_ref):
        i = pl.program_id(2)
        causal = lax.broadcasted_iota(jnp.int32, (tq, tq), 1) <= lax.broadcasted_iota(jnp.int32, (tq, tq), 0)
        q8 = [q_ref[0, :, sl] * ATT_SCALE for sl in HEADS]

        def block(j, carry, diagonal):
            rows = pl.ds(pl.multiple_of(j * tq, tq), tq)
            out = []
            for hh, sl in enumerate(HEADS):
                m, l, acc = carry[3 * hh:3 * hh + 3]
                s = lax.dot_general(q8[hh], k_ref[0, rows, sl], NT, preferred_element_type=F32)
                s = s - cr_ref[0, 0, j, hh:hh + 1, :]
                if diagonal:
                    s = jnp.where(causal, s, -1e30)
                m_new = jnp.maximum(m, jnp.max(s, axis=-1, keepdims=True))
                a = jnp.exp(m - m_new)
                p = jnp.exp(s - m_new)
                l = a * l + jnp.sum(p, axis=-1, keepdims=True)
                acc = a * acc + jnp.dot(p.astype(BF16), v_ref[0, rows, sl], preferred_element_type=F32)
                out += [m_new, l, acc]
            return tuple(out)

        init = (jnp.full((tq, 1), -1e30, F32), jnp.zeros((tq, 1), F32), jnp.zeros((tq, HEAD_DIM), F32)) * 2
        carry = lax.fori_loop(0, i, lambda j, c: block(j, c, False), init)
        carry = block(i, carry, True)
        o_ref[...] = jnp.concatenate([carry[2] / carry[1], carry[5] / carry[4]], axis=-1)
        lse_ref[...] = jnp.concatenate([jnp.broadcast_to(carry[3 * hh] + jnp.log(carry[3 * hh + 1]), (tq, HEAD_DIM))
                                        for hh in range(2)], axis=-1)

    qspec = pl.BlockSpec((tq, LANES), lambda b, hp, i: (b * nq + i, hp))
    return pl.pallas_call(
        body, name=name, grid=(n_seq, N_HEADS // 2, nq),
        in_specs=[pl.BlockSpec((1, tq, LANES), lambda b, hp, i: (0, b * nq + i, hp)),
                  pl.BlockSpec((1, seq_len, LANES), lambda b, hp, i: (0, b, hp)),
                  pl.BlockSpec((1, seq_len, LANES), lambda b, hp, i: (1, b, hp)),
                  pl.BlockSpec((1, 1, nq, 2, tq), lambda b, hp, i: (b, hp, 0, 0, 0))],
        out_specs=[qspec, qspec],
        out_shape=[jax.ShapeDtypeStruct((T, D), F32), jax.ShapeDtypeStruct((T, D), F32)],
        compiler_params=_params(3),
    )(qg, kv, kv, crow)


def attn_bwd(dz, qg, kv, o, lse, crow, n_seq, seq_len, name):
    T = n_seq * seq_len
    tq = _tile(seq_len, TQ, LANES)
    nq = seq_len // tq

    def body(dz_ref, q_ref, gate_ref, o_ref, lse_ref, cr_ref, k_ref, v_ref,
             dq_ref, dgate_ref, dk_ref, dv_ref, dcr_ref, p_s, dp_s, dk_s, dv_s):
        i = pl.program_id(2)

        @pl.when(i == 0)
        def _():
            dk_s[...] = jnp.zeros_like(dk_s)
            dv_s[...] = jnp.zeros_like(dv_s)
            dcr_ref[...] = jnp.zeros_like(dcr_ref)

        dzf = dz_ref[...].astype(F32)
        sig = jax.nn.sigmoid(gate_ref[0].astype(F32))
        dob = (dzf * sig).astype(BF16)
        dgate_ref[0] = (dzf * o_ref[...] * sig * (1.0 - sig)).astype(BF16)
        causal = lax.broadcasted_iota(jnp.int32, (tq, tq), 1) <= lax.broadcasted_iota(jnp.int32, (tq, tq), 0)
        q8 = [q_ref[0, :, sl] * ATT_SCALE for sl in HEADS]
        do = [dob[:, sl] for sl in HEADS]
        lse_i = [lse_ref[:, sl.start:sl.start + 1] for sl in HEADS]

        def probs(j, dsums, diagonal):
            rows = pl.ds(pl.multiple_of(j * tq, tq), tq)
            out = []
            for hh, sl in enumerate(HEADS):
                s = lax.dot_general(q8[hh], k_ref[0, rows, sl], NT, preferred_element_type=F32)
                p = jnp.exp(s - cr_ref[0, 0, j, hh:hh + 1, :] - lse_i[hh])
                if diagonal:
                    p = jnp.where(causal, p, 0.0)
                dp = lax.dot_general(do[hh], v_ref[0, rows, sl], NT, preferred_element_type=F32)
                p_s[hh, j] = p
                dp_s[hh, j] = dp
                out.append(dsums[hh] + jnp.sum(p * dp, axis=-1, keepdims=True))
            return tuple(out)

        dsums = lax.fori_loop(0, i, lambda j, c: probs(j, c, False), (jnp.zeros((tq, 1), F32),) * 2)
        dsums = probs(i, dsums, True)

        def grads(j, dqs):
            rows = pl.ds(pl.multiple_of(j * tq, tq), tq)
            out = []
            for hh, sl in enumerate(HEADS):
                p = p_s[hh, j]
                ds = p * (dp_s[hh, j] - dsums[hh])
                dcr_ref[0, 0, j, hh:hh + 1, :] -= jnp.sum(ds, axis=0, keepdims=True)
                dsb = ds.astype(BF16)
                dk_s[rows, sl] += lax.dot_general(dsb, q8[hh], TN, preferred_element_type=F32)
                dv_s[rows, sl] += lax.dot_general(p.astype(BF16), do[hh], TN, preferred_element_type=F32)
                out.append(dqs[hh] + jnp.dot(dsb, k_ref[0, rows, sl], preferred_element_type=F32))
            return tuple(out)

        dqs = lax.fori_loop(0, i + 1, grads, (jnp.zeros((tq, HEAD_DIM), F32),) * 2)
        dq_ref[0] = (jnp.concatenate(dqs, axis=-1) * ATT_SCALE).astype(BF16)

        @pl.when(i == nq - 1)
        def _():
            dk_ref[0] = dk_s[...].astype(BF16)
            dv_ref[0] = dv_s[...].astype(BF16)

    qry2 = pl.BlockSpec((tq, LANES), lambda b, hp, i: (b * nq + i, hp))

    def qry3(p):
        return pl.BlockSpec((1, tq, LANES), lambda b, hp, i, p=p: (p, b * nq + i, hp))

    def seq3(p):
        return pl.BlockSpec((1, seq_len, LANES), lambda b, hp, i, p=p: (p, b, hp))

    crs = pl.BlockSpec((1, 1, nq, 2, tq), lambda b, hp, i: (b, hp, 0, 0, 0))
    act = jax.ShapeDtypeStruct((1, T, D), BF16)
    return pl.pallas_call(
        body, name=name, grid=(n_seq, N_HEADS // 2, nq),
        in_specs=[qry2, qry3(0), qry3(1), qry2, qry2, crs, seq3(0), seq3(1)],
        out_specs=[qry3(0), qry3(0), seq3(0), seq3(0), crs],
        out_shape=[act, act, act, act, jax.ShapeDtypeStruct((n_seq, N_HEADS // 2, nq, 2, tq), F32)],
        scratch_shapes=[pltpu.VMEM((2, nq, tq, tq), F32), pltpu.VMEM((2, nq, tq, tq), F32),
                        pltpu.VMEM((seq_len, LANES), F32), pltpu.VMEM((seq_len, LANES), F32)],
        compiler_params=_params(3),
    )(dz, qg, qg, o, lse, crow, kv, kv)


def loss_head(y, target, name):
    T = y.shape[0]
    tm = _tile(T, TM, 8)

    def body(y_ref, t_ref, dy_ref, loss_ref):
        err = y_ref[...] - t_ref[...]
        dy_ref[...] = err * (1.0 / D)
        part = 0.5 * jnp.sum(jnp.mean(err * err, axis=-1, keepdims=True), axis=0, keepdims=True)
        _accumulate(loss_ref, pl.program_id(0) == 0, jnp.broadcast_to(part, (1, LANES)))

    row = pl.BlockSpec((tm, D), lambda i: (i, 0))
    return pl.pallas_call(
        body, name=name, grid=(T // tm,), in_specs=[row, row],
        out_specs=[row, pl.BlockSpec((1, LANES), lambda i: (0, 0))],
        out_shape=[jax.ShapeDtypeStruct((T, D), F32), jax.ShapeDtypeStruct((1, LANES), F32)],
        compiler_params=_params(1),
    )(y, target)


def _adamw(w, g, m, v):
    m = ADAM_B1 * m + (1.0 - ADAM_B1) * g
    v = ADAM_B2 * v + (1.0 - ADAM_B2) * (g * g)
    m_hat = m / (1.0 - ADAM_B1 ** ADAM_STEP)
    v_hat = v / (1.0 - ADAM_B2 ** ADAM_STEP)
    delta = -ADAM_LR * (m_hat / (jnp.sqrt(v_hat) + ADAM_EPS) + ADAM_WD * w)
    return delta, m, v


def adam_sharded(w, m, v, contribs, name):
    L, R, C = w.shape
    tr = _tile(R, 256, 16)

    def body(w_ref, m_ref, v_ref, *refs):
        c_refs, (g_ref, d_ref, nm_ref, nv_ref) = refs[:2 * L], refs[2 * L:]
        l = pl.program_id(0)
        for j in range(L):
            @pl.when(l == j)
            def _(j=j):
                own_ref, recv_ref = c_refs[2 * j], c_refs[2 * j + 1]
                g = own_ref[...].astype(F32)
                for k in range(N_DEV - 1):
                    g = g + recv_ref[k].astype(F32)
                delta, nm, nv = _adamw(w_ref[0], g, m_ref[0], v_ref[0])
                g_ref[0] = g
                d_ref[0] = delta
                nm_ref[0] = nm
                nv_ref[0] = nv

    blk = pl.BlockSpec((1, tr, C), lambda l, i: (l, i, 0))
    in_specs = [blk, blk, blk]
    inputs = [w, m, v]
    for j, (own, recv) in enumerate(contribs):
        in_specs.append(pl.BlockSpec((tr, C), lambda l, i, j=j: (jnp.where(l == j, i, 0), 0)))
        in_specs.append(pl.BlockSpec((N_DEV - 1, tr, C), lambda l, i, j=j: (0, jnp.where(l == j, i, 0), 0)))
        inputs += [own, recv]
    shp = jax.ShapeDtypeStruct((L, R, C), F32)
    return pl.pallas_call(
        body, name=name, grid=(L, R // tr), in_specs=in_specs, out_specs=[blk] * 4, out_shape=[shp] * 4,
        compiler_params=_params(2),
    )(*inputs)


def adam_small(params, total, extra_grads, name):
    n, ne = len(params), len(extra_grads)

    def body(*refs):
        total_ref, extra_refs = refs[0], refs[1:1 + ne]
        ins, outs = refs[1 + ne:1 + ne + 3 * n], refs[1 + ne + 3 * n:]
        for k, (_, _, _, where) in enumerate(params):
            if isinstance(where, int):
                g = extra_refs[where][...]
            else:
                row, rows, width = where
                g = total_ref[row:row + rows, 0:width]
            delta, nm, nv = _adamw(ins[3 * k][...], g, ins[3 * k + 1][...], ins[3 * k + 2][...])
            outs[4 * k][...] = g
            outs[4 * k + 1][...] = delta
            outs[4 * k + 2][...] = nm
            outs[4 * k + 3][...] = nv

    flat = [a for w, m, v, _ in params for a in (w, m, v)]
    out_shape = [jax.ShapeDtypeStruct(w.shape, F32) for w, _, _, _ in params for _ in range(4)]
    res = pl.pallas_call(body, name=name, out_shape=out_shape)(total, *extra_grads, *flat)
    return [res[4 * k:4 * k + 4] for k in range(n)]


def _place():
    return lax.axis_index("x"), lax.axis_index("y"), lax.axis_index("c")


def _peer(place, k):
    x, y, c = place
    return x ^ (k >> 2), y ^ ((k >> 1) & 1), c ^ (k & 1)


ANY = pl.BlockSpec(memory_space=pl.ANY)
HBM = pl.BlockSpec(memory_space=pltpu.HBM)
SEM = pl.BlockSpec(memory_space=pltpu.SEMAPHORE)
EFFECT = pltpu.SideEffectType.DATAFLOW_SIDE_EFFECTING


def _in_hbm(a):
    return pltpu.with_memory_space_constraint(a, pltpu.HBM)


def _number(place):
    return 4 * place[0] + 2 * place[1] + place[2]


SLOTS = {"gather_like": 4, "gather_pass": 3, "scatter": 7}


def _plan(mode, src_ref, land_ref, place):
    if mode == "gather_like":
        return [(src_ref, land_ref.at[_number(place)], _peer(place, k)) for k in (1, 2, 4, 6)]
    if mode == "gather_pass":
        slots = [land_ref.at[_number(_peer(place, k))] for k in (2, 4, 6)]
        return [(slot, slot, _peer(place, 1)) for slot in slots]
    return [(src_ref.at[_number(_peer(place, k))], land_ref.at[k - 1], _peer(place, k)) for k in range(1, N_DEV)]


def _exchange(name, groups, start, afters):
    sizes = [len(g[2]) for g in groups]
    na, ng = sum(sizes), len(groups)
    waits = groups[0][0] is not None
    n_in_sems = 2 * ng if waits else 0
    n_out_sems = 2 * ng if start else 0

    def body(*refs):
        src_refs, land_refs = refs[:na], refs[na:2 * na]
        in_sems = refs[2 * na:2 * na + n_in_sems]
        outs = refs[2 * na + n_in_sems + len(afters):]
        place = _place()
        a = 0
        for gi, n in enumerate(sizes):
            for idx in range(n):
                if waits:
                    zone = land_refs[a].at[pl.ds(0, groups[gi][4])]
                    copy = pltpu.make_async_remote_copy(
                        src_ref=zone, dst_ref=zone, send_sem=in_sems[2 * gi].at[idx], recv_sem=in_sems[2 * gi + 1].at[idx],
                        device_id=_peer(place, 1), device_id_type=MESH)
                    copy.wait_send()
                    copy.wait_recv()
                if start:
                    for src, dst, peer in _plan(start, src_refs[a], land_refs[a], place):
                        pltpu.make_async_remote_copy(
                            src_ref=src, dst_ref=dst, send_sem=outs[2 * gi].at[idx], recv_sem=outs[2 * gi + 1].at[idx],
                            device_id=peer, device_id_type=MESH).start()
                a += 1
        if start:
            outs[-1][...] = jnp.zeros_like(outs[-1])

    srcs = [_in_hbm(s) for g in groups for s in g[2]]
    lands = [_in_hbm(l) for g in groups for l in g[3]]
    sems = [s for g in groups for s in g[:2]] if waits else []
    out_shape = [pltpu.SemaphoreType.DMA((n,)) for n in sizes for _ in range(2)] if start else []
    out_shape += [pltpu.HBM(a.shape, a.dtype) for a in srcs + lands]
    out_specs = [SEM] * n_out_sems + [HBM] * (2 * na)
    if start:
        out_shape.append(jax.ShapeDtypeStruct((8, LANES), F32))
        out_specs.append(pl.BlockSpec(memory_space=pltpu.VMEM))
    res = pl.pallas_call(
        body, name=name, in_specs=[HBM] * (2 * na) + [SEM] * n_in_sems + [ANY] * len(afters),
        out_shape=out_shape, out_specs=out_specs,
        input_output_aliases={i: n_out_sems + i for i in range(2 * na)},
        compiler_params=pltpu.CompilerParams(has_side_effects=EFFECT),
    )(*srcs, *lands, *sems, *afters)
    new_sems, thru = res[:n_out_sems], res[n_out_sems:n_out_sems + 2 * na]
    out, a = [], 0
    for gi, n in enumerate(sizes):
        pair = (new_sems[2 * gi], new_sems[2 * gi + 1]) if start else (None, None)
        out.append(pair + (thru[a:a + n], thru[na + a:na + a + n], SLOTS.get(start, 0)))
        a += n
    return out, (res[-1] if start else None)


def exchange_start(pair_groups, mode, name):
    groups = [(None, None, [s for s, _ in g], [l for _, l in g], 0) for g in pair_groups]
    return _exchange(name, groups, mode, ())


def exchange_relay(groups, mode, afters, name):
    return _exchange(name, groups, mode, afters)


def exchange_wait(groups, afters, name):
    done, _ = _exchange(name, groups, None, afters)
    return [(g[2], g[3]) for g in done]


def all_reduce_small(parts, n_rows, name):
    R = n_rows
    n_parts = len(parts)

    def body(*refs):
        part_refs = refs[:n_parts]
        out_ref, buf, send_sems, recv_sems = refs[n_parts:]
        x, y, c = _place()
        me = 4 * x + 2 * y + c
        own = buf.at[me]
        own[...] = jnp.zeros((R, D), F32)
        for ref, (arr, row) in zip(part_refs, parts):
            own[row:row + arr.shape[0], 0:arr.shape[1]] = ref[...]
        copies = []
        for k in range(1, N_DEV):
            peer = (x ^ (k >> 2), y ^ ((k >> 1) & 1), c ^ (k & 1))
            copies.append(pltpu.make_async_remote_copy(
                src_ref=own, dst_ref=own, send_sem=send_sems.at[k - 1], recv_sem=recv_sems.at[k - 1],
                device_id=peer, device_id_type=MESH))
        for cp in copies:
            cp.start()
        for cp in copies:
            cp.wait()
        total = buf[0]
        for d in range(1, N_DEV):
            total = total + buf[d]
        out_ref[...] = total

    vm = pl.BlockSpec(memory_space=pltpu.VMEM)
    return pl.pallas_call(
        body, name=name, in_specs=[vm] * n_parts, out_specs=vm, out_shape=jax.ShapeDtypeStruct((R, D), F32),
        scratch_shapes=[pltpu.VMEM((N_DEV, R, D), F32), pltpu.SemaphoreType.DMA((N_DEV - 1,)),
                        pltpu.SemaphoreType.DMA((N_DEV - 1,))],
    )(*[arr for arr, _ in parts])


def _col_blocks(gathered, n_blocks):
    n, d, w = gathered.shape
    whole = gathered.transpose(1, 0, 2).reshape(d, n * w)
    return whole.reshape(d, n_blocks, n * w // n_blocks).transpose(1, 0, 2)[:, None]


def _col_shards(blocks):
    n, d, w = blocks.shape
    whole = blocks.transpose(1, 0, 2).reshape(d, n * w)
    return whole.reshape(d, N_DEV, n * w // N_DEV).transpose(1, 0, 2)


def kernel(x, ffn1_pre_g, ffn1_post_g, ffn1_w_in, ffn1_w_out, mix_pre_g, mix_post_g, ffn2_pre_g, ffn2_post_g, ffn2_w_in, ffn2_w_out, conv_w_in, conv_k, conv_w_out, kv_g, kv_w, forget_b, attn_w_qg, attn_w_o, loss_target, m_ffn1_pre_g, m_ffn1_post_g, m_ffn1_w_in, m_ffn1_w_out, m_mix_pre_g, m_mix_post_g, m_ffn2_pre_g, m_ffn2_post_g, m_ffn2_w_in, m_ffn2_w_out, m_conv_w_in, m_conv_k, m_conv_w_out, m_kv_g, m_kv_w, m_forget_b, m_attn_w_qg, m_attn_w_o, v_ffn1_pre_g, v_ffn1_post_g, v_ffn1_w_in, v_ffn1_w_out, v_mix_pre_g, v_mix_post_g, v_ffn2_pre_g, v_ffn2_post_g, v_ffn2_w_in, v_ffn2_w_out, v_conv_w_in, v_conv_k, v_conv_w_out, v_kv_g, v_kv_w, v_forget_b, v_attn_w_qg, v_attn_w_o):
    n_seq, seq_len, _ = x.shape
    T = n_seq * seq_len
    xi, yi, ci = _place()
    dev = 4 * xi + 2 * yi + ci
    x0 = x.reshape(T, D)
    target = loss_target.reshape(T, D)

    def layer(w, l):
        return w[l:l + 1].astype(BF16)

    shard_groups = [
        [layer(ffn1_w_in, 0), layer(ffn1_w_out, 0)],
        [conv_w_in.astype(BF16), conv_w_out.astype(BF16), conv_k],
        [layer(ffn2_w_in, 0), layer(ffn2_w_out, 0)],
        [kv_w.astype(BF16), layer(ffn1_w_in, 1), layer(ffn1_w_out, 1), attn_w_qg.astype(BF16), attn_w_o.astype(BF16)],
        [layer(ffn2_w_in, 1), layer(ffn2_w_out, 1)]]
    def zone(s):
        return lax.dynamic_update_slice(lax.empty((N_DEV,) + s.shape, s.dtype), s[None], (dev,) + (0,) * s.ndim)

    gathers, token = exchange_start([[(s, zone(s)) for s in g] for g in shard_groups], "gather_like", "gather_start")

    def gathered(k, after):
        passed, _ = exchange_relay([gathers[k]], "gather_pass", [after], f"gather_pass{k}")
        return exchange_wait(passed, [after], f"gather_wait{k}")[0][1]

    fb = jnp.pad(forget_b, (0, LANES - N_HEADS))[None]

    def vec(g, l):
        return g[l:l + 1]

    def behind(g, tok):
        return g + tok[:1, :1]

    grads_small = {}

    def ffn_fwd(xin, g_pre, g_post, w_in, w_out, tag):
        xn, (gu,) = rms_proj(xin, g_pre, [(w_in, 0)], [BF16], f"{tag}_in")
        tm = _tile(T, TM, 16)
        fbw = gu.shape[2]
        specs = [pl.BlockSpec((1, tm, fbw), lambda i, k: (k, i, 0)), pl.BlockSpec((1, tm, fbw), lambda i, k: (k + 4, i, 0))]
        a, h, y = mix_out(_swiglu_pro, [gu, gu], specs, (w_out, 0), xin, g_post, 0.5, 4, f"{tag}_out")
        return y, (xin, xn, gu, a, h)

    def ffn_bwd(dy, saved, g_pre, g_post, w_in, w_out, tag):
        xin, xn, gu, a, h = saved
        dh, dg_post, dgate, dup = post_bwd(dy, h, g_post, 0.5, (w_out, 0), 4, f"{tag}_bwd_out", gu=gu)
        dw_out = wgrad([a], [dh[None]], 4, w_out.shape, f"{tag}_dw_out")
        dw_in = wgrad([xn[None]], [dgate, dup], 8, w_in.shape, f"{tag}_dw_in")
        started, tok = scatter_start([dw_in.reshape(8, -1, dw_in.shape[3]), dw_out.reshape(8, -1, D)], f"{tag}_scatter_start")
        dx, dg_pre = pre_bwd([(dgate, (w_in, 0), 0), (dup, (w_in, 0), 4)], xin, behind(g_pre, tok), dy, f"{tag}_bwd_in")
        return dx, dg_pre, dg_post, started

    def scatter_start(blocked, name):
        pairs = [(g, lax.empty((N_DEV - 1,) + g.shape[1:], g.dtype)) for g in blocked]
        started, tok = exchange_start([pairs], "scatter", name)
        return started[0], tok

    w1_in, w1_out = gathered(0, token)
    x1, s_f1a = ffn_fwd(x0, vec(ffn1_pre_g, 0), vec(ffn1_post_g, 0), w1_in, w1_out, "l0_ffn1")
    cw_in_g, cw_out, ck_g = gathered(1, x1)
    cw_in = _col_blocks(cw_in_g[:, 0], 3)
    ck = ck_g[:, 0].transpose(1, 0, 2).reshape(3, D)
    xn_c, (bch,) = rms_proj(x1, vec(mix_pre_g, 0), [(cw_in, 0)], [BF16], "conv_in")
    tmc = _tile(seq_len, TM, 16)
    hb = tmc // HALO

    def cpiece(p):
        return pl.BlockSpec((1, tmc, D), lambda i, k, p=p: (p, i, 0))

    def chalo(p):
        return pl.BlockSpec((1, HALO, D), lambda i, k, p=p: (p, jnp.maximum(i * hb - 1, 0), 0))

    conv_specs = [cpiece(0), cpiece(1), cpiece(2), chalo(1), chalo(2), pl.BlockSpec((3, D), lambda i, k: (0, 0))]
    z_c, m_c, x2 = mix_out(_make_conv_pro(seq_len // tmc), [bch, bch, bch, bch, bch, ck], conv_specs, (cw_out, 0),
                           x1, vec(mix_post_g, 0), 1.0, 1, "conv_out", tm=tmc)
    w2_in, w2_out = gathered(2, x2)
    x3, s_f2a = ffn_fwd(x2, vec(ffn2_pre_g, 0), vec(ffn2_post_g, 0), w2_in, w2_out, "l0_ffn2")

    kvw_g, w1_in_b, w1_out_b, qgw_g, ow = gathered(3, x3)
    qg_w = _col_blocks(qgw_g[:, 0], 2)
    kv_whole = kvw_g.transpose(1, 0, 2).reshape(D, 2 * D + N_HEADS)
    kv_wb = kv_whole[:, :2 * D].reshape(D, 2, D).transpose(1, 0, 2)[:, None]
    f_w = jnp.pad(kv_whole[:, 2 * D:], ((0, 0), (0, LANES - N_HEADS)))[None, None]
    xn_kv, (kv, fl) = rms_proj(x3, kv_g[None], [(kv_wb, 0), (f_w, 0)], [BF16, F32], "kv_in")
    c128 = forget_fwd(fl, fb, seq_len, "forget_fwd")
    c16 = c128[:, :N_HEADS]
    tq = _tile(seq_len, TQ, LANES)
    crow = c16.reshape(n_seq, seq_len // tq, tq, N_HEADS // 2, 2).transpose(0, 3, 1, 4, 2)

    x4, s_f1b = ffn_fwd(x3, vec(ffn1_pre_g, 1), vec(ffn1_post_g, 1), w1_in_b, w1_out_b, "l1_ffn1")
    xn_a, (qg,) = rms_proj(x4, vec(mix_pre_g, 1), [(qg_w, 0)], [BF16], "attn_in")
    o, lse = attn_fwd(qg, kv, crow, n_seq, seq_len, "attn_fwd")
    tm = _tile(T, TM, 16)
    gate_specs = [pl.BlockSpec((1, tm, D), lambda i, k: (1, i, 0)), pl.BlockSpec((tm, D), lambda i, k: (i, 0))]
    z_a, m_a, x5 = mix_out(_attn_gate_pro, [qg, o], gate_specs, (ow, 0), x4, vec(mix_post_g, 1), 1.0, 1, "attn_out")
    w2_in_b, w2_out_b = gathered(4, x5)
    x6, s_f2b = ffn_fwd(x5, vec(ffn2_pre_g, 1), vec(ffn2_post_g, 1), w2_in_b, w2_out_b, "l1_ffn2")

    dy, loss_part = loss_head(x6, target, "loss_head")

    scatters = {}
    dx5, dg, dgp, scatters["ffn2", 1] = ffn_bwd(dy, s_f2b, vec(ffn2_pre_g, 1), vec(ffn2_post_g, 1), w2_in_b, w2_out_b, "l1_ffn2")
    grads_small["ffn2_pre", 1], grads_small["ffn2_post", 1] = dg, dgp
    dm_a, dgp, dz_a = post_bwd(dx5, m_a, vec(mix_post_g, 1), 1.0, (ow, 0), 1, "attn_bwd_out")
    grads_small["mix_post", 1] = dgp
    dq, dgate, dk, dv, dcrow = attn_bwd(dz_a[0], qg, kv, o, lse, crow, n_seq, seq_len, "attn_bwd")
    dx4, dg = pre_bwd([(dq, (qg_w, 0), 0), (dgate, (qg_w, 0), 1)], x4, vec(mix_pre_g, 1), dx5, "attn_bwd_in")
    grads_small["mix_pre", 1] = dg
    d_ow = wgrad([z_a], [dm_a[None]], 1, ow.shape, "attn_dw_o")
    d_qgw = wgrad([xn_a[None]], [dq, dgate], 2, (2, 1, D, D), "attn_dw_qg")
    scatters["attn"], tok = scatter_start([_col_shards(d_qgw[:, 0]), d_ow.reshape(8, -1, D)], "attn_scatter_start")
    dx3, dg, dgp, scatters["ffn1", 1] = ffn_bwd(dx4, s_f1b, vec(ffn1_pre_g, 1), behind(vec(ffn1_post_g, 1), tok),
                                                w1_in_b, w1_out_b, "l1_ffn1")
    grads_small["ffn1_pre", 1], grads_small["ffn1_post", 1] = dg, dgp

    dc16 = dcrow.transpose(0, 2, 4, 1, 3).reshape(T, N_HEADS)
    dfl, dfb = forget_bwd(jnp.pad(dc16, ((0, 0), (0, LANES - N_HEADS))), fl, fb, seq_len, "forget_bwd")
    dx3, dg_kv = pre_bwd([(dk, (kv_wb, 0), 0), (dv, (kv_wb, 0), 1), (dfl, (f_w, 0), 0)], x3, kv_g[None], dx3, "kv_bwd_in")
    d_kvw = wgrad([xn_kv[None]], [dk, dv], 2, (2, 1, D, D), "kv_dw")
    d_fw = wgrad([xn_kv[None]], [dfl.astype(BF16)], 1, (1, 1, D, LANES), "forget_dw")
    d_kv_whole = jnp.concatenate([d_kvw[0, 0], d_kvw[1, 0], d_fw[0, 0, :, :N_HEADS]], axis=1)
    wshard = D * 2 + N_HEADS
    scatters["kv"], tok = scatter_start([d_kv_whole.reshape(D, N_DEV, wshard // N_DEV).transpose(1, 0, 2)], "kv_scatter_start")

    dx2, dg, dgp, scatters["ffn2", 0] = ffn_bwd(dx3, s_f2a, vec(ffn2_pre_g, 0), behind(vec(ffn2_post_g, 0), tok),
                                                w2_in, w2_out, "l0_ffn2")
    grads_small["ffn2_pre", 0], grads_small["ffn2_post", 0] = dg, dgp
    dm_c, dgp, dz_c = post_bwd(dx2, m_c, vec(mix_post_g, 0), 1.0, (cw_out, 0), 1, "conv_bwd_out")
    grads_small["mix_post", 0] = dgp
    dbch, d_ck = conv_bwd_mix(dz_c, bch, ck, seq_len, "conv_bwd_mix")
    dx1, dg = pre_bwd([(dbch, (cw_in, 0), 0)], x1, vec(mix_pre_g, 0), dx2, "conv_bwd_in")
    grads_small["mix_pre", 0] = dg
    d_cw_out = wgrad([z_c], [dm_c[None]], 1, cw_out.shape, "conv_dw_out")
    d_cw_in = wgrad([xn_c[None]], [dbch], 3, (3, 1, D, D), "conv_dw_in")
    scatters["conv"], tok = scatter_start([_col_shards(d_cw_in[:, 0]), d_cw_out.reshape(8, -1, D)], "conv_scatter_start")
    dx0, dg, dgp, scatters["ffn1", 0] = ffn_bwd(dx1, s_f1a, vec(ffn1_pre_g, 0), behind(vec(ffn1_post_g, 0), tok),
                                                w1_in, w1_out, "l0_ffn1")
    grads_small["ffn1_pre", 0], grads_small["ffn1_post", 0] = dg, dgp

    parts_of = {}

    def scatter_end(keys, afters, name):
        for k, (sent, recv) in zip(keys, exchange_wait([scatters[k] for k in keys], afters, name)):
            parts_of[k] = [(lax.dynamic_index_in_dim(g, dev, 0, keepdims=False), r) for g, r in zip(sent, recv)]

    scatter_end([("ffn2", 1), "attn", ("ffn1", 1), "kv", ("ffn2", 0), "conv"], [dx0], "scatter_wait")
    sharded = {"ffn2_w_in": (ffn2_w_in, m_ffn2_w_in, v_ffn2_w_in), "ffn2_w_out": (ffn2_w_out, m_ffn2_w_out, v_ffn2_w_out),
               "conv_w_in": (conv_w_in, m_conv_w_in, v_conv_w_in), "conv_w_out": (conv_w_out, m_conv_w_out, v_conv_w_out),
               "kv_w": (kv_w, m_kv_w, v_kv_w), "attn_w_qg": (attn_w_qg, m_attn_w_qg, v_attn_w_qg),
               "attn_w_o": (attn_w_o, m_attn_w_o, v_attn_w_o),
               "ffn1_w_in": (ffn1_w_in, m_ffn1_w_in, v_ffn1_w_in), "ffn1_w_out": (ffn1_w_out, m_ffn1_w_out, v_ffn1_w_out)}
    out = {}
    for nm, (w, mm, vv) in sharded.items():
        if nm == "ffn1_w_in":
            scatter_end([("ffn1", 0)], [res[0] for res in out.values()], "scatter_wait_last")
        contribs = {
            "ffn1_w_in": lambda: [parts_of["ffn1", 0][0], parts_of["ffn1", 1][0]],
            "ffn1_w_out": lambda: [parts_of["ffn1", 0][1], parts_of["ffn1", 1][1]],
            "ffn2_w_in": lambda: [parts_of["ffn2", 0][0], parts_of["ffn2", 1][0]],
            "ffn2_w_out": lambda: [parts_of["ffn2", 0][1], parts_of["ffn2", 1][1]],
            "conv_w_in": lambda: [parts_of["conv"][0]], "conv_w_out": lambda: [parts_of["conv"][1]],
            "kv_w": lambda: [parts_of["kv"][0]], "attn_w_qg": lambda: [parts_of["attn"][0]],
            "attn_w_o": lambda: [parts_of["attn"][1]]}[nm]()
        shape3 = (len(contribs),) + contribs[0][0].shape
        res = adam_sharded(w.reshape(shape3), mm.reshape(shape3), vv.reshape(shape3), contribs, f"adam_{nm}")
        out[nm] = [r.reshape(w.shape) for r in res]

    small_names = ["ffn1_pre", "ffn1_post", "mix_pre", "mix_post", "ffn2_pre", "ffn2_post"]
    parts = [(grads_small[n, l], 2 * k + l) for k, n in enumerate(small_names) for l in range(2)]
    parts += [(dg_kv, 12), (dfb, 13), (d_ck, 14), (loss_part, 17)]
    total = all_reduce_small(parts, 24, "all_reduce_small")
    loss = total[17, 0]
    d_ck_mine = lax.dynamic_slice(total, (14, dev * LANES), (3, LANES))
    gains = [(ffn1_pre_g, m_ffn1_pre_g, v_ffn1_pre_g), (ffn1_post_g, m_ffn1_post_g, v_ffn1_post_g),
             (mix_pre_g, m_mix_pre_g, v_mix_pre_g), (mix_post_g, m_mix_post_g, v_mix_post_g),
             (ffn2_pre_g, m_ffn2_pre_g, v_ffn2_pre_g), (ffn2_post_g, m_ffn2_post_g, v_ffn2_post_g)]
    small_params = [(w, m, v, (2 * k, 2, D)) for k, (w, m, v) in enumerate(gains)]
    small_params += [(kv_g[None], m_kv_g[None], v_kv_g[None], (12, 1, D)),
                     (forget_b[None], m_forget_b[None], v_forget_b[None], (13, 1, N_HEADS)),
                     (conv_k[0], m_conv_k[0], v_conv_k[0], 0)]
    small_res = adam_small(small_params, total, [d_ck_mine], "adam_small")
    small_keys = [n + "_g" for n in small_names] + ["kv_g", "forget_b", "conv_k"]
    shapes = {"kv_g": kv_g.shape, "forget_b": forget_b.shape, "conv_k": conv_k.shape}
    small = [{key: res[kind].reshape(shapes.get(key, res[kind].shape)) for key, res in zip(small_keys, small_res)}
             for kind in range(4)]
    order = ["ffn1_pre_g", "ffn1_post_g", "ffn1_w_in", "ffn1_w_out", "mix_pre_g", "mix_post_g", "ffn2_pre_g", "ffn2_post_g",
             "ffn2_w_in", "ffn2_w_out", "conv_w_in", "conv_k", "conv_w_out", "kv_g", "kv_w", "forget_b", "attn_w_qg",
             "attn_w_o"]
    results = [loss, dx0.reshape(x.shape)]
    for kind in range(4):
        for nm in order:
            results.append(out[nm][kind] if nm in out else small[kind][nm])
    return tuple(results)
```

```python
import functools
import math

import jax
import jax.numpy as jnp
from jax import lax
from jax.experimental import pallas as pl
from jax.experimental.pallas import tpu as pltpu

F32, BF16 = jnp.float32, jnp.bfloat16
D = 1024
N_HEADS = 16
HEAD_DIM = 64
N_DEV = 8
RMS_EPS = 1e-6
ATT_SCALE = 1.0 / math.sqrt(HEAD_DIM)
LANES = 128
HALO = 8
TM = 512
TQ = 256
VMEM_LIMIT = 48 * 1024 * 1024
MESH = pl.DeviceIdType.MESH

ADAM_LR, ADAM_B1, ADAM_B2, ADAM_EPS, ADAM_WD, ADAM_STEP = 0.001, 0.9, 0.999, 1e-08, 0.01, 10

NT = (((1,), (1,)), ((), ()))
TN = (((0,), (0,)), ((), ()))


def _params(n_axes):
    return pltpu.CompilerParams(dimension_semantics=("arbitrary",) * n_axes, vmem_limit_bytes=VMEM_LIMIT)


def _tile(n, cap, mult):
    best = None
    for t in range(mult, min(n, cap) + 1, mult):
        if n % t == 0:
            best = t
    assert best is not None, (n, cap, mult)
    return best


def _rms_rstd(x):
    return lax.rsqrt(jnp.mean(x * x, axis=-1, keepdims=True) + RMS_EPS)


def _rms_fwd(x, g):
    return x * _rms_rstd(x) * g


def _rms_bwd(x, g, dy):
    xh = x * _rms_rstd(x)
    dyg = dy * g
    dx = _rms_rstd(x) * (dyg - xh * jnp.mean(dyg * xh, axis=-1, keepdims=True))
    return dx, jnp.sum(dy * xh, axis=0, keepdims=True)


def _accumulate(ref, first, value):
    @pl.when(first)
    def _():
        ref[...] = value

    @pl.when(jnp.logical_not(first))
    def _():
        ref[...] += value


def rms_proj(x, g, ws, out_dtypes, name):
    T = x.shape[0]
    tm = _tile(T, TM, 16)
    na = len(ws)

    def body(x_ref, g_ref, *refs):
        w_refs, xn_ref, o_refs = refs[:na], refs[na], refs[na + 1:]
        xn = _rms_fwd(x_ref[...], g_ref[...]).astype(BF16)
        xn_ref[...] = xn
        for a, (w, l) in enumerate(ws):
            for p in range(w.shape[0]):
                o_refs[a][p] = jnp.dot(xn, w_refs[a][p, l], preferred_element_type=F32).astype(o_refs[a].dtype)

    in_specs = [pl.BlockSpec((tm, D), lambda i: (i, 0)), pl.BlockSpec((1, D), lambda i: (0, 0))]
    in_specs += [pl.BlockSpec(w.shape, lambda i: (0, 0, 0, 0), pipeline_mode=pl.Buffered(1)) for w, _ in ws]
    out_specs = [pl.BlockSpec((tm, D), lambda i: (i, 0))]
    out_shape = [jax.ShapeDtypeStruct((T, D), BF16)]
    for (w, _), dt in zip(ws, out_dtypes):
        nb, wb = w.shape[0], w.shape[3]
        out_specs.append(pl.BlockSpec((nb, tm, wb), lambda i: (0, i, 0)))
        out_shape.append(jax.ShapeDtypeStruct((nb, T, wb), dt))
    res = pl.pallas_call(
        body, name=name, grid=(T // tm,), in_specs=in_specs, out_specs=out_specs, out_shape=out_shape,
        compiler_params=_params(1),
    )(x, g, *[w for w, _ in ws])
    return res[0], res[1:]


def mix_out(pro, pro_inputs, pro_specs, w, res, g_post, alpha, nk, name, tm=None):
    w4, l = w
    T = res.shape[0]
    tm = _tile(T, TM, 16) if tm is None else tm
    dpb = N_DEV // nk
    rows = w4.shape[2]
    kb = dpb * rows
    npi = len(pro_inputs)

    def body(*refs):
        pro_refs = refs[:npi]
        w_ref, res_ref, g_ref, z_ref, m_ref, y_ref, acc = refs[npi:]
        i, k = pl.program_id(0), pl.program_id(1)
        z = pro(i, k, *pro_refs).astype(BF16)
        z_ref[0] = z
        part = jnp.dot(z, w_ref[:, 0].reshape(kb, D), preferred_element_type=F32)
        _accumulate(acc, k == 0, part)

        @pl.when(k == nk - 1)
        def _():
            m = acc[...]
            m_ref[...] = m
            y_ref[...] = res_ref[...] + alpha * _rms_fwd(m, g_ref[...])

    row = pl.BlockSpec((tm, D), lambda i, k: (i, 0))
    in_specs = list(pro_specs) + [
        pl.BlockSpec((dpb, 1, rows, D), lambda i, k: (k, l, 0, 0)), row, pl.BlockSpec((1, D), lambda i, k: (0, 0))]
    z, m, y = pl.pallas_call(
        body, name=name, grid=(T // tm, nk), in_specs=in_specs,
        out_specs=[pl.BlockSpec((1, tm, kb), lambda i, k: (k, i, 0)), row, row],
        out_shape=[jax.ShapeDtypeStruct((nk, T, kb), BF16), jax.ShapeDtypeStruct((T, D), F32),
                   jax.ShapeDtypeStruct((T, D), F32)],
        scratch_shapes=[pltpu.VMEM((tm, D), F32)], compiler_params=_params(2),
    )(*pro_inputs, w4, res, g_post)
    return z, m, y


def post_bwd(dy, m, g_post, alpha, w, nk, name, gu=None):
    w4, l = w
    T = dy.shape[0]
    tm = _tile(T, TM, 16)
    dpb = N_DEV // nk
    rows = w4.shape[2]
    kb = dpb * rows
    swiglu = gu is not None

    def body(dy_ref, m_ref, g_ref, w_ref, *refs):
        if swiglu:
            gate_ref, up_ref, dm_ref, dg_ref, dgate_ref, dup_ref, dm_s = refs
        else:
            dm_ref, dg_ref, dz_ref, dm_s = refs
        i, k = pl.program_id(0), pl.program_id(1)

        @pl.when(k == 0)
        def _():
            dm, dg = _rms_bwd(m_ref[...], g_ref[...], alpha * dy_ref[...])
            dm_s[...] = dm.astype(BF16)
            dm_ref[...] = dm.astype(BF16)
            _accumulate(dg_ref, i == 0, dg)

        dz = lax.dot_general(dm_s[...], w_ref[:, 0].reshape(kb, D), NT, preferred_element_type=F32)
        if swiglu:
            gate, up = gate_ref[0].astype(F32), up_ref[0].astype(F32)
            sig = jax.nn.sigmoid(gate)
            dgate_ref[0] = (dz * up * sig * (1.0 + gate * (1.0 - sig))).astype(BF16)
            dup_ref[0] = (dz * gate * sig).astype(BF16)
        else:
            dz_ref[0] = dz.astype(BF16)

    row = pl.BlockSpec((tm, D), lambda i, k: (i, 0))
    vec = pl.BlockSpec((1, D), lambda i, k: (0, 0))
    blkk = pl.BlockSpec((1, tm, kb), lambda i, k: (k, i, 0))
    in_specs = [row, row, vec, pl.BlockSpec((dpb, 1, rows, D), lambda i, k: (k, l, 0, 0))]
    inputs = [dy, m, g_post, w4]
    out_specs = [row, vec, blkk]
    out_shape = [jax.ShapeDtypeStruct((T, D), BF16), jax.ShapeDtypeStruct((1, D), F32),
                 jax.ShapeDtypeStruct((nk, T, kb), BF16)]
    if swiglu:
        in_specs += [blkk, pl.BlockSpec((1, tm, kb), lambda i, k: (k + nk, i, 0))]
        inputs += [gu, gu]
        out_specs.append(blkk)
        out_shape.append(jax.ShapeDtypeStruct((nk, T, kb), BF16))
    return pl.pallas_call(
        body, name=name, grid=(T // tm, nk), in_specs=in_specs, out_specs=out_specs, out_shape=out_shape,
        scratch_shapes=[pltpu.VMEM((tm, D), BF16)], compiler_params=_params(2),
    )(*inputs)


def pre_bwd(pieces, x, g_pre, dres, name):
    T = x.shape[0]
    tm = _tile(T, TM, 16)
    na = len(pieces)
    weights = []
    for _, (w4, _), _ in pieces:
        if not any(w4 is w for w in weights):
            weights.append(w4)
    nw = len(weights)
    which = [[w4 is w for w in weights].index(True) for _, (w4, _), _ in pieces]

    def body(*refs):
        dz_refs, w_refs = refs[:na], refs[na:na + nw]
        x_ref, g_ref, dres_ref, dx_ref, dg_ref = refs[na + nw:]
        acc = None
        for a, (dz, (_, l), w_off) in enumerate(pieces):
            for p in range(dz.shape[0]):
                part = lax.dot_general(dz_refs[a][p].astype(BF16), w_refs[which[a]][w_off + p, l], NT,
                                       preferred_element_type=F32)
                acc = part if acc is None else acc + part
        dx, dg = _rms_bwd(x_ref[...], g_ref[...], acc)
        dx_ref[...] = dres_ref[...] + dx
        _accumulate(dg_ref, pl.program_id(0) == 0, dg)

    row = pl.BlockSpec((tm, D), lambda i: (i, 0))
    vec = pl.BlockSpec((1, D), lambda i: (0, 0))
    dz_specs = [pl.BlockSpec((dz.shape[0], tm, dz.shape[2]), lambda i: (0, i, 0)) for dz, _, _ in pieces]
    w_specs = [pl.BlockSpec(w.shape, lambda i: (0, 0, 0, 0), pipeline_mode=pl.Buffered(1)) for w in weights]
    return pl.pallas_call(
        body, name=name, grid=(T // tm,), in_specs=dz_specs + w_specs + [row, vec, row],
        out_specs=[row, vec], out_shape=[jax.ShapeDtypeStruct((T, D), F32), jax.ShapeDtypeStruct((1, D), F32)],
        compiler_params=_params(1),
    )(*[p[0] for p in pieces], *weights, x, g_pre, dres)


def wgrad(a_list, b_list, nout, out4_shape, name):
    T = a_list[0].shape[1]
    tm = _tile(T, 4 * TM, 16)
    nt = T // tm
    dpb = out4_shape[0] // nout
    _, _, R, C = out4_shape
    na, nb = len(a_list), len(b_list)

    def spans(arrs):
        ns = [a.shape[0] for a in arrs]
        return ns, [sum(ns[:k]) for k in range(len(ns))], sum(ns)

    a_ns, a_offs, a_tot = spans(a_list)
    b_ns, b_offs, b_tot = spans(b_list)
    assert a_tot in (1, nout) and b_tot in (1, nout)

    def body(*refs):
        a_refs, b_refs = refs[:na], refs[na:na + nb]
        out_ref, acc = refs[-2:]
        p, t = pl.program_id(0), pl.program_id(1)
        for ia in range(na):
            for ib in range(nb):
                conds = []
                if a_tot > 1:
                    conds += [p >= a_offs[ia], p < a_offs[ia] + a_ns[ia]]
                if b_tot > 1:
                    conds += [p >= b_offs[ib], p < b_offs[ib] + b_ns[ib]]

                def work(ia=ia, ib=ib):
                    part = lax.dot_general(a_refs[ia][0], b_refs[ib][0], TN, preferred_element_type=F32)
                    _accumulate(acc, t == 0, part)

                if conds:
                    pl.when(functools.reduce(jnp.logical_and, conds))(work)
                else:
                    work()

        @pl.when(t == nt - 1)
        def _():
            out_ref[...] = acc[...].astype(BF16).reshape(dpb, 1, R, C)

    def blk(off, n, tot):
        if tot == 1:
            return lambda p: 0
        return lambda p: jnp.clip(p - off, 0, n - 1)

    in_specs = []
    for arrs, ns, offs, tot in ((a_list, a_ns, a_offs, a_tot), (b_list, b_ns, b_offs, b_tot)):
        for arr, n, off in zip(arrs, ns, offs):
            in_specs.append(pl.BlockSpec((1, tm, arr.shape[2]), lambda p, t, f=blk(off, n, tot): (f(p), t, 0)))
    return pl.pallas_call(
        body, name=name, grid=(nout, nt), in_specs=in_specs,
        out_specs=pl.BlockSpec((dpb, 1, R, C), lambda p, t: (p, 0, 0, 0)),
        out_shape=jax.ShapeDtypeStruct(out4_shape, BF16),
        scratch_shapes=[pltpu.VMEM((dpb * R, C), F32)], compiler_params=_params(2),
    )(*a_list, *b_list)


def _swiglu_pro(i, k, gate_ref, up_ref):
    gate, up = gate_ref[0].astype(F32), up_ref[0].astype(F32)
    return gate * jax.nn.sigmoid(gate) * up


def _attn_gate_pro(i, k, gate_ref, o_ref):
    return jax.nn.sigmoid(gate_ref[0].astype(F32)) * o_ref[...].astype(F32)


def _shift_rows(u, halo, d):
    rolled = pltpu.roll(u, d, 0)
    row = lax.broadcasted_iota(jnp.int32, u.shape, 0)
    for r in range(d):
        rolled = jnp.where(row == r, halo[HALO - d + r:HALO - d + r + 1, :], rolled)
    return rolled


def _advance_rows(u, halo, d):
    n = u.shape[0]
    rolled = pltpu.roll(u, n - d, 0)
    row = lax.broadcasted_iota(jnp.int32, u.shape, 0)
    for r in range(d):
        rolled = jnp.where(row == n - d + r, halo[r:r + 1, :], rolled)
    return rolled


def _make_conv_pro(tiles_per_seq):
    def pro(i, k, b_ref, c_ref, h_ref, ch_ref, hh_ref, ck_ref):
        u = c_ref[0].astype(F32) * h_ref[0].astype(F32)
        first = (i % tiles_per_seq) == 0
        halo = jnp.where(first, 0.0, ch_ref[0].astype(F32) * hh_ref[0].astype(F32))
        ck = ck_ref[...]
        conv = ck[2:3, :] * u + ck[1:2, :] * _shift_rows(u, halo, 1) + ck[0:1, :] * _shift_rows(u, halo, 2)
        return b_ref[0].astype(F32) * conv
    return pro


def conv_bwd_mix(dz, bch, conv_k, seq_len, name):
    T = dz.shape[1]
    tm = _tile(seq_len, TM, 16)
    tps = seq_len // tm
    nt = T // tm
    hb = tm // HALO

    def body(dz_ref, b_ref, c_ref, h_ref, cp_ref, hp_ref, dzn_ref, bn_ref, ck_ref, dbch_ref, dk_ref):
        i = pl.program_id(0)
        first = (i % tps) == 0
        last = (i % tps) == tps - 1
        b, c, h = b_ref[0].astype(F32), c_ref[0].astype(F32), h_ref[0].astype(F32)
        dzt = dz_ref[0].astype(F32)
        u = c * h
        prev = jnp.where(first, 0.0, cp_ref[0].astype(F32) * hp_ref[0].astype(F32))
        u1, u2 = _shift_rows(u, prev, 1), _shift_rows(u, prev, 2)
        ck = ck_ref[...]
        conv = ck[2:3, :] * u + ck[1:2, :] * u1 + ck[0:1, :] * u2
        dconv = dzt * b
        nxt = jnp.where(last, 0.0, dzn_ref[0].astype(F32) * bn_ref[0].astype(F32))
        du = ck[2:3, :] * dconv + ck[1:2, :] * _advance_rows(dconv, nxt, 1) + ck[0:1, :] * _advance_rows(dconv, nxt, 2)
        dbch_ref[0] = (dzt * conv).astype(BF16)
        dbch_ref[1] = (du * h).astype(BF16)
        dbch_ref[2] = (du * c).astype(BF16)
        tap = lax.broadcasted_iota(jnp.int32, (3, D), 0)
        dk = jnp.where(tap == 0, jnp.sum(dconv * u2, axis=0, keepdims=True),
                       jnp.where(tap == 1, jnp.sum(dconv * u1, axis=0, keepdims=True),
                                 jnp.sum(dconv * u, axis=0, keepdims=True)))
        _accumulate(dk_ref, i == 0, dk)

    def piece(p):
        return pl.BlockSpec((1, tm, D), lambda i, p=p: (p, i, 0))

    def prev(p):
        return pl.BlockSpec((1, HALO, D), lambda i, p=p: (p, jnp.maximum(i * hb - 1, 0), 0))

    def nxt(p):
        return pl.BlockSpec((1, HALO, D), lambda i, p=p: (p, jnp.minimum((i + 1) * hb, nt * hb - 1), 0))

    return pl.pallas_call(
        body, name=name, grid=(nt,),
        in_specs=[piece(0), piece(0), piece(1), piece(2), prev(1), prev(2), nxt(0), nxt(0),
                  pl.BlockSpec((3, D), lambda i: (0, 0))],
        out_specs=[pl.BlockSpec((3, tm, D), lambda i: (0, i, 0)), pl.BlockSpec((3, D), lambda i: (0, 0))],
        out_shape=[jax.ShapeDtypeStruct((3, T, D), BF16), jax.ShapeDtypeStruct((3, D), F32)],
        compiler_params=_params(1),
    )(dz, bch, bch, bch, bch, bch, dz, bch, conv_k)


def _log_sigmoid(x):
    return jnp.minimum(x, 0.0) - jnp.log(1.0 + jnp.exp(-jnp.abs(x)))


def forget_fwd(fl, fb, seq_len, name):
    T = fl.shape[1]

    def body(fl_ref, fb_ref, c_ref):
        c = _log_sigmoid(fl_ref[0] + fb_ref[...])
        row = lax.broadcasted_iota(jnp.int32, c.shape, 0)
        k = 1
        while k < seq_len:
            c = c + jnp.where(row >= k, pltpu.roll(c, k, 0), 0.0)
            k *= 2
        c_ref[...] = c

    return pl.pallas_call(
        body, name=name, grid=(T // seq_len,),
        in_specs=[pl.BlockSpec((1, seq_len, LANES), lambda b: (0, b, 0)), pl.BlockSpec((1, LANES), lambda b: (0, 0))],
        out_specs=pl.BlockSpec((seq_len, LANES), lambda b: (b, 0)),
        out_shape=jax.ShapeDtypeStruct((T, LANES), F32), compiler_params=_params(1),
    )(fl, fb)


def forget_bwd(dc, fl, fb, seq_len, name):
    T = dc.shape[0]

    def body(dc_ref, fl_ref, fb_ref, dfl_ref, dfb_ref):
        b = pl.program_id(0)
        r = dc_ref[...]
        row = lax.broadcasted_iota(jnp.int32, r.shape, 0)
        k = 1
        while k < seq_len:
            r = r + jnp.where(row < seq_len - k, pltpu.roll(r, seq_len - k, 0), 0.0)
            k *= 2
        dfl = r * jax.nn.sigmoid(-(fl_ref[0] + fb_ref[...]))
        dfl_ref[0] = dfl
        _accumulate(dfb_ref, b == 0, jnp.sum(dfl, axis=0, keepdims=True))

    return pl.pallas_call(
        body, name=name, grid=(T // seq_len,),
        in_specs=[pl.BlockSpec((seq_len, LANES), lambda b: (b, 0)), pl.BlockSpec((1, seq_len, LANES), lambda b: (0, b, 0)),
                  pl.BlockSpec((1, LANES), lambda b: (0, 0))],
        out_specs=[pl.BlockSpec((1, seq_len, LANES), lambda b: (0, b, 0)), pl.BlockSpec((1, LANES), lambda b: (0, 0))],
        out_shape=[jax.ShapeDtypeStruct((1, T, LANES), F32), jax.ShapeDtypeStruct((1, LANES), F32)],
        compiler_params=_params(1),
    )(dc, fl, fb)


HEADS = (slice(0, HEAD_DIM), slice(HEAD_DIM, 2 * HEAD_DIM))


def attn_fwd(qg, kv, crow, n_seq, seq_len, name):
    T = n_seq * seq_len
    tq = _tile(seq_len, TQ, LANES)
    nq = seq_len // tq

    def body(q_ref, k_ref, v_ref, cr_ref, o_ref, lse_ref):
        i = pl.program_id(2)
        causal = lax.broadcasted_iota(jnp.int32, (tq, tq), 1) <= lax.broadcasted_iota(jnp.int32, (tq, tq), 0)
        q8 = [q_ref[0, :, sl] * ATT_SCALE for sl in HEADS]

        def block(j, carry, diagonal):
            rows = pl.ds(pl.multiple_of(j * tq, tq), tq)
            out = []
            for hh, sl in enumerate(HEADS):
                m, l, acc = carry[3 * hh:3 * hh + 3]
                s = lax.dot_general(q8[hh], k_ref[0, rows, sl], NT, preferred_element_type=F32)
                s = s - cr_ref[0, 0, j, hh:hh + 1, :]
                if diagonal:
                    s = jnp.where(causal, s, -1e30)
                m_new = jnp.maximum(m, jnp.max(s, axis=-1, keepdims=True))
                a = jnp.exp(m - m_new)
                p = jnp.exp(s - m_new)
                l = a * l + jnp.sum(p, axis=-1, keepdims=True)
                acc = a * acc + jnp.dot(p.astype(BF16), v_ref[0, rows, sl], preferred_element_type=F32)
                out += [m_new, l, acc]
            return tuple(out)

        init = (jnp.full((tq, 1), -1e30, F32), jnp.zeros((tq, 1), F32), jnp.zeros((tq, HEAD_DIM), F32)) * 2
        carry = lax.fori_loop(0, i, lambda j, c: block(j, c, False), init)
        carry = block(i, carry, True)
        o_ref[...] = jnp.concatenate([carry[2] / carry[1], carry[5] / carry[4]], axis=-1)
        lse_ref[...] = jnp.concatenate([jnp.broadcast_to(carry[3 * hh] + jnp.log(carry[3 * hh + 1]), (tq, HEAD_DIM))
                                        for hh in range(2)], axis=-1)

    qspec = pl.BlockSpec((tq, LANES), lambda b, hp, i: (b * nq + i, hp))
    return pl.pallas_call(
        body, name=name, grid=(n_seq, N_HEADS // 2, nq),
        in_specs=[pl.BlockSpec((1, tq, LANES), lambda b, hp, i: (0, b * nq + i, hp)),
                  pl.BlockSpec((1, seq_len, LANES), lambda b, hp, i: (0, b, hp)),
                  pl.BlockSpec((1, seq_len, LANES), lambda b, hp, i: (1, b, hp)),
                  pl.BlockSpec((1, 1, nq, 2, tq), lambda b, hp, i: (b, hp, 0, 0, 0))],
        out_specs=[qspec, qspec],
        out_shape=[jax.ShapeDtypeStruct((T, D), F32), jax.ShapeDtypeStruct((T, D), F32)],
        compiler_params=_params(3),
    )(qg, kv, kv, crow)


def attn_bwd(dz, qg, kv, o, lse, crow, n_seq, seq_len, name):
    T = n_seq * seq_len
    tq = _tile(seq_len, TQ, LANES)
    nq = seq_len // tq

    def body(dz_ref, q_ref, gate_ref, o_ref, lse_ref, cr_ref, k_ref, v_ref,
             dq_ref, dgate_ref, dk_ref, dv_ref, dcr_ref, p_s, dp_s, dk_s, dv_s):
        i = pl.program_id(2)

        @pl.when(i == 0)
        def _():
            dk_s[...] = jnp.zeros_like(dk_s)
            dv_s[...] = jnp.zeros_like(dv_s)
            dcr_ref[...] = jnp.zeros_like(dcr_ref)

        dzf = dz_ref[...].astype(F32)
        sig = jax.nn.sigmoid(gate_ref[0].astype(F32))
        dob = (dzf * sig).astype(BF16)
        dgate_ref[0] = (dzf * o_ref[...] * sig * (1.0 - sig)).astype(BF16)
        causal = lax.broadcasted_iota(jnp.int32, (tq, tq), 1) <= lax.broadcasted_iota(jnp.int32, (tq, tq), 0)
        q8 = [q_ref[0, :, sl] * ATT_SCALE for sl in HEADS]
        do = [dob[:, sl] for sl in HEADS]
        lse_i = [lse_ref[:, sl.start:sl.start + 1] for sl in HEADS]

        def probs(j, dsums, diagonal):
            rows = pl.ds(pl.multiple_of(j * tq, tq), tq)
            out = []
            for hh, sl in enumerate(HEADS):
                s = lax.dot_general(q8[hh], k_ref[0, rows, sl], NT, preferred_element_type=F32)
                p = jnp.exp(s - cr_ref[0, 0, j, hh:hh + 1, :] - lse_i[hh])
                if diagonal:
                    p = jnp.where(causal, p, 0.0)
                dp = lax.dot_general(do[hh], v_ref[0, rows, sl], NT, preferred_element_type=F32)
                p_s[hh, j] = p
                dp_s[hh, j] = dp
                out.append(dsums[hh] + jnp.sum(p * dp, axis=-1, keepdims=True))
            return tuple(out)

        dsums = lax.fori_loop(0, i, lambda j, c: probs(j, c, False), (jnp.zeros((tq, 1), F32),) * 2)
        dsums = probs(i, dsums, True)

        def grads(j, dqs):
            rows = pl.ds(pl.multiple_of(j * tq, tq), tq)
            out = []
            for hh, sl in enumerate(HEADS):
                p = p_s[hh, j]
                ds = p * (dp_s[hh, j] - dsums[hh])
                dcr_ref[0, 0, j, hh:hh + 1, :] -= jnp.sum(ds, axis=0, keepdims=True)
                dsb = ds.astype(BF16)
                dk_s[rows, sl] += lax.dot_general(dsb, q8[hh], TN, preferred_element_type=F32)
                dv_s[rows, sl] += lax.dot_general(p.astype(BF16), do[hh], TN, preferred_element_type=F32)
                out.append(dqs[hh] + jnp.dot(dsb, k_ref[0, rows, sl], preferred_element_type=F32))
            return tuple(out)

        dqs = lax.fori_loop(0, i + 1, grads, (jnp.zeros((tq, HEAD_DIM), F32),) * 2)
        dq_ref[0] = (jnp.concatenate(dqs, axis=-1) * ATT_SCALE).astype(BF16)

        @pl.when(i == nq - 1)
        def _():
            dk_ref[0] = dk_s[...].astype(BF16)
            dv_ref[0] = dv_s[...].astype(BF16)

    qry2 = pl.BlockSpec((tq, LANES), lambda b, hp, i: (b * nq + i, hp))

    def qry3(p):
        return pl.BlockSpec((1, tq, LANES), lambda b, hp, i, p=p: (p, b * nq + i, hp))

    def seq3(p):
        return pl.BlockSpec((1, seq_len, LANES), lambda b, hp, i, p=p: (p, b, hp))

    crs = pl.BlockSpec((1, 1, nq, 2, tq), lambda b, hp, i: (b, hp, 0, 0, 0))
    act = jax.ShapeDtypeStruct((1, T, D), BF16)
    return pl.pallas_call(
        body, name=name, grid=(n_seq, N_HEADS // 2, nq),
        in_specs=[qry2, qry3(0), qry3(1), qry2, qry2, crs, seq3(0), seq3(1)],
        out_specs=[qry3(0), qry3(0), seq3(0), seq3(0), crs],
        out_shape=[act, act, act, act, jax.ShapeDtypeStruct((n_seq, N_HEADS // 2, nq, 2, tq), F32)],
        scratch_shapes=[pltpu.VMEM((2, nq, tq, tq), F32), pltpu.VMEM((2, nq, tq, tq), F32),
                        pltpu.VMEM((seq_len, LANES), F32), pltpu.VMEM((seq_len, LANES), F32)],
        compiler_params=_params(3),
    )(dz, qg, qg, o, lse, crow, kv, kv)


def loss_head(y, target, name):
    T = y.shape[0]
    tm = _tile(T, TM, 8)

    def body(y_ref, t_ref, dy_ref, loss_ref):
        err = y_ref[...] - t_ref[...]
        dy_ref[...] = err * (1.0 / D)
        part = 0.5 * jnp.sum(jnp.mean(err * err, axis=-1, keepdims=True), axis=0, keepdims=True)
        _accumulate(loss_ref, pl.program_id(0) == 0, jnp.broadcast_to(part, (1, LANES)))

    row = pl.BlockSpec((tm, D), lambda i: (i, 0))
    return pl.pallas_call(
        body, name=name, grid=(T // tm,), in_specs=[row, row],
        out_specs=[row, pl.BlockSpec((1, LANES), lambda i: (0, 0))],
        out_shape=[jax.ShapeDtypeStruct((T, D), F32), jax.ShapeDtypeStruct((1, LANES), F32)],
        compiler_params=_params(1),
    )(y, target)


def _adamw(w, g, m, v):
    m = ADAM_B1 * m + (1.0 - ADAM_B1) * g
    v = ADAM_B2 * v + (1.0 - ADAM_B2) * (g * g)
    m_hat = m / (1.0 - ADAM_B1 ** ADAM_STEP)
    v_hat = v / (1.0 - ADAM_B2 ** ADAM_STEP)
    delta = -ADAM_LR * (m_hat / (jnp.sqrt(v_hat) + ADAM_EPS) + ADAM_WD * w)
    return delta, m, v


def adam_sharded(w, m, v, contribs, name):
    L, R, C = w.shape
    tr = _tile(R, 256, 16) if R % 16 == 0 else R

    def body(w_ref, m_ref, v_ref, *refs):
        c_refs, (g_ref, d_ref, nm_ref, nv_ref) = refs[:2 * L], refs[2 * L:]
        l = pl.program_id(0)
        for j in range(L):
            @pl.when(l == j)
            def _(j=j):
                own_ref, recv_ref = c_refs[2 * j], c_refs[2 * j + 1]
                g = own_ref[...].astype(F32)
                for k in range(N_DEV - 1):
                    g = g + recv_ref[k].astype(F32)
                delta, nm, nv = _adamw(w_ref[0], g, m_ref[0], v_ref[0])
                g_ref[0] = g
                d_ref[0] = delta
                nm_ref[0] = nm
                nv_ref[0] = nv

    blk = pl.BlockSpec((1, tr, C), lambda l, i: (l, i, 0))
    in_specs = [blk, blk, blk]
    inputs = [w, m, v]
    for j, (own, recv) in enumerate(contribs):
        in_specs.append(pl.BlockSpec((tr, C), lambda l, i, j=j: (jnp.where(l == j, i, 0), 0)))
        in_specs.append(pl.BlockSpec((N_DEV - 1, tr, C), lambda l, i, j=j: (0, jnp.where(l == j, i, 0), 0)))
        inputs += [own, recv]
    shp = jax.ShapeDtypeStruct((L, R, C), F32)
    return pl.pallas_call(
        body, name=name, grid=(L, R // tr), in_specs=in_specs, out_specs=[blk] * 4, out_shape=[shp] * 4,
        compiler_params=_params(2),
    )(*inputs)


def adam_small(params, total, extra_grads, name):
    n, ne = len(params), len(extra_grads)

    def body(*refs):
        total_ref, extra_refs = refs[0], refs[1:1 + ne]
        ins, outs = refs[1 + ne:1 + ne + 3 * n], refs[1 + ne + 3 * n:]
        for k, (_, _, _, where) in enumerate(params):
            if isinstance(where, int):
                g = extra_refs[where][...]
            else:
                row, rows, width = where
                g = total_ref[row:row + rows, 0:width]
            delta, nm, nv = _adamw(ins[3 * k][...], g, ins[3 * k + 1][...], ins[3 * k + 2][...])
            outs[4 * k][...] = g
            outs[4 * k + 1][...] = delta
            outs[4 * k + 2][...] = nm
            outs[4 * k + 3][...] = nv

    flat = [a for w, m, v, _ in params for a in (w, m, v)]
    out_shape = [jax.ShapeDtypeStruct(w.shape, F32) for w, _, _, _ in params for _ in range(4)]
    res = pl.pallas_call(body, name=name, out_shape=out_shape)(total, *extra_grads, *flat)
    return [res[4 * k:4 * k + 4] for k in range(n)]


def _place():
    return lax.axis_index("x"), lax.axis_index("y"), lax.axis_index("c")


def _peer(place, k):
    x, y, c = place
    return x ^ (k >> 2), y ^ ((k >> 1) & 1), c ^ (k & 1)


ANY = pl.BlockSpec(memory_space=pl.ANY)
HBM = pl.BlockSpec(memory_space=pltpu.HBM)
SEM = pl.BlockSpec(memory_space=pltpu.SEMAPHORE)
EFFECT = pltpu.SideEffectType.DATAFLOW_SIDE_EFFECTING


def _in_hbm(a):
    return pltpu.with_memory_space_constraint(a, pltpu.HBM)


def _number(place):
    return 4 * place[0] + 2 * place[1] + place[2]


SLOTS = {"gather_like": 4, "gather_pass": 3, "scatter": 7}


def _plan(mode, src_ref, land_ref, place):
    if mode == "gather_like":
        return [(src_ref, land_ref.at[_number(place)], _peer(place, k)) for k in (1, 2, 4, 6)]
    if mode == "gather_pass":
        slots = [land_ref.at[_number(_peer(place, k))] for k in (2, 4, 6)]
        return [(slot, slot, _peer(place, 1)) for slot in slots]
    return [(src_ref.at[_number(_peer(place, k))], land_ref.at[k - 1], _peer(place, k)) for k in range(1, N_DEV)]


def _exchange(name, groups, start, afters):
    sizes = [len(g[2]) for g in groups]
    na, ng = sum(sizes), len(groups)
    waits = groups[0][0] is not None
    n_in_sems = 2 * ng if waits else 0
    n_out_sems = 2 * ng if start else 0

    def body(*refs):
        src_refs, land_refs = refs[:na], refs[na:2 * na]
        in_sems = refs[2 * na:2 * na + n_in_sems]
        outs = refs[2 * na + n_in_sems + len(afters):]
        place = _place()
        a = 0
        for gi, n in enumerate(sizes):
            for idx in range(n):
                if waits:
                    zone = land_refs[a].at[pl.ds(0, groups[gi][4])]
                    copy = pltpu.make_async_remote_copy(
                        src_ref=zone, dst_ref=zone, send_sem=in_sems[2 * gi].at[idx], recv_sem=in_sems[2 * gi + 1].at[idx],
                        device_id=_peer(place, 1), device_id_type=MESH)
                    copy.wait_send()
                    copy.wait_recv()
                if start:
                    for src, dst, peer in _plan(start, src_refs[a], land_refs[a], place):
                        pltpu.make_async_remote_copy(
                            src_ref=src, dst_ref=dst, send_sem=outs[2 * gi].at[idx], recv_sem=outs[2 * gi + 1].at[idx],
                            device_id=peer, device_id_type=MESH).start()
                a += 1
        if start:
            outs[-1][...] = jnp.zeros_like(outs[-1])

    srcs = [_in_hbm(s) for g in groups for s in g[2]]
    lands = [_in_hbm(l) for g in groups for l in g[3]]
    sems = [s for g in groups for s in g[:2]] if waits else []
    out_shape = [pltpu.SemaphoreType.DMA((n,)) for n in sizes for _ in range(2)] if start else []
    out_shape += [pltpu.HBM(a.shape, a.dtype) for a in srcs + lands]
    out_specs = [SEM] * n_out_sems + [HBM] * (2 * na)
    if start:
        out_shape.append(jax.ShapeDtypeStruct((8, LANES), F32))
        out_specs.append(pl.BlockSpec(memory_space=pltpu.VMEM))
    res = pl.pallas_call(
        body, name=name, in_specs=[HBM] * (2 * na) + [SEM] * n_in_sems + [ANY] * len(afters),
        out_shape=out_shape, out_specs=out_specs,
        input_output_aliases={i: n_out_sems + i for i in range(2 * na)},
        compiler_params=pltpu.CompilerParams(has_side_effects=EFFECT),
    )(*srcs, *lands, *sems, *afters)
    new_sems, thru = res[:n_out_sems], res[n_out_sems:n_out_sems + 2 * na]
    out, a = [], 0
    for gi, n in enumerate(sizes):
        pair = (new_sems[2 * gi], new_sems[2 * gi + 1]) if start else (None, None)
        out.append(pair + (thru[a:a + n], thru[na + a:na + a + n], SLOTS.get(start, 0)))
        a += n
    return out, (res[-1] if start else None)


def exchange_start(pair_groups, mode, name):
    groups = [(None, None, [s for s, _ in g], [l for _, l in g], 0) for g in pair_groups]
    return _exchange(name, groups, mode, ())


def exchange_relay(groups, mode, afters, name):
    return _exchange(name, groups, mode, afters)


def exchange_wait(groups, afters, name):
    done, _ = _exchange(name, groups, None, afters)
    return [(g[2], g[3]) for g in done]


def all_reduce_small(parts, n_rows, name):
    R = n_rows
    n_parts = len(parts)

    def body(*refs):
        part_refs = refs[:n_parts]
        out_ref, buf, send_sems, recv_sems = refs[n_parts:]
        x, y, c = _place()
        me = 4 * x + 2 * y + c
        own = buf.at[me]
        own[...] = jnp.zeros((R, D), F32)
        for ref, (arr, row) in zip(part_refs, parts):
            own[row:row + arr.shape[0], 0:arr.shape[1]] = ref[...]
        copies = []
        for k in range(1, N_DEV):
            peer = (x ^ (k >> 2), y ^ ((k >> 1) & 1), c ^ (k & 1))
            copies.append(pltpu.make_async_remote_copy(
                src_ref=own, dst_ref=own, send_sem=send_sems.at[k - 1], recv_sem=recv_sems.at[k - 1],
                device_id=peer, device_id_type=MESH))
        for cp in copies:
            cp.start()
        for cp in copies:
            cp.wait()
        total = buf[0]
        for d in range(1, N_DEV):
            total = total + buf[d]
        out_ref[...] = total

    vm = pl.BlockSpec(memory_space=pltpu.VMEM)
    return pl.pallas_call(
        body, name=name, in_specs=[vm] * n_parts, out_specs=vm, out_shape=jax.ShapeDtypeStruct((R, D), F32),
        scratch_shapes=[pltpu.VMEM((N_DEV, R, D), F32), pltpu.SemaphoreType.DMA((N_DEV - 1,)),
                        pltpu.SemaphoreType.DMA((N_DEV - 1,))],
    )(*[arr for arr, _ in parts])


def _col_blocks(gathered, n_blocks):
    n, d, w = gathered.shape
    whole = gathered.transpose(1, 0, 2).reshape(d, n * w)
    return whole.reshape(d, n_blocks, n * w // n_blocks).transpose(1, 0, 2)[:, None]


def _col_shards(blocks):
    n, d, w = blocks.shape
    whole = blocks.transpose(1, 0, 2).reshape(d, n * w)
    return whole.reshape(d, N_DEV, n * w // N_DEV).transpose(1, 0, 2)


def kernel(x, ffn1_pre_g, ffn1_post_g, ffn1_w_in, ffn1_w_out, mix_pre_g, mix_post_g, ffn2_pre_g, ffn2_post_g, ffn2_w_in, ffn2_w_out, conv_w_in, conv_k, conv_w_out, kv_g, kv_w, forget_b, attn_w_qg, attn_w_o, loss_target, m_ffn1_pre_g, m_ffn1_post_g, m_ffn1_w_in, m_ffn1_w_out, m_mix_pre_g, m_mix_post_g, m_ffn2_pre_g, m_ffn2_post_g, m_ffn2_w_in, m_ffn2_w_out, m_conv_w_in, m_conv_k, m_conv_w_out, m_kv_g, m_kv_w, m_forget_b, m_attn_w_qg, m_attn_w_o, v_ffn1_pre_g, v_ffn1_post_g, v_ffn1_w_in, v_ffn1_w_out, v_mix_pre_g, v_mix_post_g, v_ffn2_pre_g, v_ffn2_post_g, v_ffn2_w_in, v_ffn2_w_out, v_conv_w_in, v_conv_k, v_conv_w_out, v_kv_g, v_kv_w, v_forget_b, v_attn_w_qg, v_attn_w_o):
    n_seq, seq_len, _ = x.shape
    T = n_seq * seq_len
    xi, yi, ci = _place()
    dev = 4 * xi + 2 * yi + ci
    x0 = x.reshape(T, D)
    target = loss_target.reshape(T, D)

    def layer(w, l):
        return w[l:l + 1].astype(BF16)

    shard_groups = [
        [layer(ffn1_w_in, 0), layer(ffn1_w_out, 0)],
        [conv_w_in.astype(BF16), conv_w_out.astype(BF16), conv_k],
        [layer(ffn2_w_in, 0), layer(ffn2_w_out, 0)],
        [kv_w.astype(BF16), layer(ffn1_w_in, 1), layer(ffn1_w_out, 1), attn_w_qg.astype(BF16), attn_w_o.astype(BF16)],
        [layer(ffn2_w_in, 1), layer(ffn2_w_out, 1)]]
    def zone(s):
        return lax.dynamic_update_slice(lax.empty((N_DEV,) + s.shape, s.dtype), s[None], (dev,) + (0,) * s.ndim)

    gathers, token = exchange_start([[(s, zone(s)) for s in g] for g in shard_groups], "gather_like", "gather_start")

    def gathered(k, after):
        passed, _ = exchange_relay([gathers[k]], "gather_pass", [after], f"gather_pass{k}")
        return exchange_wait(passed, [after], f"gather_wait{k}")[0][1]

    fb = jnp.pad(forget_b, (0, LANES - N_HEADS))[None]

    def vec(g, l):
        return g[l:l + 1]

    def behind(g, tok):
        return g + tok[:1, :1]

    grads_small = {}

    def ffn_fwd(xin, g_pre, g_post, w_in, w_out, tag):
        xn, (gu,) = rms_proj(xin, g_pre, [(w_in, 0)], [BF16], f"{tag}_in")
        tm = _tile(T, TM, 16)
        fbw = gu.shape[2]
        specs = [pl.BlockSpec((1, tm, fbw), lambda i, k: (k, i, 0)), pl.BlockSpec((1, tm, fbw), lambda i, k: (k + 4, i, 0))]
        a, h, y = mix_out(_swiglu_pro, [gu, gu], specs, (w_out, 0), xin, g_post, 0.5, 4, f"{tag}_out")
        return y, (xin, xn, gu, a, h)

    def ffn_bwd(dy, saved, g_pre, g_post, w_in, w_out, tag):
        xin, xn, gu, a, h = saved
        dh, dg_post, dgate, dup = post_bwd(dy, h, g_post, 0.5, (w_out, 0), 4, f"{tag}_bwd_out", gu=gu)
        dw_out = wgrad([a], [dh[None]], 4, w_out.shape, f"{tag}_dw_out")
        dw_in = wgrad([dgate, dup], [xn[None]], 8, (8, 1, w_in.shape[3], D), f"{tag}_dw_in")
        started, tok = scatter_start([dw_in.reshape(8, -1, D), dw_out.reshape(8, -1, D)], f"{tag}_scatter_start")
        dx, dg_pre = pre_bwd([(dgate, (w_in, 0), 0), (dup, (w_in, 0), 4)], xin, behind(g_pre, tok), dy, f"{tag}_bwd_in")
        return dx, dg_pre, dg_post, started

    def scatter_start(blocked, name):
        pairs = [(g, lax.empty((N_DEV - 1,) + g.shape[1:], g.dtype)) for g in blocked]
        started, tok = exchange_start([pairs], "scatter", name)
        return started[0], tok

    w1_in, w1_out = gathered(0, token)
    x1, s_f1a = ffn_fwd(x0, vec(ffn1_pre_g, 0), vec(ffn1_post_g, 0), w1_in, w1_out, "l0_ffn1")
    cw_in_g, cw_out, ck_g = gathered(1, x1)
    cw_in = _col_blocks(cw_in_g[:, 0], 3)
    ck = ck_g[:, 0].transpose(1, 0, 2).reshape(3, D)
    xn_c, (bch,) = rms_proj(x1, vec(mix_pre_g, 0), [(cw_in, 0)], [BF16], "conv_in")
    tmc = _tile(seq_len, TM, 16)
    hb = tmc // HALO

    def cpiece(p):
        return pl.BlockSpec((1, tmc, D), lambda i, k, p=p: (p, i, 0))

    def chalo(p):
        return pl.BlockSpec((1, HALO, D), lambda i, k, p=p: (p, jnp.maximum(i * hb - 1, 0), 0))

    conv_specs = [cpiece(0), cpiece(1), cpiece(2), chalo(1), chalo(2), pl.BlockSpec((3, D), lambda i, k: (0, 0))]
    z_c, m_c, x2 = mix_out(_make_conv_pro(seq_len // tmc), [bch, bch, bch, bch, bch, ck], conv_specs, (cw_out, 0),
                           x1, vec(mix_post_g, 0), 1.0, 1, "conv_out", tm=tmc)
    w2_in, w2_out = gathered(2, x2)
    x3, s_f2a = ffn_fwd(x2, vec(ffn2_pre_g, 0), vec(ffn2_post_g, 0), w2_in, w2_out, "l0_ffn2")

    kvw_g, w1_in_b, w1_out_b, qgw_g, ow = gathered(3, x3)
    qg_w = _col_blocks(qgw_g[:, 0], 2)
    kv_whole = kvw_g.transpose(1, 0, 2).reshape(D, 2 * D + N_HEADS)
    kv_wb = kv_whole[:, :2 * D].reshape(D, 2, D).transpose(1, 0, 2)[:, None]
    f_w = jnp.pad(kv_whole[:, 2 * D:], ((0, 0), (0, LANES - N_HEADS)))[None, None]
    xn_kv, (kv, fl) = rms_proj(x3, kv_g[None], [(kv_wb, 0), (f_w, 0)], [BF16, F32], "kv_in")
    c128 = forget_fwd(fl, fb, seq_len, "forget_fwd")
    c16 = c128[:, :N_HEADS]
    tq = _tile(seq_len, TQ, LANES)
    crow = c16.reshape(n_seq, seq_len // tq, tq, N_HEADS // 2, 2).transpose(0, 3, 1, 4, 2)

    x4, s_f1b = ffn_fwd(x3, vec(ffn1_pre_g, 1), vec(ffn1_post_g, 1), w1_in_b, w1_out_b, "l1_ffn1")
    xn_a, (qg,) = rms_proj(x4, vec(mix_pre_g, 1), [(qg_w, 0)], [BF16], "attn_in")
    o, lse = attn_fwd(qg, kv, crow, n_seq, seq_len, "attn_fwd")
    tm = _tile(T, TM, 16)
    gate_specs = [pl.BlockSpec((1, tm, D), lambda i, k: (1, i, 0)), pl.BlockSpec((tm, D), lambda i, k: (i, 0))]
    z_a, m_a, x5 = mix_out(_attn_gate_pro, [qg, o], gate_specs, (ow, 0), x4, vec(mix_post_g, 1), 1.0, 1, "attn_out")
    w2_in_b, w2_out_b = gathered(4, x5)
    x6, s_f2b = ffn_fwd(x5, vec(ffn2_pre_g, 1), vec(ffn2_post_g, 1), w2_in_b, w2_out_b, "l1_ffn2")

    dy, loss_part = loss_head(x6, target, "loss_head")

    scatters = {}
    dx5, dg, dgp, scatters["ffn2", 1] = ffn_bwd(dy, s_f2b, vec(ffn2_pre_g, 1), vec(ffn2_post_g, 1), w2_in_b, w2_out_b, "l1_ffn2")
    grads_small["ffn2_pre", 1], grads_small["ffn2_post", 1] = dg, dgp
    dm_a, dgp, dz_a = post_bwd(dx5, m_a, vec(mix_post_g, 1), 1.0, (ow, 0), 1, "attn_bwd_out")
    grads_small["mix_post", 1] = dgp
    dq, dgate, dk, dv, dcrow = attn_bwd(dz_a[0], qg, kv, o, lse, crow, n_seq, seq_len, "attn_bwd")
    dx4, dg = pre_bwd([(dq, (qg_w, 0), 0), (dgate, (qg_w, 0), 1)], x4, vec(mix_pre_g, 1), dx5, "attn_bwd_in")
    grads_small["mix_pre", 1] = dg
    d_ow = wgrad([z_a], [dm_a[None]], 1, ow.shape, "attn_dw_o")
    d_qgw = wgrad([xn_a[None]], [dq, dgate], 2, (2, 1, D, D), "attn_dw_qg")
    scatters["attn"], tok = scatter_start([_col_shards(d_qgw[:, 0]), d_ow.reshape(8, -1, D)], "attn_scatter_start")
    dx3, dg, dgp, scatters["ffn1", 1] = ffn_bwd(dx4, s_f1b, vec(ffn1_pre_g, 1), behind(vec(ffn1_post_g, 1), tok),
                                                w1_in_b, w1_out_b, "l1_ffn1")
    grads_small["ffn1_pre", 1], grads_small["ffn1_post", 1] = dg, dgp

    dc16 = dcrow.transpose(0, 2, 4, 1, 3).reshape(T, N_HEADS)
    dfl, dfb = forget_bwd(jnp.pad(dc16, ((0, 0), (0, LANES - N_HEADS))), fl, fb, seq_len, "forget_bwd")
    dx3, dg_kv = pre_bwd([(dk, (kv_wb, 0), 0), (dv, (kv_wb, 0), 1), (dfl, (f_w, 0), 0)], x3, kv_g[None], dx3, "kv_bwd_in")
    d_kvw = wgrad([xn_kv[None]], [dk, dv], 2, (2, 1, D, D), "kv_dw")
    d_fw = wgrad([xn_kv[None]], [dfl.astype(BF16)], 1, (1, 1, D, LANES), "forget_dw")
    d_kv_whole = jnp.concatenate([d_kvw[0, 0], d_kvw[1, 0], d_fw[0, 0, :, :N_HEADS]], axis=1)
    wshard = D * 2 + N_HEADS
    scatters["kv"], tok = scatter_start([d_kv_whole.T.reshape(N_DEV, wshard // N_DEV, D)], "kv_scatter_start")

    dx2, dg, dgp, scatters["ffn2", 0] = ffn_bwd(dx3, s_f2a, vec(ffn2_pre_g, 0), behind(vec(ffn2_post_g, 0), tok),
                                                w2_in, w2_out, "l0_ffn2")
    grads_small["ffn2_pre", 0], grads_small["ffn2_post", 0] = dg, dgp
    dm_c, dgp, dz_c = post_bwd(dx2, m_c, vec(mix_post_g, 0), 1.0, (cw_out, 0), 1, "conv_bwd_out")
    grads_small["mix_post", 0] = dgp
    dbch, d_ck = conv_bwd_mix(dz_c, bch, ck, seq_len, "conv_bwd_mix")
    dx1, dg = pre_bwd([(dbch, (cw_in, 0), 0)], x1, vec(mix_pre_g, 0), dx2, "conv_bwd_in")
    grads_small["mix_pre", 0] = dg
    d_cw_out = wgrad([z_c], [dm_c[None]], 1, cw_out.shape, "conv_dw_out")
    d_cw_in = wgrad([xn_c[None]], [dbch], 3, (3, 1, D, D), "conv_dw_in")
    scatters["conv"], tok = scatter_start([_col_shards(d_cw_in[:, 0]), d_cw_out.reshape(8, -1, D)], "conv_scatter_start")
    dx0, dg, dgp, scatters["ffn1", 0] = ffn_bwd(dx1, s_f1a, vec(ffn1_pre_g, 0), behind(vec(ffn1_post_g, 0), tok),
                                                w1_in, w1_out, "l0_ffn1")
    grads_small["ffn1_pre", 0], grads_small["ffn1_post", 0] = dg, dgp

    parts_of = {}

    def scatter_end(keys, afters, name):
        for k, (sent, recv) in zip(keys, exchange_wait([scatters[k] for k in keys], afters, name)):
            parts_of[k] = [(lax.dynamic_index_in_dim(g, dev, 0, keepdims=False), r) for g, r in zip(sent, recv)]

    scatter_end([("ffn2", 1), "attn", ("ffn1", 1), "kv", ("ffn2", 0), "conv"], [dx0], "scatter_wait")
    sharded = {"ffn2_w_in": (ffn2_w_in, m_ffn2_w_in, v_ffn2_w_in), "ffn2_w_out": (ffn2_w_out, m_ffn2_w_out, v_ffn2_w_out),
               "conv_w_in": (conv_w_in, m_conv_w_in, v_conv_w_in), "conv_w_out": (conv_w_out, m_conv_w_out, v_conv_w_out),
               "kv_w": (kv_w, m_kv_w, v_kv_w), "attn_w_qg": (attn_w_qg, m_attn_w_qg, v_attn_w_qg),
               "attn_w_o": (attn_w_o, m_attn_w_o, v_attn_w_o),
               "ffn1_w_in": (ffn1_w_in, m_ffn1_w_in, v_ffn1_w_in), "ffn1_w_out": (ffn1_w_out, m_ffn1_w_out, v_ffn1_w_out)}
    out = {}
    for nm, (w, mm, vv) in sharded.items():
        if nm == "ffn1_w_in":
            scatter_end([("ffn1", 0)], [res[0] for res in out.values()], "scatter_wait_last")
        contribs = {
            "ffn1_w_in": lambda: [parts_of["ffn1", 0][0], parts_of["ffn1", 1][0]],
            "ffn1_w_out": lambda: [parts_of["ffn1", 0][1], parts_of["ffn1", 1][1]],
            "ffn2_w_in": lambda: [parts_of["ffn2", 0][0], parts_of["ffn2", 1][0]],
            "ffn2_w_out": lambda: [parts_of["ffn2", 0][1], parts_of["ffn2", 1][1]],
            "conv_w_in": lambda: [parts_of["conv"][0]], "conv_w_out": lambda: [parts_of["conv"][1]],
            "kv_w": lambda: [parts_of["kv"][0]], "attn_w_qg": lambda: [parts_of["attn"][0]],
            "attn_w_o": lambda: [parts_of["attn"][1]]}[nm]()
        if nm in ("ffn1_w_in", "ffn2_w_in", "kv_w"):
            rows, cols = w.shape[-2:]

            def view(a):
                return a.reshape(-1, rows, cols).transpose(0, 2, 1)

            res = adam_sharded(view(w), view(mm), view(vv), contribs, f"adam_{nm}")
            out[nm] = [r.transpose(0, 2, 1).reshape(w.shape) for r in res]
        else:
            shape3 = (len(contribs),) + contribs[0][0].shape
            res = adam_sharded(w.reshape(shape3), mm.reshape(shape3), vv.reshape(shape3), contribs, f"adam_{nm}")
            out[nm] = [r.reshape(w.shape) for r in res]

    small_names = ["ffn1_pre", "ffn1_post", "mix_pre", "mix_post", "ffn2_pre", "ffn2_post"]
    parts = [(grads_small[n, l], 2 * k + l) for k, n in enumerate(small_names) for l in range(2)]
    parts += [(dg_kv, 12), (dfb, 13), (d_ck, 14), (loss_part, 17)]
    total = all_reduce_small(parts, 24, "all_reduce_small")
    loss = total[17, 0]
    d_ck_mine = lax.dynamic_slice(total, (14, dev * LANES), (3, LANES))
    gains = [(ffn1_pre_g, m_ffn1_pre_g, v_ffn1_pre_g), (ffn1_post_g, m_ffn1_post_g, v_ffn1_post_g),
             (mix_pre_g, m_mix_pre_g, v_mix_pre_g), (mix_post_g, m_mix_post_g, v_mix_post_g),
             (ffn2_pre_g, m_ffn2_pre_g, v_ffn2_pre_g), (ffn2_post_g, m_ffn2_post_g, v_ffn2_post_g)]
    small_params = [(w, m, v, (2 * k, 2, D)) for k, (w, m, v) in enumerate(gains)]
    small_params += [(kv_g[None], m_kv_g[None], v_kv_g[None], (12, 1, D)),
                     (forget_b[None], m_forget_b[None], v_forget_b[None], (13, 1, N_HEADS)),
                     (conv_k[0], m_conv_k[0], v_conv_k[0], 0)]
    small_res = adam_small(small_params, total, [d_ck_mine], "adam_small")
    small_keys = [n + "_g" for n in small_names] + ["kv_g", "forget_b", "conv_k"]
    shapes = {"kv_g": kv_g.shape, "forget_b": forget_b.shape, "conv_k": conv_k.shape}
    small = [{key: res[kind].reshape(shapes.get(key, res[kind].shape)) for key, res in zip(small_keys, small_res)}
             for kind in range(4)]
    order = ["ffn1_pre_g", "ffn1_post_g", "ffn1_w_in", "ffn1_w_out", "mix_pre_g", "mix_post_g", "ffn2_pre_g", "ffn2_post_g",
             "ffn2_w_in", "ffn2_w_out", "conv_w_in", "conv_k", "conv_w_out", "kv_g", "kv_w", "forget_b", "attn_w_qg",
             "attn_w_o"]
    results = [loss, dx0.reshape(x.shape)]
    for kind in range(4):
        for nm in order:
            results.append(out[nm][kind] if nm in out else small[kind][nm])
    return tuple(results)
```

```python
import functools
import math

import jax
import jax.numpy as jnp
from jax import lax
from jax.experimental import pallas as pl
from jax.experimental.pallas import tpu as pltpu

F32, BF16 = jnp.float32, jnp.bfloat16
D = 1024
N_HEADS = 16
HEAD_DIM = 64
N_DEV = 8
RMS_EPS = 1e-6
ATT_SCALE = 1.0 / math.sqrt(HEAD_DIM)
LANES = 128
HALO = 8
TM = 512
TQ = 256
VMEM_LIMIT = 48 * 1024 * 1024
MESH = pl.DeviceIdType.MESH

ADAM_LR, ADAM_B1, ADAM_B2, ADAM_EPS, ADAM_WD, ADAM_STEP = 0.001, 0.9, 0.999, 1e-08, 0.01, 10

NT = (((1,), (1,)), ((), ()))
TN = (((0,), (0,)), ((), ()))


def _params(n_axes):
    return pltpu.CompilerParams(dimension_semantics=("arbitrary",) * n_axes, vmem_limit_bytes=VMEM_LIMIT)


def _tile(n, cap, mult):
    best = None
    for t in range(mult, min(n, cap) + 1, mult):
        if n % t == 0:
            best = t
    assert best is not None, (n, cap, mult)
    return best


def _rms_rstd(x):
    return lax.rsqrt(jnp.mean(x * x, axis=-1, keepdims=True) + RMS_EPS)


def _rms_fwd(x, g):
    return x * _rms_rstd(x) * g


def _rms_bwd(x, g, dy):
    xh = x * _rms_rstd(x)
    dyg = dy * g
    dx = _rms_rstd(x) * (dyg - xh * jnp.mean(dyg * xh, axis=-1, keepdims=True))
    return dx, jnp.sum(dy * xh, axis=0, keepdims=True)


def _accumulate(ref, first, value):
    @pl.when(first)
    def _():
        ref[...] = value

    @pl.when(jnp.logical_not(first))
    def _():
        ref[...] += value


def rms_proj(x, g, ws, out_dtypes, name):
    T = x.shape[0]
    tm = _tile(T, TM, 16)
    na = len(ws)

    def body(x_ref, g_ref, *refs):
        w_refs, xn_ref, o_refs = refs[:na], refs[na], refs[na + 1:]
        xn = _rms_fwd(x_ref[...], g_ref[...]).astype(BF16)
        xn_ref[...] = xn
        for a, (w, l) in enumerate(ws):
            for p in range(w.shape[0]):
                o_refs[a][p] = jnp.dot(xn, w_refs[a][p, l], preferred_element_type=F32).astype(o_refs[a].dtype)

    in_specs = [pl.BlockSpec((tm, D), lambda i: (i, 0)), pl.BlockSpec((1, D), lambda i: (0, 0))]
    in_specs += [pl.BlockSpec(w.shape, lambda i: (0, 0, 0, 0), pipeline_mode=pl.Buffered(1)) for w, _ in ws]
    out_specs = [pl.BlockSpec((tm, D), lambda i: (i, 0))]
    out_shape = [jax.ShapeDtypeStruct((T, D), BF16)]
    for (w, _), dt in zip(ws, out_dtypes):
        nb, wb = w.shape[0], w.shape[3]
        out_specs.append(pl.BlockSpec((nb, tm, wb), lambda i: (0, i, 0)))
        out_shape.append(jax.ShapeDtypeStruct((nb, T, wb), dt))
    res = pl.pallas_call(
        body, name=name, grid=(T // tm,), in_specs=in_specs, out_specs=out_specs, out_shape=out_shape,
        compiler_params=_params(1),
    )(x, g, *[w for w, _ in ws])
    return res[0], res[1:]


def mix_out(pro, pro_inputs, pro_specs, w, res, g_post, alpha, nk, name, tm=None):
    w4, l = w
    T = res.shape[0]
    tm = _tile(T, TM, 16) if tm is None else tm
    dpb = N_DEV // nk
    rows = w4.shape[2]
    kb = dpb * rows
    npi = len(pro_inputs)

    def body(*refs):
        pro_refs = refs[:npi]
        w_ref, res_ref, g_ref, z_ref, m_ref, y_ref = refs[npi:]
        i = pl.program_id(0)
        m = None
        for k in range(nk):
            z = pro(i, k, *pro_refs).astype(BF16)
            z_ref[k] = z
            part = jnp.dot(z, w_ref[k * dpb:(k + 1) * dpb, l].reshape(kb, D), preferred_element_type=F32)
            m = part if m is None else m + part
        m_ref[...] = m
        y_ref[...] = res_ref[...] + alpha * _rms_fwd(m, g_ref[...])

    row = pl.BlockSpec((tm, D), lambda i: (i, 0))
    in_specs = list(pro_specs) + [
        pl.BlockSpec(w4.shape, lambda i: (0, 0, 0, 0), pipeline_mode=pl.Buffered(1)), row, pl.BlockSpec((1, D), lambda i: (0, 0))]
    z, m, y = pl.pallas_call(
        body, name=name, grid=(T // tm,), in_specs=in_specs,
        out_specs=[pl.BlockSpec((nk, tm, kb), lambda i: (0, i, 0)), row, row],
        out_shape=[jax.ShapeDtypeStruct((nk, T, kb), BF16), jax.ShapeDtypeStruct((T, D), F32),
                   jax.ShapeDtypeStruct((T, D), F32)],
        compiler_params=_params(1),
    )(*pro_inputs, w4, res, g_post)
    return z, m, y


def post_bwd(dy, m, g_post, alpha, w, nk, name, gu=None, tm=None):
    w4, l = w
    T = dy.shape[0]
    tm = _tile(T, TM, 16) if tm is None else tm
    dpb = N_DEV // nk
    rows = w4.shape[2]
    kb = dpb * rows
    swiglu = gu is not None

    def body(dy_ref, m_ref, g_ref, w_ref, *refs):
        if swiglu:
            gu_ref, dm_ref, dg_ref, dgate_ref, dup_ref = refs
        else:
            dm_ref, dg_ref, dz_ref = refs
        dm, dg = _rms_bwd(m_ref[...], g_ref[...], alpha * dy_ref[...])
        dm = dm.astype(BF16)
        dm_ref[...] = dm
        _accumulate(dg_ref, pl.program_id(0) == 0, dg)
        for k in range(nk):
            dz = lax.dot_general(dm, w_ref[k * dpb:(k + 1) * dpb, l].reshape(kb, D), NT, preferred_element_type=F32)
            if swiglu:
                gate, up = gu_ref[k].astype(F32), gu_ref[k + nk].astype(F32)
                sig = jax.nn.sigmoid(gate)
                dgate_ref[k] = (dz * up * sig * (1.0 + gate * (1.0 - sig))).astype(BF16)
                dup_ref[k] = (dz * gate * sig).astype(BF16)
            else:
                dz_ref[k] = dz.astype(BF16)

    row = pl.BlockSpec((tm, D), lambda i: (i, 0))
    vec = pl.BlockSpec((1, D), lambda i: (0, 0))
    blkk = pl.BlockSpec((nk, tm, kb), lambda i: (0, i, 0))
    in_specs = [row, row, vec, pl.BlockSpec(w4.shape, lambda i: (0, 0, 0, 0), pipeline_mode=pl.Buffered(1))]
    inputs = [dy, m, g_post, w4]
    out_specs = [row, vec, blkk]
    out_shape = [jax.ShapeDtypeStruct((T, D), BF16), jax.ShapeDtypeStruct((1, D), F32),
                 jax.ShapeDtypeStruct((nk, T, kb), BF16)]
    if swiglu:
        in_specs.append(pl.BlockSpec((2 * nk, tm, kb), lambda i: (0, i, 0)))
        inputs.append(gu)
        out_specs.append(blkk)
        out_shape.append(jax.ShapeDtypeStruct((nk, T, kb), BF16))
    return pl.pallas_call(
        body, name=name, grid=(T // tm,), in_specs=in_specs, out_specs=out_specs, out_shape=out_shape,
        compiler_params=_params(1),
    )(*inputs)


def pre_bwd(pieces, x, g_pre, dres, name):
    T = x.shape[0]
    tm = _tile(T, TM, 16)
    na = len(pieces)
    weights = []
    for _, (w4, _), _ in pieces:
        if not any(w4 is w for w in weights):
            weights.append(w4)
    nw = len(weights)
    which = [[w4 is w for w in weights].index(True) for _, (w4, _), _ in pieces]

    def body(*refs):
        dz_refs, w_refs = refs[:na], refs[na:na + nw]
        x_ref, g_ref, dres_ref, dx_ref, dg_ref = refs[na + nw:]
        acc = None
        for a, (dz, (_, l), w_off) in enumerate(pieces):
            for p in range(dz.shape[0]):
                part = lax.dot_general(dz_refs[a][p].astype(BF16), w_refs[which[a]][w_off + p, l], NT,
                                       preferred_element_type=F32)
                acc = part if acc is None else acc + part
        dx, dg = _rms_bwd(x_ref[...], g_ref[...], acc)
        dx_ref[...] = dres_ref[...] + dx
        _accumulate(dg_ref, pl.program_id(0) == 0, dg)

    row = pl.BlockSpec((tm, D), lambda i: (i, 0))
    vec = pl.BlockSpec((1, D), lambda i: (0, 0))
    dz_specs = [pl.BlockSpec((dz.shape[0], tm, dz.shape[2]), lambda i: (0, i, 0)) for dz, _, _ in pieces]
    w_specs = [pl.BlockSpec(w.shape, lambda i: (0, 0, 0, 0), pipeline_mode=pl.Buffered(1)) for w in weights]
    return pl.pallas_call(
        body, name=name, grid=(T // tm,), in_specs=dz_specs + w_specs + [row, vec, row],
        out_specs=[row, vec], out_shape=[jax.ShapeDtypeStruct((T, D), F32), jax.ShapeDtypeStruct((1, D), F32)],
        compiler_params=_params(1),
    )(*[p[0] for p in pieces], *weights, x, g_pre, dres)


def wgrad(a_list, b_list, nout, out4_shape, name):
    T = a_list[0].shape[1]
    tm = _tile(T, 4 * TM, 16)
    nt = T // tm
    dpb = out4_shape[0] // nout
    _, _, R, C = out4_shape
    na, nb = len(a_list), len(b_list)

    def spans(arrs):
        ns = [a.shape[0] for a in arrs]
        return ns, [sum(ns[:k]) for k in range(len(ns))], sum(ns)

    a_ns, a_offs, a_tot = spans(a_list)
    b_ns, b_offs, b_tot = spans(b_list)
    assert a_tot in (1, nout) and b_tot in (1, nout)

    def body(*refs):
        a_refs, b_refs = refs[:na], refs[na:na + nb]
        out_ref, acc = refs[-2:]
        p, t = pl.program_id(0), pl.program_id(1)
        for ia in range(na):
            for ib in range(nb):
                conds = []
                if a_tot > 1:
                    conds += [p >= a_offs[ia], p < a_offs[ia] + a_ns[ia]]
                if b_tot > 1:
                    conds += [p >= b_offs[ib], p < b_offs[ib] + b_ns[ib]]

                def work(ia=ia, ib=ib):
                    part = lax.dot_general(a_refs[ia][0], b_refs[ib][0], TN, preferred_element_type=F32)
                    _accumulate(acc, t == 0, part)

                if conds:
                    pl.when(functools.reduce(jnp.logical_and, conds))(work)
                else:
                    work()

        @pl.when(t == nt - 1)
        def _():
            out_ref[...] = acc[...].astype(BF16).reshape(dpb, 1, R, C)

    def blk(off, n, tot):
        if tot == 1:
            return lambda p: 0
        return lambda p: jnp.clip(p - off, 0, n - 1)

    in_specs = []
    for arrs, ns, offs, tot in ((a_list, a_ns, a_offs, a_tot), (b_list, b_ns, b_offs, b_tot)):
        for arr, n, off in zip(arrs, ns, offs):
            in_specs.append(pl.BlockSpec((1, tm, arr.shape[2]), lambda p, t, f=blk(off, n, tot): (f(p), t, 0)))
    return pl.pallas_call(
        body, name=name, grid=(nout, nt), in_specs=in_specs,
        out_specs=pl.BlockSpec((dpb, 1, R, C), lambda p, t: (p, 0, 0, 0)),
        out_shape=jax.ShapeDtypeStruct(out4_shape, BF16),
        scratch_shapes=[pltpu.VMEM((dpb * R, C), F32)], compiler_params=_params(2),
    )(*a_list, *b_list)


def _swiglu_pro(i, k, gu_ref):
    gate, up = gu_ref[k].astype(F32), gu_ref[k + gu_ref.shape[0] // 2].astype(F32)
    return gate * jax.nn.sigmoid(gate) * up


def _attn_gate_pro(i, k, gate_ref, o_ref):
    return jax.nn.sigmoid(gate_ref[0].astype(F32)) * o_ref[...].astype(F32)


def _shift_rows(u, halo, d):
    rolled = pltpu.roll(u, d, 0)
    row = lax.broadcasted_iota(jnp.int32, u.shape, 0)
    for r in range(d):
        rolled = jnp.where(row == r, halo[HALO - d + r:HALO - d + r + 1, :], rolled)
    return rolled


def _advance_rows(u, halo, d):
    n = u.shape[0]
    rolled = pltpu.roll(u, n - d, 0)
    row = lax.broadcasted_iota(jnp.int32, u.shape, 0)
    for r in range(d):
        rolled = jnp.where(row == n - d + r, halo[r:r + 1, :], rolled)
    return rolled


def _make_conv_pro(tiles_per_seq):
    def pro(i, k, b_ref, c_ref, h_ref, ch_ref, hh_ref, ck_ref):
        u = c_ref[0].astype(F32) * h_ref[0].astype(F32)
        first = (i % tiles_per_seq) == 0
        halo = jnp.where(first, 0.0, ch_ref[0].astype(F32) * hh_ref[0].astype(F32))
        ck = ck_ref[...]
        conv = ck[2:3, :] * u + ck[1:2, :] * _shift_rows(u, halo, 1) + ck[0:1, :] * _shift_rows(u, halo, 2)
        return b_ref[0].astype(F32) * conv
    return pro


def conv_bwd_mix(dz, bch, conv_k, seq_len, name):
    T = dz.shape[1]
    tm = _tile(seq_len, TM, 16)
    tps = seq_len // tm
    nt = T // tm
    hb = tm // HALO

    def body(dz_ref, b_ref, c_ref, h_ref, cp_ref, hp_ref, dzn_ref, bn_ref, ck_ref, dbch_ref, dk_ref):
        i = pl.program_id(0)
        first = (i % tps) == 0
        last = (i % tps) == tps - 1
        b, c, h = b_ref[0].astype(F32), c_ref[0].astype(F32), h_ref[0].astype(F32)
        dzt = dz_ref[0].astype(F32)
        u = c * h
        prev = jnp.where(first, 0.0, cp_ref[0].astype(F32) * hp_ref[0].astype(F32))
        u1, u2 = _shift_rows(u, prev, 1), _shift_rows(u, prev, 2)
        ck = ck_ref[...]
        conv = ck[2:3, :] * u + ck[1:2, :] * u1 + ck[0:1, :] * u2
        dconv = dzt * b
        nxt = jnp.where(last, 0.0, dzn_ref[0].astype(F32) * bn_ref[0].astype(F32))
        du = ck[2:3, :] * dconv + ck[1:2, :] * _advance_rows(dconv, nxt, 1) + ck[0:1, :] * _advance_rows(dconv, nxt, 2)
        dbch_ref[0] = (dzt * conv).astype(BF16)
        dbch_ref[1] = (du * h).astype(BF16)
        dbch_ref[2] = (du * c).astype(BF16)
        tap = lax.broadcasted_iota(jnp.int32, (3, D), 0)
        dk = jnp.where(tap == 0, jnp.sum(dconv * u2, axis=0, keepdims=True),
                       jnp.where(tap == 1, jnp.sum(dconv * u1, axis=0, keepdims=True),
                                 jnp.sum(dconv * u, axis=0, keepdims=True)))
        _accumulate(dk_ref, i == 0, dk)

    def piece(p):
        return pl.BlockSpec((1, tm, D), lambda i, p=p: (p, i, 0))

    def prev(p):
        return pl.BlockSpec((1, HALO, D), lambda i, p=p: (p, jnp.maximum(i * hb - 1, 0), 0))

    def nxt(p):
        return pl.BlockSpec((1, HALO, D), lambda i, p=p: (p, jnp.minimum((i + 1) * hb, nt * hb - 1), 0))

    return pl.pallas_call(
        body, name=name, grid=(nt,),
        in_specs=[piece(0), piece(0), piece(1), piece(2), prev(1), prev(2), nxt(0), nxt(0),
                  pl.BlockSpec((3, D), lambda i: (0, 0))],
        out_specs=[pl.BlockSpec((3, tm, D), lambda i: (0, i, 0)), pl.BlockSpec((3, D), lambda i: (0, 0))],
        out_shape=[jax.ShapeDtypeStruct((3, T, D), BF16), jax.ShapeDtypeStruct((3, D), F32)],
        compiler_params=_params(1),
    )(dz, bch, bch, bch, bch, bch, dz, bch, conv_k)


def _log_sigmoid(x):
    return jnp.minimum(x, 0.0) - jnp.log(1.0 + jnp.exp(-jnp.abs(x)))


def forget_fwd(fl, fb, seq_len, name):
    T = fl.shape[1]

    def body(fl_ref, fb_ref, c_ref):
        c = _log_sigmoid(fl_ref[0] + fb_ref[...])
        row = lax.broadcasted_iota(jnp.int32, c.shape, 0)
        k = 1
        while k < seq_len:
            c = c + jnp.where(row >= k, pltpu.roll(c, k, 0), 0.0)
            k *= 2
        c_ref[...] = c

    return pl.pallas_call(
        body, name=name, grid=(T // seq_len,),
        in_specs=[pl.BlockSpec((1, seq_len, LANES), lambda b: (0, b, 0)), pl.BlockSpec((1, LANES), lambda b: (0, 0))],
        out_specs=pl.BlockSpec((seq_len, LANES), lambda b: (b, 0)),
        out_shape=jax.ShapeDtypeStruct((T, LANES), F32), compiler_params=_params(1),
    )(fl, fb)


def forget_bwd(dc, fl, fb, seq_len, name):
    T = dc.shape[0]

    def body(dc_ref, fl_ref, fb_ref, dfl_ref, dfb_ref):
        b = pl.program_id(0)
        r = dc_ref[...]
        row = lax.broadcasted_iota(jnp.int32, r.shape, 0)
        k = 1
        while k < seq_len:
            r = r + jnp.where(row < seq_len - k, pltpu.roll(r, seq_len - k, 0), 0.0)
            k *= 2
        dfl = r * jax.nn.sigmoid(-(fl_ref[0] + fb_ref[...]))
        dfl_ref[0] = dfl
        _accumulate(dfb_ref, b == 0, jnp.sum(dfl, axis=0, keepdims=True))

    return pl.pallas_call(
        body, name=name, grid=(T // seq_len,),
        in_specs=[pl.BlockSpec((seq_len, LANES), lambda b: (b, 0)), pl.BlockSpec((1, seq_len, LANES), lambda b: (0, b, 0)),
                  pl.BlockSpec((1, LANES), lambda b: (0, 0))],
        out_specs=[pl.BlockSpec((1, seq_len, LANES), lambda b: (0, b, 0)), pl.BlockSpec((1, LANES), lambda b: (0, 0))],
        out_shape=[jax.ShapeDtypeStruct((1, T, LANES), F32), jax.ShapeDtypeStruct((1, LANES), F32)],
        compiler_params=_params(1),
    )(dc, fl, fb)


HEADS = (slice(0, HEAD_DIM), slice(HEAD_DIM, 2 * HEAD_DIM))


def attn_fwd(qg, kv, crow, n_seq, seq_len, name):
    T = n_seq * seq_len
    tq = _tile(seq_len, TQ, LANES)
    nq = seq_len // tq

    def body(q_ref, k_ref, v_ref, cr_ref, o_ref, lse_ref):
        i = pl.program_id(2)
        causal = lax.broadcasted_iota(jnp.int32, (tq, tq), 1) <= lax.broadcasted_iota(jnp.int32, (tq, tq), 0)
        q8 = [q_ref[0, :, sl] * ATT_SCALE for sl in HEADS]

        def block(j, carry, diagonal):
            rows = pl.ds(pl.multiple_of(j * tq, tq), tq)
            out = []
            for hh, sl in enumerate(HEADS):
                m, l, acc = carry[3 * hh:3 * hh + 3]
                s = lax.dot_general(q8[hh], k_ref[0, rows, sl], NT, preferred_element_type=F32)
                s = s - cr_ref[0, 0, j, hh:hh + 1, :]
                if diagonal:
                    s = jnp.where(causal, s, -1e30)
                m_new = jnp.maximum(m, jnp.max(s, axis=-1, keepdims=True))
                a = jnp.exp(m - m_new)
                p = jnp.exp(s - m_new)
                l = a * l + jnp.sum(p, axis=-1, keepdims=True)
                acc = a * acc + jnp.dot(p.astype(BF16), v_ref[0, rows, sl], preferred_element_type=F32)
                out += [m_new, l, acc]
            return tuple(out)

        init = (jnp.full((tq, 1), -1e30, F32), jnp.zeros((tq, 1), F32), jnp.zeros((tq, HEAD_DIM), F32)) * 2
        carry = lax.fori_loop(0, i, lambda j, c: block(j, c, False), init)
        carry = block(i, carry, True)
        o_ref[...] = jnp.concatenate([carry[2] / carry[1], carry[5] / carry[4]], axis=-1)
        lse_ref[...] = jnp.concatenate([jnp.broadcast_to(carry[3 * hh] + jnp.log(carry[3 * hh + 1]), (tq, HEAD_DIM))
                                        for hh in range(2)], axis=-1)

    qspec = pl.BlockSpec((tq, LANES), lambda b, hp, i: (b * nq + i, hp))
    return pl.pallas_call(
        body, name=name, grid=(n_seq, N_HEADS // 2, nq),
        in_specs=[pl.BlockSpec((1, tq, LANES), lambda b, hp, i: (0, b * nq + i, hp)),
                  pl.BlockSpec((1, seq_len, LANES), lambda b, hp, i: (0, b, hp)),
                  pl.BlockSpec((1, seq_len, LANES), lambda b, hp, i: (1, b, hp)),
                  pl.BlockSpec((1, 1, nq, 2, tq), lambda b, hp, i: (b, hp, 0, 0, 0))],
        out_specs=[qspec, qspec],
        out_shape=[jax.ShapeDtypeStruct((T, D), F32), jax.ShapeDtypeStruct((T, D), F32)],
        compiler_params=_params(3),
    )(qg, kv, kv, crow)


def attn_bwd(dz, qg, kv, o, lse, crow, n_seq, seq_len, name):
    T = n_seq * seq_len
    tq = _tile(seq_len, TQ, LANES)
    nq = seq_len // tq

    def body(dz_ref, q_ref, gate_ref, o_ref, lse_ref, cr_ref, k_ref, v_ref,
             dq_ref, dgate_ref, dk_ref, dv_ref, dcr_ref, p_s, dp_s, dk_s, dv_s):
        i = pl.program_id(2)

        @pl.when(i == 0)
        def _():
            dk_s[...] = jnp.zeros_like(dk_s)
            dv_s[...] = jnp.zeros_like(dv_s)
            dcr_ref[...] = jnp.zeros_like(dcr_ref)

        dzf = dz_ref[...].astype(F32)
        sig = jax.nn.sigmoid(gate_ref[0].astype(F32))
        dob = (dzf * sig).astype(BF16)
        dgate_ref[0] = (dzf * o_ref[...] * sig * (1.0 - sig)).astype(BF16)
        causal = lax.broadcasted_iota(jnp.int32, (tq, tq), 1) <= lax.broadcasted_iota(jnp.int32, (tq, tq), 0)
        q8 = [q_ref[0, :, sl] * ATT_SCALE for sl in HEADS]
        do = [dob[:, sl] for sl in HEADS]
        lse_i = [lse_ref[:, sl.start:sl.start + 1] for sl in HEADS]

        def probs(j, dsums, diagonal):
            rows = pl.ds(pl.multiple_of(j * tq, tq), tq)
            out = []
            for hh, sl in enumerate(HEADS):
                s = lax.dot_general(q8[hh], k_ref[0, rows, sl], NT, preferred_element_type=F32)
                p = jnp.exp(s - cr_ref[0, 0, j, hh:hh + 1, :] - lse_i[hh])
                if diagonal:
                    p = jnp.where(causal, p, 0.0)
                dp = lax.dot_general(do[hh], v_ref[0, rows, sl], NT, preferred_element_type=F32)
                p_s[hh, j] = p
                dp_s[hh, j] = dp
                out.append(dsums[hh] + jnp.sum(p * dp, axis=-1, keepdims=True))
            return tuple(out)

        dsums = lax.fori_loop(0, i, lambda j, c: probs(j, c, False), (jnp.zeros((tq, 1), F32),) * 2)
        dsums = probs(i, dsums, True)

        def grads(j, dqs):
            rows = pl.ds(pl.multiple_of(j * tq, tq), tq)
            out = []
            for hh, sl in enumerate(HEADS):
                p = p_s[hh, j]
                ds = p * (dp_s[hh, j] - dsums[hh])
                dcr_ref[0, 0, j, hh:hh + 1, :] -= jnp.sum(ds, axis=0, keepdims=True)
                dsb = ds.astype(BF16)
                dk_s[rows, sl] += lax.dot_general(dsb, q8[hh], TN, preferred_element_type=F32)
                dv_s[rows, sl] += lax.dot_general(p.astype(BF16), do[hh], TN, preferred_element_type=F32)
                out.append(dqs[hh] + jnp.dot(dsb, k_ref[0, rows, sl], preferred_element_type=F32))
            return tuple(out)

        dqs = lax.fori_loop(0, i + 1, grads, (jnp.zeros((tq, HEAD_DIM), F32),) * 2)
        dq_ref[0] = (jnp.concatenate(dqs, axis=-1) * ATT_SCALE).astype(BF16)

        @pl.when(i == nq - 1)
        def _():
            dk_ref[0] = dk_s[...].astype(BF16)
            dv_ref[0] = dv_s[...].astype(BF16)

    qry2 = pl.BlockSpec((tq, LANES), lambda b, hp, i: (b * nq + i, hp))

    def qry3(p):
        return pl.BlockSpec((1, tq, LANES), lambda b, hp, i, p=p: (p, b * nq + i, hp))

    def seq3(p):
        return pl.BlockSpec((1, seq_len, LANES), lambda b, hp, i, p=p: (p, b, hp))

    crs = pl.BlockSpec((1, 1, nq, 2, tq), lambda b, hp, i: (b, hp, 0, 0, 0))
    act = jax.ShapeDtypeStruct((1, T, D), BF16)
    return pl.pallas_call(
        body, name=name, grid=(n_seq, N_HEADS // 2, nq),
        in_specs=[qry2, qry3(0), qry3(1), qry2, qry2, crs, seq3(0), seq3(1)],
        out_specs=[qry3(0), qry3(0), seq3(0), seq3(0), crs],
        out_shape=[act, act, act, act, jax.ShapeDtypeStruct((n_seq, N_HEADS // 2, nq, 2, tq), F32)],
        scratch_shapes=[pltpu.VMEM((2, nq, tq, tq), F32), pltpu.VMEM((2, nq, tq, tq), F32),
                        pltpu.VMEM((seq_len, LANES), F32), pltpu.VMEM((seq_len, LANES), F32)],
        compiler_params=_params(3),
    )(dz, qg, qg, o, lse, crow, kv, kv)


def loss_head(y, target, name):
    T = y.shape[0]
    tm = _tile(T, TM, 8)

    def body(y_ref, t_ref, dy_ref, loss_ref):
        err = y_ref[...] - t_ref[...]
        dy_ref[...] = err * (1.0 / D)
        part = 0.5 * jnp.sum(jnp.mean(err * err, axis=-1, keepdims=True), axis=0, keepdims=True)
        _accumulate(loss_ref, pl.program_id(0) == 0, jnp.broadcast_to(part, (1, LANES)))

    row = pl.BlockSpec((tm, D), lambda i: (i, 0))
    return pl.pallas_call(
        body, name=name, grid=(T // tm,), in_specs=[row, row],
        out_specs=[row, pl.BlockSpec((1, LANES), lambda i: (0, 0))],
        out_shape=[jax.ShapeDtypeStruct((T, D), F32), jax.ShapeDtypeStruct((1, LANES), F32)],
        compiler_params=_params(1),
    )(y, target)


def _adamw(w, g, m, v):
    m = ADAM_B1 * m + (1.0 - ADAM_B1) * g
    v = ADAM_B2 * v + (1.0 - ADAM_B2) * (g * g)
    m_hat = m / (1.0 - ADAM_B1 ** ADAM_STEP)
    v_hat = v / (1.0 - ADAM_B2 ** ADAM_STEP)
    delta = -ADAM_LR * (m_hat / (jnp.sqrt(v_hat) + ADAM_EPS) + ADAM_WD * w)
    return delta, m, v


def adam_sharded(w, m, v, contribs, name):
    L, R, C = w.shape
    tr = _tile(R, 256, 16) if R % 16 == 0 else R

    def body(w_ref, m_ref, v_ref, *refs):
        c_refs, (g_ref, d_ref, nm_ref, nv_ref) = refs[:2 * L], refs[2 * L:]
        l = pl.program_id(0)
        for j in range(L):
            @pl.when(l == j)
            def _(j=j):
                own_ref, recv_ref = c_refs[2 * j], c_refs[2 * j + 1]
                g = own_ref[...].astype(F32)
                for k in range(N_DEV - 1):
                    g = g + recv_ref[k].astype(F32)
                delta, nm, nv = _adamw(w_ref[0], g, m_ref[0], v_ref[0])
                g_ref[0] = g
                d_ref[0] = delta
                nm_ref[0] = nm
                nv_ref[0] = nv

    blk = pl.BlockSpec((1, tr, C), lambda l, i: (l, i, 0))
    in_specs = [blk, blk, blk]
    inputs = [w, m, v]
    for j, (own, recv) in enumerate(contribs):
        in_specs.append(pl.BlockSpec((tr, C), lambda l, i, j=j: (jnp.where(l == j, i, 0), 0)))
        in_specs.append(pl.BlockSpec((N_DEV - 1, tr, C), lambda l, i, j=j: (0, jnp.where(l == j, i, 0), 0)))
        inputs += [own, recv]
    shp = jax.ShapeDtypeStruct((L, R, C), F32)
    return pl.pallas_call(
        body, name=name, grid=(L, R // tr), in_specs=in_specs, out_specs=[blk] * 4, out_shape=[shp] * 4,
        compiler_params=_params(2),
    )(*inputs)


def adam_small(params, total, extra_grads, name):
    n, ne = len(params), len(extra_grads)

    def body(*refs):
        total_ref, extra_refs = refs[0], refs[1:1 + ne]
        ins, outs = refs[1 + ne:1 + ne + 3 * n], refs[1 + ne + 3 * n:]
        for k, (_, _, _, where) in enumerate(params):
            if isinstance(where, int):
                g = extra_refs[where][...]
            else:
                row, rows, width = where
                g = total_ref[row:row + rows, 0:width]
            delta, nm, nv = _adamw(ins[3 * k][...], g, ins[3 * k + 1][...], ins[3 * k + 2][...])
            outs[4 * k][...] = g
            outs[4 * k + 1][...] = delta
            outs[4 * k + 2][...] = nm
            outs[4 * k + 3][...] = nv

    flat = [a for w, m, v, _ in params for a in (w, m, v)]
    out_shape = [jax.ShapeDtypeStruct(w.shape, F32) for w, _, _, _ in params for _ in range(4)]
    res = pl.pallas_call(body, name=name, out_shape=out_shape)(total, *extra_grads, *flat)
    return [res[4 * k:4 * k + 4] for k in range(n)]


def _place():
    return lax.axis_index("x"), lax.axis_index("y"), lax.axis_index("c")


def _peer(place, k):
    x, y, c = place
    return x ^ (k >> 2), y ^ ((k >> 1) & 1), c ^ (k & 1)


ANY = pl.BlockSpec(memory_space=pl.ANY)
HBM = pl.BlockSpec(memory_space=pltpu.HBM)
SEM = pl.BlockSpec(memory_space=pltpu.SEMAPHORE)
EFFECT = pltpu.SideEffectType.DATAFLOW_SIDE_EFFECTING


def _in_hbm(a):
    return pltpu.with_memory_space_constraint(a, pltpu.HBM)


def _number(place):
    return 4 * place[0] + 2 * place[1] + place[2]


SLOTS = {"gather_like": 4, "gather_pass": 3, "scatter": 7}


def _plan(mode, src_ref, land_ref, place):
    if mode == "gather_like":
        return [(src_ref, land_ref.at[_number(place)], _peer(place, k)) for k in (1, 2, 4, 6)]
    if mode == "gather_pass":
        slots = [land_ref.at[_number(_peer(place, k))] for k in (2, 4, 6)]
        return [(slot, slot, _peer(place, 1)) for slot in slots]
    return [(src_ref.at[_number(_peer(place, k))], land_ref.at[k - 1], _peer(place, k)) for k in range(1, N_DEV)]


def _exchange(name, groups, start, afters):
    sizes = [len(g[2]) for g in groups]
    na, ng = sum(sizes), len(groups)
    waits = groups[0][0] is not None
    n_in_sems = 2 * ng if waits else 0
    n_out_sems = 2 * ng if start else 0

    def body(*refs):
        src_refs, land_refs = refs[:na], refs[na:2 * na]
        in_sems = refs[2 * na:2 * na + n_in_sems]
        outs = refs[2 * na + n_in_sems + len(afters):]
        place = _place()
        a = 0
        for gi, n in enumerate(sizes):
            for idx in range(n):
                if waits:
                    zone = land_refs[a].at[pl.ds(0, groups[gi][4])]
                    copy = pltpu.make_async_remote_copy(
                        src_ref=zone, dst_ref=zone, send_sem=in_sems[2 * gi].at[idx], recv_sem=in_sems[2 * gi + 1].at[idx],
                        device_id=_peer(place, 1), device_id_type=MESH)
                    copy.wait_send()
                    copy.wait_recv()
                if start:
                    for src, dst, peer in _plan(start, src_refs[a], land_refs[a], place):
                        pltpu.make_async_remote_copy(
                            src_ref=src, dst_ref=dst, send_sem=outs[2 * gi].at[idx], recv_sem=outs[2 * gi + 1].at[idx],
                            device_id=peer, device_id_type=MESH).start()
                a += 1
        if start:
            outs[-1][...] = jnp.zeros_like(outs[-1])

    srcs = [_in_hbm(s) for g in groups for s in g[2]]
    lands = [_in_hbm(l) for g in groups for l in g[3]]
    sems = [s for g in groups for s in g[:2]] if waits else []
    out_shape = [pltpu.SemaphoreType.DMA((n,)) for n in sizes for _ in range(2)] if start else []
    out_shape += [pltpu.HBM(a.shape, a.dtype) for a in srcs + lands]
    out_specs = [SEM] * n_out_sems + [HBM] * (2 * na)
    if start:
        out_shape.append(jax.ShapeDtypeStruct((8, LANES), F32))
        out_specs.append(pl.BlockSpec(memory_space=pltpu.VMEM))
    res = pl.pallas_call(
        body, name=name, in_specs=[HBM] * (2 * na) + [SEM] * n_in_sems + [ANY] * len(afters),
        out_shape=out_shape, out_specs=out_specs,
        input_output_aliases={i: n_out_sems + i for i in range(2 * na)},
        compiler_params=pltpu.CompilerParams(has_side_effects=EFFECT),
    )(*srcs, *lands, *sems, *afters)
    new_sems, thru = res[:n_out_sems], res[n_out_sems:n_out_sems + 2 * na]
    out, a = [], 0
    for gi, n in enumerate(sizes):
        pair = (new_sems[2 * gi], new_sems[2 * gi + 1]) if start else (None, None)
        out.append(pair + (thru[a:a + n], thru[na + a:na + a + n], SLOTS.get(start, 0)))
        a += n
    return out, (res[-1] if start else None)


def exchange_start(pair_groups, mode, name):
    groups = [(None, None, [s for s, _ in g], [l for _, l in g], 0) for g in pair_groups]
    return _exchange(name, groups, mode, ())


def exchange_relay(groups, mode, afters, name):
    return _exchange(name, groups, mode, afters)


def exchange_wait(groups, afters, name):
    done, _ = _exchange(name, groups, None, afters)
    return [(g[2], g[3]) for g in done]


def all_reduce_small(parts, n_rows, name):
    R = n_rows
    n_parts = len(parts)

    def body(*refs):
        part_refs = refs[:n_parts]
        out_ref, buf, send_sems, recv_sems = refs[n_parts:]
        x, y, c = _place()
        me = 4 * x + 2 * y + c
        own = buf.at[me]
        own[...] = jnp.zeros((R, D), F32)
        for ref, (arr, row) in zip(part_refs, parts):
            own[row:row + arr.shape[0], 0:arr.shape[1]] = ref[...]
        copies = []
        for k in range(1, N_DEV):
            peer = (x ^ (k >> 2), y ^ ((k >> 1) & 1), c ^ (k & 1))
            copies.append(pltpu.make_async_remote_copy(
                src_ref=own, dst_ref=own, send_sem=send_sems.at[k - 1], recv_sem=recv_sems.at[k - 1],
                device_id=peer, device_id_type=MESH))
        for cp in copies:
            cp.start()
        for cp in copies:
            cp.wait()
        total = buf[0]
        for d in range(1, N_DEV):
            total = total + buf[d]
        out_ref[...] = total

    vm = pl.BlockSpec(memory_space=pltpu.VMEM)
    return pl.pallas_call(
        body, name=name, in_specs=[vm] * n_parts, out_specs=vm, out_shape=jax.ShapeDtypeStruct((R, D), F32),
        scratch_shapes=[pltpu.VMEM((N_DEV, R, D), F32), pltpu.SemaphoreType.DMA((N_DEV - 1,)),
                        pltpu.SemaphoreType.DMA((N_DEV - 1,))],
    )(*[arr for arr, _ in parts])


def _col_blocks(gathered, n_blocks):
    n, d, w = gathered.shape
    whole = gathered.transpose(1, 0, 2).reshape(d, n * w)
    return whole.reshape(d, n_blocks, n * w // n_blocks).transpose(1, 0, 2)[:, None]


def _col_shards(blocks):
    n, d, w = blocks.shape
    whole = blocks.transpose(1, 0, 2).reshape(d, n * w)
    return whole.reshape(d, N_DEV, n * w // N_DEV).transpose(1, 0, 2)


def kernel(x, ffn1_pre_g, ffn1_post_g, ffn1_w_in, ffn1_w_out, mix_pre_g, mix_post_g, ffn2_pre_g, ffn2_post_g, ffn2_w_in, ffn2_w_out, conv_w_in, conv_k, conv_w_out, kv_g, kv_w, forget_b, attn_w_qg, attn_w_o, loss_target, m_ffn1_pre_g, m_ffn1_post_g, m_ffn1_w_in, m_ffn1_w_out, m_mix_pre_g, m_mix_post_g, m_ffn2_pre_g, m_ffn2_post_g, m_ffn2_w_in, m_ffn2_w_out, m_conv_w_in, m_conv_k, m_conv_w_out, m_kv_g, m_kv_w, m_forget_b, m_attn_w_qg, m_attn_w_o, v_ffn1_pre_g, v_ffn1_post_g, v_ffn1_w_in, v_ffn1_w_out, v_mix_pre_g, v_mix_post_g, v_ffn2_pre_g, v_ffn2_post_g, v_ffn2_w_in, v_ffn2_w_out, v_conv_w_in, v_conv_k, v_conv_w_out, v_kv_g, v_kv_w, v_forget_b, v_attn_w_qg, v_attn_w_o):
    n_seq, seq_len, _ = x.shape
    T = n_seq * seq_len
    xi, yi, ci = _place()
    dev = 4 * xi + 2 * yi + ci
    x0 = x.reshape(T, D)
    target = loss_target.reshape(T, D)

    def layer(w, l):
        return w[l:l + 1].astype(BF16)

    shard_groups = [
        [layer(ffn1_w_in, 0), layer(ffn1_w_out, 0)],
        [conv_w_in.astype(BF16), conv_w_out.astype(BF16), conv_k],
        [layer(ffn2_w_in, 0), layer(ffn2_w_out, 0)],
        [kv_w.astype(BF16), layer(ffn1_w_in, 1), layer(ffn1_w_out, 1), attn_w_qg.astype(BF16), attn_w_o.astype(BF16)],
        [layer(ffn2_w_in, 1), layer(ffn2_w_out, 1)]]
    def zone(s):
        return lax.dynamic_update_slice(lax.empty((N_DEV,) + s.shape, s.dtype), s[None], (dev,) + (0,) * s.ndim)

    gathers, token = exchange_start([[(s, zone(s)) for s in g] for g in shard_groups], "gather_like", "gather_start")

    def gathered(k, after):
        passed, _ = exchange_relay([gathers[k]], "gather_pass", [after], f"gather_pass{k}")
        return exchange_wait(passed, [after], f"gather_wait{k}")[0][1]

    fb = jnp.pad(forget_b, (0, LANES - N_HEADS))[None]

    def vec(g, l):
        return g[l:l + 1]

    def behind(g, tok):
        return g + tok[:1, :1]

    grads_small = {}
    tm_ffn = _tile(T, TM // 2, 16)

    def ffn_fwd(xin, g_pre, g_post, w_in, w_out, tag):
        xn, (gu,) = rms_proj(xin, g_pre, [(w_in, 0)], [BF16], f"{tag}_in")
        specs = [pl.BlockSpec((gu.shape[0], tm_ffn, gu.shape[2]), lambda i: (0, i, 0))]
        a, h, y = mix_out(_swiglu_pro, [gu], specs, (w_out, 0), xin, g_post, 0.5, 4, f"{tag}_out", tm=tm_ffn)
        return y, (xin, xn, gu, a, h)

    def ffn_bwd(dy, saved, g_pre, g_post, w_in, w_out, tag):
        xin, xn, gu, a, h = saved
        dh, dg_post, dgate, dup = post_bwd(dy, h, g_post, 0.5, (w_out, 0), 4, f"{tag}_bwd_out", gu=gu, tm=tm_ffn)
        dw_out = wgrad([a], [dh[None]], 4, w_out.shape, f"{tag}_dw_out")
        dw_in = wgrad([dgate, dup], [xn[None]], 8, (8, 1, w_in.shape[3], D), f"{tag}_dw_in")
        started, tok = scatter_start([dw_in.reshape(8, -1, D), dw_out.reshape(8, -1, D)], f"{tag}_scatter_start")
        dx, dg_pre = pre_bwd([(dgate, (w_in, 0), 0), (dup, (w_in, 0), 4)], xin, behind(g_pre, tok), dy, f"{tag}_bwd_in")
        return dx, dg_pre, dg_post, started

    def scatter_start(blocked, name):
        pairs = [(g, lax.empty((N_DEV - 1,) + g.shape[1:], g.dtype)) for g in blocked]
        started, tok = exchange_start([pairs], "scatter", name)
        return started[0], tok

    w1_in, w1_out = gathered(0, token)
    x1, s_f1a = ffn_fwd(x0, vec(ffn1_pre_g, 0), vec(ffn1_post_g, 0), w1_in, w1_out, "l0_ffn1")
    cw_in_g, cw_out, ck_g = gathered(1, x1)
    cw_in = _col_blocks(cw_in_g[:, 0], 3)
    ck = ck_g[:, 0].transpose(1, 0, 2).reshape(3, D)
    xn_c, (bch,) = rms_proj(x1, vec(mix_pre_g, 0), [(cw_in, 0)], [BF16], "conv_in")
    tmc = _tile(seq_len, TM, 16)
    hb = tmc // HALO

    def cpiece(p):
        return pl.BlockSpec((1, tmc, D), lambda i, p=p: (p, i, 0))

    def chalo(p):
        return pl.BlockSpec((1, HALO, D), lambda i, p=p: (p, jnp.maximum(i * hb - 1, 0), 0))

    conv_specs = [cpiece(0), cpiece(1), cpiece(2), chalo(1), chalo(2), pl.BlockSpec((3, D), lambda i: (0, 0))]
    z_c, m_c, x2 = mix_out(_make_conv_pro(seq_len // tmc), [bch, bch, bch, bch, bch, ck], conv_specs, (cw_out, 0),
                           x1, vec(mix_post_g, 0), 1.0, 1, "conv_out", tm=tmc)
    w2_in, w2_out = gathered(2, x2)
    x3, s_f2a = ffn_fwd(x2, vec(ffn2_pre_g, 0), vec(ffn2_post_g, 0), w2_in, w2_out, "l0_ffn2")

    kvw_g, w1_in_b, w1_out_b, qgw_g, ow = gathered(3, x3)
    qg_w = _col_blocks(qgw_g[:, 0], 2)
    kv_whole = kvw_g.transpose(1, 0, 2).reshape(D, 2 * D + N_HEADS)
    kv_wb = kv_whole[:, :2 * D].reshape(D, 2, D).transpose(1, 0, 2)[:, None]
    f_w = jnp.pad(kv_whole[:, 2 * D:], ((0, 0), (0, LANES - N_HEADS)))[None, None]
    xn_kv, (kv, fl) = rms_proj(x3, kv_g[None], [(kv_wb, 0), (f_w, 0)], [BF16, F32], "kv_in")
    c128 = forget_fwd(fl, fb, seq_len, "forget_fwd")
    c16 = c128[:, :N_HEADS]
    tq = _tile(seq_len, TQ, LANES)
    crow = c16.reshape(n_seq, seq_len // tq, tq, N_HEADS // 2, 2).transpose(0, 3, 1, 4, 2)

    x4, s_f1b = ffn_fwd(x3, vec(ffn1_pre_g, 1), vec(ffn1_post_g, 1), w1_in_b, w1_out_b, "l1_ffn1")
    xn_a, (qg,) = rms_proj(x4, vec(mix_pre_g, 1), [(qg_w, 0)], [BF16], "attn_in")
    o, lse = attn_fwd(qg, kv, crow, n_seq, seq_len, "attn_fwd")
    tm = _tile(T, TM, 16)
    gate_specs = [pl.BlockSpec((1, tm, D), lambda i: (1, i, 0)), pl.BlockSpec((tm, D), lambda i: (i, 0))]
    z_a, m_a, x5 = mix_out(_attn_gate_pro, [qg, o], gate_specs, (ow, 0), x4, vec(mix_post_g, 1), 1.0, 1, "attn_out")
    w2_in_b, w2_out_b = gathered(4, x5)
    x6, s_f2b = ffn_fwd(x5, vec(ffn2_pre_g, 1), vec(ffn2_post_g, 1), w2_in_b, w2_out_b, "l1_ffn2")

    dy, loss_part = loss_head(x6, target, "loss_head")

    scatters = {}
    dx5, dg, dgp, scatters["ffn2", 1] = ffn_bwd(dy, s_f2b, vec(ffn2_pre_g, 1), vec(ffn2_post_g, 1), w2_in_b, w2_out_b, "l1_ffn2")
    grads_small["ffn2_pre", 1], grads_small["ffn2_post", 1] = dg, dgp
    dm_a, dgp, dz_a = post_bwd(dx5, m_a, vec(mix_post_g, 1), 1.0, (ow, 0), 1, "attn_bwd_out")
    grads_small["mix_post", 1] = dgp
    dq, dgate, dk, dv, dcrow = attn_bwd(dz_a[0], qg, kv, o, lse, crow, n_seq, seq_len, "attn_bwd")
    dx4, dg = pre_bwd([(dq, (qg_w, 0), 0), (dgate, (qg_w, 0), 1)], x4, vec(mix_pre_g, 1), dx5, "attn_bwd_in")
    grads_small["mix_pre", 1] = dg
    d_ow = wgrad([z_a], [dm_a[None]], 1, ow.shape, "attn_dw_o")
    d_qgw = wgrad([xn_a[None]], [dq, dgate], 2, (2, 1, D, D), "attn_dw_qg")
    scatters["attn"], tok = scatter_start([_col_shards(d_qgw[:, 0]), d_ow.reshape(8, -1, D)], "attn_scatter_start")
    dx3, dg, dgp, scatters["ffn1", 1] = ffn_bwd(dx4, s_f1b, vec(ffn1_pre_g, 1), behind(vec(ffn1_post_g, 1), tok),
                                                w1_in_b, w1_out_b, "l1_ffn1")
    grads_small["ffn1_pre", 1], grads_small["ffn1_post", 1] = dg, dgp

    dc16 = dcrow.transpose(0, 2, 4, 1, 3).reshape(T, N_HEADS)
    dfl, dfb = forget_bwd(jnp.pad(dc16, ((0, 0), (0, LANES - N_HEADS))), fl, fb, seq_len, "forget_bwd")
    dx3, dg_kv = pre_bwd([(dk, (kv_wb, 0), 0), (dv, (kv_wb, 0), 1), (dfl, (f_w, 0), 0)], x3, kv_g[None], dx3, "kv_bwd_in")
    d_kvw = wgrad([xn_kv[None]], [dk, dv], 2, (2, 1, D, D), "kv_dw")
    d_fw = wgrad([xn_kv[None]], [dfl.astype(BF16)], 1, (1, 1, D, LANES), "forget_dw")
    d_kv_whole = jnp.concatenate([d_kvw[0, 0], d_kvw[1, 0], d_fw[0, 0, :, :N_HEADS]], axis=1)
    wshard = D * 2 + N_HEADS
    scatters["kv"], tok = scatter_start([d_kv_whole.T.reshape(N_DEV, wshard // N_DEV, D)], "kv_scatter_start")

    dx2, dg, dgp, scatters["ffn2", 0] = ffn_bwd(dx3, s_f2a, vec(ffn2_pre_g, 0), behind(vec(ffn2_post_g, 0), tok),
                                                w2_in, w2_out, "l0_ffn2")
    grads_small["ffn2_pre", 0], grads_small["ffn2_post", 0] = dg, dgp
    dm_c, dgp, dz_c = post_bwd(dx2, m_c, vec(mix_post_g, 0), 1.0, (cw_out, 0), 1, "conv_bwd_out")
    grads_small["mix_post", 0] = dgp
    dbch, d_ck = conv_bwd_mix(dz_c, bch, ck, seq_len, "conv_bwd_mix")
    dx1, dg = pre_bwd([(dbch, (cw_in, 0), 0)], x1, vec(mix_pre_g, 0), dx2, "conv_bwd_in")
    grads_small["mix_pre", 0] = dg
    d_cw_out = wgrad([z_c], [dm_c[None]], 1, cw_out.shape, "conv_dw_out")
    d_cw_in = wgrad([xn_c[None]], [dbch], 3, (3, 1, D, D), "conv_dw_in")
    scatters["conv"], tok = scatter_start([_col_shards(d_cw_in[:, 0]), d_cw_out.reshape(8, -1, D)], "conv_scatter_start")
    dx0, dg, dgp, scatters["ffn1", 0] = ffn_bwd(dx1, s_f1a, vec(ffn1_pre_g, 0), behind(vec(ffn1_post_g, 0), tok),
                                                w1_in, w1_out, "l0_ffn1")
    grads_small["ffn1_pre", 0], grads_small["ffn1_post", 0] = dg, dgp

    parts_of = {}

    def scatter_end(keys, afters, name):
        for k, (sent, recv) in zip(keys, exchange_wait([scatters[k] for k in keys], afters, name)):
            parts_of[k] = [(lax.dynamic_index_in_dim(g, dev, 0, keepdims=False), r) for g, r in zip(sent, recv)]

    scatter_end([("ffn2", 1), "attn", ("ffn1", 1), "kv", ("ffn2", 0), "conv"], [dx0], "scatter_wait")
    sharded = {"ffn2_w_in": (ffn2_w_in, m_ffn2_w_in, v_ffn2_w_in), "ffn2_w_out": (ffn2_w_out, m_ffn2_w_out, v_ffn2_w_out),
               "conv_w_in": (conv_w_in, m_conv_w_in, v_conv_w_in), "conv_w_out": (conv_w_out, m_conv_w_out, v_conv_w_out),
               "kv_w": (kv_w, m_kv_w, v_kv_w), "attn_w_qg": (attn_w_qg, m_attn_w_qg, v_attn_w_qg),
               "attn_w_o": (attn_w_o, m_attn_w_o, v_attn_w_o),
               "ffn1_w_in": (ffn1_w_in, m_ffn1_w_in, v_ffn1_w_in), "ffn1_w_out": (ffn1_w_out, m_ffn1_w_out, v_ffn1_w_out)}
    out = {}
    for nm, (w, mm, vv) in sharded.items():
        if nm == "ffn1_w_in":
            scatter_end([("ffn1", 0)], [res[0] for res in out.values()], "scatter_wait_last")
        contribs = {
            "ffn1_w_in": lambda: [parts_of["ffn1", 0][0], parts_of["ffn1", 1][0]],
            "ffn1_w_out": lambda: [parts_of["ffn1", 0][1], parts_of["ffn1", 1][1]],
            "ffn2_w_in": lambda: [parts_of["ffn2", 0][0], parts_of["ffn2", 1][0]],
            "ffn2_w_out": lambda: [parts_of["ffn2", 0][1], parts_of["ffn2", 1][1]],
            "conv_w_in": lambda: [parts_of["conv"][0]], "conv_w_out": lambda: [parts_of["conv"][1]],
            "kv_w": lambda: [parts_of["kv"][0]], "attn_w_qg": lambda: [parts_of["attn"][0]],
            "attn_w_o": lambda: [parts_of["attn"][1]]}[nm]()
        if nm in ("ffn1_w_in", "ffn2_w_in", "kv_w"):
            rows, cols = w.shape[-2:]

            def view(a):
                return a.reshape(-1, rows, cols).transpose(0, 2, 1)

            res = adam_sharded(view(w), view(mm), view(vv), contribs, f"adam_{nm}")
            out[nm] = [r.transpose(0, 2, 1).reshape(w.shape) for r in res]
        else:
            shape3 = (len(contribs),) + contribs[0][0].shape
            res = adam_sharded(w.reshape(shape3), mm.reshape(shape3), vv.reshape(shape3), contribs, f"adam_{nm}")
            out[nm] = [r.reshape(w.shape) for r in res]

    small_names = ["ffn1_pre", "ffn1_post", "mix_pre", "mix_post", "ffn2_pre", "ffn2_post"]
    parts = [(grads_small[n, l], 2 * k + l) for k, n in enumerate(small_names) for l in range(2)]
    parts += [(dg_kv, 12), (dfb, 13), (d_ck, 14), (loss_part, 17)]
    total = all_reduce_small(parts, 24, "all_reduce_small")
    loss = total[17, 0]
    d_ck_mine = lax.dynamic_slice(total, (14, dev * LANES), (3, LANES))
    gains = [(ffn1_pre_g, m_ffn1_pre_g, v_ffn1_pre_g), (ffn1_post_g, m_ffn1_post_g, v_ffn1_post_g),
             (mix_pre_g, m_mix_pre_g, v_mix_pre_g), (mix_post_g, m_mix_post_g, v_mix_post_g),
             (ffn2_pre_g, m_ffn2_pre_g, v_ffn2_pre_g), (ffn2_post_g, m_ffn2_post_g, v_ffn2_post_g)]
    small_params = [(w, m, v, (2 * k, 2, D)) for k, (w, m, v) in enumerate(gains)]
    small_params += [(kv_g[None], m_kv_g[None], v_kv_g[None], (12, 1, D)),
                     (forget_b[None], m_forget_b[None], v_forget_b[None], (13, 1, N_HEADS)),
                     (conv_k[0], m_conv_k[0], v_conv_k[0], 0)]
    small_res = adam_small(small_params, total, [d_ck_mine], "adam_small")
    small_keys = [n + "_g" for n in small_names] + ["kv_g", "forget_b", "conv_k"]
    shapes = {"kv_g": kv_g.shape, "forget_b": forget_b.shape, "conv_k": conv_k.shape}
    small = [{key: res[kind].reshape(shapes.get(key, res[kind].shape)) for key, res in zip(small_keys, small_res)}
             for kind in range(4)]
    order = ["ffn1_pre_g", "ffn1_post_g", "ffn1_w_in", "ffn1_w_out", "mix_pre_g", "mix_post_g", "ffn2_pre_g", "ffn2_post_g",
             "ffn2_w_in", "ffn2_w_out", "conv_w_in", "conv_k", "conv_w_out", "kv_g", "kv_w", "forget_b", "attn_w_qg",
             "attn_w_o"]
    results = [loss, dx0.reshape(x.shape)]
    for kind in range(4):
        for nm in order:
            results.append(out[nm][kind] if nm in out else small[kind][nm])
    return tuple(results)
```

```python
import functools
import math

import jax
import jax.numpy as jnp
from jax import lax
from jax.experimental import pallas as pl
from jax.experimental.pallas import tpu as pltpu

F32, BF16 = jnp.float32, jnp.bfloat16
D = 1024
N_HEADS = 16
HEAD_DIM = 64
N_DEV = 8
RMS_EPS = 1e-6
ATT_SCALE = 1.0 / math.sqrt(HEAD_DIM)
LANES = 128
HALO = 8
TM = 512
TQ = 256
VMEM_LIMIT = 48 * 1024 * 1024
MESH = pl.DeviceIdType.MESH

ADAM_LR, ADAM_B1, ADAM_B2, ADAM_EPS, ADAM_WD, ADAM_STEP = 0.001, 0.9, 0.999, 1e-08, 0.01, 10

NT = (((1,), (1,)), ((), ()))
TN = (((0,), (0,)), ((), ()))


def _params(n_axes):
    return pltpu.CompilerParams(dimension_semantics=("arbitrary",) * n_axes, vmem_limit_bytes=VMEM_LIMIT)


def _tile(n, cap, mult):
    best = None
    for t in range(mult, min(n, cap) + 1, mult):
        if n % t == 0:
            best = t
    assert best is not None, (n, cap, mult)
    return best


def _rms_rstd(x):
    return lax.rsqrt(jnp.mean(x * x, axis=-1, keepdims=True) + RMS_EPS)


def _rms_fwd(x, g):
    return x * _rms_rstd(x) * g


def _rms_bwd(x, g, dy):
    xh = x * _rms_rstd(x)
    dyg = dy * g
    dx = _rms_rstd(x) * (dyg - xh * jnp.mean(dyg * xh, axis=-1, keepdims=True))
    return dx, jnp.sum(dy * xh, axis=0, keepdims=True)


def _accumulate(ref, first, value):
    @pl.when(first)
    def _():
        ref[...] = value

    @pl.when(jnp.logical_not(first))
    def _():
        ref[...] += value


def rms_proj(x, g, ws, out_dtypes, name):
    T = x.shape[0]
    tm = _tile(T, TM, 16)
    na = len(ws)

    def body(x_ref, g_ref, *refs):
        w_refs, xn_ref, o_refs = refs[:na], refs[na], refs[na + 1:]
        xn = _rms_fwd(x_ref[...], g_ref[...]).astype(BF16)
        xn_ref[...] = xn
        for a, (w, l) in enumerate(ws):
            for p in range(w.shape[0]):
                o_refs[a][p] = jnp.dot(xn, w_refs[a][p, l], preferred_element_type=F32).astype(o_refs[a].dtype)

    in_specs = [pl.BlockSpec((tm, D), lambda i: (i, 0)), pl.BlockSpec((1, D), lambda i: (0, 0))]
    in_specs += [pl.BlockSpec(w.shape, lambda i: (0, 0, 0, 0), pipeline_mode=pl.Buffered(1)) for w, _ in ws]
    out_specs = [pl.BlockSpec((tm, D), lambda i: (i, 0))]
    out_shape = [jax.ShapeDtypeStruct((T, D), BF16)]
    for (w, _), dt in zip(ws, out_dtypes):
        nb, wb = w.shape[0], w.shape[3]
        out_specs.append(pl.BlockSpec((nb, tm, wb), lambda i: (0, i, 0)))
        out_shape.append(jax.ShapeDtypeStruct((nb, T, wb), dt))
    res = pl.pallas_call(
        body, name=name, grid=(T // tm,), in_specs=in_specs, out_specs=out_specs, out_shape=out_shape,
        compiler_params=_params(1),
    )(x, g, *[w for w, _ in ws])
    return res[0], res[1:]


def mix_out(pro, pro_inputs, pro_specs, w, res, g_post, alpha, nk, name, tm=None):
    w4, l = w
    T = res.shape[0]
    tm = _tile(T, TM, 16) if tm is None else tm
    dpb = N_DEV // nk
    rows = w4.shape[2]
    kb = dpb * rows
    npi = len(pro_inputs)

    def body(*refs):
        pro_refs = refs[:npi]
        w_ref, res_ref, g_ref, z_ref, m_ref, y_ref = refs[npi:]
        i = pl.program_id(0)
        m = None
        for k in range(nk):
            z = pro(i, k, *pro_refs).astype(BF16)
            z_ref[k] = z
            part = jnp.dot(z, w_ref[k * dpb:(k + 1) * dpb, l].reshape(kb, D), preferred_element_type=F32)
            m = part if m is None else m + part
        m_ref[...] = m
        y_ref[...] = res_ref[...] + alpha * _rms_fwd(m, g_ref[...])

    row = pl.BlockSpec((tm, D), lambda i: (i, 0))
    in_specs = list(pro_specs) + [
        pl.BlockSpec(w4.shape, lambda i: (0, 0, 0, 0), pipeline_mode=pl.Buffered(1)), row, pl.BlockSpec((1, D), lambda i: (0, 0))]
    z, m, y = pl.pallas_call(
        body, name=name, grid=(T // tm,), in_specs=in_specs,
        out_specs=[pl.BlockSpec((nk, tm, kb), lambda i: (0, i, 0)), row, row],
        out_shape=[jax.ShapeDtypeStruct((nk, T, kb), BF16), jax.ShapeDtypeStruct((T, D), F32),
                   jax.ShapeDtypeStruct((T, D), F32)],
        compiler_params=_params(1),
    )(*pro_inputs, w4, res, g_post)
    return z, m, y


def post_bwd(dy, m, g_post, alpha, w, nk, name, gu=None, tm=None):
    w4, l = w
    T = dy.shape[0]
    tm = _tile(T, TM, 16) if tm is None else tm
    dpb = N_DEV // nk
    rows = w4.shape[2]
    kb = dpb * rows
    swiglu = gu is not None

    def body(dy_ref, m_ref, g_ref, w_ref, *refs):
        if swiglu:
            gu_ref, dm_ref, dg_ref, dgate_ref, dup_ref = refs
        else:
            dm_ref, dg_ref, dz_ref = refs
        dm, dg = _rms_bwd(m_ref[...], g_ref[...], alpha * dy_ref[...])
        dm = dm.astype(BF16)
        dm_ref[...] = dm
        _accumulate(dg_ref, pl.program_id(0) == 0, dg)
        for k in range(nk):
            dz = lax.dot_general(dm, w_ref[k * dpb:(k + 1) * dpb, l].reshape(kb, D), NT, preferred_element_type=F32)
            if swiglu:
                gate, up = gu_ref[k].astype(F32), gu_ref[k + nk].astype(F32)
                sig = jax.nn.sigmoid(gate)
                dgate_ref[k] = (dz * up * sig * (1.0 + gate * (1.0 - sig))).astype(BF16)
                dup_ref[k] = (dz * gate * sig).astype(BF16)
            else:
                dz_ref[k] = dz.astype(BF16)

    row = pl.BlockSpec((tm, D), lambda i: (i, 0))
    vec = pl.BlockSpec((1, D), lambda i: (0, 0))
    blkk = pl.BlockSpec((nk, tm, kb), lambda i: (0, i, 0))
    in_specs = [row, row, vec, pl.BlockSpec(w4.shape, lambda i: (0, 0, 0, 0), pipeline_mode=pl.Buffered(1))]
    inputs = [dy, m, g_post, w4]
    out_specs = [row, vec, blkk]
    out_shape = [jax.ShapeDtypeStruct((T, D), BF16), jax.ShapeDtypeStruct((1, D), F32),
                 jax.ShapeDtypeStruct((nk, T, kb), BF16)]
    if swiglu:
        in_specs.append(pl.BlockSpec((2 * nk, tm, kb), lambda i: (0, i, 0)))
        inputs.append(gu)
        out_specs.append(blkk)
        out_shape.append(jax.ShapeDtypeStruct((nk, T, kb), BF16))
    return pl.pallas_call(
        body, name=name, grid=(T // tm,), in_specs=in_specs, out_specs=out_specs, out_shape=out_shape,
        compiler_params=_params(1),
    )(*inputs)


def pre_bwd(pieces, x, g_pre, dres, name):
    T = x.shape[0]
    tm = _tile(T, TM, 16)
    na = len(pieces)
    weights = []
    for _, (w4, _), _ in pieces:
        if not any(w4 is w for w in weights):
            weights.append(w4)
    nw = len(weights)
    which = [[w4 is w for w in weights].index(True) for _, (w4, _), _ in pieces]

    def body(*refs):
        dz_refs, w_refs = refs[:na], refs[na:na + nw]
        x_ref, g_ref, dres_ref, dx_ref, dg_ref = refs[na + nw:]
        acc = None
        for a, (dz, (_, l), w_off) in enumerate(pieces):
            for p in range(dz.shape[0]):
                part = lax.dot_general(dz_refs[a][p].astype(BF16), w_refs[which[a]][w_off + p, l], NT,
                                       preferred_element_type=F32)
                acc = part if acc is None else acc + part
        dx, dg = _rms_bwd(x_ref[...], g_ref[...], acc)
        dx_ref[...] = dres_ref[...] + dx
        _accumulate(dg_ref, pl.program_id(0) == 0, dg)

    row = pl.BlockSpec((tm, D), lambda i: (i, 0))
    vec = pl.BlockSpec((1, D), lambda i: (0, 0))
    dz_specs = [pl.BlockSpec((dz.shape[0], tm, dz.shape[2]), lambda i: (0, i, 0)) for dz, _, _ in pieces]
    w_specs = [pl.BlockSpec(w.shape, lambda i: (0, 0, 0, 0), pipeline_mode=pl.Buffered(1)) for w in weights]
    return pl.pallas_call(
        body, name=name, grid=(T // tm,), in_specs=dz_specs + w_specs + [row, vec, row],
        out_specs=[row, vec], out_shape=[jax.ShapeDtypeStruct((T, D), F32), jax.ShapeDtypeStruct((1, D), F32)],
        compiler_params=_params(1),
    )(*[p[0] for p in pieces], *weights, x, g_pre, dres)


def wgrad(a_list, b_list, nout, out4_shape, name):
    T = a_list[0].shape[1]
    tm = _tile(T, 4 * TM, 16)
    nt = T // tm
    dpb = out4_shape[0] // nout
    _, _, R, C = out4_shape
    na, nb = len(a_list), len(b_list)

    def spans(arrs):
        ns = [a.shape[0] for a in arrs]
        return ns, [sum(ns[:k]) for k in range(len(ns))], sum(ns)

    a_ns, a_offs, a_tot = spans(a_list)
    b_ns, b_offs, b_tot = spans(b_list)
    assert a_tot in (1, nout) and b_tot in (1, nout)

    def body(*refs):
        a_refs, b_refs = refs[:na], refs[na:na + nb]
        out_ref, acc = refs[-2:]
        p, t = pl.program_id(0), pl.program_id(1)
        for ia in range(na):
            for ib in range(nb):
                conds = []
                if a_tot > 1:
                    conds += [p >= a_offs[ia], p < a_offs[ia] + a_ns[ia]]
                if b_tot > 1:
                    conds += [p >= b_offs[ib], p < b_offs[ib] + b_ns[ib]]

                def work(ia=ia, ib=ib):
                    part = lax.dot_general(a_refs[ia][0], b_refs[ib][0], TN, preferred_element_type=F32)
                    _accumulate(acc, t == 0, part)

                if conds:
                    pl.when(functools.reduce(jnp.logical_and, conds))(work)
                else:
                    work()

        @pl.when(t == nt - 1)
        def _():
            out_ref[...] = acc[...].astype(BF16).reshape(dpb, 1, R, C)

    def blk(off, n, tot):
        if tot == 1:
            return lambda p: 0
        return lambda p: jnp.clip(p - off, 0, n - 1)

    in_specs = []
    for arrs, ns, offs, tot in ((a_list, a_ns, a_offs, a_tot), (b_list, b_ns, b_offs, b_tot)):
        for arr, n, off in zip(arrs, ns, offs):
            in_specs.append(pl.BlockSpec((1, tm, arr.shape[2]), lambda p, t, f=blk(off, n, tot): (f(p), t, 0)))
    return pl.pallas_call(
        body, name=name, grid=(nout, nt), in_specs=in_specs,
        out_specs=pl.BlockSpec((dpb, 1, R, C), lambda p, t: (p, 0, 0, 0)),
        out_shape=jax.ShapeDtypeStruct(out4_shape, BF16),
        scratch_shapes=[pltpu.VMEM((dpb * R, C), F32)], compiler_params=_params(2),
    )(*a_list, *b_list)


def _swiglu_pro(i, k, gu_ref):
    gate, up = gu_ref[k].astype(F32), gu_ref[k + gu_ref.shape[0] // 2].astype(F32)
    return gate * jax.nn.sigmoid(gate) * up


def _attn_gate_pro(i, k, gate_ref, o_ref):
    return jax.nn.sigmoid(gate_ref[0].astype(F32)) * o_ref[...].astype(F32)


def _shift_rows(u, halo, d):
    rolled = pltpu.roll(u, d, 0)
    row = lax.broadcasted_iota(jnp.int32, u.shape, 0)
    for r in range(d):
        rolled = jnp.where(row == r, halo[HALO - d + r:HALO - d + r + 1, :], rolled)
    return rolled


def _advance_rows(u, halo, d):
    n = u.shape[0]
    rolled = pltpu.roll(u, n - d, 0)
    row = lax.broadcasted_iota(jnp.int32, u.shape, 0)
    for r in range(d):
        rolled = jnp.where(row == n - d + r, halo[r:r + 1, :], rolled)
    return rolled


def _make_conv_pro(tiles_per_seq):
    def pro(i, k, b_ref, c_ref, h_ref, ch_ref, hh_ref, ck_ref):
        u = c_ref[0].astype(F32) * h_ref[0].astype(F32)
        first = (i % tiles_per_seq) == 0
        halo = jnp.where(first, 0.0, ch_ref[0].astype(F32) * hh_ref[0].astype(F32))
        ck = ck_ref[...]
        conv = ck[2:3, :] * u + ck[1:2, :] * _shift_rows(u, halo, 1) + ck[0:1, :] * _shift_rows(u, halo, 2)
        return b_ref[0].astype(F32) * conv
    return pro


def conv_bwd_mix(dz, bch, conv_k, seq_len, name):
    T = dz.shape[1]
    tm = _tile(seq_len, TM, 16)
    tps = seq_len // tm
    nt = T // tm
    hb = tm // HALO

    def body(dz_ref, b_ref, c_ref, h_ref, cp_ref, hp_ref, dzn_ref, bn_ref, ck_ref, dbch_ref, dk_ref):
        i = pl.program_id(0)
        first = (i % tps) == 0
        last = (i % tps) == tps - 1
        b, c, h = b_ref[0].astype(F32), c_ref[0].astype(F32), h_ref[0].astype(F32)
        dzt = dz_ref[0].astype(F32)
        u = c * h
        prev = jnp.where(first, 0.0, cp_ref[0].astype(F32) * hp_ref[0].astype(F32))
        u1, u2 = _shift_rows(u, prev, 1), _shift_rows(u, prev, 2)
        ck = ck_ref[...]
        conv = ck[2:3, :] * u + ck[1:2, :] * u1 + ck[0:1, :] * u2
        dconv = dzt * b
        nxt = jnp.where(last, 0.0, dzn_ref[0].astype(F32) * bn_ref[0].astype(F32))
        du = ck[2:3, :] * dconv + ck[1:2, :] * _advance_rows(dconv, nxt, 1) + ck[0:1, :] * _advance_rows(dconv, nxt, 2)
        dbch_ref[0] = (dzt * conv).astype(BF16)
        dbch_ref[1] = (du * h).astype(BF16)
        dbch_ref[2] = (du * c).astype(BF16)
        tap = lax.broadcasted_iota(jnp.int32, (3, D), 0)
        dk = jnp.where(tap == 0, jnp.sum(dconv * u2, axis=0, keepdims=True),
                       jnp.where(tap == 1, jnp.sum(dconv * u1, axis=0, keepdims=True),
                                 jnp.sum(dconv * u, axis=0, keepdims=True)))
        _accumulate(dk_ref, i == 0, dk)

    def piece(p):
        return pl.BlockSpec((1, tm, D), lambda i, p=p: (p, i, 0))

    def prev(p):
        return pl.BlockSpec((1, HALO, D), lambda i, p=p: (p, jnp.maximum(i * hb - 1, 0), 0))

    def nxt(p):
        return pl.BlockSpec((1, HALO, D), lambda i, p=p: (p, jnp.minimum((i + 1) * hb, nt * hb - 1), 0))

    return pl.pallas_call(
        body, name=name, grid=(nt,),
        in_specs=[piece(0), piece(0), piece(1), piece(2), prev(1), prev(2), nxt(0), nxt(0),
                  pl.BlockSpec((3, D), lambda i: (0, 0))],
        out_specs=[pl.BlockSpec((3, tm, D), lambda i: (0, i, 0)), pl.BlockSpec((3, D), lambda i: (0, 0))],
        out_shape=[jax.ShapeDtypeStruct((3, T, D), BF16), jax.ShapeDtypeStruct((3, D), F32)],
        compiler_params=_params(1),
    )(dz, bch, bch, bch, bch, bch, dz, bch, conv_k)


def _log_sigmoid(x):
    return jnp.minimum(x, 0.0) - jnp.log(1.0 + jnp.exp(-jnp.abs(x)))


def forget_fwd(fl, fb, seq_len, name):
    T = fl.shape[1]

    def body(fl_ref, fb_ref, c_ref):
        c = _log_sigmoid(fl_ref[0] + fb_ref[...])
        row = lax.broadcasted_iota(jnp.int32, c.shape, 0)
        k = 1
        while k < seq_len:
            c = c + jnp.where(row >= k, pltpu.roll(c, k, 0), 0.0)
            k *= 2
        c_ref[...] = c

    return pl.pallas_call(
        body, name=name, grid=(T // seq_len,),
        in_specs=[pl.BlockSpec((1, seq_len, LANES), lambda b: (0, b, 0)), pl.BlockSpec((1, LANES), lambda b: (0, 0))],
        out_specs=pl.BlockSpec((seq_len, LANES), lambda b: (b, 0)),
        out_shape=jax.ShapeDtypeStruct((T, LANES), F32), compiler_params=_params(1),
    )(fl, fb)


def forget_bwd(dc, fl, fb, seq_len, name):
    T = dc.shape[0]

    def body(dc_ref, fl_ref, fb_ref, dfl_ref, dfb_ref):
        b = pl.program_id(0)
        r = dc_ref[...]
        row = lax.broadcasted_iota(jnp.int32, r.shape, 0)
        k = 1
        while k < seq_len:
            r = r + jnp.where(row < seq_len - k, pltpu.roll(r, seq_len - k, 0), 0.0)
            k *= 2
        dfl = r * jax.nn.sigmoid(-(fl_ref[0] + fb_ref[...]))
        dfl_ref[0] = dfl
        _accumulate(dfb_ref, b == 0, jnp.sum(dfl, axis=0, keepdims=True))

    return pl.pallas_call(
        body, name=name, grid=(T // seq_len,),
        in_specs=[pl.BlockSpec((seq_len, LANES), lambda b: (b, 0)), pl.BlockSpec((1, seq_len, LANES), lambda b: (0, b, 0)),
                  pl.BlockSpec((1, LANES), lambda b: (0, 0))],
        out_specs=[pl.BlockSpec((1, seq_len, LANES), lambda b: (0, b, 0)), pl.BlockSpec((1, LANES), lambda b: (0, 0))],
        out_shape=[jax.ShapeDtypeStruct((1, T, LANES), F32), jax.ShapeDtypeStruct((1, LANES), F32)],
        compiler_params=_params(1),
    )(dc, fl, fb)


HEADS = (slice(0, HEAD_DIM), slice(HEAD_DIM, 2 * HEAD_DIM))


def attn_fwd(qg, kv, cb, n_seq, seq_len, name):
    T = n_seq * seq_len
    tq = _tile(seq_len, TQ, LANES)
    nq = seq_len // tq

    def body(q_ref, k_ref, v_ref, cb_ref, o_ref, lse_ref):
        i = pl.program_id(2)
        causal = lax.broadcasted_iota(jnp.int32, (tq, tq), 0) <= lax.broadcasted_iota(jnp.int32, (tq, tq), 1)
        q8 = [q_ref[0, :, sl] * ATT_SCALE for sl in HEADS]

        def block(j, carry, diagonal):
            rows = pl.ds(pl.multiple_of(j * tq, tq), tq)
            out = []
            for hh, sl in enumerate(HEADS):
                m, l, acc = carry[3 * hh:3 * hh + 3]
                s = lax.dot_general(k_ref[0, rows, sl], q8[hh], NT, preferred_element_type=F32)
                s = s - cb_ref[rows, sl.start:sl.start + 1]
                if diagonal:
                    s = jnp.where(causal, s, -1e30)
                m_new = jnp.maximum(m, jnp.max(s, axis=0, keepdims=True))
                a = jnp.exp(m - m_new)
                p = jnp.exp(s - m_new)
                l = a * l + jnp.sum(p, axis=0, keepdims=True)
                acc = a * acc + lax.dot_general(v_ref[0, rows, sl], p.astype(BF16), TN, preferred_element_type=F32)
                out += [m_new, l, acc]
            return tuple(out)

        init = (jnp.full((1, tq), -1e30, F32), jnp.zeros((1, tq), F32), jnp.zeros((HEAD_DIM, tq), F32)) * 2
        carry = lax.fori_loop(0, i, lambda j, c: block(j, c, False), init)
        carry = block(i, carry, True)
        o_ref[...] = jnp.concatenate([carry[2] / carry[1], carry[5] / carry[4]], axis=0).T
        for hh in range(2):
            lse_ref[0, 0, 0, hh:hh + 1, :] = carry[3 * hh] + jnp.log(carry[3 * hh + 1])

    seq2 = pl.BlockSpec((seq_len, LANES), lambda b, hp, i: (b, hp))
    return pl.pallas_call(
        body, name=name, grid=(n_seq, N_HEADS // 2, nq),
        in_specs=[pl.BlockSpec((1, tq, LANES), lambda b, hp, i: (0, b * nq + i, hp)),
                  pl.BlockSpec((1, seq_len, LANES), lambda b, hp, i: (0, b, hp)),
                  pl.BlockSpec((1, seq_len, LANES), lambda b, hp, i: (1, b, hp)), seq2],
        out_specs=[pl.BlockSpec((tq, LANES), lambda b, hp, i: (b * nq + i, hp)),
                   pl.BlockSpec((1, 1, 1, 2, tq), lambda b, hp, i: (b, hp, i, 0, 0))],
        out_shape=[jax.ShapeDtypeStruct((T, D), F32), jax.ShapeDtypeStruct((n_seq, N_HEADS // 2, nq, 2, tq), F32)],
        compiler_params=_params(3),
    )(qg, kv, kv, cb)


def attn_bwd(dz, qg, kv, o, lse, cb, n_seq, seq_len, name):
    T = n_seq * seq_len
    tq = _tile(seq_len, TQ, LANES)
    nq = seq_len // tq

    def body(dz_ref, q_ref, gate_ref, o_ref, lse_ref, cb_ref, k_ref, v_ref,
             dq_ref, dgate_ref, dk_ref, dv_ref, dc_ref, p_s, dp_s, dk_s, dv_s, dc_s):
        i = pl.program_id(2)

        @pl.when(i == 0)
        def _():
            dk_s[...] = jnp.zeros_like(dk_s)
            dv_s[...] = jnp.zeros_like(dv_s)
            dc_s[...] = jnp.zeros_like(dc_s)

        dzf = dz_ref[...].astype(F32)
        sig = jax.nn.sigmoid(gate_ref[0].astype(F32))
        dob = (dzf * sig).astype(BF16)
        dgate_ref[0] = (dzf * o_ref[...] * sig * (1.0 - sig)).astype(BF16)
        causal = lax.broadcasted_iota(jnp.int32, (tq, tq), 0) <= lax.broadcasted_iota(jnp.int32, (tq, tq), 1)
        q8 = [q_ref[0, :, sl] * ATT_SCALE for sl in HEADS]
        do = [dob[:, sl] for sl in HEADS]
        lse_i = [lse_ref[0, 0, 0, hh:hh + 1, :] for hh in range(2)]

        def probs(j, dsums, diagonal):
            rows = pl.ds(pl.multiple_of(j * tq, tq), tq)
            out = []
            for hh, sl in enumerate(HEADS):
                s = lax.dot_general(k_ref[0, rows, sl], q8[hh], NT, preferred_element_type=F32)
                p = jnp.exp(s - cb_ref[rows, sl.start:sl.start + 1] - lse_i[hh])
                if diagonal:
                    p = jnp.where(causal, p, 0.0)
                dp = lax.dot_general(v_ref[0, rows, sl], do[hh], NT, preferred_element_type=F32)
                p_s[hh, j] = p
                dp_s[hh, j] = dp
                out.append(dsums[hh] + jnp.sum(p * dp, axis=0, keepdims=True))
            return tuple(out)

        dsums = lax.fori_loop(0, i, lambda j, c: probs(j, c, False), (jnp.zeros((1, tq), F32),) * 2)
        dsums = probs(i, dsums, True)

        def grads(j, dqs):
            rows = pl.ds(pl.multiple_of(j * tq, tq), tq)
            out = []
            for hh, sl in enumerate(HEADS):
                p = p_s[hh, j]
                ds = p * (dp_s[hh, j] - dsums[hh])
                dc_s[hh, rows, :] -= jnp.sum(ds, axis=1, keepdims=True)
                dsb = ds.astype(BF16)
                dk_s[rows, sl] += jnp.dot(dsb, q8[hh], preferred_element_type=F32)
                dv_s[rows, sl] += jnp.dot(p.astype(BF16), do[hh], preferred_element_type=F32)
                out.append(dqs[hh] + lax.dot_general(k_ref[0, rows, sl], dsb, TN, preferred_element_type=F32))
            return tuple(out)

        dqs = lax.fori_loop(0, i + 1, grads, (jnp.zeros((HEAD_DIM, tq), F32),) * 2)
        dq_ref[0] = (jnp.concatenate(dqs, axis=0).T * ATT_SCALE).astype(BF16)

        @pl.when(i == nq - 1)
        def _():
            dk_ref[0] = dk_s[...].astype(BF16)
            dv_ref[0] = dv_s[...].astype(BF16)
            dc_ref[...] = jnp.zeros_like(dc_ref)
            for hh, sl in enumerate(HEADS):
                dc_ref[:, sl.start:sl.start + 1] = dc_s[hh]

    qry2 = pl.BlockSpec((tq, LANES), lambda b, hp, i: (b * nq + i, hp))
    seq2 = pl.BlockSpec((seq_len, LANES), lambda b, hp, i: (b, hp))

    def qry3(p):
        return pl.BlockSpec((1, tq, LANES), lambda b, hp, i, p=p: (p, b * nq + i, hp))

    def seq3(p):
        return pl.BlockSpec((1, seq_len, LANES), lambda b, hp, i, p=p: (p, b, hp))

    act = jax.ShapeDtypeStruct((1, T, D), BF16)
    return pl.pallas_call(
        body, name=name, grid=(n_seq, N_HEADS // 2, nq),
        in_specs=[qry2, qry3(0), qry3(1), qry2, pl.BlockSpec((1, 1, 1, 2, tq), lambda b, hp, i: (b, hp, i, 0, 0)), seq2,
                  seq3(0), seq3(1)],
        out_specs=[qry3(0), qry3(0), seq3(0), seq3(0), seq2],
        out_shape=[act, act, act, act, jax.ShapeDtypeStruct((T, D), F32)],
        scratch_shapes=[pltpu.VMEM((2, nq, tq, tq), F32), pltpu.VMEM((2, nq, tq, tq), F32),
                        pltpu.VMEM((seq_len, LANES), F32), pltpu.VMEM((seq_len, LANES), F32),
                        pltpu.VMEM((2, seq_len, 1), F32)],
        compiler_params=_params(3),
    )(dz, qg, qg, o, lse, cb, kv, kv)


def loss_head(y, target, name):
    T = y.shape[0]
    tm = _tile(T, TM, 8)

    def body(y_ref, t_ref, dy_ref, loss_ref):
        err = y_ref[...] - t_ref[...]
        dy_ref[...] = err * (1.0 / D)
        part = 0.5 * jnp.sum(jnp.mean(err * err, axis=-1, keepdims=True), axis=0, keepdims=True)
        _accumulate(loss_ref, pl.program_id(0) == 0, jnp.broadcast_to(part, (1, LANES)))

    row = pl.BlockSpec((tm, D), lambda i: (i, 0))
    return pl.pallas_call(
        body, name=name, grid=(T // tm,), in_specs=[row, row],
        out_specs=[row, pl.BlockSpec((1, LANES), lambda i: (0, 0))],
        out_shape=[jax.ShapeDtypeStruct((T, D), F32), jax.ShapeDtypeStruct((1, LANES), F32)],
        compiler_params=_params(1),
    )(y, target)


def _adamw(w, g, m, v):
    m = ADAM_B1 * m + (1.0 - ADAM_B1) * g
    v = ADAM_B2 * v + (1.0 - ADAM_B2) * (g * g)
    m_hat = m / (1.0 - ADAM_B1 ** ADAM_STEP)
    v_hat = v / (1.0 - ADAM_B2 ** ADAM_STEP)
    delta = -ADAM_LR * (m_hat / (jnp.sqrt(v_hat) + ADAM_EPS) + ADAM_WD * w)
    return delta, m, v


def adam_sharded(w, m, v, contribs, name):
    L, R, C = w.shape
    tr = _tile(R, 256, 16) if R % 16 == 0 else R

    def body(w_ref, m_ref, v_ref, *refs):
        c_refs, (g_ref, d_ref, nm_ref, nv_ref) = refs[:2 * L], refs[2 * L:]
        l = pl.program_id(0)
        for j in range(L):
            @pl.when(l == j)
            def _(j=j):
                own_ref, recv_ref = c_refs[2 * j], c_refs[2 * j + 1]
                g = own_ref[...].astype(F32)
                for k in range(N_DEV - 1):
                    g = g + recv_ref[k].astype(F32)
                delta, nm, nv = _adamw(w_ref[0], g, m_ref[0], v_ref[0])
                g_ref[0] = g
                d_ref[0] = delta
                nm_ref[0] = nm
                nv_ref[0] = nv

    blk = pl.BlockSpec((1, tr, C), lambda l, i: (l, i, 0))
    in_specs = [blk, blk, blk]
    inputs = [w, m, v]
    for j, (own, recv) in enumerate(contribs):
        in_specs.append(pl.BlockSpec((tr, C), lambda l, i, j=j: (jnp.where(l == j, i, 0), 0)))
        in_specs.append(pl.BlockSpec((N_DEV - 1, tr, C), lambda l, i, j=j: (0, jnp.where(l == j, i, 0), 0)))
        inputs += [own, recv]
    shp = jax.ShapeDtypeStruct((L, R, C), F32)
    return pl.pallas_call(
        body, name=name, grid=(L, R // tr), in_specs=in_specs, out_specs=[blk] * 4, out_shape=[shp] * 4,
        compiler_params=_params(2),
    )(*inputs)


def adam_small(params, total, extra_grads, name):
    n, ne = len(params), len(extra_grads)

    def body(*refs):
        total_ref, extra_refs = refs[0], refs[1:1 + ne]
        ins, outs = refs[1 + ne:1 + ne + 3 * n], refs[1 + ne + 3 * n:]
        for k, (_, _, _, where) in enumerate(params):
            if isinstance(where, int):
                g = extra_refs[where][...]
            else:
                row, rows, width = where
                g = total_ref[row:row + rows, 0:width]
            delta, nm, nv = _adamw(ins[3 * k][...], g, ins[3 * k + 1][...], ins[3 * k + 2][...])
            outs[4 * k][...] = g
            outs[4 * k + 1][...] = delta
            outs[4 * k + 2][...] = nm
            outs[4 * k + 3][...] = nv

    flat = [a for w, m, v, _ in params for a in (w, m, v)]
    out_shape = [jax.ShapeDtypeStruct(w.shape, F32) for w, _, _, _ in params for _ in range(4)]
    res = pl.pallas_call(body, name=name, out_shape=out_shape)(total, *extra_grads, *flat)
    return [res[4 * k:4 * k + 4] for k in range(n)]


def _place():
    return lax.axis_index("x"), lax.axis_index("y"), lax.axis_index("c")


def _peer(place, k):
    x, y, c = place
    return x ^ (k >> 2), y ^ ((k >> 1) & 1), c ^ (k & 1)


ANY = pl.BlockSpec(memory_space=pl.ANY)
HBM = pl.BlockSpec(memory_space=pltpu.HBM)
SEM = pl.BlockSpec(memory_space=pltpu.SEMAPHORE)
EFFECT = pltpu.SideEffectType.DATAFLOW_SIDE_EFFECTING


def _in_hbm(a):
    return pltpu.with_memory_space_constraint(a, pltpu.HBM)


def _number(place):
    return 4 * place[0] + 2 * place[1] + place[2]


SLOTS = {"gather_like": 4, "gather_pass": 3, "scatter": 7}


def _plan(mode, src_ref, land_ref, place):
    if mode == "gather_like":
        return [(src_ref, land_ref.at[_number(place)], _peer(place, k)) for k in (1, 2, 4, 6)]
    if mode == "gather_pass":
        slots = [land_ref.at[_number(_peer(place, k))] for k in (2, 4, 6)]
        return [(slot, slot, _peer(place, 1)) for slot in slots]
    return [(src_ref.at[_number(_peer(place, k))], land_ref.at[k - 1], _peer(place, k)) for k in range(1, N_DEV)]


def _exchange(name, groups, start, afters):
    sizes = [len(g[2]) for g in groups]
    na, ng = sum(sizes), len(groups)
    waits = groups[0][0] is not None
    n_in_sems = 2 * ng if waits else 0
    n_out_sems = 2 * ng if start else 0

    def body(*refs):
        src_refs, land_refs = refs[:na], refs[na:2 * na]
        in_sems = refs[2 * na:2 * na + n_in_sems]
        outs = refs[2 * na + n_in_sems + len(afters):]
        place = _place()
        a = 0
        for gi, n in enumerate(sizes):
            for idx in range(n):
                if waits:
                    zone = land_refs[a].at[pl.ds(0, groups[gi][4])]
                    copy = pltpu.make_async_remote_copy(
                        src_ref=zone, dst_ref=zone, send_sem=in_sems[2 * gi].at[idx], recv_sem=in_sems[2 * gi + 1].at[idx],
                        device_id=_peer(place, 1), device_id_type=MESH)
                    copy.wait_send()
                    copy.wait_recv()
                if start:
                    for src, dst, peer in _plan(start, src_refs[a], land_refs[a], place):
                        pltpu.make_async_remote_copy(
                            src_ref=src, dst_ref=dst, send_sem=outs[2 * gi].at[idx], recv_sem=outs[2 * gi + 1].at[idx],
                            device_id=peer, device_id_type=MESH).start()
                a += 1
        if start:
            outs[-1][...] = jnp.zeros_like(outs[-1])

    srcs = [_in_hbm(s) for g in groups for s in g[2]]
    lands = [_in_hbm(l) for g in groups for l in g[3]]
    sems = [s for g in groups for s in g[:2]] if waits else []
    out_shape = [pltpu.SemaphoreType.DMA((n,)) for n in sizes for _ in range(2)] if start else []
    out_shape += [pltpu.HBM(a.shape, a.dtype) for a in srcs + lands]
    out_specs = [SEM] * n_out_sems + [HBM] * (2 * na)
    if start:
        out_shape.append(jax.ShapeDtypeStruct((8, LANES), F32))
        out_specs.append(pl.BlockSpec(memory_space=pltpu.VMEM))
    res = pl.pallas_call(
        body, name=name, in_specs=[HBM] * (2 * na) + [SEM] * n_in_sems + [ANY] * len(afters),
        out_shape=out_shape, out_specs=out_specs,
        input_output_aliases={i: n_out_sems + i for i in range(2 * na)},
        compiler_params=pltpu.CompilerParams(has_side_effects=EFFECT),
    )(*srcs, *lands, *sems, *afters)
    new_sems, thru = res[:n_out_sems], res[n_out_sems:n_out_sems + 2 * na]
    out, a = [], 0
    for gi, n in enumerate(sizes):
        pair = (new_sems[2 * gi], new_sems[2 * gi + 1]) if start else (None, None)
        out.append(pair + (thru[a:a + n], thru[na + a:na + a + n], SLOTS.get(start, 0)))
        a += n
    return out, (res[-1] if start else None)


def exchange_start(pair_groups, mode, name):
    groups = [(None, None, [s for s, _ in g], [l for _, l in g], 0) for g in pair_groups]
    return _exchange(name, groups, mode, ())


def exchange_relay(groups, mode, afters, name):
    return _exchange(name, groups, mode, afters)


def exchange_wait(groups, afters, name):
    done, _ = _exchange(name, groups, None, afters)
    return [(g[2], g[3]) for g in done]


def all_reduce_small(parts, n_rows, name):
    R = n_rows
    n_parts = len(parts)

    def body(*refs):
        part_refs = refs[:n_parts]
        out_ref, buf, send_sems, recv_sems = refs[n_parts:]
        x, y, c = _place()
        me = 4 * x + 2 * y + c
        own = buf.at[me]
        own[...] = jnp.zeros((R, D), F32)
        for ref, (arr, row) in zip(part_refs, parts):
            own[row:row + arr.shape[0], 0:arr.shape[1]] = ref[...]
        copies = []
        for k in range(1, N_DEV):
            peer = (x ^ (k >> 2), y ^ ((k >> 1) & 1), c ^ (k & 1))
            copies.append(pltpu.make_async_remote_copy(
                src_ref=own, dst_ref=own, send_sem=send_sems.at[k - 1], recv_sem=recv_sems.at[k - 1],
                device_id=peer, device_id_type=MESH))
        for cp in copies:
            cp.start()
        for cp in copies:
            cp.wait()
        total = buf[0]
        for d in range(1, N_DEV):
            total = total + buf[d]
        out_ref[...] = total

    vm = pl.BlockSpec(memory_space=pltpu.VMEM)
    return pl.pallas_call(
        body, name=name, in_specs=[vm] * n_parts, out_specs=vm, out_shape=jax.ShapeDtypeStruct((R, D), F32),
        scratch_shapes=[pltpu.VMEM((N_DEV, R, D), F32), pltpu.SemaphoreType.DMA((N_DEV - 1,)),
                        pltpu.SemaphoreType.DMA((N_DEV - 1,))],
    )(*[arr for arr, _ in parts])


def _col_blocks(gathered, n_blocks):
    n, d, w = gathered.shape
    whole = gathered.transpose(1, 0, 2).reshape(d, n * w)
    return whole.reshape(d, n_blocks, n * w // n_blocks).transpose(1, 0, 2)[:, None]


def _col_shards(blocks):
    n, d, w = blocks.shape
    whole = blocks.transpose(1, 0, 2).reshape(d, n * w)
    return whole.reshape(d, N_DEV, n * w // N_DEV).transpose(1, 0, 2)


def kernel(x, ffn1_pre_g, ffn1_post_g, ffn1_w_in, ffn1_w_out, mix_pre_g, mix_post_g, ffn2_pre_g, ffn2_post_g, ffn2_w_in, ffn2_w_out, conv_w_in, conv_k, conv_w_out, kv_g, kv_w, forget_b, attn_w_qg, attn_w_o, loss_target, m_ffn1_pre_g, m_ffn1_post_g, m_ffn1_w_in, m_ffn1_w_out, m_mix_pre_g, m_mix_post_g, m_ffn2_pre_g, m_ffn2_post_g, m_ffn2_w_in, m_ffn2_w_out, m_conv_w_in, m_conv_k, m_conv_w_out, m_kv_g, m_kv_w, m_forget_b, m_attn_w_qg, m_attn_w_o, v_ffn1_pre_g, v_ffn1_post_g, v_ffn1_w_in, v_ffn1_w_out, v_mix_pre_g, v_mix_post_g, v_ffn2_pre_g, v_ffn2_post_g, v_ffn2_w_in, v_ffn2_w_out, v_conv_w_in, v_conv_k, v_conv_w_out, v_kv_g, v_kv_w, v_forget_b, v_attn_w_qg, v_attn_w_o):
    n_seq, seq_len, _ = x.shape
    T = n_seq * seq_len
    xi, yi, ci = _place()
    dev = 4 * xi + 2 * yi + ci
    x0 = x.reshape(T, D)
    target = loss_target.reshape(T, D)

    def layer(w, l):
        return w[l:l + 1].astype(BF16)

    shard_groups = [
        [layer(ffn1_w_in, 0), layer(ffn1_w_out, 0)],
        [conv_w_in.astype(BF16), conv_w_out.astype(BF16), conv_k],
        [layer(ffn2_w_in, 0), layer(ffn2_w_out, 0)],
        [kv_w.astype(BF16), layer(ffn1_w_in, 1), layer(ffn1_w_out, 1), attn_w_qg.astype(BF16), attn_w_o.astype(BF16)],
        [layer(ffn2_w_in, 1), layer(ffn2_w_out, 1)]]
    def zone(s):
        return lax.dynamic_update_slice(lax.empty((N_DEV,) + s.shape, s.dtype), s[None], (dev,) + (0,) * s.ndim)

    gathers, token = exchange_start([[(s, zone(s)) for s in g] for g in shard_groups], "gather_like", "gather_start")

    def gathered(k, after):
        passed, _ = exchange_relay([gathers[k]], "gather_pass", [after], f"gather_pass{k}")
        return exchange_wait(passed, [after], f"gather_wait{k}")[0][1]

    fb = jnp.pad(forget_b, (0, LANES - N_HEADS))[None]

    def vec(g, l):
        return g[l:l + 1]

    def behind(g, tok):
        return g + tok[:1, :1]

    grads_small = {}
    tm_ffn = _tile(T, TM // 2, 16)

    def ffn_fwd(xin, g_pre, g_post, w_in, w_out, tag):
        xn, (gu,) = rms_proj(xin, g_pre, [(w_in, 0)], [BF16], f"{tag}_in")
        specs = [pl.BlockSpec((gu.shape[0], tm_ffn, gu.shape[2]), lambda i: (0, i, 0))]
        a, h, y = mix_out(_swiglu_pro, [gu], specs, (w_out, 0), xin, g_post, 0.5, 4, f"{tag}_out", tm=tm_ffn)
        return y, (xin, xn, gu, a, h)

    def ffn_bwd(dy, saved, g_pre, g_post, w_in, w_out, tag):
        xin, xn, gu, a, h = saved
        dh, dg_post, dgate, dup = post_bwd(dy, h, g_post, 0.5, (w_out, 0), 4, f"{tag}_bwd_out", gu=gu, tm=tm_ffn)
        dw_out = wgrad([a], [dh[None]], 4, w_out.shape, f"{tag}_dw_out")
        dw_in = wgrad([dgate, dup], [xn[None]], 8, (8, 1, w_in.shape[3], D), f"{tag}_dw_in")
        started, tok = scatter_start([dw_in.reshape(8, -1, D), dw_out.reshape(8, -1, D)], f"{tag}_scatter_start")
        dx, dg_pre = pre_bwd([(dgate, (w_in, 0), 0), (dup, (w_in, 0), 4)], xin, behind(g_pre, tok), dy, f"{tag}_bwd_in")
        return dx, dg_pre, dg_post, started

    def scatter_start(blocked, name):
        pairs = [(g, lax.empty((N_DEV - 1,) + g.shape[1:], g.dtype)) for g in blocked]
        started, tok = exchange_start([pairs], "scatter", name)
        return started[0], tok

    w1_in, w1_out = gathered(0, token)
    x1, s_f1a = ffn_fwd(x0, vec(ffn1_pre_g, 0), vec(ffn1_post_g, 0), w1_in, w1_out, "l0_ffn1")
    cw_in_g, cw_out, ck_g = gathered(1, x1)
    cw_in = _col_blocks(cw_in_g[:, 0], 3)
    ck = ck_g[:, 0].transpose(1, 0, 2).reshape(3, D)
    xn_c, (bch,) = rms_proj(x1, vec(mix_pre_g, 0), [(cw_in, 0)], [BF16], "conv_in")
    tmc = _tile(seq_len, TM, 16)
    hb = tmc // HALO

    def cpiece(p):
        return pl.BlockSpec((1, tmc, D), lambda i, p=p: (p, i, 0))

    def chalo(p):
        return pl.BlockSpec((1, HALO, D), lambda i, p=p: (p, jnp.maximum(i * hb - 1, 0), 0))

    conv_specs = [cpiece(0), cpiece(1), cpiece(2), chalo(1), chalo(2), pl.BlockSpec((3, D), lambda i: (0, 0))]
    z_c, m_c, x2 = mix_out(_make_conv_pro(seq_len // tmc), [bch, bch, bch, bch, bch, ck], conv_specs, (cw_out, 0),
                           x1, vec(mix_post_g, 0), 1.0, 1, "conv_out", tm=tmc)
    w2_in, w2_out = gathered(2, x2)
    x3, s_f2a = ffn_fwd(x2, vec(ffn2_pre_g, 0), vec(ffn2_post_g, 0), w2_in, w2_out, "l0_ffn2")

    kvw_g, w1_in_b, w1_out_b, qgw_g, ow = gathered(3, x3)
    qg_w = _col_blocks(qgw_g[:, 0], 2)
    kv_whole = kvw_g.transpose(1, 0, 2).reshape(D, 2 * D + N_HEADS)
    kv_wb = kv_whole[:, :2 * D].reshape(D, 2, D).transpose(1, 0, 2)[:, None]
    f_w = jnp.pad(kv_whole[:, 2 * D:], ((0, 0), (0, LANES - N_HEADS)))[None, None]
    xn_kv, (kv, fl) = rms_proj(x3, kv_g[None], [(kv_wb, 0), (f_w, 0)], [BF16, F32], "kv_in")
    c128 = forget_fwd(fl, fb, seq_len, "forget_fwd")
    cb = jnp.repeat(c128[:, :N_HEADS], HEAD_DIM, axis=1)

    x4, s_f1b = ffn_fwd(x3, vec(ffn1_pre_g, 1), vec(ffn1_post_g, 1), w1_in_b, w1_out_b, "l1_ffn1")
    xn_a, (qg,) = rms_proj(x4, vec(mix_pre_g, 1), [(qg_w, 0)], [BF16], "attn_in")
    o, lse = attn_fwd(qg, kv, cb, n_seq, seq_len, "attn_fwd")
    tm = _tile(T, TM, 16)
    gate_specs = [pl.BlockSpec((1, tm, D), lambda i: (1, i, 0)), pl.BlockSpec((tm, D), lambda i: (i, 0))]
    z_a, m_a, x5 = mix_out(_attn_gate_pro, [qg, o], gate_specs, (ow, 0), x4, vec(mix_post_g, 1), 1.0, 1, "attn_out")
    w2_in_b, w2_out_b = gathered(4, x5)
    x6, s_f2b = ffn_fwd(x5, vec(ffn2_pre_g, 1), vec(ffn2_post_g, 1), w2_in_b, w2_out_b, "l1_ffn2")

    dy, loss_part = loss_head(x6, target, "loss_head")

    scatters = {}
    dx5, dg, dgp, scatters["ffn2", 1] = ffn_bwd(dy, s_f2b, vec(ffn2_pre_g, 1), vec(ffn2_post_g, 1), w2_in_b, w2_out_b, "l1_ffn2")
    grads_small["ffn2_pre", 1], grads_small["ffn2_post", 1] = dg, dgp
    dm_a, dgp, dz_a = post_bwd(dx5, m_a, vec(mix_post_g, 1), 1.0, (ow, 0), 1, "attn_bwd_out")
    grads_small["mix_post", 1] = dgp
    dq, dgate, dk, dv, dcb = attn_bwd(dz_a[0], qg, kv, o, lse, cb, n_seq, seq_len, "attn_bwd")
    dx4, dg = pre_bwd([(dq, (qg_w, 0), 0), (dgate, (qg_w, 0), 1)], x4, vec(mix_pre_g, 1), dx5, "attn_bwd_in")
    grads_small["mix_pre", 1] = dg
    d_ow = wgrad([z_a], [dm_a[None]], 1, ow.shape, "attn_dw_o")
    d_qgw = wgrad([xn_a[None]], [dq, dgate], 2, (2, 1, D, D), "attn_dw_qg")
    scatters["attn"], tok = scatter_start([_col_shards(d_qgw[:, 0]), d_ow.reshape(8, -1, D)], "attn_scatter_start")
    dx3, dg, dgp, scatters["ffn1", 1] = ffn_bwd(dx4, s_f1b, vec(ffn1_pre_g, 1), behind(vec(ffn1_post_g, 1), tok),
                                                w1_in_b, w1_out_b, "l1_ffn1")
    grads_small["ffn1_pre", 1], grads_small["ffn1_post", 1] = dg, dgp

    dc16 = dcb.reshape(T, N_HEADS, HEAD_DIM)[:, :, 0]
    dfl, dfb = forget_bwd(jnp.pad(dc16, ((0, 0), (0, LANES - N_HEADS))), fl, fb, seq_len, "forget_bwd")
    dx3, dg_kv = pre_bwd([(dk, (kv_wb, 0), 0), (dv, (kv_wb, 0), 1), (dfl, (f_w, 0), 0)], x3, kv_g[None], dx3, "kv_bwd_in")
    d_kvw = wgrad([xn_kv[None]], [dk, dv], 2, (2, 1, D, D), "kv_dw")
    d_fw = wgrad([xn_kv[None]], [dfl.astype(BF16)], 1, (1, 1, D, LANES), "forget_dw")
    d_kv_whole = jnp.concatenate([d_kvw[0, 0], d_kvw[1, 0], d_fw[0, 0, :, :N_HEADS]], axis=1)
    wshard = D * 2 + N_HEADS
    scatters["kv"], tok = scatter_start([d_kv_whole.T.reshape(N_DEV, wshard // N_DEV, D)], "kv_scatter_start")

    dx2, dg, dgp, scatters["ffn2", 0] = ffn_bwd(dx3, s_f2a, vec(ffn2_pre_g, 0), behind(vec(ffn2_post_g, 0), tok),
                                                w2_in, w2_out, "l0_ffn2")
    grads_small["ffn2_pre", 0], grads_small["ffn2_post", 0] = dg, dgp
    dm_c, dgp, dz_c = post_bwd(dx2, m_c, vec(mix_post_g, 0), 1.0, (cw_out, 0), 1, "conv_bwd_out")
    grads_small["mix_post", 0] = dgp
    dbch, d_ck = conv_bwd_mix(dz_c, bch, ck, seq_len, "conv_bwd_mix")
    dx1, dg = pre_bwd([(dbch, (cw_in, 0), 0)], x1, vec(mix_pre_g, 0), dx2, "conv_bwd_in")
    grads_small["mix_pre", 0] = dg
    d_cw_out = wgrad([z_c], [dm_c[None]], 1, cw_out.shape, "conv_dw_out")
    d_cw_in = wgrad([xn_c[None]], [dbch], 3, (3, 1, D, D), "conv_dw_in")
    scatters["conv"], tok = scatter_start([_col_shards(d_cw_in[:, 0]), d_cw_out.reshape(8, -1, D)], "conv_scatter_start")
    dx0, dg, dgp, scatters["ffn1", 0] = ffn_bwd(dx1, s_f1a, vec(ffn1_pre_g, 0), behind(vec(ffn1_post_g, 0), tok),
                                                w1_in, w1_out, "l0_ffn1")
    grads_small["ffn1_pre", 0], grads_small["ffn1_post", 0] = dg, dgp

    parts_of = {}

    def scatter_end(keys, afters, name):
        for k, (sent, recv) in zip(keys, exchange_wait([scatters[k] for k in keys], afters, name)):
            parts_of[k] = [(lax.dynamic_index_in_dim(g, dev, 0, keepdims=False), r) for g, r in zip(sent, recv)]

    scatter_end([("ffn2", 1), "attn", ("ffn1", 1), "kv", ("ffn2", 0), "conv"], [dx0], "scatter_wait")
    sharded = {"ffn2_w_in": (ffn2_w_in, m_ffn2_w_in, v_ffn2_w_in), "ffn2_w_out": (ffn2_w_out, m_ffn2_w_out, v_ffn2_w_out),
               "conv_w_in": (conv_w_in, m_conv_w_in, v_conv_w_in), "conv_w_out": (conv_w_out, m_conv_w_out, v_conv_w_out),
               "kv_w": (kv_w, m_kv_w, v_kv_w), "attn_w_qg": (attn_w_qg, m_attn_w_qg, v_attn_w_qg),
               "attn_w_o": (attn_w_o, m_attn_w_o, v_attn_w_o),
               "ffn1_w_in": (ffn1_w_in, m_ffn1_w_in, v_ffn1_w_in), "ffn1_w_out": (ffn1_w_out, m_ffn1_w_out, v_ffn1_w_out)}
    out = {}
    for nm, (w, mm, vv) in sharded.items():
        if nm == "ffn1_w_in":
            scatter_end([("ffn1", 0)], [res[0] for res in out.values()], "scatter_wait_last")
        contribs = {
            "ffn1_w_in": lambda: [parts_of["ffn1", 0][0], parts_of["ffn1", 1][0]],
            "ffn1_w_out": lambda: [parts_of["ffn1", 0][1], parts_of["ffn1", 1][1]],
            "ffn2_w_in": lambda: [parts_of["ffn2", 0][0], parts_of["ffn2", 1][0]],
            "ffn2_w_out": lambda: [parts_of["ffn2", 0][1], parts_of["ffn2", 1][1]],
            "conv_w_in": lambda: [parts_of["conv"][0]], "conv_w_out": lambda: [parts_of["conv"][1]],
            "kv_w": lambda: [parts_of["kv"][0]], "attn_w_qg": lambda: [parts_of["attn"][0]],
            "attn_w_o": lambda: [parts_of["attn"][1]]}[nm]()
        if nm in ("ffn1_w_in", "ffn2_w_in", "kv_w"):
            rows, cols = w.shape[-2:]

            def view(a):
                return a.reshape(-1, rows, cols).transpose(0, 2, 1)

            res = adam_sharded(view(w), view(mm), view(vv), contribs, f"adam_{nm}")
            out[nm] = [r.transpose(0, 2, 1).reshape(w.shape) for r in res]
        else:
            shape3 = (len(contribs),) + contribs[0][0].shape
            res = adam_sharded(w.reshape(shape3), mm.reshape(shape3), vv.reshape(shape3), contribs, f"adam_{nm}")
            out[nm] = [r.reshape(w.shape) for r in res]

    small_names = ["ffn1_pre", "ffn1_post", "mix_pre", "mix_post", "ffn2_pre", "ffn2_post"]
    parts = [(grads_small[n, l], 2 * k + l) for k, n in enumerate(small_names) for l in range(2)]
    parts += [(dg_kv, 12), (dfb, 13), (d_ck, 14), (loss_part, 17)]
    total = all_reduce_small(parts, 24, "all_reduce_small")
    loss = total[17, 0]
    d_ck_mine = lax.dynamic_slice(total, (14, dev * LANES), (3, LANES))
    gains = [(ffn1_pre_g, m_ffn1_pre_g, v_ffn1_pre_g), (ffn1_post_g, m_ffn1_post_g, v_ffn1_post_g),
             (mix_pre_g, m_mix_pre_g, v_mix_pre_g), (mix_post_g, m_mix_post_g, v_mix_post_g),
             (ffn2_pre_g, m_ffn2_pre_g, v_ffn2_pre_g), (ffn2_post_g, m_ffn2_post_g, v_ffn2_post_g)]
    small_params = [(w, m, v, (2 * k, 2, D)) for k, (w, m, v) in enumerate(gains)]
    small_params += [(kv_g[None], m_kv_g[None], v_kv_g[None], (12, 1, D)),
                     (forget_b[None], m_forget_b[None], v_forget_b[None], (13, 1, N_HEADS)),
                     (conv_k[0], m_conv_k[0], v_conv_k[0], 0)]
    small_res = adam_small(small_params, total, [d_ck_mine], "adam_small")
    small_keys = [n + "_g" for n in small_names] + ["kv_g", "forget_b", "conv_k"]
    shapes = {"kv_g": kv_g.shape, "forget_b": forget_b.shape, "conv_k": conv_k.shape}
    small = [{key: res[kind].reshape(shapes.get(key, res[kind].shape)) for key, res in zip(small_keys, small_res)}
             for kind in range(4)]
    order = ["ffn1_pre_g", "ffn1_post_g", "ffn1_w_in", "ffn1_w_out", "mix_pre_g", "mix_post_g", "ffn2_pre_g", "ffn2_post_g",
             "ffn2_w_in", "ffn2_w_out", "conv_w_in", "conv_k", "conv_w_out", "kv_g", "kv_w", "forget_b", "attn_w_qg",
             "attn_w_o"]
    results = [loss, dx0.reshape(x.shape)]
    for kind in range(4):
        for nm in order:
            results.append(out[nm][kind] if nm in out else small[kind][nm])
    return tuple(results)
```

```python
import functools
import math

import jax
import jax.numpy as jnp
from jax import lax
from jax.experimental import pallas as pl
from jax.experimental.pallas import tpu as pltpu

F32, BF16 = jnp.float32, jnp.bfloat16
D = 1024
N_HEADS = 16
HEAD_DIM = 64
N_DEV = 8
RMS_EPS = 1e-6
ATT_SCALE = 1.0 / math.sqrt(HEAD_DIM)
LANES = 128
HALO = 8
TM = 512
TQ = 256
VMEM_LIMIT = 48 * 1024 * 1024
VMEM_BIG = 58 * 1024 * 1024
MESH = pl.DeviceIdType.MESH

ADAM_LR, ADAM_B1, ADAM_B2, ADAM_EPS, ADAM_WD, ADAM_STEP = 0.001, 0.9, 0.999, 1e-08, 0.01, 10

NT = (((1,), (1,)), ((), ()))
TN = (((0,), (0,)), ((), ()))


def _params(n_axes, vmem_limit=VMEM_LIMIT):
    return pltpu.CompilerParams(dimension_semantics=("arbitrary",) * n_axes, vmem_limit_bytes=vmem_limit)


def _tile(n, cap, mult):
    best = None
    for t in range(mult, min(n, cap) + 1, mult):
        if n % t == 0:
            best = t
    assert best is not None, (n, cap, mult)
    return best


def _rms_rstd(x):
    return lax.rsqrt(jnp.mean(x * x, axis=-1, keepdims=True) + RMS_EPS)


def _rms_fwd(x, g):
    return x * _rms_rstd(x) * g


def _rms_bwd(x, g, dy):
    xh = x * _rms_rstd(x)
    dyg = dy * g
    dx = _rms_rstd(x) * (dyg - xh * jnp.mean(dyg * xh, axis=-1, keepdims=True))
    return dx, jnp.sum(dy * xh, axis=0, keepdims=True)


def _accumulate(ref, first, value):
    @pl.when(first)
    def _():
        ref[...] = value

    @pl.when(jnp.logical_not(first))
    def _():
        ref[...] += value


def rms_proj(x, g, ws, out_dtypes, name, tm=None, vmem_limit=VMEM_LIMIT):
    T = x.shape[0]
    tm = _tile(T, TM, 16) if tm is None else tm
    na = len(ws)

    def body(x_ref, g_ref, *refs):
        w_refs, xn_ref, o_refs = refs[:na], refs[na], refs[na + 1:]
        xn = _rms_fwd(x_ref[...], g_ref[...]).astype(BF16)
        xn_ref[...] = xn
        for a, (w, l) in enumerate(ws):
            for p in range(w.shape[0]):
                o_refs[a][p] = jnp.dot(xn, w_refs[a][p, l], preferred_element_type=F32).astype(o_refs[a].dtype)

    in_specs = [pl.BlockSpec((tm, D), lambda i: (i, 0)), pl.BlockSpec((1, D), lambda i: (0, 0))]
    in_specs += [pl.BlockSpec(w.shape, lambda i: (0, 0, 0, 0), pipeline_mode=pl.Buffered(1)) for w, _ in ws]
    out_specs = [pl.BlockSpec((tm, D), lambda i: (i, 0))]
    out_shape = [jax.ShapeDtypeStruct((T, D), BF16)]
    for (w, _), dt in zip(ws, out_dtypes):
        nb, wb = w.shape[0], w.shape[3]
        out_specs.append(pl.BlockSpec((nb, tm, wb), lambda i: (0, i, 0)))
        out_shape.append(jax.ShapeDtypeStruct((nb, T, wb), dt))
    res = pl.pallas_call(
        body, name=name, grid=(T // tm,), in_specs=in_specs, out_specs=out_specs, out_shape=out_shape,
        compiler_params=_params(1, vmem_limit),
    )(x, g, *[w for w, _ in ws])
    return res[0], res[1:]


def mix_out(pro, pro_inputs, pro_specs, w, res, g_post, alpha, nk, name, tm=None):
    w4, l = w
    T = res.shape[0]
    tm = _tile(T, TM, 16) if tm is None else tm
    dpb = N_DEV // nk
    rows = w4.shape[2]
    kb = dpb * rows
    npi = len(pro_inputs)

    def body(*refs):
        pro_refs = refs[:npi]
        w_ref, res_ref, g_ref, z_ref, m_ref, y_ref = refs[npi:]
        i = pl.program_id(0)
        m = None
        for k in range(nk):
            z = pro(i, k, *pro_refs).astype(BF16)
            z_ref[k] = z
            part = jnp.dot(z, w_ref[k * dpb:(k + 1) * dpb, l].reshape(kb, D), preferred_element_type=F32)
            m = part if m is None else m + part
        m_ref[...] = m
        y_ref[...] = res_ref[...] + alpha * _rms_fwd(m, g_ref[...])

    row = pl.BlockSpec((tm, D), lambda i: (i, 0))
    in_specs = list(pro_specs) + [
        pl.BlockSpec(w4.shape, lambda i: (0, 0, 0, 0), pipeline_mode=pl.Buffered(1)), row, pl.BlockSpec((1, D), lambda i: (0, 0))]
    z, m, y = pl.pallas_call(
        body, name=name, grid=(T // tm,), in_specs=in_specs,
        out_specs=[pl.BlockSpec((nk, tm, kb), lambda i: (0, i, 0)), row, row],
        out_shape=[jax.ShapeDtypeStruct((nk, T, kb), BF16), jax.ShapeDtypeStruct((T, D), F32),
                   jax.ShapeDtypeStruct((T, D), F32)],
        compiler_params=_params(1),
    )(*pro_inputs, w4, res, g_post)
    return z, m, y


def post_bwd(dy, m, g_post, alpha, w, nk, name, gu=None, tm=None):
    w4, l = w
    T = dy.shape[0]
    tm = _tile(T, TM, 16) if tm is None else tm
    dpb = N_DEV // nk
    rows = w4.shape[2]
    kb = dpb * rows
    swiglu = gu is not None

    def body(dy_ref, m_ref, g_ref, w_ref, *refs):
        if swiglu:
            gu_ref, dm_ref, dg_ref, dgate_ref, dup_ref = refs
        else:
            dm_ref, dg_ref, dz_ref = refs
        dm, dg = _rms_bwd(m_ref[...], g_ref[...], alpha * dy_ref[...])
        dm = dm.astype(BF16)
        dm_ref[...] = dm
        _accumulate(dg_ref, pl.program_id(0) == 0, dg)
        for k in range(nk):
            dz = lax.dot_general(dm, w_ref[k * dpb:(k + 1) * dpb, l].reshape(kb, D), NT, preferred_element_type=F32)
            if swiglu:
                gate, up = gu_ref[k].astype(F32), gu_ref[k + nk].astype(F32)
                sig = jax.nn.sigmoid(gate)
                dgate_ref[k] = (dz * up * sig * (1.0 + gate * (1.0 - sig))).astype(BF16)
                dup_ref[k] = (dz * gate * sig).astype(BF16)
            else:
                dz_ref[k] = dz.astype(BF16)

    row = pl.BlockSpec((tm, D), lambda i: (i, 0))
    vec = pl.BlockSpec((1, D), lambda i: (0, 0))
    blkk = pl.BlockSpec((nk, tm, kb), lambda i: (0, i, 0))
    in_specs = [row, row, vec, pl.BlockSpec(w4.shape, lambda i: (0, 0, 0, 0), pipeline_mode=pl.Buffered(1))]
    inputs = [dy, m, g_post, w4]
    out_specs = [row, vec, blkk]
    out_shape = [jax.ShapeDtypeStruct((T, D), BF16), jax.ShapeDtypeStruct((1, D), F32),
                 jax.ShapeDtypeStruct((nk, T, kb), BF16)]
    if swiglu:
        in_specs.append(pl.BlockSpec((2 * nk, tm, kb), lambda i: (0, i, 0)))
        inputs.append(gu)
        out_specs.append(blkk)
        out_shape.append(jax.ShapeDtypeStruct((nk, T, kb), BF16))
    return pl.pallas_call(
        body, name=name, grid=(T // tm,), in_specs=in_specs, out_specs=out_specs, out_shape=out_shape,
        compiler_params=_params(1),
    )(*inputs)


def pre_bwd(pieces, x, g_pre, dres, name):
    T = x.shape[0]
    tm = _tile(T, TM, 16)
    na = len(pieces)
    weights = []
    for _, (w4, _), _ in pieces:
        if not any(w4 is w for w in weights):
            weights.append(w4)
    nw = len(weights)
    which = [[w4 is w for w in weights].index(True) for _, (w4, _), _ in pieces]

    def body(*refs):
        dz_refs, w_refs = refs[:na], refs[na:na + nw]
        x_ref, g_ref, dres_ref, dx_ref, dg_ref = refs[na + nw:]
        acc = None
        for a, (dz, (_, l), w_off) in enumerate(pieces):
            for p in range(dz.shape[0]):
                part = lax.dot_general(dz_refs[a][p].astype(BF16), w_refs[which[a]][w_off + p, l], NT,
                                       preferred_element_type=F32)
                acc = part if acc is None else acc + part
        dx, dg = _rms_bwd(x_ref[...], g_ref[...], acc)
        dx_ref[...] = dres_ref[...] + dx
        _accumulate(dg_ref, pl.program_id(0) == 0, dg)

    row = pl.BlockSpec((tm, D), lambda i: (i, 0))
    vec = pl.BlockSpec((1, D), lambda i: (0, 0))
    dz_specs = [pl.BlockSpec((dz.shape[0], tm, dz.shape[2]), lambda i: (0, i, 0)) for dz, _, _ in pieces]
    w_specs = [pl.BlockSpec(w.shape, lambda i: (0, 0, 0, 0), pipeline_mode=pl.Buffered(1)) for w in weights]
    return pl.pallas_call(
        body, name=name, grid=(T // tm,), in_specs=dz_specs + w_specs + [row, vec, row],
        out_specs=[row, vec], out_shape=[jax.ShapeDtypeStruct((T, D), F32), jax.ShapeDtypeStruct((1, D), F32)],
        compiler_params=_params(1),
    )(*[p[0] for p in pieces], *weights, x, g_pre, dres)


def wgrad(a_list, b_list, nout, out4_shape, name):
    T = a_list[0].shape[1]
    tm = _tile(T, 4 * TM, 16)
    nt = T // tm
    dpb = out4_shape[0] // nout
    _, _, R, C = out4_shape
    na, nb = len(a_list), len(b_list)

    def spans(arrs):
        ns = [a.shape[0] for a in arrs]
        return ns, [sum(ns[:k]) for k in range(len(ns))], sum(ns)

    a_ns, a_offs, a_tot = spans(a_list)
    b_ns, b_offs, b_tot = spans(b_list)
    assert a_tot in (1, nout) and b_tot in (1, nout)

    def body(*refs):
        a_refs, b_refs = refs[:na], refs[na:na + nb]
        out_ref, acc = refs[-2:]
        p, t = pl.program_id(0), pl.program_id(1)
        for ia in range(na):
            for ib in range(nb):
                conds = []
                if a_tot > 1:
                    conds += [p >= a_offs[ia], p < a_offs[ia] + a_ns[ia]]
                if b_tot > 1:
                    conds += [p >= b_offs[ib], p < b_offs[ib] + b_ns[ib]]

                def work(ia=ia, ib=ib):
                    part = lax.dot_general(a_refs[ia][0], b_refs[ib][0], TN, preferred_element_type=F32)
                    _accumulate(acc, t == 0, part)

                if conds:
                    pl.when(functools.reduce(jnp.logical_and, conds))(work)
                else:
                    work()

        @pl.when(t == nt - 1)
        def _():
            out_ref[...] = acc[...].astype(BF16).reshape(dpb, 1, R, C)

    def blk(off, n, tot):
        if tot == 1:
            return lambda p: 0
        return lambda p: jnp.clip(p - off, 0, n - 1)

    in_specs = []
    for arrs, ns, offs, tot in ((a_list, a_ns, a_offs, a_tot), (b_list, b_ns, b_offs, b_tot)):
        for arr, n, off in zip(arrs, ns, offs):
            in_specs.append(pl.BlockSpec((1, tm, arr.shape[2]), lambda p, t, f=blk(off, n, tot): (f(p), t, 0)))
    return pl.pallas_call(
        body, name=name, grid=(nout, nt), in_specs=in_specs,
        out_specs=pl.BlockSpec((dpb, 1, R, C), lambda p, t: (p, 0, 0, 0)),
        out_shape=jax.ShapeDtypeStruct(out4_shape, BF16),
        scratch_shapes=[pltpu.VMEM((dpb * R, C), F32)], compiler_params=_params(2),
    )(*a_list, *b_list)


def _swiglu_pro(i, k, gu_ref):
    gate, up = gu_ref[k].astype(F32), gu_ref[k + gu_ref.shape[0] // 2].astype(F32)
    return gate * jax.nn.sigmoid(gate) * up


def _attn_gate_pro(i, k, gate_ref, o_ref):
    return jax.nn.sigmoid(gate_ref[0].astype(F32)) * o_ref[...].astype(F32)


def _shift_rows(u, halo, d):
    rolled = pltpu.roll(u, d, 0)
    row = lax.broadcasted_iota(jnp.int32, u.shape, 0)
    for r in range(d):
        rolled = jnp.where(row == r, halo[HALO - d + r:HALO - d + r + 1, :], rolled)
    return rolled


def _advance_rows(u, halo, d):
    n = u.shape[0]
    rolled = pltpu.roll(u, n - d, 0)
    row = lax.broadcasted_iota(jnp.int32, u.shape, 0)
    for r in range(d):
        rolled = jnp.where(row == n - d + r, halo[r:r + 1, :], rolled)
    return rolled


def _make_conv_pro(tiles_per_seq):
    def pro(i, k, b_ref, c_ref, h_ref, ch_ref, hh_ref, ck_ref):
        u = c_ref[0].astype(F32) * h_ref[0].astype(F32)
        first = (i % tiles_per_seq) == 0
        halo = jnp.where(first, 0.0, ch_ref[0].astype(F32) * hh_ref[0].astype(F32))
        ck = ck_ref[...]
        conv = ck[2:3, :] * u + ck[1:2, :] * _shift_rows(u, halo, 1) + ck[0:1, :] * _shift_rows(u, halo, 2)
        return b_ref[0].astype(F32) * conv
    return pro


def conv_bwd_mix(dz, bch, conv_k, seq_len, name):
    T = dz.shape[1]
    tm = _tile(seq_len, TM, 16)
    tps = seq_len // tm
    nt = T // tm
    hb = tm // HALO

    def body(dz_ref, b_ref, c_ref, h_ref, cp_ref, hp_ref, dzn_ref, bn_ref, ck_ref, dbch_ref, dk_ref):
        i = pl.program_id(0)
        first = (i % tps) == 0
        last = (i % tps) == tps - 1
        b, c, h = b_ref[0].astype(F32), c_ref[0].astype(F32), h_ref[0].astype(F32)
        dzt = dz_ref[0].astype(F32)
        u = c * h
        prev = jnp.where(first, 0.0, cp_ref[0].astype(F32) * hp_ref[0].astype(F32))
        u1, u2 = _shift_rows(u, prev, 1), _shift_rows(u, prev, 2)
        ck = ck_ref[...]
        conv = ck[2:3, :] * u + ck[1:2, :] * u1 + ck[0:1, :] * u2
        dconv = dzt * b
        nxt = jnp.where(last, 0.0, dzn_ref[0].astype(F32) * bn_ref[0].astype(F32))
        du = ck[2:3, :] * dconv + ck[1:2, :] * _advance_rows(dconv, nxt, 1) + ck[0:1, :] * _advance_rows(dconv, nxt, 2)
        dbch_ref[0] = (dzt * conv).astype(BF16)
        dbch_ref[1] = (du * h).astype(BF16)
        dbch_ref[2] = (du * c).astype(BF16)
        tap = lax.broadcasted_iota(jnp.int32, (3, D), 0)
        dk = jnp.where(tap == 0, jnp.sum(dconv * u2, axis=0, keepdims=True),
                       jnp.where(tap == 1, jnp.sum(dconv * u1, axis=0, keepdims=True),
                                 jnp.sum(dconv * u, axis=0, keepdims=True)))
        _accumulate(dk_ref, i == 0, dk)

    def piece(p):
        return pl.BlockSpec((1, tm, D), lambda i, p=p: (p, i, 0))

    def prev(p):
        return pl.BlockSpec((1, HALO, D), lambda i, p=p: (p, jnp.maximum(i * hb - 1, 0), 0))

    def nxt(p):
        return pl.BlockSpec((1, HALO, D), lambda i, p=p: (p, jnp.minimum((i + 1) * hb, nt * hb - 1), 0))

    return pl.pallas_call(
        body, name=name, grid=(nt,),
        in_specs=[piece(0), piece(0), piece(1), piece(2), prev(1), prev(2), nxt(0), nxt(0),
                  pl.BlockSpec((3, D), lambda i: (0, 0))],
        out_specs=[pl.BlockSpec((3, tm, D), lambda i: (0, i, 0)), pl.BlockSpec((3, D), lambda i: (0, 0))],
        out_shape=[jax.ShapeDtypeStruct((3, T, D), BF16), jax.ShapeDtypeStruct((3, D), F32)],
        compiler_params=_params(1),
    )(dz, bch, bch, bch, bch, bch, dz, bch, conv_k)


def _log_sigmoid(x):
    return jnp.minimum(x, 0.0) - jnp.log(1.0 + jnp.exp(-jnp.abs(x)))


def forget_fwd(fl, fb, seq_len, name):
    T = fl.shape[1]

    def body(fl_ref, fb_ref, c_ref):
        c = _log_sigmoid(fl_ref[0] + fb_ref[...])
        row = lax.broadcasted_iota(jnp.int32, c.shape, 0)
        k = 1
        while k < seq_len:
            c = c + jnp.where(row >= k, pltpu.roll(c, k, 0), 0.0)
            k *= 2
        c_ref[...] = c

    return pl.pallas_call(
        body, name=name, grid=(T // seq_len,),
        in_specs=[pl.BlockSpec((1, seq_len, LANES), lambda b: (0, b, 0)), pl.BlockSpec((1, LANES), lambda b: (0, 0))],
        out_specs=pl.BlockSpec((seq_len, LANES), lambda b: (b, 0)),
        out_shape=jax.ShapeDtypeStruct((T, LANES), F32), compiler_params=_params(1),
    )(fl, fb)


def forget_bwd(dc, fl, fb, seq_len, name):
    T = dc.shape[0]

    def body(dc_ref, fl_ref, fb_ref, dfl_ref, dfb_ref):
        b = pl.program_id(0)
        r = dc_ref[...]
        row = lax.broadcasted_iota(jnp.int32, r.shape, 0)
        k = 1
        while k < seq_len:
            r = r + jnp.where(row < seq_len - k, pltpu.roll(r, seq_len - k, 0), 0.0)
            k *= 2
        dfl = r * jax.nn.sigmoid(-(fl_ref[0] + fb_ref[...]))
        dfl_ref[0] = dfl
        _accumulate(dfb_ref, b == 0, jnp.sum(dfl, axis=0, keepdims=True))

    return pl.pallas_call(
        body, name=name, grid=(T // seq_len,),
        in_specs=[pl.BlockSpec((seq_len, LANES), lambda b: (b, 0)), pl.BlockSpec((1, seq_len, LANES), lambda b: (0, b, 0)),
                  pl.BlockSpec((1, LANES), lambda b: (0, 0))],
        out_specs=[pl.BlockSpec((1, seq_len, LANES), lambda b: (0, b, 0)), pl.BlockSpec((1, LANES), lambda b: (0, 0))],
        out_shape=[jax.ShapeDtypeStruct((1, T, LANES), F32), jax.ShapeDtypeStruct((1, LANES), F32)],
        compiler_params=_params(1),
    )(dc, fl, fb)


HEADS = (slice(0, HEAD_DIM), slice(HEAD_DIM, 2 * HEAD_DIM))


def _key_blocks(i, step, carry, pairs):
    if not pairs:
        return step(i, lax.fori_loop(0, i, lambda j, c: step(j, c, False), carry), True)

    def pair(jj, c):
        return step(2 * jj + 1, step(2 * jj, c, False), False)

    carry = lax.fori_loop(0, i // 2, pair, carry)
    return lax.cond(i % 2 == 1, lambda c: step(i, step(i - 1, c, False), True), lambda c: step(i, c, True), carry)


def attn_fwd(qg, kv, cb, n_seq, seq_len, name):
    T = n_seq * seq_len
    tq = _tile(seq_len, TQ, LANES)
    nq = seq_len // tq

    def body(q_ref, k_ref, v_ref, cb_ref, o_ref, lse_ref):
        i = pl.program_id(2)
        causal = lax.broadcasted_iota(jnp.int32, (tq, tq), 0) <= lax.broadcasted_iota(jnp.int32, (tq, tq), 1)
        q8 = [q_ref[0, :, sl] * ATT_SCALE for sl in HEADS]

        def block(j, carry, diagonal):
            rows = pl.ds(pl.multiple_of(j * tq, tq), tq)
            out = []
            for hh, sl in enumerate(HEADS):
                m, l, acc = carry[3 * hh:3 * hh + 3]
                s = lax.dot_general(k_ref[0, rows, sl], q8[hh], NT, preferred_element_type=F32)
                s = s - cb_ref[rows, sl.start:sl.start + 1]
                if diagonal:
                    s = jnp.where(causal, s, -1e30)
                m_new = jnp.maximum(m, jnp.max(s, axis=0, keepdims=True))
                a = jnp.exp(m - m_new)
                p = jnp.exp(s - m_new)
                l = a * l + jnp.sum(p, axis=0, keepdims=True)
                acc = a * acc + lax.dot_general(v_ref[0, rows, sl], p.astype(BF16), TN, preferred_element_type=F32)
                out += [m_new, l, acc]
            return tuple(out)

        init = (jnp.full((1, tq), -1e30, F32), jnp.zeros((1, tq), F32), jnp.zeros((HEAD_DIM, tq), F32)) * 2
        carry = _key_blocks(i, block, init, True)
        o_ref[...] = jnp.concatenate([carry[2] / carry[1], carry[5] / carry[4]], axis=0).T
        for hh in range(2):
            lse_ref[0, 0, 0, hh:hh + 1, :] = carry[3 * hh] + jnp.log(carry[3 * hh + 1])

    seq2 = pl.BlockSpec((seq_len, LANES), lambda b, hp, i: (b, hp))
    return pl.pallas_call(
        body, name=name, grid=(n_seq, N_HEADS // 2, nq),
        in_specs=[pl.BlockSpec((1, tq, LANES), lambda b, hp, i: (0, b * nq + i, hp)),
                  pl.BlockSpec((1, seq_len, LANES), lambda b, hp, i: (0, b, hp)),
                  pl.BlockSpec((1, seq_len, LANES), lambda b, hp, i: (1, b, hp)), seq2],
        out_specs=[pl.BlockSpec((tq, LANES), lambda b, hp, i: (b * nq + i, hp)),
                   pl.BlockSpec((1, 1, 1, 2, tq), lambda b, hp, i: (b, hp, i, 0, 0))],
        out_shape=[jax.ShapeDtypeStruct((T, D), F32), jax.ShapeDtypeStruct((n_seq, N_HEADS // 2, nq, 2, tq), F32)],
        compiler_params=_params(3),
    )(qg, kv, kv, cb)


def attn_bwd(dz, qg, kv, o, lse, cb, n_seq, seq_len, name):
    T = n_seq * seq_len
    tq = _tile(seq_len, TQ, LANES)
    nq = seq_len // tq

    def body(dz_ref, q_ref, gate_ref, o_ref, lse_ref, cb_ref, k_ref, v_ref,
             dq_ref, dgate_ref, dk_ref, dv_ref, dc_ref, p_s, dp_s, dk_s, dv_s, dc_s):
        i = pl.program_id(2)

        @pl.when(i == 0)
        def _():
            dk_s[...] = jnp.zeros_like(dk_s)
            dv_s[...] = jnp.zeros_like(dv_s)
            dc_s[...] = jnp.zeros_like(dc_s)

        dzf = dz_ref[...].astype(F32)
        sig = jax.nn.sigmoid(gate_ref[0].astype(F32))
        dob = (dzf * sig).astype(BF16)
        dgate_ref[0] = (dzf * o_ref[...] * sig * (1.0 - sig)).astype(BF16)
        causal = lax.broadcasted_iota(jnp.int32, (tq, tq), 0) <= lax.broadcasted_iota(jnp.int32, (tq, tq), 1)
        q8 = [q_ref[0, :, sl] * ATT_SCALE for sl in HEADS]
        do = [dob[:, sl] for sl in HEADS]
        lse_i = [lse_ref[0, 0, 0, hh:hh + 1, :] for hh in range(2)]

        def probs(j, dsums, diagonal):
            rows = pl.ds(pl.multiple_of(j * tq, tq), tq)
            out = []
            for hh, sl in enumerate(HEADS):
                s = lax.dot_general(k_ref[0, rows, sl], q8[hh], NT, preferred_element_type=F32)
                p = jnp.exp(s - cb_ref[rows, sl.start:sl.start + 1] - lse_i[hh])
                if diagonal:
                    p = jnp.where(causal, p, 0.0)
                dp = lax.dot_general(v_ref[0, rows, sl], do[hh], NT, preferred_element_type=F32)
                p_s[hh, j] = p
                dp_s[hh, j] = dp
                out.append(dsums[hh] + jnp.sum(p * dp, axis=0, keepdims=True))
            return tuple(out)

        dsums = _key_blocks(i, probs, (jnp.zeros((1, tq), F32),) * 2, True)

        def grads(j, dqs):
            rows = pl.ds(pl.multiple_of(j * tq, tq), tq)
            out = []
            for hh, sl in enumerate(HEADS):
                p = p_s[hh, j]
                ds = p * (dp_s[hh, j] - dsums[hh])
                dc_s[hh, rows, :] -= jnp.sum(ds, axis=1, keepdims=True)
                dsb = ds.astype(BF16)
                dk_s[rows, sl] += jnp.dot(dsb, q8[hh], preferred_element_type=F32)
                dv_s[rows, sl] += jnp.dot(p.astype(BF16), do[hh], preferred_element_type=F32)
                out.append(dqs[hh] + lax.dot_general(k_ref[0, rows, sl], dsb, TN, preferred_element_type=F32))
            return tuple(out)

        dqs = _key_blocks(i, lambda j, c, diagonal: grads(j, c), (jnp.zeros((HEAD_DIM, tq), F32),) * 2, False)
        dq_ref[0] = (jnp.concatenate(dqs, axis=0).T * ATT_SCALE).astype(BF16)

        @pl.when(i == nq - 1)
        def _():
            dk_ref[0] = dk_s[...].astype(BF16)
            dv_ref[0] = dv_s[...].astype(BF16)
            dc_ref[...] = jnp.zeros_like(dc_ref)
            for hh, sl in enumerate(HEADS):
                dc_ref[:, sl.start:sl.start + 1] = dc_s[hh]

    qry2 = pl.BlockSpec((tq, LANES), lambda b, hp, i: (b * nq + i, hp))
    seq2 = pl.BlockSpec((seq_len, LANES), lambda b, hp, i: (b, hp))

    def qry3(p):
        return pl.BlockSpec((1, tq, LANES), lambda b, hp, i, p=p: (p, b * nq + i, hp))

    def seq3(p):
        return pl.BlockSpec((1, seq_len, LANES), lambda b, hp, i, p=p: (p, b, hp))

    act = jax.ShapeDtypeStruct((1, T, D), BF16)
    return pl.pallas_call(
        body, name=name, grid=(n_seq, N_HEADS // 2, nq),
        in_specs=[qry2, qry3(0), qry3(1), qry2, pl.BlockSpec((1, 1, 1, 2, tq), lambda b, hp, i: (b, hp, i, 0, 0)), seq2,
                  seq3(0), seq3(1)],
        out_specs=[qry3(0), qry3(0), seq3(0), seq3(0), seq2],
        out_shape=[act, act, act, act, jax.ShapeDtypeStruct((T, D), F32)],
        scratch_shapes=[pltpu.VMEM((2, nq, tq, tq), F32), pltpu.VMEM((2, nq, tq, tq), F32),
                        pltpu.VMEM((seq_len, LANES), F32), pltpu.VMEM((seq_len, LANES), F32),
                        pltpu.VMEM((2, seq_len, 1), F32)],
        compiler_params=_params(3),
    )(dz, qg, qg, o, lse, cb, kv, kv)


def loss_head(y, target, name):
    T = y.shape[0]
    tm = _tile(T, TM, 8)

    def body(y_ref, t_ref, dy_ref, loss_ref):
        err = y_ref[...] - t_ref[...]
        dy_ref[...] = err * (1.0 / D)
        part = 0.5 * jnp.sum(jnp.mean(err * err, axis=-1, keepdims=True), axis=0, keepdims=True)
        _accumulate(loss_ref, pl.program_id(0) == 0, jnp.broadcast_to(part, (1, LANES)))

    row = pl.BlockSpec((tm, D), lambda i: (i, 0))
    return pl.pallas_call(
        body, name=name, grid=(T // tm,), in_specs=[row, row],
        out_specs=[row, pl.BlockSpec((1, LANES), lambda i: (0, 0))],
        out_shape=[jax.ShapeDtypeStruct((T, D), F32), jax.ShapeDtypeStruct((1, LANES), F32)],
        compiler_params=_params(1),
    )(y, target)


def _adamw(w, g, m, v):
    m = ADAM_B1 * m + (1.0 - ADAM_B1) * g
    v = ADAM_B2 * v + (1.0 - ADAM_B2) * (g * g)
    m_hat = m / (1.0 - ADAM_B1 ** ADAM_STEP)
    v_hat = v / (1.0 - ADAM_B2 ** ADAM_STEP)
    delta = -ADAM_LR * (m_hat / (jnp.sqrt(v_hat) + ADAM_EPS) + ADAM_WD * w)
    return delta, m, v


def adam_sharded(w, m, v, contribs, name):
    L, R, C = w.shape
    tr = _tile(R, 256, 16) if R % 16 == 0 else R

    def body(w_ref, m_ref, v_ref, *refs):
        c_refs, (g_ref, d_ref, nm_ref, nv_ref) = refs[:2 * L], refs[2 * L:]
        l = pl.program_id(0)
        for j in range(L):
            @pl.when(l == j)
            def _(j=j):
                own_ref, recv_ref = c_refs[2 * j], c_refs[2 * j + 1]
                g = own_ref[...].astype(F32)
                for k in range(N_DEV - 1):
                    g = g + recv_ref[k].astype(F32)
                delta, nm, nv = _adamw(w_ref[0], g, m_ref[0], v_ref[0])
                g_ref[0] = g
                d_ref[0] = delta
                nm_ref[0] = nm
                nv_ref[0] = nv

    blk = pl.BlockSpec((1, tr, C), lambda l, i: (l, i, 0))
    in_specs = [blk, blk, blk]
    inputs = [w, m, v]
    for j, (own, recv) in enumerate(contribs):
        in_specs.append(pl.BlockSpec((tr, C), lambda l, i, j=j: (jnp.where(l == j, i, 0), 0)))
        in_specs.append(pl.BlockSpec((N_DEV - 1, tr, C), lambda l, i, j=j: (0, jnp.where(l == j, i, 0), 0)))
        inputs += [own, recv]
    shp = jax.ShapeDtypeStruct((L, R, C), F32)
    return pl.pallas_call(
        body, name=name, grid=(L, R // tr), in_specs=in_specs, out_specs=[blk] * 4, out_shape=[shp] * 4,
        compiler_params=_params(2),
    )(*inputs)


def adam_small(params, total, extra_grads, name):
    n, ne = len(params), len(extra_grads)

    def body(*refs):
        total_ref, extra_refs = refs[0], refs[1:1 + ne]
        ins, outs = refs[1 + ne:1 + ne + 3 * n], refs[1 + ne + 3 * n:]
        for k, (_, _, _, where) in enumerate(params):
            if isinstance(where, int):
                g = extra_refs[where][...]
            else:
                row, rows, width = where
                g = total_ref[row:row + rows, 0:width]
            delta, nm, nv = _adamw(ins[3 * k][...], g, ins[3 * k + 1][...], ins[3 * k + 2][...])
            outs[4 * k][...] = g
            outs[4 * k + 1][...] = delta
            outs[4 * k + 2][...] = nm
            outs[4 * k + 3][...] = nv

    flat = [a for w, m, v, _ in params for a in (w, m, v)]
    out_shape = [jax.ShapeDtypeStruct(w.shape, F32) for w, _, _, _ in params for _ in range(4)]
    res = pl.pallas_call(body, name=name, out_shape=out_shape)(total, *extra_grads, *flat)
    return [res[4 * k:4 * k + 4] for k in range(n)]


def _place():
    return lax.axis_index("x"), lax.axis_index("y"), lax.axis_index("c")


def _peer(place, k):
    x, y, c = place
    return x ^ (k >> 2), y ^ ((k >> 1) & 1), c ^ (k & 1)


ANY = pl.BlockSpec(memory_space=pl.ANY)
HBM = pl.BlockSpec(memory_space=pltpu.HBM)
SEM = pl.BlockSpec(memory_space=pltpu.SEMAPHORE)
EFFECT = pltpu.SideEffectType.DATAFLOW_SIDE_EFFECTING


def _in_hbm(a):
    return pltpu.with_memory_space_constraint(a, pltpu.HBM)


def _number(place):
    return 4 * place[0] + 2 * place[1] + place[2]


SLOTS = {"gather_like": 4, "gather_pass": 3, "scatter": 7}


def _plan(mode, src_ref, land_ref, place):
    if mode == "gather_like":
        return [(src_ref, land_ref.at[_number(place)], _peer(place, k)) for k in (1, 2, 4, 6)]
    if mode == "gather_pass":
        slots = [land_ref.at[_number(_peer(place, k))] for k in (2, 4, 6)]
        return [(slot, slot, _peer(place, 1)) for slot in slots]
    return [(src_ref.at[_number(_peer(place, k))], land_ref.at[k - 1], _peer(place, k)) for k in range(1, N_DEV)]


def _exchange(name, groups, start, afters):
    sizes = [len(g[2]) for g in groups]
    na, ng = sum(sizes), len(groups)
    waits = groups[0][0] is not None
    n_in_sems = 2 * ng if waits else 0
    n_out_sems = 2 * ng if start else 0

    def body(*refs):
        src_refs, land_refs = refs[:na], refs[na:2 * na]
        in_sems = refs[2 * na:2 * na + n_in_sems]
        outs = refs[2 * na + n_in_sems + len(afters):]
        place = _place()
        a = 0
        for gi, n in enumerate(sizes):
            for idx in range(n):
                if waits:
                    zone = land_refs[a].at[pl.ds(0, groups[gi][4])]
                    copy = pltpu.make_async_remote_copy(
                        src_ref=zone, dst_ref=zone, send_sem=in_sems[2 * gi].at[idx], recv_sem=in_sems[2 * gi + 1].at[idx],
                        device_id=_peer(place, 1), device_id_type=MESH)
                    copy.wait_send()
                    copy.wait_recv()
                if start:
                    for src, dst, peer in _plan(start, src_refs[a], land_refs[a], place):
                        pltpu.make_async_remote_copy(
                            src_ref=src, dst_ref=dst, send_sem=outs[2 * gi].at[idx], recv_sem=outs[2 * gi + 1].at[idx],
                            device_id=peer, device_id_type=MESH).start()
                a += 1
        if start:
            outs[-1][...] = jnp.zeros_like(outs[-1])

    srcs = [_in_hbm(s) for g in groups for s in g[2]]
    lands = [_in_hbm(l) for g in groups for l in g[3]]
    sems = [s for g in groups for s in g[:2]] if waits else []
    out_shape = [pltpu.SemaphoreType.DMA((n,)) for n in sizes for _ in range(2)] if start else []
    out_shape += [pltpu.HBM(a.shape, a.dtype) for a in srcs + lands]
    out_specs = [SEM] * n_out_sems + [HBM] * (2 * na)
    if start:
        out_shape.append(jax.ShapeDtypeStruct((8, LANES), F32))
        out_specs.append(pl.BlockSpec(memory_space=pltpu.VMEM))
    res = pl.pallas_call(
        body, name=name, in_specs=[HBM] * (2 * na) + [SEM] * n_in_sems + [ANY] * len(afters),
        out_shape=out_shape, out_specs=out_specs,
        input_output_aliases={i: n_out_sems + i for i in range(2 * na)},
        compiler_params=pltpu.CompilerParams(has_side_effects=EFFECT),
    )(*srcs, *lands, *sems, *afters)
    new_sems, thru = res[:n_out_sems], res[n_out_sems:n_out_sems + 2 * na]
    out, a = [], 0
    for gi, n in enumerate(sizes):
        pair = (new_sems[2 * gi], new_sems[2 * gi + 1]) if start else (None, None)
        out.append(pair + (thru[a:a + n], thru[na + a:na + a + n], SLOTS.get(start, 0)))
        a += n
    return out, (res[-1] if start else None)


def exchange_start(pair_groups, mode, name):
    groups = [(None, None, [s for s, _ in g], [l for _, l in g], 0) for g in pair_groups]
    return _exchange(name, groups, mode, ())


def exchange_relay(groups, mode, afters, name):
    return _exchange(name, groups, mode, afters)


def exchange_wait(groups, afters, name):
    done, _ = _exchange(name, groups, None, afters)
    return [(g[2], g[3]) for g in done]


def all_reduce_small(parts, n_rows, name):
    R = n_rows
    n_parts = len(parts)

    def body(*refs):
        part_refs = refs[:n_parts]
        out_ref, buf, send_sems, recv_sems = refs[n_parts:]
        x, y, c = _place()
        me = 4 * x + 2 * y + c
        own = buf.at[me]
        own[...] = jnp.zeros((R, D), F32)
        for ref, (arr, row) in zip(part_refs, parts):
            own[row:row + arr.shape[0], 0:arr.shape[1]] = ref[...]
        copies = []
        for k in range(1, N_DEV):
            peer = (x ^ (k >> 2), y ^ ((k >> 1) & 1), c ^ (k & 1))
            copies.append(pltpu.make_async_remote_copy(
                src_ref=own, dst_ref=own, send_sem=send_sems.at[k - 1], recv_sem=recv_sems.at[k - 1],
                device_id=peer, device_id_type=MESH))
        for cp in copies:
            cp.start()
        for cp in copies:
            cp.wait()
        total = buf[0]
        for d in range(1, N_DEV):
            total = total + buf[d]
        out_ref[...] = total

    vm = pl.BlockSpec(memory_space=pltpu.VMEM)
    return pl.pallas_call(
        body, name=name, in_specs=[vm] * n_parts, out_specs=vm, out_shape=jax.ShapeDtypeStruct((R, D), F32),
        scratch_shapes=[pltpu.VMEM((N_DEV, R, D), F32), pltpu.SemaphoreType.DMA((N_DEV - 1,)),
                        pltpu.SemaphoreType.DMA((N_DEV - 1,))],
    )(*[arr for arr, _ in parts])


def _col_blocks(gathered, n_blocks):
    n, d, w = gathered.shape
    whole = gathered.transpose(1, 0, 2).reshape(d, n * w)
    return whole.reshape(d, n_blocks, n * w // n_blocks).transpose(1, 0, 2)[:, None]


def _col_shards(blocks):
    n, d, w = blocks.shape
    whole = blocks.transpose(1, 0, 2).reshape(d, n * w)
    return whole.reshape(d, N_DEV, n * w // N_DEV).transpose(1, 0, 2)


def kernel(x, ffn1_pre_g, ffn1_post_g, ffn1_w_in, ffn1_w_out, mix_pre_g, mix_post_g, ffn2_pre_g, ffn2_post_g, ffn2_w_in, ffn2_w_out, conv_w_in, conv_k, conv_w_out, kv_g, kv_w, forget_b, attn_w_qg, attn_w_o, loss_target, m_ffn1_pre_g, m_ffn1_post_g, m_ffn1_w_in, m_ffn1_w_out, m_mix_pre_g, m_mix_post_g, m_ffn2_pre_g, m_ffn2_post_g, m_ffn2_w_in, m_ffn2_w_out, m_conv_w_in, m_conv_k, m_conv_w_out, m_kv_g, m_kv_w, m_forget_b, m_attn_w_qg, m_attn_w_o, v_ffn1_pre_g, v_ffn1_post_g, v_ffn1_w_in, v_ffn1_w_out, v_mix_pre_g, v_mix_post_g, v_ffn2_pre_g, v_ffn2_post_g, v_ffn2_w_in, v_ffn2_w_out, v_conv_w_in, v_conv_k, v_conv_w_out, v_kv_g, v_kv_w, v_forget_b, v_attn_w_qg, v_attn_w_o):
    n_seq, seq_len, _ = x.shape
    T = n_seq * seq_len
    xi, yi, ci = _place()
    dev = 4 * xi + 2 * yi + ci
    x0 = x.reshape(T, D)
    target = loss_target.reshape(T, D)

    def zone(s):
        return lax.dynamic_update_slice(lax.empty((N_DEV,) + s.shape, s.dtype), s[None], (dev,) + (0,) * s.ndim)

    first = [ffn1_w_in[0:1].astype(BF16), ffn1_w_out[0:1].astype(BF16)]
    gathers, token = exchange_start([[(s, zone(s)) for s in first]], "gather_like", "gather_start0")

    def cast(w):
        return (w + token[0, 0]).astype(BF16)

    shard_groups = [
        [cast(conv_w_in), cast(conv_w_out), conv_k],
        [cast(ffn2_w_in[0:1]), cast(ffn2_w_out[0:1])],
        [cast(kv_w), cast(ffn1_w_in[1:2]), cast(ffn1_w_out[1:2]), cast(attn_w_qg), cast(attn_w_o)],
        [cast(ffn2_w_in[1:2]), cast(ffn2_w_out[1:2])]]
    later, token = exchange_start([[(s, zone(s)) for s in g] for g in shard_groups], "gather_like", "gather_start1")
    gathers = gathers + later

    def gathered(k, after):
        passed, _ = exchange_relay([gathers[k]], "gather_pass", [after], f"gather_pass{k}")
        return exchange_wait(passed, [after], f"gather_wait{k}")[0][1]

    fb = jnp.pad(forget_b, (0, LANES - N_HEADS))[None]

    def vec(g, l):
        return g[l:l + 1]

    def behind(g, tok):
        return g + tok[:1, :1]

    grads_small = {}
    tm_ffn = _tile(T, TM, 16)

    def ffn_fwd(xin, g_pre, g_post, w_in, w_out, tag):
        xn, (gu,) = rms_proj(xin, g_pre, [(w_in, 0)], [BF16], f"{tag}_in", tm=_tile(T, 2 * TM, 16), vmem_limit=VMEM_BIG)
        specs = [pl.BlockSpec((gu.shape[0], tm_ffn, gu.shape[2]), lambda i: (0, i, 0))]
        a, h, y = mix_out(_swiglu_pro, [gu], specs, (w_out, 0), xin, g_post, 0.5, 4, f"{tag}_out", tm=tm_ffn)
        return y, (xin, xn, gu, a, h)

    def ffn_bwd(dy, saved, g_pre, g_post, w_in, w_out, tag):
        xin, xn, gu, a, h = saved
        dh, dg_post, dgate, dup = post_bwd(dy, h, g_post, 0.5, (w_out, 0), 4, f"{tag}_bwd_out", gu=gu, tm=tm_ffn)
        dw_out = wgrad([a], [dh[None]], 4, w_out.shape, f"{tag}_dw_out")
        dw_in = wgrad([dgate, dup], [xn[None]], 8, (8, 1, w_in.shape[3], D), f"{tag}_dw_in")
        started, tok = scatter_start([dw_in.reshape(8, -1, D), dw_out.reshape(8, -1, D)], f"{tag}_scatter_start")
        dx, dg_pre = pre_bwd([(dgate, (w_in, 0), 0), (dup, (w_in, 0), 4)], xin, behind(g_pre, tok), dy, f"{tag}_bwd_in")
        return dx, dg_pre, dg_post, started

    def scatter_start(blocked, name):
        pairs = [(g, lax.empty((N_DEV - 1,) + g.shape[1:], g.dtype)) for g in blocked]
        started, tok = exchange_start([pairs], "scatter", name)
        return started[0], tok

    w1_in, w1_out = gathered(0, token)
    x1, s_f1a = ffn_fwd(x0, vec(ffn1_pre_g, 0), vec(ffn1_post_g, 0), w1_in, w1_out, "l0_ffn1")
    cw_in_g, cw_out, ck_g = gathered(1, x1)
    cw_in = _col_blocks(cw_in_g[:, 0], 3)
    ck = ck_g[:, 0].transpose(1, 0, 2).reshape(3, D)
    xn_c, (bch,) = rms_proj(x1, vec(mix_pre_g, 0), [(cw_in, 0)], [BF16], "conv_in")
    tmc = _tile(seq_len, TM, 16)
    hb = tmc // HALO

    def cpiece(p):
        return pl.BlockSpec((1, tmc, D), lambda i, p=p: (p, i, 0))

    def chalo(p):
        return pl.BlockSpec((1, HALO, D), lambda i, p=p: (p, jnp.maximum(i * hb - 1, 0), 0))

    conv_specs = [cpiece(0), cpiece(1), cpiece(2), chalo(1), chalo(2), pl.BlockSpec((3, D), lambda i: (0, 0))]
    z_c, m_c, x2 = mix_out(_make_conv_pro(seq_len // tmc), [bch, bch, bch, bch, bch, ck], conv_specs, (cw_out, 0),
                           x1, vec(mix_post_g, 0), 1.0, 1, "conv_out", tm=tmc)
    w2_in, w2_out = gathered(2, x2)
    x3, s_f2a = ffn_fwd(x2, vec(ffn2_pre_g, 0), vec(ffn2_post_g, 0), w2_in, w2_out, "l0_ffn2")

    kvw_g, w1_in_b, w1_out_b, qgw_g, ow = gathered(3, x3)
    qg_w = _col_blocks(qgw_g[:, 0], 2)
    kv_whole = kvw_g.transpose(1, 0, 2).reshape(D, 2 * D + N_HEADS)
    kv_wb = kv_whole[:, :2 * D].reshape(D, 2, D).transpose(1, 0, 2)[:, None]
    f_w = jnp.pad(kv_whole[:, 2 * D:], ((0, 0), (0, LANES - N_HEADS)))[None, None]
    xn_kv, (kv, fl) = rms_proj(x3, kv_g[None], [(kv_wb, 0), (f_w, 0)], [BF16, F32], "kv_in")
    c128 = forget_fwd(fl, fb, seq_len, "forget_fwd")
    cb = jnp.repeat(c128[:, :N_HEADS], HEAD_DIM, axis=1)

    x4, s_f1b = ffn_fwd(x3, vec(ffn1_pre_g, 1), vec(ffn1_post_g, 1), w1_in_b, w1_out_b, "l1_ffn1")
    xn_a, (qg,) = rms_proj(x4, vec(mix_pre_g, 1), [(qg_w, 0)], [BF16], "attn_in")
    o, lse = attn_fwd(qg, kv, cb, n_seq, seq_len, "attn_fwd")
    tm = _tile(T, TM, 16)
    gate_specs = [pl.BlockSpec((1, tm, D), lambda i: (1, i, 0)), pl.BlockSpec((tm, D), lambda i: (i, 0))]
    z_a, m_a, x5 = mix_out(_attn_gate_pro, [qg, o], gate_specs, (ow, 0), x4, vec(mix_post_g, 1), 1.0, 1, "attn_out")
    w2_in_b, w2_out_b = gathered(4, x5)
    x6, s_f2b = ffn_fwd(x5, vec(ffn2_pre_g, 1), vec(ffn2_post_g, 1), w2_in_b, w2_out_b, "l1_ffn2")

    dy, loss_part = loss_head(x6, target, "loss_head")

    scatters = {}
    dx5, dg, dgp, scatters["ffn2", 1] = ffn_bwd(dy, s_f2b, vec(ffn2_pre_g, 1), vec(ffn2_post_g, 1), w2_in_b, w2_out_b, "l1_ffn2")
    grads_small["ffn2_pre", 1], grads_small["ffn2_post", 1] = dg, dgp
    dm_a, dgp, dz_a = post_bwd(dx5, m_a, vec(mix_post_g, 1), 1.0, (ow, 0), 1, "attn_bwd_out")
    grads_small["mix_post", 1] = dgp
    dq, dgate, dk, dv, dcb = attn_bwd(dz_a[0], qg, kv, o, lse, cb, n_seq, seq_len, "attn_bwd")
    dx4, dg = pre_bwd([(dq, (qg_w, 0), 0), (dgate, (qg_w, 0), 1)], x4, vec(mix_pre_g, 1), dx5, "attn_bwd_in")
    grads_small["mix_pre", 1] = dg
    d_ow = wgrad([z_a], [dm_a[None]], 1, ow.shape, "attn_dw_o")
    d_qgw = wgrad([xn_a[None]], [dq, dgate], 2, (2, 1, D, D), "attn_dw_qg")
    scatters["attn"], tok = scatter_start([_col_shards(d_qgw[:, 0]), d_ow.reshape(8, -1, D)], "attn_scatter_start")
    dx3, dg, dgp, scatters["ffn1", 1] = ffn_bwd(dx4, s_f1b, vec(ffn1_pre_g, 1), behind(vec(ffn1_post_g, 1), tok),
                                                w1_in_b, w1_out_b, "l1_ffn1")
    grads_small["ffn1_pre", 1], grads_small["ffn1_post", 1] = dg, dgp

    dc16 = dcb.reshape(T, N_HEADS, HEAD_DIM)[:, :, 0]
    dfl, dfb = forget_bwd(jnp.pad(dc16, ((0, 0), (0, LANES - N_HEADS))), fl, fb, seq_len, "forget_bwd")
    dx3, dg_kv = pre_bwd([(dk, (kv_wb, 0), 0), (dv, (kv_wb, 0), 1), (dfl, (f_w, 0), 0)], x3, kv_g[None], dx3, "kv_bwd_in")
    d_kvw = wgrad([xn_kv[None]], [dk, dv], 2, (2, 1, D, D), "kv_dw")
    d_fw = wgrad([xn_kv[None]], [dfl.astype(BF16)], 1, (1, 1, D, LANES), "forget_dw")
    d_kv_whole = jnp.concatenate([d_kvw[0, 0], d_kvw[1, 0], d_fw[0, 0, :, :N_HEADS]], axis=1)
    wshard = D * 2 + N_HEADS
    scatters["kv"], tok = scatter_start([d_kv_whole.T.reshape(N_DEV, wshard // N_DEV, D)], "kv_scatter_start")

    dx2, dg, dgp, scatters["ffn2", 0] = ffn_bwd(dx3, s_f2a, vec(ffn2_pre_g, 0), behind(vec(ffn2_post_g, 0), tok),
                                                w2_in, w2_out, "l0_ffn2")
    grads_small["ffn2_pre", 0], grads_small["ffn2_post", 0] = dg, dgp
    dm_c, dgp, dz_c = post_bwd(dx2, m_c, vec(mix_post_g, 0), 1.0, (cw_out, 0), 1, "conv_bwd_out")
    grads_small["mix_post", 0] = dgp
    dbch, d_ck = conv_bwd_mix(dz_c, bch, ck, seq_len, "conv_bwd_mix")
    dx1, dg = pre_bwd([(dbch, (cw_in, 0), 0)], x1, vec(mix_pre_g, 0), dx2, "conv_bwd_in")
    grads_small["mix_pre", 0] = dg
    d_cw_out = wgrad([z_c], [dm_c[None]], 1, cw_out.shape, "conv_dw_out")
    d_cw_in = wgrad([xn_c[None]], [dbch], 3, (3, 1, D, D), "conv_dw_in")
    scatters["conv"], tok = scatter_start([_col_shards(d_cw_in[:, 0]), d_cw_out.reshape(8, -1, D)], "conv_scatter_start")
    dx0, dg, dgp, scatters["ffn1", 0] = ffn_bwd(dx1, s_f1a, vec(ffn1_pre_g, 0), behind(vec(ffn1_post_g, 0), tok),
                                                w1_in, w1_out, "l0_ffn1")
    grads_small["ffn1_pre", 0], grads_small["ffn1_post", 0] = dg, dgp

    parts_of = {}

    def scatter_end(keys, afters, name):
        for k, (sent, recv) in zip(keys, exchange_wait([scatters[k] for k in keys], afters, name)):
            parts_of[k] = [(lax.dynamic_index_in_dim(g, dev, 0, keepdims=False), r) for g, r in zip(sent, recv)]

    scatter_end([("ffn2", 1), "attn", ("ffn1", 1), "kv", ("ffn2", 0), "conv"], [dx0], "scatter_wait")
    sharded = {"ffn2_w_in": (ffn2_w_in, m_ffn2_w_in, v_ffn2_w_in), "ffn2_w_out": (ffn2_w_out, m_ffn2_w_out, v_ffn2_w_out),
               "conv_w_in": (conv_w_in, m_conv_w_in, v_conv_w_in), "conv_w_out": (conv_w_out, m_conv_w_out, v_conv_w_out),
               "kv_w": (kv_w, m_kv_w, v_kv_w), "attn_w_qg": (attn_w_qg, m_attn_w_qg, v_attn_w_qg),
               "attn_w_o": (attn_w_o, m_attn_w_o, v_attn_w_o),
               "ffn1_w_in": (ffn1_w_in, m_ffn1_w_in, v_ffn1_w_in), "ffn1_w_out": (ffn1_w_out, m_ffn1_w_out, v_ffn1_w_out)}
    out = {}
    for nm, (w, mm, vv) in sharded.items():
        if nm == "ffn1_w_in":
            scatter_end([("ffn1", 0)], [res[0] for res in out.values()], "scatter_wait_last")
        contribs = {
            "ffn1_w_in": lambda: [parts_of["ffn1", 0][0], parts_of["ffn1", 1][0]],
            "ffn1_w_out": lambda: [parts_of["ffn1", 0][1], parts_of["ffn1", 1][1]],
            "ffn2_w_in": lambda: [parts_of["ffn2", 0][0], parts_of["ffn2", 1][0]],
            "ffn2_w_out": lambda: [parts_of["ffn2", 0][1], parts_of["ffn2", 1][1]],
            "conv_w_in": lambda: [parts_of["conv"][0]], "conv_w_out": lambda: [parts_of["conv"][1]],
            "kv_w": lambda: [parts_of["kv"][0]], "attn_w_qg": lambda: [parts_of["attn"][0]],
            "attn_w_o": lambda: [parts_of["attn"][1]]}[nm]()
        if nm in ("ffn1_w_in", "ffn2_w_in", "kv_w"):
            rows, cols = w.shape[-2:]

            def view(a):
                return a.reshape(-1, rows, cols).transpose(0, 2, 1)

            res = adam_sharded(view(w), view(mm), view(vv), contribs, f"adam_{nm}")
            out[nm] = [r.transpose(0, 2, 1).reshape(w.shape) for r in res]
        else:
            shape3 = (len(contribs),) + contribs[0][0].shape
            res = adam_sharded(w.reshape(shape3), mm.reshape(shape3), vv.reshape(shape3), contribs, f"adam_{nm}")
            out[nm] = [r.reshape(w.shape) for r in res]

    small_names = ["ffn1_pre", "ffn1_post", "mix_pre", "mix_post", "ffn2_pre", "ffn2_post"]
    parts = [(grads_small[n, l], 2 * k + l) for k, n in enumerate(small_names) for l in range(2)]
    parts += [(dg_kv, 12), (dfb, 13), (d_ck, 14), (loss_part, 17)]
    total = all_reduce_small(parts, 24, "all_reduce_small")
    loss = total[17, 0]
    d_ck_mine = lax.dynamic_slice(total, (14, dev * LANES), (3, LANES))
    gains = [(ffn1_pre_g, m_ffn1_pre_g, v_ffn1_pre_g), (ffn1_post_g, m_ffn1_post_g, v_ffn1_post_g),
             (mix_pre_g, m_mix_pre_g, v_mix_pre_g), (mix_post_g, m_mix_post_g, v_mix_post_g),
             (ffn2_pre_g, m_ffn2_pre_g, v_ffn2_pre_g), (ffn2_post_g, m_ffn2_post_g, v_ffn2_post_g)]
    small_params = [(w, m, v, (2 * k, 2, D)) for k, (w, m, v) in enumerate(gains)]
    small_params += [(kv_g[None], m_kv_g[None], v_kv_g[None], (12, 1, D)),
                     (forget_b[None], m_forget_b[None], v_forget_b[None], (13, 1, N_HEADS)),
                     (conv_k[0], m_conv_k[0], v_conv_k[0], 0)]
    small_res = adam_small(small_params, total, [d_ck_mine], "adam_small")
    small_keys = [n + "_g" for n in small_names] + ["kv_g", "forget_b", "conv_k"]
    shapes = {"kv_g": kv_g.shape, "forget_b": forget_b.shape, "conv_k": conv_k.shape}
    small = [{key: res[kind].reshape(shapes.get(key, res[kind].shape)) for key, res in zip(small_keys, small_res)}
             for kind in range(4)]
    order = ["ffn1_pre_g", "ffn1_post_g", "ffn1_w_in", "ffn1_w_out", "mix_pre_g", "mix_post_g", "ffn2_pre_g", "ffn2_post_g",
             "ffn2_w_in", "ffn2_w_out", "conv_w_in", "conv_k", "conv_w_out", "kv_g", "kv_w", "forget_b", "attn_w_qg",
             "attn_w_o"]
    results = [loss, dx0.reshape(x.shape)]
    for kind in range(4):
        for nm in order:
            results.append(out[nm][kind] if nm in out else small[kind][nm])
    return tuple(results)
```

```python
import functools
import math

import jax
import jax.numpy as jnp
from jax import lax
from jax.experimental import pallas as pl
from jax.experimental.pallas import tpu as pltpu

F32, BF16 = jnp.float32, jnp.bfloat16
D = 1024
N_HEADS = 16
HEAD_DIM = 64
N_DEV = 8
RMS_EPS = 1e-6
ATT_SCALE = 1.0 / math.sqrt(HEAD_DIM)
LANES = 128
HALO = 8
TM = 512
TQ = 256
VMEM_LIMIT = 48 * 1024 * 1024
MESH = pl.DeviceIdType.MESH

ADAM_LR, ADAM_B1, ADAM_B2, ADAM_EPS, ADAM_WD, ADAM_STEP = 0.001, 0.9, 0.999, 1e-08, 0.01, 10

NT = (((1,), (1,)), ((), ()))
TN = (((0,), (0,)), ((), ()))


def _params(n_axes, vmem_limit=VMEM_LIMIT):
    return pltpu.CompilerParams(dimension_semantics=("arbitrary",) * n_axes, vmem_limit_bytes=vmem_limit)


def _tile(n, cap, mult):
    best = None
    for t in range(mult, min(n, cap) + 1, mult):
        if n % t == 0:
            best = t
    assert best is not None, (n, cap, mult)
    return best


def _rms_rstd(x):
    return lax.rsqrt(jnp.mean(x * x, axis=-1, keepdims=True) + RMS_EPS)


def _rms_fwd(x, g):
    return x * _rms_rstd(x) * g


def _rms_bwd(x, g, dy):
    xh = x * _rms_rstd(x)
    dyg = dy * g
    dx = _rms_rstd(x) * (dyg - xh * jnp.mean(dyg * xh, axis=-1, keepdims=True))
    return dx, jnp.sum(dy * xh, axis=0, keepdims=True)


def _accumulate(ref, first, value):
    @pl.when(first)
    def _():
        ref[...] = value

    @pl.when(jnp.logical_not(first))
    def _():
        ref[...] += value


def rms_proj(x, g, ws, out_dtypes, name):
    T = x.shape[0]
    tm = _tile(T, TM, 16)
    na = len(ws)

    def body(x_ref, g_ref, *refs):
        w_refs, xn_ref, o_refs = refs[:na], refs[na], refs[na + 1:]
        xn = _rms_fwd(x_ref[...], g_ref[...]).astype(BF16)
        xn_ref[...] = xn
        for a, (w, l) in enumerate(ws):
            for p in range(w.shape[0]):
                o_refs[a][p] = jnp.dot(xn, w_refs[a][p, l], preferred_element_type=F32).astype(o_refs[a].dtype)

    in_specs = [pl.BlockSpec((tm, D), lambda i: (i, 0)), pl.BlockSpec((1, D), lambda i: (0, 0))]
    in_specs += [pl.BlockSpec(w.shape, lambda i: (0, 0, 0, 0), pipeline_mode=pl.Buffered(1)) for w, _ in ws]
    out_specs = [pl.BlockSpec((tm, D), lambda i: (i, 0))]
    out_shape = [jax.ShapeDtypeStruct((T, D), BF16)]
    for (w, _), dt in zip(ws, out_dtypes):
        nb, wb = w.shape[0], w.shape[3]
        out_specs.append(pl.BlockSpec((nb, tm, wb), lambda i: (0, i, 0)))
        out_shape.append(jax.ShapeDtypeStruct((nb, T, wb), dt))
    res = pl.pallas_call(
        body, name=name, grid=(T // tm,), in_specs=in_specs, out_specs=out_specs, out_shape=out_shape,
        compiler_params=_params(1),
    )(x, g, *[w for w, _ in ws])
    return res[0], res[1:]


def mix_out(pro, pro_inputs, pro_specs, w, res, g_post, alpha, nk, name, tm=None):
    w4, l = w
    T = res.shape[0]
    tm = _tile(T, TM, 16) if tm is None else tm
    dpb = N_DEV // nk
    rows = w4.shape[2]
    kb = dpb * rows
    npi = len(pro_inputs)

    def body(*refs):
        pro_refs = refs[:npi]
        w_ref, res_ref, g_ref, z_ref, m_ref, y_ref = refs[npi:]
        i = pl.program_id(0)
        m = None
        for k in range(nk):
            z = pro(i, k, *pro_refs).astype(BF16)
            z_ref[k] = z
            part = jnp.dot(z, w_ref[k * dpb:(k + 1) * dpb, l].reshape(kb, D), preferred_element_type=F32)
            m = part if m is None else m + part
        m_ref[...] = m
        y_ref[...] = res_ref[...] + alpha * _rms_fwd(m, g_ref[...])

    row = pl.BlockSpec((tm, D), lambda i: (i, 0))
    in_specs = list(pro_specs) + [
        pl.BlockSpec(w4.shape, lambda i: (0, 0, 0, 0), pipeline_mode=pl.Buffered(1)), row, pl.BlockSpec((1, D), lambda i: (0, 0))]
    z, m, y = pl.pallas_call(
        body, name=name, grid=(T // tm,), in_specs=in_specs,
        out_specs=[pl.BlockSpec((nk, tm, kb), lambda i: (0, i, 0)), row, row],
        out_shape=[jax.ShapeDtypeStruct((nk, T, kb), BF16), jax.ShapeDtypeStruct((T, D), F32),
                   jax.ShapeDtypeStruct((T, D), F32)],
        compiler_params=_params(1),
    )(*pro_inputs, w4, res, g_post)
    return z, m, y


def post_bwd(dy, m, g_post, alpha, w, nk, name, gu=None, tm=None):
    w4, l = w
    T = dy.shape[0]
    tm = _tile(T, TM, 16) if tm is None else tm
    dpb = N_DEV // nk
    rows = w4.shape[2]
    kb = dpb * rows
    swiglu = gu is not None

    def body(dy_ref, m_ref, g_ref, w_ref, *refs):
        if swiglu:
            gu_ref, dm_ref, dg_ref, dgate_ref, dup_ref = refs
        else:
            dm_ref, dg_ref, dz_ref = refs
        dm, dg = _rms_bwd(m_ref[...], g_ref[...], alpha * dy_ref[...])
        dm = dm.astype(BF16)
        dm_ref[...] = dm
        _accumulate(dg_ref, pl.program_id(0) == 0, dg)
        for k in range(nk):
            dz = lax.dot_general(dm, w_ref[k * dpb:(k + 1) * dpb, l].reshape(kb, D), NT, preferred_element_type=F32)
            if swiglu:
                gate, up = gu_ref[k].astype(F32), gu_ref[k + nk].astype(F32)
                sig = jax.nn.sigmoid(gate)
                dgate_ref[k] = (dz * up * sig * (1.0 + gate * (1.0 - sig))).astype(BF16)
                dup_ref[k] = (dz * gate * sig).astype(BF16)
            else:
                dz_ref[k] = dz.astype(BF16)

    row = pl.BlockSpec((tm, D), lambda i: (i, 0))
    vec = pl.BlockSpec((1, D), lambda i: (0, 0))
    blkk = pl.BlockSpec((nk, tm, kb), lambda i: (0, i, 0))
    in_specs = [row, row, vec, pl.BlockSpec(w4.shape, lambda i: (0, 0, 0, 0), pipeline_mode=pl.Buffered(1))]
    inputs = [dy, m, g_post, w4]
    out_specs = [row, vec, blkk]
    out_shape = [jax.ShapeDtypeStruct((T, D), BF16), jax.ShapeDtypeStruct((1, D), F32),
                 jax.ShapeDtypeStruct((nk, T, kb), BF16)]
    if swiglu:
        in_specs.append(pl.BlockSpec((2 * nk, tm, kb), lambda i: (0, i, 0)))
        inputs.append(gu)
        out_specs.append(blkk)
        out_shape.append(jax.ShapeDtypeStruct((nk, T, kb), BF16))
    return pl.pallas_call(
        body, name=name, grid=(T // tm,), in_specs=in_specs, out_specs=out_specs, out_shape=out_shape,
        compiler_params=_params(1),
    )(*inputs)


def pre_bwd(pieces, x, g_pre, dres, name):
    T = x.shape[0]
    tm = _tile(T, TM, 16)
    na = len(pieces)
    weights = []
    for _, (w4, _), _ in pieces:
        if not any(w4 is w for w in weights):
            weights.append(w4)
    nw = len(weights)
    which = [[w4 is w for w in weights].index(True) for _, (w4, _), _ in pieces]

    def body(*refs):
        dz_refs, w_refs = refs[:na], refs[na:na + nw]
        x_ref, g_ref, dres_ref, dx_ref, dg_ref = refs[na + nw:]
        acc = None
        for a, (dz, (_, l), w_off) in enumerate(pieces):
            for p in range(dz.shape[0]):
                part = lax.dot_general(dz_refs[a][p].astype(BF16), w_refs[which[a]][w_off + p, l], NT,
                                       preferred_element_type=F32)
                acc = part if acc is None else acc + part
        dx, dg = _rms_bwd(x_ref[...], g_ref[...], acc)
        dx_ref[...] = dres_ref[...] + dx
        _accumulate(dg_ref, pl.program_id(0) == 0, dg)

    row = pl.BlockSpec((tm, D), lambda i: (i, 0))
    vec = pl.BlockSpec((1, D), lambda i: (0, 0))
    dz_specs = [pl.BlockSpec((dz.shape[0], tm, dz.shape[2]), lambda i: (0, i, 0)) for dz, _, _ in pieces]
    w_specs = [pl.BlockSpec(w.shape, lambda i: (0, 0, 0, 0), pipeline_mode=pl.Buffered(1)) for w in weights]
    return pl.pallas_call(
        body, name=name, grid=(T // tm,), in_specs=dz_specs + w_specs + [row, vec, row],
        out_specs=[row, vec], out_shape=[jax.ShapeDtypeStruct((T, D), F32), jax.ShapeDtypeStruct((1, D), F32)],
        compiler_params=_params(1),
    )(*[p[0] for p in pieces], *weights, x, g_pre, dres)


def wgrad(a_list, b_list, nout, out4_shape, name):
    T = a_list[0].shape[1]
    tm = _tile(T, 4 * TM, 16)
    nt = T // tm
    dpb = out4_shape[0] // nout
    _, _, R, C = out4_shape
    na, nb = len(a_list), len(b_list)

    def spans(arrs):
        ns = [a.shape[0] for a in arrs]
        return ns, [sum(ns[:k]) for k in range(len(ns))], sum(ns)

    a_ns, a_offs, a_tot = spans(a_list)
    b_ns, b_offs, b_tot = spans(b_list)
    assert a_tot in (1, nout) and b_tot in (1, nout)

    def body(*refs):
        a_refs, b_refs = refs[:na], refs[na:na + nb]
        out_ref, acc = refs[-2:]
        p, t = pl.program_id(0), pl.program_id(1)
        for ia in range(na):
            for ib in range(nb):
                conds = []
                if a_tot > 1:
                    conds += [p >= a_offs[ia], p < a_offs[ia] + a_ns[ia]]
                if b_tot > 1:
                    conds += [p >= b_offs[ib], p < b_offs[ib] + b_ns[ib]]

                def work(ia=ia, ib=ib):
                    part = lax.dot_general(a_refs[ia][0], b_refs[ib][0], TN, preferred_element_type=F32)
                    _accumulate(acc, t == 0, part)

                if conds:
                    pl.when(functools.reduce(jnp.logical_and, conds))(work)
                else:
                    work()

        @pl.when(t == nt - 1)
        def _():
            out_ref[...] = acc[...].astype(BF16).reshape(dpb, 1, R, C)

    def blk(off, n, tot):
        if tot == 1:
            return lambda p: 0
        return lambda p: jnp.clip(p - off, 0, n - 1)

    in_specs = []
    for arrs, ns, offs, tot in ((a_list, a_ns, a_offs, a_tot), (b_list, b_ns, b_offs, b_tot)):
        for arr, n, off in zip(arrs, ns, offs):
            in_specs.append(pl.BlockSpec((1, tm, arr.shape[2]), lambda p, t, f=blk(off, n, tot): (f(p), t, 0)))
    return pl.pallas_call(
        body, name=name, grid=(nout, nt), in_specs=in_specs,
        out_specs=pl.BlockSpec((dpb, 1, R, C), lambda p, t: (p, 0, 0, 0)),
        out_shape=jax.ShapeDtypeStruct(out4_shape, BF16),
        scratch_shapes=[pltpu.VMEM((dpb * R, C), F32)], compiler_params=_params(2),
    )(*a_list, *b_list)


def _swiglu_pro(i, k, gu_ref):
    gate, up = gu_ref[k].astype(F32), gu_ref[k + gu_ref.shape[0] // 2].astype(F32)
    return gate * jax.nn.sigmoid(gate) * up


def _attn_gate_pro(i, k, gate_ref, o_ref):
    return jax.nn.sigmoid(gate_ref[0].astype(F32)) * o_ref[...].astype(F32)


def _shift_rows(u, halo, d):
    rolled = pltpu.roll(u, d, 0)
    row = lax.broadcasted_iota(jnp.int32, u.shape, 0)
    for r in range(d):
        rolled = jnp.where(row == r, halo[HALO - d + r:HALO - d + r + 1, :], rolled)
    return rolled


def _advance_rows(u, halo, d):
    n = u.shape[0]
    rolled = pltpu.roll(u, n - d, 0)
    row = lax.broadcasted_iota(jnp.int32, u.shape, 0)
    for r in range(d):
        rolled = jnp.where(row == n - d + r, halo[r:r + 1, :], rolled)
    return rolled


def _make_conv_pro(tiles_per_seq):
    def pro(i, k, b_ref, c_ref, h_ref, ch_ref, hh_ref, ck_ref):
        u = c_ref[0].astype(F32) * h_ref[0].astype(F32)
        first = (i % tiles_per_seq) == 0
        halo = jnp.where(first, 0.0, ch_ref[0].astype(F32) * hh_ref[0].astype(F32))
        ck = ck_ref[...]
        conv = ck[2:3, :] * u + ck[1:2, :] * _shift_rows(u, halo, 1) + ck[0:1, :] * _shift_rows(u, halo, 2)
        return b_ref[0].astype(F32) * conv
    return pro


def conv_bwd_mix(dz, bch, conv_k, seq_len, name):
    T = dz.shape[1]
    tm = _tile(seq_len, TM, 16)
    tps = seq_len // tm
    nt = T // tm
    hb = tm // HALO

    def body(dz_ref, b_ref, c_ref, h_ref, cp_ref, hp_ref, dzn_ref, bn_ref, ck_ref, dbch_ref, dk_ref):
        i = pl.program_id(0)
        first = (i % tps) == 0
        last = (i % tps) == tps - 1
        b, c, h = b_ref[0].astype(F32), c_ref[0].astype(F32), h_ref[0].astype(F32)
        dzt = dz_ref[0].astype(F32)
        u = c * h
        prev = jnp.where(first, 0.0, cp_ref[0].astype(F32) * hp_ref[0].astype(F32))
        u1, u2 = _shift_rows(u, prev, 1), _shift_rows(u, prev, 2)
        ck = ck_ref[...]
        conv = ck[2:3, :] * u + ck[1:2, :] * u1 + ck[0:1, :] * u2
        dconv = dzt * b
        nxt = jnp.where(last, 0.0, dzn_ref[0].astype(F32) * bn_ref[0].astype(F32))
        du = ck[2:3, :] * dconv + ck[1:2, :] * _advance_rows(dconv, nxt, 1) + ck[0:1, :] * _advance_rows(dconv, nxt, 2)
        dbch_ref[0] = (dzt * conv).astype(BF16)
        dbch_ref[1] = (du * h).astype(BF16)
        dbch_ref[2] = (du * c).astype(BF16)
        tap = lax.broadcasted_iota(jnp.int32, (3, D), 0)
        dk = jnp.where(tap == 0, jnp.sum(dconv * u2, axis=0, keepdims=True),
                       jnp.where(tap == 1, jnp.sum(dconv * u1, axis=0, keepdims=True),
                                 jnp.sum(dconv * u, axis=0, keepdims=True)))
        _accumulate(dk_ref, i == 0, dk)

    def piece(p):
        return pl.BlockSpec((1, tm, D), lambda i, p=p: (p, i, 0))

    def prev(p):
        return pl.BlockSpec((1, HALO, D), lambda i, p=p: (p, jnp.maximum(i * hb - 1, 0), 0))

    def nxt(p):
        return pl.BlockSpec((1, HALO, D), lambda i, p=p: (p, jnp.minimum((i + 1) * hb, nt * hb - 1), 0))

    return pl.pallas_call(
        body, name=name, grid=(nt,),
        in_specs=[piece(0), piece(0), piece(1), piece(2), prev(1), prev(2), nxt(0), nxt(0),
                  pl.BlockSpec((3, D), lambda i: (0, 0))],
        out_specs=[pl.BlockSpec((3, tm, D), lambda i: (0, i, 0)), pl.BlockSpec((3, D), lambda i: (0, 0))],
        out_shape=[jax.ShapeDtypeStruct((3, T, D), BF16), jax.ShapeDtypeStruct((3, D), F32)],
        compiler_params=_params(1),
    )(dz, bch, bch, bch, bch, bch, dz, bch, conv_k)


def _log_sigmoid(x):
    return jnp.minimum(x, 0.0) - jnp.log(1.0 + jnp.exp(-jnp.abs(x)))


def forget_fwd(fl, fb, seq_len, name):
    T = fl.shape[1]

    def body(fl_ref, fb_ref, c_ref):
        c = _log_sigmoid(fl_ref[0] + fb_ref[...])
        row = lax.broadcasted_iota(jnp.int32, c.shape, 0)
        k = 1
        while k < seq_len:
            c = c + jnp.where(row >= k, pltpu.roll(c, k, 0), 0.0)
            k *= 2
        c_ref[...] = c

    return pl.pallas_call(
        body, name=name, grid=(T // seq_len,),
        in_specs=[pl.BlockSpec((1, seq_len, LANES), lambda b: (0, b, 0)), pl.BlockSpec((1, LANES), lambda b: (0, 0))],
        out_specs=pl.BlockSpec((seq_len, LANES), lambda b: (b, 0)),
        out_shape=jax.ShapeDtypeStruct((T, LANES), F32), compiler_params=_params(1),
    )(fl, fb)


def forget_bwd(dc, fl, fb, seq_len, name):
    T = dc.shape[0]

    def body(dc_ref, fl_ref, fb_ref, dfl_ref, dfb_ref):
        b = pl.program_id(0)
        r = dc_ref[...]
        row = lax.broadcasted_iota(jnp.int32, r.shape, 0)
        k = 1
        while k < seq_len:
            r = r + jnp.where(row < seq_len - k, pltpu.roll(r, seq_len - k, 0), 0.0)
            k *= 2
        dfl = r * jax.nn.sigmoid(-(fl_ref[0] + fb_ref[...]))
        dfl_ref[0] = dfl
        _accumulate(dfb_ref, b == 0, jnp.sum(dfl, axis=0, keepdims=True))

    return pl.pallas_call(
        body, name=name, grid=(T // seq_len,),
        in_specs=[pl.BlockSpec((seq_len, LANES), lambda b: (b, 0)), pl.BlockSpec((1, seq_len, LANES), lambda b: (0, b, 0)),
                  pl.BlockSpec((1, LANES), lambda b: (0, 0))],
        out_specs=[pl.BlockSpec((1, seq_len, LANES), lambda b: (0, b, 0)), pl.BlockSpec((1, LANES), lambda b: (0, 0))],
        out_shape=[jax.ShapeDtypeStruct((1, T, LANES), F32), jax.ShapeDtypeStruct((1, LANES), F32)],
        compiler_params=_params(1),
    )(dc, fl, fb)


HEADS = (slice(0, HEAD_DIM), slice(HEAD_DIM, 2 * HEAD_DIM))


def _key_blocks(i, step, carry):
    carry = lax.fori_loop(0, i // 2, lambda jj, c: step(2 * jj, c, False, 2), carry)
    return lax.cond(i % 2 == 1, lambda c: step(i - 1, c, True, 2), lambda c: step(i, c, True, 1), carry)


def _causal(width, tq):
    keys = lax.broadcasted_iota(jnp.int32, (width * tq, tq), 0)
    return keys <= lax.broadcasted_iota(jnp.int32, (width * tq, tq), 1) + (width - 1) * tq


def attn_fwd(qg, kv, cb, n_seq, seq_len, name):
    T = n_seq * seq_len
    tq = _tile(seq_len, TQ, LANES)
    nq = seq_len // tq

    def body(q_ref, k_ref, v_ref, cb_ref, o_ref, lse_ref):
        i = pl.program_id(2)
        q8 = [q_ref[0, :, sl] * ATT_SCALE for sl in HEADS]

        def block(j, carry, diagonal, width):
            rows = pl.ds(pl.multiple_of(j * tq, tq), width * tq)
            out = []
            for hh, sl in enumerate(HEADS):
                m, l, acc = carry[3 * hh:3 * hh + 3]
                s = lax.dot_general(k_ref[0, rows, sl], q8[hh], NT, preferred_element_type=F32)
                s = s - cb_ref[rows, sl.start:sl.start + 1]
                if diagonal:
                    s = jnp.where(_causal(width, tq), s, -1e30)
                m_new = jnp.maximum(m, jnp.max(s, axis=0, keepdims=True))
                a = jnp.exp(m - m_new)
                p = jnp.exp(s - m_new)
                l = a * l + jnp.sum(p, axis=0, keepdims=True)
                acc = a * acc + lax.dot_general(v_ref[0, rows, sl], p.astype(BF16), TN, preferred_element_type=F32)
                out += [m_new, l, acc]
            return tuple(out)

        init = (jnp.full((1, tq), -1e30, F32), jnp.zeros((1, tq), F32), jnp.zeros((HEAD_DIM, tq), F32)) * 2
        carry = _key_blocks(i, block, init)
        o_ref[...] = jnp.concatenate([carry[2] / carry[1], carry[5] / carry[4]], axis=0).T
        for hh in range(2):
            lse_ref[0, 0, 0, hh:hh + 1, :] = carry[3 * hh] + jnp.log(carry[3 * hh + 1])

    seq2 = pl.BlockSpec((seq_len, LANES), lambda b, hp, i: (b, hp))
    return pl.pallas_call(
        body, name=name, grid=(n_seq, N_HEADS // 2, nq),
        in_specs=[pl.BlockSpec((1, tq, LANES), lambda b, hp, i: (0, b * nq + i, hp)),
                  pl.BlockSpec((1, seq_len, LANES), lambda b, hp, i: (0, b, hp)),
                  pl.BlockSpec((1, seq_len, LANES), lambda b, hp, i: (1, b, hp)), seq2],
        out_specs=[pl.BlockSpec((tq, LANES), lambda b, hp, i: (b * nq + i, hp)),
                   pl.BlockSpec((1, 1, 1, 2, tq), lambda b, hp, i: (b, hp, i, 0, 0))],
        out_shape=[jax.ShapeDtypeStruct((T, D), F32), jax.ShapeDtypeStruct((n_seq, N_HEADS // 2, nq, 2, tq), F32)],
        compiler_params=_params(3),
    )(qg, kv, kv, cb)


def attn_bwd(dz, qg, kv, o, lse, cb, n_seq, seq_len, name):
    T = n_seq * seq_len
    tq = _tile(seq_len, TQ, LANES)
    nq = seq_len // tq

    def body(dz_ref, q_ref, gate_ref, o_ref, lse_ref, cb_ref, k_ref, v_ref,
             dq_ref, dgate_ref, dk_ref, dv_ref, dc_ref, p_s, dp_s, dk_s, dv_s, dc_s):
        i = pl.program_id(2)

        @pl.when(i == 0)
        def _():
            dk_s[...] = jnp.zeros_like(dk_s)
            dv_s[...] = jnp.zeros_like(dv_s)
            dc_s[...] = jnp.zeros_like(dc_s)

        dzf = dz_ref[...].astype(F32)
        sig = jax.nn.sigmoid(gate_ref[0].astype(F32))
        dob = (dzf * sig).astype(BF16)
        dgate_ref[0] = (dzf * o_ref[...] * sig * (1.0 - sig)).astype(BF16)
        q8 = [q_ref[0, :, sl] * ATT_SCALE for sl in HEADS]
        do = [dob[:, sl] for sl in HEADS]
        lse_i = [lse_ref[0, 0, 0, hh:hh + 1, :] for hh in range(2)]

        def probs(j, dsums, diagonal, width):
            rows = pl.ds(pl.multiple_of(j * tq, tq), width * tq)
            out = []
            for hh, sl in enumerate(HEADS):
                s = lax.dot_general(k_ref[0, rows, sl], q8[hh], NT, preferred_element_type=F32)
                p = jnp.exp(s - cb_ref[rows, sl.start:sl.start + 1] - lse_i[hh])
                if diagonal:
                    p = jnp.where(_causal(width, tq), p, 0.0)
                dp = lax.dot_general(v_ref[0, rows, sl], do[hh], NT, preferred_element_type=F32)
                p_s[hh, rows, :] = p
                dp_s[hh, rows, :] = dp
                out.append(dsums[hh] + jnp.sum(p * dp, axis=0, keepdims=True))
            return tuple(out)

        dsums = _key_blocks(i, probs, (jnp.zeros((1, tq), F32),) * 2)

        def grads(j, dqs, diagonal, width):
            rows = pl.ds(pl.multiple_of(j * tq, tq), width * tq)
            out = []
            for hh, sl in enumerate(HEADS):
                p = p_s[hh, rows, :]
                ds = p * (dp_s[hh, rows, :] - dsums[hh])
                dc_s[hh, rows, :] -= jnp.sum(ds, axis=1, keepdims=True)
                dsb = ds.astype(BF16)
                dk_s[rows, sl] += jnp.dot(dsb, q8[hh], preferred_element_type=F32)
                dv_s[rows, sl] += jnp.dot(p.astype(BF16), do[hh], preferred_element_type=F32)
                out.append(dqs[hh] + lax.dot_general(k_ref[0, rows, sl], dsb, TN, preferred_element_type=F32))
            return tuple(out)

        dqs = _key_blocks(i, grads, (jnp.zeros((HEAD_DIM, tq), F32),) * 2)
        dq_ref[0] = (jnp.concatenate(dqs, axis=0).T * ATT_SCALE).astype(BF16)

        @pl.when(i == nq - 1)
        def _():
            dk_ref[0] = dk_s[...].astype(BF16)
            dv_ref[0] = dv_s[...].astype(BF16)
            dc_ref[...] = jnp.zeros_like(dc_ref)
            for hh, sl in enumerate(HEADS):
                dc_ref[:, sl.start:sl.start + 1] = dc_s[hh]

    qry2 = pl.BlockSpec((tq, LANES), lambda b, hp, i: (b * nq + i, hp))
    seq2 = pl.BlockSpec((seq_len, LANES), lambda b, hp, i: (b, hp))

    def qry3(p):
        return pl.BlockSpec((1, tq, LANES), lambda b, hp, i, p=p: (p, b * nq + i, hp))

    def seq3(p):
        return pl.BlockSpec((1, seq_len, LANES), lambda b, hp, i, p=p: (p, b, hp))

    act = jax.ShapeDtypeStruct((1, T, D), BF16)
    return pl.pallas_call(
        body, name=name, grid=(n_seq, N_HEADS // 2, nq),
        in_specs=[qry2, qry3(0), qry3(1), qry2, pl.BlockSpec((1, 1, 1, 2, tq), lambda b, hp, i: (b, hp, i, 0, 0)), seq2,
                  seq3(0), seq3(1)],
        out_specs=[qry3(0), qry3(0), seq3(0), seq3(0), seq2],
        out_shape=[act, act, act, act, jax.ShapeDtypeStruct((T, D), F32)],
        scratch_shapes=[pltpu.VMEM((2, seq_len, tq), F32), pltpu.VMEM((2, seq_len, tq), F32),
                        pltpu.VMEM((seq_len, LANES), F32), pltpu.VMEM((seq_len, LANES), F32),
                        pltpu.VMEM((2, seq_len, 1), F32)],
        compiler_params=_params(3),
    )(dz, qg, qg, o, lse, cb, kv, kv)


def loss_head(y, target, name):
    T = y.shape[0]
    tm = _tile(T, TM, 8)

    def body(y_ref, t_ref, dy_ref, loss_ref):
        err = y_ref[...] - t_ref[...]
        dy_ref[...] = err * (1.0 / D)
        part = 0.5 * jnp.sum(jnp.mean(err * err, axis=-1, keepdims=True), axis=0, keepdims=True)
        _accumulate(loss_ref, pl.program_id(0) == 0, jnp.broadcast_to(part, (1, LANES)))

    row = pl.BlockSpec((tm, D), lambda i: (i, 0))
    return pl.pallas_call(
        body, name=name, grid=(T // tm,), in_specs=[row, row],
        out_specs=[row, pl.BlockSpec((1, LANES), lambda i: (0, 0))],
        out_shape=[jax.ShapeDtypeStruct((T, D), F32), jax.ShapeDtypeStruct((1, LANES), F32)],
        compiler_params=_params(1),
    )(y, target)


def _adamw(w, g, m, v):
    m = ADAM_B1 * m + (1.0 - ADAM_B1) * g
    v = ADAM_B2 * v + (1.0 - ADAM_B2) * (g * g)
    m_hat = m / (1.0 - ADAM_B1 ** ADAM_STEP)
    v_hat = v / (1.0 - ADAM_B2 ** ADAM_STEP)
    delta = -ADAM_LR * (m_hat / (jnp.sqrt(v_hat) + ADAM_EPS) + ADAM_WD * w)
    return delta, m, v


def adam_sharded(w, m, v, contribs, name):
    L, R, C = w.shape
    tr = _tile(R, 256, 16) if R % 16 == 0 else R

    def body(w_ref, m_ref, v_ref, *refs):
        c_refs, (g_ref, d_ref, nm_ref, nv_ref) = refs[:2 * L], refs[2 * L:]
        l = pl.program_id(0)
        for j in range(L):
            @pl.when(l == j)
            def _(j=j):
                own_ref, recv_ref = c_refs[2 * j], c_refs[2 * j + 1]
                g = own_ref[...].astype(F32)
                for k in range(N_DEV - 1):
                    g = g + recv_ref[k].astype(F32)
                delta, nm, nv = _adamw(w_ref[0], g, m_ref[0], v_ref[0])
                g_ref[0] = g
                d_ref[0] = delta
                nm_ref[0] = nm
                nv_ref[0] = nv

    blk = pl.BlockSpec((1, tr, C), lambda l, i: (l, i, 0))
    in_specs = [blk, blk, blk]
    inputs = [w, m, v]
    for j, (own, recv) in enumerate(contribs):
        in_specs.append(pl.BlockSpec((tr, C), lambda l, i, j=j: (jnp.where(l == j, i, 0), 0)))
        in_specs.append(pl.BlockSpec((N_DEV - 1, tr, C), lambda l, i, j=j: (0, jnp.where(l == j, i, 0), 0)))
        inputs += [own, recv]
    shp = jax.ShapeDtypeStruct((L, R, C), F32)
    return pl.pallas_call(
        body, name=name, grid=(L, R // tr), in_specs=in_specs, out_specs=[blk] * 4, out_shape=[shp] * 4,
        compiler_params=_params(2),
    )(*inputs)


def adam_small(params, total, extra_grads, name):
    n, ne = len(params), len(extra_grads)

    def body(*refs):
        total_ref, extra_refs = refs[0], refs[1:1 + ne]
        ins, outs = refs[1 + ne:1 + ne + 3 * n], refs[1 + ne + 3 * n:]
        for k, (_, _, _, where) in enumerate(params):
            if isinstance(where, int):
                g = extra_refs[where][...]
            else:
                row, rows, width = where
                g = total_ref[row:row + rows, 0:width]
            delta, nm, nv = _adamw(ins[3 * k][...], g, ins[3 * k + 1][...], ins[3 * k + 2][...])
            outs[4 * k][...] = g
            outs[4 * k + 1][...] = delta
            outs[4 * k + 2][...] = nm
            outs[4 * k + 3][...] = nv

    flat = [a for w, m, v, _ in params for a in (w, m, v)]
    out_shape = [jax.ShapeDtypeStruct(w.shape, F32) for w, _, _, _ in params for _ in range(4)]
    res = pl.pallas_call(body, name=name, out_shape=out_shape)(total, *extra_grads, *flat)
    return [res[4 * k:4 * k + 4] for k in range(n)]


def _place():
    return lax.axis_index("x"), lax.axis_index("y"), lax.axis_index("c")


def _peer(place, k):
    x, y, c = place
    return x ^ (k >> 2), y ^ ((k >> 1) & 1), c ^ (k & 1)


ANY = pl.BlockSpec(memory_space=pl.ANY)
HBM = pl.BlockSpec(memory_space=pltpu.HBM)
SEM = pl.BlockSpec(memory_space=pltpu.SEMAPHORE)
EFFECT = pltpu.SideEffectType.DATAFLOW_SIDE_EFFECTING


def _in_hbm(a):
    return pltpu.with_memory_space_constraint(a, pltpu.HBM)


def _number(place):
    return 4 * place[0] + 2 * place[1] + place[2]


SLOTS = {"gather_like": 4, "gather_pass": 3, "scatter": 7}


def _plan(mode, src_ref, land_ref, place):
    if mode == "gather_like":
        return [(src_ref, land_ref.at[_number(place)], _peer(place, k)) for k in (1, 2, 4, 6)]
    if mode == "gather_pass":
        slots = [land_ref.at[_number(_peer(place, k))] for k in (2, 4, 6)]
        return [(slot, slot, _peer(place, 1)) for slot in slots]
    return [(src_ref.at[_number(_peer(place, k))], land_ref.at[k - 1], _peer(place, k)) for k in range(1, N_DEV)]


def _exchange(name, groups, start, afters):
    sizes = [len(g[2]) for g in groups]
    na, ng = sum(sizes), len(groups)
    waits = groups[0][0] is not None
    n_in_sems = 2 * ng if waits else 0
    n_out_sems = 2 * ng if start else 0

    def body(*refs):
        src_refs, land_refs = refs[:na], refs[na:2 * na]
        in_sems = refs[2 * na:2 * na + n_in_sems]
        outs = refs[2 * na + n_in_sems + len(afters):]
        place = _place()
        a = 0
        for gi, n in enumerate(sizes):
            for idx in range(n):
                if waits:
                    zone = land_refs[a].at[pl.ds(0, groups[gi][4])]
                    copy = pltpu.make_async_remote_copy(
                        src_ref=zone, dst_ref=zone, send_sem=in_sems[2 * gi].at[idx], recv_sem=in_sems[2 * gi + 1].at[idx],
                        device_id=_peer(place, 1), device_id_type=MESH)
                    copy.wait_send()
                    copy.wait_recv()
                if start:
                    for src, dst, peer in _plan(start, src_refs[a], land_refs[a], place):
                        pltpu.make_async_remote_copy(
                            src_ref=src, dst_ref=dst, send_sem=outs[2 * gi].at[idx], recv_sem=outs[2 * gi + 1].at[idx],
                            device_id=peer, device_id_type=MESH).start()
                a += 1
        if start:
            outs[-1][...] = jnp.zeros_like(outs[-1])

    srcs = [_in_hbm(s) for g in groups for s in g[2]]
    lands = [_in_hbm(l) for g in groups for l in g[3]]
    sems = [s for g in groups for s in g[:2]] if waits else []
    out_shape = [pltpu.SemaphoreType.DMA((n,)) for n in sizes for _ in range(2)] if start else []
    out_shape += [pltpu.HBM(a.shape, a.dtype) for a in srcs + lands]
    out_specs = [SEM] * n_out_sems + [HBM] * (2 * na)
    if start:
        out_shape.append(jax.ShapeDtypeStruct((8, LANES), F32))
        out_specs.append(pl.BlockSpec(memory_space=pltpu.VMEM))
    res = pl.pallas_call(
        body, name=name, in_specs=[HBM] * (2 * na) + [SEM] * n_in_sems + [ANY] * len(afters),
        out_shape=out_shape, out_specs=out_specs,
        input_output_aliases={i: n_out_sems + i for i in range(2 * na)},
        compiler_params=pltpu.CompilerParams(has_side_effects=EFFECT),
    )(*srcs, *lands, *sems, *afters)
    new_sems, thru = res[:n_out_sems], res[n_out_sems:n_out_sems + 2 * na]
    out, a = [], 0
    for gi, n in enumerate(sizes):
        pair = (new_sems[2 * gi], new_sems[2 * gi + 1]) if start else (None, None)
        out.append(pair + (thru[a:a + n], thru[na + a:na + a + n], SLOTS.get(start, 0)))
        a += n
    return out, (res[-1] if start else None)


def exchange_start(pair_groups, mode, name):
    groups = [(None, None, [s for s, _ in g], [l for _, l in g], 0) for g in pair_groups]
    return _exchange(name, groups, mode, ())


def exchange_relay(groups, mode, afters, name):
    return _exchange(name, groups, mode, afters)


def exchange_wait(groups, afters, name):
    done, _ = _exchange(name, groups, None, afters)
    return [(g[2], g[3]) for g in done]


def all_reduce_small(parts, n_rows, name):
    R = n_rows
    n_parts = len(parts)

    def body(*refs):
        part_refs = refs[:n_parts]
        out_ref, buf, send_sems, recv_sems = refs[n_parts:]
        x, y, c = _place()
        me = 4 * x + 2 * y + c
        own = buf.at[me]
        own[...] = jnp.zeros((R, D), F32)
        for ref, (arr, row) in zip(part_refs, parts):
            own[row:row + arr.shape[0], 0:arr.shape[1]] = ref[...]
        copies = []
        for k in range(1, N_DEV):
            peer = (x ^ (k >> 2), y ^ ((k >> 1) & 1), c ^ (k & 1))
            copies.append(pltpu.make_async_remote_copy(
                src_ref=own, dst_ref=own, send_sem=send_sems.at[k - 1], recv_sem=recv_sems.at[k - 1],
                device_id=peer, device_id_type=MESH))
        for cp in copies:
            cp.start()
        for cp in copies:
            cp.wait()
        total = buf[0]
        for d in range(1, N_DEV):
            total = total + buf[d]
        out_ref[...] = total

    vm = pl.BlockSpec(memory_space=pltpu.VMEM)
    return pl.pallas_call(
        body, name=name, in_specs=[vm] * n_parts, out_specs=vm, out_shape=jax.ShapeDtypeStruct((R, D), F32),
        scratch_shapes=[pltpu.VMEM((N_DEV, R, D), F32), pltpu.SemaphoreType.DMA((N_DEV - 1,)),
                        pltpu.SemaphoreType.DMA((N_DEV - 1,))],
    )(*[arr for arr, _ in parts])


def _col_blocks(gathered, n_blocks):
    n, d, w = gathered.shape
    whole = gathered.transpose(1, 0, 2).reshape(d, n * w)
    return whole.reshape(d, n_blocks, n * w // n_blocks).transpose(1, 0, 2)[:, None]


def _col_shards(blocks):
    n, d, w = blocks.shape
    whole = blocks.transpose(1, 0, 2).reshape(d, n * w)
    return whole.reshape(d, N_DEV, n * w // N_DEV).transpose(1, 0, 2)


def kernel(x, ffn1_pre_g, ffn1_post_g, ffn1_w_in, ffn1_w_out, mix_pre_g, mix_post_g, ffn2_pre_g, ffn2_post_g, ffn2_w_in, ffn2_w_out, conv_w_in, conv_k, conv_w_out, kv_g, kv_w, forget_b, attn_w_qg, attn_w_o, loss_target, m_ffn1_pre_g, m_ffn1_post_g, m_ffn1_w_in, m_ffn1_w_out, m_mix_pre_g, m_mix_post_g, m_ffn2_pre_g, m_ffn2_post_g, m_ffn2_w_in, m_ffn2_w_out, m_conv_w_in, m_conv_k, m_conv_w_out, m_kv_g, m_kv_w, m_forget_b, m_attn_w_qg, m_attn_w_o, v_ffn1_pre_g, v_ffn1_post_g, v_ffn1_w_in, v_ffn1_w_out, v_mix_pre_g, v_mix_post_g, v_ffn2_pre_g, v_ffn2_post_g, v_ffn2_w_in, v_ffn2_w_out, v_conv_w_in, v_conv_k, v_conv_w_out, v_kv_g, v_kv_w, v_forget_b, v_attn_w_qg, v_attn_w_o):
    n_seq, seq_len, _ = x.shape
    T = n_seq * seq_len
    xi, yi, ci = _place()
    dev = 4 * xi + 2 * yi + ci
    x0 = x.reshape(T, D)
    target = loss_target.reshape(T, D)

    def zone(s):
        return lax.dynamic_update_slice(lax.empty((N_DEV,) + s.shape, s.dtype), s[None], (dev,) + (0,) * s.ndim)

    first = [ffn1_w_in[0:1].astype(BF16), ffn1_w_out[0:1].astype(BF16)]
    gathers, token = exchange_start([[(s, zone(s)) for s in first]], "gather_like", "gather_start0")

    def cast(w):
        return (w + token[0, 0]).astype(BF16)

    shard_groups = [
        [cast(conv_w_in), cast(conv_w_out), conv_k],
        [cast(ffn2_w_in[0:1]), cast(ffn2_w_out[0:1])],
        [cast(kv_w), cast(ffn1_w_in[1:2]), cast(ffn1_w_out[1:2]), cast(attn_w_qg), cast(attn_w_o)],
        [cast(ffn2_w_in[1:2]), cast(ffn2_w_out[1:2])]]
    later, token = exchange_start([[(s, zone(s)) for s in g] for g in shard_groups], "gather_like", "gather_start1")
    gathers = gathers + later

    def gathered(k, after):
        passed, _ = exchange_relay([gathers[k]], "gather_pass", [after], f"gather_pass{k}")
        return exchange_wait(passed, [after], f"gather_wait{k}")[0][1]

    fb = jnp.pad(forget_b, (0, LANES - N_HEADS))[None]

    def vec(g, l):
        return g[l:l + 1]

    def behind(g, tok):
        return g + tok[:1, :1]

    grads_small = {}
    tm_ffn = _tile(T, TM, 16)

    def ffn_fwd(xin, g_pre, g_post, w_in, w_out, tag):
        xn, (gu,) = rms_proj(xin, g_pre, [(w_in, 0)], [BF16], f"{tag}_in")
        specs = [pl.BlockSpec((gu.shape[0], tm_ffn, gu.shape[2]), lambda i: (0, i, 0))]
        a, h, y = mix_out(_swiglu_pro, [gu], specs, (w_out, 0), xin, g_post, 0.5, 4, f"{tag}_out", tm=tm_ffn)
        return y, (xin, xn, gu, a, h)

    def ffn_bwd(dy, saved, g_pre, g_post, w_in, w_out, tag):
        xin, xn, gu, a, h = saved
        dh, dg_post, dgate, dup = post_bwd(dy, h, g_post, 0.5, (w_out, 0), 4, f"{tag}_bwd_out", gu=gu, tm=tm_ffn)
        dw_out = wgrad([a], [dh[None]], 4, w_out.shape, f"{tag}_dw_out")
        dw_in = wgrad([dgate, dup], [xn[None]], 8, (8, 1, w_in.shape[3], D), f"{tag}_dw_in")
        started, tok = scatter_start([dw_in.reshape(8, -1, D), dw_out.reshape(8, -1, D)], f"{tag}_scatter_start")
        dx, dg_pre = pre_bwd([(dgate, (w_in, 0), 0), (dup, (w_in, 0), 4)], xin, behind(g_pre, tok), dy, f"{tag}_bwd_in")
        return dx, dg_pre, dg_post, started

    def scatter_start(blocked, name):
        pairs = [(g, lax.empty((N_DEV - 1,) + g.shape[1:], g.dtype)) for g in blocked]
        started, tok = exchange_start([pairs], "scatter", name)
        return started[0], tok

    w1_in, w1_out = gathered(0, token)
    x1, s_f1a = ffn_fwd(x0, vec(ffn1_pre_g, 0), vec(ffn1_post_g, 0), w1_in, w1_out, "l0_ffn1")
    cw_in_g, cw_out, ck_g = gathered(1, x1)
    cw_in = _col_blocks(cw_in_g[:, 0], 3)
    ck = ck_g[:, 0].transpose(1, 0, 2).reshape(3, D)
    xn_c, (bch,) = rms_proj(x1, vec(mix_pre_g, 0), [(cw_in, 0)], [BF16], "conv_in")
    tmc = _tile(seq_len, TM, 16)
    hb = tmc // HALO

    def cpiece(p):
        return pl.BlockSpec((1, tmc, D), lambda i, p=p: (p, i, 0))

    def chalo(p):
        return pl.BlockSpec((1, HALO, D), lambda i, p=p: (p, jnp.maximum(i * hb - 1, 0), 0))

    conv_specs = [cpiece(0), cpiece(1), cpiece(2), chalo(1), chalo(2), pl.BlockSpec((3, D), lambda i: (0, 0))]
    z_c, m_c, x2 = mix_out(_make_conv_pro(seq_len // tmc), [bch, bch, bch, bch, bch, ck], conv_specs, (cw_out, 0),
                           x1, vec(mix_post_g, 0), 1.0, 1, "conv_out", tm=tmc)
    w2_in, w2_out = gathered(2, x2)
    x3, s_f2a = ffn_fwd(x2, vec(ffn2_pre_g, 0), vec(ffn2_post_g, 0), w2_in, w2_out, "l0_ffn2")

    kvw_g, w1_in_b, w1_out_b, qgw_g, ow = gathered(3, x3)
    qg_w = _col_blocks(qgw_g[:, 0], 2)
    kv_whole = kvw_g.transpose(1, 0, 2).reshape(D, 2 * D + N_HEADS)
    kv_wb = kv_whole[:, :2 * D].reshape(D, 2, D).transpose(1, 0, 2)[:, None]
    f_w = jnp.pad(kv_whole[:, 2 * D:], ((0, 0), (0, LANES - N_HEADS)))[None, None]
    xn_kv, (kv, fl) = rms_proj(x3, kv_g[None], [(kv_wb, 0), (f_w, 0)], [BF16, F32], "kv_in")
    c128 = forget_fwd(fl, fb, seq_len, "forget_fwd")
    cb = jnp.repeat(c128[:, :N_HEADS], HEAD_DIM, axis=1)

    x4, s_f1b = ffn_fwd(x3, vec(ffn1_pre_g, 1), vec(ffn1_post_g, 1), w1_in_b, w1_out_b, "l1_ffn1")
    xn_a, (qg,) = rms_proj(x4, vec(mix_pre_g, 1), [(qg_w, 0)], [BF16], "attn_in")
    o, lse = attn_fwd(qg, kv, cb, n_seq, seq_len, "attn_fwd")
    tm = _tile(T, TM, 16)
    gate_specs = [pl.BlockSpec((1, tm, D), lambda i: (1, i, 0)), pl.BlockSpec((tm, D), lambda i: (i, 0))]
    z_a, m_a, x5 = mix_out(_attn_gate_pro, [qg, o], gate_specs, (ow, 0), x4, vec(mix_post_g, 1), 1.0, 1, "attn_out")
    w2_in_b, w2_out_b = gathered(4, x5)
    x6, s_f2b = ffn_fwd(x5, vec(ffn2_pre_g, 1), vec(ffn2_post_g, 1), w2_in_b, w2_out_b, "l1_ffn2")

    dy, loss_part = loss_head(x6, target, "loss_head")

    scatters = {}
    dx5, dg, dgp, scatters["ffn2", 1] = ffn_bwd(dy, s_f2b, vec(ffn2_pre_g, 1), vec(ffn2_post_g, 1), w2_in_b, w2_out_b, "l1_ffn2")
    grads_small["ffn2_pre", 1], grads_small["ffn2_post", 1] = dg, dgp
    dm_a, dgp, dz_a = post_bwd(dx5, m_a, vec(mix_post_g, 1), 1.0, (ow, 0), 1, "attn_bwd_out")
    grads_small["mix_post", 1] = dgp
    dq, dgate, dk, dv, dcb = attn_bwd(dz_a[0], qg, kv, o, lse, cb, n_seq, seq_len, "attn_bwd")
    dx4, dg = pre_bwd([(dq, (qg_w, 0), 0), (dgate, (qg_w, 0), 1)], x4, vec(mix_pre_g, 1), dx5, "attn_bwd_in")
    grads_small["mix_pre", 1] = dg
    d_ow = wgrad([z_a], [dm_a[None]], 1, ow.shape, "attn_dw_o")
    d_qgw = wgrad([xn_a[None]], [dq, dgate], 2, (2, 1, D, D), "attn_dw_qg")
    scatters["attn"], tok = scatter_start([_col_shards(d_qgw[:, 0]), d_ow.reshape(8, -1, D)], "attn_scatter_start")
    dx3, dg, dgp, scatters["ffn1", 1] = ffn_bwd(dx4, s_f1b, vec(ffn1_pre_g, 1), behind(vec(ffn1_post_g, 1), tok),
                                                w1_in_b, w1_out_b, "l1_ffn1")
    grads_small["ffn1_pre", 1], grads_small["ffn1_post", 1] = dg, dgp

    dc16 = dcb.reshape(T, N_HEADS, HEAD_DIM)[:, :, 0]
    dfl, dfb = forget_bwd(jnp.pad(dc16, ((0, 0), (0, LANES - N_HEADS))), fl, fb, seq_len, "forget_bwd")
    dx3, dg_kv = pre_bwd([(dk, (kv_wb, 0), 0), (dv, (kv_wb, 0), 1), (dfl, (f_w, 0), 0)], x3, kv_g[None], dx3, "kv_bwd_in")
    d_kvw = wgrad([xn_kv[None]], [dk, dv], 2, (2, 1, D, D), "kv_dw")
    d_fw = wgrad([xn_kv[None]], [dfl.astype(BF16)], 1, (1, 1, D, LANES), "forget_dw")
    d_kv_whole = jnp.concatenate([d_kvw[0, 0], d_kvw[1, 0], d_fw[0, 0, :, :N_HEADS]], axis=1)
    wshard = D * 2 + N_HEADS
    scatters["kv"], tok = scatter_start([d_kv_whole.T.reshape(N_DEV, wshard // N_DEV, D)], "kv_scatter_start")

    dx2, dg, dgp, scatters["ffn2", 0] = ffn_bwd(dx3, s_f2a, vec(ffn2_pre_g, 0), behind(vec(ffn2_post_g, 0), tok),
                                                w2_in, w2_out, "l0_ffn2")
    grads_small["ffn2_pre", 0], grads_small["ffn2_post", 0] = dg, dgp
    dm_c, dgp, dz_c = post_bwd(dx2, m_c, vec(mix_post_g, 0), 1.0, (cw_out, 0), 1, "conv_bwd_out")
    grads_small["mix_post", 0] = dgp
    dbch, d_ck = conv_bwd_mix(dz_c, bch, ck, seq_len, "conv_bwd_mix")
    dx1, dg = pre_bwd([(dbch, (cw_in, 0), 0)], x1, vec(mix_pre_g, 0), dx2, "conv_bwd_in")
    grads_small["mix_pre", 0] = dg
    d_cw_out = wgrad([z_c], [dm_c[None]], 1, cw_out.shape, "conv_dw_out")
    d_cw_in = wgrad([xn_c[None]], [dbch], 3, (3, 1, D, D), "conv_dw_in")
    scatters["conv"], tok = scatter_start([_col_shards(d_cw_in[:, 0]), d_cw_out.reshape(8, -1, D)], "conv_scatter_start")
    dx0, dg, dgp, scatters["ffn1", 0] = ffn_bwd(dx1, s_f1a, vec(ffn1_pre_g, 0), behind(vec(ffn1_post_g, 0), tok),
                                                w1_in, w1_out, "l0_ffn1")
    grads_small["ffn1_pre", 0], grads_small["ffn1_post", 0] = dg, dgp

    parts_of = {}

    def scatter_end(keys, afters, name):
        for k, (sent, recv) in zip(keys, exchange_wait([scatters[k] for k in keys], afters, name)):
            parts_of[k] = [(lax.dynamic_index_in_dim(g, dev, 0, keepdims=False), r) for g, r in zip(sent, recv)]

    scatter_end([("ffn2", 1), "attn", ("ffn1", 1), "kv", ("ffn2", 0), "conv"], [dx0], "scatter_wait")
    sharded = {"ffn2_w_in": (ffn2_w_in, m_ffn2_w_in, v_ffn2_w_in), "ffn2_w_out": (ffn2_w_out, m_ffn2_w_out, v_ffn2_w_out),
               "conv_w_in": (conv_w_in, m_conv_w_in, v_conv_w_in), "conv_w_out": (conv_w_out, m_conv_w_out, v_conv_w_out),
               "kv_w": (kv_w, m_kv_w, v_kv_w), "attn_w_qg": (attn_w_qg, m_attn_w_qg, v_attn_w_qg),
               "attn_w_o": (attn_w_o, m_attn_w_o, v_attn_w_o),
               "ffn1_w_in": (ffn1_w_in, m_ffn1_w_in, v_ffn1_w_in), "ffn1_w_out": (ffn1_w_out, m_ffn1_w_out, v_ffn1_w_out)}
    out = {}
    for nm, (w, mm, vv) in sharded.items():
        if nm == "ffn1_w_in":
            scatter_end([("ffn1", 0)], [res[0] for res in out.values()], "scatter_wait_last")
        contribs = {
            "ffn1_w_in": lambda: [parts_of["ffn1", 0][0], parts_of["ffn1", 1][0]],
            "ffn1_w_out": lambda: [parts_of["ffn1", 0][1], parts_of["ffn1", 1][1]],
            "ffn2_w_in": lambda: [parts_of["ffn2", 0][0], parts_of["ffn2", 1][0]],
            "ffn2_w_out": lambda: [parts_of["ffn2", 0][1], parts_of["ffn2", 1][1]],
            "conv_w_in": lambda: [parts_of["conv"][0]], "conv_w_out": lambda: [parts_of["conv"][1]],
            "kv_w": lambda: [parts_of["kv"][0]], "attn_w_qg": lambda: [parts_of["attn"][0]],
            "attn_w_o": lambda: [parts_of["attn"][1]]}[nm]()
        if nm in ("ffn1_w_in", "ffn2_w_in", "kv_w"):
            rows, cols = w.shape[-2:]

            def view(a):
                return a.reshape(-1, rows, cols).transpose(0, 2, 1)

            res = adam_sharded(view(w), view(mm), view(vv), contribs, f"adam_{nm}")
            out[nm] = [r.transpose(0, 2, 1).reshape(w.shape) for r in res]
        else:
            shape3 = (len(contribs),) + contribs[0][0].shape
            res = adam_sharded(w.reshape(shape3), mm.reshape(shape3), vv.reshape(shape3), contribs, f"adam_{nm}")
            out[nm] = [r.reshape(w.shape) for r in res]

    small_names = ["ffn1_pre", "ffn1_post", "mix_pre", "mix_post", "ffn2_pre", "ffn2_post"]
    parts = [(grads_small[n, l], 2 * k + l) for k, n in enumerate(small_names) for l in range(2)]
    parts += [(dg_kv, 12), (dfb, 13), (d_ck, 14), (loss_part, 17)]
    total = all_reduce_small(parts, 24, "all_reduce_small")
    loss = total[17, 0]
    d_ck_mine = lax.dynamic_slice(total, (14, dev * LANES), (3, LANES))
    gains = [(ffn1_pre_g, m_ffn1_pre_g, v_ffn1_pre_g), (ffn1_post_g, m_ffn1_post_g, v_ffn1_post_g),
             (mix_pre_g, m_mix_pre_g, v_mix_pre_g), (mix_post_g, m_mix_post_g, v_mix_post_g),
             (ffn2_pre_g, m_ffn2_pre_g, v_ffn2_pre_g), (ffn2_post_g, m_ffn2_post_g, v_ffn2_post_g)]
    small_params = [(w, m, v, (2 * k, 2, D)) for k, (w, m, v) in enumerate(gains)]
    small_params += [(kv_g[None], m_kv_g[None], v_kv_g[None], (12, 1, D)),
                     (forget_b[None], m_forget_b[None], v_forget_b[None], (13, 1, N_HEADS)),
                     (conv_k[0], m_conv_k[0], v_conv_k[0], 0)]
    small_res = adam_small(small_params, total, [d_ck_mine], "adam_small")
    small_keys = [n + "_g" for n in small_names] + ["kv_g", "forget_b", "conv_k"]
    shapes = {"kv_g": kv_g.shape, "forget_b": forget_b.shape, "conv_k": conv_k.shape}
    small = [{key: res[kind].reshape(shapes.get(key, res[kind].shape)) for key, res in zip(small_keys, small_res)}
             for kind in range(4)]
    order = ["ffn1_pre_g", "ffn1_post_g", "ffn1_w_in", "ffn1_w_out", "mix_pre_g", "mix_post_g", "ffn2_pre_g", "ffn2_post_g",
             "ffn2_w_in", "ffn2_w_out", "conv_w_in", "conv_k", "conv_w_out", "kv_g", "kv_w", "forget_b", "attn_w_qg",
             "attn_w_o"]
    results = [loss, dx0.reshape(x.shape)]
    for kind in range(4):
        for nm in order:
            results.append(out[nm][kind] if nm in out else small[kind][nm])
    return tuple(results)
```

```python
import functools
import math

import jax
import jax.numpy as jnp
from jax import lax
from jax.experimental import pallas as pl
from jax.experimental.pallas import tpu as pltpu

F32, BF16 = jnp.float32, jnp.bfloat16
D = 1024
N_HEADS = 16
HEAD_DIM = 64
N_DEV = 8
RMS_EPS = 1e-6
ATT_SCALE = 1.0 / math.sqrt(HEAD_DIM)
LANES = 128
HALO = 8
TM = 512
TQ = 512
VMEM_LIMIT = 48 * 1024 * 1024
MESH = pl.DeviceIdType.MESH

ADAM_LR, ADAM_B1, ADAM_B2, ADAM_EPS, ADAM_WD, ADAM_STEP = 0.001, 0.9, 0.999, 1e-08, 0.01, 10

NT = (((1,), (1,)), ((), ()))
TN = (((0,), (0,)), ((), ()))


def _params(n_axes, vmem_limit=VMEM_LIMIT):
    return pltpu.CompilerParams(dimension_semantics=("arbitrary",) * n_axes, vmem_limit_bytes=vmem_limit)


def _tile(n, cap, mult):
    best = None
    for t in range(mult, min(n, cap) + 1, mult):
        if n % t == 0:
            best = t
    assert best is not None, (n, cap, mult)
    return best


def _rms_rstd(x):
    return lax.rsqrt(jnp.mean(x * x, axis=-1, keepdims=True) + RMS_EPS)


def _rms_fwd(x, g):
    return x * _rms_rstd(x) * g


def _rms_bwd(x, g, dy):
    xh = x * _rms_rstd(x)
    dyg = dy * g
    dx = _rms_rstd(x) * (dyg - xh * jnp.mean(dyg * xh, axis=-1, keepdims=True))
    return dx, jnp.sum(dy * xh, axis=0, keepdims=True)


def _accumulate(ref, first, value):
    @pl.when(first)
    def _():
        ref[...] = value

    @pl.when(jnp.logical_not(first))
    def _():
        ref[...] += value


def rms_proj(x, g, ws, out_dtypes, name):
    T = x.shape[0]
    tm = _tile(T, TM, 16)
    na = len(ws)

    def body(x_ref, g_ref, *refs):
        w_refs, xn_ref, o_refs = refs[:na], refs[na], refs[na + 1:]
        xn = _rms_fwd(x_ref[...], g_ref[...]).astype(BF16)
        xn_ref[...] = xn
        for a, (w, l) in enumerate(ws):
            for p in range(w.shape[0]):
                o_refs[a][p] = jnp.dot(xn, w_refs[a][p, l], preferred_element_type=F32).astype(o_refs[a].dtype)

    in_specs = [pl.BlockSpec((tm, D), lambda i: (i, 0)), pl.BlockSpec((1, D), lambda i: (0, 0))]
    in_specs += [pl.BlockSpec(w.shape, lambda i: (0, 0, 0, 0), pipeline_mode=pl.Buffered(1)) for w, _ in ws]
    out_specs = [pl.BlockSpec((tm, D), lambda i: (i, 0))]
    out_shape = [jax.ShapeDtypeStruct((T, D), BF16)]
    for (w, _), dt in zip(ws, out_dtypes):
        nb, wb = w.shape[0], w.shape[3]
        out_specs.append(pl.BlockSpec((nb, tm, wb), lambda i: (0, i, 0)))
        out_shape.append(jax.ShapeDtypeStruct((nb, T, wb), dt))
    res = pl.pallas_call(
        body, name=name, grid=(T // tm,), in_specs=in_specs, out_specs=out_specs, out_shape=out_shape,
        compiler_params=_params(1),
    )(x, g, *[w for w, _ in ws])
    return res[0], res[1:]


def mix_out(pro, pro_inputs, pro_specs, w, res, g_post, alpha, nk, name, tm=None):
    w4, l = w
    T = res.shape[0]
    tm = _tile(T, TM, 16) if tm is None else tm
    dpb = N_DEV // nk
    rows = w4.shape[2]
    kb = dpb * rows
    npi = len(pro_inputs)

    def body(*refs):
        pro_refs = refs[:npi]
        w_ref, res_ref, g_ref, z_ref, m_ref, y_ref = refs[npi:]
        i = pl.program_id(0)
        m = None
        for k in range(nk):
            z = pro(i, k, *pro_refs).astype(BF16)
            z_ref[k] = z
            part = jnp.dot(z, w_ref[k * dpb:(k + 1) * dpb, l].reshape(kb, D), preferred_element_type=F32)
            m = part if m is None else m + part
        m_ref[...] = m
        y_ref[...] = res_ref[...] + alpha * _rms_fwd(m, g_ref[...])

    row = pl.BlockSpec((tm, D), lambda i: (i, 0))
    in_specs = list(pro_specs) + [
        pl.BlockSpec(w4.shape, lambda i: (0, 0, 0, 0), pipeline_mode=pl.Buffered(1)), row, pl.BlockSpec((1, D), lambda i: (0, 0))]
    z, m, y = pl.pallas_call(
        body, name=name, grid=(T // tm,), in_specs=in_specs,
        out_specs=[pl.BlockSpec((nk, tm, kb), lambda i: (0, i, 0)), row, row],
        out_shape=[jax.ShapeDtypeStruct((nk, T, kb), BF16), jax.ShapeDtypeStruct((T, D), F32),
                   jax.ShapeDtypeStruct((T, D), F32)],
        compiler_params=_params(1),
    )(*pro_inputs, w4, res, g_post)
    return z, m, y


def post_bwd(dy, m, g_post, alpha, w, nk, name, gu=None, tm=None):
    w4, l = w
    T = dy.shape[0]
    tm = _tile(T, TM, 16) if tm is None else tm
    dpb = N_DEV // nk
    rows = w4.shape[2]
    kb = dpb * rows
    swiglu = gu is not None

    def body(dy_ref, m_ref, g_ref, w_ref, *refs):
        if swiglu:
            gu_ref, dm_ref, dg_ref, dgate_ref, dup_ref = refs
        else:
            dm_ref, dg_ref, dz_ref = refs
        dm, dg = _rms_bwd(m_ref[...], g_ref[...], alpha * dy_ref[...])
        dm = dm.astype(BF16)
        dm_ref[...] = dm
        _accumulate(dg_ref, pl.program_id(0) == 0, dg)
        for k in range(nk):
            dz = lax.dot_general(dm, w_ref[k * dpb:(k + 1) * dpb, l].reshape(kb, D), NT, preferred_element_type=F32)
            if swiglu:
                gate, up = gu_ref[k].astype(F32), gu_ref[k + nk].astype(F32)
                sig = jax.nn.sigmoid(gate)
                dgate_ref[k] = (dz * up * sig * (1.0 + gate * (1.0 - sig))).astype(BF16)
                dup_ref[k] = (dz * gate * sig).astype(BF16)
            else:
                dz_ref[k] = dz.astype(BF16)

    row = pl.BlockSpec((tm, D), lambda i: (i, 0))
    vec = pl.BlockSpec((1, D), lambda i: (0, 0))
    blkk = pl.BlockSpec((nk, tm, kb), lambda i: (0, i, 0))
    in_specs = [row, row, vec, pl.BlockSpec(w4.shape, lambda i: (0, 0, 0, 0), pipeline_mode=pl.Buffered(1))]
    inputs = [dy, m, g_post, w4]
    out_specs = [row, vec, blkk]
    out_shape = [jax.ShapeDtypeStruct((T, D), BF16), jax.ShapeDtypeStruct((1, D), F32),
                 jax.ShapeDtypeStruct((nk, T, kb), BF16)]
    if swiglu:
        in_specs.append(pl.BlockSpec((2 * nk, tm, kb), lambda i: (0, i, 0)))
        inputs.append(gu)
        out_specs.append(blkk)
        out_shape.append(jax.ShapeDtypeStruct((nk, T, kb), BF16))
    return pl.pallas_call(
        body, name=name, grid=(T // tm,), in_specs=in_specs, out_specs=out_specs, out_shape=out_shape,
        compiler_params=_params(1),
    )(*inputs)


def pre_bwd(pieces, x, g_pre, dres, name):
    T = x.shape[0]
    tm = _tile(T, TM, 16)
    na = len(pieces)
    weights = []
    for _, (w4, _), _ in pieces:
        if not any(w4 is w for w in weights):
            weights.append(w4)
    nw = len(weights)
    which = [[w4 is w for w in weights].index(True) for _, (w4, _), _ in pieces]

    def body(*refs):
        dz_refs, w_refs = refs[:na], refs[na:na + nw]
        x_ref, g_ref, dres_ref, dx_ref, dg_ref = refs[na + nw:]
        acc = None
        for a, (dz, (_, l), w_off) in enumerate(pieces):
            for p in range(dz.shape[0]):
                part = lax.dot_general(dz_refs[a][p].astype(BF16), w_refs[which[a]][w_off + p, l], NT,
                                       preferred_element_type=F32)
                acc = part if acc is None else acc + part
        dx, dg = _rms_bwd(x_ref[...], g_ref[...], acc)
        dx_ref[...] = dres_ref[...] + dx
        _accumulate(dg_ref, pl.program_id(0) == 0, dg)

    row = pl.BlockSpec((tm, D), lambda i: (i, 0))
    vec = pl.BlockSpec((1, D), lambda i: (0, 0))
    dz_specs = [pl.BlockSpec((dz.shape[0], tm, dz.shape[2]), lambda i: (0, i, 0)) for dz, _, _ in pieces]
    w_specs = [pl.BlockSpec(w.shape, lambda i: (0, 0, 0, 0), pipeline_mode=pl.Buffered(1)) for w in weights]
    return pl.pallas_call(
        body, name=name, grid=(T // tm,), in_specs=dz_specs + w_specs + [row, vec, row],
        out_specs=[row, vec], out_shape=[jax.ShapeDtypeStruct((T, D), F32), jax.ShapeDtypeStruct((1, D), F32)],
        compiler_params=_params(1),
    )(*[p[0] for p in pieces], *weights, x, g_pre, dres)


def wgrad(a_list, b_list, nout, out4_shape, name):
    T = a_list[0].shape[1]
    tm = _tile(T, 4 * TM, 16)
    nt = T // tm
    dpb = out4_shape[0] // nout
    _, _, R, C = out4_shape
    na, nb = len(a_list), len(b_list)

    def spans(arrs):
        ns = [a.shape[0] for a in arrs]
        return ns, [sum(ns[:k]) for k in range(len(ns))], sum(ns)

    a_ns, a_offs, a_tot = spans(a_list)
    b_ns, b_offs, b_tot = spans(b_list)
    assert a_tot in (1, nout) and b_tot in (1, nout)

    def body(*refs):
        a_refs, b_refs = refs[:na], refs[na:na + nb]
        out_ref, acc = refs[-2:]
        p, t = pl.program_id(0), pl.program_id(1)
        for ia in range(na):
            for ib in range(nb):
                conds = []
                if a_tot > 1:
                    conds += [p >= a_offs[ia], p < a_offs[ia] + a_ns[ia]]
                if b_tot > 1:
                    conds += [p >= b_offs[ib], p < b_offs[ib] + b_ns[ib]]

                def work(ia=ia, ib=ib):
                    part = lax.dot_general(a_refs[ia][0], b_refs[ib][0], TN, preferred_element_type=F32)
                    _accumulate(acc, t == 0, part)

                if conds:
                    pl.when(functools.reduce(jnp.logical_and, conds))(work)
                else:
                    work()

        @pl.when(t == nt - 1)
        def _():
            out_ref[...] = acc[...].astype(BF16).reshape(dpb, 1, R, C)

    def blk(off, n, tot):
        if tot == 1:
            return lambda p: 0
        return lambda p: jnp.clip(p - off, 0, n - 1)

    in_specs = []
    for arrs, ns, offs, tot in ((a_list, a_ns, a_offs, a_tot), (b_list, b_ns, b_offs, b_tot)):
        for arr, n, off in zip(arrs, ns, offs):
            in_specs.append(pl.BlockSpec((1, tm, arr.shape[2]), lambda p, t, f=blk(off, n, tot): (f(p), t, 0)))
    return pl.pallas_call(
        body, name=name, grid=(nout, nt), in_specs=in_specs,
        out_specs=pl.BlockSpec((dpb, 1, R, C), lambda p, t: (p, 0, 0, 0)),
        out_shape=jax.ShapeDtypeStruct(out4_shape, BF16),
        scratch_shapes=[pltpu.VMEM((dpb * R, C), F32)], compiler_params=_params(2),
    )(*a_list, *b_list)


def _swiglu_pro(i, k, gu_ref):
    gate, up = gu_ref[k].astype(F32), gu_ref[k + gu_ref.shape[0] // 2].astype(F32)
    return gate * jax.nn.sigmoid(gate) * up


def _attn_gate_pro(i, k, gate_ref, o_ref):
    return jax.nn.sigmoid(gate_ref[0].astype(F32)) * o_ref[...].astype(F32)


def _shift_rows(u, halo, d):
    rolled = pltpu.roll(u, d, 0)
    row = lax.broadcasted_iota(jnp.int32, u.shape, 0)
    for r in range(d):
        rolled = jnp.where(row == r, halo[HALO - d + r:HALO - d + r + 1, :], rolled)
    return rolled


def _advance_rows(u, halo, d):
    n = u.shape[0]
    rolled = pltpu.roll(u, n - d, 0)
    row = lax.broadcasted_iota(jnp.int32, u.shape, 0)
    for r in range(d):
        rolled = jnp.where(row == n - d + r, halo[r:r + 1, :], rolled)
    return rolled


def _make_conv_pro(tiles_per_seq):
    def pro(i, k, b_ref, c_ref, h_ref, ch_ref, hh_ref, ck_ref):
        u = c_ref[0].astype(F32) * h_ref[0].astype(F32)
        first = (i % tiles_per_seq) == 0
        halo = jnp.where(first, 0.0, ch_ref[0].astype(F32) * hh_ref[0].astype(F32))
        ck = ck_ref[...]
        conv = ck[2:3, :] * u + ck[1:2, :] * _shift_rows(u, halo, 1) + ck[0:1, :] * _shift_rows(u, halo, 2)
        return b_ref[0].astype(F32) * conv
    return pro


def conv_bwd_mix(dz, bch, conv_k, seq_len, name):
    T = dz.shape[1]
    tm = _tile(seq_len, TM, 16)
    tps = seq_len // tm
    nt = T // tm
    hb = tm // HALO

    def body(dz_ref, b_ref, c_ref, h_ref, cp_ref, hp_ref, dzn_ref, bn_ref, ck_ref, dbch_ref, dk_ref):
        i = pl.program_id(0)
        first = (i % tps) == 0
        last = (i % tps) == tps - 1
        b, c, h = b_ref[0].astype(F32), c_ref[0].astype(F32), h_ref[0].astype(F32)
        dzt = dz_ref[0].astype(F32)
        u = c * h
        prev = jnp.where(first, 0.0, cp_ref[0].astype(F32) * hp_ref[0].astype(F32))
        u1, u2 = _shift_rows(u, prev, 1), _shift_rows(u, prev, 2)
        ck = ck_ref[...]
        conv = ck[2:3, :] * u + ck[1:2, :] * u1 + ck[0:1, :] * u2
        dconv = dzt * b
        nxt = jnp.where(last, 0.0, dzn_ref[0].astype(F32) * bn_ref[0].astype(F32))
        du = ck[2:3, :] * dconv + ck[1:2, :] * _advance_rows(dconv, nxt, 1) + ck[0:1, :] * _advance_rows(dconv, nxt, 2)
        dbch_ref[0] = (dzt * conv).astype(BF16)
        dbch_ref[1] = (du * h).astype(BF16)
        dbch_ref[2] = (du * c).astype(BF16)
        tap = lax.broadcasted_iota(jnp.int32, (3, D), 0)
        dk = jnp.where(tap == 0, jnp.sum(dconv * u2, axis=0, keepdims=True),
                       jnp.where(tap == 1, jnp.sum(dconv * u1, axis=0, keepdims=True),
                                 jnp.sum(dconv * u, axis=0, keepdims=True)))
        _accumulate(dk_ref, i == 0, dk)

    def piece(p):
        return pl.BlockSpec((1, tm, D), lambda i, p=p: (p, i, 0))

    def prev(p):
        return pl.BlockSpec((1, HALO, D), lambda i, p=p: (p, jnp.maximum(i * hb - 1, 0), 0))

    def nxt(p):
        return pl.BlockSpec((1, HALO, D), lambda i, p=p: (p, jnp.minimum((i + 1) * hb, nt * hb - 1), 0))

    return pl.pallas_call(
        body, name=name, grid=(nt,),
        in_specs=[piece(0), piece(0), piece(1), piece(2), prev(1), prev(2), nxt(0), nxt(0),
                  pl.BlockSpec((3, D), lambda i: (0, 0))],
        out_specs=[pl.BlockSpec((3, tm, D), lambda i: (0, i, 0)), pl.BlockSpec((3, D), lambda i: (0, 0))],
        out_shape=[jax.ShapeDtypeStruct((3, T, D), BF16), jax.ShapeDtypeStruct((3, D), F32)],
        compiler_params=_params(1),
    )(dz, bch, bch, bch, bch, bch, dz, bch, conv_k)


def _log_sigmoid(x):
    return jnp.minimum(x, 0.0) - jnp.log(1.0 + jnp.exp(-jnp.abs(x)))


def forget_fwd(fl, fb, seq_len, name):
    T = fl.shape[1]

    def body(fl_ref, fb_ref, c_ref):
        c = _log_sigmoid(fl_ref[0] + fb_ref[...])
        row = lax.broadcasted_iota(jnp.int32, c.shape, 0)
        k = 1
        while k < seq_len:
            c = c + jnp.where(row >= k, pltpu.roll(c, k, 0), 0.0)
            k *= 2
        c_ref[...] = c

    return pl.pallas_call(
        body, name=name, grid=(T // seq_len,),
        in_specs=[pl.BlockSpec((1, seq_len, LANES), lambda b: (0, b, 0)), pl.BlockSpec((1, LANES), lambda b: (0, 0))],
        out_specs=pl.BlockSpec((seq_len, LANES), lambda b: (b, 0)),
        out_shape=jax.ShapeDtypeStruct((T, LANES), F32), compiler_params=_params(1),
    )(fl, fb)


def forget_bwd(dc, fl, fb, seq_len, name):
    T = dc.shape[0]

    def body(dc_ref, fl_ref, fb_ref, dfl_ref, dfb_ref):
        b = pl.program_id(0)
        r = dc_ref[...]
        row = lax.broadcasted_iota(jnp.int32, r.shape, 0)
        k = 1
        while k < seq_len:
            r = r + jnp.where(row < seq_len - k, pltpu.roll(r, seq_len - k, 0), 0.0)
            k *= 2
        dfl = r * jax.nn.sigmoid(-(fl_ref[0] + fb_ref[...]))
        dfl_ref[0] = dfl
        _accumulate(dfb_ref, b == 0, jnp.sum(dfl, axis=0, keepdims=True))

    return pl.pallas_call(
        body, name=name, grid=(T // seq_len,),
        in_specs=[pl.BlockSpec((seq_len, LANES), lambda b: (b, 0)), pl.BlockSpec((1, seq_len, LANES), lambda b: (0, b, 0)),
                  pl.BlockSpec((1, LANES), lambda b: (0, 0))],
        out_specs=[pl.BlockSpec((1, seq_len, LANES), lambda b: (0, b, 0)), pl.BlockSpec((1, LANES), lambda b: (0, 0))],
        out_shape=[jax.ShapeDtypeStruct((1, T, LANES), F32), jax.ShapeDtypeStruct((1, LANES), F32)],
        compiler_params=_params(1),
    )(dc, fl, fb)


HEADS = (slice(0, HEAD_DIM), slice(HEAD_DIM, 2 * HEAD_DIM))


def _key_blocks(i, step, carry):
    carry = lax.fori_loop(0, i // 2, lambda jj, c: step(2 * jj, c, False, 2), carry)
    return lax.cond(i % 2 == 1, lambda c: step(i - 1, c, True, 2), lambda c: step(i, c, True, 1), carry)


def _causal(width, tq):
    keys = lax.broadcasted_iota(jnp.int32, (width * tq, tq), 0)
    return keys <= lax.broadcasted_iota(jnp.int32, (width * tq, tq), 1) + (width - 1) * tq


def attn_fwd(qg, kv, cb, n_seq, seq_len, name):
    T = n_seq * seq_len
    tq = _tile(seq_len, TQ, LANES)
    nq = seq_len // tq

    def body(q_ref, k_ref, v_ref, cb_ref, o_ref, lse_ref):
        i = pl.program_id(2)
        q8 = [q_ref[0, :, sl] * ATT_SCALE for sl in HEADS]

        def block(j, carry, diagonal, width):
            rows = pl.ds(pl.multiple_of(j * tq, tq), width * tq)
            out = []
            for hh, sl in enumerate(HEADS):
                m, l, acc = carry[3 * hh:3 * hh + 3]
                s = lax.dot_general(k_ref[0, rows, sl], q8[hh], NT, preferred_element_type=F32)
                s = s - cb_ref[rows, sl.start:sl.start + 1]
                if diagonal:
                    s = jnp.where(_causal(width, tq), s, -1e30)
                m_new = jnp.maximum(m, jnp.max(s, axis=0, keepdims=True))
                a = jnp.exp(m - m_new)
                p = jnp.exp(s - m_new)
                l = a * l + jnp.sum(p, axis=0, keepdims=True)
                acc = a * acc + lax.dot_general(v_ref[0, rows, sl], p.astype(BF16), TN, preferred_element_type=F32)
                out += [m_new, l, acc]
            return tuple(out)

        init = (jnp.full((1, tq), -1e30, F32), jnp.zeros((1, tq), F32), jnp.zeros((HEAD_DIM, tq), F32)) * 2
        carry = _key_blocks(i, block, init)
        o_ref[...] = jnp.concatenate([carry[2] / carry[1], carry[5] / carry[4]], axis=0).T
        for hh in range(2):
            lse_ref[0, 0, 0, hh:hh + 1, :] = carry[3 * hh] + jnp.log(carry[3 * hh + 1])

    seq2 = pl.BlockSpec((seq_len, LANES), lambda b, hp, i: (b, hp))
    return pl.pallas_call(
        body, name=name, grid=(n_seq, N_HEADS // 2, nq),
        in_specs=[pl.BlockSpec((1, tq, LANES), lambda b, hp, i: (0, b * nq + i, hp)),
                  pl.BlockSpec((1, seq_len, LANES), lambda b, hp, i: (0, b, hp)),
                  pl.BlockSpec((1, seq_len, LANES), lambda b, hp, i: (1, b, hp)), seq2],
        out_specs=[pl.BlockSpec((tq, LANES), lambda b, hp, i: (b * nq + i, hp)),
                   pl.BlockSpec((1, 1, 1, 2, tq), lambda b, hp, i: (b, hp, i, 0, 0))],
        out_shape=[jax.ShapeDtypeStruct((T, D), F32), jax.ShapeDtypeStruct((n_seq, N_HEADS // 2, nq, 2, tq), F32)],
        compiler_params=_params(3),
    )(qg, kv, kv, cb)


def attn_bwd(dz, qg, kv, o, lse, cb, n_seq, seq_len, name):
    T = n_seq * seq_len
    tq = _tile(seq_len, TQ, LANES)
    nq = seq_len // tq

    def body(dz_ref, q_ref, gate_ref, o_ref, lse_ref, cb_ref, k_ref, v_ref,
             dq_ref, dgate_ref, dk_ref, dv_ref, dc_ref, p_s, dp_s, dk_s, dv_s, dc_s):
        i = pl.program_id(2)

        @pl.when(i == 0)
        def _():
            dk_s[...] = jnp.zeros_like(dk_s)
            dv_s[...] = jnp.zeros_like(dv_s)
            dc_s[...] = jnp.zeros_like(dc_s)

        dzf = dz_ref[...].astype(F32)
        sig = jax.nn.sigmoid(gate_ref[0].astype(F32))
        dob = (dzf * sig).astype(BF16)
        dgate_ref[0] = (dzf * o_ref[...] * sig * (1.0 - sig)).astype(BF16)
        q8 = [q_ref[0, :, sl] * ATT_SCALE for sl in HEADS]
        do = [dob[:, sl] for sl in HEADS]
        lse_i = [lse_ref[0, 0, 0, hh:hh + 1, :] for hh in range(2)]

        def probs(j, dsums, diagonal, width):
            rows = pl.ds(pl.multiple_of(j * tq, tq), width * tq)
            out = []
            for hh, sl in enumerate(HEADS):
                s = lax.dot_general(k_ref[0, rows, sl], q8[hh], NT, preferred_element_type=F32)
                p = jnp.exp(s - cb_ref[rows, sl.start:sl.start + 1] - lse_i[hh])
                if diagonal:
                    p = jnp.where(_causal(width, tq), p, 0.0)
                dp = lax.dot_general(v_ref[0, rows, sl], do[hh], NT, preferred_element_type=F32)
                p_s[hh, rows, :] = p
                dp_s[hh, rows, :] = dp
                out.append(dsums[hh] + jnp.sum(p * dp, axis=0, keepdims=True))
            return tuple(out)

        dsums = _key_blocks(i, probs, (jnp.zeros((1, tq), F32),) * 2)

        def grads(j, dqs, diagonal, width):
            rows = pl.ds(pl.multiple_of(j * tq, tq), width * tq)
            out = []
            for hh, sl in enumerate(HEADS):
                p = p_s[hh, rows, :]
                ds = p * (dp_s[hh, rows, :] - dsums[hh])
                dc_s[hh, rows, :] -= jnp.sum(ds, axis=1, keepdims=True)
                dsb = ds.astype(BF16)
                dk_s[rows, sl] += jnp.dot(dsb, q8[hh], preferred_element_type=F32)
                dv_s[rows, sl] += jnp.dot(p.astype(BF16), do[hh], preferred_element_type=F32)
                out.append(dqs[hh] + lax.dot_general(k_ref[0, rows, sl], dsb, TN, preferred_element_type=F32))
            return tuple(out)

        dqs = _key_blocks(i, grads, (jnp.zeros((HEAD_DIM, tq), F32),) * 2)
        dq_ref[0] = (jnp.concatenate(dqs, axis=0).T * ATT_SCALE).astype(BF16)

        @pl.when(i == nq - 1)
        def _():
            dk_ref[0] = dk_s[...].astype(BF16)
            dv_ref[0] = dv_s[...].astype(BF16)
            dc_ref[...] = jnp.zeros_like(dc_ref)
            for hh, sl in enumerate(HEADS):
                dc_ref[:, sl.start:sl.start + 1] = dc_s[hh]

    qry2 = pl.BlockSpec((tq, LANES), lambda b, hp, i: (b * nq + i, hp))
    seq2 = pl.BlockSpec((seq_len, LANES), lambda b, hp, i: (b, hp))

    def qry3(p):
        return pl.BlockSpec((1, tq, LANES), lambda b, hp, i, p=p: (p, b * nq + i, hp))

    def seq3(p):
        return pl.BlockSpec((1, seq_len, LANES), lambda b, hp, i, p=p: (p, b, hp))

    act = jax.ShapeDtypeStruct((1, T, D), BF16)
    return pl.pallas_call(
        body, name=name, grid=(n_seq, N_HEADS // 2, nq),
        in_specs=[qry2, qry3(0), qry3(1), qry2, pl.BlockSpec((1, 1, 1, 2, tq), lambda b, hp, i: (b, hp, i, 0, 0)), seq2,
                  seq3(0), seq3(1)],
        out_specs=[qry3(0), qry3(0), seq3(0), seq3(0), seq2],
        out_shape=[act, act, act, act, jax.ShapeDtypeStruct((T, D), F32)],
        scratch_shapes=[pltpu.VMEM((2, seq_len, tq), F32), pltpu.VMEM((2, seq_len, tq), F32),
                        pltpu.VMEM((seq_len, LANES), F32), pltpu.VMEM((seq_len, LANES), F32),
                        pltpu.VMEM((2, seq_len, 1), F32)],
        compiler_params=_params(3),
    )(dz, qg, qg, o, lse, cb, kv, kv)


def loss_head(y, target, name):
    T = y.shape[0]
    tm = _tile(T, TM, 8)

    def body(y_ref, t_ref, dy_ref, loss_ref):
        err = y_ref[...] - t_ref[...]
        dy_ref[...] = err * (1.0 / D)
        part = 0.5 * jnp.sum(jnp.mean(err * err, axis=-1, keepdims=True), axis=0, keepdims=True)
        _accumulate(loss_ref, pl.program_id(0) == 0, jnp.broadcast_to(part, (1, LANES)))

    row = pl.BlockSpec((tm, D), lambda i: (i, 0))
    return pl.pallas_call(
        body, name=name, grid=(T // tm,), in_specs=[row, row],
        out_specs=[row, pl.BlockSpec((1, LANES), lambda i: (0, 0))],
        out_shape=[jax.ShapeDtypeStruct((T, D), F32), jax.ShapeDtypeStruct((1, LANES), F32)],
        compiler_params=_params(1),
    )(y, target)


def _adamw(w, g, m, v):
    m = ADAM_B1 * m + (1.0 - ADAM_B1) * g
    v = ADAM_B2 * v + (1.0 - ADAM_B2) * (g * g)
    m_hat = m / (1.0 - ADAM_B1 ** ADAM_STEP)
    v_hat = v / (1.0 - ADAM_B2 ** ADAM_STEP)
    delta = -ADAM_LR * (m_hat / (jnp.sqrt(v_hat) + ADAM_EPS) + ADAM_WD * w)
    return delta, m, v


def adam_sharded(w, m, v, contribs, name):
    L, R, C = w.shape
    tr = _tile(R, 256, 16) if R % 16 == 0 else R

    def body(w_ref, m_ref, v_ref, *refs):
        c_refs, (g_ref, d_ref, nm_ref, nv_ref) = refs[:2 * L], refs[2 * L:]
        l = pl.program_id(0)
        for j in range(L):
            @pl.when(l == j)
            def _(j=j):
                own_ref, recv_ref = c_refs[2 * j], c_refs[2 * j + 1]
                g = own_ref[...].astype(F32)
                for k in range(N_DEV - 1):
                    g = g + recv_ref[k].astype(F32)
                delta, nm, nv = _adamw(w_ref[0], g, m_ref[0], v_ref[0])
                g_ref[0] = g
                d_ref[0] = delta
                nm_ref[0] = nm
                nv_ref[0] = nv

    blk = pl.BlockSpec((1, tr, C), lambda l, i: (l, i, 0))
    in_specs = [blk, blk, blk]
    inputs = [w, m, v]
    for j, (own, recv) in enumerate(contribs):
        in_specs.append(pl.BlockSpec((tr, C), lambda l, i, j=j: (jnp.where(l == j, i, 0), 0)))
        in_specs.append(pl.BlockSpec((N_DEV - 1, tr, C), lambda l, i, j=j: (0, jnp.where(l == j, i, 0), 0)))
        inputs += [own, recv]
    shp = jax.ShapeDtypeStruct((L, R, C), F32)
    return pl.pallas_call(
        body, name=name, grid=(L, R // tr), in_specs=in_specs, out_specs=[blk] * 4, out_shape=[shp] * 4,
        compiler_params=_params(2),
    )(*inputs)


def adam_small(params, total, extra_grads, name):
    n, ne = len(params), len(extra_grads)

    def body(*refs):
        total_ref, extra_refs = refs[0], refs[1:1 + ne]
        ins, outs = refs[1 + ne:1 + ne + 3 * n], refs[1 + ne + 3 * n:]
        for k, (_, _, _, where) in enumerate(params):
            if isinstance(where, int):
                g = extra_refs[where][...]
            else:
                row, rows, width = where
                g = total_ref[row:row + rows, 0:width]
            delta, nm, nv = _adamw(ins[3 * k][...], g, ins[3 * k + 1][...], ins[3 * k + 2][...])
            outs[4 * k][...] = g
            outs[4 * k + 1][...] = delta
            outs[4 * k + 2][...] = nm
            outs[4 * k + 3][...] = nv

    flat = [a for w, m, v, _ in params for a in (w, m, v)]
    out_shape = [jax.ShapeDtypeStruct(w.shape, F32) for w, _, _, _ in params for _ in range(4)]
    res = pl.pallas_call(body, name=name, out_shape=out_shape)(total, *extra_grads, *flat)
    return [res[4 * k:4 * k + 4] for k in range(n)]


def _place():
    return lax.axis_index("x"), lax.axis_index("y"), lax.axis_index("c")


def _peer(place, k):
    x, y, c = place
    return x ^ (k >> 2), y ^ ((k >> 1) & 1), c ^ (k & 1)


ANY = pl.BlockSpec(memory_space=pl.ANY)
HBM = pl.BlockSpec(memory_space=pltpu.HBM)
SEM = pl.BlockSpec(memory_space=pltpu.SEMAPHORE)
EFFECT = pltpu.SideEffectType.DATAFLOW_SIDE_EFFECTING


def _in_hbm(a):
    return pltpu.with_memory_space_constraint(a, pltpu.HBM)


def _number(place):
    return 4 * place[0] + 2 * place[1] + place[2]


SLOTS = {"gather_like": 4, "gather_pass": 3, "scatter": 7}


def _plan(mode, src_ref, land_ref, place):
    if mode == "gather_like":
        return [(src_ref, land_ref.at[_number(place)], _peer(place, k)) for k in (1, 2, 4, 6)]
    if mode == "gather_pass":
        slots = [land_ref.at[_number(_peer(place, k))] for k in (2, 4, 6)]
        return [(slot, slot, _peer(place, 1)) for slot in slots]
    return [(src_ref.at[_number(_peer(place, k))], land_ref.at[k - 1], _peer(place, k)) for k in range(1, N_DEV)]


def _exchange(name, groups, start, afters):
    sizes = [len(g[2]) for g in groups]
    na, ng = sum(sizes), len(groups)
    waits = groups[0][0] is not None
    n_in_sems = 2 * ng if waits else 0
    n_out_sems = 2 * ng if start else 0

    def body(*refs):
        src_refs, land_refs = refs[:na], refs[na:2 * na]
        in_sems = refs[2 * na:2 * na + n_in_sems]
        outs = refs[2 * na + n_in_sems + len(afters):]
        place = _place()
        a = 0
        for gi, n in enumerate(sizes):
            for idx in range(n):
                if waits:
                    zone = land_refs[a].at[pl.ds(0, groups[gi][4])]
                    copy = pltpu.make_async_remote_copy(
                        src_ref=zone, dst_ref=zone, send_sem=in_sems[2 * gi].at[idx], recv_sem=in_sems[2 * gi + 1].at[idx],
                        device_id=_peer(place, 1), device_id_type=MESH)
                    copy.wait_send()
                    copy.wait_recv()
                if start:
                    for src, dst, peer in _plan(start, src_refs[a], land_refs[a], place):
                        pltpu.make_async_remote_copy(
                            src_ref=src, dst_ref=dst, send_sem=outs[2 * gi].at[idx], recv_sem=outs[2 * gi + 1].at[idx],
                            device_id=peer, device_id_type=MESH).start()
                a += 1
        if start:
            outs[-1][...] = jnp.zeros_like(outs[-1])

    srcs = [_in_hbm(s) for g in groups for s in g[2]]
    lands = [_in_hbm(l) for g in groups for l in g[3]]
    sems = [s for g in groups for s in g[:2]] if waits else []
    out_shape = [pltpu.SemaphoreType.DMA((n,)) for n in sizes for _ in range(2)] if start else []
    out_shape += [pltpu.HBM(a.shape, a.dtype) for a in srcs + lands]
    out_specs = [SEM] * n_out_sems + [HBM] * (2 * na)
    if start:
        out_shape.append(jax.ShapeDtypeStruct((8, LANES), F32))
        out_specs.append(pl.BlockSpec(memory_space=pltpu.VMEM))
    res = pl.pallas_call(
        body, name=name, in_specs=[HBM] * (2 * na) + [SEM] * n_in_sems + [ANY] * len(afters),
        out_shape=out_shape, out_specs=out_specs,
        input_output_aliases={i: n_out_sems + i for i in range(2 * na)},
        compiler_params=pltpu.CompilerParams(has_side_effects=EFFECT),
    )(*srcs, *lands, *sems, *afters)
    new_sems, thru = res[:n_out_sems], res[n_out_sems:n_out_sems + 2 * na]
    out, a = [], 0
    for gi, n in enumerate(sizes):
        pair = (new_sems[2 * gi], new_sems[2 * gi + 1]) if start else (None, None)
        out.append(pair + (thru[a:a + n], thru[na + a:na + a + n], SLOTS.get(start, 0)))
        a += n
    return out, (res[-1] if start else None)


def exchange_start(pair_groups, mode, name):
    groups = [(None, None, [s for s, _ in g], [l for _, l in g], 0) for g in pair_groups]
    return _exchange(name, groups, mode, ())


def exchange_relay(groups, mode, afters, name):
    return _exchange(name, groups, mode, afters)


def exchange_wait(groups, afters, name):
    done, _ = _exchange(name, groups, None, afters)
    return [(g[2], g[3]) for g in done]


def all_reduce_small(parts, n_rows, name):
    R = n_rows
    n_parts = len(parts)

    def body(*refs):
        part_refs = refs[:n_parts]
        out_ref, buf, send_sems, recv_sems = refs[n_parts:]
        x, y, c = _place()
        me = 4 * x + 2 * y + c
        own = buf.at[me]
        own[...] = jnp.zeros((R, D), F32)
        for ref, (arr, row) in zip(part_refs, parts):
            own[row:row + arr.shape[0], 0:arr.shape[1]] = ref[...]
        copies = []
        for k in range(1, N_DEV):
            peer = (x ^ (k >> 2), y ^ ((k >> 1) & 1), c ^ (k & 1))
            copies.append(pltpu.make_async_remote_copy(
                src_ref=own, dst_ref=own, send_sem=send_sems.at[k - 1], recv_sem=recv_sems.at[k - 1],
                device_id=peer, device_id_type=MESH))
        for cp in copies:
            cp.start()
        for cp in copies:
            cp.wait()
        total = buf[0]
        for d in range(1, N_DEV):
            total = total + buf[d]
        out_ref[...] = total

    vm = pl.BlockSpec(memory_space=pltpu.VMEM)
    return pl.pallas_call(
        body, name=name, in_specs=[vm] * n_parts, out_specs=vm, out_shape=jax.ShapeDtypeStruct((R, D), F32),
        scratch_shapes=[pltpu.VMEM((N_DEV, R, D), F32), pltpu.SemaphoreType.DMA((N_DEV - 1,)),
                        pltpu.SemaphoreType.DMA((N_DEV - 1,))],
    )(*[arr for arr, _ in parts])


def _col_blocks(gathered, n_blocks):
    n, d, w = gathered.shape
    whole = gathered.transpose(1, 0, 2).reshape(d, n * w)
    return whole.reshape(d, n_blocks, n * w // n_blocks).transpose(1, 0, 2)[:, None]


def _col_shards(blocks):
    n, d, w = blocks.shape
    whole = blocks.transpose(1, 0, 2).reshape(d, n * w)
    return whole.reshape(d, N_DEV, n * w // N_DEV).transpose(1, 0, 2)


def kernel(x, ffn1_pre_g, ffn1_post_g, ffn1_w_in, ffn1_w_out, mix_pre_g, mix_post_g, ffn2_pre_g, ffn2_post_g, ffn2_w_in, ffn2_w_out, conv_w_in, conv_k, conv_w_out, kv_g, kv_w, forget_b, attn_w_qg, attn_w_o, loss_target, m_ffn1_pre_g, m_ffn1_post_g, m_ffn1_w_in, m_ffn1_w_out, m_mix_pre_g, m_mix_post_g, m_ffn2_pre_g, m_ffn2_post_g, m_ffn2_w_in, m_ffn2_w_out, m_conv_w_in, m_conv_k, m_conv_w_out, m_kv_g, m_kv_w, m_forget_b, m_attn_w_qg, m_attn_w_o, v_ffn1_pre_g, v_ffn1_post_g, v_ffn1_w_in, v_ffn1_w_out, v_mix_pre_g, v_mix_post_g, v_ffn2_pre_g, v_ffn2_post_g, v_ffn2_w_in, v_ffn2_w_out, v_conv_w_in, v_conv_k, v_conv_w_out, v_kv_g, v_kv_w, v_forget_b, v_attn_w_qg, v_attn_w_o):
    n_seq, seq_len, _ = x.shape
    T = n_seq * seq_len
    xi, yi, ci = _place()
    dev = 4 * xi + 2 * yi + ci
    x0 = x.reshape(T, D)
    target = loss_target.reshape(T, D)

    def zone(s):
        return lax.dynamic_update_slice(lax.empty((N_DEV,) + s.shape, s.dtype), s[None], (dev,) + (0,) * s.ndim)

    first = [ffn1_w_in[0:1].astype(BF16), ffn1_w_out[0:1].astype(BF16)]
    gathers, token = exchange_start([[(s, zone(s)) for s in first]], "gather_like", "gather_start0")

    def cast(w):
        return (w + token[0, 0]).astype(BF16)

    shard_groups = [
        [cast(conv_w_in), cast(conv_w_out), conv_k],
        [cast(ffn2_w_in[0:1]), cast(ffn2_w_out[0:1])],
        [cast(kv_w), cast(ffn1_w_in[1:2]), cast(ffn1_w_out[1:2]), cast(attn_w_qg), cast(attn_w_o)],
        [cast(ffn2_w_in[1:2]), cast(ffn2_w_out[1:2])]]
    later, token = exchange_start([[(s, zone(s)) for s in g] for g in shard_groups], "gather_like", "gather_start1")
    gathers = gathers + later

    def gathered(k, after):
        passed, _ = exchange_relay([gathers[k]], "gather_pass", [after], f"gather_pass{k}")
        return exchange_wait(passed, [after], f"gather_wait{k}")[0][1]

    fb = jnp.pad(forget_b, (0, LANES - N_HEADS))[None]

    def vec(g, l):
        return g[l:l + 1]

    def behind(g, tok):
        return g + tok[:1, :1]

    grads_small = {}
    tm_ffn = _tile(T, TM, 16)

    def ffn_fwd(xin, g_pre, g_post, w_in, w_out, tag):
        xn, (gu,) = rms_proj(xin, g_pre, [(w_in, 0)], [BF16], f"{tag}_in")
        specs = [pl.BlockSpec((gu.shape[0], tm_ffn, gu.shape[2]), lambda i: (0, i, 0))]
        a, h, y = mix_out(_swiglu_pro, [gu], specs, (w_out, 0), xin, g_post, 0.5, 4, f"{tag}_out", tm=tm_ffn)
        return y, (xin, xn, gu, a, h)

    def ffn_bwd(dy, saved, g_pre, g_post, w_in, w_out, tag):
        xin, xn, gu, a, h = saved
        dh, dg_post, dgate, dup = post_bwd(dy, h, g_post, 0.5, (w_out, 0), 4, f"{tag}_bwd_out", gu=gu, tm=tm_ffn)
        dw_out = wgrad([a], [dh[None]], 4, w_out.shape, f"{tag}_dw_out")
        dw_in = wgrad([dgate, dup], [xn[None]], 8, (8, 1, w_in.shape[3], D), f"{tag}_dw_in")
        started, tok = scatter_start([dw_in.reshape(8, -1, D), dw_out.reshape(8, -1, D)], f"{tag}_scatter_start")
        dx, dg_pre = pre_bwd([(dgate, (w_in, 0), 0), (dup, (w_in, 0), 4)], xin, behind(g_pre, tok), dy, f"{tag}_bwd_in")
        return dx, dg_pre, dg_post, started

    def scatter_start(blocked, name):
        pairs = [(g, lax.empty((N_DEV - 1,) + g.shape[1:], g.dtype)) for g in blocked]
        started, tok = exchange_start([pairs], "scatter", name)
        return started[0], tok

    w1_in, w1_out = gathered(0, token)
    x1, s_f1a = ffn_fwd(x0, vec(ffn1_pre_g, 0), vec(ffn1_post_g, 0), w1_in, w1_out, "l0_ffn1")
    cw_in_g, cw_out, ck_g = gathered(1, x1)
    cw_in = _col_blocks(cw_in_g[:, 0], 3)
    ck = ck_g[:, 0].transpose(1, 0, 2).reshape(3, D)
    xn_c, (bch,) = rms_proj(x1, vec(mix_pre_g, 0), [(cw_in, 0)], [BF16], "conv_in")
    tmc = _tile(seq_len, TM, 16)
    hb = tmc // HALO

    def cpiece(p):
        return pl.BlockSpec((1, tmc, D), lambda i, p=p: (p, i, 0))

    def chalo(p):
        return pl.BlockSpec((1, HALO, D), lambda i, p=p: (p, jnp.maximum(i * hb - 1, 0), 0))

    conv_specs = [cpiece(0), cpiece(1), cpiece(2), chalo(1), chalo(2), pl.BlockSpec((3, D), lambda i: (0, 0))]
    z_c, m_c, x2 = mix_out(_make_conv_pro(seq_len // tmc), [bch, bch, bch, bch, bch, ck], conv_specs, (cw_out, 0),
                           x1, vec(mix_post_g, 0), 1.0, 1, "conv_out", tm=tmc)
    w2_in, w2_out = gathered(2, x2)
    x3, s_f2a = ffn_fwd(x2, vec(ffn2_pre_g, 0), vec(ffn2_post_g, 0), w2_in, w2_out, "l0_ffn2")

    kvw_g, w1_in_b, w1_out_b, qgw_g, ow = gathered(3, x3)
    qg_w = _col_blocks(qgw_g[:, 0], 2)
    kv_whole = kvw_g.transpose(1, 0, 2).reshape(D, 2 * D + N_HEADS)
    kv_wb = kv_whole[:, :2 * D].reshape(D, 2, D).transpose(1, 0, 2)[:, None]
    f_w = jnp.pad(kv_whole[:, 2 * D:], ((0, 0), (0, LANES - N_HEADS)))[None, None]
    xn_kv, (kv, fl) = rms_proj(x3, kv_g[None], [(kv_wb, 0), (f_w, 0)], [BF16, F32], "kv_in")
    c128 = forget_fwd(fl, fb, seq_len, "forget_fwd")
    cb = jnp.repeat(c128[:, :N_HEADS], HEAD_DIM, axis=1)

    x4, s_f1b = ffn_fwd(x3, vec(ffn1_pre_g, 1), vec(ffn1_post_g, 1), w1_in_b, w1_out_b, "l1_ffn1")
    xn_a, (qg,) = rms_proj(x4, vec(mix_pre_g, 1), [(qg_w, 0)], [BF16], "attn_in")
    o, lse = attn_fwd(qg, kv, cb, n_seq, seq_len, "attn_fwd")
    tm = _tile(T, TM, 16)
    gate_specs = [pl.BlockSpec((1, tm, D), lambda i: (1, i, 0)), pl.BlockSpec((tm, D), lambda i: (i, 0))]
    z_a, m_a, x5 = mix_out(_attn_gate_pro, [qg, o], gate_specs, (ow, 0), x4, vec(mix_post_g, 1), 1.0, 1, "attn_out")
    w2_in_b, w2_out_b = gathered(4, x5)
    x6, s_f2b = ffn_fwd(x5, vec(ffn2_pre_g, 1), vec(ffn2_post_g, 1), w2_in_b, w2_out_b, "l1_ffn2")

    dy, loss_part = loss_head(x6, target, "loss_head")

    scatters = {}
    dx5, dg, dgp, scatters["ffn2", 1] = ffn_bwd(dy, s_f2b, vec(ffn2_pre_g, 1), vec(ffn2_post_g, 1), w2_in_b, w2_out_b, "l1_ffn2")
    grads_small["ffn2_pre", 1], grads_small["ffn2_post", 1] = dg, dgp
    dm_a, dgp, dz_a = post_bwd(dx5, m_a, vec(mix_post_g, 1), 1.0, (ow, 0), 1, "attn_bwd_out")
    grads_small["mix_post", 1] = dgp
    dq, dgate, dk, dv, dcb = attn_bwd(dz_a[0], qg, kv, o, lse, cb, n_seq, seq_len, "attn_bwd")
    dx4, dg = pre_bwd([(dq, (qg_w, 0), 0), (dgate, (qg_w, 0), 1)], x4, vec(mix_pre_g, 1), dx5, "attn_bwd_in")
    grads_small["mix_pre", 1] = dg
    d_ow = wgrad([z_a], [dm_a[None]], 1, ow.shape, "attn_dw_o")
    d_qgw = wgrad([xn_a[None]], [dq, dgate], 2, (2, 1, D, D), "attn_dw_qg")
    scatters["attn"], tok = scatter_start([_col_shards(d_qgw[:, 0]), d_ow.reshape(8, -1, D)], "attn_scatter_start")
    dx3, dg, dgp, scatters["ffn1", 1] = ffn_bwd(dx4, s_f1b, vec(ffn1_pre_g, 1), behind(vec(ffn1_post_g, 1), tok),
                                                w1_in_b, w1_out_b, "l1_ffn1")
    grads_small["ffn1_pre", 1], grads_small["ffn1_post", 1] = dg, dgp

    dc16 = dcb.reshape(T, N_HEADS, HEAD_DIM)[:, :, 0]
    dfl, dfb = forget_bwd(jnp.pad(dc16, ((0, 0), (0, LANES - N_HEADS))), fl, fb, seq_len, "forget_bwd")
    dx3, dg_kv = pre_bwd([(dk, (kv_wb, 0), 0), (dv, (kv_wb, 0), 1), (dfl, (f_w, 0), 0)], x3, kv_g[None], dx3, "kv_bwd_in")
    d_kvw = wgrad([xn_kv[None]], [dk, dv], 2, (2, 1, D, D), "kv_dw")
    d_fw = wgrad([xn_kv[None]], [dfl.astype(BF16)], 1, (1, 1, D, LANES), "forget_dw")
    d_kv_whole = jnp.concatenate([d_kvw[0, 0], d_kvw[1, 0], d_fw[0, 0, :, :N_HEADS]], axis=1)
    wshard = D * 2 + N_HEADS
    scatters["kv"], tok = scatter_start([d_kv_whole.T.reshape(N_DEV, wshard // N_DEV, D)], "kv_scatter_start")

    dx2, dg, dgp, scatters["ffn2", 0] = ffn_bwd(dx3, s_f2a, vec(ffn2_pre_g, 0), behind(vec(ffn2_post_g, 0), tok),
                                                w2_in, w2_out, "l0_ffn2")
    grads_small["ffn2_pre", 0], grads_small["ffn2_post", 0] = dg, dgp
    dm_c, dgp, dz_c = post_bwd(dx2, m_c, vec(mix_post_g, 0), 1.0, (cw_out, 0), 1, "conv_bwd_out")
    grads_small["mix_post", 0] = dgp
    dbch, d_ck = conv_bwd_mix(dz_c, bch, ck, seq_len, "conv_bwd_mix")
    dx1, dg = pre_bwd([(dbch, (cw_in, 0), 0)], x1, vec(mix_pre_g, 0), dx2, "conv_bwd_in")
    grads_small["mix_pre", 0] = dg
    d_cw_out = wgrad([z_c], [dm_c[None]], 1, cw_out.shape, "conv_dw_out")
    d_cw_in = wgrad([xn_c[None]], [dbch], 3, (3, 1, D, D), "conv_dw_in")
    scatters["conv"], tok = scatter_start([_col_shards(d_cw_in[:, 0]), d_cw_out.reshape(8, -1, D)], "conv_scatter_start")
    dx0, dg, dgp, scatters["ffn1", 0] = ffn_bwd(dx1, s_f1a, vec(ffn1_pre_g, 0), behind(vec(ffn1_post_g, 0), tok),
                                                w1_in, w1_out, "l0_ffn1")
    grads_small["ffn1_pre", 0], grads_small["ffn1_post", 0] = dg, dgp

    parts_of = {}

    def scatter_end(keys, afters, name):
        for k, (sent, recv) in zip(keys, exchange_wait([scatters[k] for k in keys], afters, name)):
            parts_of[k] = [(lax.dynamic_index_in_dim(g, dev, 0, keepdims=False), r) for g, r in zip(sent, recv)]

    scatter_end([("ffn2", 1), "attn", ("ffn1", 1), "kv", ("ffn2", 0), "conv"], [dx0], "scatter_wait")
    sharded = {"ffn2_w_in": (ffn2_w_in, m_ffn2_w_in, v_ffn2_w_in), "ffn2_w_out": (ffn2_w_out, m_ffn2_w_out, v_ffn2_w_out),
               "conv_w_in": (conv_w_in, m_conv_w_in, v_conv_w_in), "conv_w_out": (conv_w_out, m_conv_w_out, v_conv_w_out),
               "kv_w": (kv_w, m_kv_w, v_kv_w), "attn_w_qg": (attn_w_qg, m_attn_w_qg, v_attn_w_qg),
               "attn_w_o": (attn_w_o, m_attn_w_o, v_attn_w_o),
               "ffn1_w_in": (ffn1_w_in, m_ffn1_w_in, v_ffn1_w_in), "ffn1_w_out": (ffn1_w_out, m_ffn1_w_out, v_ffn1_w_out)}
    out = {}
    for nm, (w, mm, vv) in sharded.items():
        if nm == "ffn1_w_in":
            scatter_end([("ffn1", 0)], [res[0] for res in out.values()], "scatter_wait_last")
        contribs = {
            "ffn1_w_in": lambda: [parts_of["ffn1", 0][0], parts_of["ffn1", 1][0]],
            "ffn1_w_out": lambda: [parts_of["ffn1", 0][1], parts_of["ffn1", 1][1]],
            "ffn2_w_in": lambda: [parts_of["ffn2", 0][0], parts_of["ffn2", 1][0]],
            "ffn2_w_out": lambda: [parts_of["ffn2", 0][1], parts_of["ffn2", 1][1]],
            "conv_w_in": lambda: [parts_of["conv"][0]], "conv_w_out": lambda: [parts_of["conv"][1]],
            "kv_w": lambda: [parts_of["kv"][0]], "attn_w_qg": lambda: [parts_of["attn"][0]],
            "attn_w_o": lambda: [parts_of["attn"][1]]}[nm]()
        if nm in ("ffn1_w_in", "ffn2_w_in", "kv_w"):
            rows, cols = w.shape[-2:]

            def view(a):
                return a.reshape(-1, rows, cols).transpose(0, 2, 1)

            res = adam_sharded(view(w), view(mm), view(vv), contribs, f"adam_{nm}")
            out[nm] = [r.transpose(0, 2, 1).reshape(w.shape) for r in res]
        else:
            shape3 = (len(contribs),) + contribs[0][0].shape
            res = adam_sharded(w.reshape(shape3), mm.reshape(shape3), vv.reshape(shape3), contribs, f"adam_{nm}")
            out[nm] = [r.reshape(w.shape) for r in res]

    small_names = ["ffn1_pre", "ffn1_post", "mix_pre", "mix_post", "ffn2_pre", "ffn2_post"]
    parts = [(grads_small[n, l], 2 * k + l) for k, n in enumerate(small_names) for l in range(2)]
    parts += [(dg_kv, 12), (dfb, 13), (d_ck, 14), (loss_part, 17)]
    total = all_reduce_small(parts, 24, "all_reduce_small")
    loss = total[17, 0]
    d_ck_mine = lax.dynamic_slice(total, (14, dev * LANES), (3, LANES))
    gains = [(ffn1_pre_g, m_ffn1_pre_g, v_ffn1_pre_g), (ffn1_post_g, m_ffn1_post_g, v_ffn1_post_g),
             (mix_pre_g, m_mix_pre_g, v_mix_pre_g), (mix_post_g, m_mix_post_g, v_mix_post_g),
             (ffn2_pre_g, m_ffn2_pre_g, v_ffn2_pre_g), (ffn2_post_g, m_ffn2_post_g, v_ffn2_post_g)]
    small_params = [(w, m, v, (2 * k, 2, D)) for k, (w, m, v) in enumerate(gains)]
    small_params += [(kv_g[None], m_kv_g[None], v_kv_g[None], (12, 1, D)),
                     (forget_b[None], m_forget_b[None], v_forget_b[None], (13, 1, N_HEADS)),
                     (conv_k[0], m_conv_k[0], v_conv_k[0], 0)]
    small_res = adam_small(small_params, total, [d_ck_mine], "adam_small")
    small_keys = [n + "_g" for n in small_names] + ["kv_g", "forget_b", "conv_k"]
    shapes = {"kv_g": kv_g.shape, "forget_b": forget_b.shape, "conv_k": conv_k.shape}
    small = [{key: res[kind].reshape(shapes.get(key, res[kind].shape)) for key, res in zip(small_keys, small_res)}
             for kind in range(4)]
    order = ["ffn1_pre_g", "ffn1_post_g", "ffn1_w_in", "ffn1_w_out", "mix_pre_g", "mix_post_g", "ffn2_pre_g", "ffn2_post_g",
             "ffn2_w_in", "ffn2_w_out", "conv_w_in", "conv_k", "conv_w_out", "kv_g", "kv_w", "forget_b", "attn_w_qg",
             "attn_w_o"]
    results = [loss, dx0.reshape(x.shape)]
    for kind in range(4):
        for nm in order:
            results.append(out[nm][kind] if nm in out else small[kind][nm])
    return tuple(results)
```

```python
import functools
import math

import jax
import jax.numpy as jnp
from jax import lax
from jax.experimental import pallas as pl
from jax.experimental.pallas import tpu as pltpu

F32, BF16 = jnp.float32, jnp.bfloat16
D = 1024
N_HEADS = 16
HEAD_DIM = 64
N_DEV = 8
RMS_EPS = 1e-6
ATT_SCALE = 1.0 / math.sqrt(HEAD_DIM)
LANES = 128
HALO = 8
TM = 512
TQ = 512
VMEM_LIMIT = 48 * 1024 * 1024
MESH = pl.DeviceIdType.MESH

ADAM_LR, ADAM_B1, ADAM_B2, ADAM_EPS, ADAM_WD, ADAM_STEP = 0.001, 0.9, 0.999, 1e-08, 0.01, 10

NT = (((1,), (1,)), ((), ()))
TN = (((0,), (0,)), ((), ()))


def _params(n_axes, vmem_limit=VMEM_LIMIT):
    return pltpu.CompilerParams(dimension_semantics=("arbitrary",) * n_axes, vmem_limit_bytes=vmem_limit)


def _tile(n, cap, mult):
    best = None
    for t in range(mult, min(n, cap) + 1, mult):
        if n % t == 0:
            best = t
    assert best is not None, (n, cap, mult)
    return best


def _rms_rstd(x):
    return lax.rsqrt(jnp.mean(x * x, axis=-1, keepdims=True) + RMS_EPS)


def _rms_fwd(x, g):
    return x * _rms_rstd(x) * g


def _rms_bwd(x, g, dy):
    xh = x * _rms_rstd(x)
    dyg = dy * g
    dx = _rms_rstd(x) * (dyg - xh * jnp.mean(dyg * xh, axis=-1, keepdims=True))
    return dx, jnp.sum(dy * xh, axis=0, keepdims=True)


def _accumulate(ref, first, value):
    @pl.when(first)
    def _():
        ref[...] = value

    @pl.when(jnp.logical_not(first))
    def _():
        ref[...] += value


def rms_proj(x, g, ws, out_dtypes, name, transposed=False):
    T = x.shape[0]
    tm = _tile(T, TM, 16)
    na = len(ws)

    def body(x_ref, g_ref, *refs):
        w_refs, xn_ref, o_refs = refs[:na], refs[na], refs[na + 1:]
        xn = _rms_fwd(x_ref[...], g_ref[...]).astype(BF16)
        xn_ref[...] = xn
        for a, (w, l) in enumerate(ws):
            for p in range(w.shape[0]):
                if transposed:
                    y = lax.dot_general(xn, w_refs[a][p, l], NT, preferred_element_type=F32)
                else:
                    y = jnp.dot(xn, w_refs[a][p, l], preferred_element_type=F32)
                o_refs[a][p] = y.astype(o_refs[a].dtype)

    in_specs = [pl.BlockSpec((tm, D), lambda i: (i, 0)), pl.BlockSpec((1, D), lambda i: (0, 0))]
    in_specs += [pl.BlockSpec(w.shape, lambda i: (0, 0, 0, 0), pipeline_mode=pl.Buffered(1)) for w, _ in ws]
    out_specs = [pl.BlockSpec((tm, D), lambda i: (i, 0))]
    out_shape = [jax.ShapeDtypeStruct((T, D), BF16)]
    for (w, _), dt in zip(ws, out_dtypes):
        nb, wb = w.shape[0], w.shape[2 if transposed else 3]
        out_specs.append(pl.BlockSpec((nb, tm, wb), lambda i: (0, i, 0)))
        out_shape.append(jax.ShapeDtypeStruct((nb, T, wb), dt))
    res = pl.pallas_call(
        body, name=name, grid=(T // tm,), in_specs=in_specs, out_specs=out_specs, out_shape=out_shape,
        compiler_params=_params(1),
    )(x, g, *[w for w, _ in ws])
    return res[0], res[1:]


def mix_out(pro, pro_inputs, pro_specs, w, res, g_post, alpha, nk, name, tm=None):
    w4, l = w
    T = res.shape[0]
    tm = _tile(T, TM, 16) if tm is None else tm
    dpb = N_DEV // nk
    rows = w4.shape[2]
    kb = dpb * rows
    npi = len(pro_inputs)

    def body(*refs):
        pro_refs = refs[:npi]
        w_ref, res_ref, g_ref, z_ref, m_ref, y_ref = refs[npi:]
        i = pl.program_id(0)
        m = None
        for k in range(nk):
            z = pro(i, k, *pro_refs).astype(BF16)
            z_ref[k] = z
            part = jnp.dot(z, w_ref[k * dpb:(k + 1) * dpb, l].reshape(kb, D), preferred_element_type=F32)
            m = part if m is None else m + part
        m_ref[...] = m
        y_ref[...] = res_ref[...] + alpha * _rms_fwd(m, g_ref[...])

    row = pl.BlockSpec((tm, D), lambda i: (i, 0))
    in_specs = list(pro_specs) + [
        pl.BlockSpec(w4.shape, lambda i: (0, 0, 0, 0), pipeline_mode=pl.Buffered(1)), row, pl.BlockSpec((1, D), lambda i: (0, 0))]
    z, m, y = pl.pallas_call(
        body, name=name, grid=(T // tm,), in_specs=in_specs,
        out_specs=[pl.BlockSpec((nk, tm, kb), lambda i: (0, i, 0)), row, row],
        out_shape=[jax.ShapeDtypeStruct((nk, T, kb), BF16), jax.ShapeDtypeStruct((T, D), F32),
                   jax.ShapeDtypeStruct((T, D), F32)],
        compiler_params=_params(1),
    )(*pro_inputs, w4, res, g_post)
    return z, m, y


def post_bwd(dy, m, g_post, alpha, w, nk, name, gu=None, tm=None):
    w4, l = w
    T = dy.shape[0]
    tm = _tile(T, TM, 16) if tm is None else tm
    dpb = N_DEV // nk
    rows = w4.shape[2]
    kb = dpb * rows
    swiglu = gu is not None

    def body(dy_ref, m_ref, g_ref, w_ref, *refs):
        if swiglu:
            gu_ref, dm_ref, dg_ref, dgate_ref, dup_ref = refs
        else:
            dm_ref, dg_ref, dz_ref = refs
        dm, dg = _rms_bwd(m_ref[...], g_ref[...], alpha * dy_ref[...])
        dm = dm.astype(BF16)
        dm_ref[...] = dm
        _accumulate(dg_ref, pl.program_id(0) == 0, dg)
        for k in range(nk):
            dz = lax.dot_general(dm, w_ref[k * dpb:(k + 1) * dpb, l].reshape(kb, D), NT, preferred_element_type=F32)
            if swiglu:
                gate, up = gu_ref[k].astype(F32), gu_ref[k + nk].astype(F32)
                sig = jax.nn.sigmoid(gate)
                dgate_ref[k] = (dz * up * sig * (1.0 + gate * (1.0 - sig))).astype(BF16)
                dup_ref[k] = (dz * gate * sig).astype(BF16)
            else:
                dz_ref[k] = dz.astype(BF16)

    row = pl.BlockSpec((tm, D), lambda i: (i, 0))
    vec = pl.BlockSpec((1, D), lambda i: (0, 0))
    blkk = pl.BlockSpec((nk, tm, kb), lambda i: (0, i, 0))
    in_specs = [row, row, vec, pl.BlockSpec(w4.shape, lambda i: (0, 0, 0, 0), pipeline_mode=pl.Buffered(1))]
    inputs = [dy, m, g_post, w4]
    out_specs = [row, vec, blkk]
    out_shape = [jax.ShapeDtypeStruct((T, D), BF16), jax.ShapeDtypeStruct((1, D), F32),
                 jax.ShapeDtypeStruct((nk, T, kb), BF16)]
    if swiglu:
        in_specs.append(pl.BlockSpec((2 * nk, tm, kb), lambda i: (0, i, 0)))
        inputs.append(gu)
        out_specs.append(blkk)
        out_shape.append(jax.ShapeDtypeStruct((nk, T, kb), BF16))
    return pl.pallas_call(
        body, name=name, grid=(T // tm,), in_specs=in_specs, out_specs=out_specs, out_shape=out_shape,
        compiler_params=_params(1),
    )(*inputs)


def pre_bwd(pieces, x, g_pre, dres, name, transposed=False):
    T = x.shape[0]
    tm = _tile(T, TM, 16)
    na = len(pieces)
    weights = []
    for _, (w4, _), _ in pieces:
        if not any(w4 is w for w in weights):
            weights.append(w4)
    nw = len(weights)
    which = [[w4 is w for w in weights].index(True) for _, (w4, _), _ in pieces]

    def body(*refs):
        dz_refs, w_refs = refs[:na], refs[na:na + nw]
        x_ref, g_ref, dres_ref, dx_ref, dg_ref = refs[na + nw:]
        acc = None
        for a, (dz, (_, l), w_off) in enumerate(pieces):
            for p in range(dz.shape[0]):
                w = w_refs[which[a]][w_off + p, l]
                if transposed:
                    part = jnp.dot(dz_refs[a][p].astype(BF16), w, preferred_element_type=F32)
                else:
                    part = lax.dot_general(dz_refs[a][p].astype(BF16), w, NT, preferred_element_type=F32)
                acc = part if acc is None else acc + part
        dx, dg = _rms_bwd(x_ref[...], g_ref[...], acc)
        dx_ref[...] = dres_ref[...] + dx
        _accumulate(dg_ref, pl.program_id(0) == 0, dg)

    row = pl.BlockSpec((tm, D), lambda i: (i, 0))
    vec = pl.BlockSpec((1, D), lambda i: (0, 0))
    dz_specs = [pl.BlockSpec((dz.shape[0], tm, dz.shape[2]), lambda i: (0, i, 0)) for dz, _, _ in pieces]
    w_specs = [pl.BlockSpec(w.shape, lambda i: (0, 0, 0, 0), pipeline_mode=pl.Buffered(1)) for w in weights]
    return pl.pallas_call(
        body, name=name, grid=(T // tm,), in_specs=dz_specs + w_specs + [row, vec, row],
        out_specs=[row, vec], out_shape=[jax.ShapeDtypeStruct((T, D), F32), jax.ShapeDtypeStruct((1, D), F32)],
        compiler_params=_params(1),
    )(*[p[0] for p in pieces], *weights, x, g_pre, dres)


def wgrad(a_list, b_list, nout, out4_shape, name):
    T = a_list[0].shape[1]
    tm = _tile(T, 4 * TM, 16)
    nt = T // tm
    dpb = out4_shape[0] // nout
    _, _, R, C = out4_shape
    na, nb = len(a_list), len(b_list)

    def spans(arrs):
        ns = [a.shape[0] for a in arrs]
        return ns, [sum(ns[:k]) for k in range(len(ns))], sum(ns)

    a_ns, a_offs, a_tot = spans(a_list)
    b_ns, b_offs, b_tot = spans(b_list)
    assert a_tot in (1, nout) and b_tot in (1, nout)

    def body(*refs):
        a_refs, b_refs = refs[:na], refs[na:na + nb]
        out_ref, acc = refs[-2:]
        p, t = pl.program_id(0), pl.program_id(1)
        for ia in range(na):
            for ib in range(nb):
                conds = []
                if a_tot > 1:
                    conds += [p >= a_offs[ia], p < a_offs[ia] + a_ns[ia]]
                if b_tot > 1:
                    conds += [p >= b_offs[ib], p < b_offs[ib] + b_ns[ib]]

                def work(ia=ia, ib=ib):
                    part = lax.dot_general(a_refs[ia][0], b_refs[ib][0], TN, preferred_element_type=F32)
                    _accumulate(acc, t == 0, part)

                if conds:
                    pl.when(functools.reduce(jnp.logical_and, conds))(work)
                else:
                    work()

        @pl.when(t == nt - 1)
        def _():
            out_ref[...] = acc[...].astype(BF16).reshape(dpb, 1, R, C)

    def blk(off, n, tot):
        if tot == 1:
            return lambda p: 0
        return lambda p: jnp.clip(p - off, 0, n - 1)

    in_specs = []
    for arrs, ns, offs, tot in ((a_list, a_ns, a_offs, a_tot), (b_list, b_ns, b_offs, b_tot)):
        for arr, n, off in zip(arrs, ns, offs):
            in_specs.append(pl.BlockSpec((1, tm, arr.shape[2]), lambda p, t, f=blk(off, n, tot): (f(p), t, 0)))
    return pl.pallas_call(
        body, name=name, grid=(nout, nt), in_specs=in_specs,
        out_specs=pl.BlockSpec((dpb, 1, R, C), lambda p, t: (p, 0, 0, 0)),
        out_shape=jax.ShapeDtypeStruct(out4_shape, BF16),
        scratch_shapes=[pltpu.VMEM((dpb * R, C), F32)], compiler_params=_params(2),
    )(*a_list, *b_list)


def _swiglu_pro(i, k, gu_ref):
    gate, up = gu_ref[k].astype(F32), gu_ref[k + gu_ref.shape[0] // 2].astype(F32)
    return gate * jax.nn.sigmoid(gate) * up


def _attn_gate_pro(i, k, gate_ref, o_ref):
    return jax.nn.sigmoid(gate_ref[0].astype(F32)) * o_ref[...].astype(F32)


def _shift_rows(u, halo, d):
    rolled = pltpu.roll(u, d, 0)
    row = lax.broadcasted_iota(jnp.int32, u.shape, 0)
    for r in range(d):
        rolled = jnp.where(row == r, halo[HALO - d + r:HALO - d + r + 1, :], rolled)
    return rolled


def _advance_rows(u, halo, d):
    n = u.shape[0]
    rolled = pltpu.roll(u, n - d, 0)
    row = lax.broadcasted_iota(jnp.int32, u.shape, 0)
    for r in range(d):
        rolled = jnp.where(row == n - d + r, halo[r:r + 1, :], rolled)
    return rolled


def _make_conv_pro(tiles_per_seq):
    def pro(i, k, b_ref, c_ref, h_ref, ch_ref, hh_ref, ck_ref):
        u = c_ref[0].astype(F32) * h_ref[0].astype(F32)
        first = (i % tiles_per_seq) == 0
        halo = jnp.where(first, 0.0, ch_ref[0].astype(F32) * hh_ref[0].astype(F32))
        ck = ck_ref[...]
        conv = ck[2:3, :] * u + ck[1:2, :] * _shift_rows(u, halo, 1) + ck[0:1, :] * _shift_rows(u, halo, 2)
        return b_ref[0].astype(F32) * conv
    return pro


def conv_bwd_mix(dz, bch, conv_k, seq_len, name):
    T = dz.shape[1]
    tm = _tile(seq_len, TM, 16)
    tps = seq_len // tm
    nt = T // tm
    hb = tm // HALO

    def body(dz_ref, b_ref, c_ref, h_ref, cp_ref, hp_ref, dzn_ref, bn_ref, ck_ref, dbch_ref, dk_ref):
        i = pl.program_id(0)
        first = (i % tps) == 0
        last = (i % tps) == tps - 1
        b, c, h = b_ref[0].astype(F32), c_ref[0].astype(F32), h_ref[0].astype(F32)
        dzt = dz_ref[0].astype(F32)
        u = c * h
        prev = jnp.where(first, 0.0, cp_ref[0].astype(F32) * hp_ref[0].astype(F32))
        u1, u2 = _shift_rows(u, prev, 1), _shift_rows(u, prev, 2)
        ck = ck_ref[...]
        conv = ck[2:3, :] * u + ck[1:2, :] * u1 + ck[0:1, :] * u2
        dconv = dzt * b
        nxt = jnp.where(last, 0.0, dzn_ref[0].astype(F32) * bn_ref[0].astype(F32))
        du = ck[2:3, :] * dconv + ck[1:2, :] * _advance_rows(dconv, nxt, 1) + ck[0:1, :] * _advance_rows(dconv, nxt, 2)
        dbch_ref[0] = (dzt * conv).astype(BF16)
        dbch_ref[1] = (du * h).astype(BF16)
        dbch_ref[2] = (du * c).astype(BF16)
        tap = lax.broadcasted_iota(jnp.int32, (3, D), 0)
        dk = jnp.where(tap == 0, jnp.sum(dconv * u2, axis=0, keepdims=True),
                       jnp.where(tap == 1, jnp.sum(dconv * u1, axis=0, keepdims=True),
                                 jnp.sum(dconv * u, axis=0, keepdims=True)))
        _accumulate(dk_ref, i == 0, dk)

    def piece(p):
        return pl.BlockSpec((1, tm, D), lambda i, p=p: (p, i, 0))

    def prev(p):
        return pl.BlockSpec((1, HALO, D), lambda i, p=p: (p, jnp.maximum(i * hb - 1, 0), 0))

    def nxt(p):
        return pl.BlockSpec((1, HALO, D), lambda i, p=p: (p, jnp.minimum((i + 1) * hb, nt * hb - 1), 0))

    return pl.pallas_call(
        body, name=name, grid=(nt,),
        in_specs=[piece(0), piece(0), piece(1), piece(2), prev(1), prev(2), nxt(0), nxt(0),
                  pl.BlockSpec((3, D), lambda i: (0, 0))],
        out_specs=[pl.BlockSpec((3, tm, D), lambda i: (0, i, 0)), pl.BlockSpec((3, D), lambda i: (0, 0))],
        out_shape=[jax.ShapeDtypeStruct((3, T, D), BF16), jax.ShapeDtypeStruct((3, D), F32)],
        compiler_params=_params(1),
    )(dz, bch, bch, bch, bch, bch, dz, bch, conv_k)


def _log_sigmoid(x):
    return jnp.minimum(x, 0.0) - jnp.log(1.0 + jnp.exp(-jnp.abs(x)))


def forget_fwd(fl, fb, seq_len, name):
    T = fl.shape[1]

    def body(fl_ref, fb_ref, c_ref):
        c = _log_sigmoid(fl_ref[0] + fb_ref[...])
        row = lax.broadcasted_iota(jnp.int32, c.shape, 0)
        k = 1
        while k < seq_len:
            c = c + jnp.where(row >= k, pltpu.roll(c, k, 0), 0.0)
            k *= 2
        c_ref[...] = c

    return pl.pallas_call(
        body, name=name, grid=(T // seq_len,),
        in_specs=[pl.BlockSpec((1, seq_len, LANES), lambda b: (0, b, 0)), pl.BlockSpec((1, LANES), lambda b: (0, 0))],
        out_specs=pl.BlockSpec((seq_len, LANES), lambda b: (b, 0)),
        out_shape=jax.ShapeDtypeStruct((T, LANES), F32), compiler_params=_params(1),
    )(fl, fb)


def forget_bwd(dc, fl, fb, seq_len, name):
    T = dc.shape[0]

    def body(dc_ref, fl_ref, fb_ref, dfl_ref, dfb_ref):
        b = pl.program_id(0)
        r = dc_ref[...]
        row = lax.broadcasted_iota(jnp.int32, r.shape, 0)
        k = 1
        while k < seq_len:
            r = r + jnp.where(row < seq_len - k, pltpu.roll(r, seq_len - k, 0), 0.0)
            k *= 2
        dfl = r * jax.nn.sigmoid(-(fl_ref[0] + fb_ref[...]))
        dfl_ref[0] = dfl
        _accumulate(dfb_ref, b == 0, jnp.sum(dfl, axis=0, keepdims=True))

    return pl.pallas_call(
        body, name=name, grid=(T // seq_len,),
        in_specs=[pl.BlockSpec((seq_len, LANES), lambda b: (b, 0)), pl.BlockSpec((1, seq_len, LANES), lambda b: (0, b, 0)),
                  pl.BlockSpec((1, LANES), lambda b: (0, 0))],
        out_specs=[pl.BlockSpec((1, seq_len, LANES), lambda b: (0, b, 0)), pl.BlockSpec((1, LANES), lambda b: (0, 0))],
        out_shape=[jax.ShapeDtypeStruct((1, T, LANES), F32), jax.ShapeDtypeStruct((1, LANES), F32)],
        compiler_params=_params(1),
    )(dc, fl, fb)


HEADS = (slice(0, HEAD_DIM), slice(HEAD_DIM, 2 * HEAD_DIM))


def _key_blocks(i, step, carry):
    carry = lax.fori_loop(0, i // 2, lambda jj, c: step(2 * jj, c, False, 2), carry)
    return lax.cond(i % 2 == 1, lambda c: step(i - 1, c, True, 2), lambda c: step(i, c, True, 1), carry)


def _causal(width, tq):
    keys = lax.broadcasted_iota(jnp.int32, (width * tq, tq), 0)
    return keys <= lax.broadcasted_iota(jnp.int32, (width * tq, tq), 1) + (width - 1) * tq


def attn_fwd(qg, kv, cb, n_seq, seq_len, name):
    T = n_seq * seq_len
    tq = _tile(seq_len, TQ, LANES)
    nq = seq_len // tq

    def body(q_ref, k_ref, v_ref, cb_ref, o_ref, lse_ref):
        i = pl.program_id(2)
        q8 = [q_ref[0, :, sl] * ATT_SCALE for sl in HEADS]

        def block(j, carry, diagonal, width):
            rows = pl.ds(pl.multiple_of(j * tq, tq), width * tq)
            out = []
            for hh, sl in enumerate(HEADS):
                m, l, acc = carry[3 * hh:3 * hh + 3]
                s = lax.dot_general(k_ref[0, rows, sl], q8[hh], NT, preferred_element_type=F32)
                s = s - cb_ref[rows, sl.start:sl.start + 1]
                if diagonal:
                    s = jnp.where(_causal(width, tq), s, -1e30)
                m_new = jnp.maximum(m, jnp.max(s, axis=0, keepdims=True))
                a = jnp.exp(m - m_new)
                p = jnp.exp(s - m_new)
                l = a * l + jnp.sum(p, axis=0, keepdims=True)
                acc = a * acc + lax.dot_general(v_ref[0, rows, sl], p.astype(BF16), TN, preferred_element_type=F32)
                out += [m_new, l, acc]
            return tuple(out)

        init = (jnp.full((1, tq), -1e30, F32), jnp.zeros((1, tq), F32), jnp.zeros((HEAD_DIM, tq), F32)) * 2
        carry = _key_blocks(i, block, init)
        o_ref[...] = jnp.concatenate([carry[2] / carry[1], carry[5] / carry[4]], axis=0).T
        for hh in range(2):
            lse_ref[0, 0, 0, hh:hh + 1, :] = carry[3 * hh] + jnp.log(carry[3 * hh + 1])

    seq2 = pl.BlockSpec((seq_len, LANES), lambda b, hp, i: (b, hp))
    return pl.pallas_call(
        body, name=name, grid=(n_seq, N_HEADS // 2, nq),
        in_specs=[pl.BlockSpec((1, tq, LANES), lambda b, hp, i: (0, b * nq + i, hp)),
                  pl.BlockSpec((1, seq_len, LANES), lambda b, hp, i: (0, b, hp)),
                  pl.BlockSpec((1, seq_len, LANES), lambda b, hp, i: (1, b, hp)), seq2],
        out_specs=[pl.BlockSpec((tq, LANES), lambda b, hp, i: (b * nq + i, hp)),
                   pl.BlockSpec((1, 1, 1, 2, tq), lambda b, hp, i: (b, hp, i, 0, 0))],
        out_shape=[jax.ShapeDtypeStruct((T, D), F32), jax.ShapeDtypeStruct((n_seq, N_HEADS // 2, nq, 2, tq), F32)],
        compiler_params=_params(3),
    )(qg, kv, kv, cb)


def attn_bwd(dz, qg, kv, o, lse, cb, n_seq, seq_len, name):
    T = n_seq * seq_len
    tq = _tile(seq_len, TQ, LANES)
    nq = seq_len // tq

    def body(dz_ref, q_ref, gate_ref, o_ref, lse_ref, cb_ref, k_ref, v_ref,
             dq_ref, dgate_ref, dk_ref, dv_ref, dc_ref, p_s, dp_s, dk_s, dv_s, dc_s):
        i = pl.program_id(2)

        @pl.when(i == 0)
        def _():
            dk_s[...] = jnp.zeros_like(dk_s)
            dv_s[...] = jnp.zeros_like(dv_s)
            dc_s[...] = jnp.zeros_like(dc_s)

        dzf = dz_ref[...].astype(F32)
        sig = jax.nn.sigmoid(gate_ref[0].astype(F32))
        dob = (dzf * sig).astype(BF16)
        dgate_ref[0] = (dzf * o_ref[...] * sig * (1.0 - sig)).astype(BF16)
        q8 = [q_ref[0, :, sl] * ATT_SCALE for sl in HEADS]
        do = [dob[:, sl] for sl in HEADS]
        lse_i = [lse_ref[0, 0, 0, hh:hh + 1, :] for hh in range(2)]

        def probs(j, dsums, diagonal, width):
            rows = pl.ds(pl.multiple_of(j * tq, tq), width * tq)
            out = []
            for hh, sl in enumerate(HEADS):
                s = lax.dot_general(k_ref[0, rows, sl], q8[hh], NT, preferred_element_type=F32)
                p = jnp.exp(s - cb_ref[rows, sl.start:sl.start + 1] - lse_i[hh])
                if diagonal:
                    p = jnp.where(_causal(width, tq), p, 0.0)
                dp = lax.dot_general(v_ref[0, rows, sl], do[hh], NT, preferred_element_type=F32)
                p_s[hh, rows, :] = p
                dp_s[hh, rows, :] = dp
                out.append(dsums[hh] + jnp.sum(p * dp, axis=0, keepdims=True))
            return tuple(out)

        dsums = _key_blocks(i, probs, (jnp.zeros((1, tq), F32),) * 2)

        def grads(j, dqs, diagonal, width):
            rows = pl.ds(pl.multiple_of(j * tq, tq), width * tq)
            out = []
            for hh, sl in enumerate(HEADS):
                p = p_s[hh, rows, :]
                ds = p * (dp_s[hh, rows, :] - dsums[hh])
                dc_s[hh, rows, :] -= jnp.sum(ds, axis=1, keepdims=True)
                dsb = ds.astype(BF16)
                dk_s[rows, sl] += jnp.dot(dsb, q8[hh], preferred_element_type=F32)
                dv_s[rows, sl] += jnp.dot(p.astype(BF16), do[hh], preferred_element_type=F32)
                out.append(dqs[hh] + lax.dot_general(k_ref[0, rows, sl], dsb, TN, preferred_element_type=F32))
            return tuple(out)

        dqs = _key_blocks(i, grads, (jnp.zeros((HEAD_DIM, tq), F32),) * 2)
        dq_ref[0] = (jnp.concatenate(dqs, axis=0).T * ATT_SCALE).astype(BF16)

        @pl.when(i == nq - 1)
        def _():
            dk_ref[0] = dk_s[...].astype(BF16)
            dv_ref[0] = dv_s[...].astype(BF16)
            dc_ref[...] = jnp.zeros_like(dc_ref)
            for hh, sl in enumerate(HEADS):
                dc_ref[:, sl.start:sl.start + 1] = dc_s[hh]

    qry2 = pl.BlockSpec((tq, LANES), lambda b, hp, i: (b * nq + i, hp))
    seq2 = pl.BlockSpec((seq_len, LANES), lambda b, hp, i: (b, hp))

    def qry3(p):
        return pl.BlockSpec((1, tq, LANES), lambda b, hp, i, p=p: (p, b * nq + i, hp))

    def seq3(p):
        return pl.BlockSpec((1, seq_len, LANES), lambda b, hp, i, p=p: (p, b, hp))

    act = jax.ShapeDtypeStruct((1, T, D), BF16)
    return pl.pallas_call(
        body, name=name, grid=(n_seq, N_HEADS // 2, nq),
        in_specs=[qry2, qry3(0), qry3(1), qry2, pl.BlockSpec((1, 1, 1, 2, tq), lambda b, hp, i: (b, hp, i, 0, 0)), seq2,
                  seq3(0), seq3(1)],
        out_specs=[qry3(0), qry3(0), seq3(0), seq3(0), seq2],
        out_shape=[act, act, act, act, jax.ShapeDtypeStruct((T, D), F32)],
        scratch_shapes=[pltpu.VMEM((2, seq_len, tq), F32), pltpu.VMEM((2, seq_len, tq), F32),
                        pltpu.VMEM((seq_len, LANES), F32), pltpu.VMEM((seq_len, LANES), F32),
                        pltpu.VMEM((2, seq_len, 1), F32)],
        compiler_params=_params(3),
    )(dz, qg, qg, o, lse, cb, kv, kv)


def loss_head(y, target, name):
    T = y.shape[0]
    tm = _tile(T, TM, 8)

    def body(y_ref, t_ref, dy_ref, loss_ref):
        err = y_ref[...] - t_ref[...]
        dy_ref[...] = err * (1.0 / D)
        part = 0.5 * jnp.sum(jnp.mean(err * err, axis=-1, keepdims=True), axis=0, keepdims=True)
        _accumulate(loss_ref, pl.program_id(0) == 0, jnp.broadcast_to(part, (1, LANES)))

    row = pl.BlockSpec((tm, D), lambda i: (i, 0))
    return pl.pallas_call(
        body, name=name, grid=(T // tm,), in_specs=[row, row],
        out_specs=[row, pl.BlockSpec((1, LANES), lambda i: (0, 0))],
        out_shape=[jax.ShapeDtypeStruct((T, D), F32), jax.ShapeDtypeStruct((1, LANES), F32)],
        compiler_params=_params(1),
    )(y, target)


def _adamw(w, g, m, v):
    m = ADAM_B1 * m + (1.0 - ADAM_B1) * g
    v = ADAM_B2 * v + (1.0 - ADAM_B2) * (g * g)
    m_hat = m / (1.0 - ADAM_B1 ** ADAM_STEP)
    v_hat = v / (1.0 - ADAM_B2 ** ADAM_STEP)
    delta = -ADAM_LR * (m_hat / (jnp.sqrt(v_hat) + ADAM_EPS) + ADAM_WD * w)
    return delta, m, v


def adam_sharded(w, m, v, contribs, me, name):
    L, R, C = w.shape
    tr = _tile(R, 256, 16) if R % 16 == 0 else R

    def body(me_ref, w_ref, m_ref, v_ref, *refs):
        c_refs, (g_ref, d_ref, nm_ref, nv_ref) = refs[:2 * L], refs[2 * L:]
        l = pl.program_id(0)
        for j in range(L):
            @pl.when(l == j)
            def _(j=j):
                own_ref, recv_ref = c_refs[2 * j], c_refs[2 * j + 1]
                g = own_ref[0].astype(F32)
                for k in range(N_DEV - 1):
                    g = g + recv_ref[k].astype(F32)
                delta, nm, nv = _adamw(w_ref[0], g, m_ref[0], v_ref[0])
                g_ref[0] = g
                d_ref[0] = delta
                nm_ref[0] = nm
                nv_ref[0] = nv

    blk = pl.BlockSpec((1, tr, C), lambda l, i, s: (l, i, 0))
    in_specs = [blk, blk, blk]
    inputs = [w, m, v]
    for j, (mine, recv) in enumerate(contribs):
        in_specs.append(pl.BlockSpec((1, tr, C), lambda l, i, s, j=j: (s[0], jnp.where(l == j, i, 0), 0)))
        in_specs.append(pl.BlockSpec((N_DEV - 1, tr, C), lambda l, i, s, j=j: (0, jnp.where(l == j, i, 0), 0)))
        inputs += [mine, recv]
    shp = jax.ShapeDtypeStruct((L, R, C), F32)
    grid_spec = pltpu.PrefetchScalarGridSpec(num_scalar_prefetch=1, grid=(L, R // tr), in_specs=in_specs, out_specs=[blk] * 4)
    return pl.pallas_call(body, name=name, grid_spec=grid_spec, out_shape=[shp] * 4, compiler_params=_params(2))(me, *inputs)


def cast_place(w, l, me, dtype, after, name):
    _, R, C = w.shape
    tr = _tile(R, 512, 16) if R % 16 == 0 else R

    def body(me_ref, w_ref, after_ref, o_ref):
        o_ref[0] = w_ref[0].astype(dtype)

    grid_spec = pltpu.PrefetchScalarGridSpec(
        num_scalar_prefetch=1, grid=(R // tr,), in_specs=[pl.BlockSpec((1, tr, C), lambda i, s: (l, i, 0)), ANY],
        out_specs=pl.BlockSpec((1, tr, C), lambda i, s: (s[0], i, 0)))
    return pl.pallas_call(body, name=name, grid_spec=grid_spec, out_shape=jax.ShapeDtypeStruct((N_DEV, R, C), dtype),
                          compiler_params=_params(1))(me, w, after)


def adam_small(params, total, extra_grads, name):
    n, ne = len(params), len(extra_grads)

    def body(*refs):
        total_ref, extra_refs = refs[0], refs[1:1 + ne]
        ins, outs = refs[1 + ne:1 + ne + 3 * n], refs[1 + ne + 3 * n:]
        for k, (_, _, _, where) in enumerate(params):
            if isinstance(where, int):
                g = extra_refs[where][...]
            else:
                row, rows, width = where
                g = total_ref[row:row + rows, 0:width]
            delta, nm, nv = _adamw(ins[3 * k][...], g, ins[3 * k + 1][...], ins[3 * k + 2][...])
            outs[4 * k][...] = g
            outs[4 * k + 1][...] = delta
            outs[4 * k + 2][...] = nm
            outs[4 * k + 3][...] = nv

    flat = [a for w, m, v, _ in params for a in (w, m, v)]
    out_shape = [jax.ShapeDtypeStruct(w.shape, F32) for w, _, _, _ in params for _ in range(4)]
    res = pl.pallas_call(body, name=name, out_shape=out_shape)(total, *extra_grads, *flat)
    return [res[4 * k:4 * k + 4] for k in range(n)]


def _place():
    return lax.axis_index("x"), lax.axis_index("y"), lax.axis_index("c")


def _peer(place, k):
    x, y, c = place
    return x ^ (k >> 2), y ^ ((k >> 1) & 1), c ^ (k & 1)


ANY = pl.BlockSpec(memory_space=pl.ANY)
HBM = pl.BlockSpec(memory_space=pltpu.HBM)
SEM = pl.BlockSpec(memory_space=pltpu.SEMAPHORE)
EFFECT = pltpu.SideEffectType.DATAFLOW_SIDE_EFFECTING


def _in_hbm(a):
    return pltpu.with_memory_space_constraint(a, pltpu.HBM)


def _number(place):
    return 4 * place[0] + 2 * place[1] + place[2]


SLOTS = {"gather_like": 4, "gather_pass": 3, "scatter": 7}


def _plan(mode, src_ref, land_ref, place):
    if mode == "gather_like":
        mine = land_ref.at[_number(place)]
        return [(mine, mine, _peer(place, k)) for k in (1, 2, 4, 6)]
    if mode == "gather_pass":
        slots = [land_ref.at[_number(_peer(place, k))] for k in (2, 4, 6)]
        return [(slot, slot, _peer(place, 1)) for slot in slots]
    return [(src_ref.at[_number(_peer(place, k))], land_ref.at[k - 1], _peer(place, k)) for k in range(1, N_DEV)]


def _exchange(name, groups, start, afters):
    sizes = [len(g[3]) for g in groups]
    na, ng = sum(sizes), len(groups)
    ns = sum(len(g[2]) for g in groups)
    waits = groups[0][0] is not None
    n_in_sems = 2 * ng if waits else 0
    n_out_sems = 2 * ng if start else 0

    def body(*refs):
        src_refs, land_refs = (refs[:ns] if ns else [None] * na), refs[ns:ns + na]
        in_sems = refs[ns + na:ns + na + n_in_sems]
        outs = refs[ns + na + n_in_sems + len(afters):]
        place = _place()
        a = 0
        for gi, n in enumerate(sizes):
            for idx in range(n):
                if waits:
                    zone = land_refs[a].at[pl.ds(0, groups[gi][4])]
                    copy = pltpu.make_async_remote_copy(
                        src_ref=zone, dst_ref=zone, send_sem=in_sems[2 * gi].at[idx], recv_sem=in_sems[2 * gi + 1].at[idx],
                        device_id=_peer(place, 1), device_id_type=MESH)
                    copy.wait_send()
                    copy.wait_recv()
                if start:
                    for src, dst, peer in _plan(start, src_refs[a], land_refs[a], place):
                        pltpu.make_async_remote_copy(
                            src_ref=src, dst_ref=dst, send_sem=outs[2 * gi].at[idx], recv_sem=outs[2 * gi + 1].at[idx],
                            device_id=peer, device_id_type=MESH).start()
                a += 1
        if start:
            outs[-1][...] = jnp.zeros_like(outs[-1])

    srcs = [_in_hbm(s) for g in groups for s in g[2]]
    lands = [_in_hbm(l) for g in groups for l in g[3]]
    sems = [s for g in groups for s in g[:2]] if waits else []
    out_shape = [pltpu.SemaphoreType.DMA((n,)) for n in sizes for _ in range(2)] if start else []
    out_shape += [pltpu.HBM(a.shape, a.dtype) for a in srcs + lands]
    out_specs = [SEM] * n_out_sems + [HBM] * (ns + na)
    if start:
        out_shape.append(jax.ShapeDtypeStruct((8, LANES), F32))
        out_specs.append(pl.BlockSpec(memory_space=pltpu.VMEM))
    res = pl.pallas_call(
        body, name=name, in_specs=[HBM] * (ns + na) + [SEM] * n_in_sems + [ANY] * len(afters),
        out_shape=out_shape, out_specs=out_specs,
        input_output_aliases={i: n_out_sems + i for i in range(ns + na)},
        compiler_params=pltpu.CompilerParams(has_side_effects=EFFECT),
    )(*srcs, *lands, *sems, *afters)
    new_sems, thru = res[:n_out_sems], res[n_out_sems:n_out_sems + ns + na]
    out, a = [], 0
    for gi, n in enumerate(sizes):
        pair = (new_sems[2 * gi], new_sems[2 * gi + 1]) if start else (None, None)
        out.append(pair + (thru[a:a + n] if ns else [], thru[ns + a:ns + a + n], SLOTS.get(start, 0)))
        a += n
    return out, (res[-1] if start else None)


def exchange_start(pair_groups, mode, name):
    groups = [(None, None, [s for s, _ in g if s is not None], [l for _, l in g], 0) for g in pair_groups]
    return _exchange(name, groups, mode, ())


def exchange_relay(groups, mode, afters, name):
    return _exchange(name, groups, mode, afters)


def exchange_wait(groups, afters, name):
    done, _ = _exchange(name, groups, None, afters)
    return [(g[2], g[3]) for g in done]


def all_reduce_small(parts, n_rows, name):
    R = n_rows
    n_parts = len(parts)

    def body(*refs):
        part_refs = refs[:n_parts]
        out_ref, buf, send_sems, recv_sems = refs[n_parts:]
        x, y, c = _place()
        me = 4 * x + 2 * y + c
        own = buf.at[me]
        own[...] = jnp.zeros((R, D), F32)
        for ref, (arr, row) in zip(part_refs, parts):
            own[row:row + arr.shape[0], 0:arr.shape[1]] = ref[...]
        copies = []
        for k in range(1, N_DEV):
            peer = (x ^ (k >> 2), y ^ ((k >> 1) & 1), c ^ (k & 1))
            copies.append(pltpu.make_async_remote_copy(
                src_ref=own, dst_ref=own, send_sem=send_sems.at[k - 1], recv_sem=recv_sems.at[k - 1],
                device_id=peer, device_id_type=MESH))
        for cp in copies:
            cp.start()
        for cp in copies:
            cp.wait()
        total = buf[0]
        for d in range(1, N_DEV):
            total = total + buf[d]
        out_ref[...] = total

    vm = pl.BlockSpec(memory_space=pltpu.VMEM)
    return pl.pallas_call(
        body, name=name, in_specs=[vm] * n_parts, out_specs=vm, out_shape=jax.ShapeDtypeStruct((R, D), F32),
        scratch_shapes=[pltpu.VMEM((N_DEV, R, D), F32), pltpu.SemaphoreType.DMA((N_DEV - 1,)),
                        pltpu.SemaphoreType.DMA((N_DEV - 1,))],
    )(*[arr for arr, _ in parts])


def _col_blocks(gathered, n_blocks):
    n, d, w = gathered.shape
    whole = gathered.transpose(1, 0, 2).reshape(d, n * w)
    return whole.reshape(d, n_blocks, n * w // n_blocks).transpose(1, 0, 2)[:, None]


def _col_shards(blocks):
    n, d, w = blocks.shape
    whole = blocks.transpose(1, 0, 2).reshape(d, n * w)
    return whole.reshape(d, N_DEV, n * w // N_DEV).transpose(1, 0, 2)


def kernel(x, ffn1_pre_g, ffn1_post_g, ffn1_w_in, ffn1_w_out, mix_pre_g, mix_post_g, ffn2_pre_g, ffn2_post_g, ffn2_w_in, ffn2_w_out, conv_w_in, conv_k, conv_w_out, kv_g, kv_w, forget_b, attn_w_qg, attn_w_o, loss_target, m_ffn1_pre_g, m_ffn1_post_g, m_ffn1_w_in, m_ffn1_w_out, m_mix_pre_g, m_mix_post_g, m_ffn2_pre_g, m_ffn2_post_g, m_ffn2_w_in, m_ffn2_w_out, m_conv_w_in, m_conv_k, m_conv_w_out, m_kv_g, m_kv_w, m_forget_b, m_attn_w_qg, m_attn_w_o, v_ffn1_pre_g, v_ffn1_post_g, v_ffn1_w_in, v_ffn1_w_out, v_mix_pre_g, v_mix_post_g, v_ffn2_pre_g, v_ffn2_post_g, v_ffn2_w_in, v_ffn2_w_out, v_conv_w_in, v_conv_k, v_conv_w_out, v_kv_g, v_kv_w, v_forget_b, v_attn_w_qg, v_attn_w_o):
    n_seq, seq_len, _ = x.shape
    T = n_seq * seq_len
    xi, yi, ci = _place()
    dev = 4 * xi + 2 * yi + ci
    x0 = x.reshape(T, D)
    target = loss_target.reshape(T, D)

    me = dev.reshape(1).astype(jnp.int32)
    w1_t, w2_t, kv_t = ffn1_w_in.transpose(0, 2, 1), ffn2_w_in.transpose(0, 2, 1), kv_w.T[None]

    def zones(specs, after):
        return [(None, cast_place(w, l, me, dt, after, f"place_{nm}")) for nm, w, l, dt in specs]

    gathers, token = exchange_start([zones([("w1_in0", w1_t, 0, BF16), ("w1_out0", ffn1_w_out, 0, BF16)], me)],
                                    "gather_like", "gather_start0")
    shard_groups = [
        [("cw_in", conv_w_in, 0, BF16), ("cw_out", conv_w_out, 0, BF16), ("ck", conv_k, 0, F32)],
        [("w2_in0", w2_t, 0, BF16), ("w2_out0", ffn2_w_out, 0, BF16)],
        [("kv", kv_t, 0, BF16), ("w1_in1", w1_t, 1, BF16), ("w1_out1", ffn1_w_out, 1, BF16),
         ("qg", attn_w_qg, 0, BF16), ("ow", attn_w_o, 0, BF16)],
        [("w2_in1", w2_t, 1, BF16), ("w2_out1", ffn2_w_out, 1, BF16)]]
    later, token = exchange_start([zones(g, token) for g in shard_groups], "gather_like", "gather_start1")
    gathers = gathers + later

    def gathered(k, after):
        passed, _ = exchange_relay([gathers[k]], "gather_pass", [after], f"gather_pass{k}")
        return [z[:, None] for z in exchange_wait(passed, [after], f"gather_wait{k}")[0][1]]

    fb = jnp.pad(forget_b, (0, LANES - N_HEADS))[None]

    def vec(g, l):
        return g[l:l + 1]

    def behind(g, tok):
        return g + tok[:1, :1]

    grads_small = {}
    tm_ffn = _tile(T, TM, 16)

    def ffn_fwd(xin, g_pre, g_post, w_in, w_out, tag):
        xn, (gu,) = rms_proj(xin, g_pre, [(w_in, 0)], [BF16], f"{tag}_in", transposed=True)
        specs = [pl.BlockSpec((gu.shape[0], tm_ffn, gu.shape[2]), lambda i: (0, i, 0))]
        a, h, y = mix_out(_swiglu_pro, [gu], specs, (w_out, 0), xin, g_post, 0.5, 4, f"{tag}_out", tm=tm_ffn)
        return y, (xin, xn, gu, a, h)

    def ffn_bwd(dy, saved, g_pre, g_post, w_in, w_out, tag):
        xin, xn, gu, a, h = saved
        dh, dg_post, dgate, dup = post_bwd(dy, h, g_post, 0.5, (w_out, 0), 4, f"{tag}_bwd_out", gu=gu, tm=tm_ffn)
        dw_out = wgrad([a], [dh[None]], 4, w_out.shape, f"{tag}_dw_out")
        dw_in = wgrad([dgate, dup], [xn[None]], 8, w_in.shape, f"{tag}_dw_in")
        started, tok = scatter_start([dw_in.reshape(8, -1, D), dw_out.reshape(8, -1, D)], f"{tag}_scatter_start")
        dx, dg_pre = pre_bwd([(dgate, (w_in, 0), 0), (dup, (w_in, 0), 4)], xin, behind(g_pre, tok), dy, f"{tag}_bwd_in",
                             transposed=True)
        return dx, dg_pre, dg_post, started

    def scatter_start(blocked, name):
        pairs = [(g, lax.empty((N_DEV - 1,) + g.shape[1:], g.dtype)) for g in blocked]
        started, tok = exchange_start([pairs], "scatter", name)
        return started[0], tok

    w1_in, w1_out = gathered(0, token)
    x1, s_f1a = ffn_fwd(x0, vec(ffn1_pre_g, 0), vec(ffn1_post_g, 0), w1_in, w1_out, "l0_ffn1")
    cw_in_g, cw_out, ck_g = gathered(1, x1)
    cw_in = _col_blocks(cw_in_g[:, 0], 3)
    ck = ck_g[:, 0].transpose(1, 0, 2).reshape(3, D)
    xn_c, (bch,) = rms_proj(x1, vec(mix_pre_g, 0), [(cw_in, 0)], [BF16], "conv_in")
    tmc = _tile(seq_len, TM, 16)
    hb = tmc // HALO

    def cpiece(p):
        return pl.BlockSpec((1, tmc, D), lambda i, p=p: (p, i, 0))

    def chalo(p):
        return pl.BlockSpec((1, HALO, D), lambda i, p=p: (p, jnp.maximum(i * hb - 1, 0), 0))

    conv_specs = [cpiece(0), cpiece(1), cpiece(2), chalo(1), chalo(2), pl.BlockSpec((3, D), lambda i: (0, 0))]
    z_c, m_c, x2 = mix_out(_make_conv_pro(seq_len // tmc), [bch, bch, bch, bch, bch, ck], conv_specs, (cw_out, 0),
                           x1, vec(mix_post_g, 0), 1.0, 1, "conv_out", tm=tmc)
    w2_in, w2_out = gathered(2, x2)
    x3, s_f2a = ffn_fwd(x2, vec(ffn2_pre_g, 0), vec(ffn2_post_g, 0), w2_in, w2_out, "l0_ffn2")

    kvw_g, w1_in_b, w1_out_b, qgw_g, ow = gathered(3, x3)
    qg_w = _col_blocks(qgw_g[:, 0], 2)
    kv_whole = kvw_g.reshape(2 * D + N_HEADS, D).T
    kv_wb = kv_whole[:, :2 * D].reshape(D, 2, D).transpose(1, 0, 2)[:, None]
    f_w = jnp.pad(kv_whole[:, 2 * D:], ((0, 0), (0, LANES - N_HEADS)))[None, None]
    xn_kv, (kv, fl) = rms_proj(x3, kv_g[None], [(kv_wb, 0), (f_w, 0)], [BF16, F32], "kv_in")
    c128 = forget_fwd(fl, fb, seq_len, "forget_fwd")
    cb = jnp.repeat(c128[:, :N_HEADS], HEAD_DIM, axis=1)

    x4, s_f1b = ffn_fwd(x3, vec(ffn1_pre_g, 1), vec(ffn1_post_g, 1), w1_in_b, w1_out_b, "l1_ffn1")
    xn_a, (qg,) = rms_proj(x4, vec(mix_pre_g, 1), [(qg_w, 0)], [BF16], "attn_in")
    o, lse = attn_fwd(qg, kv, cb, n_seq, seq_len, "attn_fwd")
    tm = _tile(T, TM, 16)
    gate_specs = [pl.BlockSpec((1, tm, D), lambda i: (1, i, 0)), pl.BlockSpec((tm, D), lambda i: (i, 0))]
    z_a, m_a, x5 = mix_out(_attn_gate_pro, [qg, o], gate_specs, (ow, 0), x4, vec(mix_post_g, 1), 1.0, 1, "attn_out")
    w2_in_b, w2_out_b = gathered(4, x5)
    x6, s_f2b = ffn_fwd(x5, vec(ffn2_pre_g, 1), vec(ffn2_post_g, 1), w2_in_b, w2_out_b, "l1_ffn2")

    dy, loss_part = loss_head(x6, target, "loss_head")

    scatters = {}
    dx5, dg, dgp, scatters["ffn2", 1] = ffn_bwd(dy, s_f2b, vec(ffn2_pre_g, 1), vec(ffn2_post_g, 1), w2_in_b, w2_out_b, "l1_ffn2")
    grads_small["ffn2_pre", 1], grads_small["ffn2_post", 1] = dg, dgp
    dm_a, dgp, dz_a = post_bwd(dx5, m_a, vec(mix_post_g, 1), 1.0, (ow, 0), 1, "attn_bwd_out")
    grads_small["mix_post", 1] = dgp
    dq, dgate, dk, dv, dcb = attn_bwd(dz_a[0], qg, kv, o, lse, cb, n_seq, seq_len, "attn_bwd")
    dx4, dg = pre_bwd([(dq, (qg_w, 0), 0), (dgate, (qg_w, 0), 1)], x4, vec(mix_pre_g, 1), dx5, "attn_bwd_in")
    grads_small["mix_pre", 1] = dg
    d_ow = wgrad([z_a], [dm_a[None]], 1, ow.shape, "attn_dw_o")
    d_qgw = wgrad([xn_a[None]], [dq, dgate], 2, (2, 1, D, D), "attn_dw_qg")
    scatters["attn"], tok = scatter_start([_col_shards(d_qgw[:, 0]), d_ow.reshape(8, -1, D)], "attn_scatter_start")
    dx3, dg, dgp, scatters["ffn1", 1] = ffn_bwd(dx4, s_f1b, vec(ffn1_pre_g, 1), behind(vec(ffn1_post_g, 1), tok),
                                                w1_in_b, w1_out_b, "l1_ffn1")
    grads_small["ffn1_pre", 1], grads_small["ffn1_post", 1] = dg, dgp

    dc16 = dcb.reshape(T, N_HEADS, HEAD_DIM)[:, :, 0]
    dfl, dfb = forget_bwd(jnp.pad(dc16, ((0, 0), (0, LANES - N_HEADS))), fl, fb, seq_len, "forget_bwd")
    dx3, dg_kv = pre_bwd([(dk, (kv_wb, 0), 0), (dv, (kv_wb, 0), 1), (dfl, (f_w, 0), 0)], x3, kv_g[None], dx3, "kv_bwd_in")
    d_kvw = wgrad([xn_kv[None]], [dk, dv], 2, (2, 1, D, D), "kv_dw")
    d_fw = wgrad([xn_kv[None]], [dfl.astype(BF16)], 1, (1, 1, D, LANES), "forget_dw")
    d_kv_whole = jnp.concatenate([d_kvw[0, 0], d_kvw[1, 0], d_fw[0, 0, :, :N_HEADS]], axis=1)
    wshard = D * 2 + N_HEADS
    scatters["kv"], tok = scatter_start([d_kv_whole.T.reshape(N_DEV, wshard // N_DEV, D)], "kv_scatter_start")

    dx2, dg, dgp, scatters["ffn2", 0] = ffn_bwd(dx3, s_f2a, vec(ffn2_pre_g, 0), behind(vec(ffn2_post_g, 0), tok),
                                                w2_in, w2_out, "l0_ffn2")
    grads_small["ffn2_pre", 0], grads_small["ffn2_post", 0] = dg, dgp
    dm_c, dgp, dz_c = post_bwd(dx2, m_c, vec(mix_post_g, 0), 1.0, (cw_out, 0), 1, "conv_bwd_out")
    grads_small["mix_post", 0] = dgp
    dbch, d_ck = conv_bwd_mix(dz_c, bch, ck, seq_len, "conv_bwd_mix")
    dx1, dg = pre_bwd([(dbch, (cw_in, 0), 0)], x1, vec(mix_pre_g, 0), dx2, "conv_bwd_in")
    grads_small["mix_pre", 0] = dg
    d_cw_out = wgrad([z_c], [dm_c[None]], 1, cw_out.shape, "conv_dw_out")
    d_cw_in = wgrad([xn_c[None]], [dbch], 3, (3, 1, D, D), "conv_dw_in")
    scatters["conv"], tok = scatter_start([_col_shards(d_cw_in[:, 0]), d_cw_out.reshape(8, -1, D)], "conv_scatter_start")
    dx0, dg, dgp, scatters["ffn1", 0] = ffn_bwd(dx1, s_f1a, vec(ffn1_pre_g, 0), behind(vec(ffn1_post_g, 0), tok),
                                                w1_in, w1_out, "l0_ffn1")
    grads_small["ffn1_pre", 0], grads_small["ffn1_post", 0] = dg, dgp

    parts_of = {}

    def scatter_end(keys, afters, name):
        for k, (sent, recv) in zip(keys, exchange_wait([scatters[k] for k in keys], afters, name)):
            parts_of[k] = list(zip(sent, recv))

    scatter_end([("ffn2", 1), "attn", ("ffn1", 1), "kv", ("ffn2", 0), "conv"], [dx0], "scatter_wait")
    sharded = {"ffn2_w_in": (ffn2_w_in, m_ffn2_w_in, v_ffn2_w_in), "ffn2_w_out": (ffn2_w_out, m_ffn2_w_out, v_ffn2_w_out),
               "conv_w_in": (conv_w_in, m_conv_w_in, v_conv_w_in), "conv_w_out": (conv_w_out, m_conv_w_out, v_conv_w_out),
               "kv_w": (kv_w, m_kv_w, v_kv_w), "attn_w_qg": (attn_w_qg, m_attn_w_qg, v_attn_w_qg),
               "attn_w_o": (attn_w_o, m_attn_w_o, v_attn_w_o),
               "ffn1_w_in": (ffn1_w_in, m_ffn1_w_in, v_ffn1_w_in), "ffn1_w_out": (ffn1_w_out, m_ffn1_w_out, v_ffn1_w_out)}
    out = {}
    for nm, (w, mm, vv) in sharded.items():
        if nm == "ffn1_w_in":
            scatter_end([("ffn1", 0)], [res[0] for res in out.values()], "scatter_wait_last")
        contribs = {
            "ffn1_w_in": lambda: [parts_of["ffn1", 0][0], parts_of["ffn1", 1][0]],
            "ffn1_w_out": lambda: [parts_of["ffn1", 0][1], parts_of["ffn1", 1][1]],
            "ffn2_w_in": lambda: [parts_of["ffn2", 0][0], parts_of["ffn2", 1][0]],
            "ffn2_w_out": lambda: [parts_of["ffn2", 0][1], parts_of["ffn2", 1][1]],
            "conv_w_in": lambda: [parts_of["conv"][0]], "conv_w_out": lambda: [parts_of["conv"][1]],
            "kv_w": lambda: [parts_of["kv"][0]], "attn_w_qg": lambda: [parts_of["attn"][0]],
            "attn_w_o": lambda: [parts_of["attn"][1]]}[nm]()
        if nm in ("ffn1_w_in", "ffn2_w_in", "kv_w"):
            rows, cols = w.shape[-2:]

            def view(a):
                return a.reshape(-1, rows, cols).transpose(0, 2, 1)

            res = adam_sharded(view(w), view(mm), view(vv), contribs, me, f"adam_{nm}")
            out[nm] = [r.transpose(0, 2, 1).reshape(w.shape) for r in res]
        else:
            shape3 = (len(contribs),) + contribs[0][0].shape[1:]
            res = adam_sharded(w.reshape(shape3), mm.reshape(shape3), vv.reshape(shape3), contribs, me, f"adam_{nm}")
            out[nm] = [r.reshape(w.shape) for r in res]

    small_names = ["ffn1_pre", "ffn1_post", "mix_pre", "mix_post", "ffn2_pre", "ffn2_post"]
    parts = [(grads_small[n, l], 2 * k + l) for k, n in enumerate(small_names) for l in range(2)]
    parts += [(dg_kv, 12), (dfb, 13), (d_ck, 14), (loss_part, 17)]
    total = all_reduce_small(parts, 24, "all_reduce_small")
    loss = total[17, 0]
    d_ck_mine = lax.dynamic_slice(total, (14, dev * LANES), (3, LANES))
    gains = [(ffn1_pre_g, m_ffn1_pre_g, v_ffn1_pre_g), (ffn1_post_g, m_ffn1_post_g, v_ffn1_post_g),
             (mix_pre_g, m_mix_pre_g, v_mix_pre_g), (mix_post_g, m_mix_post_g, v_mix_post_g),
             (ffn2_pre_g, m_ffn2_pre_g, v_ffn2_pre_g), (ffn2_post_g, m_ffn2_post_g, v_ffn2_post_g)]
    small_params = [(w, m, v, (2 * k, 2, D)) for k, (w, m, v) in enumerate(gains)]
    small_params += [(kv_g[None], m_kv_g[None], v_kv_g[None], (12, 1, D)),
                     (forget_b[None], m_forget_b[None], v_forget_b[None], (13, 1, N_HEADS)),
                     (conv_k[0], m_conv_k[0], v_conv_k[0], 0)]
    small_res = adam_small(small_params, total, [d_ck_mine], "adam_small")
    small_keys = [n + "_g" for n in small_names] + ["kv_g", "forget_b", "conv_k"]
    shapes = {"kv_g": kv_g.shape, "forget_b": forget_b.shape, "conv_k": conv_k.shape}
    small = [{key: res[kind].reshape(shapes.get(key, res[kind].shape)) for key, res in zip(small_keys, small_res)}
             for kind in range(4)]
    order = ["ffn1_pre_g", "ffn1_post_g", "ffn1_w_in", "ffn1_w_out", "mix_pre_g", "mix_post_g", "ffn2_pre_g", "ffn2_post_g",
             "ffn2_w_in", "ffn2_w_out", "conv_w_in", "conv_k", "conv_w_out", "kv_g", "kv_w", "forget_b", "attn_w_qg",
             "attn_w_o"]
    results = [loss, dx0.reshape(x.shape)]
    for kind in range(4):
        for nm in order:
            results.append(out[nm][kind] if nm in out else small[kind][nm])
    return tuple(results)
```

```python
import functools
import math

import jax
import jax.numpy as jnp
from jax import lax
from jax.experimental import pallas as pl
from jax.experimental.pallas import tpu as pltpu

F32, BF16 = jnp.float32, jnp.bfloat16
D = 1024
N_HEADS = 16
HEAD_DIM = 64
N_DEV = 8
RMS_EPS = 1e-6
ATT_SCALE = 1.0 / math.sqrt(HEAD_DIM)
LANES = 128
HALO = 8
TM = 512
TQ = 512
VMEM_LIMIT = 48 * 1024 * 1024
MESH = pl.DeviceIdType.MESH

ADAM_LR, ADAM_B1, ADAM_B2, ADAM_EPS, ADAM_WD, ADAM_STEP = 0.001, 0.9, 0.999, 1e-08, 0.01, 10

NT = (((1,), (1,)), ((), ()))
TN = (((0,), (0,)), ((), ()))


def _params(n_axes, vmem_limit=VMEM_LIMIT):
    return pltpu.CompilerParams(dimension_semantics=("arbitrary",) * n_axes, vmem_limit_bytes=vmem_limit)


def _tile(n, cap, mult):
    best = None
    for t in range(mult, min(n, cap) + 1, mult):
        if n % t == 0:
            best = t
    assert best is not None, (n, cap, mult)
    return best


def _rms_rstd(x):
    return lax.rsqrt(jnp.mean(x * x, axis=-1, keepdims=True) + RMS_EPS)


def _rms_fwd(x, g):
    return x * _rms_rstd(x) * g


def _rms_bwd(x, g, dy):
    xh = x * _rms_rstd(x)
    dyg = dy * g
    dx = _rms_rstd(x) * (dyg - xh * jnp.mean(dyg * xh, axis=-1, keepdims=True))
    return dx, jnp.sum(dy * xh, axis=0, keepdims=True)


def _accumulate(ref, first, value):
    @pl.when(first)
    def _():
        ref[...] = value

    @pl.when(jnp.logical_not(first))
    def _():
        ref[...] += value


def rms_proj(x, g, ws, out_dtypes, name, transposed=False):
    T = x.shape[0]
    tm = _tile(T, TM, 16)
    na = len(ws)

    def body(x_ref, g_ref, *refs):
        w_refs, xn_ref, o_refs = refs[:na], refs[na], refs[na + 1:]
        xn = _rms_fwd(x_ref[...], g_ref[...]).astype(BF16)
        xn_ref[...] = xn
        for a, (w, l) in enumerate(ws):
            for p in range(w.shape[0]):
                if transposed:
                    y = lax.dot_general(xn, w_refs[a][p, l], NT, preferred_element_type=F32)
                else:
                    y = jnp.dot(xn, w_refs[a][p, l], preferred_element_type=F32)
                o_refs[a][p] = y.astype(o_refs[a].dtype)

    in_specs = [pl.BlockSpec((tm, D), lambda i: (i, 0)), pl.BlockSpec((1, D), lambda i: (0, 0))]
    in_specs += [pl.BlockSpec(w.shape, lambda i: (0, 0, 0, 0), pipeline_mode=pl.Buffered(1)) for w, _ in ws]
    out_specs = [pl.BlockSpec((tm, D), lambda i: (i, 0))]
    out_shape = [jax.ShapeDtypeStruct((T, D), BF16)]
    for (w, _), dt in zip(ws, out_dtypes):
        nb, wb = w.shape[0], w.shape[2 if transposed else 3]
        out_specs.append(pl.BlockSpec((nb, tm, wb), lambda i: (0, i, 0)))
        out_shape.append(jax.ShapeDtypeStruct((nb, T, wb), dt))
    res = pl.pallas_call(
        body, name=name, grid=(T // tm,), in_specs=in_specs, out_specs=out_specs, out_shape=out_shape,
        compiler_params=_params(1),
    )(x, g, *[w for w, _ in ws])
    return res[0], res[1:]


def mix_out(pro, pro_inputs, pro_specs, w, res, g_post, alpha, nk, name, tm=None):
    w4, l = w
    T = res.shape[0]
    tm = _tile(T, TM, 16) if tm is None else tm
    dpb = N_DEV // nk
    rows = w4.shape[2]
    kb = dpb * rows
    npi = len(pro_inputs)

    def body(*refs):
        pro_refs = refs[:npi]
        w_ref, res_ref, g_ref, z_ref, m_ref, y_ref = refs[npi:]
        i = pl.program_id(0)
        m = None
        for k in range(nk):
            z = pro(i, k, *pro_refs).astype(BF16)
            z_ref[k] = z
            part = jnp.dot(z, w_ref[k * dpb:(k + 1) * dpb, l].reshape(kb, D), preferred_element_type=F32)
            m = part if m is None else m + part
        m_ref[...] = m
        y_ref[...] = res_ref[...] + alpha * _rms_fwd(m, g_ref[...])

    row = pl.BlockSpec((tm, D), lambda i: (i, 0))
    in_specs = list(pro_specs) + [
        pl.BlockSpec(w4.shape, lambda i: (0, 0, 0, 0), pipeline_mode=pl.Buffered(1)), row, pl.BlockSpec((1, D), lambda i: (0, 0))]
    z, m, y = pl.pallas_call(
        body, name=name, grid=(T // tm,), in_specs=in_specs,
        out_specs=[pl.BlockSpec((nk, tm, kb), lambda i: (0, i, 0)), row, row],
        out_shape=[jax.ShapeDtypeStruct((nk, T, kb), BF16), jax.ShapeDtypeStruct((T, D), F32),
                   jax.ShapeDtypeStruct((T, D), F32)],
        compiler_params=_params(1),
    )(*pro_inputs, w4, res, g_post)
    return z, m, y


def post_bwd(dy, m, g_post, alpha, w, nk, name, gu=None, tm=None):
    w4, l = w
    T = dy.shape[0]
    tm = _tile(T, TM, 16) if tm is None else tm
    dpb = N_DEV // nk
    rows = w4.shape[2]
    kb = dpb * rows
    swiglu = gu is not None

    def body(dy_ref, m_ref, g_ref, w_ref, *refs):
        if swiglu:
            gu_ref, dm_ref, dg_ref, dgu_ref = refs
        else:
            dm_ref, dg_ref, dz_ref = refs
        dm, dg = _rms_bwd(m_ref[...], g_ref[...], alpha * dy_ref[...])
        dm = dm.astype(BF16)
        dm_ref[...] = dm
        _accumulate(dg_ref, pl.program_id(0) == 0, dg)
        for k in range(nk):
            dz = lax.dot_general(dm, w_ref[k * dpb:(k + 1) * dpb, l].reshape(kb, D), NT, preferred_element_type=F32)
            if swiglu:
                gate, up = gu_ref[k].astype(F32), gu_ref[k + nk].astype(F32)
                sig = jax.nn.sigmoid(gate)
                dgu_ref[k] = (dz * up * sig * (1.0 + gate * (1.0 - sig))).astype(BF16)
                dgu_ref[k + nk] = (dz * gate * sig).astype(BF16)
            else:
                dz_ref[k] = dz.astype(BF16)

    row = pl.BlockSpec((tm, D), lambda i: (i, 0))
    vec = pl.BlockSpec((1, D), lambda i: (0, 0))
    blkk = pl.BlockSpec((nk, tm, kb), lambda i: (0, i, 0))
    in_specs = [row, row, vec, pl.BlockSpec(w4.shape, lambda i: (0, 0, 0, 0), pipeline_mode=pl.Buffered(1))]
    inputs = [dy, m, g_post, w4]
    n_dz = 2 * nk if swiglu else nk
    out_specs = [row, vec, pl.BlockSpec((n_dz, tm, kb), lambda i: (0, i, 0))]
    out_shape = [jax.ShapeDtypeStruct((T, D), BF16), jax.ShapeDtypeStruct((1, D), F32),
                 jax.ShapeDtypeStruct((n_dz, T, kb), BF16)]
    if swiglu:
        in_specs.append(pl.BlockSpec((2 * nk, tm, kb), lambda i: (0, i, 0)))
        inputs.append(gu)
    return pl.pallas_call(
        body, name=name, grid=(T // tm,), in_specs=in_specs, out_specs=out_specs, out_shape=out_shape,
        compiler_params=_params(1),
    )(*inputs)


def pre_bwd(pieces, x, g_pre, dres, name, transposed=False):
    T = x.shape[0]
    tm = _tile(T, TM, 16)
    na = len(pieces)
    weights = []
    for _, (w4, _), _ in pieces:
        if not any(w4 is w for w in weights):
            weights.append(w4)
    nw = len(weights)
    which = [[w4 is w for w in weights].index(True) for _, (w4, _), _ in pieces]

    def body(*refs):
        dz_refs, w_refs = refs[:na], refs[na:na + nw]
        x_ref, g_ref, dres_ref, dx_ref, dg_ref = refs[na + nw:]
        acc = None
        for a, (dz, (_, l), w_off) in enumerate(pieces):
            for p in range(dz.shape[0]):
                w = w_refs[which[a]][w_off + p, l]
                if transposed:
                    part = jnp.dot(dz_refs[a][p].astype(BF16), w, preferred_element_type=F32)
                else:
                    part = lax.dot_general(dz_refs[a][p].astype(BF16), w, NT, preferred_element_type=F32)
                acc = part if acc is None else acc + part
        dx, dg = _rms_bwd(x_ref[...], g_ref[...], acc)
        dx_ref[...] = dres_ref[...] + dx
        _accumulate(dg_ref, pl.program_id(0) == 0, dg)

    row = pl.BlockSpec((tm, D), lambda i: (i, 0))
    vec = pl.BlockSpec((1, D), lambda i: (0, 0))
    dz_specs = [pl.BlockSpec((dz.shape[0], tm, dz.shape[2]), lambda i: (0, i, 0)) for dz, _, _ in pieces]
    w_specs = [pl.BlockSpec(w.shape, lambda i: (0, 0, 0, 0), pipeline_mode=pl.Buffered(1)) for w in weights]
    return pl.pallas_call(
        body, name=name, grid=(T // tm,), in_specs=dz_specs + w_specs + [row, vec, row],
        out_specs=[row, vec], out_shape=[jax.ShapeDtypeStruct((T, D), F32), jax.ShapeDtypeStruct((1, D), F32)],
        compiler_params=_params(1),
    )(*[p[0] for p in pieces], *weights, x, g_pre, dres)


def wgrad(a_list, b_list, nout, out4_shape, name, a_cols=None, tm_cap=None):
    T = a_list[0].shape[1]
    tm = _tile(T, 4 * TM if tm_cap is None else tm_cap, 16)
    nt = T // tm
    dpb = out4_shape[0] // nout
    _, _, R, C = out4_shape
    na, nb = len(a_list), len(b_list)
    a_w = a_list[0].shape[2] if a_cols is None else a_cols
    a_per = a_list[0].shape[2] // a_w

    def spans(arrs, per):
        ns = [a.shape[0] * per for a in arrs]
        return ns, [sum(ns[:k]) for k in range(len(ns))], sum(ns)

    a_ns, a_offs, a_tot = spans(a_list, a_per)
    b_ns, b_offs, b_tot = spans(b_list, 1)
    assert a_tot in (1, nout) and b_tot in (1, nout)

    def body(*refs):
        a_refs, b_refs = refs[:na], refs[na:na + nb]
        out_ref, acc = refs[-2:]
        p, t = pl.program_id(0), pl.program_id(1)
        for ia in range(na):
            for ib in range(nb):
                conds = []
                if a_tot > 1:
                    conds += [p >= a_offs[ia], p < a_offs[ia] + a_ns[ia]]
                if b_tot > 1:
                    conds += [p >= b_offs[ib], p < b_offs[ib] + b_ns[ib]]

                def work(ia=ia, ib=ib):
                    part = lax.dot_general(a_refs[ia][0], b_refs[ib][0], TN, preferred_element_type=F32)
                    _accumulate(acc, t == 0, part)

                if conds:
                    pl.when(functools.reduce(jnp.logical_and, conds))(work)
                else:
                    work()

        @pl.when(t == nt - 1)
        def _():
            out_ref[...] = acc[...].astype(BF16).reshape(dpb, 1, R, C)

    def blk(off, n, tot):
        if tot == 1:
            return lambda p: 0
        return lambda p: jnp.clip(p - off, 0, n - 1)

    in_specs = []
    for arr, n, off in zip(a_list, a_ns, a_offs):
        in_specs.append(pl.BlockSpec((1, tm, a_w), lambda p, t, f=blk(off, n, a_tot): (f(p) // a_per, t, f(p) % a_per)))
    for arr, n, off in zip(b_list, b_ns, b_offs):
        in_specs.append(pl.BlockSpec((1, tm, arr.shape[2]), lambda p, t, f=blk(off, n, b_tot): (f(p), t, 0)))
    return pl.pallas_call(
        body, name=name, grid=(nout, nt), in_specs=in_specs,
        out_specs=pl.BlockSpec((dpb, 1, R, C), lambda p, t: (p, 0, 0, 0)),
        out_shape=jax.ShapeDtypeStruct(out4_shape, BF16),
        scratch_shapes=[pltpu.VMEM((dpb * R, C), F32)], compiler_params=_params(2),
    )(*a_list, *b_list)


def _swiglu_pro(i, k, gu_ref):
    gate, up = gu_ref[k].astype(F32), gu_ref[k + gu_ref.shape[0] // 2].astype(F32)
    return gate * jax.nn.sigmoid(gate) * up


def _attn_gate_pro(i, k, gate_ref, o_ref):
    return jax.nn.sigmoid(gate_ref[0].astype(F32)) * o_ref[...].astype(F32)


def _shift_rows(u, halo, d):
    rolled = pltpu.roll(u, d, 0)
    row = lax.broadcasted_iota(jnp.int32, u.shape, 0)
    for r in range(d):
        rolled = jnp.where(row == r, halo[HALO - d + r:HALO - d + r + 1, :], rolled)
    return rolled


def _advance_rows(u, halo, d):
    n = u.shape[0]
    rolled = pltpu.roll(u, n - d, 0)
    row = lax.broadcasted_iota(jnp.int32, u.shape, 0)
    for r in range(d):
        rolled = jnp.where(row == n - d + r, halo[r:r + 1, :], rolled)
    return rolled


def _make_conv_pro(tiles_per_seq):
    def pro(i, k, b_ref, c_ref, h_ref, ch_ref, hh_ref, ck_ref):
        u = c_ref[0].astype(F32) * h_ref[0].astype(F32)
        first = (i % tiles_per_seq) == 0
        halo = jnp.where(first, 0.0, ch_ref[0].astype(F32) * hh_ref[0].astype(F32))
        ck = ck_ref[...]
        conv = ck[2:3, :] * u + ck[1:2, :] * _shift_rows(u, halo, 1) + ck[0:1, :] * _shift_rows(u, halo, 2)
        return b_ref[0].astype(F32) * conv
    return pro


def conv_bwd_mix(dz, bch, conv_k, seq_len, name):
    T = dz.shape[1]
    tm = _tile(seq_len, TM, 16)
    tps = seq_len // tm
    nt = T // tm
    hb = tm // HALO

    def body(dz_ref, b_ref, c_ref, h_ref, cp_ref, hp_ref, dzn_ref, bn_ref, ck_ref, dbch_ref, dk_ref):
        i = pl.program_id(0)
        first = (i % tps) == 0
        last = (i % tps) == tps - 1
        b, c, h = b_ref[0].astype(F32), c_ref[0].astype(F32), h_ref[0].astype(F32)
        dzt = dz_ref[0].astype(F32)
        u = c * h
        prev = jnp.where(first, 0.0, cp_ref[0].astype(F32) * hp_ref[0].astype(F32))
        u1, u2 = _shift_rows(u, prev, 1), _shift_rows(u, prev, 2)
        ck = ck_ref[...]
        conv = ck[2:3, :] * u + ck[1:2, :] * u1 + ck[0:1, :] * u2
        dconv = dzt * b
        nxt = jnp.where(last, 0.0, dzn_ref[0].astype(F32) * bn_ref[0].astype(F32))
        du = ck[2:3, :] * dconv + ck[1:2, :] * _advance_rows(dconv, nxt, 1) + ck[0:1, :] * _advance_rows(dconv, nxt, 2)
        dbch_ref[0] = (dzt * conv).astype(BF16)
        dbch_ref[1] = (du * h).astype(BF16)
        dbch_ref[2] = (du * c).astype(BF16)
        tap = lax.broadcasted_iota(jnp.int32, (3, D), 0)
        dk = jnp.where(tap == 0, jnp.sum(dconv * u2, axis=0, keepdims=True),
                       jnp.where(tap == 1, jnp.sum(dconv * u1, axis=0, keepdims=True),
                                 jnp.sum(dconv * u, axis=0, keepdims=True)))
        _accumulate(dk_ref, i == 0, dk)

    def piece(p):
        return pl.BlockSpec((1, tm, D), lambda i, p=p: (p, i, 0))

    def prev(p):
        return pl.BlockSpec((1, HALO, D), lambda i, p=p: (p, jnp.maximum(i * hb - 1, 0), 0))

    def nxt(p):
        return pl.BlockSpec((1, HALO, D), lambda i, p=p: (p, jnp.minimum((i + 1) * hb, nt * hb - 1), 0))

    return pl.pallas_call(
        body, name=name, grid=(nt,),
        in_specs=[piece(0), piece(0), piece(1), piece(2), prev(1), prev(2), nxt(0), nxt(0),
                  pl.BlockSpec((3, D), lambda i: (0, 0))],
        out_specs=[pl.BlockSpec((3, tm, D), lambda i: (0, i, 0)), pl.BlockSpec((3, D), lambda i: (0, 0))],
        out_shape=[jax.ShapeDtypeStruct((3, T, D), BF16), jax.ShapeDtypeStruct((3, D), F32)],
        compiler_params=_params(1),
    )(dz, bch, bch, bch, bch, bch, dz, bch, conv_k)


def _log_sigmoid(x):
    return jnp.minimum(x, 0.0) - jnp.log(1.0 + jnp.exp(-jnp.abs(x)))


def forget_fwd(fl, fb, seq_len, name):
    T = fl.shape[1]

    def body(fl_ref, fb_ref, c_ref):
        c = _log_sigmoid(fl_ref[0] + fb_ref[...])
        row = lax.broadcasted_iota(jnp.int32, c.shape, 0)
        k = 1
        while k < seq_len:
            c = c + jnp.where(row >= k, pltpu.roll(c, k, 0), 0.0)
            k *= 2
        c_ref[...] = c

    return pl.pallas_call(
        body, name=name, grid=(T // seq_len,),
        in_specs=[pl.BlockSpec((1, seq_len, LANES), lambda b: (0, b, 0)), pl.BlockSpec((1, LANES), lambda b: (0, 0))],
        out_specs=pl.BlockSpec((seq_len, LANES), lambda b: (b, 0)),
        out_shape=jax.ShapeDtypeStruct((T, LANES), F32), compiler_params=_params(1),
    )(fl, fb)


def forget_bwd(dc, fl, fb, seq_len, name):
    T = dc.shape[0]

    def body(dc_ref, fl_ref, fb_ref, dfl_ref, dfb_ref):
        b = pl.program_id(0)
        r = dc_ref[...]
        row = lax.broadcasted_iota(jnp.int32, r.shape, 0)
        k = 1
        while k < seq_len:
            r = r + jnp.where(row < seq_len - k, pltpu.roll(r, seq_len - k, 0), 0.0)
            k *= 2
        dfl = r * jax.nn.sigmoid(-(fl_ref[0] + fb_ref[...]))
        dfl_ref[0] = dfl
        _accumulate(dfb_ref, b == 0, jnp.sum(dfl, axis=0, keepdims=True))

    return pl.pallas_call(
        body, name=name, grid=(T // seq_len,),
        in_specs=[pl.BlockSpec((seq_len, LANES), lambda b: (b, 0)), pl.BlockSpec((1, seq_len, LANES), lambda b: (0, b, 0)),
                  pl.BlockSpec((1, LANES), lambda b: (0, 0))],
        out_specs=[pl.BlockSpec((1, seq_len, LANES), lambda b: (0, b, 0)), pl.BlockSpec((1, LANES), lambda b: (0, 0))],
        out_shape=[jax.ShapeDtypeStruct((1, T, LANES), F32), jax.ShapeDtypeStruct((1, LANES), F32)],
        compiler_params=_params(1),
    )(dc, fl, fb)


HEADS = (slice(0, HEAD_DIM), slice(HEAD_DIM, 2 * HEAD_DIM))


def _key_blocks(i, step, carry):
    carry = lax.fori_loop(0, i // 2, lambda jj, c: step(2 * jj, c, False, 2), carry)
    return lax.cond(i % 2 == 1, lambda c: step(i - 1, c, True, 2), lambda c: step(i, c, True, 1), carry)


def _causal(width, tq):
    keys = lax.broadcasted_iota(jnp.int32, (width * tq, tq), 0)
    return keys <= lax.broadcasted_iota(jnp.int32, (width * tq, tq), 1) + (width - 1) * tq


def attn_fwd(qg, kv, cb, n_seq, seq_len, name):
    T = n_seq * seq_len
    tq = _tile(seq_len, TQ, LANES)
    nq = seq_len // tq

    def body(q_ref, k_ref, v_ref, cb_ref, o_ref, lse_ref):
        i = pl.program_id(2)
        q8 = [q_ref[0, :, sl] * ATT_SCALE for sl in HEADS]

        def block(j, carry, diagonal, width):
            rows = pl.ds(pl.multiple_of(j * tq, tq), width * tq)
            out = []
            for hh, sl in enumerate(HEADS):
                m, l, acc = carry[3 * hh:3 * hh + 3]
                s = lax.dot_general(k_ref[0, rows, sl], q8[hh], NT, preferred_element_type=F32)
                s = s - cb_ref[rows, sl.start:sl.start + 1]
                if diagonal:
                    s = jnp.where(_causal(width, tq), s, -1e30)
                m_new = jnp.maximum(m, jnp.max(s, axis=0, keepdims=True))
                a = jnp.exp(m - m_new)
                p = jnp.exp(s - m_new)
                l = a * l + jnp.sum(p, axis=0, keepdims=True)
                acc = a * acc + lax.dot_general(v_ref[0, rows, sl], p.astype(BF16), TN, preferred_element_type=F32)
                out += [m_new, l, acc]
            return tuple(out)

        init = (jnp.full((1, tq), -1e30, F32), jnp.zeros((1, tq), F32), jnp.zeros((HEAD_DIM, tq), F32)) * 2
        carry = _key_blocks(i, block, init)
        o_ref[...] = jnp.concatenate([carry[2] / carry[1], carry[5] / carry[4]], axis=0).T
        for hh in range(2):
            lse_ref[0, 0, 0, hh:hh + 1, :] = carry[3 * hh] + jnp.log(carry[3 * hh + 1])

    seq2 = pl.BlockSpec((seq_len, LANES), lambda b, hp, i: (b, hp))
    return pl.pallas_call(
        body, name=name, grid=(n_seq, N_HEADS // 2, nq),
        in_specs=[pl.BlockSpec((1, tq, LANES), lambda b, hp, i: (0, b * nq + i, hp)),
                  pl.BlockSpec((1, seq_len, LANES), lambda b, hp, i: (0, b, hp)),
                  pl.BlockSpec((1, seq_len, LANES), lambda b, hp, i: (1, b, hp)), seq2],
        out_specs=[pl.BlockSpec((tq, LANES), lambda b, hp, i: (b * nq + i, hp)),
                   pl.BlockSpec((1, 1, 1, 2, tq), lambda b, hp, i: (b, hp, i, 0, 0))],
        out_shape=[jax.ShapeDtypeStruct((T, D), F32), jax.ShapeDtypeStruct((n_seq, N_HEADS // 2, nq, 2, tq), F32)],
        compiler_params=_params(3),
    )(qg, kv, kv, cb)


def attn_bwd(dz, qg, kv, o, lse, cb, n_seq, seq_len, name):
    T = n_seq * seq_len
    tq = _tile(seq_len, TQ, LANES)
    nq = seq_len // tq

    def body(dz_ref, q_ref, gate_ref, o_ref, lse_ref, cb_ref, k_ref, v_ref,
             dq_ref, dgate_ref, dk_ref, dv_ref, dc_ref, p_s, dp_s, dk_s, dv_s, dc_s):
        i = pl.program_id(2)

        @pl.when(i == 0)
        def _():
            dk_s[...] = jnp.zeros_like(dk_s)
            dv_s[...] = jnp.zeros_like(dv_s)
            dc_s[...] = jnp.zeros_like(dc_s)

        dzf = dz_ref[...].astype(F32)
        sig = jax.nn.sigmoid(gate_ref[0].astype(F32))
        dob = (dzf * sig).astype(BF16)
        dgate_ref[0] = (dzf * o_ref[...] * sig * (1.0 - sig)).astype(BF16)
        q8 = [q_ref[0, :, sl] * ATT_SCALE for sl in HEADS]
        do = [dob[:, sl] for sl in HEADS]
        lse_i = [lse_ref[0, 0, 0, hh:hh + 1, :] for hh in range(2)]

        def probs(j, dsums, diagonal, width):
            rows = pl.ds(pl.multiple_of(j * tq, tq), width * tq)
            out = []
            for hh, sl in enumerate(HEADS):
                s = lax.dot_general(k_ref[0, rows, sl], q8[hh], NT, preferred_element_type=F32)
                p = jnp.exp(s - cb_ref[rows, sl.start:sl.start + 1] - lse_i[hh])
                if diagonal:
                    p = jnp.where(_causal(width, tq), p, 0.0)
                dp = lax.dot_general(v_ref[0, rows, sl], do[hh], NT, preferred_element_type=F32)
                p_s[hh, rows, :] = p
                dp_s[hh, rows, :] = dp
                out.append(dsums[hh] + jnp.sum(p * dp, axis=0, keepdims=True))
            return tuple(out)

        dsums = _key_blocks(i, probs, (jnp.zeros((1, tq), F32),) * 2)

        def grads(j, dqs, diagonal, width):
            rows = pl.ds(pl.multiple_of(j * tq, tq), width * tq)
            out = []
            for hh, sl in enumerate(HEADS):
                p = p_s[hh, rows, :]
                ds = p * (dp_s[hh, rows, :] - dsums[hh])
                dc_s[hh, rows, :] -= jnp.sum(ds, axis=1, keepdims=True)
                dsb = ds.astype(BF16)
                dk_s[rows, sl] += jnp.dot(dsb, q8[hh], preferred_element_type=F32)
                dv_s[rows, sl] += jnp.dot(p.astype(BF16), do[hh], preferred_element_type=F32)
                out.append(dqs[hh] + lax.dot_general(k_ref[0, rows, sl], dsb, TN, preferred_element_type=F32))
            return tuple(out)

        dqs = _key_blocks(i, grads, (jnp.zeros((HEAD_DIM, tq), F32),) * 2)
        dq_ref[0] = (jnp.concatenate(dqs, axis=0).T * ATT_SCALE).astype(BF16)

        @pl.when(i == nq - 1)
        def _():
            dk_ref[0] = dk_s[...].astype(BF16)
            dv_ref[0] = dv_s[...].astype(BF16)
            dc_ref[...] = jnp.zeros_like(dc_ref)
            for hh, sl in enumerate(HEADS):
                dc_ref[:, sl.start:sl.start + 1] = dc_s[hh]

    qry2 = pl.BlockSpec((tq, LANES), lambda b, hp, i: (b * nq + i, hp))
    seq2 = pl.BlockSpec((seq_len, LANES), lambda b, hp, i: (b, hp))

    def qry3(p):
        return pl.BlockSpec((1, tq, LANES), lambda b, hp, i, p=p: (p, b * nq + i, hp))

    def seq3(p):
        return pl.BlockSpec((1, seq_len, LANES), lambda b, hp, i, p=p: (p, b, hp))

    act = jax.ShapeDtypeStruct((1, T, D), BF16)
    return pl.pallas_call(
        body, name=name, grid=(n_seq, N_HEADS // 2, nq),
        in_specs=[qry2, qry3(0), qry3(1), qry2, pl.BlockSpec((1, 1, 1, 2, tq), lambda b, hp, i: (b, hp, i, 0, 0)), seq2,
                  seq3(0), seq3(1)],
        out_specs=[qry3(0), qry3(0), seq3(0), seq3(0), seq2],
        out_shape=[act, act, act, act, jax.ShapeDtypeStruct((T, D), F32)],
        scratch_shapes=[pltpu.VMEM((2, seq_len, tq), F32), pltpu.VMEM((2, seq_len, tq), F32),
                        pltpu.VMEM((seq_len, LANES), F32), pltpu.VMEM((seq_len, LANES), F32),
                        pltpu.VMEM((2, seq_len, 1), F32)],
        compiler_params=_params(3),
    )(dz, qg, qg, o, lse, cb, kv, kv)


def loss_head(y, target, name):
    T = y.shape[0]
    tm = _tile(T, TM, 8)

    def body(y_ref, t_ref, dy_ref, loss_ref):
        err = y_ref[...] - t_ref[...]
        dy_ref[...] = err * (1.0 / D)
        part = 0.5 * jnp.sum(jnp.mean(err * err, axis=-1, keepdims=True), axis=0, keepdims=True)
        _accumulate(loss_ref, pl.program_id(0) == 0, jnp.broadcast_to(part, (1, LANES)))

    row = pl.BlockSpec((tm, D), lambda i: (i, 0))
    return pl.pallas_call(
        body, name=name, grid=(T // tm,), in_specs=[row, row],
        out_specs=[row, pl.BlockSpec((1, LANES), lambda i: (0, 0))],
        out_shape=[jax.ShapeDtypeStruct((T, D), F32), jax.ShapeDtypeStruct((1, LANES), F32)],
        compiler_params=_params(1),
    )(y, target)


def _adamw(w, g, m, v):
    m = ADAM_B1 * m + (1.0 - ADAM_B1) * g
    v = ADAM_B2 * v + (1.0 - ADAM_B2) * (g * g)
    m_hat = m / (1.0 - ADAM_B1 ** ADAM_STEP)
    v_hat = v / (1.0 - ADAM_B2 ** ADAM_STEP)
    delta = -ADAM_LR * (m_hat / (jnp.sqrt(v_hat) + ADAM_EPS) + ADAM_WD * w)
    return delta, m, v


def adam_sharded(w, m, v, contribs, me, name):
    L, R, C = w.shape
    tr = _tile(R, 256, 16) if R % 16 == 0 else R

    def body(me_ref, w_ref, m_ref, v_ref, *refs):
        c_refs, (g_ref, d_ref, nm_ref, nv_ref) = refs[:2 * L], refs[2 * L:]
        l = pl.program_id(0)
        for j in range(L):
            @pl.when(l == j)
            def _(j=j):
                own_ref, recv_ref = c_refs[2 * j], c_refs[2 * j + 1]
                g = own_ref[0].astype(F32)
                for k in range(N_DEV - 1):
                    g = g + recv_ref[k].astype(F32)
                delta, nm, nv = _adamw(w_ref[0], g, m_ref[0], v_ref[0])
                g_ref[0] = g
                d_ref[0] = delta
                nm_ref[0] = nm
                nv_ref[0] = nv

    blk = pl.BlockSpec((1, tr, C), lambda l, i, s: (l, i, 0))
    in_specs = [blk, blk, blk]
    inputs = [w, m, v]
    for j, (mine, recv) in enumerate(contribs):
        in_specs.append(pl.BlockSpec((1, tr, C), lambda l, i, s, j=j: (s[0], jnp.where(l == j, i, 0), 0)))
        in_specs.append(pl.BlockSpec((N_DEV - 1, tr, C), lambda l, i, s, j=j: (0, jnp.where(l == j, i, 0), 0)))
        inputs += [mine, recv]
    shp = jax.ShapeDtypeStruct((L, R, C), F32)
    grid_spec = pltpu.PrefetchScalarGridSpec(num_scalar_prefetch=1, grid=(L, R // tr), in_specs=in_specs, out_specs=[blk] * 4)
    return pl.pallas_call(body, name=name, grid_spec=grid_spec, out_shape=[shp] * 4, compiler_params=_params(2))(me, *inputs)


def cast_place(w, l, me, dtype, after, name):
    _, R, C = w.shape
    tr = _tile(R, 512, 16) if R % 16 == 0 else R

    def body(me_ref, w_ref, after_ref, o_ref):
        o_ref[0] = w_ref[0].astype(dtype)

    grid_spec = pltpu.PrefetchScalarGridSpec(
        num_scalar_prefetch=1, grid=(R // tr,), in_specs=[pl.BlockSpec((1, tr, C), lambda i, s: (l, i, 0)), ANY],
        out_specs=pl.BlockSpec((1, tr, C), lambda i, s: (s[0], i, 0)))
    return pl.pallas_call(body, name=name, grid_spec=grid_spec, out_shape=jax.ShapeDtypeStruct((N_DEV, R, C), dtype),
                          compiler_params=_params(1))(me, w, after)


def adam_small(params, total, extra_grads, name):
    n, ne = len(params), len(extra_grads)

    def body(*refs):
        total_ref, extra_refs = refs[0], refs[1:1 + ne]
        ins, outs = refs[1 + ne:1 + ne + 3 * n], refs[1 + ne + 3 * n:]
        for k, (_, _, _, where) in enumerate(params):
            if isinstance(where, int):
                g = extra_refs[where][...]
            else:
                row, rows, width = where
                g = total_ref[row:row + rows, 0:width]
            delta, nm, nv = _adamw(ins[3 * k][...], g, ins[3 * k + 1][...], ins[3 * k + 2][...])
            outs[4 * k][...] = g
            outs[4 * k + 1][...] = delta
            outs[4 * k + 2][...] = nm
            outs[4 * k + 3][...] = nv

    flat = [a for w, m, v, _ in params for a in (w, m, v)]
    out_shape = [jax.ShapeDtypeStruct(w.shape, F32) for w, _, _, _ in params for _ in range(4)]
    res = pl.pallas_call(body, name=name, out_shape=out_shape)(total, *extra_grads, *flat)
    return [res[4 * k:4 * k + 4] for k in range(n)]


def _place():
    return lax.axis_index("x"), lax.axis_index("y"), lax.axis_index("c")


def _peer(place, k):
    x, y, c = place
    return x ^ (k >> 2), y ^ ((k >> 1) & 1), c ^ (k & 1)


ANY = pl.BlockSpec(memory_space=pl.ANY)
HBM = pl.BlockSpec(memory_space=pltpu.HBM)
SEM = pl.BlockSpec(memory_space=pltpu.SEMAPHORE)
EFFECT = pltpu.SideEffectType.DATAFLOW_SIDE_EFFECTING


def _in_hbm(a):
    return pltpu.with_memory_space_constraint(a, pltpu.HBM)


def _number(place):
    return 4 * place[0] + 2 * place[1] + place[2]


SLOTS = {"gather_like": 4, "gather_pass": 3, "scatter": 7}


def _plan(mode, src_ref, land_ref, place):
    if mode == "gather_like":
        mine = land_ref.at[_number(place)]
        return [(mine, mine, _peer(place, k)) for k in (1, 2, 4, 6)]
    if mode == "gather_pass":
        slots = [land_ref.at[_number(_peer(place, k))] for k in (2, 4, 6)]
        return [(slot, slot, _peer(place, 1)) for slot in slots]
    return [(src_ref.at[_number(_peer(place, k))], land_ref.at[k - 1], _peer(place, k)) for k in range(1, N_DEV)]


def _exchange(name, groups, start, afters):
    sizes = [len(g[3]) for g in groups]
    na, ng = sum(sizes), len(groups)
    ns = sum(len(g[2]) for g in groups)
    waits = groups[0][0] is not None
    n_in_sems = 2 * ng if waits else 0
    n_out_sems = 2 * ng if start else 0

    def body(*refs):
        src_refs, land_refs = (refs[:ns] if ns else [None] * na), refs[ns:ns + na]
        in_sems = refs[ns + na:ns + na + n_in_sems]
        outs = refs[ns + na + n_in_sems + len(afters):]
        place = _place()
        a = 0
        for gi, n in enumerate(sizes):
            for idx in range(n):
                if waits:
                    zone = land_refs[a].at[pl.ds(0, groups[gi][4])]
                    copy = pltpu.make_async_remote_copy(
                        src_ref=zone, dst_ref=zone, send_sem=in_sems[2 * gi].at[idx], recv_sem=in_sems[2 * gi + 1].at[idx],
                        device_id=_peer(place, 1), device_id_type=MESH)
                    copy.wait_send()
                    copy.wait_recv()
                if start:
                    for src, dst, peer in _plan(start, src_refs[a], land_refs[a], place):
                        pltpu.make_async_remote_copy(
                            src_ref=src, dst_ref=dst, send_sem=outs[2 * gi].at[idx], recv_sem=outs[2 * gi + 1].at[idx],
                            device_id=peer, device_id_type=MESH).start()
                a += 1
        if start:
            outs[-1][...] = jnp.zeros_like(outs[-1])

    srcs = [_in_hbm(s) for g in groups for s in g[2]]
    lands = [_in_hbm(l) for g in groups for l in g[3]]
    sems = [s for g in groups for s in g[:2]] if waits else []
    out_shape = [pltpu.SemaphoreType.DMA((n,)) for n in sizes for _ in range(2)] if start else []
    out_shape += [pltpu.HBM(a.shape, a.dtype) for a in srcs + lands]
    out_specs = [SEM] * n_out_sems + [HBM] * (ns + na)
    if start:
        out_shape.append(jax.ShapeDtypeStruct((8, LANES), F32))
        out_specs.append(pl.BlockSpec(memory_space=pltpu.VMEM))
    res = pl.pallas_call(
        body, name=name, in_specs=[HBM] * (ns + na) + [SEM] * n_in_sems + [ANY] * len(afters),
        out_shape=out_shape, out_specs=out_specs,
        input_output_aliases={i: n_out_sems + i for i in range(ns + na)},
        compiler_params=pltpu.CompilerParams(has_side_effects=EFFECT),
    )(*srcs, *lands, *sems, *afters)
    new_sems, thru = res[:n_out_sems], res[n_out_sems:n_out_sems + ns + na]
    out, a = [], 0
    for gi, n in enumerate(sizes):
        pair = (new_sems[2 * gi], new_sems[2 * gi + 1]) if start else (None, None)
        out.append(pair + (thru[a:a + n] if ns else [], thru[ns + a:ns + a + n], SLOTS.get(start, 0)))
        a += n
    return out, (res[-1] if start else None)


def exchange_start(pair_groups, mode, name):
    groups = [(None, None, [s for s, _ in g if s is not None], [l for _, l in g], 0) for g in pair_groups]
    return _exchange(name, groups, mode, ())


def exchange_relay(groups, mode, afters, name):
    return _exchange(name, groups, mode, afters)


def exchange_wait(groups, afters, name):
    done, _ = _exchange(name, groups, None, afters)
    return [(g[2], g[3]) for g in done]


def all_reduce_small(parts, n_rows, name):
    R = n_rows
    n_parts = len(parts)

    def body(*refs):
        part_refs = refs[:n_parts]
        out_ref, buf, send_sems, recv_sems = refs[n_parts:]
        x, y, c = _place()
        me = 4 * x + 2 * y + c
        own = buf.at[me]
        own[...] = jnp.zeros((R, D), F32)
        for ref, (arr, row) in zip(part_refs, parts):
            own[row:row + arr.shape[0], 0:arr.shape[1]] = ref[...]
        copies = []
        for k in range(1, N_DEV):
            peer = (x ^ (k >> 2), y ^ ((k >> 1) & 1), c ^ (k & 1))
            copies.append(pltpu.make_async_remote_copy(
                src_ref=own, dst_ref=own, send_sem=send_sems.at[k - 1], recv_sem=recv_sems.at[k - 1],
                device_id=peer, device_id_type=MESH))
        for cp in copies:
            cp.start()
        for cp in copies:
            cp.wait()
        total = buf[0]
        for d in range(1, N_DEV):
            total = total + buf[d]
        out_ref[...] = total

    vm = pl.BlockSpec(memory_space=pltpu.VMEM)
    return pl.pallas_call(
        body, name=name, in_specs=[vm] * n_parts, out_specs=vm, out_shape=jax.ShapeDtypeStruct((R, D), F32),
        scratch_shapes=[pltpu.VMEM((N_DEV, R, D), F32), pltpu.SemaphoreType.DMA((N_DEV - 1,)),
                        pltpu.SemaphoreType.DMA((N_DEV - 1,))],
    )(*[arr for arr, _ in parts])


def _col_blocks(gathered, n_blocks):
    n, d, w = gathered.shape
    whole = gathered.transpose(1, 0, 2).reshape(d, n * w)
    return whole.reshape(d, n_blocks, n * w // n_blocks).transpose(1, 0, 2)[:, None]


def _col_shards(blocks):
    n, d, w = blocks.shape
    whole = blocks.transpose(1, 0, 2).reshape(d, n * w)
    return whole.reshape(d, N_DEV, n * w // N_DEV).transpose(1, 0, 2)


def kernel(x, ffn1_pre_g, ffn1_post_g, ffn1_w_in, ffn1_w_out, mix_pre_g, mix_post_g, ffn2_pre_g, ffn2_post_g, ffn2_w_in, ffn2_w_out, conv_w_in, conv_k, conv_w_out, kv_g, kv_w, forget_b, attn_w_qg, attn_w_o, loss_target, m_ffn1_pre_g, m_ffn1_post_g, m_ffn1_w_in, m_ffn1_w_out, m_mix_pre_g, m_mix_post_g, m_ffn2_pre_g, m_ffn2_post_g, m_ffn2_w_in, m_ffn2_w_out, m_conv_w_in, m_conv_k, m_conv_w_out, m_kv_g, m_kv_w, m_forget_b, m_attn_w_qg, m_attn_w_o, v_ffn1_pre_g, v_ffn1_post_g, v_ffn1_w_in, v_ffn1_w_out, v_mix_pre_g, v_mix_post_g, v_ffn2_pre_g, v_ffn2_post_g, v_ffn2_w_in, v_ffn2_w_out, v_conv_w_in, v_conv_k, v_conv_w_out, v_kv_g, v_kv_w, v_forget_b, v_attn_w_qg, v_attn_w_o):
    n_seq, seq_len, _ = x.shape
    T = n_seq * seq_len
    xi, yi, ci = _place()
    dev = 4 * xi + 2 * yi + ci
    x0 = x.reshape(T, D)
    target = loss_target.reshape(T, D)

    me = dev.reshape(1).astype(jnp.int32)
    w1_t, w2_t, kv_t = ffn1_w_in.transpose(0, 2, 1), ffn2_w_in.transpose(0, 2, 1), kv_w.T[None]

    def zones(specs, after):
        return [(None, cast_place(w, l, me, dt, after, f"place_{nm}")) for nm, w, l, dt in specs]

    gathers, token = exchange_start([zones([("w1_in0", w1_t, 0, BF16), ("w1_out0", ffn1_w_out, 0, BF16)], me)],
                                    "gather_like", "gather_start0")
    shard_groups = [
        [("cw_in", conv_w_in, 0, BF16), ("cw_out", conv_w_out, 0, BF16), ("ck", conv_k, 0, F32)],
        [("w2_in0", w2_t, 0, BF16), ("w2_out0", ffn2_w_out, 0, BF16)],
        [("kv", kv_t, 0, BF16), ("w1_in1", w1_t, 1, BF16), ("w1_out1", ffn1_w_out, 1, BF16),
         ("qg", attn_w_qg, 0, BF16), ("ow", attn_w_o, 0, BF16)],
        [("w2_in1", w2_t, 1, BF16), ("w2_out1", ffn2_w_out, 1, BF16)]]
    later, token = exchange_start([zones(g, token) for g in shard_groups], "gather_like", "gather_start1")
    gathers = gathers + later

    def gathered(k, after):
        passed, _ = exchange_relay([gathers[k]], "gather_pass", [after], f"gather_pass{k}")
        return [z[:, None] for z in exchange_wait(passed, [after], f"gather_wait{k}")[0][1]]

    fb = jnp.pad(forget_b, (0, LANES - N_HEADS))[None]

    def vec(g, l):
        return g[l:l + 1]

    def behind(g, tok):
        return g + tok[:1, :1]

    grads_small = {}
    tm_ffn = _tile(T, TM // 2, 16)

    def ffn_fwd(xin, g_pre, g_post, w_in, w_out, tag):
        w_gu = w_in.reshape(2, 1, -1, D)
        xn, (gu,) = rms_proj(xin, g_pre, [(w_gu, 0)], [BF16], f"{tag}_in", transposed=True)
        specs = [pl.BlockSpec((2, tm_ffn, gu.shape[2]), lambda i: (0, i, 0))]
        a, h, y = mix_out(_swiglu_pro, [gu], specs, (w_out, 0), xin, g_post, 0.5, 1, f"{tag}_out", tm=tm_ffn)
        return y, (xin, xn, gu, a, h)

    def ffn_bwd(dy, saved, g_pre, g_post, w_in, w_out, tag):
        xin, xn, gu, a, h = saved
        w_gu = w_in.reshape(2, 1, -1, D)
        half = gu.shape[2] // 2
        dh, dg_post, dgu = post_bwd(dy, h, g_post, 0.5, (w_out, 0), 1, f"{tag}_bwd_out", gu=gu, tm=tm_ffn)
        dw_out = wgrad([a], [dh[None]], 2, (2, 1, half, D), f"{tag}_dw_out", a_cols=half)
        dw_in = wgrad([dgu], [xn[None]], 4, (4, 1, half, D), f"{tag}_dw_in", a_cols=half)
        started, tok = scatter_start([dw_in.reshape(8, -1, D), dw_out.reshape(8, -1, D)], f"{tag}_scatter_start")
        dx, dg_pre = pre_bwd([(dgu, (w_gu, 0), 0)], xin, behind(g_pre, tok), dy, f"{tag}_bwd_in", transposed=True)
        return dx, dg_pre, dg_post, started

    def scatter_start(blocked, name):
        pairs = [(g, lax.empty((N_DEV - 1,) + g.shape[1:], g.dtype)) for g in blocked]
        started, tok = exchange_start([pairs], "scatter", name)
        return started[0], tok

    w1_in, w1_out = gathered(0, token)
    x1, s_f1a = ffn_fwd(x0, vec(ffn1_pre_g, 0), vec(ffn1_post_g, 0), w1_in, w1_out, "l0_ffn1")
    cw_in_g, cw_out, ck_g = gathered(1, x1)
    cw_in = _col_blocks(cw_in_g[:, 0], 3)
    ck = ck_g[:, 0].transpose(1, 0, 2).reshape(3, D)
    xn_c, (bch,) = rms_proj(x1, vec(mix_pre_g, 0), [(cw_in, 0)], [BF16], "conv_in")
    tmc = _tile(seq_len, TM, 16)
    hb = tmc // HALO

    def cpiece(p):
        return pl.BlockSpec((1, tmc, D), lambda i, p=p: (p, i, 0))

    def chalo(p):
        return pl.BlockSpec((1, HALO, D), lambda i, p=p: (p, jnp.maximum(i * hb - 1, 0), 0))

    conv_specs = [cpiece(0), cpiece(1), cpiece(2), chalo(1), chalo(2), pl.BlockSpec((3, D), lambda i: (0, 0))]
    z_c, m_c, x2 = mix_out(_make_conv_pro(seq_len // tmc), [bch, bch, bch, bch, bch, ck], conv_specs, (cw_out, 0),
                           x1, vec(mix_post_g, 0), 1.0, 1, "conv_out", tm=tmc)
    w2_in, w2_out = gathered(2, x2)
    x3, s_f2a = ffn_fwd(x2, vec(ffn2_pre_g, 0), vec(ffn2_post_g, 0), w2_in, w2_out, "l0_ffn2")

    kvw_g, w1_in_b, w1_out_b, qgw_g, ow = gathered(3, x3)
    qg_w = _col_blocks(qgw_g[:, 0], 2)
    kv_whole = kvw_g.reshape(2 * D + N_HEADS, D).T
    kv_wb = kv_whole[:, :2 * D].reshape(D, 2, D).transpose(1, 0, 2)[:, None]
    f_w = jnp.pad(kv_whole[:, 2 * D:], ((0, 0), (0, LANES - N_HEADS)))[None, None]
    xn_kv, (kv, fl) = rms_proj(x3, kv_g[None], [(kv_wb, 0), (f_w, 0)], [BF16, F32], "kv_in")
    c128 = forget_fwd(fl, fb, seq_len, "forget_fwd")
    cb = jnp.repeat(c128[:, :N_HEADS], HEAD_DIM, axis=1)

    x4, s_f1b = ffn_fwd(x3, vec(ffn1_pre_g, 1), vec(ffn1_post_g, 1), w1_in_b, w1_out_b, "l1_ffn1")
    xn_a, (qg,) = rms_proj(x4, vec(mix_pre_g, 1), [(qg_w, 0)], [BF16], "attn_in")
    o, lse = attn_fwd(qg, kv, cb, n_seq, seq_len, "attn_fwd")
    tm = _tile(T, TM, 16)
    gate_specs = [pl.BlockSpec((1, tm, D), lambda i: (1, i, 0)), pl.BlockSpec((tm, D), lambda i: (i, 0))]
    z_a, m_a, x5 = mix_out(_attn_gate_pro, [qg, o], gate_specs, (ow, 0), x4, vec(mix_post_g, 1), 1.0, 1, "attn_out")
    w2_in_b, w2_out_b = gathered(4, x5)
    x6, s_f2b = ffn_fwd(x5, vec(ffn2_pre_g, 1), vec(ffn2_post_g, 1), w2_in_b, w2_out_b, "l1_ffn2")

    dy, loss_part = loss_head(x6, target, "loss_head")

    scatters = {}
    dx5, dg, dgp, scatters["ffn2", 1] = ffn_bwd(dy, s_f2b, vec(ffn2_pre_g, 1), vec(ffn2_post_g, 1), w2_in_b, w2_out_b, "l1_ffn2")
    grads_small["ffn2_pre", 1], grads_small["ffn2_post", 1] = dg, dgp
    dm_a, dgp, dz_a = post_bwd(dx5, m_a, vec(mix_post_g, 1), 1.0, (ow, 0), 1, "attn_bwd_out")
    grads_small["mix_post", 1] = dgp
    dq, dgate, dk, dv, dcb = attn_bwd(dz_a[0], qg, kv, o, lse, cb, n_seq, seq_len, "attn_bwd")
    dx4, dg = pre_bwd([(dq, (qg_w, 0), 0), (dgate, (qg_w, 0), 1)], x4, vec(mix_pre_g, 1), dx5, "attn_bwd_in")
    grads_small["mix_pre", 1] = dg
    d_ow = wgrad([z_a], [dm_a[None]], 1, ow.shape, "attn_dw_o")
    d_qgw = wgrad([xn_a[None]], [dq, dgate], 2, (2, 1, D, D), "attn_dw_qg")
    scatters["attn"], tok = scatter_start([_col_shards(d_qgw[:, 0]), d_ow.reshape(8, -1, D)], "attn_scatter_start")
    dx3, dg, dgp, scatters["ffn1", 1] = ffn_bwd(dx4, s_f1b, vec(ffn1_pre_g, 1), behind(vec(ffn1_post_g, 1), tok),
                                                w1_in_b, w1_out_b, "l1_ffn1")
    grads_small["ffn1_pre", 1], grads_small["ffn1_post", 1] = dg, dgp

    dc16 = dcb.reshape(T, N_HEADS, HEAD_DIM)[:, :, 0]
    dfl, dfb = forget_bwd(jnp.pad(dc16, ((0, 0), (0, LANES - N_HEADS))), fl, fb, seq_len, "forget_bwd")
    dx3, dg_kv = pre_bwd([(dk, (kv_wb, 0), 0), (dv, (kv_wb, 0), 1), (dfl, (f_w, 0), 0)], x3, kv_g[None], dx3, "kv_bwd_in")
    d_kvw = wgrad([xn_kv[None]], [dk, dv], 2, (2, 1, D, D), "kv_dw")
    d_fw = wgrad([xn_kv[None]], [dfl.astype(BF16)], 1, (1, 1, D, LANES), "forget_dw")
    d_kv_whole = jnp.concatenate([d_kvw[0, 0], d_kvw[1, 0], d_fw[0, 0, :, :N_HEADS]], axis=1)
    wshard = D * 2 + N_HEADS
    scatters["kv"], tok = scatter_start([d_kv_whole.T.reshape(N_DEV, wshard // N_DEV, D)], "kv_scatter_start")

    dx2, dg, dgp, scatters["ffn2", 0] = ffn_bwd(dx3, s_f2a, vec(ffn2_pre_g, 0), behind(vec(ffn2_post_g, 0), tok),
                                                w2_in, w2_out, "l0_ffn2")
    grads_small["ffn2_pre", 0], grads_small["ffn2_post", 0] = dg, dgp
    dm_c, dgp, dz_c = post_bwd(dx2, m_c, vec(mix_post_g, 0), 1.0, (cw_out, 0), 1, "conv_bwd_out")
    grads_small["mix_post", 0] = dgp
    dbch, d_ck = conv_bwd_mix(dz_c, bch, ck, seq_len, "conv_bwd_mix")
    dx1, dg = pre_bwd([(dbch, (cw_in, 0), 0)], x1, vec(mix_pre_g, 0), dx2, "conv_bwd_in")
    grads_small["mix_pre", 0] = dg
    d_cw_out = wgrad([z_c], [dm_c[None]], 1, cw_out.shape, "conv_dw_out")
    d_cw_in = wgrad([xn_c[None]], [dbch], 3, (3, 1, D, D), "conv_dw_in")
    scatters["conv"], tok = scatter_start([_col_shards(d_cw_in[:, 0]), d_cw_out.reshape(8, -1, D)], "conv_scatter_start")
    dx0, dg, dgp, scatters["ffn1", 0] = ffn_bwd(dx1, s_f1a, vec(ffn1_pre_g, 0), behind(vec(ffn1_post_g, 0), tok),
                                                w1_in, w1_out, "l0_ffn1")
    grads_small["ffn1_pre", 0], grads_small["ffn1_post", 0] = dg, dgp

    parts_of = {}

    def scatter_end(keys, afters, name):
        for k, (sent, recv) in zip(keys, exchange_wait([scatters[k] for k in keys], afters, name)):
            parts_of[k] = list(zip(sent, recv))

    scatter_end([("ffn2", 1), "attn", ("ffn1", 1), "kv", ("ffn2", 0), "conv"], [dx0], "scatter_wait")
    sharded = {"ffn2_w_in": (ffn2_w_in, m_ffn2_w_in, v_ffn2_w_in), "ffn2_w_out": (ffn2_w_out, m_ffn2_w_out, v_ffn2_w_out),
               "conv_w_in": (conv_w_in, m_conv_w_in, v_conv_w_in), "conv_w_out": (conv_w_out, m_conv_w_out, v_conv_w_out),
               "kv_w": (kv_w, m_kv_w, v_kv_w), "attn_w_qg": (attn_w_qg, m_attn_w_qg, v_attn_w_qg),
               "attn_w_o": (attn_w_o, m_attn_w_o, v_attn_w_o),
               "ffn1_w_in": (ffn1_w_in, m_ffn1_w_in, v_ffn1_w_in), "ffn1_w_out": (ffn1_w_out, m_ffn1_w_out, v_ffn1_w_out)}
    out = {}
    for nm, (w, mm, vv) in sharded.items():
        if nm == "ffn1_w_in":
            scatter_end([("ffn1", 0)], [res[0] for res in out.values()], "scatter_wait_last")
        contribs = {
            "ffn1_w_in": lambda: [parts_of["ffn1", 0][0], parts_of["ffn1", 1][0]],
            "ffn1_w_out": lambda: [parts_of["ffn1", 0][1], parts_of["ffn1", 1][1]],
            "ffn2_w_in": lambda: [parts_of["ffn2", 0][0], parts_of["ffn2", 1][0]],
            "ffn2_w_out": lambda: [parts_of["ffn2", 0][1], parts_of["ffn2", 1][1]],
            "conv_w_in": lambda: [parts_of["conv"][0]], "conv_w_out": lambda: [parts_of["conv"][1]],
            "kv_w": lambda: [parts_of["kv"][0]], "attn_w_qg": lambda: [parts_of["attn"][0]],
            "attn_w_o": lambda: [parts_of["attn"][1]]}[nm]()
        if nm in ("ffn1_w_in", "ffn2_w_in", "kv_w"):
            rows, cols = w.shape[-2:]

            def view(a):
                return a.reshape(-1, rows, cols).transpose(0, 2, 1)

            res = adam_sharded(view(w), view(mm), view(vv), contribs, me, f"adam_{nm}")
            out[nm] = [r.transpose(0, 2, 1).reshape(w.shape) for r in res]
        else:
            shape3 = (len(contribs),) + contribs[0][0].shape[1:]
            res = adam_sharded(w.reshape(shape3), mm.reshape(shape3), vv.reshape(shape3), contribs, me, f"adam_{nm}")
            out[nm] = [r.reshape(w.shape) for r in res]

    small_names = ["ffn1_pre", "ffn1_post", "mix_pre", "mix_post", "ffn2_pre", "ffn2_post"]
    parts = [(grads_small[n, l], 2 * k + l) for k, n in enumerate(small_names) for l in range(2)]
    parts += [(dg_kv, 12), (dfb, 13), (d_ck, 14), (loss_part, 17)]
    total = all_reduce_small(parts, 24, "all_reduce_small")
    loss = total[17, 0]
    d_ck_mine = lax.dynamic_slice(total, (14, dev * LANES), (3, LANES))
    gains = [(ffn1_pre_g, m_ffn1_pre_g, v_ffn1_pre_g), (ffn1_post_g, m_ffn1_post_g, v_ffn1_post_g),
             (mix_pre_g, m_mix_pre_g, v_mix_pre_g), (mix_post_g, m_mix_post_g, v_mix_post_g),
             (ffn2_pre_g, m_ffn2_pre_g, v_ffn2_pre_g), (ffn2_post_g, m_ffn2_post_g, v_ffn2_post_g)]
    small_params = [(w, m, v, (2 * k, 2, D)) for k, (w, m, v) in enumerate(gains)]
    small_params += [(kv_g[None], m_kv_g[None], v_kv_g[None], (12, 1, D)),
                     (forget_b[None], m_forget_b[None], v_forget_b[None], (13, 1, N_HEADS)),
                     (conv_k[0], m_conv_k[0], v_conv_k[0], 0)]
    small_res = adam_small(small_params, total, [d_ck_mine], "adam_small")
    small_keys = [n + "_g" for n in small_names] + ["kv_g", "forget_b", "conv_k"]
    shapes = {"kv_g": kv_g.shape, "forget_b": forget_b.shape, "conv_k": conv_k.shape}
    small = [{key: res[kind].reshape(shapes.get(key, res[kind].shape)) for key, res in zip(small_keys, small_res)}
             for kind in range(4)]
    order = ["ffn1_pre_g", "ffn1_post_g", "ffn1_w_in", "ffn1_w_out", "mix_pre_g", "mix_post_g", "ffn2_pre_g", "ffn2_post_g",
             "ffn2_w_in", "ffn2_w_out", "conv_w_in", "conv_k", "conv_w_out", "kv_g", "kv_w", "forget_b", "attn_w_qg",
             "attn_w_o"]
    results = [loss, dx0.reshape(x.shape)]
    for kind in range(4):
        for nm in order:
            results.append(out[nm][kind] if nm in out else small[kind][nm])
    return tuple(results)
```

```python
import functools
import math

import jax
import jax.numpy as jnp
from jax import lax
from jax.experimental import pallas as pl
from jax.experimental.pallas import tpu as pltpu

F32, BF16 = jnp.float32, jnp.bfloat16
D = 1024
N_HEADS = 16
HEAD_DIM = 64
N_DEV = 8
RMS_EPS = 1e-6
ATT_SCALE = 1.0 / math.sqrt(HEAD_DIM)
LANES = 128
HALO = 8
TM = 512
TQ = 512
VMEM_LIMIT = 48 * 1024 * 1024
MESH = pl.DeviceIdType.MESH

ADAM_LR, ADAM_B1, ADAM_B2, ADAM_EPS, ADAM_WD, ADAM_STEP = 0.001, 0.9, 0.999, 1e-08, 0.01, 10

NT = (((1,), (1,)), ((), ()))
TN = (((0,), (0,)), ((), ()))


def _params(n_axes, vmem_limit=VMEM_LIMIT):
    return pltpu.CompilerParams(dimension_semantics=("arbitrary",) * n_axes, vmem_limit_bytes=vmem_limit)


def _tile(n, cap, mult):
    best = None
    for t in range(mult, min(n, cap) + 1, mult):
        if n % t == 0:
            best = t
    assert best is not None, (n, cap, mult)
    return best


def _rms_rstd(x):
    return lax.rsqrt(jnp.mean(x * x, axis=-1, keepdims=True) + RMS_EPS)


def _rms_fwd(x, g):
    return x * _rms_rstd(x) * g


def _rms_bwd(x, g, dy):
    xh = x * _rms_rstd(x)
    dyg = dy * g
    dx = _rms_rstd(x) * (dyg - xh * jnp.mean(dyg * xh, axis=-1, keepdims=True))
    return dx, jnp.sum(dy * xh, axis=0, keepdims=True)


def _accumulate(ref, first, value):
    @pl.when(first)
    def _():
        ref[...] = value

    @pl.when(jnp.logical_not(first))
    def _():
        ref[...] += value


def rms_proj(x, g, ws, out_dtypes, name, transposed=False):
    T = x.shape[0]
    tm = _tile(T, TM, 16)
    na = len(ws)

    def body(x_ref, g_ref, *refs):
        w_refs, xn_ref, o_refs = refs[:na], refs[na], refs[na + 1:]
        xn = _rms_fwd(x_ref[...], g_ref[...]).astype(BF16)
        xn_ref[...] = xn
        for a, (w, l) in enumerate(ws):
            for p in range(w.shape[0]):
                if transposed:
                    y = lax.dot_general(xn, w_refs[a][p, l], NT, preferred_element_type=F32)
                else:
                    y = jnp.dot(xn, w_refs[a][p, l], preferred_element_type=F32)
                o_refs[a][p] = y.astype(o_refs[a].dtype)

    in_specs = [pl.BlockSpec((tm, D), lambda i: (i, 0)), pl.BlockSpec((1, D), lambda i: (0, 0))]
    in_specs += [pl.BlockSpec(w.shape, lambda i: (0, 0, 0, 0), pipeline_mode=pl.Buffered(1)) for w, _ in ws]
    out_specs = [pl.BlockSpec((tm, D), lambda i: (i, 0))]
    out_shape = [jax.ShapeDtypeStruct((T, D), BF16)]
    for (w, _), dt in zip(ws, out_dtypes):
        nb, wb = w.shape[0], w.shape[2 if transposed else 3]
        out_specs.append(pl.BlockSpec((nb, tm, wb), lambda i: (0, i, 0)))
        out_shape.append(jax.ShapeDtypeStruct((nb, T, wb), dt))
    res = pl.pallas_call(
        body, name=name, grid=(T // tm,), in_specs=in_specs, out_specs=out_specs, out_shape=out_shape,
        compiler_params=_params(1),
    )(x, g, *[w for w, _ in ws])
    return res[0], res[1:]


def mix_out(pro, pro_inputs, pro_specs, w, res, g_post, alpha, nk, name, tm=None):
    w4, l = w
    T = res.shape[0]
    tm = _tile(T, TM, 16) if tm is None else tm
    dpb = N_DEV // nk
    rows = w4.shape[2]
    kb = dpb * rows
    npi = len(pro_inputs)

    def body(*refs):
        pro_refs = refs[:npi]
        w_ref, res_ref, g_ref, z_ref, m_ref, y_ref = refs[npi:]
        i = pl.program_id(0)
        m = None
        for k in range(nk):
            z = pro(i, k, *pro_refs).astype(BF16)
            z_ref[k] = z
            part = jnp.dot(z, w_ref[k * dpb:(k + 1) * dpb, l].reshape(kb, D), preferred_element_type=F32)
            m = part if m is None else m + part
        m_ref[...] = m
        y_ref[...] = res_ref[...] + alpha * _rms_fwd(m, g_ref[...])

    row = pl.BlockSpec((tm, D), lambda i: (i, 0))
    in_specs = list(pro_specs) + [
        pl.BlockSpec(w4.shape, lambda i: (0, 0, 0, 0), pipeline_mode=pl.Buffered(1)), row, pl.BlockSpec((1, D), lambda i: (0, 0))]
    z, m, y = pl.pallas_call(
        body, name=name, grid=(T // tm,), in_specs=in_specs,
        out_specs=[pl.BlockSpec((nk, tm, kb), lambda i: (0, i, 0)), row, row],
        out_shape=[jax.ShapeDtypeStruct((nk, T, kb), BF16), jax.ShapeDtypeStruct((T, D), F32),
                   jax.ShapeDtypeStruct((T, D), F32)],
        compiler_params=_params(1),
    )(*pro_inputs, w4, res, g_post)
    return z, m, y


def post_bwd(dy, m, g_post, alpha, w, nk, name, gu=None, tm=None):
    w4, l = w
    T = dy.shape[0]
    tm = _tile(T, TM, 16) if tm is None else tm
    dpb = N_DEV // nk
    rows = w4.shape[2]
    kb = dpb * rows
    swiglu = gu is not None

    def body(dy_ref, m_ref, g_ref, w_ref, *refs):
        if swiglu:
            gu_ref, dm_ref, dg_ref, dgu_ref = refs
        else:
            dm_ref, dg_ref, dz_ref = refs
        dm, dg = _rms_bwd(m_ref[...], g_ref[...], alpha * dy_ref[...])
        dm = dm.astype(BF16)
        dm_ref[...] = dm
        _accumulate(dg_ref, pl.program_id(0) == 0, dg)
        for k in range(nk):
            dz = lax.dot_general(dm, w_ref[k * dpb:(k + 1) * dpb, l].reshape(kb, D), NT, preferred_element_type=F32)
            if swiglu:
                gate, up = gu_ref[k].astype(F32), gu_ref[k + nk].astype(F32)
                sig = jax.nn.sigmoid(gate)
                dgu_ref[k] = (dz * up * sig * (1.0 + gate * (1.0 - sig))).astype(BF16)
                dgu_ref[k + nk] = (dz * gate * sig).astype(BF16)
            else:
                dz_ref[k] = dz.astype(BF16)

    row = pl.BlockSpec((tm, D), lambda i: (i, 0))
    vec = pl.BlockSpec((1, D), lambda i: (0, 0))
    blkk = pl.BlockSpec((nk, tm, kb), lambda i: (0, i, 0))
    in_specs = [row, row, vec, pl.BlockSpec(w4.shape, lambda i: (0, 0, 0, 0), pipeline_mode=pl.Buffered(1))]
    inputs = [dy, m, g_post, w4]
    n_dz = 2 * nk if swiglu else nk
    out_specs = [row, vec, pl.BlockSpec((n_dz, tm, kb), lambda i: (0, i, 0))]
    out_shape = [jax.ShapeDtypeStruct((T, D), BF16), jax.ShapeDtypeStruct((1, D), F32),
                 jax.ShapeDtypeStruct((n_dz, T, kb), BF16)]
    if swiglu:
        in_specs.append(pl.BlockSpec((2 * nk, tm, kb), lambda i: (0, i, 0)))
        inputs.append(gu)
    return pl.pallas_call(
        body, name=name, grid=(T // tm,), in_specs=in_specs, out_specs=out_specs, out_shape=out_shape,
        compiler_params=_params(1),
    )(*inputs)


def pre_bwd(pieces, x, g_pre, dres, name, transposed=False):
    T = x.shape[0]
    tm = _tile(T, TM, 16)
    na = len(pieces)
    weights = []
    for _, (w4, _), _ in pieces:
        if not any(w4 is w for w in weights):
            weights.append(w4)
    nw = len(weights)
    which = [[w4 is w for w in weights].index(True) for _, (w4, _), _ in pieces]

    def body(*refs):
        dz_refs, w_refs = refs[:na], refs[na:na + nw]
        x_ref, g_ref, dres_ref, dx_ref, dg_ref = refs[na + nw:]
        acc = None
        for a, (dz, (_, l), w_off) in enumerate(pieces):
            for p in range(dz.shape[0]):
                w = w_refs[which[a]][w_off + p, l]
                if transposed:
                    part = jnp.dot(dz_refs[a][p].astype(BF16), w, preferred_element_type=F32)
                else:
                    part = lax.dot_general(dz_refs[a][p].astype(BF16), w, NT, preferred_element_type=F32)
                acc = part if acc is None else acc + part
        dx, dg = _rms_bwd(x_ref[...], g_ref[...], acc)
        dx_ref[...] = dres_ref[...] + dx
        _accumulate(dg_ref, pl.program_id(0) == 0, dg)

    row = pl.BlockSpec((tm, D), lambda i: (i, 0))
    vec = pl.BlockSpec((1, D), lambda i: (0, 0))
    dz_specs = [pl.BlockSpec((dz.shape[0], tm, dz.shape[2]), lambda i: (0, i, 0)) for dz, _, _ in pieces]
    w_specs = [pl.BlockSpec(w.shape, lambda i: (0, 0, 0, 0), pipeline_mode=pl.Buffered(1)) for w in weights]
    return pl.pallas_call(
        body, name=name, grid=(T // tm,), in_specs=dz_specs + w_specs + [row, vec, row],
        out_specs=[row, vec], out_shape=[jax.ShapeDtypeStruct((T, D), F32), jax.ShapeDtypeStruct((1, D), F32)],
        compiler_params=_params(1),
    )(*[p[0] for p in pieces], *weights, x, g_pre, dres)


def wgrad(a_list, b_list, nout, out4_shape, name, a_cols=None, tm_cap=None):
    T = a_list[0].shape[1]
    tm = _tile(T, 4 * TM if tm_cap is None else tm_cap, 16)
    nt = T // tm
    dpb = out4_shape[0] // nout
    _, _, R, C = out4_shape
    na, nb = len(a_list), len(b_list)
    a_w = a_list[0].shape[2] if a_cols is None else a_cols
    a_per = a_list[0].shape[2] // a_w

    def spans(arrs, per):
        ns = [a.shape[0] * per for a in arrs]
        return ns, [sum(ns[:k]) for k in range(len(ns))], sum(ns)

    a_ns, a_offs, a_tot = spans(a_list, a_per)
    b_ns, b_offs, b_tot = spans(b_list, 1)
    assert a_tot in (1, nout) and b_tot in (1, nout)

    def body(*refs):
        a_refs, b_refs = refs[:na], refs[na:na + nb]
        out_ref, acc = refs[-2:]
        p, t = pl.program_id(0), pl.program_id(1)
        for ia in range(na):
            for ib in range(nb):
                conds = []
                if a_tot > 1:
                    conds += [p >= a_offs[ia], p < a_offs[ia] + a_ns[ia]]
                if b_tot > 1:
                    conds += [p >= b_offs[ib], p < b_offs[ib] + b_ns[ib]]

                def work(ia=ia, ib=ib):
                    part = lax.dot_general(a_refs[ia][0], b_refs[ib][0], TN, preferred_element_type=F32)
                    _accumulate(acc, t == 0, part)

                if conds:
                    pl.when(functools.reduce(jnp.logical_and, conds))(work)
                else:
                    work()

        @pl.when(t == nt - 1)
        def _():
            out_ref[...] = acc[...].astype(BF16).reshape(dpb, 1, R, C)

    def blk(off, n, tot):
        if tot == 1:
            return lambda p: 0
        return lambda p: jnp.clip(p - off, 0, n - 1)

    in_specs = []
    for arr, n, off in zip(a_list, a_ns, a_offs):
        in_specs.append(pl.BlockSpec((1, tm, a_w), lambda p, t, f=blk(off, n, a_tot): (f(p) // a_per, t, f(p) % a_per)))
    for arr, n, off in zip(b_list, b_ns, b_offs):
        in_specs.append(pl.BlockSpec((1, tm, arr.shape[2]), lambda p, t, f=blk(off, n, b_tot): (f(p), t, 0)))
    return pl.pallas_call(
        body, name=name, grid=(nout, nt), in_specs=in_specs,
        out_specs=pl.BlockSpec((dpb, 1, R, C), lambda p, t: (p, 0, 0, 0)),
        out_shape=jax.ShapeDtypeStruct(out4_shape, BF16),
        scratch_shapes=[pltpu.VMEM((dpb * R, C), F32)], compiler_params=_params(2),
    )(*a_list, *b_list)


def _swiglu_pro(i, k, gu_ref):
    gate, up = gu_ref[k].astype(F32), gu_ref[k + gu_ref.shape[0] // 2].astype(F32)
    return gate * jax.nn.sigmoid(gate) * up


def _attn_gate_pro(i, k, gate_ref, o_ref):
    return jax.nn.sigmoid(gate_ref[0].astype(F32)) * o_ref[...].astype(F32)


def _shift_rows(u, halo, d):
    rolled = pltpu.roll(u, d, 0)
    row = lax.broadcasted_iota(jnp.int32, u.shape, 0)
    for r in range(d):
        rolled = jnp.where(row == r, halo[HALO - d + r:HALO - d + r + 1, :], rolled)
    return rolled


def _advance_rows(u, halo, d):
    n = u.shape[0]
    rolled = pltpu.roll(u, n - d, 0)
    row = lax.broadcasted_iota(jnp.int32, u.shape, 0)
    for r in range(d):
        rolled = jnp.where(row == n - d + r, halo[r:r + 1, :], rolled)
    return rolled


def _make_conv_pro(tiles_per_seq):
    def pro(i, k, b_ref, c_ref, h_ref, ch_ref, hh_ref, ck_ref):
        u = c_ref[0].astype(F32) * h_ref[0].astype(F32)
        first = (i % tiles_per_seq) == 0
        halo = jnp.where(first, 0.0, ch_ref[0].astype(F32) * hh_ref[0].astype(F32))
        ck = ck_ref[...]
        conv = ck[2:3, :] * u + ck[1:2, :] * _shift_rows(u, halo, 1) + ck[0:1, :] * _shift_rows(u, halo, 2)
        return b_ref[0].astype(F32) * conv
    return pro


def conv_bwd_mix(dz, bch, conv_k, seq_len, name):
    T = dz.shape[1]
    tm = _tile(seq_len, TM, 16)
    tps = seq_len // tm
    nt = T // tm
    hb = tm // HALO

    def body(dz_ref, b_ref, c_ref, h_ref, cp_ref, hp_ref, dzn_ref, bn_ref, ck_ref, dbch_ref, dk_ref):
        i = pl.program_id(0)
        first = (i % tps) == 0
        last = (i % tps) == tps - 1
        b, c, h = b_ref[0].astype(F32), c_ref[0].astype(F32), h_ref[0].astype(F32)
        dzt = dz_ref[0].astype(F32)
        u = c * h
        prev = jnp.where(first, 0.0, cp_ref[0].astype(F32) * hp_ref[0].astype(F32))
        u1, u2 = _shift_rows(u, prev, 1), _shift_rows(u, prev, 2)
        ck = ck_ref[...]
        conv = ck[2:3, :] * u + ck[1:2, :] * u1 + ck[0:1, :] * u2
        dconv = dzt * b
        nxt = jnp.where(last, 0.0, dzn_ref[0].astype(F32) * bn_ref[0].astype(F32))
        du = ck[2:3, :] * dconv + ck[1:2, :] * _advance_rows(dconv, nxt, 1) + ck[0:1, :] * _advance_rows(dconv, nxt, 2)
        dbch_ref[0] = (dzt * conv).astype(BF16)
        dbch_ref[1] = (du * h).astype(BF16)
        dbch_ref[2] = (du * c).astype(BF16)
        tap = lax.broadcasted_iota(jnp.int32, (3, D), 0)
        dk = jnp.where(tap == 0, jnp.sum(dconv * u2, axis=0, keepdims=True),
                       jnp.where(tap == 1, jnp.sum(dconv * u1, axis=0, keepdims=True),
                                 jnp.sum(dconv * u, axis=0, keepdims=True)))
        _accumulate(dk_ref, i == 0, dk)

    def piece(p):
        return pl.BlockSpec((1, tm, D), lambda i, p=p: (p, i, 0))

    def prev(p):
        return pl.BlockSpec((1, HALO, D), lambda i, p=p: (p, jnp.maximum(i * hb - 1, 0), 0))

    def nxt(p):
        return pl.BlockSpec((1, HALO, D), lambda i, p=p: (p, jnp.minimum((i + 1) * hb, nt * hb - 1), 0))

    return pl.pallas_call(
        body, name=name, grid=(nt,),
        in_specs=[piece(0), piece(0), piece(1), piece(2), prev(1), prev(2), nxt(0), nxt(0),
                  pl.BlockSpec((3, D), lambda i: (0, 0))],
        out_specs=[pl.BlockSpec((3, tm, D), lambda i: (0, i, 0)), pl.BlockSpec((3, D), lambda i: (0, 0))],
        out_shape=[jax.ShapeDtypeStruct((3, T, D), BF16), jax.ShapeDtypeStruct((3, D), F32)],
        compiler_params=_params(1),
    )(dz, bch, bch, bch, bch, bch, dz, bch, conv_k)


def _log_sigmoid(x):
    return jnp.minimum(x, 0.0) - jnp.log(1.0 + jnp.exp(-jnp.abs(x)))


def forget_fwd(fl, fb, seq_len, name):
    T = fl.shape[1]

    def body(fl_ref, fb_ref, c_ref):
        c = _log_sigmoid(fl_ref[0] + fb_ref[...])
        row = lax.broadcasted_iota(jnp.int32, c.shape, 0)
        k = 1
        while k < seq_len:
            c = c + jnp.where(row >= k, pltpu.roll(c, k, 0), 0.0)
            k *= 2
        c_ref[...] = c

    return pl.pallas_call(
        body, name=name, grid=(T // seq_len,),
        in_specs=[pl.BlockSpec((1, seq_len, LANES), lambda b: (0, b, 0)), pl.BlockSpec((1, LANES), lambda b: (0, 0))],
        out_specs=pl.BlockSpec((seq_len, LANES), lambda b: (b, 0)),
        out_shape=jax.ShapeDtypeStruct((T, LANES), F32), compiler_params=_params(1),
    )(fl, fb)


def forget_bwd(dc, fl, fb, seq_len, name):
    T = dc.shape[0]

    def body(dc_ref, fl_ref, fb_ref, dfl_ref, dfb_ref):
        b = pl.program_id(0)
        r = dc_ref[...]
        row = lax.broadcasted_iota(jnp.int32, r.shape, 0)
        k = 1
        while k < seq_len:
            r = r + jnp.where(row < seq_len - k, pltpu.roll(r, seq_len - k, 0), 0.0)
            k *= 2
        dfl = r * jax.nn.sigmoid(-(fl_ref[0] + fb_ref[...]))
        dfl_ref[0] = dfl
        _accumulate(dfb_ref, b == 0, jnp.sum(dfl, axis=0, keepdims=True))

    return pl.pallas_call(
        body, name=name, grid=(T // seq_len,),
        in_specs=[pl.BlockSpec((seq_len, LANES), lambda b: (b, 0)), pl.BlockSpec((1, seq_len, LANES), lambda b: (0, b, 0)),
                  pl.BlockSpec((1, LANES), lambda b: (0, 0))],
        out_specs=[pl.BlockSpec((1, seq_len, LANES), lambda b: (0, b, 0)), pl.BlockSpec((1, LANES), lambda b: (0, 0))],
        out_shape=[jax.ShapeDtypeStruct((1, T, LANES), F32), jax.ShapeDtypeStruct((1, LANES), F32)],
        compiler_params=_params(1),
    )(dc, fl, fb)


HEADS = (slice(0, HEAD_DIM), slice(HEAD_DIM, 2 * HEAD_DIM))


def _key_blocks(i, step, carry):
    carry = lax.fori_loop(0, i // 2, lambda jj, c: step(2 * jj, c, False, 2), carry)
    return lax.cond(i % 2 == 1, lambda c: step(i - 1, c, True, 2), lambda c: step(i, c, True, 1), carry)


def _causal(width, tq):
    keys = lax.broadcasted_iota(jnp.int32, (width * tq, tq), 0)
    return keys <= lax.broadcasted_iota(jnp.int32, (width * tq, tq), 1) + (width - 1) * tq


def attn_fwd(qg, kv, cb, n_seq, seq_len, name):
    T = n_seq * seq_len
    tq = _tile(seq_len, TQ, LANES)
    nq = seq_len // tq

    def body(q_ref, k_ref, v_ref, cb_ref, o_ref, lse_ref):
        i = pl.program_id(2)
        q8 = [q_ref[0, :, sl] * ATT_SCALE for sl in HEADS]

        def block(j, carry, diagonal, width):
            rows = pl.ds(pl.multiple_of(j * tq, tq), width * tq)
            out = []
            for hh, sl in enumerate(HEADS):
                m, l, acc = carry[3 * hh:3 * hh + 3]
                s = lax.dot_general(k_ref[0, rows, sl], q8[hh], NT, preferred_element_type=F32)
                s = s - cb_ref[rows, sl.start:sl.start + 1]
                if diagonal:
                    s = jnp.where(_causal(width, tq), s, -1e30)
                m_new = jnp.maximum(m, jnp.max(s, axis=0, keepdims=True))
                a = jnp.exp(m - m_new)
                p = jnp.exp(s - m_new)
                l = a * l + jnp.sum(p, axis=0, keepdims=True)
                acc = a * acc + lax.dot_general(v_ref[0, rows, sl], p.astype(BF16), TN, preferred_element_type=F32)
                out += [m_new, l, acc]
            return tuple(out)

        init = (jnp.full((1, tq), -1e30, F32), jnp.zeros((1, tq), F32), jnp.zeros((HEAD_DIM, tq), F32)) * 2
        carry = _key_blocks(i, block, init)
        o_ref[...] = jnp.concatenate([carry[2] / carry[1], carry[5] / carry[4]], axis=0).T
        for hh in range(2):
            lse_ref[0, 0, 0, hh:hh + 1, :] = carry[3 * hh] + jnp.log(carry[3 * hh + 1])

    seq2 = pl.BlockSpec((seq_len, LANES), lambda b, hp, i: (b, hp))
    return pl.pallas_call(
        body, name=name, grid=(n_seq, N_HEADS // 2, nq),
        in_specs=[pl.BlockSpec((1, tq, LANES), lambda b, hp, i: (0, b * nq + i, hp)),
                  pl.BlockSpec((1, seq_len, LANES), lambda b, hp, i: (0, b, hp)),
                  pl.BlockSpec((1, seq_len, LANES), lambda b, hp, i: (1, b, hp)), seq2],
        out_specs=[pl.BlockSpec((tq, LANES), lambda b, hp, i: (b * nq + i, hp)),
                   pl.BlockSpec((1, 1, 1, 2, tq), lambda b, hp, i: (b, hp, i, 0, 0))],
        out_shape=[jax.ShapeDtypeStruct((T, D), F32), jax.ShapeDtypeStruct((n_seq, N_HEADS // 2, nq, 2, tq), F32)],
        compiler_params=_params(3),
    )(qg, kv, kv, cb)


def attn_bwd(dz, qg, kv, o, lse, cb, n_seq, seq_len, name):
    T = n_seq * seq_len
    tq = _tile(seq_len, TQ, LANES)
    nq = seq_len // tq

    def body(dz_ref, q_ref, gate_ref, o_ref, lse_ref, cb_ref, k_ref, v_ref,
             dq_ref, dgate_ref, dk_ref, dv_ref, dc_ref, p_s, dp_s, dk_s, dv_s, dc_s):
        i = pl.program_id(2)

        @pl.when(i == 0)
        def _():
            dk_s[...] = jnp.zeros_like(dk_s)
            dv_s[...] = jnp.zeros_like(dv_s)
            dc_s[...] = jnp.zeros_like(dc_s)

        dzf = dz_ref[...].astype(F32)
        sig = jax.nn.sigmoid(gate_ref[0].astype(F32))
        dob = (dzf * sig).astype(BF16)
        dgate_ref[0] = (dzf * o_ref[...] * sig * (1.0 - sig)).astype(BF16)
        q8 = [q_ref[0, :, sl] * ATT_SCALE for sl in HEADS]
        do = [dob[:, sl] for sl in HEADS]
        lse_i = [lse_ref[0, 0, 0, hh:hh + 1, :] for hh in range(2)]

        def probs(j, dsums, diagonal, width):
            rows = pl.ds(pl.multiple_of(j * tq, tq), width * tq)
            out = []
            for hh, sl in enumerate(HEADS):
                s = lax.dot_general(k_ref[0, rows, sl], q8[hh], NT, preferred_element_type=F32)
                p = jnp.exp(s - cb_ref[rows, sl.start:sl.start + 1] - lse_i[hh])
                if diagonal:
                    p = jnp.where(_causal(width, tq), p, 0.0)
                dp = lax.dot_general(v_ref[0, rows, sl], do[hh], NT, preferred_element_type=F32)
                p_s[hh, rows, :] = p
                dp_s[hh, rows, :] = dp
                out.append(dsums[hh] + jnp.sum(p * dp, axis=0, keepdims=True))
            return tuple(out)

        dsums = _key_blocks(i, probs, (jnp.zeros((1, tq), F32),) * 2)

        def grads(j, dqs, diagonal, width):
            rows = pl.ds(pl.multiple_of(j * tq, tq), width * tq)
            out = []
            for hh, sl in enumerate(HEADS):
                p = p_s[hh, rows, :]
                ds = p * (dp_s[hh, rows, :] - dsums[hh])
                dc_s[hh, rows, :] -= jnp.sum(ds, axis=1, keepdims=True)
                dsb = ds.astype(BF16)
                dk_s[rows, sl] += jnp.dot(dsb, q8[hh], preferred_element_type=F32)
                dv_s[rows, sl] += jnp.dot(p.astype(BF16), do[hh], preferred_element_type=F32)
                out.append(dqs[hh] + lax.dot_general(k_ref[0, rows, sl], dsb, TN, preferred_element_type=F32))
            return tuple(out)

        dqs = _key_blocks(i, grads, (jnp.zeros((HEAD_DIM, tq), F32),) * 2)
        dq_ref[0] = (jnp.concatenate(dqs, axis=0).T * ATT_SCALE).astype(BF16)

        @pl.when(i == nq - 1)
        def _():
            dk_ref[0] = dk_s[...].astype(BF16)
            dv_ref[0] = dv_s[...].astype(BF16)
            dc_ref[...] = jnp.zeros_like(dc_ref)
            for hh, sl in enumerate(HEADS):
                dc_ref[:, sl.start:sl.start + 1] = dc_s[hh]

    qry2 = pl.BlockSpec((tq, LANES), lambda b, hp, i: (b * nq + i, hp))
    seq2 = pl.BlockSpec((seq_len, LANES), lambda b, hp, i: (b, hp))

    def qry3(p):
        return pl.BlockSpec((1, tq, LANES), lambda b, hp, i, p=p: (p, b * nq + i, hp))

    def seq3(p):
        return pl.BlockSpec((1, seq_len, LANES), lambda b, hp, i, p=p: (p, b, hp))

    act = jax.ShapeDtypeStruct((1, T, D), BF16)
    return pl.pallas_call(
        body, name=name, grid=(n_seq, N_HEADS // 2, nq),
        in_specs=[qry2, qry3(0), qry3(1), qry2, pl.BlockSpec((1, 1, 1, 2, tq), lambda b, hp, i: (b, hp, i, 0, 0)), seq2,
                  seq3(0), seq3(1)],
        out_specs=[qry3(0), qry3(0), seq3(0), seq3(0), seq2],
        out_shape=[act, act, act, act, jax.ShapeDtypeStruct((T, D), F32)],
        scratch_shapes=[pltpu.VMEM((2, seq_len, tq), F32), pltpu.VMEM((2, seq_len, tq), F32),
                        pltpu.VMEM((seq_len, LANES), F32), pltpu.VMEM((seq_len, LANES), F32),
                        pltpu.VMEM((2, seq_len, 1), F32)],
        compiler_params=_params(3),
    )(dz, qg, qg, o, lse, cb, kv, kv)


def loss_head(y, target, name):
    T = y.shape[0]
    tm = _tile(T, TM, 8)

    def body(y_ref, t_ref, dy_ref, loss_ref):
        err = y_ref[...] - t_ref[...]
        dy_ref[...] = err * (1.0 / D)
        part = 0.5 * jnp.sum(jnp.mean(err * err, axis=-1, keepdims=True), axis=0, keepdims=True)
        _accumulate(loss_ref, pl.program_id(0) == 0, jnp.broadcast_to(part, (1, LANES)))

    row = pl.BlockSpec((tm, D), lambda i: (i, 0))
    return pl.pallas_call(
        body, name=name, grid=(T // tm,), in_specs=[row, row],
        out_specs=[row, pl.BlockSpec((1, LANES), lambda i: (0, 0))],
        out_shape=[jax.ShapeDtypeStruct((T, D), F32), jax.ShapeDtypeStruct((1, LANES), F32)],
        compiler_params=_params(1),
    )(y, target)


def _adamw(w, g, m, v):
    m = ADAM_B1 * m + (1.0 - ADAM_B1) * g
    v = ADAM_B2 * v + (1.0 - ADAM_B2) * (g * g)
    m_hat = m / (1.0 - ADAM_B1 ** ADAM_STEP)
    v_hat = v / (1.0 - ADAM_B2 ** ADAM_STEP)
    delta = -ADAM_LR * (m_hat / (jnp.sqrt(v_hat) + ADAM_EPS) + ADAM_WD * w)
    return delta, m, v


def adam_sharded(w, m, v, contribs, me, name):
    L, R, C = w.shape
    tr = _tile(R, 256, 16) if R % 16 == 0 else R

    def body(me_ref, w_ref, m_ref, v_ref, *refs):
        c_refs, (g_ref, d_ref, nm_ref, nv_ref) = refs[:2 * L], refs[2 * L:]
        l = pl.program_id(0)
        for j in range(L):
            @pl.when(l == j)
            def _(j=j):
                own_ref, recv_ref = c_refs[2 * j], c_refs[2 * j + 1]
                g = own_ref[0].astype(F32)
                for k in range(N_DEV - 1):
                    g = g + recv_ref[k].astype(F32)
                delta, nm, nv = _adamw(w_ref[0], g, m_ref[0], v_ref[0])
                g_ref[0] = g
                d_ref[0] = delta
                nm_ref[0] = nm
                nv_ref[0] = nv

    blk = pl.BlockSpec((1, tr, C), lambda l, i, s: (l, i, 0))
    in_specs = [blk, blk, blk]
    inputs = [w, m, v]
    for j, (mine, recv) in enumerate(contribs):
        in_specs.append(pl.BlockSpec((1, tr, C), lambda l, i, s, j=j: (s[0], jnp.where(l == j, i, 0), 0)))
        in_specs.append(pl.BlockSpec((N_DEV - 1, tr, C), lambda l, i, s, j=j: (0, jnp.where(l == j, i, 0), 0)))
        inputs += [mine, recv]
    shp = jax.ShapeDtypeStruct((L, R, C), F32)
    grid_spec = pltpu.PrefetchScalarGridSpec(num_scalar_prefetch=1, grid=(L, R // tr), in_specs=in_specs, out_specs=[blk] * 4)
    return pl.pallas_call(body, name=name, grid_spec=grid_spec, out_shape=[shp] * 4, compiler_params=_params(2))(me, *inputs)


def cast_place(w, l, me, dtype, after, name):
    _, R, C = w.shape
    tr = _tile(R, 512, 16) if R % 16 == 0 else R

    def body(me_ref, w_ref, after_ref, o_ref):
        o_ref[0] = w_ref[0].astype(dtype)

    grid_spec = pltpu.PrefetchScalarGridSpec(
        num_scalar_prefetch=1, grid=(R // tr,), in_specs=[pl.BlockSpec((1, tr, C), lambda i, s: (l, i, 0)), ANY],
        out_specs=pl.BlockSpec((1, tr, C), lambda i, s: (s[0], i, 0)))
    return pl.pallas_call(body, name=name, grid_spec=grid_spec, out_shape=jax.ShapeDtypeStruct((N_DEV, R, C), dtype),
                          compiler_params=_params(1))(me, w, after)


def adam_small(params, total, extra_grads, name):
    n, ne = len(params), len(extra_grads)

    def body(*refs):
        total_ref, extra_refs = refs[0], refs[1:1 + ne]
        ins, outs = refs[1 + ne:1 + ne + 3 * n], refs[1 + ne + 3 * n:]
        for k, (_, _, _, where) in enumerate(params):
            if isinstance(where, int):
                g = extra_refs[where][...]
            else:
                row, rows, width = where
                g = total_ref[row:row + rows, 0:width]
            delta, nm, nv = _adamw(ins[3 * k][...], g, ins[3 * k + 1][...], ins[3 * k + 2][...])
            outs[4 * k][...] = g
            outs[4 * k + 1][...] = delta
            outs[4 * k + 2][...] = nm
            outs[4 * k + 3][...] = nv

    flat = [a for w, m, v, _ in params for a in (w, m, v)]
    out_shape = [jax.ShapeDtypeStruct(w.shape, F32) for w, _, _, _ in params for _ in range(4)]
    res = pl.pallas_call(body, name=name, out_shape=out_shape)(total, *extra_grads, *flat)
    return [res[4 * k:4 * k + 4] for k in range(n)]


def _place():
    return lax.axis_index("x"), lax.axis_index("y"), lax.axis_index("c")


def _peer(place, k):
    x, y, c = place
    return x ^ (k >> 2), y ^ ((k >> 1) & 1), c ^ (k & 1)


ANY = pl.BlockSpec(memory_space=pl.ANY)
HBM = pl.BlockSpec(memory_space=pltpu.HBM)
SEM = pl.BlockSpec(memory_space=pltpu.SEMAPHORE)
EFFECT = pltpu.SideEffectType.DATAFLOW_SIDE_EFFECTING


def _in_hbm(a):
    return pltpu.with_memory_space_constraint(a, pltpu.HBM)


def _number(place):
    return 4 * place[0] + 2 * place[1] + place[2]


SLOTS = {"gather_like": 4, "gather_pass": 3, "scatter": 7}


def _plan(mode, src_ref, land_ref, place):
    if mode == "gather_like":
        mine = land_ref.at[_number(place)]
        return [(mine, mine, _peer(place, k)) for k in (1, 2, 4, 6)]
    if mode == "gather_pass":
        slots = [land_ref.at[_number(_peer(place, k))] for k in (2, 4, 6)]
        return [(slot, slot, _peer(place, 1)) for slot in slots]
    return [(src_ref.at[_number(_peer(place, k))], land_ref.at[k - 1], _peer(place, k)) for k in range(1, N_DEV)]


def _exchange(name, groups, start, afters):
    sizes = [len(g[3]) for g in groups]
    na, ng = sum(sizes), len(groups)
    ns = sum(len(g[2]) for g in groups)
    waits = groups[0][0] is not None
    n_in_sems = 2 * ng if waits else 0
    n_out_sems = 2 * ng if start else 0

    def body(*refs):
        src_refs, land_refs = (refs[:ns] if ns else [None] * na), refs[ns:ns + na]
        in_sems = refs[ns + na:ns + na + n_in_sems]
        outs = refs[ns + na + n_in_sems + len(afters):]
        place = _place()
        a = 0
        for gi, n in enumerate(sizes):
            for idx in range(n):
                if waits:
                    zone = land_refs[a].at[pl.ds(0, groups[gi][4])]
                    copy = pltpu.make_async_remote_copy(
                        src_ref=zone, dst_ref=zone, send_sem=in_sems[2 * gi].at[idx], recv_sem=in_sems[2 * gi + 1].at[idx],
                        device_id=_peer(place, 1), device_id_type=MESH)
                    copy.wait_send()
                    copy.wait_recv()
                if start:
                    for src, dst, peer in _plan(start, src_refs[a], land_refs[a], place):
                        pltpu.make_async_remote_copy(
                            src_ref=src, dst_ref=dst, send_sem=outs[2 * gi].at[idx], recv_sem=outs[2 * gi + 1].at[idx],
                            device_id=peer, device_id_type=MESH).start()
                a += 1
        if start:
            outs[-1][...] = jnp.zeros_like(outs[-1])

    srcs = [_in_hbm(s) for g in groups for s in g[2]]
    lands = [_in_hbm(l) for g in groups for l in g[3]]
    sems = [s for g in groups for s in g[:2]] if waits else []
    out_shape = [pltpu.SemaphoreType.DMA((n,)) for n in sizes for _ in range(2)] if start else []
    out_shape += [pltpu.HBM(a.shape, a.dtype) for a in srcs + lands]
    out_specs = [SEM] * n_out_sems + [HBM] * (ns + na)
    if start:
        out_shape.append(jax.ShapeDtypeStruct((8, LANES), F32))
        out_specs.append(pl.BlockSpec(memory_space=pltpu.VMEM))
    res = pl.pallas_call(
        body, name=name, in_specs=[HBM] * (ns + na) + [SEM] * n_in_sems + [ANY] * len(afters),
        out_shape=out_shape, out_specs=out_specs,
        input_output_aliases={i: n_out_sems + i for i in range(ns + na)},
        compiler_params=pltpu.CompilerParams(has_side_effects=EFFECT),
    )(*srcs, *lands, *sems, *afters)
    new_sems, thru = res[:n_out_sems], res[n_out_sems:n_out_sems + ns + na]
    out, a = [], 0
    for gi, n in enumerate(sizes):
        pair = (new_sems[2 * gi], new_sems[2 * gi + 1]) if start else (None, None)
        out.append(pair + (thru[a:a + n] if ns else [], thru[ns + a:ns + a + n], SLOTS.get(start, 0)))
        a += n
    return out, (res[-1] if start else None)


def exchange_start(pair_groups, mode, name):
    groups = [(None, None, [s for s, _ in g if s is not None], [l for _, l in g], 0) for g in pair_groups]
    return _exchange(name, groups, mode, ())


def exchange_relay(groups, mode, afters, name):
    return _exchange(name, groups, mode, afters)


def exchange_wait(groups, afters, name):
    done, _ = _exchange(name, groups, None, afters)
    return [(g[2], g[3]) for g in done]


def all_reduce_small(parts, n_rows, after, name):
    R = n_rows
    n_parts = len(parts)

    def body(*refs):
        part_refs = refs[:n_parts]
        out_ref, buf, send_sems, recv_sems = refs[n_parts + 1:]
        x, y, c = _place()
        me = 4 * x + 2 * y + c
        own = buf.at[me]
        own[...] = jnp.zeros((R, D), F32)
        for ref, (arr, row) in zip(part_refs, parts):
            own[row:row + arr.shape[0], 0:arr.shape[1]] = ref[...]
        copies = []
        for k in range(1, N_DEV):
            peer = (x ^ (k >> 2), y ^ ((k >> 1) & 1), c ^ (k & 1))
            copies.append(pltpu.make_async_remote_copy(
                src_ref=own, dst_ref=own, send_sem=send_sems.at[k - 1], recv_sem=recv_sems.at[k - 1],
                device_id=peer, device_id_type=MESH))
        for cp in copies:
            cp.start()
        for cp in copies:
            cp.wait()
        total = buf[0]
        for d in range(1, N_DEV):
            total = total + buf[d]
        out_ref[...] = total

    vm = pl.BlockSpec(memory_space=pltpu.VMEM)
    return pl.pallas_call(
        body, name=name, in_specs=[vm] * n_parts + [ANY], out_specs=vm, out_shape=jax.ShapeDtypeStruct((R, D), F32),
        scratch_shapes=[pltpu.VMEM((N_DEV, R, D), F32), pltpu.SemaphoreType.DMA((N_DEV - 1,)),
                        pltpu.SemaphoreType.DMA((N_DEV - 1,))],
    )(*[arr for arr, _ in parts], after)


def _col_blocks(gathered, n_blocks):
    n, d, w = gathered.shape
    whole = gathered.transpose(1, 0, 2).reshape(d, n * w)
    return whole.reshape(d, n_blocks, n * w // n_blocks).transpose(1, 0, 2)[:, None]


def _col_shards(blocks):
    n, d, w = blocks.shape
    whole = blocks.transpose(1, 0, 2).reshape(d, n * w)
    return whole.reshape(d, N_DEV, n * w // N_DEV).transpose(1, 0, 2)


def kernel(x, ffn1_pre_g, ffn1_post_g, ffn1_w_in, ffn1_w_out, mix_pre_g, mix_post_g, ffn2_pre_g, ffn2_post_g, ffn2_w_in, ffn2_w_out, conv_w_in, conv_k, conv_w_out, kv_g, kv_w, forget_b, attn_w_qg, attn_w_o, loss_target, m_ffn1_pre_g, m_ffn1_post_g, m_ffn1_w_in, m_ffn1_w_out, m_mix_pre_g, m_mix_post_g, m_ffn2_pre_g, m_ffn2_post_g, m_ffn2_w_in, m_ffn2_w_out, m_conv_w_in, m_conv_k, m_conv_w_out, m_kv_g, m_kv_w, m_forget_b, m_attn_w_qg, m_attn_w_o, v_ffn1_pre_g, v_ffn1_post_g, v_ffn1_w_in, v_ffn1_w_out, v_mix_pre_g, v_mix_post_g, v_ffn2_pre_g, v_ffn2_post_g, v_ffn2_w_in, v_ffn2_w_out, v_conv_w_in, v_conv_k, v_conv_w_out, v_kv_g, v_kv_w, v_forget_b, v_attn_w_qg, v_attn_w_o):
    n_seq, seq_len, _ = x.shape
    T = n_seq * seq_len
    xi, yi, ci = _place()
    dev = 4 * xi + 2 * yi + ci
    x0 = x.reshape(T, D)
    target = loss_target.reshape(T, D)

    me = dev.reshape(1).astype(jnp.int32)
    w1_t, w2_t, kv_t = ffn1_w_in.transpose(0, 2, 1), ffn2_w_in.transpose(0, 2, 1), kv_w.T[None]

    def zones(specs, after):
        return [(None, cast_place(w, l, me, dt, after, f"place_{nm}")) for nm, w, l, dt in specs]

    gathers, token = exchange_start([zones([("w1_in0", w1_t, 0, BF16)], me)], "gather_like", "gather_start0")
    shard_groups = [
        [("w1_out0", ffn1_w_out, 0, BF16)],
        [("cw_in", conv_w_in, 0, BF16), ("cw_out", conv_w_out, 0, BF16), ("ck", conv_k, 0, F32)],
        [("w2_in0", w2_t, 0, BF16), ("w2_out0", ffn2_w_out, 0, BF16)],
        [("kv", kv_t, 0, BF16), ("w1_in1", w1_t, 1, BF16), ("w1_out1", ffn1_w_out, 1, BF16),
         ("qg", attn_w_qg, 0, BF16), ("ow", attn_w_o, 0, BF16)],
        [("w2_in1", w2_t, 1, BF16), ("w2_out1", ffn2_w_out, 1, BF16)]]
    later, token = exchange_start([zones(g, token) for g in shard_groups], "gather_like", "gather_start1")
    gathers = gathers + later

    def gathered(k, after):
        passed, _ = exchange_relay([gathers[k]], "gather_pass", [after], f"gather_pass{k}")
        return [z[:, None] for z in exchange_wait(passed, [after], f"gather_wait{k}")[0][1]]

    fb = jnp.pad(forget_b, (0, LANES - N_HEADS))[None]

    def vec(g, l):
        return g[l:l + 1]

    def behind(g, tok):
        return g + tok[:1, :1]

    grads_small = {}
    tm_ffn = _tile(T, TM // 2, 16)

    def ffn_fwd(xin, g_pre, g_post, w_in, w_out, tag):
        w_gu = w_in.reshape(2, 1, -1, D)
        xn, (gu,) = rms_proj(xin, g_pre, [(w_gu, 0)], [BF16], f"{tag}_in", transposed=True)
        specs = [pl.BlockSpec((2, tm_ffn, gu.shape[2]), lambda i: (0, i, 0))]
        if callable(w_out):
            w_out = w_out(xn)
        a, h, y = mix_out(_swiglu_pro, [gu], specs, (w_out, 0), xin, g_post, 0.5, 1, f"{tag}_out", tm=tm_ffn)
        return y, (xin, xn, gu, a, h), w_out

    def ffn_bwd(dy, saved, g_pre, g_post, w_in, w_out, tag):
        xin, xn, gu, a, h = saved
        w_gu = w_in.reshape(2, 1, -1, D)
        half = gu.shape[2] // 2
        dh, dg_post, dgu = post_bwd(dy, h, g_post, 0.5, (w_out, 0), 1, f"{tag}_bwd_out", gu=gu, tm=tm_ffn)
        dw_out = wgrad([a], [dh[None]], 2, (2, 1, half, D), f"{tag}_dw_out", a_cols=half)
        dw_in = wgrad([dgu], [xn[None]], 4, (4, 1, half, D), f"{tag}_dw_in", a_cols=half)
        started, tok = scatter_start([dw_in.reshape(8, -1, D), dw_out.reshape(8, -1, D)], f"{tag}_scatter_start")
        dx, dg_pre = pre_bwd([(dgu, (w_gu, 0), 0)], xin, behind(g_pre, tok), dy, f"{tag}_bwd_in", transposed=True)
        return dx, dg_pre, dg_post, started

    def scatter_start(blocked, name):
        pairs = [(g, lax.empty((N_DEV - 1,) + g.shape[1:], g.dtype)) for g in blocked]
        started, tok = exchange_start([pairs], "scatter", name)
        return started[0], tok

    (w1_in,) = gathered(0, token)
    x1, s_f1a, w1_out = ffn_fwd(x0, vec(ffn1_pre_g, 0), vec(ffn1_post_g, 0), w1_in, lambda after: gathered(1, after)[0],
                                "l0_ffn1")
    cw_in_g, cw_out, ck_g = gathered(2, x1)
    cw_in = _col_blocks(cw_in_g[:, 0], 3)
    ck = ck_g[:, 0].transpose(1, 0, 2).reshape(3, D)
    xn_c, (bch,) = rms_proj(x1, vec(mix_pre_g, 0), [(cw_in, 0)], [BF16], "conv_in")
    tmc = _tile(seq_len, TM, 16)
    hb = tmc // HALO

    def cpiece(p):
        return pl.BlockSpec((1, tmc, D), lambda i, p=p: (p, i, 0))

    def chalo(p):
        return pl.BlockSpec((1, HALO, D), lambda i, p=p: (p, jnp.maximum(i * hb - 1, 0), 0))

    conv_specs = [cpiece(0), cpiece(1), cpiece(2), chalo(1), chalo(2), pl.BlockSpec((3, D), lambda i: (0, 0))]
    z_c, m_c, x2 = mix_out(_make_conv_pro(seq_len // tmc), [bch, bch, bch, bch, bch, ck], conv_specs, (cw_out, 0),
                           x1, vec(mix_post_g, 0), 1.0, 1, "conv_out", tm=tmc)
    w2_in, w2_out = gathered(3, x2)
    x3, s_f2a, _ = ffn_fwd(x2, vec(ffn2_pre_g, 0), vec(ffn2_post_g, 0), w2_in, w2_out, "l0_ffn2")

    kvw_g, w1_in_b, w1_out_b, qgw_g, ow = gathered(4, x3)
    qg_w = _col_blocks(qgw_g[:, 0], 2)
    kv_whole = kvw_g.reshape(2 * D + N_HEADS, D).T
    kv_wb = kv_whole[:, :2 * D].reshape(D, 2, D).transpose(1, 0, 2)[:, None]
    f_w = jnp.pad(kv_whole[:, 2 * D:], ((0, 0), (0, LANES - N_HEADS)))[None, None]
    xn_kv, (kv, fl) = rms_proj(x3, kv_g[None], [(kv_wb, 0), (f_w, 0)], [BF16, F32], "kv_in")
    c128 = forget_fwd(fl, fb, seq_len, "forget_fwd")
    cb = jnp.repeat(c128[:, :N_HEADS], HEAD_DIM, axis=1)

    x4, s_f1b, _ = ffn_fwd(x3, vec(ffn1_pre_g, 1), vec(ffn1_post_g, 1), w1_in_b, w1_out_b, "l1_ffn1")
    xn_a, (qg,) = rms_proj(x4, vec(mix_pre_g, 1), [(qg_w, 0)], [BF16], "attn_in")
    o, lse = attn_fwd(qg, kv, cb, n_seq, seq_len, "attn_fwd")
    tm = _tile(T, TM, 16)
    gate_specs = [pl.BlockSpec((1, tm, D), lambda i: (1, i, 0)), pl.BlockSpec((tm, D), lambda i: (i, 0))]
    z_a, m_a, x5 = mix_out(_attn_gate_pro, [qg, o], gate_specs, (ow, 0), x4, vec(mix_post_g, 1), 1.0, 1, "attn_out")
    w2_in_b, w2_out_b = gathered(5, x5)
    x6, s_f2b, _ = ffn_fwd(x5, vec(ffn2_pre_g, 1), vec(ffn2_post_g, 1), w2_in_b, w2_out_b, "l1_ffn2")

    dy, loss_part = loss_head(x6, target, "loss_head")

    scatters = {}
    dx5, dg, dgp, scatters["ffn2", 1] = ffn_bwd(dy, s_f2b, vec(ffn2_pre_g, 1), vec(ffn2_post_g, 1), w2_in_b, w2_out_b, "l1_ffn2")
    grads_small["ffn2_pre", 1], grads_small["ffn2_post", 1] = dg, dgp
    dm_a, dgp, dz_a = post_bwd(dx5, m_a, vec(mix_post_g, 1), 1.0, (ow, 0), 1, "attn_bwd_out")
    grads_small["mix_post", 1] = dgp
    dq, dgate, dk, dv, dcb = attn_bwd(dz_a[0], qg, kv, o, lse, cb, n_seq, seq_len, "attn_bwd")
    dx4, dg = pre_bwd([(dq, (qg_w, 0), 0), (dgate, (qg_w, 0), 1)], x4, vec(mix_pre_g, 1), dx5, "attn_bwd_in")
    grads_small["mix_pre", 1] = dg
    d_ow = wgrad([z_a], [dm_a[None]], 1, ow.shape, "attn_dw_o")
    d_qgw = wgrad([xn_a[None]], [dq, dgate], 2, (2, 1, D, D), "attn_dw_qg")
    scatters["attn"], tok = scatter_start([_col_shards(d_qgw[:, 0]), d_ow.reshape(8, -1, D)], "attn_scatter_start")
    dx3, dg, dgp, scatters["ffn1", 1] = ffn_bwd(dx4, s_f1b, vec(ffn1_pre_g, 1), behind(vec(ffn1_post_g, 1), tok),
                                                w1_in_b, w1_out_b, "l1_ffn1")
    grads_small["ffn1_pre", 1], grads_small["ffn1_post", 1] = dg, dgp

    dc16 = dcb.reshape(T, N_HEADS, HEAD_DIM)[:, :, 0]
    dfl, dfb = forget_bwd(jnp.pad(dc16, ((0, 0), (0, LANES - N_HEADS))), fl, fb, seq_len, "forget_bwd")
    dx3, dg_kv = pre_bwd([(dk, (kv_wb, 0), 0), (dv, (kv_wb, 0), 1), (dfl, (f_w, 0), 0)], x3, kv_g[None], dx3, "kv_bwd_in")
    d_kvw = wgrad([xn_kv[None]], [dk, dv], 2, (2, 1, D, D), "kv_dw")
    d_fw = wgrad([xn_kv[None]], [dfl.astype(BF16)], 1, (1, 1, D, LANES), "forget_dw")
    d_kv_whole = jnp.concatenate([d_kvw[0, 0], d_kvw[1, 0], d_fw[0, 0, :, :N_HEADS]], axis=1)
    wshard = D * 2 + N_HEADS
    scatters["kv"], tok = scatter_start([d_kv_whole.T.reshape(N_DEV, wshard // N_DEV, D)], "kv_scatter_start")

    dx2, dg, dgp, scatters["ffn2", 0] = ffn_bwd(dx3, s_f2a, vec(ffn2_pre_g, 0), behind(vec(ffn2_post_g, 0), tok),
                                                w2_in, w2_out, "l0_ffn2")
    grads_small["ffn2_pre", 0], grads_small["ffn2_post", 0] = dg, dgp
    dm_c, dgp, dz_c = post_bwd(dx2, m_c, vec(mix_post_g, 0), 1.0, (cw_out, 0), 1, "conv_bwd_out")
    grads_small["mix_post", 0] = dgp
    dbch, d_ck = conv_bwd_mix(dz_c, bch, ck, seq_len, "conv_bwd_mix")
    dx1, dg = pre_bwd([(dbch, (cw_in, 0), 0)], x1, vec(mix_pre_g, 0), dx2, "conv_bwd_in")
    grads_small["mix_pre", 0] = dg
    d_cw_out = wgrad([z_c], [dm_c[None]], 1, cw_out.shape, "conv_dw_out")
    d_cw_in = wgrad([xn_c[None]], [dbch], 3, (3, 1, D, D), "conv_dw_in")
    scatters["conv"], tok = scatter_start([_col_shards(d_cw_in[:, 0]), d_cw_out.reshape(8, -1, D)], "conv_scatter_start")
    dx0, dg, dgp, scatters["ffn1", 0] = ffn_bwd(dx1, s_f1a, vec(ffn1_pre_g, 0), behind(vec(ffn1_post_g, 0), tok),
                                                w1_in, w1_out, "l0_ffn1")
    grads_small["ffn1_pre", 0], grads_small["ffn1_post", 0] = dg, dgp

    parts_of = {}

    def scatter_end(keys, afters, name):
        for k, (sent, recv) in zip(keys, exchange_wait([scatters[k] for k in keys], afters, name)):
            parts_of[k] = list(zip(sent, recv))

    scatter_end([("ffn2", 1), "attn", ("ffn1", 1), "kv", ("ffn2", 0), "conv"], [dx0], "scatter_wait")
    sharded = {"ffn2_w_in": (ffn2_w_in, m_ffn2_w_in, v_ffn2_w_in), "ffn2_w_out": (ffn2_w_out, m_ffn2_w_out, v_ffn2_w_out),
               "conv_w_in": (conv_w_in, m_conv_w_in, v_conv_w_in), "conv_w_out": (conv_w_out, m_conv_w_out, v_conv_w_out),
               "kv_w": (kv_w, m_kv_w, v_kv_w), "attn_w_qg": (attn_w_qg, m_attn_w_qg, v_attn_w_qg),
               "attn_w_o": (attn_w_o, m_attn_w_o, v_attn_w_o),
               "ffn1_w_in": (ffn1_w_in, m_ffn1_w_in, v_ffn1_w_in), "ffn1_w_out": (ffn1_w_out, m_ffn1_w_out, v_ffn1_w_out)}
    out = {}
    for nm, (w, mm, vv) in sharded.items():
        if nm == "ffn1_w_in":
            scatter_end([("ffn1", 0)], [res[0] for res in out.values()], "scatter_wait_last")
        contribs = {
            "ffn1_w_in": lambda: [parts_of["ffn1", 0][0], parts_of["ffn1", 1][0]],
            "ffn1_w_out": lambda: [parts_of["ffn1", 0][1], parts_of["ffn1", 1][1]],
            "ffn2_w_in": lambda: [parts_of["ffn2", 0][0], parts_of["ffn2", 1][0]],
            "ffn2_w_out": lambda: [parts_of["ffn2", 0][1], parts_of["ffn2", 1][1]],
            "conv_w_in": lambda: [parts_of["conv"][0]], "conv_w_out": lambda: [parts_of["conv"][1]],
            "kv_w": lambda: [parts_of["kv"][0]], "attn_w_qg": lambda: [parts_of["attn"][0]],
            "attn_w_o": lambda: [parts_of["attn"][1]]}[nm]()
        if nm in ("ffn1_w_in", "ffn2_w_in", "kv_w"):
            rows, cols = w.shape[-2:]

            def view(a):
                return a.reshape(-1, rows, cols).transpose(0, 2, 1)

            res = adam_sharded(view(w), view(mm), view(vv), contribs, me, f"adam_{nm}")
            out[nm] = [r.transpose(0, 2, 1).reshape(w.shape) for r in res]
        else:
            shape3 = (len(contribs),) + contribs[0][0].shape[1:]
            res = adam_sharded(w.reshape(shape3), mm.reshape(shape3), vv.reshape(shape3), contribs, me, f"adam_{nm}")
            out[nm] = [r.reshape(w.shape) for r in res]

    small_names = ["ffn1_pre", "ffn1_post", "mix_pre", "mix_post", "ffn2_pre", "ffn2_post"]
    parts = [(grads_small[n, l], 2 * k + l) for k, n in enumerate(small_names) for l in range(2)]
    parts += [(dg_kv, 12), (dfb, 13), (d_ck, 14), (loss_part, 17)]
    total = all_reduce_small(parts, 24, out["ffn1_w_in"][0], "all_reduce_small")
    loss = total[17, 0]
    d_ck_mine = lax.dynamic_slice(total, (14, dev * LANES), (3, LANES))
    gains = [(ffn1_pre_g, m_ffn1_pre_g, v_ffn1_pre_g), (ffn1_post_g, m_ffn1_post_g, v_ffn1_post_g),
             (mix_pre_g, m_mix_pre_g, v_mix_pre_g), (mix_post_g, m_mix_post_g, v_mix_post_g),
             (ffn2_pre_g, m_ffn2_pre_g, v_ffn2_pre_g), (ffn2_post_g, m_ffn2_post_g, v_ffn2_post_g)]
    small_params = [(w, m, v, (2 * k, 2, D)) for k, (w, m, v) in enumerate(gains)]
    small_params += [(kv_g[None], m_kv_g[None], v_kv_g[None], (12, 1, D)),
                     (forget_b[None], m_forget_b[None], v_forget_b[None], (13, 1, N_HEADS)),
                     (conv_k[0], m_conv_k[0], v_conv_k[0], 0)]
    small_res = adam_small(small_params, total, [d_ck_mine], "adam_small")
    small_keys = [n + "_g" for n in small_names] + ["kv_g", "forget_b", "conv_k"]
    shapes = {"kv_g": kv_g.shape, "forget_b": forget_b.shape, "conv_k": conv_k.shape}
    small = [{key: res[kind].reshape(shapes.get(key, res[kind].shape)) for key, res in zip(small_keys, small_res)}
             for kind in range(4)]
    order = ["ffn1_pre_g", "ffn1_post_g", "ffn1_w_in", "ffn1_w_out", "mix_pre_g", "mix_post_g", "ffn2_pre_g", "ffn2_post_g",
             "ffn2_w_in", "ffn2_w_out", "conv_w_in", "conv_k", "conv_w_out", "kv_g", "kv_w", "forget_b", "attn_w_qg",
             "attn_w_o"]
    results = [loss, dx0.reshape(x.shape)]
    for kind in range(4):
        for nm in order:
            results.append(out[nm][kind] if nm in out else small[kind][nm])
    return tuple(results)
```

```python
import functools
import math

import jax
import jax.numpy as jnp
from jax import lax
from jax.experimental import pallas as pl
from jax.experimental.pallas import tpu as pltpu

F32, BF16 = jnp.float32, jnp.bfloat16
D = 1024
N_HEADS = 16
HEAD_DIM = 64
N_DEV = 8
RMS_EPS = 1e-6
ATT_SCALE = 1.0 / math.sqrt(HEAD_DIM)
LANES = 128
HALO = 8
TM = 512
TQ = 512
VMEM_LIMIT = 48 * 1024 * 1024
MESH = pl.DeviceIdType.MESH

ADAM_LR, ADAM_B1, ADAM_B2, ADAM_EPS, ADAM_WD, ADAM_STEP = 0.001, 0.9, 0.999, 1e-08, 0.01, 10

NT = (((1,), (1,)), ((), ()))
TN = (((0,), (0,)), ((), ()))


def _params(n_axes, vmem_limit=VMEM_LIMIT):
    return pltpu.CompilerParams(dimension_semantics=("arbitrary",) * n_axes, vmem_limit_bytes=vmem_limit)


def _tile(n, cap, mult):
    best = None
    for t in range(mult, min(n, cap) + 1, mult):
        if n % t == 0:
            best = t
    assert best is not None, (n, cap, mult)
    return best


def _rms_rstd(x):
    return lax.rsqrt(jnp.mean(x * x, axis=-1, keepdims=True) + RMS_EPS)


def _rms_fwd(x, g):
    return x * _rms_rstd(x) * g


def _rms_bwd(x, g, dy):
    xh = x * _rms_rstd(x)
    dyg = dy * g
    dx = _rms_rstd(x) * (dyg - xh * jnp.mean(dyg * xh, axis=-1, keepdims=True))
    return dx, jnp.sum(dy * xh, axis=0, keepdims=True)


def _accumulate(ref, first, value):
    @pl.when(first)
    def _():
        ref[...] = value

    @pl.when(jnp.logical_not(first))
    def _():
        ref[...] += value


def rms_proj(x, g, ws, out_dtypes, name, transposed=False):
    T = x.shape[0]
    tm = _tile(T, TM, 16)
    na = len(ws)

    def body(x_ref, g_ref, *refs):
        w_refs, xn_ref, o_refs = refs[:na], refs[na], refs[na + 1:]
        xn = _rms_fwd(x_ref[...], g_ref[...]).astype(BF16)
        xn_ref[...] = xn
        for a, (w, l) in enumerate(ws):
            for p in range(w.shape[0]):
                if transposed:
                    y = lax.dot_general(xn, w_refs[a][p, l], NT, preferred_element_type=F32)
                else:
                    y = jnp.dot(xn, w_refs[a][p, l], preferred_element_type=F32)
                o_refs[a][p] = y.astype(o_refs[a].dtype)

    in_specs = [pl.BlockSpec((tm, D), lambda i: (i, 0)), pl.BlockSpec((1, D), lambda i: (0, 0))]
    in_specs += [pl.BlockSpec(w.shape, lambda i: (0, 0, 0, 0), pipeline_mode=pl.Buffered(1)) for w, _ in ws]
    out_specs = [pl.BlockSpec((tm, D), lambda i: (i, 0))]
    out_shape = [jax.ShapeDtypeStruct((T, D), BF16)]
    for (w, _), dt in zip(ws, out_dtypes):
        nb, wb = w.shape[0], w.shape[2 if transposed else 3]
        out_specs.append(pl.BlockSpec((nb, tm, wb), lambda i: (0, i, 0)))
        out_shape.append(jax.ShapeDtypeStruct((nb, T, wb), dt))
    res = pl.pallas_call(
        body, name=name, grid=(T // tm,), in_specs=in_specs, out_specs=out_specs, out_shape=out_shape,
        compiler_params=_params(1),
    )(x, g, *[w for w, _ in ws])
    return res[0], res[1:]


def mix_out(pro, pro_inputs, pro_specs, w, res, g_post, alpha, nk, name, tm=None):
    w4, l = w
    T = res.shape[0]
    tm = _tile(T, TM, 16) if tm is None else tm
    dpb = N_DEV // nk
    rows = w4.shape[2]
    kb = dpb * rows
    npi = len(pro_inputs)

    def body(*refs):
        pro_refs = refs[:npi]
        w_ref, res_ref, g_ref, z_ref, m_ref, y_ref = refs[npi:]
        i = pl.program_id(0)
        m = None
        for k in range(nk):
            z = pro(i, k, *pro_refs).astype(BF16)
            z_ref[k] = z
            part = jnp.dot(z, w_ref[k * dpb:(k + 1) * dpb, l].reshape(kb, D), preferred_element_type=F32)
            m = part if m is None else m + part
        m_ref[...] = m
        y_ref[...] = res_ref[...] + alpha * _rms_fwd(m, g_ref[...])

    row = pl.BlockSpec((tm, D), lambda i: (i, 0))
    in_specs = list(pro_specs) + [
        pl.BlockSpec(w4.shape, lambda i: (0, 0, 0, 0), pipeline_mode=pl.Buffered(1)), row, pl.BlockSpec((1, D), lambda i: (0, 0))]
    z, m, y = pl.pallas_call(
        body, name=name, grid=(T // tm,), in_specs=in_specs,
        out_specs=[pl.BlockSpec((nk, tm, kb), lambda i: (0, i, 0)), row, row],
        out_shape=[jax.ShapeDtypeStruct((nk, T, kb), BF16), jax.ShapeDtypeStruct((T, D), F32),
                   jax.ShapeDtypeStruct((T, D), F32)],
        compiler_params=_params(1),
    )(*pro_inputs, w4, res, g_post)
    return z, m, y


def post_bwd(dy, m, g_post, alpha, w, nk, name, gu=None, tm=None):
    w4, l = w
    T = dy.shape[0]
    tm = _tile(T, TM, 16) if tm is None else tm
    dpb = N_DEV // nk
    rows = w4.shape[2]
    kb = dpb * rows
    swiglu = gu is not None

    def body(dy_ref, m_ref, g_ref, w_ref, *refs):
        if swiglu:
            gu_ref, dm_ref, dg_ref, dgu_ref = refs
        else:
            dm_ref, dg_ref, dz_ref = refs
        dm, dg = _rms_bwd(m_ref[...], g_ref[...], alpha * dy_ref[...])
        dm = dm.astype(BF16)
        dm_ref[...] = dm
        _accumulate(dg_ref, pl.program_id(0) == 0, dg)
        for k in range(nk):
            dz = lax.dot_general(dm, w_ref[k * dpb:(k + 1) * dpb, l].reshape(kb, D), NT, preferred_element_type=F32)
            if swiglu:
                gate, up = gu_ref[k].astype(F32), gu_ref[k + nk].astype(F32)
                sig = jax.nn.sigmoid(gate)
                dgu_ref[k] = (dz * up * sig * (1.0 + gate * (1.0 - sig))).astype(BF16)
                dgu_ref[k + nk] = (dz * gate * sig).astype(BF16)
            else:
                dz_ref[k] = dz.astype(BF16)

    row = pl.BlockSpec((tm, D), lambda i: (i, 0))
    vec = pl.BlockSpec((1, D), lambda i: (0, 0))
    blkk = pl.BlockSpec((nk, tm, kb), lambda i: (0, i, 0))
    in_specs = [row, row, vec, pl.BlockSpec(w4.shape, lambda i: (0, 0, 0, 0), pipeline_mode=pl.Buffered(1))]
    inputs = [dy, m, g_post, w4]
    n_dz = 2 * nk if swiglu else nk
    out_specs = [row, vec, pl.BlockSpec((n_dz, tm, kb), lambda i: (0, i, 0))]
    out_shape = [jax.ShapeDtypeStruct((T, D), BF16), jax.ShapeDtypeStruct((1, D), F32),
                 jax.ShapeDtypeStruct((n_dz, T, kb), BF16)]
    if swiglu:
        in_specs.append(pl.BlockSpec((2 * nk, tm, kb), lambda i: (0, i, 0)))
        inputs.append(gu)
    return pl.pallas_call(
        body, name=name, grid=(T // tm,), in_specs=in_specs, out_specs=out_specs, out_shape=out_shape,
        compiler_params=_params(1),
    )(*inputs)


def pre_bwd(pieces, x, g_pre, dres, name, transposed=False):
    T = x.shape[0]
    tm = _tile(T, TM, 16)
    na = len(pieces)
    weights = []
    for _, (w4, _), _ in pieces:
        if not any(w4 is w for w in weights):
            weights.append(w4)
    nw = len(weights)
    which = [[w4 is w for w in weights].index(True) for _, (w4, _), _ in pieces]

    def body(*refs):
        dz_refs, w_refs = refs[:na], refs[na:na + nw]
        x_ref, g_ref, dres_ref, dx_ref, dg_ref = refs[na + nw:]
        acc = None
        for a, (dz, (_, l), w_off) in enumerate(pieces):
            for p in range(dz.shape[0]):
                w = w_refs[which[a]][w_off + p, l]
                if transposed:
                    part = jnp.dot(dz_refs[a][p].astype(BF16), w, preferred_element_type=F32)
                else:
                    part = lax.dot_general(dz_refs[a][p].astype(BF16), w, NT, preferred_element_type=F32)
                acc = part if acc is None else acc + part
        dx, dg = _rms_bwd(x_ref[...], g_ref[...], acc)
        dx_ref[...] = dres_ref[...] + dx
        _accumulate(dg_ref, pl.program_id(0) == 0, dg)

    row = pl.BlockSpec((tm, D), lambda i: (i, 0))
    vec = pl.BlockSpec((1, D), lambda i: (0, 0))
    dz_specs = [pl.BlockSpec((dz.shape[0], tm, dz.shape[2]), lambda i: (0, i, 0)) for dz, _, _ in pieces]
    w_specs = [pl.BlockSpec(w.shape, lambda i: (0, 0, 0, 0), pipeline_mode=pl.Buffered(1)) for w in weights]
    return pl.pallas_call(
        body, name=name, grid=(T // tm,), in_specs=dz_specs + w_specs + [row, vec, row],
        out_specs=[row, vec], out_shape=[jax.ShapeDtypeStruct((T, D), F32), jax.ShapeDtypeStruct((1, D), F32)],
        compiler_params=_params(1),
    )(*[p[0] for p in pieces], *weights, x, g_pre, dres)


def wgrad(a_list, b_list, nout, out4_shape, name, a_cols=None, tm_cap=None, after=None):
    T = a_list[0].shape[1]
    tm = _tile(T, 4 * TM if tm_cap is None else tm_cap, 16)
    nt = T // tm
    dpb = out4_shape[0] // nout
    _, _, R, C = out4_shape
    na, nb = len(a_list), len(b_list)
    a_w = a_list[0].shape[2] if a_cols is None else a_cols
    a_per = a_list[0].shape[2] // a_w

    def spans(arrs, per):
        ns = [a.shape[0] * per for a in arrs]
        return ns, [sum(ns[:k]) for k in range(len(ns))], sum(ns)

    a_ns, a_offs, a_tot = spans(a_list, a_per)
    b_ns, b_offs, b_tot = spans(b_list, 1)
    assert a_tot in (1, nout) and b_tot in (1, nout)

    def body(*refs):
        a_refs, b_refs = refs[:na], refs[na:na + nb]
        out_ref, acc = refs[-2:]
        p, t = pl.program_id(0), pl.program_id(1)
        for ia in range(na):
            for ib in range(nb):
                conds = []
                if a_tot > 1:
                    conds += [p >= a_offs[ia], p < a_offs[ia] + a_ns[ia]]
                if b_tot > 1:
                    conds += [p >= b_offs[ib], p < b_offs[ib] + b_ns[ib]]

                def work(ia=ia, ib=ib):
                    part = lax.dot_general(a_refs[ia][0], b_refs[ib][0], TN, preferred_element_type=F32)
                    _accumulate(acc, t == 0, part)

                if conds:
                    pl.when(functools.reduce(jnp.logical_and, conds))(work)
                else:
                    work()

        @pl.when(t == nt - 1)
        def _():
            out_ref[...] = acc[...].astype(BF16).reshape(dpb, 1, R, C)

    def blk(off, n, tot):
        if tot == 1:
            return lambda p: 0
        return lambda p: jnp.clip(p - off, 0, n - 1)

    in_specs = []
    for arr, n, off in zip(a_list, a_ns, a_offs):
        in_specs.append(pl.BlockSpec((1, tm, a_w), lambda p, t, f=blk(off, n, a_tot): (f(p) // a_per, t, f(p) % a_per)))
    for arr, n, off in zip(b_list, b_ns, b_offs):
        in_specs.append(pl.BlockSpec((1, tm, arr.shape[2]), lambda p, t, f=blk(off, n, b_tot): (f(p), t, 0)))
    extra = [] if after is None else [after]
    return pl.pallas_call(
        body, name=name, grid=(nout, nt), in_specs=in_specs + [ANY] * len(extra),
        out_specs=pl.BlockSpec((dpb, 1, R, C), lambda p, t: (p, 0, 0, 0)),
        out_shape=jax.ShapeDtypeStruct(out4_shape, BF16),
        scratch_shapes=[pltpu.VMEM((dpb * R, C), F32)], compiler_params=_params(2),
    )(*a_list, *b_list, *extra)


def _swiglu_pro(i, k, gu_ref):
    gate, up = gu_ref[k].astype(F32), gu_ref[k + gu_ref.shape[0] // 2].astype(F32)
    return gate * jax.nn.sigmoid(gate) * up


def _attn_gate_pro(i, k, gate_ref, o_ref):
    return jax.nn.sigmoid(gate_ref[0].astype(F32)) * o_ref[...].astype(F32)


def _shift_rows(u, halo, d):
    rolled = pltpu.roll(u, d, 0)
    row = lax.broadcasted_iota(jnp.int32, u.shape, 0)
    for r in range(d):
        rolled = jnp.where(row == r, halo[HALO - d + r:HALO - d + r + 1, :], rolled)
    return rolled


def _advance_rows(u, halo, d):
    n = u.shape[0]
    rolled = pltpu.roll(u, n - d, 0)
    row = lax.broadcasted_iota(jnp.int32, u.shape, 0)
    for r in range(d):
        rolled = jnp.where(row == n - d + r, halo[r:r + 1, :], rolled)
    return rolled


def _make_conv_pro(tiles_per_seq):
    def pro(i, k, b_ref, c_ref, h_ref, ch_ref, hh_ref, ck_ref):
        u = c_ref[0].astype(F32) * h_ref[0].astype(F32)
        first = (i % tiles_per_seq) == 0
        halo = jnp.where(first, 0.0, ch_ref[0].astype(F32) * hh_ref[0].astype(F32))
        ck = ck_ref[...]
        conv = ck[2:3, :] * u + ck[1:2, :] * _shift_rows(u, halo, 1) + ck[0:1, :] * _shift_rows(u, halo, 2)
        return b_ref[0].astype(F32) * conv
    return pro


def conv_bwd_mix(dz, bch, conv_k, seq_len, name):
    T = dz.shape[1]
    tm = _tile(seq_len, TM, 16)
    tps = seq_len // tm
    nt = T // tm
    hb = tm // HALO

    def body(dz_ref, b_ref, c_ref, h_ref, cp_ref, hp_ref, dzn_ref, bn_ref, ck_ref, dbch_ref, dk_ref):
        i = pl.program_id(0)
        first = (i % tps) == 0
        last = (i % tps) == tps - 1
        b, c, h = b_ref[0].astype(F32), c_ref[0].astype(F32), h_ref[0].astype(F32)
        dzt = dz_ref[0].astype(F32)
        u = c * h
        prev = jnp.where(first, 0.0, cp_ref[0].astype(F32) * hp_ref[0].astype(F32))
        u1, u2 = _shift_rows(u, prev, 1), _shift_rows(u, prev, 2)
        ck = ck_ref[...]
        conv = ck[2:3, :] * u + ck[1:2, :] * u1 + ck[0:1, :] * u2
        dconv = dzt * b
        nxt = jnp.where(last, 0.0, dzn_ref[0].astype(F32) * bn_ref[0].astype(F32))
        du = ck[2:3, :] * dconv + ck[1:2, :] * _advance_rows(dconv, nxt, 1) + ck[0:1, :] * _advance_rows(dconv, nxt, 2)
        dbch_ref[0] = (dzt * conv).astype(BF16)
        dbch_ref[1] = (du * h).astype(BF16)
        dbch_ref[2] = (du * c).astype(BF16)
        tap = lax.broadcasted_iota(jnp.int32, (3, D), 0)
        dk = jnp.where(tap == 0, jnp.sum(dconv * u2, axis=0, keepdims=True),
                       jnp.where(tap == 1, jnp.sum(dconv * u1, axis=0, keepdims=True),
                                 jnp.sum(dconv * u, axis=0, keepdims=True)))
        _accumulate(dk_ref, i == 0, dk)

    def piece(p):
        return pl.BlockSpec((1, tm, D), lambda i, p=p: (p, i, 0))

    def prev(p):
        return pl.BlockSpec((1, HALO, D), lambda i, p=p: (p, jnp.maximum(i * hb - 1, 0), 0))

    def nxt(p):
        return pl.BlockSpec((1, HALO, D), lambda i, p=p: (p, jnp.minimum((i + 1) * hb, nt * hb - 1), 0))

    return pl.pallas_call(
        body, name=name, grid=(nt,),
        in_specs=[piece(0), piece(0), piece(1), piece(2), prev(1), prev(2), nxt(0), nxt(0),
                  pl.BlockSpec((3, D), lambda i: (0, 0))],
        out_specs=[pl.BlockSpec((3, tm, D), lambda i: (0, i, 0)), pl.BlockSpec((3, D), lambda i: (0, 0))],
        out_shape=[jax.ShapeDtypeStruct((3, T, D), BF16), jax.ShapeDtypeStruct((3, D), F32)],
        compiler_params=_params(1),
    )(dz, bch, bch, bch, bch, bch, dz, bch, conv_k)


def _log_sigmoid(x):
    return jnp.minimum(x, 0.0) - jnp.log(1.0 + jnp.exp(-jnp.abs(x)))


def forget_fwd(fl, fb, seq_len, name):
    T = fl.shape[1]

    def body(fl_ref, fb_ref, c_ref):
        c = _log_sigmoid(fl_ref[0] + fb_ref[...])
        row = lax.broadcasted_iota(jnp.int32, c.shape, 0)
        k = 1
        while k < seq_len:
            c = c + jnp.where(row >= k, pltpu.roll(c, k, 0), 0.0)
            k *= 2
        c_ref[...] = c

    return pl.pallas_call(
        body, name=name, grid=(T // seq_len,),
        in_specs=[pl.BlockSpec((1, seq_len, LANES), lambda b: (0, b, 0)), pl.BlockSpec((1, LANES), lambda b: (0, 0))],
        out_specs=pl.BlockSpec((seq_len, LANES), lambda b: (b, 0)),
        out_shape=jax.ShapeDtypeStruct((T, LANES), F32), compiler_params=_params(1),
    )(fl, fb)


def forget_bwd(dc, fl, fb, seq_len, name):
    T = dc.shape[0]

    def body(dc_ref, fl_ref, fb_ref, dfl_ref, dfb_ref):
        b = pl.program_id(0)
        r = dc_ref[...]
        row = lax.broadcasted_iota(jnp.int32, r.shape, 0)
        k = 1
        while k < seq_len:
            r = r + jnp.where(row < seq_len - k, pltpu.roll(r, seq_len - k, 0), 0.0)
            k *= 2
        dfl = r * jax.nn.sigmoid(-(fl_ref[0] + fb_ref[...]))
        dfl_ref[0] = dfl
        _accumulate(dfb_ref, b == 0, jnp.sum(dfl, axis=0, keepdims=True))

    return pl.pallas_call(
        body, name=name, grid=(T // seq_len,),
        in_specs=[pl.BlockSpec((seq_len, LANES), lambda b: (b, 0)), pl.BlockSpec((1, seq_len, LANES), lambda b: (0, b, 0)),
                  pl.BlockSpec((1, LANES), lambda b: (0, 0))],
        out_specs=[pl.BlockSpec((1, seq_len, LANES), lambda b: (0, b, 0)), pl.BlockSpec((1, LANES), lambda b: (0, 0))],
        out_shape=[jax.ShapeDtypeStruct((1, T, LANES), F32), jax.ShapeDtypeStruct((1, LANES), F32)],
        compiler_params=_params(1),
    )(dc, fl, fb)


HEADS = (slice(0, HEAD_DIM), slice(HEAD_DIM, 2 * HEAD_DIM))


def _key_blocks(i, step, carry):
    carry = lax.fori_loop(0, i // 2, lambda jj, c: step(2 * jj, c, False, 2), carry)
    return lax.cond(i % 2 == 1, lambda c: step(i - 1, c, True, 2), lambda c: step(i, c, True, 1), carry)


def _causal(width, tq):
    keys = lax.broadcasted_iota(jnp.int32, (width * tq, tq), 0)
    return keys <= lax.broadcasted_iota(jnp.int32, (width * tq, tq), 1) + (width - 1) * tq


def attn_fwd(qg, kv, cb, n_seq, seq_len, name):
    T = n_seq * seq_len
    tq = _tile(seq_len, TQ, LANES)
    nq = seq_len // tq

    def body(q_ref, k_ref, v_ref, cb_ref, o_ref, lse_ref):
        i = pl.program_id(2)
        q8 = [q_ref[0, :, sl] * ATT_SCALE for sl in HEADS]

        def block(j, carry, diagonal, width):
            rows = pl.ds(pl.multiple_of(j * tq, tq), width * tq)
            out = []
            for hh, sl in enumerate(HEADS):
                m, l, acc = carry[3 * hh:3 * hh + 3]
                s = lax.dot_general(k_ref[0, rows, sl], q8[hh], NT, preferred_element_type=F32)
                s = s - cb_ref[rows, sl.start:sl.start + 1]
                if diagonal:
                    s = jnp.where(_causal(width, tq), s, -1e30)
                m_new = jnp.maximum(m, jnp.max(s, axis=0, keepdims=True))
                a = jnp.exp(m - m_new)
                p = jnp.exp(s - m_new)
                l = a * l + jnp.sum(p, axis=0, keepdims=True)
                acc = a * acc + lax.dot_general(v_ref[0, rows, sl], p.astype(BF16), TN, preferred_element_type=F32)
                out += [m_new, l, acc]
            return tuple(out)

        init = (jnp.full((1, tq), -1e30, F32), jnp.zeros((1, tq), F32), jnp.zeros((HEAD_DIM, tq), F32)) * 2
        carry = _key_blocks(i, block, init)
        o_ref[...] = jnp.concatenate([carry[2] / carry[1], carry[5] / carry[4]], axis=0).T
        for hh in range(2):
            lse_ref[0, 0, 0, hh:hh + 1, :] = carry[3 * hh] + jnp.log(carry[3 * hh + 1])

    seq2 = pl.BlockSpec((seq_len, LANES), lambda b, hp, i: (b, hp))
    return pl.pallas_call(
        body, name=name, grid=(n_seq, N_HEADS // 2, nq),
        in_specs=[pl.BlockSpec((1, tq, LANES), lambda b, hp, i: (0, b * nq + i, hp)),
                  pl.BlockSpec((1, seq_len, LANES), lambda b, hp, i: (0, b, hp)),
                  pl.BlockSpec((1, seq_len, LANES), lambda b, hp, i: (1, b, hp)), seq2],
        out_specs=[pl.BlockSpec((tq, LANES), lambda b, hp, i: (b * nq + i, hp)),
                   pl.BlockSpec((1, 1, 1, 2, tq), lambda b, hp, i: (b, hp, i, 0, 0))],
        out_shape=[jax.ShapeDtypeStruct((T, D), F32), jax.ShapeDtypeStruct((n_seq, N_HEADS // 2, nq, 2, tq), F32)],
        compiler_params=_params(3),
    )(qg, kv, kv, cb)


def attn_bwd(dz, qg, kv, o, lse, cb, n_seq, seq_len, name):
    T = n_seq * seq_len
    tq = _tile(seq_len, TQ, LANES)
    nq = seq_len // tq

    def body(dz_ref, q_ref, gate_ref, o_ref, lse_ref, cb_ref, k_ref, v_ref,
             dq_ref, dgate_ref, dk_ref, dv_ref, dc_ref, p_s, dp_s, dk_s, dv_s, dc_s):
        i = pl.program_id(2)

        @pl.when(i == 0)
        def _():
            dk_s[...] = jnp.zeros_like(dk_s)
            dv_s[...] = jnp.zeros_like(dv_s)
            dc_s[...] = jnp.zeros_like(dc_s)

        dzf = dz_ref[...].astype(F32)
        sig = jax.nn.sigmoid(gate_ref[0].astype(F32))
        dob = (dzf * sig).astype(BF16)
        dgate_ref[0] = (dzf * o_ref[...] * sig * (1.0 - sig)).astype(BF16)
        q8 = [q_ref[0, :, sl] * ATT_SCALE for sl in HEADS]
        do = [dob[:, sl] for sl in HEADS]
        lse_i = [lse_ref[0, 0, 0, hh:hh + 1, :] for hh in range(2)]

        def probs(j, dsums, diagonal, width):
            rows = pl.ds(pl.multiple_of(j * tq, tq), width * tq)
            out = []
            for hh, sl in enumerate(HEADS):
                s = lax.dot_general(k_ref[0, rows, sl], q8[hh], NT, preferred_element_type=F32)
                p = jnp.exp(s - cb_ref[rows, sl.start:sl.start + 1] - lse_i[hh])
                if diagonal:
                    p = jnp.where(_causal(width, tq), p, 0.0)
                dp = lax.dot_general(v_ref[0, rows, sl], do[hh], NT, preferred_element_type=F32)
                p_s[hh, rows, :] = p
                dp_s[hh, rows, :] = dp
                out.append(dsums[hh] + jnp.sum(p * dp, axis=0, keepdims=True))
            return tuple(out)

        dsums = _key_blocks(i, probs, (jnp.zeros((1, tq), F32),) * 2)

        def grads(j, dqs, diagonal, width):
            rows = pl.ds(pl.multiple_of(j * tq, tq), width * tq)
            out = []
            for hh, sl in enumerate(HEADS):
                p = p_s[hh, rows, :]
                ds = p * (dp_s[hh, rows, :] - dsums[hh])
                dc_s[hh, rows, :] -= jnp.sum(ds, axis=1, keepdims=True)
                dsb = ds.astype(BF16)
                dk_s[rows, sl] += jnp.dot(dsb, q8[hh], preferred_element_type=F32)
                dv_s[rows, sl] += jnp.dot(p.astype(BF16), do[hh], preferred_element_type=F32)
                out.append(dqs[hh] + lax.dot_general(k_ref[0, rows, sl], dsb, TN, preferred_element_type=F32))
            return tuple(out)

        dqs = _key_blocks(i, grads, (jnp.zeros((HEAD_DIM, tq), F32),) * 2)
        dq_ref[0] = (jnp.concatenate(dqs, axis=0).T * ATT_SCALE).astype(BF16)

        @pl.when(i == nq - 1)
        def _():
            dk_ref[0] = dk_s[...].astype(BF16)
            dv_ref[0] = dv_s[...].astype(BF16)
            dc_ref[...] = jnp.zeros_like(dc_ref)
            for hh, sl in enumerate(HEADS):
                dc_ref[:, sl.start:sl.start + 1] = dc_s[hh]

    qry2 = pl.BlockSpec((tq, LANES), lambda b, hp, i: (b * nq + i, hp))
    seq2 = pl.BlockSpec((seq_len, LANES), lambda b, hp, i: (b, hp))

    def qry3(p):
        return pl.BlockSpec((1, tq, LANES), lambda b, hp, i, p=p: (p, b * nq + i, hp))

    def seq3(p):
        return pl.BlockSpec((1, seq_len, LANES), lambda b, hp, i, p=p: (p, b, hp))

    act = jax.ShapeDtypeStruct((1, T, D), BF16)
    return pl.pallas_call(
        body, name=name, grid=(n_seq, N_HEADS // 2, nq),
        in_specs=[qry2, qry3(0), qry3(1), qry2, pl.BlockSpec((1, 1, 1, 2, tq), lambda b, hp, i: (b, hp, i, 0, 0)), seq2,
                  seq3(0), seq3(1)],
        out_specs=[qry3(0), qry3(0), seq3(0), seq3(0), seq2],
        out_shape=[act, act, act, act, jax.ShapeDtypeStruct((T, D), F32)],
        scratch_shapes=[pltpu.VMEM((2, seq_len, tq), F32), pltpu.VMEM((2, seq_len, tq), F32),
                        pltpu.VMEM((seq_len, LANES), F32), pltpu.VMEM((seq_len, LANES), F32),
                        pltpu.VMEM((2, seq_len, 1), F32)],
        compiler_params=_params(3),
    )(dz, qg, qg, o, lse, cb, kv, kv)


def loss_head(y, target, name):
    T = y.shape[0]
    tm = _tile(T, TM, 8)

    def body(y_ref, t_ref, dy_ref, loss_ref):
        err = y_ref[...] - t_ref[...]
        dy_ref[...] = err * (1.0 / D)
        part = 0.5 * jnp.sum(jnp.mean(err * err, axis=-1, keepdims=True), axis=0, keepdims=True)
        _accumulate(loss_ref, pl.program_id(0) == 0, jnp.broadcast_to(part, (1, LANES)))

    row = pl.BlockSpec((tm, D), lambda i: (i, 0))
    return pl.pallas_call(
        body, name=name, grid=(T // tm,), in_specs=[row, row],
        out_specs=[row, pl.BlockSpec((1, LANES), lambda i: (0, 0))],
        out_shape=[jax.ShapeDtypeStruct((T, D), F32), jax.ShapeDtypeStruct((1, LANES), F32)],
        compiler_params=_params(1),
    )(y, target)


def _adamw(w, g, m, v):
    m = ADAM_B1 * m + (1.0 - ADAM_B1) * g
    v = ADAM_B2 * v + (1.0 - ADAM_B2) * (g * g)
    m_hat = m / (1.0 - ADAM_B1 ** ADAM_STEP)
    v_hat = v / (1.0 - ADAM_B2 ** ADAM_STEP)
    delta = -ADAM_LR * (m_hat / (jnp.sqrt(v_hat) + ADAM_EPS) + ADAM_WD * w)
    return delta, m, v


def adam_sharded(w, m, v, contribs, me, name):
    L, R, C = w.shape
    tr = _tile(R, 256, 16) if R % 16 == 0 else R

    def body(me_ref, w_ref, m_ref, v_ref, *refs):
        c_refs, (g_ref, d_ref, nm_ref, nv_ref) = refs[:2 * L], refs[2 * L:]
        l = pl.program_id(0)
        for j in range(L):
            @pl.when(l == j)
            def _(j=j):
                own_ref, recv_ref = c_refs[2 * j], c_refs[2 * j + 1]
                g = own_ref[0].astype(F32)
                for k in range(N_DEV - 1):
                    g = g + recv_ref[k].astype(F32)
                delta, nm, nv = _adamw(w_ref[0], g, m_ref[0], v_ref[0])
                g_ref[0] = g
                d_ref[0] = delta
                nm_ref[0] = nm
                nv_ref[0] = nv

    blk = pl.BlockSpec((1, tr, C), lambda l, i, s: (l, i, 0))
    in_specs = [blk, blk, blk]
    inputs = [w, m, v]
    for j, (mine, recv) in enumerate(contribs):
        in_specs.append(pl.BlockSpec((1, tr, C), lambda l, i, s, j=j: (s[0], jnp.where(l == j, i, 0), 0)))
        in_specs.append(pl.BlockSpec((N_DEV - 1, tr, C), lambda l, i, s, j=j: (0, jnp.where(l == j, i, 0), 0)))
        inputs += [mine, recv]
    shp = jax.ShapeDtypeStruct((L, R, C), F32)
    grid_spec = pltpu.PrefetchScalarGridSpec(num_scalar_prefetch=1, grid=(L, R // tr), in_specs=in_specs, out_specs=[blk] * 4)
    return pl.pallas_call(body, name=name, grid_spec=grid_spec, out_shape=[shp] * 4, compiler_params=_params(2))(me, *inputs)


def cast_place(w, l, me, dtype, after, name):
    _, R, C = w.shape
    tr = _tile(R, 512, 16) if R % 16 == 0 else R

    def body(me_ref, w_ref, after_ref, o_ref):
        o_ref[0] = w_ref[0].astype(dtype)

    grid_spec = pltpu.PrefetchScalarGridSpec(
        num_scalar_prefetch=1, grid=(R // tr,), in_specs=[pl.BlockSpec((1, tr, C), lambda i, s: (l, i, 0)), ANY],
        out_specs=pl.BlockSpec((1, tr, C), lambda i, s: (s[0], i, 0)))
    return pl.pallas_call(body, name=name, grid_spec=grid_spec, out_shape=jax.ShapeDtypeStruct((N_DEV, R, C), dtype),
                          compiler_params=_params(1))(me, w, after)


def adam_small(params, total, extra_grads, name):
    n, ne = len(params), len(extra_grads)

    def body(*refs):
        total_ref, extra_refs = refs[0], refs[1:1 + ne]
        ins, outs = refs[1 + ne:1 + ne + 3 * n], refs[1 + ne + 3 * n:]
        for k, (_, _, _, where) in enumerate(params):
            if isinstance(where, int):
                g = extra_refs[where][...]
            else:
                row, rows, width = where
                g = total_ref[row:row + rows, 0:width]
            delta, nm, nv = _adamw(ins[3 * k][...], g, ins[3 * k + 1][...], ins[3 * k + 2][...])
            outs[4 * k][...] = g
            outs[4 * k + 1][...] = delta
            outs[4 * k + 2][...] = nm
            outs[4 * k + 3][...] = nv

    flat = [a for w, m, v, _ in params for a in (w, m, v)]
    out_shape = [jax.ShapeDtypeStruct(w.shape, F32) for w, _, _, _ in params for _ in range(4)]
    res = pl.pallas_call(body, name=name, out_shape=out_shape)(total, *extra_grads, *flat)
    return [res[4 * k:4 * k + 4] for k in range(n)]


def _place():
    return lax.axis_index("x"), lax.axis_index("y"), lax.axis_index("c")


def _peer(place, k):
    x, y, c = place
    return x ^ (k >> 2), y ^ ((k >> 1) & 1), c ^ (k & 1)


ANY = pl.BlockSpec(memory_space=pl.ANY)
HBM = pl.BlockSpec(memory_space=pltpu.HBM)
SEM = pl.BlockSpec(memory_space=pltpu.SEMAPHORE)
EFFECT = pltpu.SideEffectType.DATAFLOW_SIDE_EFFECTING


def _in_hbm(a):
    return pltpu.with_memory_space_constraint(a, pltpu.HBM)


def _number(place):
    return 4 * place[0] + 2 * place[1] + place[2]


SLOTS = {"gather_like": 4, "gather_pass": 3, "scatter": 7}


def _plan(mode, src_ref, land_ref, place):
    if mode == "gather_like":
        mine = land_ref.at[_number(place)]
        return [(mine, mine, _peer(place, k)) for k in (1, 2, 4, 6)]
    if mode == "gather_pass":
        slots = [land_ref.at[_number(_peer(place, k))] for k in (2, 4, 6)]
        return [(slot, slot, _peer(place, 1)) for slot in slots]
    return [(src_ref.at[_number(_peer(place, k))], land_ref.at[k - 1], _peer(place, k)) for k in range(1, N_DEV)]


def _exchange(name, groups, start, afters):
    sizes = [len(g[3]) for g in groups]
    na, ng = sum(sizes), len(groups)
    ns = sum(len(g[2]) for g in groups)
    waits = groups[0][0] is not None
    n_in_sems = 2 * ng if waits else 0
    n_out_sems = 2 * ng if start else 0

    def body(*refs):
        src_refs, land_refs = (refs[:ns] if ns else [None] * na), refs[ns:ns + na]
        in_sems = refs[ns + na:ns + na + n_in_sems]
        outs = refs[ns + na + n_in_sems + len(afters):]
        place = _place()
        a = 0
        for gi, n in enumerate(sizes):
            for idx in range(n):
                if waits:
                    zone = land_refs[a].at[pl.ds(0, groups[gi][4])]
                    copy = pltpu.make_async_remote_copy(
                        src_ref=zone, dst_ref=zone, send_sem=in_sems[2 * gi].at[idx], recv_sem=in_sems[2 * gi + 1].at[idx],
                        device_id=_peer(place, 1), device_id_type=MESH)
                    copy.wait_send()
                    copy.wait_recv()
                if start:
                    for src, dst, peer in _plan(start, src_refs[a], land_refs[a], place):
                        pltpu.make_async_remote_copy(
                            src_ref=src, dst_ref=dst, send_sem=outs[2 * gi].at[idx], recv_sem=outs[2 * gi + 1].at[idx],
                            device_id=peer, device_id_type=MESH).start()
                a += 1
        if start:
            outs[-1][...] = jnp.zeros_like(outs[-1])

    srcs = [_in_hbm(s) for g in groups for s in g[2]]
    lands = [_in_hbm(l) for g in groups for l in g[3]]
    sems = [s for g in groups for s in g[:2]] if waits else []
    out_shape = [pltpu.SemaphoreType.DMA((n,)) for n in sizes for _ in range(2)] if start else []
    out_shape += [pltpu.HBM(a.shape, a.dtype) for a in srcs + lands]
    out_specs = [SEM] * n_out_sems + [HBM] * (ns + na)
    if start:
        out_shape.append(jax.ShapeDtypeStruct((8, LANES), F32))
        out_specs.append(pl.BlockSpec(memory_space=pltpu.VMEM))
    res = pl.pallas_call(
        body, name=name, in_specs=[HBM] * (ns + na) + [SEM] * n_in_sems + [ANY] * len(afters),
        out_shape=out_shape, out_specs=out_specs,
        input_output_aliases={i: n_out_sems + i for i in range(ns + na)},
        compiler_params=pltpu.CompilerParams(has_side_effects=EFFECT),
    )(*srcs, *lands, *sems, *afters)
    new_sems, thru = res[:n_out_sems], res[n_out_sems:n_out_sems + ns + na]
    out, a = [], 0
    for gi, n in enumerate(sizes):
        pair = (new_sems[2 * gi], new_sems[2 * gi + 1]) if start else (None, None)
        out.append(pair + (thru[a:a + n] if ns else [], thru[ns + a:ns + a + n], SLOTS.get(start, 0)))
        a += n
    return out, (res[-1] if start else None)


def exchange_start(pair_groups, mode, name):
    groups = [(None, None, [s for s, _ in g if s is not None], [l for _, l in g], 0) for g in pair_groups]
    return _exchange(name, groups, mode, ())


def exchange_relay(groups, mode, afters, name):
    return _exchange(name, groups, mode, afters)


def exchange_wait(groups, afters, name):
    done, _ = _exchange(name, groups, None, afters)
    return [(g[2], g[3]) for g in done]


def all_reduce_small(parts, n_rows, after, name):
    R = n_rows
    n_parts = len(parts)

    def body(*refs):
        part_refs = refs[:n_parts]
        out_ref, buf, send_sems, recv_sems = refs[n_parts + 1:]
        x, y, c = _place()
        me = 4 * x + 2 * y + c
        own = buf.at[me]
        own[...] = jnp.zeros((R, D), F32)
        for ref, (arr, row) in zip(part_refs, parts):
            own[row:row + arr.shape[0], 0:arr.shape[1]] = ref[...]
        copies = []
        for k in range(1, N_DEV):
            peer = (x ^ (k >> 2), y ^ ((k >> 1) & 1), c ^ (k & 1))
            copies.append(pltpu.make_async_remote_copy(
                src_ref=own, dst_ref=own, send_sem=send_sems.at[k - 1], recv_sem=recv_sems.at[k - 1],
                device_id=peer, device_id_type=MESH))
        for cp in copies:
            cp.start()
        for cp in copies:
            cp.wait()
        total = buf[0]
        for d in range(1, N_DEV):
            total = total + buf[d]
        out_ref[...] = total

    vm = pl.BlockSpec(memory_space=pltpu.VMEM)
    return pl.pallas_call(
        body, name=name, in_specs=[vm] * n_parts + [ANY], out_specs=vm, out_shape=jax.ShapeDtypeStruct((R, D), F32),
        scratch_shapes=[pltpu.VMEM((N_DEV, R, D), F32), pltpu.SemaphoreType.DMA((N_DEV - 1,)),
                        pltpu.SemaphoreType.DMA((N_DEV - 1,))],
    )(*[arr for arr, _ in parts], after)


def _col_blocks(gathered, n_blocks):
    n, d, w = gathered.shape
    whole = gathered.transpose(1, 0, 2).reshape(d, n * w)
    return whole.reshape(d, n_blocks, n * w // n_blocks).transpose(1, 0, 2)[:, None]


def _col_shards(blocks):
    n, d, w = blocks.shape
    whole = blocks.transpose(1, 0, 2).reshape(d, n * w)
    return whole.reshape(d, N_DEV, n * w // N_DEV).transpose(1, 0, 2)


def kernel(x, ffn1_pre_g, ffn1_post_g, ffn1_w_in, ffn1_w_out, mix_pre_g, mix_post_g, ffn2_pre_g, ffn2_post_g, ffn2_w_in, ffn2_w_out, conv_w_in, conv_k, conv_w_out, kv_g, kv_w, forget_b, attn_w_qg, attn_w_o, loss_target, m_ffn1_pre_g, m_ffn1_post_g, m_ffn1_w_in, m_ffn1_w_out, m_mix_pre_g, m_mix_post_g, m_ffn2_pre_g, m_ffn2_post_g, m_ffn2_w_in, m_ffn2_w_out, m_conv_w_in, m_conv_k, m_conv_w_out, m_kv_g, m_kv_w, m_forget_b, m_attn_w_qg, m_attn_w_o, v_ffn1_pre_g, v_ffn1_post_g, v_ffn1_w_in, v_ffn1_w_out, v_mix_pre_g, v_mix_post_g, v_ffn2_pre_g, v_ffn2_post_g, v_ffn2_w_in, v_ffn2_w_out, v_conv_w_in, v_conv_k, v_conv_w_out, v_kv_g, v_kv_w, v_forget_b, v_attn_w_qg, v_attn_w_o):
    n_seq, seq_len, _ = x.shape
    T = n_seq * seq_len
    xi, yi, ci = _place()
    dev = 4 * xi + 2 * yi + ci
    x0 = x.reshape(T, D)
    target = loss_target.reshape(T, D)

    me = dev.reshape(1).astype(jnp.int32)
    w1_t, w2_t, kv_t = ffn1_w_in.transpose(0, 2, 1), ffn2_w_in.transpose(0, 2, 1), kv_w.T[None]

    def zones(specs, after):
        return [(None, cast_place(w, l, me, dt, after, f"place_{nm}")) for nm, w, l, dt in specs]

    gathers, token = exchange_start([zones([("w1_in0", w1_t, 0, BF16)], me)], "gather_like", "gather_start0")
    shard_groups = [
        [("w1_out0", ffn1_w_out, 0, BF16)],
        [("cw_in", conv_w_in, 0, BF16), ("cw_out", conv_w_out, 0, BF16), ("ck", conv_k, 0, F32)],
        [("w2_in0", w2_t, 0, BF16), ("w2_out0", ffn2_w_out, 0, BF16)],
        [("kv", kv_t, 0, BF16), ("w1_in1", w1_t, 1, BF16), ("w1_out1", ffn1_w_out, 1, BF16),
         ("qg", attn_w_qg, 0, BF16), ("ow", attn_w_o, 0, BF16)],
        [("w2_in1", w2_t, 1, BF16), ("w2_out1", ffn2_w_out, 1, BF16)]]
    later, token = exchange_start([zones(g, token) for g in shard_groups], "gather_like", "gather_start1")
    gathers = gathers + later

    def gathered(k, after):
        passed, _ = exchange_relay([gathers[k]], "gather_pass", [after], f"gather_pass{k}")
        return [z[:, None] for z in exchange_wait(passed, [after], f"gather_wait{k}")[0][1]]

    fb = jnp.pad(forget_b, (0, LANES - N_HEADS))[None]

    def vec(g, l):
        return g[l:l + 1]

    def behind(g, tok):
        return g + tok[:1, :1]

    grads_small = {}
    tm_ffn = _tile(T, TM // 2, 16)

    def ffn_fwd(xin, g_pre, g_post, w_in, w_out, tag):
        w_gu = w_in.reshape(2, 1, -1, D)
        xn, (gu,) = rms_proj(xin, g_pre, [(w_gu, 0)], [BF16], f"{tag}_in", transposed=True)
        specs = [pl.BlockSpec((2, tm_ffn, gu.shape[2]), lambda i: (0, i, 0))]
        if callable(w_out):
            w_out = w_out(xn)
        a, h, y = mix_out(_swiglu_pro, [gu], specs, (w_out, 0), xin, g_post, 0.5, 1, f"{tag}_out", tm=tm_ffn)
        return y, (xin, xn, gu, a, h), w_out

    def ffn_bwd(dy, saved, g_pre, g_post, w_in, w_out, tag):
        xin, xn, gu, a, h = saved
        w_gu = w_in.reshape(2, 1, -1, D)
        half = gu.shape[2] // 2
        dh, dg_post, dgu = post_bwd(dy, h, g_post, 0.5, (w_out, 0), 1, f"{tag}_bwd_out", gu=gu, tm=tm_ffn)
        dw_out = wgrad([a], [dh[None]], 2, (2, 1, half, D), f"{tag}_dw_out", a_cols=half)
        started_out, tok = scatter_start([dw_out.reshape(8, -1, D)], f"{tag}_out_scatter_start")
        dw_in = wgrad([dgu], [xn[None]], 4, (4, 1, half, D), f"{tag}_dw_in", a_cols=half, after=tok)
        started_in, tok = scatter_start([dw_in.reshape(8, -1, D)], f"{tag}_in_scatter_start")
        dx, dg_pre = pre_bwd([(dgu, (w_gu, 0), 0)], xin, behind(g_pre, tok), dy, f"{tag}_bwd_in", transposed=True)
        return dx, dg_pre, dg_post, [started_in, started_out]

    def scatter_start(blocked, name):
        pairs = [(g, lax.empty((N_DEV - 1,) + g.shape[1:], g.dtype)) for g in blocked]
        started, tok = exchange_start([pairs], "scatter", name)
        return started[0], tok

    (w1_in,) = gathered(0, token)
    x1, s_f1a, w1_out = ffn_fwd(x0, vec(ffn1_pre_g, 0), vec(ffn1_post_g, 0), w1_in, lambda after: gathered(1, after)[0],
                                "l0_ffn1")
    cw_in_g, cw_out, ck_g = gathered(2, x1)
    cw_in = _col_blocks(cw_in_g[:, 0], 3)
    ck = ck_g[:, 0].transpose(1, 0, 2).reshape(3, D)
    xn_c, (bch,) = rms_proj(x1, vec(mix_pre_g, 0), [(cw_in, 0)], [BF16], "conv_in")
    tmc = _tile(seq_len, TM, 16)
    hb = tmc // HALO

    def cpiece(p):
        return pl.BlockSpec((1, tmc, D), lambda i, p=p: (p, i, 0))

    def chalo(p):
        return pl.BlockSpec((1, HALO, D), lambda i, p=p: (p, jnp.maximum(i * hb - 1, 0), 0))

    conv_specs = [cpiece(0), cpiece(1), cpiece(2), chalo(1), chalo(2), pl.BlockSpec((3, D), lambda i: (0, 0))]
    z_c, m_c, x2 = mix_out(_make_conv_pro(seq_len // tmc), [bch, bch, bch, bch, bch, ck], conv_specs, (cw_out, 0),
                           x1, vec(mix_post_g, 0), 1.0, 1, "conv_out", tm=tmc)
    w2_in, w2_out = gathered(3, x2)
    x3, s_f2a, _ = ffn_fwd(x2, vec(ffn2_pre_g, 0), vec(ffn2_post_g, 0), w2_in, w2_out, "l0_ffn2")

    kvw_g, w1_in_b, w1_out_b, qgw_g, ow = gathered(4, x3)
    qg_w = _col_blocks(qgw_g[:, 0], 2)
    kv_whole = kvw_g.reshape(2 * D + N_HEADS, D).T
    kv_wb = kv_whole[:, :2 * D].reshape(D, 2, D).transpose(1, 0, 2)[:, None]
    f_w = jnp.pad(kv_whole[:, 2 * D:], ((0, 0), (0, LANES - N_HEADS)))[None, None]
    xn_kv, (kv, fl) = rms_proj(x3, kv_g[None], [(kv_wb, 0), (f_w, 0)], [BF16, F32], "kv_in")
    c128 = forget_fwd(fl, fb, seq_len, "forget_fwd")
    cb = jnp.repeat(c128[:, :N_HEADS], HEAD_DIM, axis=1)

    x4, s_f1b, _ = ffn_fwd(x3, vec(ffn1_pre_g, 1), vec(ffn1_post_g, 1), w1_in_b, w1_out_b, "l1_ffn1")
    xn_a, (qg,) = rms_proj(x4, vec(mix_pre_g, 1), [(qg_w, 0)], [BF16], "attn_in")
    o, lse = attn_fwd(qg, kv, cb, n_seq, seq_len, "attn_fwd")
    tm = _tile(T, TM, 16)
    gate_specs = [pl.BlockSpec((1, tm, D), lambda i: (1, i, 0)), pl.BlockSpec((tm, D), lambda i: (i, 0))]
    z_a, m_a, x5 = mix_out(_attn_gate_pro, [qg, o], gate_specs, (ow, 0), x4, vec(mix_post_g, 1), 1.0, 1, "attn_out")
    w2_in_b, w2_out_b = gathered(5, x5)
    x6, s_f2b, _ = ffn_fwd(x5, vec(ffn2_pre_g, 1), vec(ffn2_post_g, 1), w2_in_b, w2_out_b, "l1_ffn2")

    dy, loss_part = loss_head(x6, target, "loss_head")

    scatters = {}
    dx5, dg, dgp, scatters["ffn2", 1] = ffn_bwd(dy, s_f2b, vec(ffn2_pre_g, 1), vec(ffn2_post_g, 1), w2_in_b, w2_out_b, "l1_ffn2")
    grads_small["ffn2_pre", 1], grads_small["ffn2_post", 1] = dg, dgp
    dm_a, dgp, dz_a = post_bwd(dx5, m_a, vec(mix_post_g, 1), 1.0, (ow, 0), 1, "attn_bwd_out")
    grads_small["mix_post", 1] = dgp
    dq, dgate, dk, dv, dcb = attn_bwd(dz_a[0], qg, kv, o, lse, cb, n_seq, seq_len, "attn_bwd")
    dx4, dg = pre_bwd([(dq, (qg_w, 0), 0), (dgate, (qg_w, 0), 1)], x4, vec(mix_pre_g, 1), dx5, "attn_bwd_in")
    grads_small["mix_pre", 1] = dg
    d_ow = wgrad([z_a], [dm_a[None]], 1, ow.shape, "attn_dw_o")
    d_qgw = wgrad([xn_a[None]], [dq, dgate], 2, (2, 1, D, D), "attn_dw_qg")
    scatters["attn"], tok = scatter_start([_col_shards(d_qgw[:, 0]), d_ow.reshape(8, -1, D)], "attn_scatter_start")
    dx3, dg, dgp, scatters["ffn1", 1] = ffn_bwd(dx4, s_f1b, vec(ffn1_pre_g, 1), behind(vec(ffn1_post_g, 1), tok),
                                                w1_in_b, w1_out_b, "l1_ffn1")
    grads_small["ffn1_pre", 1], grads_small["ffn1_post", 1] = dg, dgp

    dc16 = dcb.reshape(T, N_HEADS, HEAD_DIM)[:, :, 0]
    dfl, dfb = forget_bwd(jnp.pad(dc16, ((0, 0), (0, LANES - N_HEADS))), fl, fb, seq_len, "forget_bwd")
    dx3, dg_kv = pre_bwd([(dk, (kv_wb, 0), 0), (dv, (kv_wb, 0), 1), (dfl, (f_w, 0), 0)], x3, kv_g[None], dx3, "kv_bwd_in")
    d_kvw = wgrad([xn_kv[None]], [dk, dv], 2, (2, 1, D, D), "kv_dw")
    d_fw = wgrad([xn_kv[None]], [dfl.astype(BF16)], 1, (1, 1, D, LANES), "forget_dw")
    d_kv_whole = jnp.concatenate([d_kvw[0, 0], d_kvw[1, 0], d_fw[0, 0, :, :N_HEADS]], axis=1)
    wshard = D * 2 + N_HEADS
    scatters["kv"], tok = scatter_start([d_kv_whole.T.reshape(N_DEV, wshard // N_DEV, D)], "kv_scatter_start")

    dx2, dg, dgp, scatters["ffn2", 0] = ffn_bwd(dx3, s_f2a, vec(ffn2_pre_g, 0), behind(vec(ffn2_post_g, 0), tok),
                                                w2_in, w2_out, "l0_ffn2")
    grads_small["ffn2_pre", 0], grads_small["ffn2_post", 0] = dg, dgp
    dm_c, dgp, dz_c = post_bwd(dx2, m_c, vec(mix_post_g, 0), 1.0, (cw_out, 0), 1, "conv_bwd_out")
    grads_small["mix_post", 0] = dgp
    dbch, d_ck = conv_bwd_mix(dz_c, bch, ck, seq_len, "conv_bwd_mix")
    dx1, dg = pre_bwd([(dbch, (cw_in, 0), 0)], x1, vec(mix_pre_g, 0), dx2, "conv_bwd_in")
    grads_small["mix_pre", 0] = dg
    d_cw_out = wgrad([z_c], [dm_c[None]], 1, cw_out.shape, "conv_dw_out")
    d_cw_in = wgrad([xn_c[None]], [dbch], 3, (3, 1, D, D), "conv_dw_in")
    scatters["conv"], tok = scatter_start([_col_shards(d_cw_in[:, 0]), d_cw_out.reshape(8, -1, D)], "conv_scatter_start")
    dx0, dg, dgp, scatters["ffn1", 0] = ffn_bwd(dx1, s_f1a, vec(ffn1_pre_g, 0), behind(vec(ffn1_post_g, 0), tok),
                                                w1_in, w1_out, "l0_ffn1")
    grads_small["ffn1_pre", 0], grads_small["ffn1_post", 0] = dg, dgp

    parts_of = {}

    def scatter_end(keys, afters, name):
        flat = [(k, g) for k in keys for g in (scatters[k] if isinstance(scatters[k], list) else [scatters[k]])]
        for (k, _), (sent, recv) in zip(flat, exchange_wait([g for _, g in flat], afters, name)):
            parts_of.setdefault(k, []).extend(zip(sent, recv))

    scatter_end([("ffn2", 1), "attn", ("ffn1", 1), "kv", ("ffn2", 0), "conv"], [dx0], "scatter_wait")
    sharded = {"ffn2_w_in": (ffn2_w_in, m_ffn2_w_in, v_ffn2_w_in), "ffn2_w_out": (ffn2_w_out, m_ffn2_w_out, v_ffn2_w_out),
               "conv_w_in": (conv_w_in, m_conv_w_in, v_conv_w_in), "conv_w_out": (conv_w_out, m_conv_w_out, v_conv_w_out),
               "kv_w": (kv_w, m_kv_w, v_kv_w), "attn_w_qg": (attn_w_qg, m_attn_w_qg, v_attn_w_qg),
               "attn_w_o": (attn_w_o, m_attn_w_o, v_attn_w_o),
               "ffn1_w_in": (ffn1_w_in, m_ffn1_w_in, v_ffn1_w_in), "ffn1_w_out": (ffn1_w_out, m_ffn1_w_out, v_ffn1_w_out)}
    out = {}
    for nm, (w, mm, vv) in sharded.items():
        if nm == "ffn1_w_in":
            scatter_end([("ffn1", 0)], [res[0] for res in out.values()], "scatter_wait_last")
        contribs = {
            "ffn1_w_in": lambda: [parts_of["ffn1", 0][0], parts_of["ffn1", 1][0]],
            "ffn1_w_out": lambda: [parts_of["ffn1", 0][1], parts_of["ffn1", 1][1]],
            "ffn2_w_in": lambda: [parts_of["ffn2", 0][0], parts_of["ffn2", 1][0]],
            "ffn2_w_out": lambda: [parts_of["ffn2", 0][1], parts_of["ffn2", 1][1]],
            "conv_w_in": lambda: [parts_of["conv"][0]], "conv_w_out": lambda: [parts_of["conv"][1]],
            "kv_w": lambda: [parts_of["kv"][0]], "attn_w_qg": lambda: [parts_of["attn"][0]],
            "attn_w_o": lambda: [parts_of["attn"][1]]}[nm]()
        if nm in ("ffn1_w_in", "ffn2_w_in", "kv_w"):
            rows, cols = w.shape[-2:]

            def view(a):
                return a.reshape(-1, rows, cols).transpose(0, 2, 1)

            res = adam_sharded(view(w), view(mm), view(vv), contribs, me, f"adam_{nm}")
            out[nm] = [r.transpose(0, 2, 1).reshape(w.shape) for r in res]
        else:
            shape3 = (len(contribs),) + contribs[0][0].shape[1:]
            res = adam_sharded(w.reshape(shape3), mm.reshape(shape3), vv.reshape(shape3), contribs, me, f"adam_{nm}")
            out[nm] = [r.reshape(w.shape) for r in res]

    small_names = ["ffn1_pre", "ffn1_post", "mix_pre", "mix_post", "ffn2_pre", "ffn2_post"]
    parts = [(grads_small[n, l], 2 * k + l) for k, n in enumerate(small_names) for l in range(2)]
    parts += [(dg_kv, 12), (dfb, 13), (d_ck, 14), (loss_part, 17)]
    total = all_reduce_small(parts, 24, out["ffn1_w_in"][0], "all_reduce_small")
    loss = total[17, 0]
    d_ck_mine = lax.dynamic_slice(total, (14, dev * LANES), (3, LANES))
    gains = [(ffn1_pre_g, m_ffn1_pre_g, v_ffn1_pre_g), (ffn1_post_g, m_ffn1_post_g, v_ffn1_post_g),
             (mix_pre_g, m_mix_pre_g, v_mix_pre_g), (mix_post_g, m_mix_post_g, v_mix_post_g),
             (ffn2_pre_g, m_ffn2_pre_g, v_ffn2_pre_g), (ffn2_post_g, m_ffn2_post_g, v_ffn2_post_g)]
    small_params = [(w, m, v, (2 * k, 2, D)) for k, (w, m, v) in enumerate(gains)]
    small_params += [(kv_g[None], m_kv_g[None], v_kv_g[None], (12, 1, D)),
                     (forget_b[None], m_forget_b[None], v_forget_b[None], (13, 1, N_HEADS)),
                     (conv_k[0], m_conv_k[0], v_conv_k[0], 0)]
    small_res = adam_small(small_params, total, [d_ck_mine], "adam_small")
    small_keys = [n + "_g" for n in small_names] + ["kv_g", "forget_b", "conv_k"]
    shapes = {"kv_g": kv_g.shape, "forget_b": forget_b.shape, "conv_k": conv_k.shape}
    small = [{key: res[kind].reshape(shapes.get(key, res[kind].shape)) for key, res in zip(small_keys, small_res)}
             for kind in range(4)]
    order = ["ffn1_pre_g", "ffn1_post_g", "ffn1_w_in", "ffn1_w_out", "mix_pre_g", "mix_post_g", "ffn2_pre_g", "ffn2_post_g",
             "ffn2_w_in", "ffn2_w_out", "conv_w_in", "conv_k", "conv_w_out", "kv_g", "kv_w", "forget_b", "attn_w_qg",
             "attn_w_o"]
    results = [loss, dx0.reshape(x.shape)]
    for kind in range(4):
        for nm in order:
            results.append(out[nm][kind] if nm in out else small[kind][nm])
    return tuple(results)
```

```python
import functools
import math

import jax
import jax.numpy as jnp
from jax import lax
from jax.experimental import pallas as pl
from jax.experimental.pallas import tpu as pltpu

F32, BF16 = jnp.float32, jnp.bfloat16
D = 1024
N_HEADS = 16
HEAD_DIM = 64
N_DEV = 8
RMS_EPS = 1e-6
ATT_SCALE = 1.0 / math.sqrt(HEAD_DIM)
LANES = 128
HALO = 8
TM = 512
TQ = 512
VMEM_LIMIT = 48 * 1024 * 1024
MESH = pl.DeviceIdType.MESH

ADAM_LR, ADAM_B1, ADAM_B2, ADAM_EPS, ADAM_WD, ADAM_STEP = 0.001, 0.9, 0.999, 1e-08, 0.01, 10

NT = (((1,), (1,)), ((), ()))
TN = (((0,), (0,)), ((), ()))


def _params(n_axes, vmem_limit=VMEM_LIMIT):
    return pltpu.CompilerParams(dimension_semantics=("arbitrary",) * n_axes, vmem_limit_bytes=vmem_limit)


def _tile(n, cap, mult):
    best = None
    for t in range(mult, min(n, cap) + 1, mult):
        if n % t == 0:
            best = t
    assert best is not None, (n, cap, mult)
    return best


def _rms_rstd(x):
    return lax.rsqrt(jnp.mean(x * x, axis=-1, keepdims=True) + RMS_EPS)


def _rms_fwd(x, g):
    return x * _rms_rstd(x) * g


def _rms_bwd(x, g, dy):
    xh = x * _rms_rstd(x)
    dyg = dy * g
    dx = _rms_rstd(x) * (dyg - xh * jnp.mean(dyg * xh, axis=-1, keepdims=True))
    return dx, jnp.sum(dy * xh, axis=0, keepdims=True)


def _accumulate(ref, first, value):
    @pl.when(first)
    def _():
        ref[...] = value

    @pl.when(jnp.logical_not(first))
    def _():
        ref[...] += value


def rms_proj(x, g, ws, out_dtypes, name, transposed=False):
    T = x.shape[0]
    tm = _tile(T, TM, 16)
    na = len(ws)

    def body(x_ref, g_ref, *refs):
        w_refs, xn_ref, o_refs = refs[:na], refs[na], refs[na + 1:]
        xn = _rms_fwd(x_ref[...], g_ref[...]).astype(BF16)
        xn_ref[...] = xn
        for a, (w, l) in enumerate(ws):
            for p in range(w.shape[0]):
                if transposed:
                    y = lax.dot_general(xn, w_refs[a][p, l], NT, preferred_element_type=F32)
                else:
                    y = jnp.dot(xn, w_refs[a][p, l], preferred_element_type=F32)
                o_refs[a][p] = y.astype(o_refs[a].dtype)

    in_specs = [pl.BlockSpec((tm, D), lambda i: (i, 0)), pl.BlockSpec((1, D), lambda i: (0, 0))]
    in_specs += [pl.BlockSpec(w.shape, lambda i: (0, 0, 0, 0), pipeline_mode=pl.Buffered(1)) for w, _ in ws]
    out_specs = [pl.BlockSpec((tm, D), lambda i: (i, 0))]
    out_shape = [jax.ShapeDtypeStruct((T, D), BF16)]
    for (w, _), dt in zip(ws, out_dtypes):
        nb, wb = w.shape[0], w.shape[2 if transposed else 3]
        out_specs.append(pl.BlockSpec((nb, tm, wb), lambda i: (0, i, 0)))
        out_shape.append(jax.ShapeDtypeStruct((nb, T, wb), dt))
    res = pl.pallas_call(
        body, name=name, grid=(T // tm,), in_specs=in_specs, out_specs=out_specs, out_shape=out_shape,
        compiler_params=_params(1),
    )(x, g, *[w for w, _ in ws])
    return res[0], res[1:]


def mix_out(pro, pro_inputs, pro_specs, w, res, g_post, alpha, nk, name, tm=None):
    w4, l = w
    T = res.shape[0]
    tm = _tile(T, TM, 16) if tm is None else tm
    dpb = N_DEV // nk
    rows = w4.shape[2]
    kb = dpb * rows
    npi = len(pro_inputs)

    def body(*refs):
        pro_refs = refs[:npi]
        w_ref, res_ref, g_ref, z_ref, m_ref, y_ref = refs[npi:]
        i = pl.program_id(0)
        m = None
        for k in range(nk):
            z = pro(i, k, *pro_refs).astype(BF16)
            z_ref[k] = z
            part = jnp.dot(z, w_ref[k * dpb:(k + 1) * dpb, l].reshape(kb, D), preferred_element_type=F32)
            m = part if m is None else m + part
        m_ref[...] = m
        y_ref[...] = res_ref[...] + alpha * _rms_fwd(m, g_ref[...])

    row = pl.BlockSpec((tm, D), lambda i: (i, 0))
    in_specs = list(pro_specs) + [
        pl.BlockSpec(w4.shape, lambda i: (0, 0, 0, 0), pipeline_mode=pl.Buffered(1)), row, pl.BlockSpec((1, D), lambda i: (0, 0))]
    z, m, y = pl.pallas_call(
        body, name=name, grid=(T // tm,), in_specs=in_specs,
        out_specs=[pl.BlockSpec((nk, tm, kb), lambda i: (0, i, 0)), row, row],
        out_shape=[jax.ShapeDtypeStruct((nk, T, kb), BF16), jax.ShapeDtypeStruct((T, D), F32),
                   jax.ShapeDtypeStruct((T, D), F32)],
        compiler_params=_params(1),
    )(*pro_inputs, w4, res, g_post)
    return z, m, y


def post_bwd(dy, m, g_post, alpha, w, nk, name, gu=None, tm=None):
    w4, l = w
    T = dy.shape[0]
    tm = _tile(T, TM, 16) if tm is None else tm
    dpb = N_DEV // nk
    rows = w4.shape[2]
    kb = dpb * rows
    swiglu = gu is not None

    def body(dy_ref, m_ref, g_ref, w_ref, *refs):
        if swiglu:
            gu_ref, dm_ref, dg_ref, dgu_ref = refs
        else:
            dm_ref, dg_ref, dz_ref = refs
        dm, dg = _rms_bwd(m_ref[...], g_ref[...], alpha * dy_ref[...])
        dm = dm.astype(BF16)
        dm_ref[...] = dm
        _accumulate(dg_ref, pl.program_id(0) == 0, dg)
        for k in range(nk):
            dz = lax.dot_general(dm, w_ref[k * dpb:(k + 1) * dpb, l].reshape(kb, D), NT, preferred_element_type=F32)
            if swiglu:
                gate, up = gu_ref[k].astype(F32), gu_ref[k + nk].astype(F32)
                sig = jax.nn.sigmoid(gate)
                dgu_ref[k] = (dz * up * sig * (1.0 + gate * (1.0 - sig))).astype(BF16)
                dgu_ref[k + nk] = (dz * gate * sig).astype(BF16)
            else:
                dz_ref[k] = dz.astype(BF16)

    row = pl.BlockSpec((tm, D), lambda i: (i, 0))
    vec = pl.BlockSpec((1, D), lambda i: (0, 0))
    blkk = pl.BlockSpec((nk, tm, kb), lambda i: (0, i, 0))
    in_specs = [row, row, vec, pl.BlockSpec(w4.shape, lambda i: (0, 0, 0, 0), pipeline_mode=pl.Buffered(1))]
    inputs = [dy, m, g_post, w4]
    n_dz = 2 * nk if swiglu else nk
    out_specs = [row, vec, pl.BlockSpec((n_dz, tm, kb), lambda i: (0, i, 0))]
    out_shape = [jax.ShapeDtypeStruct((T, D), BF16), jax.ShapeDtypeStruct((1, D), F32),
                 jax.ShapeDtypeStruct((n_dz, T, kb), BF16)]
    if swiglu:
        in_specs.append(pl.BlockSpec((2 * nk, tm, kb), lambda i: (0, i, 0)))
        inputs.append(gu)
    return pl.pallas_call(
        body, name=name, grid=(T // tm,), in_specs=in_specs, out_specs=out_specs, out_shape=out_shape,
        compiler_params=_params(1),
    )(*inputs)


def pre_bwd(pieces, x, g_pre, dres, name, transposed=False):
    T = x.shape[0]
    tm = _tile(T, TM, 16)
    na = len(pieces)
    weights = []
    for _, (w4, _), _ in pieces:
        if not any(w4 is w for w in weights):
            weights.append(w4)
    nw = len(weights)
    which = [[w4 is w for w in weights].index(True) for _, (w4, _), _ in pieces]

    def body(*refs):
        dz_refs, w_refs = refs[:na], refs[na:na + nw]
        x_ref, g_ref, dres_ref, dx_ref, dg_ref = refs[na + nw:]
        acc = None
        for a, (dz, (_, l), w_off) in enumerate(pieces):
            for p in range(dz.shape[0]):
                w = w_refs[which[a]][w_off + p, l]
                if transposed:
                    part = jnp.dot(dz_refs[a][p].astype(BF16), w, preferred_element_type=F32)
                else:
                    part = lax.dot_general(dz_refs[a][p].astype(BF16), w, NT, preferred_element_type=F32)
                acc = part if acc is None else acc + part
        dx, dg = _rms_bwd(x_ref[...], g_ref[...], acc)
        dx_ref[...] = dres_ref[...] + dx
        _accumulate(dg_ref, pl.program_id(0) == 0, dg)

    row = pl.BlockSpec((tm, D), lambda i: (i, 0))
    vec = pl.BlockSpec((1, D), lambda i: (0, 0))
    dz_specs = [pl.BlockSpec((dz.shape[0], tm, dz.shape[2]), lambda i: (0, i, 0)) for dz, _, _ in pieces]
    w_specs = [pl.BlockSpec(w.shape, lambda i: (0, 0, 0, 0), pipeline_mode=pl.Buffered(1)) for w in weights]
    return pl.pallas_call(
        body, name=name, grid=(T // tm,), in_specs=dz_specs + w_specs + [row, vec, row],
        out_specs=[row, vec], out_shape=[jax.ShapeDtypeStruct((T, D), F32), jax.ShapeDtypeStruct((1, D), F32)],
        compiler_params=_params(1),
    )(*[p[0] for p in pieces], *weights, x, g_pre, dres)


def wgrad(a_list, b_list, nout, out4_shape, name, a_cols=None, tm_cap=None, after=None):
    T = a_list[0].shape[1]
    tm = _tile(T, 4 * TM if tm_cap is None else tm_cap, 16)
    nt = T // tm
    dpb = out4_shape[0] // nout
    _, _, R, C = out4_shape
    na, nb = len(a_list), len(b_list)
    a_w = a_list[0].shape[2] if a_cols is None else a_cols
    a_per = a_list[0].shape[2] // a_w

    def spans(arrs, per):
        ns = [a.shape[0] * per for a in arrs]
        return ns, [sum(ns[:k]) for k in range(len(ns))], sum(ns)

    a_ns, a_offs, a_tot = spans(a_list, a_per)
    b_ns, b_offs, b_tot = spans(b_list, 1)
    assert a_tot in (1, nout) and b_tot in (1, nout)

    def body(*refs):
        a_refs, b_refs = refs[:na], refs[na:na + nb]
        out_ref, acc = refs[-2:]
        p, t = pl.program_id(0), pl.program_id(1)
        for ia in range(na):
            for ib in range(nb):
                conds = []
                if a_tot > 1:
                    conds += [p >= a_offs[ia], p < a_offs[ia] + a_ns[ia]]
                if b_tot > 1:
                    conds += [p >= b_offs[ib], p < b_offs[ib] + b_ns[ib]]

                def work(ia=ia, ib=ib):
                    part = lax.dot_general(a_refs[ia][0], b_refs[ib][0], TN, preferred_element_type=F32)
                    _accumulate(acc, t == 0, part)

                if conds:
                    pl.when(functools.reduce(jnp.logical_and, conds))(work)
                else:
                    work()

        @pl.when(t == nt - 1)
        def _():
            out_ref[...] = acc[...].astype(BF16).reshape(dpb, 1, R, C)

    def blk(off, n, tot):
        if tot == 1:
            return lambda p: 0
        return lambda p: jnp.clip(p - off, 0, n - 1)

    in_specs = []
    for arr, n, off in zip(a_list, a_ns, a_offs):
        in_specs.append(pl.BlockSpec((1, tm, a_w), lambda p, t, f=blk(off, n, a_tot): (f(p) // a_per, t, f(p) % a_per)))
    for arr, n, off in zip(b_list, b_ns, b_offs):
        in_specs.append(pl.BlockSpec((1, tm, arr.shape[2]), lambda p, t, f=blk(off, n, b_tot): (f(p), t, 0)))
    extra = [] if after is None else [after]
    return pl.pallas_call(
        body, name=name, grid=(nout, nt), in_specs=in_specs + [ANY] * len(extra),
        out_specs=pl.BlockSpec((dpb, 1, R, C), lambda p, t: (p, 0, 0, 0)),
        out_shape=jax.ShapeDtypeStruct(out4_shape, BF16),
        scratch_shapes=[pltpu.VMEM((dpb * R, C), F32)], compiler_params=_params(2),
    )(*a_list, *b_list, *extra)


def _swiglu_pro(i, k, gu_ref):
    gate, up = gu_ref[k].astype(F32), gu_ref[k + gu_ref.shape[0] // 2].astype(F32)
    return gate * jax.nn.sigmoid(gate) * up


def _attn_gate_pro(i, k, gate_ref, o_ref):
    return jax.nn.sigmoid(gate_ref[0].astype(F32)) * o_ref[...].astype(F32)


def _shift_rows(u, halo, d):
    rolled = pltpu.roll(u, d, 0)
    row = lax.broadcasted_iota(jnp.int32, u.shape, 0)
    for r in range(d):
        rolled = jnp.where(row == r, halo[HALO - d + r:HALO - d + r + 1, :], rolled)
    return rolled


def _advance_rows(u, halo, d):
    n = u.shape[0]
    rolled = pltpu.roll(u, n - d, 0)
    row = lax.broadcasted_iota(jnp.int32, u.shape, 0)
    for r in range(d):
        rolled = jnp.where(row == n - d + r, halo[r:r + 1, :], rolled)
    return rolled


def _make_conv_pro(tiles_per_seq):
    def pro(i, k, b_ref, c_ref, h_ref, ch_ref, hh_ref, ck_ref):
        u = c_ref[0].astype(F32) * h_ref[0].astype(F32)
        first = (i % tiles_per_seq) == 0
        halo = jnp.where(first, 0.0, ch_ref[0].astype(F32) * hh_ref[0].astype(F32))
        ck = ck_ref[...]
        conv = ck[2:3, :] * u + ck[1:2, :] * _shift_rows(u, halo, 1) + ck[0:1, :] * _shift_rows(u, halo, 2)
        return b_ref[0].astype(F32) * conv
    return pro


def conv_bwd_mix(dz, bch, conv_k, seq_len, name):
    T = dz.shape[1]
    tm = _tile(seq_len, TM, 16)
    tps = seq_len // tm
    nt = T // tm
    hb = tm // HALO

    def body(dz_ref, b_ref, c_ref, h_ref, cp_ref, hp_ref, dzn_ref, bn_ref, ck_ref, dbch_ref, dk_ref):
        i = pl.program_id(0)
        first = (i % tps) == 0
        last = (i % tps) == tps - 1
        b, c, h = b_ref[0].astype(F32), c_ref[0].astype(F32), h_ref[0].astype(F32)
        dzt = dz_ref[0].astype(F32)
        u = c * h
        prev = jnp.where(first, 0.0, cp_ref[0].astype(F32) * hp_ref[0].astype(F32))
        u1, u2 = _shift_rows(u, prev, 1), _shift_rows(u, prev, 2)
        ck = ck_ref[...]
        conv = ck[2:3, :] * u + ck[1:2, :] * u1 + ck[0:1, :] * u2
        dconv = dzt * b
        nxt = jnp.where(last, 0.0, dzn_ref[0].astype(F32) * bn_ref[0].astype(F32))
        du = ck[2:3, :] * dconv + ck[1:2, :] * _advance_rows(dconv, nxt, 1) + ck[0:1, :] * _advance_rows(dconv, nxt, 2)
        dbch_ref[0] = (dzt * conv).astype(BF16)
        dbch_ref[1] = (du * h).astype(BF16)
        dbch_ref[2] = (du * c).astype(BF16)
        tap = lax.broadcasted_iota(jnp.int32, (3, D), 0)
        dk = jnp.where(tap == 0, jnp.sum(dconv * u2, axis=0, keepdims=True),
                       jnp.where(tap == 1, jnp.sum(dconv * u1, axis=0, keepdims=True),
                                 jnp.sum(dconv * u, axis=0, keepdims=True)))
        _accumulate(dk_ref, i == 0, dk)

    def piece(p):
        return pl.BlockSpec((1, tm, D), lambda i, p=p: (p, i, 0))

    def prev(p):
        return pl.BlockSpec((1, HALO, D), lambda i, p=p: (p, jnp.maximum(i * hb - 1, 0), 0))

    def nxt(p):
        return pl.BlockSpec((1, HALO, D), lambda i, p=p: (p, jnp.minimum((i + 1) * hb, nt * hb - 1), 0))

    return pl.pallas_call(
        body, name=name, grid=(nt,),
        in_specs=[piece(0), piece(0), piece(1), piece(2), prev(1), prev(2), nxt(0), nxt(0),
                  pl.BlockSpec((3, D), lambda i: (0, 0))],
        out_specs=[pl.BlockSpec((3, tm, D), lambda i: (0, i, 0)), pl.BlockSpec((3, D), lambda i: (0, 0))],
        out_shape=[jax.ShapeDtypeStruct((3, T, D), BF16), jax.ShapeDtypeStruct((3, D), F32)],
        compiler_params=_params(1),
    )(dz, bch, bch, bch, bch, bch, dz, bch, conv_k)


def _log_sigmoid(x):
    return jnp.minimum(x, 0.0) - jnp.log(1.0 + jnp.exp(-jnp.abs(x)))


def forget_fwd(fl, fb, seq_len, name):
    T = fl.shape[1]

    def body(fl_ref, fb_ref, c_ref):
        c = _log_sigmoid(fl_ref[0] + fb_ref[...])
        row = lax.broadcasted_iota(jnp.int32, c.shape, 0)
        k = 1
        while k < seq_len:
            c = c + jnp.where(row >= k, pltpu.roll(c, k, 0), 0.0)
            k *= 2
        c_ref[...] = c

    return pl.pallas_call(
        body, name=name, grid=(T // seq_len,),
        in_specs=[pl.BlockSpec((1, seq_len, LANES), lambda b: (0, b, 0)), pl.BlockSpec((1, LANES), lambda b: (0, 0))],
        out_specs=pl.BlockSpec((seq_len, LANES), lambda b: (b, 0)),
        out_shape=jax.ShapeDtypeStruct((T, LANES), F32), compiler_params=_params(1),
    )(fl, fb)


def forget_bwd(dc, fl, fb, seq_len, name):
    T = dc.shape[0]

    def body(dc_ref, fl_ref, fb_ref, dfl_ref, dfb_ref):
        b = pl.program_id(0)
        r = dc_ref[...]
        row = lax.broadcasted_iota(jnp.int32, r.shape, 0)
        k = 1
        while k < seq_len:
            r = r + jnp.where(row < seq_len - k, pltpu.roll(r, seq_len - k, 0), 0.0)
            k *= 2
        dfl = r * jax.nn.sigmoid(-(fl_ref[0] + fb_ref[...]))
        dfl_ref[0] = dfl
        _accumulate(dfb_ref, b == 0, jnp.sum(dfl, axis=0, keepdims=True))

    return pl.pallas_call(
        body, name=name, grid=(T // seq_len,),
        in_specs=[pl.BlockSpec((seq_len, LANES), lambda b: (b, 0)), pl.BlockSpec((1, seq_len, LANES), lambda b: (0, b, 0)),
                  pl.BlockSpec((1, LANES), lambda b: (0, 0))],
        out_specs=[pl.BlockSpec((1, seq_len, LANES), lambda b: (0, b, 0)), pl.BlockSpec((1, LANES), lambda b: (0, 0))],
        out_shape=[jax.ShapeDtypeStruct((1, T, LANES), F32), jax.ShapeDtypeStruct((1, LANES), F32)],
        compiler_params=_params(1),
    )(dc, fl, fb)


HEADS = (slice(0, HEAD_DIM), slice(HEAD_DIM, 2 * HEAD_DIM))


def _key_blocks(i, step, carry):
    carry = lax.fori_loop(0, i // 2, lambda jj, c: step(2 * jj, c, False, 2), carry)
    return lax.cond(i % 2 == 1, lambda c: step(i - 1, c, True, 2), lambda c: step(i, c, True, 1), carry)


def _causal(width, tq):
    keys = lax.broadcasted_iota(jnp.int32, (width * tq, tq), 0)
    return keys <= lax.broadcasted_iota(jnp.int32, (width * tq, tq), 1) + (width - 1) * tq


def attn_fwd(qg, kv, cb, n_seq, seq_len, name):
    T = n_seq * seq_len
    tq = _tile(seq_len, TQ, LANES)
    nq = seq_len // tq

    def body(q_ref, k_ref, v_ref, cb_ref, o_ref, lse_ref):
        i = pl.program_id(2)
        q8 = [q_ref[0, :, sl] * ATT_SCALE for sl in HEADS]

        def block(j, carry, diagonal, width):
            rows = pl.ds(pl.multiple_of(j * tq, tq), width * tq)
            out = []
            for hh, sl in enumerate(HEADS):
                m, l, acc = carry[3 * hh:3 * hh + 3]
                s = lax.dot_general(k_ref[0, rows, sl], q8[hh], NT, preferred_element_type=F32)
                s = s - cb_ref[rows, sl.start:sl.start + 1]
                if diagonal:
                    s = jnp.where(_causal(width, tq), s, -1e30)
                m_new = jnp.maximum(m, jnp.max(s, axis=0, keepdims=True))
                a = jnp.exp(m - m_new)
                p = jnp.exp(s - m_new)
                l = a * l + jnp.sum(p, axis=0, keepdims=True)
                acc = a * acc + lax.dot_general(v_ref[0, rows, sl], p.astype(BF16), TN, preferred_element_type=F32)
                out += [m_new, l, acc]
            return tuple(out)

        init = (jnp.full((1, tq), -1e30, F32), jnp.zeros((1, tq), F32), jnp.zeros((HEAD_DIM, tq), F32)) * 2
        carry = _key_blocks(i, block, init)
        o_ref[...] = jnp.concatenate([carry[2] / carry[1], carry[5] / carry[4]], axis=0).T
        for hh in range(2):
            lse_ref[0, 0, 0, hh:hh + 1, :] = carry[3 * hh] + jnp.log(carry[3 * hh + 1])

    seq2 = pl.BlockSpec((seq_len, LANES), lambda b, hp, i: (b, hp))
    return pl.pallas_call(
        body, name=name, grid=(n_seq, N_HEADS // 2, nq),
        in_specs=[pl.BlockSpec((1, tq, LANES), lambda b, hp, i: (0, b * nq + i, hp)),
                  pl.BlockSpec((1, seq_len, LANES), lambda b, hp, i: (0, b, hp)),
                  pl.BlockSpec((1, seq_len, LANES), lambda b, hp, i: (1, b, hp)), seq2],
        out_specs=[pl.BlockSpec((tq, LANES), lambda b, hp, i: (b * nq + i, hp)),
                   pl.BlockSpec((1, 1, 1, 2, tq), lambda b, hp, i: (b, hp, i, 0, 0))],
        out_shape=[jax.ShapeDtypeStruct((T, D), F32), jax.ShapeDtypeStruct((n_seq, N_HEADS // 2, nq, 2, tq), F32)],
        compiler_params=_params(3),
    )(qg, kv, kv, cb)


def attn_bwd(dz, qg, kv, o, lse, cb, n_seq, seq_len, name):
    T = n_seq * seq_len
    tq = _tile(seq_len, TQ, LANES)
    nq = seq_len // tq

    def body(dz_ref, q_ref, gate_ref, o_ref, lse_ref, cb_ref, k_ref, v_ref,
             dq_ref, dgate_ref, dk_ref, dv_ref, dc_ref, p_s, dp_s, dk_s, dv_s, dc_s):
        i = pl.program_id(2)

        @pl.when(i == 0)
        def _():
            dk_s[...] = jnp.zeros_like(dk_s)
            dv_s[...] = jnp.zeros_like(dv_s)
            dc_s[...] = jnp.zeros_like(dc_s)

        dzf = dz_ref[...].astype(F32)
        sig = jax.nn.sigmoid(gate_ref[0].astype(F32))
        dob = (dzf * sig).astype(BF16)
        dgate_ref[0] = (dzf * o_ref[...] * sig * (1.0 - sig)).astype(BF16)
        q8 = [q_ref[0, :, sl] * ATT_SCALE for sl in HEADS]
        do = [dob[:, sl] for sl in HEADS]
        lse_i = [lse_ref[0, 0, 0, hh:hh + 1, :] for hh in range(2)]

        def probs(j, dsums, diagonal, width):
            rows = pl.ds(pl.multiple_of(j * tq, tq), width * tq)
            out = []
            for hh, sl in enumerate(HEADS):
                s = lax.dot_general(k_ref[0, rows, sl], q8[hh], NT, preferred_element_type=F32)
                p = jnp.exp(s - cb_ref[rows, sl.start:sl.start + 1] - lse_i[hh])
                if diagonal:
                    p = jnp.where(_causal(width, tq), p, 0.0)
                dp = lax.dot_general(v_ref[0, rows, sl], do[hh], NT, preferred_element_type=F32)
                p_s[hh, rows, :] = p
                dp_s[hh, rows, :] = dp
                out.append(dsums[hh] + jnp.sum(p * dp, axis=0, keepdims=True))
            return tuple(out)

        dsums = _key_blocks(i, probs, (jnp.zeros((1, tq), F32),) * 2)

        def grads(j, dqs, diagonal, width):
            rows = pl.ds(pl.multiple_of(j * tq, tq), width * tq)
            out = []
            for hh, sl in enumerate(HEADS):
                p = p_s[hh, rows, :]
                ds = p * (dp_s[hh, rows, :] - dsums[hh])
                dc_s[hh, rows, :] -= jnp.sum(ds, axis=1, keepdims=True)
                dsb = ds.astype(BF16)
                dk_s[rows, sl] += jnp.dot(dsb, q8[hh], preferred_element_type=F32)
                dv_s[rows, sl] += jnp.dot(p.astype(BF16), do[hh], preferred_element_type=F32)
                out.append(dqs[hh] + lax.dot_general(k_ref[0, rows, sl], dsb, TN, preferred_element_type=F32))
            return tuple(out)

        dqs = _key_blocks(i, grads, (jnp.zeros((HEAD_DIM, tq), F32),) * 2)
        dq_ref[0] = (jnp.concatenate(dqs, axis=0).T * ATT_SCALE).astype(BF16)

        @pl.when(i == nq - 1)
        def _():
            dk_ref[0] = dk_s[...].astype(BF16)
            dv_ref[0] = dv_s[...].astype(BF16)
            dc_ref[...] = jnp.zeros_like(dc_ref)
            for hh, sl in enumerate(HEADS):
                dc_ref[:, sl.start:sl.start + 1] = dc_s[hh]

    qry2 = pl.BlockSpec((tq, LANES), lambda b, hp, i: (b * nq + i, hp))
    seq2 = pl.BlockSpec((seq_len, LANES), lambda b, hp, i: (b, hp))

    def qry3(p):
        return pl.BlockSpec((1, tq, LANES), lambda b, hp, i, p=p: (p, b * nq + i, hp))

    def seq3(p):
        return pl.BlockSpec((1, seq_len, LANES), lambda b, hp, i, p=p: (p, b, hp))

    act = jax.ShapeDtypeStruct((1, T, D), BF16)
    return pl.pallas_call(
        body, name=name, grid=(n_seq, N_HEADS // 2, nq),
        in_specs=[qry2, qry3(0), qry3(1), qry2, pl.BlockSpec((1, 1, 1, 2, tq), lambda b, hp, i: (b, hp, i, 0, 0)), seq2,
                  seq3(0), seq3(1)],
        out_specs=[qry3(0), qry3(0), seq3(0), seq3(0), seq2],
        out_shape=[act, act, act, act, jax.ShapeDtypeStruct((T, D), F32)],
        scratch_shapes=[pltpu.VMEM((2, seq_len, tq), F32), pltpu.VMEM((2, seq_len, tq), F32),
                        pltpu.VMEM((seq_len, LANES), F32), pltpu.VMEM((seq_len, LANES), F32),
                        pltpu.VMEM((2, seq_len, 1), F32)],
        compiler_params=_params(3),
    )(dz, qg, qg, o, lse, cb, kv, kv)


def loss_head(y, target, name):
    T = y.shape[0]
    tm = _tile(T, TM, 8)

    def body(y_ref, t_ref, dy_ref, loss_ref):
        err = y_ref[...] - t_ref[...]
        dy_ref[...] = err * (1.0 / D)
        part = 0.5 * jnp.sum(jnp.mean(err * err, axis=-1, keepdims=True), axis=0, keepdims=True)
        _accumulate(loss_ref, pl.program_id(0) == 0, jnp.broadcast_to(part, (1, LANES)))

    row = pl.BlockSpec((tm, D), lambda i: (i, 0))
    return pl.pallas_call(
        body, name=name, grid=(T // tm,), in_specs=[row, row],
        out_specs=[row, pl.BlockSpec((1, LANES), lambda i: (0, 0))],
        out_shape=[jax.ShapeDtypeStruct((T, D), F32), jax.ShapeDtypeStruct((1, LANES), F32)],
        compiler_params=_params(1),
    )(y, target)


def _adamw(w, g, m, v):
    m = ADAM_B1 * m + (1.0 - ADAM_B1) * g
    v = ADAM_B2 * v + (1.0 - ADAM_B2) * (g * g)
    m_hat = m / (1.0 - ADAM_B1 ** ADAM_STEP)
    v_hat = v / (1.0 - ADAM_B2 ** ADAM_STEP)
    delta = -ADAM_LR * (m_hat / (jnp.sqrt(v_hat) + ADAM_EPS) + ADAM_WD * w)
    return delta, m, v


def adam_sharded(w, m, v, contribs, me, name):
    L, R, C = w.shape
    tr = _tile(R, 256, 16) if R % 16 == 0 else R

    def body(me_ref, w_ref, m_ref, v_ref, *refs):
        c_refs, (g_ref, d_ref, nm_ref, nv_ref) = refs[:2 * L], refs[2 * L:]
        l = pl.program_id(0)
        for j in range(L):
            @pl.when(l == j)
            def _(j=j):
                own_ref, recv_ref = c_refs[2 * j], c_refs[2 * j + 1]
                g = own_ref[0].astype(F32)
                for k in range(N_DEV - 1):
                    g = g + recv_ref[k].astype(F32)
                delta, nm, nv = _adamw(w_ref[0], g, m_ref[0], v_ref[0])
                g_ref[0] = g
                d_ref[0] = delta
                nm_ref[0] = nm
                nv_ref[0] = nv

    blk = pl.BlockSpec((1, tr, C), lambda l, i, s: (l, i, 0))
    in_specs = [blk, blk, blk]
    inputs = [w, m, v]
    for j, (mine, recv) in enumerate(contribs):
        in_specs.append(pl.BlockSpec((1, tr, C), lambda l, i, s, j=j: (s[0], jnp.where(l == j, i, 0), 0)))
        in_specs.append(pl.BlockSpec((N_DEV - 1, tr, C), lambda l, i, s, j=j: (0, jnp.where(l == j, i, 0), 0)))
        inputs += [mine, recv]
    shp = jax.ShapeDtypeStruct((L, R, C), F32)
    grid_spec = pltpu.PrefetchScalarGridSpec(num_scalar_prefetch=1, grid=(L, R // tr), in_specs=in_specs, out_specs=[blk] * 4)
    return pl.pallas_call(body, name=name, grid_spec=grid_spec, out_shape=[shp] * 4, compiler_params=_params(2))(me, *inputs)


def cast_place(w, l, me, dtype, after, name):
    _, R, C = w.shape
    tr = _tile(R, 512, 16) if R % 16 == 0 else R

    def body(me_ref, w_ref, after_ref, o_ref):
        o_ref[0] = w_ref[0].astype(dtype)

    grid_spec = pltpu.PrefetchScalarGridSpec(
        num_scalar_prefetch=1, grid=(R // tr,), in_specs=[pl.BlockSpec((1, tr, C), lambda i, s: (l, i, 0)), ANY],
        out_specs=pl.BlockSpec((1, tr, C), lambda i, s: (s[0], i, 0)))
    return pl.pallas_call(body, name=name, grid_spec=grid_spec, out_shape=jax.ShapeDtypeStruct((N_DEV, R, C), dtype),
                          compiler_params=_params(1))(me, w, after)


def adam_small(params, total, extra_grads, name):
    n, ne = len(params), len(extra_grads)

    def body(*refs):
        total_ref, extra_refs = refs[0], refs[1:1 + ne]
        ins, outs = refs[1 + ne:1 + ne + 3 * n], refs[1 + ne + 3 * n:]
        for k, (_, _, _, where) in enumerate(params):
            if isinstance(where, int):
                g = extra_refs[where][...]
            else:
                row, rows, width = where
                g = total_ref[row:row + rows, 0:width]
            delta, nm, nv = _adamw(ins[3 * k][...], g, ins[3 * k + 1][...], ins[3 * k + 2][...])
            outs[4 * k][...] = g
            outs[4 * k + 1][...] = delta
            outs[4 * k + 2][...] = nm
            outs[4 * k + 3][...] = nv

    flat = [a for w, m, v, _ in params for a in (w, m, v)]
    out_shape = [jax.ShapeDtypeStruct(w.shape, F32) for w, _, _, _ in params for _ in range(4)]
    res = pl.pallas_call(body, name=name, out_shape=out_shape)(total, *extra_grads, *flat)
    return [res[4 * k:4 * k + 4] for k in range(n)]


def _place():
    return lax.axis_index("x"), lax.axis_index("y"), lax.axis_index("c")


def _peer(place, k):
    x, y, c = place
    return x ^ (k >> 2), y ^ ((k >> 1) & 1), c ^ (k & 1)


ANY = pl.BlockSpec(memory_space=pl.ANY)
HBM = pl.BlockSpec(memory_space=pltpu.HBM)
SEM = pl.BlockSpec(memory_space=pltpu.SEMAPHORE)
EFFECT = pltpu.SideEffectType.DATAFLOW_SIDE_EFFECTING


def _in_hbm(a):
    return pltpu.with_memory_space_constraint(a, pltpu.HBM)


def _number(place):
    return 4 * place[0] + 2 * place[1] + place[2]


SLOTS = {"gather_like": 4, "gather_pass": 3, "scatter": 7}


def _plan(mode, src_ref, land_ref, place):
    if mode == "gather_like":
        mine = land_ref.at[_number(place)]
        return [(mine, mine, _peer(place, k)) for k in (1, 2, 4, 6)]
    if mode == "gather_pass":
        slots = [land_ref.at[_number(_peer(place, k))] for k in (2, 4, 6)]
        return [(slot, slot, _peer(place, 1)) for slot in slots]
    return [(src_ref.at[_number(_peer(place, k))], land_ref.at[k - 1], _peer(place, k)) for k in range(1, N_DEV)]


def _exchange(name, groups, start, afters):
    sizes = [len(g[3]) for g in groups]
    na, ng = sum(sizes), len(groups)
    ns = sum(len(g[2]) for g in groups)
    waits = groups[0][0] is not None
    n_in_sems = 2 * ng if waits else 0
    n_out_sems = 2 * ng if start else 0

    def body(*refs):
        src_refs, land_refs = (refs[:ns] if ns else [None] * na), refs[ns:ns + na]
        in_sems = refs[ns + na:ns + na + n_in_sems]
        outs = refs[ns + na + n_in_sems + len(afters):]
        place = _place()
        a = 0
        for gi, n in enumerate(sizes):
            for idx in range(n):
                if waits:
                    zone = land_refs[a].at[pl.ds(0, groups[gi][4])]
                    copy = pltpu.make_async_remote_copy(
                        src_ref=zone, dst_ref=zone, send_sem=in_sems[2 * gi].at[idx], recv_sem=in_sems[2 * gi + 1].at[idx],
                        device_id=_peer(place, 1), device_id_type=MESH)
                    copy.wait_send()
                    copy.wait_recv()
                if start:
                    for src, dst, peer in _plan(start, src_refs[a], land_refs[a], place):
                        pltpu.make_async_remote_copy(
                            src_ref=src, dst_ref=dst, send_sem=outs[2 * gi].at[idx], recv_sem=outs[2 * gi + 1].at[idx],
                            device_id=peer, device_id_type=MESH).start()
                a += 1
        if start:
            outs[-1][...] = jnp.zeros_like(outs[-1])

    srcs = [_in_hbm(s) for g in groups for s in g[2]]
    lands = [_in_hbm(l) for g in groups for l in g[3]]
    sems = [s for g in groups for s in g[:2]] if waits else []
    out_shape = [pltpu.SemaphoreType.DMA((n,)) for n in sizes for _ in range(2)] if start else []
    out_shape += [pltpu.HBM(a.shape, a.dtype) for a in srcs + lands]
    out_specs = [SEM] * n_out_sems + [HBM] * (ns + na)
    if start:
        out_shape.append(jax.ShapeDtypeStruct((8, LANES), F32))
        out_specs.append(pl.BlockSpec(memory_space=pltpu.VMEM))
    res = pl.pallas_call(
        body, name=name, in_specs=[HBM] * (ns + na) + [SEM] * n_in_sems + [ANY] * len(afters),
        out_shape=out_shape, out_specs=out_specs,
        input_output_aliases={i: n_out_sems + i for i in range(ns + na)},
        compiler_params=pltpu.CompilerParams(has_side_effects=EFFECT),
    )(*srcs, *lands, *sems, *afters)
    new_sems, thru = res[:n_out_sems], res[n_out_sems:n_out_sems + ns + na]
    out, a = [], 0
    for gi, n in enumerate(sizes):
        pair = (new_sems[2 * gi], new_sems[2 * gi + 1]) if start else (None, None)
        out.append(pair + (thru[a:a + n] if ns else [], thru[ns + a:ns + a + n], SLOTS.get(start, 0)))
        a += n
    return out, (res[-1] if start else None)


def exchange_start(pair_groups, mode, name):
    groups = [(None, None, [s for s, _ in g if s is not None], [l for _, l in g], 0) for g in pair_groups]
    return _exchange(name, groups, mode, ())


def exchange_relay(groups, mode, afters, name):
    return _exchange(name, groups, mode, afters)


def exchange_wait(groups, afters, name):
    done, _ = _exchange(name, groups, None, afters)
    return [(g[2], g[3]) for g in done]


def all_reduce_small(parts, n_rows, after, name):
    R = n_rows
    n_parts = len(parts)

    def body(*refs):
        part_refs = refs[:n_parts]
        out_ref, buf, send_sems, recv_sems = refs[n_parts + 1:]
        x, y, c = _place()
        me = 4 * x + 2 * y + c
        own = buf.at[me]
        own[...] = jnp.zeros((R, D), F32)
        for ref, (arr, row) in zip(part_refs, parts):
            own[row:row + arr.shape[0], 0:arr.shape[1]] = ref[...]
        copies = []
        for k in range(1, N_DEV):
            peer = (x ^ (k >> 2), y ^ ((k >> 1) & 1), c ^ (k & 1))
            copies.append(pltpu.make_async_remote_copy(
                src_ref=own, dst_ref=own, send_sem=send_sems.at[k - 1], recv_sem=recv_sems.at[k - 1],
                device_id=peer, device_id_type=MESH))
        for cp in copies:
            cp.start()
        for cp in copies:
            cp.wait()
        total = buf[0]
        for d in range(1, N_DEV):
            total = total + buf[d]
        out_ref[...] = total

    vm = pl.BlockSpec(memory_space=pltpu.VMEM)
    return pl.pallas_call(
        body, name=name, in_specs=[vm] * n_parts + [ANY], out_specs=vm, out_shape=jax.ShapeDtypeStruct((R, D), F32),
        scratch_shapes=[pltpu.VMEM((N_DEV, R, D), F32), pltpu.SemaphoreType.DMA((N_DEV - 1,)),
                        pltpu.SemaphoreType.DMA((N_DEV - 1,))],
    )(*[arr for arr, _ in parts], after)


def _col_blocks(gathered, n_blocks):
    n, d, w = gathered.shape
    whole = gathered.transpose(1, 0, 2).reshape(d, n * w)
    return whole.reshape(d, n_blocks, n * w // n_blocks).transpose(1, 0, 2)[:, None]


def _col_shards(blocks):
    n, d, w = blocks.shape
    whole = blocks.transpose(1, 0, 2).reshape(d, n * w)
    return whole.reshape(d, N_DEV, n * w // N_DEV).transpose(1, 0, 2)


def kernel(x, ffn1_pre_g, ffn1_post_g, ffn1_w_in, ffn1_w_out, mix_pre_g, mix_post_g, ffn2_pre_g, ffn2_post_g, ffn2_w_in, ffn2_w_out, conv_w_in, conv_k, conv_w_out, kv_g, kv_w, forget_b, attn_w_qg, attn_w_o, loss_target, m_ffn1_pre_g, m_ffn1_post_g, m_ffn1_w_in, m_ffn1_w_out, m_mix_pre_g, m_mix_post_g, m_ffn2_pre_g, m_ffn2_post_g, m_ffn2_w_in, m_ffn2_w_out, m_conv_w_in, m_conv_k, m_conv_w_out, m_kv_g, m_kv_w, m_forget_b, m_attn_w_qg, m_attn_w_o, v_ffn1_pre_g, v_ffn1_post_g, v_ffn1_w_in, v_ffn1_w_out, v_mix_pre_g, v_mix_post_g, v_ffn2_pre_g, v_ffn2_post_g, v_ffn2_w_in, v_ffn2_w_out, v_conv_w_in, v_conv_k, v_conv_w_out, v_kv_g, v_kv_w, v_forget_b, v_attn_w_qg, v_attn_w_o):
    n_seq, seq_len, _ = x.shape
    T = n_seq * seq_len
    xi, yi, ci = _place()
    dev = 4 * xi + 2 * yi + ci
    x0 = x.reshape(T, D)
    target = loss_target.reshape(T, D)

    me = dev.reshape(1).astype(jnp.int32)
    w1_t, w2_t, kv_t = ffn1_w_in.transpose(0, 2, 1), ffn2_w_in.transpose(0, 2, 1), kv_w.T[None]

    def zones(specs, after):
        return [(None, cast_place(w, l, me, dt, after, f"place_{nm}")) for nm, w, l, dt in specs]

    gathers, token = exchange_start([zones([("w1_in0", w1_t, 0, BF16)], me)], "gather_like", "gather_start0")
    shard_groups = [
        [("w1_out0", ffn1_w_out, 0, BF16)],
        [("cw_in", conv_w_in, 0, BF16), ("cw_out", conv_w_out, 0, BF16), ("ck", conv_k, 0, F32)],
        [("w2_in0", w2_t, 0, BF16), ("w2_out0", ffn2_w_out, 0, BF16)],
        [("kv", kv_t, 0, BF16), ("w1_in1", w1_t, 1, BF16), ("w1_out1", ffn1_w_out, 1, BF16),
         ("qg", attn_w_qg, 0, BF16), ("ow", attn_w_o, 0, BF16)],
        [("w2_in1", w2_t, 1, BF16), ("w2_out1", ffn2_w_out, 1, BF16)]]
    later, token = exchange_start([zones(g, token) for g in shard_groups], "gather_like", "gather_start1")
    gathers = gathers + later

    def gathered(k, after):
        passed, _ = exchange_relay([gathers[k]], "gather_pass", [after], f"gather_pass{k}")
        return [z[:, None] for z in exchange_wait(passed, [after], f"gather_wait{k}")[0][1]]

    fb = jnp.pad(forget_b, (0, LANES - N_HEADS))[None]

    def vec(g, l):
        return g[l:l + 1]

    def behind(g, tok):
        return g + tok[:1, :1]

    grads_small = {}
    tm_ffn = _tile(T, TM, 16)

    def ffn_fwd(xin, g_pre, g_post, w_in, w_out, tag):
        w_gu = w_in.reshape(2, 1, -1, D)
        xn, (gu,) = rms_proj(xin, g_pre, [(w_gu, 0)], [BF16], f"{tag}_in", transposed=True)
        specs = [pl.BlockSpec((2, tm_ffn, gu.shape[2]), lambda i: (0, i, 0))]
        if callable(w_out):
            w_out = w_out(xn)
        a, h, y = mix_out(_swiglu_pro, [gu], specs, (w_out, 0), xin, g_post, 0.5, 1, f"{tag}_out", tm=tm_ffn)
        return y, (xin, xn, gu, a, h), w_out

    def ffn_bwd(dy, saved, g_pre, g_post, w_in, w_out, tag):
        xin, xn, gu, a, h = saved
        w_gu = w_in.reshape(2, 1, -1, D)
        half = gu.shape[2] // 2
        dh, dg_post, dgu = post_bwd(dy, h, g_post, 0.5, (w_out, 0), 1, f"{tag}_bwd_out", gu=gu, tm=tm_ffn)
        dw_out = wgrad([a], [dh[None]], 2, (2, 1, half, D), f"{tag}_dw_out", a_cols=half)
        started_out, tok = scatter_start([dw_out.reshape(8, -1, D)], f"{tag}_out_scatter_start")
        dw_in = wgrad([dgu], [xn[None]], 4, (4, 1, half, D), f"{tag}_dw_in", a_cols=half, after=tok)
        started_in, tok = scatter_start([dw_in.reshape(8, -1, D)], f"{tag}_in_scatter_start")
        dx, dg_pre = pre_bwd([(dgu, (w_gu, 0), 0)], xin, behind(g_pre, tok), dy, f"{tag}_bwd_in", transposed=True)
        return dx, dg_pre, dg_post, [started_in, started_out]

    def scatter_start(blocked, name):
        pairs = [(g, lax.empty((N_DEV - 1,) + g.shape[1:], g.dtype)) for g in blocked]
        started, tok = exchange_start([pairs], "scatter", name)
        return started[0], tok

    (w1_in,) = gathered(0, token)
    x1, s_f1a, w1_out = ffn_fwd(x0, vec(ffn1_pre_g, 0), vec(ffn1_post_g, 0), w1_in, lambda after: gathered(1, after)[0],
                                "l0_ffn1")
    cw_in_g, cw_out, ck_g = gathered(2, x1)
    cw_in = _col_blocks(cw_in_g[:, 0], 3)
    ck = ck_g[:, 0].transpose(1, 0, 2).reshape(3, D)
    xn_c, (bch,) = rms_proj(x1, vec(mix_pre_g, 0), [(cw_in, 0)], [BF16], "conv_in")
    tmc = _tile(seq_len, TM, 16)
    hb = tmc // HALO

    def cpiece(p):
        return pl.BlockSpec((1, tmc, D), lambda i, p=p: (p, i, 0))

    def chalo(p):
        return pl.BlockSpec((1, HALO, D), lambda i, p=p: (p, jnp.maximum(i * hb - 1, 0), 0))

    conv_specs = [cpiece(0), cpiece(1), cpiece(2), chalo(1), chalo(2), pl.BlockSpec((3, D), lambda i: (0, 0))]
    z_c, m_c, x2 = mix_out(_make_conv_pro(seq_len // tmc), [bch, bch, bch, bch, bch, ck], conv_specs, (cw_out, 0),
                           x1, vec(mix_post_g, 0), 1.0, 1, "conv_out", tm=tmc)
    w2_in, w2_out = gathered(3, x2)
    x3, s_f2a, _ = ffn_fwd(x2, vec(ffn2_pre_g, 0), vec(ffn2_post_g, 0), w2_in, w2_out, "l0_ffn2")

    kvw_g, w1_in_b, w1_out_b, qgw_g, ow = gathered(4, x3)
    qg_w = _col_blocks(qgw_g[:, 0], 2)
    kv_whole = kvw_g.reshape(2 * D + N_HEADS, D).T
    kv_wb = kv_whole[:, :2 * D].reshape(D, 2, D).transpose(1, 0, 2)[:, None]
    f_w = jnp.pad(kv_whole[:, 2 * D:], ((0, 0), (0, LANES - N_HEADS)))[None, None]
    xn_kv, (kv, fl) = rms_proj(x3, kv_g[None], [(kv_wb, 0), (f_w, 0)], [BF16, F32], "kv_in")
    c128 = forget_fwd(fl, fb, seq_len, "forget_fwd")
    cb = jnp.repeat(c128[:, :N_HEADS], HEAD_DIM, axis=1)

    x4, s_f1b, _ = ffn_fwd(x3, vec(ffn1_pre_g, 1), vec(ffn1_post_g, 1), w1_in_b, w1_out_b, "l1_ffn1")
    xn_a, (qg,) = rms_proj(x4, vec(mix_pre_g, 1), [(qg_w, 0)], [BF16], "attn_in")
    o, lse = attn_fwd(qg, kv, cb, n_seq, seq_len, "attn_fwd")
    tm = _tile(T, TM, 16)
    gate_specs = [pl.BlockSpec((1, tm, D), lambda i: (1, i, 0)), pl.BlockSpec((tm, D), lambda i: (i, 0))]
    z_a, m_a, x5 = mix_out(_attn_gate_pro, [qg, o], gate_specs, (ow, 0), x4, vec(mix_post_g, 1), 1.0, 1, "attn_out")
    w2_in_b, w2_out_b = gathered(5, x5)
    x6, s_f2b, _ = ffn_fwd(x5, vec(ffn2_pre_g, 1), vec(ffn2_post_g, 1), w2_in_b, w2_out_b, "l1_ffn2")

    dy, loss_part = loss_head(x6, target, "loss_head")

    scatters = {}
    dx5, dg, dgp, scatters["ffn2", 1] = ffn_bwd(dy, s_f2b, vec(ffn2_pre_g, 1), vec(ffn2_post_g, 1), w2_in_b, w2_out_b, "l1_ffn2")
    grads_small["ffn2_pre", 1], grads_small["ffn2_post", 1] = dg, dgp
    dm_a, dgp, dz_a = post_bwd(dx5, m_a, vec(mix_post_g, 1), 1.0, (ow, 0), 1, "attn_bwd_out")
    grads_small["mix_post", 1] = dgp
    dq, dgate, dk, dv, dcb = attn_bwd(dz_a[0], qg, kv, o, lse, cb, n_seq, seq_len, "attn_bwd")
    dx4, dg = pre_bwd([(dq, (qg_w, 0), 0), (dgate, (qg_w, 0), 1)], x4, vec(mix_pre_g, 1), dx5, "attn_bwd_in")
    grads_small["mix_pre", 1] = dg
    d_ow = wgrad([z_a], [dm_a[None]], 1, ow.shape, "attn_dw_o")
    d_qgw = wgrad([xn_a[None]], [dq, dgate], 2, (2, 1, D, D), "attn_dw_qg")
    scatters["attn"], tok = scatter_start([_col_shards(d_qgw[:, 0]), d_ow.reshape(8, -1, D)], "attn_scatter_start")
    dx3, dg, dgp, scatters["ffn1", 1] = ffn_bwd(dx4, s_f1b, vec(ffn1_pre_g, 1), behind(vec(ffn1_post_g, 1), tok),
                                                w1_in_b, w1_out_b, "l1_ffn1")
    grads_small["ffn1_pre", 1], grads_small["ffn1_post", 1] = dg, dgp

    dc16 = dcb.reshape(T, N_HEADS, HEAD_DIM)[:, :, 0]
    dfl, dfb = forget_bwd(jnp.pad(dc16, ((0, 0), (0, LANES - N_HEADS))), fl, fb, seq_len, "forget_bwd")
    dx3, dg_kv = pre_bwd([(dk, (kv_wb, 0), 0), (dv, (kv_wb, 0), 1), (dfl, (f_w, 0), 0)], x3, kv_g[None], dx3, "kv_bwd_in")
    d_kvw = wgrad([xn_kv[None]], [dk, dv], 2, (2, 1, D, D), "kv_dw")
    d_fw = wgrad([xn_kv[None]], [dfl.astype(BF16)], 1, (1, 1, D, LANES), "forget_dw")
    d_kv_whole = jnp.concatenate([d_kvw[0, 0], d_kvw[1, 0], d_fw[0, 0, :, :N_HEADS]], axis=1)
    wshard = D * 2 + N_HEADS
    scatters["kv"], tok = scatter_start([d_kv_whole.T.reshape(N_DEV, wshard // N_DEV, D)], "kv_scatter_start")

    dx2, dg, dgp, scatters["ffn2", 0] = ffn_bwd(dx3, s_f2a, vec(ffn2_pre_g, 0), behind(vec(ffn2_post_g, 0), tok),
                                                w2_in, w2_out, "l0_ffn2")
    grads_small["ffn2_pre", 0], grads_small["ffn2_post", 0] = dg, dgp
    dm_c, dgp, dz_c = post_bwd(dx2, m_c, vec(mix_post_g, 0), 1.0, (cw_out, 0), 1, "conv_bwd_out")
    grads_small["mix_post", 0] = dgp
    dbch, d_ck = conv_bwd_mix(dz_c, bch, ck, seq_len, "conv_bwd_mix")
    dx1, dg = pre_bwd([(dbch, (cw_in, 0), 0)], x1, vec(mix_pre_g, 0), dx2, "conv_bwd_in")
    grads_small["mix_pre", 0] = dg
    d_cw_out = wgrad([z_c], [dm_c[None]], 1, cw_out.shape, "conv_dw_out")
    d_cw_in = wgrad([xn_c[None]], [dbch], 3, (3, 1, D, D), "conv_dw_in")
    scatters["conv"], tok = scatter_start([_col_shards(d_cw_in[:, 0]), d_cw_out.reshape(8, -1, D)], "conv_scatter_start")
    dx0, dg, dgp, scatters["ffn1", 0] = ffn_bwd(dx1, s_f1a, vec(ffn1_pre_g, 0), behind(vec(ffn1_post_g, 0), tok),
                                                w1_in, w1_out, "l0_ffn1")
    grads_small["ffn1_pre", 0], grads_small["ffn1_post", 0] = dg, dgp

    parts_of = {}

    def scatter_end(keys, afters, name):
        flat = [(k, g) for k in keys for g in (scatters[k] if isinstance(scatters[k], list) else [scatters[k]])]
        for (k, _), (sent, recv) in zip(flat, exchange_wait([g for _, g in flat], afters, name)):
            parts_of.setdefault(k, []).extend(zip(sent, recv))

    scatter_end([("ffn2", 1), "attn", ("ffn1", 1), "kv", ("ffn2", 0), "conv"], [dx0], "scatter_wait")
    sharded = {"ffn2_w_in": (ffn2_w_in, m_ffn2_w_in, v_ffn2_w_in), "ffn2_w_out": (ffn2_w_out, m_ffn2_w_out, v_ffn2_w_out),
               "conv_w_in": (conv_w_in, m_conv_w_in, v_conv_w_in), "conv_w_out": (conv_w_out, m_conv_w_out, v_conv_w_out),
               "kv_w": (kv_w, m_kv_w, v_kv_w), "attn_w_qg": (attn_w_qg, m_attn_w_qg, v_attn_w_qg),
               "attn_w_o": (attn_w_o, m_attn_w_o, v_attn_w_o),
               "ffn1_w_in": (ffn1_w_in, m_ffn1_w_in, v_ffn1_w_in), "ffn1_w_out": (ffn1_w_out, m_ffn1_w_out, v_ffn1_w_out)}
    out = {}
    for nm, (w, mm, vv) in sharded.items():
        if nm == "ffn1_w_in":
            scatter_end([("ffn1", 0)], [res[0] for res in out.values()], "scatter_wait_last")
        contribs = {
            "ffn1_w_in": lambda: [parts_of["ffn1", 0][0], parts_of["ffn1", 1][0]],
            "ffn1_w_out": lambda: [parts_of["ffn1", 0][1], parts_of["ffn1", 1][1]],
            "ffn2_w_in": lambda: [parts_of["ffn2", 0][0], parts_of["ffn2", 1][0]],
            "ffn2_w_out": lambda: [parts_of["ffn2", 0][1], parts_of["ffn2", 1][1]],
            "conv_w_in": lambda: [parts_of["conv"][0]], "conv_w_out": lambda: [parts_of["conv"][1]],
            "kv_w": lambda: [parts_of["kv"][0]], "attn_w_qg": lambda: [parts_of["attn"][0]],
            "attn_w_o": lambda: [parts_of["attn"][1]]}[nm]()
        if nm in ("ffn1_w_in", "ffn2_w_in", "kv_w"):
            rows, cols = w.shape[-2:]

            def view(a):
                return a.reshape(-1, rows, cols).transpose(0, 2, 1)

            res = adam_sharded(view(w), view(mm), view(vv), contribs, me, f"adam_{nm}")
            out[nm] = [r.transpose(0, 2, 1).reshape(w.shape) for r in res]
        else:
            shape3 = (len(contribs),) + contribs[0][0].shape[1:]
            res = adam_sharded(w.reshape(shape3), mm.reshape(shape3), vv.reshape(shape3), contribs, me, f"adam_{nm}")
            out[nm] = [r.reshape(w.shape) for r in res]

    small_names = ["ffn1_pre", "ffn1_post", "mix_pre", "mix_post", "ffn2_pre", "ffn2_post"]
    parts = [(grads_small[n, l], 2 * k + l) for k, n in enumerate(small_names) for l in range(2)]
    parts += [(dg_kv, 12), (dfb, 13), (d_ck, 14), (loss_part, 17)]
    total = all_reduce_small(parts, 24, out["ffn1_w_in"][0], "all_reduce_small")
    loss = total[17, 0]
    d_ck_mine = lax.dynamic_slice(total, (14, dev * LANES), (3, LANES))
    gains = [(ffn1_pre_g, m_ffn1_pre_g, v_ffn1_pre_g), (ffn1_post_g, m_ffn1_post_g, v_ffn1_post_g),
             (mix_pre_g, m_mix_pre_g, v_mix_pre_g), (mix_post_g, m_mix_post_g, v_mix_post_g),
             (ffn2_pre_g, m_ffn2_pre_g, v_ffn2_pre_g), (ffn2_post_g, m_ffn2_post_g, v_ffn2_post_g)]
    small_params = [(w, m, v, (2 * k, 2, D)) for k, (w, m, v) in enumerate(gains)]
    small_params += [(kv_g[None], m_kv_g[None], v_kv_g[None], (12, 1, D)),
                     (forget_b[None], m_forget_b[None], v_forget_b[None], (13, 1, N_HEADS)),
                     (conv_k[0], m_conv_k[0], v_conv_k[0], 0)]
    small_res = adam_small(small_params, total, [d_ck_mine], "adam_small")
    small_keys = [n + "_g" for n in small_names] + ["kv_g", "forget_b", "conv_k"]
    shapes = {"kv_g": kv_g.shape, "forget_b": forget_b.shape, "conv_k": conv_k.shape}
    small = [{key: res[kind].reshape(shapes.get(key, res[kind].shape)) for key, res in zip(small_keys, small_res)}
             for kind in range(4)]
    order = ["ffn1_pre_g", "ffn1_post_g", "ffn1_w_in", "ffn1_w_out", "mix_pre_g", "mix_post_g", "ffn2_pre_g", "ffn2_post_g",
             "ffn2_w_in", "ffn2_w_out", "conv_w_in", "conv_k", "conv_w_out", "kv_g", "kv_w", "forget_b", "attn_w_qg",
             "attn_w_o"]
    results = [loss, dx0.reshape(x.shape)]
    for kind in range(4):
        for nm in order:
            results.append(out[nm][kind] if nm in out else small[kind][nm])
    return tuple(results)
```

```python
import functools
import math

import jax
import jax.numpy as jnp
from jax import lax
from jax.experimental import pallas as pl
from jax.experimental.pallas import tpu as pltpu

F32, BF16 = jnp.float32, jnp.bfloat16
D = 1024
N_HEADS = 16
HEAD_DIM = 64
N_DEV = 8
RMS_EPS = 1e-6
ATT_SCALE = 1.0 / math.sqrt(HEAD_DIM)
LANES = 128
HALO = 8
TM = 512
TQ = 512
VMEM_LIMIT = 48 * 1024 * 1024
MESH = pl.DeviceIdType.MESH

ADAM_LR, ADAM_B1, ADAM_B2, ADAM_EPS, ADAM_WD, ADAM_STEP = 0.001, 0.9, 0.999, 1e-08, 0.01, 10

NT = (((1,), (1,)), ((), ()))
TN = (((0,), (0,)), ((), ()))


def _params(n_axes, vmem_limit=VMEM_LIMIT):
    return pltpu.CompilerParams(dimension_semantics=("arbitrary",) * n_axes, vmem_limit_bytes=vmem_limit)


def _tile(n, cap, mult):
    best = None
    for t in range(mult, min(n, cap) + 1, mult):
        if n % t == 0:
            best = t
    assert best is not None, (n, cap, mult)
    return best


def _rms_rstd(x):
    return lax.rsqrt(jnp.mean(x * x, axis=-1, keepdims=True) + RMS_EPS)


def _rms_fwd(x, g):
    return x * _rms_rstd(x) * g


def _rms_bwd(x, g, dy):
    xh = x * _rms_rstd(x)
    dyg = dy * g
    dx = _rms_rstd(x) * (dyg - xh * jnp.mean(dyg * xh, axis=-1, keepdims=True))
    return dx, jnp.sum(dy * xh, axis=0, keepdims=True)


def _accumulate(ref, first, value):
    @pl.when(first)
    def _():
        ref[...] = value

    @pl.when(jnp.logical_not(first))
    def _():
        ref[...] += value


def rms_proj(x, g, ws, out_dtypes, name, transposed=False):
    T = x.shape[0]
    tm = _tile(T, TM, 16)
    na = len(ws)

    def body(x_ref, g_ref, *refs):
        w_refs, xn_ref, o_refs = refs[:na], refs[na], refs[na + 1:]
        xn = _rms_fwd(x_ref[...], g_ref[...]).astype(BF16)
        xn_ref[...] = xn
        for a, (w, l) in enumerate(ws):
            for p in range(w.shape[0]):
                if transposed:
                    y = lax.dot_general(xn, w_refs[a][p, l], NT, preferred_element_type=F32)
                else:
                    y = jnp.dot(xn, w_refs[a][p, l], preferred_element_type=F32)
                o_refs[a][p] = y.astype(o_refs[a].dtype)

    in_specs = [pl.BlockSpec((tm, D), lambda i: (i, 0)), pl.BlockSpec((1, D), lambda i: (0, 0))]
    in_specs += [pl.BlockSpec(w.shape, lambda i: (0, 0, 0, 0), pipeline_mode=pl.Buffered(1)) for w, _ in ws]
    out_specs = [pl.BlockSpec((tm, D), lambda i: (i, 0))]
    out_shape = [jax.ShapeDtypeStruct((T, D), BF16)]
    for (w, _), dt in zip(ws, out_dtypes):
        nb, wb = w.shape[0], w.shape[2 if transposed else 3]
        out_specs.append(pl.BlockSpec((nb, tm, wb), lambda i: (0, i, 0)))
        out_shape.append(jax.ShapeDtypeStruct((nb, T, wb), dt))
    res = pl.pallas_call(
        body, name=name, grid=(T // tm,), in_specs=in_specs, out_specs=out_specs, out_shape=out_shape,
        compiler_params=_params(1),
    )(x, g, *[w for w, _ in ws])
    return res[0], res[1:]


def mix_out(pro, pro_inputs, pro_specs, w, res, g_post, alpha, nk, name, tm=None):
    w4, l = w
    T = res.shape[0]
    tm = _tile(T, TM, 16) if tm is None else tm
    dpb = N_DEV // nk
    rows = w4.shape[2]
    kb = dpb * rows
    npi = len(pro_inputs)

    def body(*refs):
        pro_refs = refs[:npi]
        w_ref, res_ref, g_ref, z_ref, m_ref, y_ref = refs[npi:]
        i = pl.program_id(0)
        m = None
        for k in range(nk):
            z = pro(i, k, *pro_refs).astype(BF16)
            z_ref[k] = z
            part = jnp.dot(z, w_ref[k * dpb:(k + 1) * dpb, l].reshape(kb, D), preferred_element_type=F32)
            m = part if m is None else m + part
        m_ref[...] = m
        y_ref[...] = res_ref[...] + alpha * _rms_fwd(m, g_ref[...])

    row = pl.BlockSpec((tm, D), lambda i: (i, 0))
    in_specs = list(pro_specs) + [
        pl.BlockSpec(w4.shape, lambda i: (0, 0, 0, 0), pipeline_mode=pl.Buffered(1)), row, pl.BlockSpec((1, D), lambda i: (0, 0))]
    z, m, y = pl.pallas_call(
        body, name=name, grid=(T // tm,), in_specs=in_specs,
        out_specs=[pl.BlockSpec((nk, tm, kb), lambda i: (0, i, 0)), row, row],
        out_shape=[jax.ShapeDtypeStruct((nk, T, kb), BF16), jax.ShapeDtypeStruct((T, D), F32),
                   jax.ShapeDtypeStruct((T, D), F32)],
        compiler_params=_params(1),
    )(*pro_inputs, w4, res, g_post)
    return z, m, y


def post_bwd(dy, m, g_post, alpha, w, nk, name, gu=None, tm=None):
    w4, l = w
    T = dy.shape[0]
    tm = _tile(T, TM, 16) if tm is None else tm
    dpb = N_DEV // nk
    rows = w4.shape[2]
    kb = dpb * rows
    swiglu = gu is not None

    def body(dy_ref, m_ref, g_ref, w_ref, *refs):
        if swiglu:
            gu_ref, dm_ref, dg_ref, dgu_ref = refs
        else:
            dm_ref, dg_ref, dz_ref = refs
        dm, dg = _rms_bwd(m_ref[...], g_ref[...], alpha * dy_ref[...])
        dm = dm.astype(BF16)
        dm_ref[...] = dm
        _accumulate(dg_ref, pl.program_id(0) == 0, dg)
        for k in range(nk):
            dz = lax.dot_general(dm, w_ref[k * dpb:(k + 1) * dpb, l].reshape(kb, D), NT, preferred_element_type=F32)
            if swiglu:
                gate, up = gu_ref[k].astype(F32), gu_ref[k + nk].astype(F32)
                sig = jax.nn.sigmoid(gate)
                dgu_ref[k] = (dz * up * sig * (1.0 + gate * (1.0 - sig))).astype(BF16)
                dgu_ref[k + nk] = (dz * gate * sig).astype(BF16)
            else:
                dz_ref[k] = dz.astype(BF16)

    row = pl.BlockSpec((tm, D), lambda i: (i, 0))
    vec = pl.BlockSpec((1, D), lambda i: (0, 0))
    blkk = pl.BlockSpec((nk, tm, kb), lambda i: (0, i, 0))
    in_specs = [row, row, vec, pl.BlockSpec(w4.shape, lambda i: (0, 0, 0, 0), pipeline_mode=pl.Buffered(1))]
    inputs = [dy, m, g_post, w4]
    n_dz = 2 * nk if swiglu else nk
    out_specs = [row, vec, pl.BlockSpec((n_dz, tm, kb), lambda i: (0, i, 0))]
    out_shape = [jax.ShapeDtypeStruct((T, D), BF16), jax.ShapeDtypeStruct((1, D), F32),
                 jax.ShapeDtypeStruct((n_dz, T, kb), BF16)]
    if swiglu:
        in_specs.append(pl.BlockSpec((2 * nk, tm, kb), lambda i: (0, i, 0)))
        inputs.append(gu)
    return pl.pallas_call(
        body, name=name, grid=(T // tm,), in_specs=in_specs, out_specs=out_specs, out_shape=out_shape,
        compiler_params=_params(1),
    )(*inputs)


def pre_bwd(pieces, x, g_pre, dres, name, transposed=False):
    T = x.shape[0]
    tm = _tile(T, TM, 16)
    na = len(pieces)
    weights = []
    for _, (w4, _), _ in pieces:
        if not any(w4 is w for w in weights):
            weights.append(w4)
    nw = len(weights)
    which = [[w4 is w for w in weights].index(True) for _, (w4, _), _ in pieces]

    def body(*refs):
        dz_refs, w_refs = refs[:na], refs[na:na + nw]
        x_ref, g_ref, dres_ref, dx_ref, dg_ref = refs[na + nw:]
        acc = None
        for a, (dz, (_, l), w_off) in enumerate(pieces):
            for p in range(dz.shape[0]):
                w = w_refs[which[a]][w_off + p, l]
                if transposed:
                    part = jnp.dot(dz_refs[a][p].astype(BF16), w, preferred_element_type=F32)
                else:
                    part = lax.dot_general(dz_refs[a][p].astype(BF16), w, NT, preferred_element_type=F32)
                acc = part if acc is None else acc + part
        dx, dg = _rms_bwd(x_ref[...], g_ref[...], acc)
        dx_ref[...] = dres_ref[...] + dx
        _accumulate(dg_ref, pl.program_id(0) == 0, dg)

    row = pl.BlockSpec((tm, D), lambda i: (i, 0))
    vec = pl.BlockSpec((1, D), lambda i: (0, 0))
    dz_specs = [pl.BlockSpec((dz.shape[0], tm, dz.shape[2]), lambda i: (0, i, 0)) for dz, _, _ in pieces]
    w_specs = [pl.BlockSpec(w.shape, lambda i: (0, 0, 0, 0), pipeline_mode=pl.Buffered(1)) for w in weights]
    return pl.pallas_call(
        body, name=name, grid=(T // tm,), in_specs=dz_specs + w_specs + [row, vec, row],
        out_specs=[row, vec], out_shape=[jax.ShapeDtypeStruct((T, D), F32), jax.ShapeDtypeStruct((1, D), F32)],
        compiler_params=_params(1),
    )(*[p[0] for p in pieces], *weights, x, g_pre, dres)


def wgrad(a_list, b_list, nout, out4_shape, name, a_cols=None, tm_cap=None, after=None):
    T = a_list[0].shape[1]
    tm = _tile(T, 4 * TM if tm_cap is None else tm_cap, 16)
    nt = T // tm
    dpb = out4_shape[0] // nout
    _, _, R, C = out4_shape
    na, nb = len(a_list), len(b_list)
    a_w = a_list[0].shape[2] if a_cols is None else a_cols
    a_per = a_list[0].shape[2] // a_w

    def spans(arrs, per):
        ns = [a.shape[0] * per for a in arrs]
        return ns, [sum(ns[:k]) for k in range(len(ns))], sum(ns)

    a_ns, a_offs, a_tot = spans(a_list, a_per)
    b_ns, b_offs, b_tot = spans(b_list, 1)
    assert a_tot in (1, nout) and b_tot in (1, nout)

    def body(*refs):
        a_refs, b_refs = refs[:na], refs[na:na + nb]
        out_ref, acc = refs[-2:]
        p, t = pl.program_id(0), pl.program_id(1)
        for ia in range(na):
            for ib in range(nb):
                conds = []
                if a_tot > 1:
                    conds += [p >= a_offs[ia], p < a_offs[ia] + a_ns[ia]]
                if b_tot > 1:
                    conds += [p >= b_offs[ib], p < b_offs[ib] + b_ns[ib]]

                def work(ia=ia, ib=ib):
                    part = lax.dot_general(a_refs[ia][0], b_refs[ib][0], TN, preferred_element_type=F32)
                    _accumulate(acc, t == 0, part)

                if conds:
                    pl.when(functools.reduce(jnp.logical_and, conds))(work)
                else:
                    work()

        @pl.when(t == nt - 1)
        def _():
            out_ref[...] = acc[...].astype(BF16).reshape(dpb, 1, R, C)

    def blk(off, n, tot):
        if tot == 1:
            return lambda p: 0
        return lambda p: jnp.clip(p - off, 0, n - 1)

    in_specs = []
    for arr, n, off in zip(a_list, a_ns, a_offs):
        in_specs.append(pl.BlockSpec((1, tm, a_w), lambda p, t, f=blk(off, n, a_tot): (f(p) // a_per, t, f(p) % a_per)))
    for arr, n, off in zip(b_list, b_ns, b_offs):
        in_specs.append(pl.BlockSpec((1, tm, arr.shape[2]), lambda p, t, f=blk(off, n, b_tot): (f(p), t, 0)))
    extra = [] if after is None else [after]
    return pl.pallas_call(
        body, name=name, grid=(nout, nt), in_specs=in_specs + [ANY] * len(extra),
        out_specs=pl.BlockSpec((dpb, 1, R, C), lambda p, t: (p, 0, 0, 0)),
        out_shape=jax.ShapeDtypeStruct(out4_shape, BF16),
        scratch_shapes=[pltpu.VMEM((dpb * R, C), F32)], compiler_params=_params(2),
    )(*a_list, *b_list, *extra)


def _swiglu_pro(i, k, gu_ref):
    gate, up = gu_ref[k].astype(F32), gu_ref[k + gu_ref.shape[0] // 2].astype(F32)
    return gate * jax.nn.sigmoid(gate) * up


def _attn_gate_pro(i, k, gate_ref, o_ref):
    return jax.nn.sigmoid(gate_ref[0].astype(F32)) * o_ref[...].astype(F32)


def _shift_rows(u, halo, d):
    rolled = pltpu.roll(u, d, 0)
    row = lax.broadcasted_iota(jnp.int32, u.shape, 0)
    for r in range(d):
        rolled = jnp.where(row == r, halo[HALO - d + r:HALO - d + r + 1, :], rolled)
    return rolled


def _advance_rows(u, halo, d):
    n = u.shape[0]
    rolled = pltpu.roll(u, n - d, 0)
    row = lax.broadcasted_iota(jnp.int32, u.shape, 0)
    for r in range(d):
        rolled = jnp.where(row == n - d + r, halo[r:r + 1, :], rolled)
    return rolled


def _make_conv_pro(tiles_per_seq):
    def pro(i, k, b_ref, c_ref, h_ref, ch_ref, hh_ref, ck_ref):
        u = c_ref[0].astype(F32) * h_ref[0].astype(F32)
        first = (i % tiles_per_seq) == 0
        halo = jnp.where(first, 0.0, ch_ref[0].astype(F32) * hh_ref[0].astype(F32))
        ck = ck_ref[...]
        conv = ck[2:3, :] * u + ck[1:2, :] * _shift_rows(u, halo, 1) + ck[0:1, :] * _shift_rows(u, halo, 2)
        return b_ref[0].astype(F32) * conv
    return pro


def conv_bwd_mix(dz, bch, conv_k, seq_len, name):
    T = dz.shape[1]
    tm = _tile(seq_len, TM, 16)
    tps = seq_len // tm
    nt = T // tm
    hb = tm // HALO

    def body(dz_ref, b_ref, c_ref, h_ref, cp_ref, hp_ref, dzn_ref, bn_ref, ck_ref, dbch_ref, dk_ref):
        i = pl.program_id(0)
        first = (i % tps) == 0
        last = (i % tps) == tps - 1
        b, c, h = b_ref[0].astype(F32), c_ref[0].astype(F32), h_ref[0].astype(F32)
        dzt = dz_ref[0].astype(F32)
        u = c * h
        prev = jnp.where(first, 0.0, cp_ref[0].astype(F32) * hp_ref[0].astype(F32))
        u1, u2 = _shift_rows(u, prev, 1), _shift_rows(u, prev, 2)
        ck = ck_ref[...]
        conv = ck[2:3, :] * u + ck[1:2, :] * u1 + ck[0:1, :] * u2
        dconv = dzt * b
        nxt = jnp.where(last, 0.0, dzn_ref[0].astype(F32) * bn_ref[0].astype(F32))
        du = ck[2:3, :] * dconv + ck[1:2, :] * _advance_rows(dconv, nxt, 1) + ck[0:1, :] * _advance_rows(dconv, nxt, 2)
        dbch_ref[0] = (dzt * conv).astype(BF16)
        dbch_ref[1] = (du * h).astype(BF16)
        dbch_ref[2] = (du * c).astype(BF16)
        tap = lax.broadcasted_iota(jnp.int32, (3, D), 0)
        dk = jnp.where(tap == 0, jnp.sum(dconv * u2, axis=0, keepdims=True),
                       jnp.where(tap == 1, jnp.sum(dconv * u1, axis=0, keepdims=True),
                                 jnp.sum(dconv * u, axis=0, keepdims=True)))
        _accumulate(dk_ref, i == 0, dk)

    def piece(p):
        return pl.BlockSpec((1, tm, D), lambda i, p=p: (p, i, 0))

    def prev(p):
        return pl.BlockSpec((1, HALO, D), lambda i, p=p: (p, jnp.maximum(i * hb - 1, 0), 0))

    def nxt(p):
        return pl.BlockSpec((1, HALO, D), lambda i, p=p: (p, jnp.minimum((i + 1) * hb, nt * hb - 1), 0))

    return pl.pallas_call(
        body, name=name, grid=(nt,),
        in_specs=[piece(0), piece(0), piece(1), piece(2), prev(1), prev(2), nxt(0), nxt(0),
                  pl.BlockSpec((3, D), lambda i: (0, 0))],
        out_specs=[pl.BlockSpec((3, tm, D), lambda i: (0, i, 0)), pl.BlockSpec((3, D), lambda i: (0, 0))],
        out_shape=[jax.ShapeDtypeStruct((3, T, D), BF16), jax.ShapeDtypeStruct((3, D), F32)],
        compiler_params=_params(1),
    )(dz, bch, bch, bch, bch, bch, dz, bch, conv_k)


def _log_sigmoid(x):
    return jnp.minimum(x, 0.0) - jnp.log(1.0 + jnp.exp(-jnp.abs(x)))


def forget_fwd(fl, fb, seq_len, name):
    T = fl.shape[1]

    def body(fl_ref, fb_ref, c_ref):
        c = _log_sigmoid(fl_ref[0] + fb_ref[...])
        row = lax.broadcasted_iota(jnp.int32, c.shape, 0)
        k = 1
        while k < seq_len:
            c = c + jnp.where(row >= k, pltpu.roll(c, k, 0), 0.0)
            k *= 2
        c_ref[...] = jnp.concatenate([jnp.broadcast_to(c[:, h:h + 1], (seq_len, HEAD_DIM)) for h in range(N_HEADS)], axis=1)

    return pl.pallas_call(
        body, name=name, grid=(T // seq_len,),
        in_specs=[pl.BlockSpec((1, seq_len, LANES), lambda b: (0, b, 0)), pl.BlockSpec((1, LANES), lambda b: (0, 0))],
        out_specs=pl.BlockSpec((seq_len, D), lambda b: (b, 0)),
        out_shape=jax.ShapeDtypeStruct((T, D), F32), compiler_params=_params(1),
    )(fl, fb)


def forget_bwd(dcb, fl, fb, seq_len, name):
    T = dcb.shape[0]
    pick = jnp.zeros((D, LANES), F32).at[HEAD_DIM * jnp.arange(N_HEADS), jnp.arange(N_HEADS)].set(1.0)

    def body(dc_ref, pick_ref, fl_ref, fb_ref, dfl_ref, dfb_ref):
        b = pl.program_id(0)
        r = jnp.dot(dc_ref[...], pick_ref[...], precision=lax.Precision.HIGHEST, preferred_element_type=F32)
        row = lax.broadcasted_iota(jnp.int32, r.shape, 0)
        k = 1
        while k < seq_len:
            r = r + jnp.where(row < seq_len - k, pltpu.roll(r, seq_len - k, 0), 0.0)
            k *= 2
        dfl = r * jax.nn.sigmoid(-(fl_ref[0] + fb_ref[...]))
        dfl_ref[0] = dfl
        _accumulate(dfb_ref, b == 0, jnp.sum(dfl, axis=0, keepdims=True))

    return pl.pallas_call(
        body, name=name, grid=(T // seq_len,),
        in_specs=[pl.BlockSpec((seq_len, D), lambda b: (b, 0)), pl.BlockSpec((D, LANES), lambda b: (0, 0)),
                  pl.BlockSpec((1, seq_len, LANES), lambda b: (0, b, 0)), pl.BlockSpec((1, LANES), lambda b: (0, 0))],
        out_specs=[pl.BlockSpec((1, seq_len, LANES), lambda b: (0, b, 0)), pl.BlockSpec((1, LANES), lambda b: (0, 0))],
        out_shape=[jax.ShapeDtypeStruct((1, T, LANES), F32), jax.ShapeDtypeStruct((1, LANES), F32)],
        compiler_params=_params(1),
    )(dcb, pick, fl, fb)


HEADS = (slice(0, HEAD_DIM), slice(HEAD_DIM, 2 * HEAD_DIM))


def _key_blocks(i, step, carry):
    carry = lax.fori_loop(0, i // 2, lambda jj, c: step(2 * jj, c, False, 2), carry)
    return lax.cond(i % 2 == 1, lambda c: step(i - 1, c, True, 2), lambda c: step(i, c, True, 1), carry)


def _causal(width, tq):
    keys = lax.broadcasted_iota(jnp.int32, (width * tq, tq), 0)
    return keys <= lax.broadcasted_iota(jnp.int32, (width * tq, tq), 1) + (width - 1) * tq


def attn_fwd(qg, kv, cb, n_seq, seq_len, name):
    T = n_seq * seq_len
    tq = _tile(seq_len, TQ, LANES)
    nq = seq_len // tq

    def body(q_ref, k_ref, v_ref, cb_ref, o_ref, lse_ref):
        i = pl.program_id(2)
        q8 = [q_ref[0, :, sl] * ATT_SCALE for sl in HEADS]

        def block(j, carry, diagonal, width):
            rows = pl.ds(pl.multiple_of(j * tq, tq), width * tq)
            out = []
            for hh, sl in enumerate(HEADS):
                m, l, acc = carry[3 * hh:3 * hh + 3]
                s = lax.dot_general(k_ref[0, rows, sl], q8[hh], NT, preferred_element_type=F32)
                s = s - cb_ref[rows, sl.start:sl.start + 1]
                if diagonal:
                    s = jnp.where(_causal(width, tq), s, -1e30)
                m_new = jnp.maximum(m, jnp.max(s, axis=0, keepdims=True))
                a = jnp.exp(m - m_new)
                p = jnp.exp(s - m_new)
                l = a * l + jnp.sum(p, axis=0, keepdims=True)
                acc = a * acc + lax.dot_general(v_ref[0, rows, sl], p.astype(BF16), TN, preferred_element_type=F32)
                out += [m_new, l, acc]
            return tuple(out)

        init = (jnp.full((1, tq), -1e30, F32), jnp.zeros((1, tq), F32), jnp.zeros((HEAD_DIM, tq), F32)) * 2
        carry = _key_blocks(i, block, init)
        o_ref[...] = jnp.concatenate([carry[2] / carry[1], carry[5] / carry[4]], axis=0).T
        for hh in range(2):
            lse_ref[0, 0, 0, hh:hh + 1, :] = carry[3 * hh] + jnp.log(carry[3 * hh + 1])

    seq2 = pl.BlockSpec((seq_len, LANES), lambda b, hp, i: (b, hp))
    return pl.pallas_call(
        body, name=name, grid=(n_seq, N_HEADS // 2, nq),
        in_specs=[pl.BlockSpec((1, tq, LANES), lambda b, hp, i: (0, b * nq + i, hp)),
                  pl.BlockSpec((1, seq_len, LANES), lambda b, hp, i: (0, b, hp)),
                  pl.BlockSpec((1, seq_len, LANES), lambda b, hp, i: (1, b, hp)), seq2],
        out_specs=[pl.BlockSpec((tq, LANES), lambda b, hp, i: (b * nq + i, hp)),
                   pl.BlockSpec((1, 1, 1, 2, tq), lambda b, hp, i: (b, hp, i, 0, 0))],
        out_shape=[jax.ShapeDtypeStruct((T, D), F32), jax.ShapeDtypeStruct((n_seq, N_HEADS // 2, nq, 2, tq), F32)],
        compiler_params=_params(3),
    )(qg, kv, kv, cb)


def attn_bwd(dz, qg, kv, o, lse, cb, n_seq, seq_len, name):
    T = n_seq * seq_len
    tq = _tile(seq_len, TQ, LANES)
    nq = seq_len // tq

    def body(dz_ref, q_ref, gate_ref, o_ref, lse_ref, cb_ref, k_ref, v_ref,
             dq_ref, dgate_ref, dk_ref, dv_ref, dc_ref, p_s, dp_s, dk_s, dv_s, dc_s):
        i = pl.program_id(2)

        @pl.when(i == 0)
        def _():
            dk_s[...] = jnp.zeros_like(dk_s)
            dv_s[...] = jnp.zeros_like(dv_s)
            dc_s[...] = jnp.zeros_like(dc_s)

        dzf = dz_ref[...].astype(F32)
        sig = jax.nn.sigmoid(gate_ref[0].astype(F32))
        dob = (dzf * sig).astype(BF16)
        dgate_ref[0] = (dzf * o_ref[...] * sig * (1.0 - sig)).astype(BF16)
        q8 = [q_ref[0, :, sl] * ATT_SCALE for sl in HEADS]
        do = [dob[:, sl] for sl in HEADS]
        lse_i = [lse_ref[0, 0, 0, hh:hh + 1, :] for hh in range(2)]

        def probs(j, dsums, diagonal, width):
            rows = pl.ds(pl.multiple_of(j * tq, tq), width * tq)
            out = []
            for hh, sl in enumerate(HEADS):
                s = lax.dot_general(k_ref[0, rows, sl], q8[hh], NT, preferred_element_type=F32)
                p = jnp.exp(s - cb_ref[rows, sl.start:sl.start + 1] - lse_i[hh])
                if diagonal:
                    p = jnp.where(_causal(width, tq), p, 0.0)
                dp = lax.dot_general(v_ref[0, rows, sl], do[hh], NT, preferred_element_type=F32)
                p_s[hh, rows, :] = p
                dp_s[hh, rows, :] = dp
                out.append(dsums[hh] + jnp.sum(p * dp, axis=0, keepdims=True))
            return tuple(out)

        dsums = _key_blocks(i, probs, (jnp.zeros((1, tq), F32),) * 2)

        def grads(j, dqs, diagonal, width):
            rows = pl.ds(pl.multiple_of(j * tq, tq), width * tq)
            out = []
            for hh, sl in enumerate(HEADS):
                p = p_s[hh, rows, :]
                ds = p * (dp_s[hh, rows, :] - dsums[hh])
                dc_s[hh, rows, :] -= jnp.sum(ds, axis=1, keepdims=True)
                dsb = ds.astype(BF16)
                dk_s[rows, sl] += jnp.dot(dsb, q8[hh], preferred_element_type=F32)
                dv_s[rows, sl] += jnp.dot(p.astype(BF16), do[hh], preferred_element_type=F32)
                out.append(dqs[hh] + lax.dot_general(k_ref[0, rows, sl], dsb, TN, preferred_element_type=F32))
            return tuple(out)

        dqs = _key_blocks(i, grads, (jnp.zeros((HEAD_DIM, tq), F32),) * 2)
        dq_ref[0] = (jnp.concatenate(dqs, axis=0).T * ATT_SCALE).astype(BF16)

        @pl.when(i == nq - 1)
        def _():
            dk_ref[0] = dk_s[...].astype(BF16)
            dv_ref[0] = dv_s[...].astype(BF16)
            dc_ref[...] = jnp.zeros_like(dc_ref)
            for hh, sl in enumerate(HEADS):
                dc_ref[:, sl.start:sl.start + 1] = dc_s[hh]

    qry2 = pl.BlockSpec((tq, LANES), lambda b, hp, i: (b * nq + i, hp))
    seq2 = pl.BlockSpec((seq_len, LANES), lambda b, hp, i: (b, hp))

    def qry3(p):
        return pl.BlockSpec((1, tq, LANES), lambda b, hp, i, p=p: (p, b * nq + i, hp))

    def seq3(p):
        return pl.BlockSpec((1, seq_len, LANES), lambda b, hp, i, p=p: (p, b, hp))

    act = jax.ShapeDtypeStruct((1, T, D), BF16)
    return pl.pallas_call(
        body, name=name, grid=(n_seq, N_HEADS // 2, nq),
        in_specs=[qry2, qry3(0), qry3(1), qry2, pl.BlockSpec((1, 1, 1, 2, tq), lambda b, hp, i: (b, hp, i, 0, 0)), seq2,
                  seq3(0), seq3(1)],
        out_specs=[qry3(0), qry3(0), seq3(0), seq3(0), seq2],
        out_shape=[act, act, act, act, jax.ShapeDtypeStruct((T, D), F32)],
        scratch_shapes=[pltpu.VMEM((2, seq_len, tq), F32), pltpu.VMEM((2, seq_len, tq), F32),
                        pltpu.VMEM((seq_len, LANES), F32), pltpu.VMEM((seq_len, LANES), F32),
                        pltpu.VMEM((2, seq_len, 1), F32)],
        compiler_params=_params(3),
    )(dz, qg, qg, o, lse, cb, kv, kv)


def loss_head(y, target, name):
    T = y.shape[0]
    tm = _tile(T, TM, 8)

    def body(y_ref, t_ref, dy_ref, loss_ref):
        err = y_ref[...] - t_ref[...]
        dy_ref[...] = err * (1.0 / D)
        part = 0.5 * jnp.sum(jnp.mean(err * err, axis=-1, keepdims=True), axis=0, keepdims=True)
        _accumulate(loss_ref, pl.program_id(0) == 0, jnp.broadcast_to(part, (1, LANES)))

    row = pl.BlockSpec((tm, D), lambda i: (i, 0))
    return pl.pallas_call(
        body, name=name, grid=(T // tm,), in_specs=[row, row],
        out_specs=[row, pl.BlockSpec((1, LANES), lambda i: (0, 0))],
        out_shape=[jax.ShapeDtypeStruct((T, D), F32), jax.ShapeDtypeStruct((1, LANES), F32)],
        compiler_params=_params(1),
    )(y, target)


def _adamw(w, g, m, v):
    m = ADAM_B1 * m + (1.0 - ADAM_B1) * g
    v = ADAM_B2 * v + (1.0 - ADAM_B2) * (g * g)
    m_hat = m / (1.0 - ADAM_B1 ** ADAM_STEP)
    v_hat = v / (1.0 - ADAM_B2 ** ADAM_STEP)
    delta = -ADAM_LR * (m_hat / (jnp.sqrt(v_hat) + ADAM_EPS) + ADAM_WD * w)
    return delta, m, v


def adam_sharded(w, m, v, contribs, me, name):
    L, R, C = w.shape
    tr = _tile(R, 256, 16) if R % 16 == 0 else R

    def body(me_ref, w_ref, m_ref, v_ref, *refs):
        c_refs, (g_ref, d_ref, nm_ref, nv_ref) = refs[:2 * L], refs[2 * L:]
        l = pl.program_id(0)
        for j in range(L):
            @pl.when(l == j)
            def _(j=j):
                own_ref, recv_ref = c_refs[2 * j], c_refs[2 * j + 1]
                g = own_ref[0].astype(F32)
                for k in range(N_DEV - 1):
                    g = g + recv_ref[k].astype(F32)
                delta, nm, nv = _adamw(w_ref[0], g, m_ref[0], v_ref[0])
                g_ref[0] = g
                d_ref[0] = delta
                nm_ref[0] = nm
                nv_ref[0] = nv

    blk = pl.BlockSpec((1, tr, C), lambda l, i, s: (l, i, 0))
    in_specs = [blk, blk, blk]
    inputs = [w, m, v]
    for j, (mine, recv) in enumerate(contribs):
        in_specs.append(pl.BlockSpec((1, tr, C), lambda l, i, s, j=j: (s[0], jnp.where(l == j, i, 0), 0)))
        in_specs.append(pl.BlockSpec((N_DEV - 1, tr, C), lambda l, i, s, j=j: (0, jnp.where(l == j, i, 0), 0)))
        inputs += [mine, recv]
    shp = jax.ShapeDtypeStruct((L, R, C), F32)
    grid_spec = pltpu.PrefetchScalarGridSpec(num_scalar_prefetch=1, grid=(L, R // tr), in_specs=in_specs, out_specs=[blk] * 4)
    return pl.pallas_call(body, name=name, grid_spec=grid_spec, out_shape=[shp] * 4, compiler_params=_params(2))(me, *inputs)


def cast_place(w, l, me, dtype, after, name):
    _, R, C = w.shape
    tr = _tile(R, 512, 16) if R % 16 == 0 else R

    def body(me_ref, w_ref, after_ref, o_ref):
        o_ref[0] = w_ref[0].astype(dtype)

    grid_spec = pltpu.PrefetchScalarGridSpec(
        num_scalar_prefetch=1, grid=(R // tr,), in_specs=[pl.BlockSpec((1, tr, C), lambda i, s: (l, i, 0)), ANY],
        out_specs=pl.BlockSpec((1, tr, C), lambda i, s: (s[0], i, 0)))
    return pl.pallas_call(body, name=name, grid_spec=grid_spec, out_shape=jax.ShapeDtypeStruct((N_DEV, R, C), dtype),
                          compiler_params=_params(1))(me, w, after)


def adam_small(params, total, extra_grads, name):
    n, ne = len(params), len(extra_grads)

    def body(*refs):
        total_ref, extra_refs = refs[0], refs[1:1 + ne]
        ins, outs = refs[1 + ne:1 + ne + 3 * n], refs[1 + ne + 3 * n:]
        for k, (_, _, _, where) in enumerate(params):
            if isinstance(where, int):
                g = extra_refs[where][...]
            else:
                row, rows, width = where
                g = total_ref[row:row + rows, 0:width]
            delta, nm, nv = _adamw(ins[3 * k][...], g, ins[3 * k + 1][...], ins[3 * k + 2][...])
            outs[4 * k][...] = g
            outs[4 * k + 1][...] = delta
            outs[4 * k + 2][...] = nm
            outs[4 * k + 3][...] = nv

    flat = [a for w, m, v, _ in params for a in (w, m, v)]
    out_shape = [jax.ShapeDtypeStruct(w.shape, F32) for w, _, _, _ in params for _ in range(4)]
    res = pl.pallas_call(body, name=name, out_shape=out_shape)(total, *extra_grads, *flat)
    return [res[4 * k:4 * k + 4] for k in range(n)]


def _place():
    return lax.axis_index("x"), lax.axis_index("y"), lax.axis_index("c")


def _peer(place, k):
    x, y, c = place
    return x ^ (k >> 2), y ^ ((k >> 1) & 1), c ^ (k & 1)


ANY = pl.BlockSpec(memory_space=pl.ANY)
HBM = pl.BlockSpec(memory_space=pltpu.HBM)
SEM = pl.BlockSpec(memory_space=pltpu.SEMAPHORE)
EFFECT = pltpu.SideEffectType.DATAFLOW_SIDE_EFFECTING


def _in_hbm(a):
    return pltpu.with_memory_space_constraint(a, pltpu.HBM)


def _number(place):
    return 4 * place[0] + 2 * place[1] + place[2]


SLOTS = {"gather_like": 4, "gather_pass": 3, "scatter": 7}


def _plan(mode, src_ref, land_ref, place):
    if mode == "gather_like":
        mine = land_ref.at[_number(place)]
        return [(mine, mine, _peer(place, k)) for k in (1, 2, 4, 6)]
    if mode == "gather_pass":
        slots = [land_ref.at[_number(_peer(place, k))] for k in (2, 4, 6)]
        return [(slot, slot, _peer(place, 1)) for slot in slots]
    return [(src_ref.at[_number(_peer(place, k))], land_ref.at[k - 1], _peer(place, k)) for k in range(1, N_DEV)]


def _exchange(name, groups, start, afters):
    sizes = [len(g[3]) for g in groups]
    na, ng = sum(sizes), len(groups)
    ns = sum(len(g[2]) for g in groups)
    waits = groups[0][0] is not None
    n_in_sems = 2 * ng if waits else 0
    n_out_sems = 2 * ng if start else 0

    def body(*refs):
        src_refs, land_refs = (refs[:ns] if ns else [None] * na), refs[ns:ns + na]
        in_sems = refs[ns + na:ns + na + n_in_sems]
        outs = refs[ns + na + n_in_sems + len(afters):]
        place = _place()
        a = 0
        for gi, n in enumerate(sizes):
            for idx in range(n):
                if waits:
                    zone = land_refs[a].at[pl.ds(0, groups[gi][4])]
                    copy = pltpu.make_async_remote_copy(
                        src_ref=zone, dst_ref=zone, send_sem=in_sems[2 * gi].at[idx], recv_sem=in_sems[2 * gi + 1].at[idx],
                        device_id=_peer(place, 1), device_id_type=MESH)
                    copy.wait_send()
                    copy.wait_recv()
                if start:
                    for src, dst, peer in _plan(start, src_refs[a], land_refs[a], place):
                        pltpu.make_async_remote_copy(
                            src_ref=src, dst_ref=dst, send_sem=outs[2 * gi].at[idx], recv_sem=outs[2 * gi + 1].at[idx],
                            device_id=peer, device_id_type=MESH).start()
                a += 1
        if start:
            outs[-1][...] = jnp.zeros_like(outs[-1])

    srcs = [_in_hbm(s) for g in groups for s in g[2]]
    lands = [_in_hbm(l) for g in groups for l in g[3]]
    sems = [s for g in groups for s in g[:2]] if waits else []
    out_shape = [pltpu.SemaphoreType.DMA((n,)) for n in sizes for _ in range(2)] if start else []
    out_shape += [pltpu.HBM(a.shape, a.dtype) for a in srcs + lands]
    out_specs = [SEM] * n_out_sems + [HBM] * (ns + na)
    if start:
        out_shape.append(jax.ShapeDtypeStruct((8, LANES), F32))
        out_specs.append(pl.BlockSpec(memory_space=pltpu.VMEM))
    res = pl.pallas_call(
        body, name=name, in_specs=[HBM] * (ns + na) + [SEM] * n_in_sems + [ANY] * len(afters),
        out_shape=out_shape, out_specs=out_specs,
        input_output_aliases={i: n_out_sems + i for i in range(ns + na)},
        compiler_params=pltpu.CompilerParams(has_side_effects=EFFECT),
    )(*srcs, *lands, *sems, *afters)
    new_sems, thru = res[:n_out_sems], res[n_out_sems:n_out_sems + ns + na]
    out, a = [], 0
    for gi, n in enumerate(sizes):
        pair = (new_sems[2 * gi], new_sems[2 * gi + 1]) if start else (None, None)
        out.append(pair + (thru[a:a + n] if ns else [], thru[ns + a:ns + a + n], SLOTS.get(start, 0)))
        a += n
    return out, (res[-1] if start else None)


def exchange_start(pair_groups, mode, name):
    groups = [(None, None, [s for s, _ in g if s is not None], [l for _, l in g], 0) for g in pair_groups]
    return _exchange(name, groups, mode, ())


def exchange_relay(groups, mode, afters, name):
    return _exchange(name, groups, mode, afters)


def exchange_wait(groups, afters, name):
    done, _ = _exchange(name, groups, None, afters)
    return [(g[2], g[3]) for g in done]


def all_reduce_small(parts, n_rows, after, name):
    R = n_rows
    n_parts = len(parts)

    def body(*refs):
        part_refs = refs[:n_parts]
        out_ref, buf, send_sems, recv_sems = refs[n_parts + 1:]
        x, y, c = _place()
        me = 4 * x + 2 * y + c
        own = buf.at[me]
        own[...] = jnp.zeros((R, D), F32)
        for ref, (arr, row) in zip(part_refs, parts):
            own[row:row + arr.shape[0], 0:arr.shape[1]] = ref[...]
        copies = []
        for k in range(1, N_DEV):
            peer = (x ^ (k >> 2), y ^ ((k >> 1) & 1), c ^ (k & 1))
            copies.append(pltpu.make_async_remote_copy(
                src_ref=own, dst_ref=own, send_sem=send_sems.at[k - 1], recv_sem=recv_sems.at[k - 1],
                device_id=peer, device_id_type=MESH))
        for cp in copies:
            cp.start()
        for cp in copies:
            cp.wait()
        total = buf[0]
        for d in range(1, N_DEV):
            total = total + buf[d]
        out_ref[...] = total

    vm = pl.BlockSpec(memory_space=pltpu.VMEM)
    return pl.pallas_call(
        body, name=name, in_specs=[vm] * n_parts + [ANY], out_specs=vm, out_shape=jax.ShapeDtypeStruct((R, D), F32),
        scratch_shapes=[pltpu.VMEM((N_DEV, R, D), F32), pltpu.SemaphoreType.DMA((N_DEV - 1,)),
                        pltpu.SemaphoreType.DMA((N_DEV - 1,))],
    )(*[arr for arr, _ in parts], after)


def _col_blocks(gathered, n_blocks):
    n, d, w = gathered.shape
    whole = gathered.transpose(1, 0, 2).reshape(d, n * w)
    return whole.reshape(d, n_blocks, n * w // n_blocks).transpose(1, 0, 2)[:, None]


def _col_shards(blocks):
    n, d, w = blocks.shape
    whole = blocks.transpose(1, 0, 2).reshape(d, n * w)
    return whole.reshape(d, N_DEV, n * w // N_DEV).transpose(1, 0, 2)


def kernel(x, ffn1_pre_g, ffn1_post_g, ffn1_w_in, ffn1_w_out, mix_pre_g, mix_post_g, ffn2_pre_g, ffn2_post_g, ffn2_w_in, ffn2_w_out, conv_w_in, conv_k, conv_w_out, kv_g, kv_w, forget_b, attn_w_qg, attn_w_o, loss_target, m_ffn1_pre_g, m_ffn1_post_g, m_ffn1_w_in, m_ffn1_w_out, m_mix_pre_g, m_mix_post_g, m_ffn2_pre_g, m_ffn2_post_g, m_ffn2_w_in, m_ffn2_w_out, m_conv_w_in, m_conv_k, m_conv_w_out, m_kv_g, m_kv_w, m_forget_b, m_attn_w_qg, m_attn_w_o, v_ffn1_pre_g, v_ffn1_post_g, v_ffn1_w_in, v_ffn1_w_out, v_mix_pre_g, v_mix_post_g, v_ffn2_pre_g, v_ffn2_post_g, v_ffn2_w_in, v_ffn2_w_out, v_conv_w_in, v_conv_k, v_conv_w_out, v_kv_g, v_kv_w, v_forget_b, v_attn_w_qg, v_attn_w_o):
    n_seq, seq_len, _ = x.shape
    T = n_seq * seq_len
    xi, yi, ci = _place()
    dev = 4 * xi + 2 * yi + ci
    x0 = x.reshape(T, D)
    target = loss_target.reshape(T, D)

    me = dev.reshape(1).astype(jnp.int32)
    w1_t, w2_t, kv_t = ffn1_w_in.transpose(0, 2, 1), ffn2_w_in.transpose(0, 2, 1), kv_w.T[None]

    def zones(specs, after):
        return [(None, cast_place(w, l, me, dt, after, f"place_{nm}")) for nm, w, l, dt in specs]

    gathers, token = exchange_start([zones([("w1_in0", w1_t, 0, BF16)], me)], "gather_like", "gather_start0")
    shard_groups = [
        [("w1_out0", ffn1_w_out, 0, BF16)],
        [("cw_in", conv_w_in, 0, BF16), ("cw_out", conv_w_out, 0, BF16), ("ck", conv_k, 0, F32)],
        [("w2_in0", w2_t, 0, BF16), ("w2_out0", ffn2_w_out, 0, BF16)],
        [("kv", kv_t, 0, BF16), ("w1_in1", w1_t, 1, BF16), ("w1_out1", ffn1_w_out, 1, BF16),
         ("qg", attn_w_qg, 0, BF16), ("ow", attn_w_o, 0, BF16)],
        [("w2_in1", w2_t, 1, BF16), ("w2_out1", ffn2_w_out, 1, BF16)]]
    later, token = exchange_start([zones(g, token) for g in shard_groups], "gather_like", "gather_start1")
    gathers = gathers + later

    def gathered(k, after):
        passed, _ = exchange_relay([gathers[k]], "gather_pass", [after], f"gather_pass{k}")
        return [z[:, None] for z in exchange_wait(passed, [after], f"gather_wait{k}")[0][1]]

    fb = jnp.pad(forget_b, (0, LANES - N_HEADS))[None]

    def vec(g, l):
        return g[l:l + 1]

    def behind(g, tok):
        return g + tok[:1, :1]

    grads_small = {}
    tm_ffn = _tile(T, TM, 16)

    def ffn_fwd(xin, g_pre, g_post, w_in, w_out, tag):
        w_gu = w_in.reshape(2, 1, -1, D)
        xn, (gu,) = rms_proj(xin, g_pre, [(w_gu, 0)], [BF16], f"{tag}_in", transposed=True)
        specs = [pl.BlockSpec((2, tm_ffn, gu.shape[2]), lambda i: (0, i, 0))]
        if callable(w_out):
            w_out = w_out(xn)
        a, h, y = mix_out(_swiglu_pro, [gu], specs, (w_out, 0), xin, g_post, 0.5, 1, f"{tag}_out", tm=tm_ffn)
        return y, (xin, xn, gu, a, h), w_out

    def ffn_bwd(dy, saved, g_pre, g_post, w_in, w_out, tag):
        xin, xn, gu, a, h = saved
        w_gu = w_in.reshape(2, 1, -1, D)
        half = gu.shape[2] // 2
        dh, dg_post, dgu = post_bwd(dy, h, g_post, 0.5, (w_out, 0), 1, f"{tag}_bwd_out", gu=gu, tm=tm_ffn)
        dw_out = wgrad([a], [dh[None]], 2, (2, 1, half, D), f"{tag}_dw_out", a_cols=half)
        started_out, tok = scatter_start([dw_out.reshape(8, -1, D)], f"{tag}_out_scatter_start")
        dw_in = wgrad([dgu], [xn[None]], 4, (4, 1, half, D), f"{tag}_dw_in", a_cols=half, after=tok)
        started_in, tok = scatter_start([dw_in.reshape(8, -1, D)], f"{tag}_in_scatter_start")
        dx, dg_pre = pre_bwd([(dgu, (w_gu, 0), 0)], xin, behind(g_pre, tok), dy, f"{tag}_bwd_in", transposed=True)
        return dx, dg_pre, dg_post, [started_in, started_out]

    def scatter_start(blocked, name):
        pairs = [(g, lax.empty((N_DEV - 1,) + g.shape[1:], g.dtype)) for g in blocked]
        started, tok = exchange_start([pairs], "scatter", name)
        return started[0], tok

    (w1_in,) = gathered(0, token)
    x1, s_f1a, w1_out = ffn_fwd(x0, vec(ffn1_pre_g, 0), vec(ffn1_post_g, 0), w1_in, lambda after: gathered(1, after)[0],
                                "l0_ffn1")
    cw_in_g, cw_out, ck_g = gathered(2, x1)
    cw_in = _col_blocks(cw_in_g[:, 0], 3)
    ck = ck_g[:, 0].transpose(1, 0, 2).reshape(3, D)
    xn_c, (bch,) = rms_proj(x1, vec(mix_pre_g, 0), [(cw_in, 0)], [BF16], "conv_in")
    tmc = _tile(seq_len, TM, 16)
    hb = tmc // HALO

    def cpiece(p):
        return pl.BlockSpec((1, tmc, D), lambda i, p=p: (p, i, 0))

    def chalo(p):
        return pl.BlockSpec((1, HALO, D), lambda i, p=p: (p, jnp.maximum(i * hb - 1, 0), 0))

    conv_specs = [cpiece(0), cpiece(1), cpiece(2), chalo(1), chalo(2), pl.BlockSpec((3, D), lambda i: (0, 0))]
    z_c, m_c, x2 = mix_out(_make_conv_pro(seq_len // tmc), [bch, bch, bch, bch, bch, ck], conv_specs, (cw_out, 0),
                           x1, vec(mix_post_g, 0), 1.0, 1, "conv_out", tm=tmc)
    w2_in, w2_out = gathered(3, x2)
    x3, s_f2a, _ = ffn_fwd(x2, vec(ffn2_pre_g, 0), vec(ffn2_post_g, 0), w2_in, w2_out, "l0_ffn2")

    kvw_g, w1_in_b, w1_out_b, qgw_g, ow = gathered(4, x3)
    qg_w = _col_blocks(qgw_g[:, 0], 2)
    kv_whole = kvw_g.reshape(2 * D + N_HEADS, D).T
    kv_wb = kv_whole[:, :2 * D].reshape(D, 2, D).transpose(1, 0, 2)[:, None]
    f_w = jnp.pad(kv_whole[:, 2 * D:], ((0, 0), (0, LANES - N_HEADS)))[None, None]
    xn_kv, (kv, fl) = rms_proj(x3, kv_g[None], [(kv_wb, 0), (f_w, 0)], [BF16, F32], "kv_in")
    cb = forget_fwd(fl, fb, seq_len, "forget_fwd")

    x4, s_f1b, _ = ffn_fwd(x3, vec(ffn1_pre_g, 1), vec(ffn1_post_g, 1), w1_in_b, w1_out_b, "l1_ffn1")
    xn_a, (qg,) = rms_proj(x4, vec(mix_pre_g, 1), [(qg_w, 0)], [BF16], "attn_in")
    o, lse = attn_fwd(qg, kv, cb, n_seq, seq_len, "attn_fwd")
    tm = _tile(T, TM, 16)
    gate_specs = [pl.BlockSpec((1, tm, D), lambda i: (1, i, 0)), pl.BlockSpec((tm, D), lambda i: (i, 0))]
    z_a, m_a, x5 = mix_out(_attn_gate_pro, [qg, o], gate_specs, (ow, 0), x4, vec(mix_post_g, 1), 1.0, 1, "attn_out")
    w2_in_b, w2_out_b = gathered(5, x5)
    x6, s_f2b, _ = ffn_fwd(x5, vec(ffn2_pre_g, 1), vec(ffn2_post_g, 1), w2_in_b, w2_out_b, "l1_ffn2")

    dy, loss_part = loss_head(x6, target, "loss_head")

    scatters = {}
    dx5, dg, dgp, scatters["ffn2", 1] = ffn_bwd(dy, s_f2b, vec(ffn2_pre_g, 1), vec(ffn2_post_g, 1), w2_in_b, w2_out_b, "l1_ffn2")
    grads_small["ffn2_pre", 1], grads_small["ffn2_post", 1] = dg, dgp
    dm_a, dgp, dz_a = post_bwd(dx5, m_a, vec(mix_post_g, 1), 1.0, (ow, 0), 1, "attn_bwd_out")
    grads_small["mix_post", 1] = dgp
    dq, dgate, dk, dv, dcb = attn_bwd(dz_a[0], qg, kv, o, lse, cb, n_seq, seq_len, "attn_bwd")
    dx4, dg = pre_bwd([(dq, (qg_w, 0), 0), (dgate, (qg_w, 0), 1)], x4, vec(mix_pre_g, 1), dx5, "attn_bwd_in")
    grads_small["mix_pre", 1] = dg
    d_ow = wgrad([z_a], [dm_a[None]], 1, ow.shape, "attn_dw_o")
    d_qgw = wgrad([xn_a[None]], [dq, dgate], 2, (2, 1, D, D), "attn_dw_qg")
    scatters["attn"], tok = scatter_start([_col_shards(d_qgw[:, 0]), d_ow.reshape(8, -1, D)], "attn_scatter_start")
    dx3, dg, dgp, scatters["ffn1", 1] = ffn_bwd(dx4, s_f1b, vec(ffn1_pre_g, 1), behind(vec(ffn1_post_g, 1), tok),
                                                w1_in_b, w1_out_b, "l1_ffn1")
    grads_small["ffn1_pre", 1], grads_small["ffn1_post", 1] = dg, dgp

    dfl, dfb = forget_bwd(dcb, fl, fb, seq_len, "forget_bwd")
    dx3, dg_kv = pre_bwd([(dk, (kv_wb, 0), 0), (dv, (kv_wb, 0), 1), (dfl, (f_w, 0), 0)], x3, kv_g[None], dx3, "kv_bwd_in")
    d_kvw = wgrad([xn_kv[None]], [dk, dv], 2, (2, 1, D, D), "kv_dw")
    d_fw = wgrad([xn_kv[None]], [dfl.astype(BF16)], 1, (1, 1, D, LANES), "forget_dw")
    d_kv_whole = jnp.concatenate([d_kvw[0, 0], d_kvw[1, 0], d_fw[0, 0, :, :N_HEADS]], axis=1)
    wshard = D * 2 + N_HEADS
    scatters["kv"], tok = scatter_start([d_kv_whole.T.reshape(N_DEV, wshard // N_DEV, D)], "kv_scatter_start")

    dx2, dg, dgp, scatters["ffn2", 0] = ffn_bwd(dx3, s_f2a, vec(ffn2_pre_g, 0), behind(vec(ffn2_post_g, 0), tok),
                                                w2_in, w2_out, "l0_ffn2")
    grads_small["ffn2_pre", 0], grads_small["ffn2_post", 0] = dg, dgp
    dm_c, dgp, dz_c = post_bwd(dx2, m_c, vec(mix_post_g, 0), 1.0, (cw_out, 0), 1, "conv_bwd_out")
    grads_small["mix_post", 0] = dgp
    dbch, d_ck = conv_bwd_mix(dz_c, bch, ck, seq_len, "conv_bwd_mix")
    dx1, dg = pre_bwd([(dbch, (cw_in, 0), 0)], x1, vec(mix_pre_g, 0), dx2, "conv_bwd_in")
    grads_small["mix_pre", 0] = dg
    d_cw_out = wgrad([z_c], [dm_c[None]], 1, cw_out.shape, "conv_dw_out")
    d_cw_in = wgrad([xn_c[None]], [dbch], 3, (3, 1, D, D), "conv_dw_in")
    scatters["conv"], tok = scatter_start([_col_shards(d_cw_in[:, 0]), d_cw_out.reshape(8, -1, D)], "conv_scatter_start")
    dx0, dg, dgp, scatters["ffn1", 0] = ffn_bwd(dx1, s_f1a, vec(ffn1_pre_g, 0), behind(vec(ffn1_post_g, 0), tok),
                                                w1_in, w1_out, "l0_ffn1")
    grads_small["ffn1_pre", 0], grads_small["ffn1_post", 0] = dg, dgp

    parts_of = {}

    def scatter_end(keys, afters, name):
        flat = [(k, g) for k in keys for g in (scatters[k] if isinstance(scatters[k], list) else [scatters[k]])]
        for (k, _), (sent, recv) in zip(flat, exchange_wait([g for _, g in flat], afters, name)):
            parts_of.setdefault(k, []).extend(zip(sent, recv))

    scatter_end([("ffn2", 1), "attn", ("ffn1", 1), "kv", ("ffn2", 0), "conv"], [dx0], "scatter_wait")
    sharded = {"ffn2_w_in": (ffn2_w_in, m_ffn2_w_in, v_ffn2_w_in), "ffn2_w_out": (ffn2_w_out, m_ffn2_w_out, v_ffn2_w_out),
               "conv_w_in": (conv_w_in, m_conv_w_in, v_conv_w_in), "conv_w_out": (conv_w_out, m_conv_w_out, v_conv_w_out),
               "kv_w": (kv_w, m_kv_w, v_kv_w), "attn_w_qg": (attn_w_qg, m_attn_w_qg, v_attn_w_qg),
               "attn_w_o": (attn_w_o, m_attn_w_o, v_attn_w_o),
               "ffn1_w_in": (ffn1_w_in, m_ffn1_w_in, v_ffn1_w_in), "ffn1_w_out": (ffn1_w_out, m_ffn1_w_out, v_ffn1_w_out)}
    out = {}
    for nm, (w, mm, vv) in sharded.items():
        if nm == "ffn1_w_in":
            scatter_end([("ffn1", 0)], [res[0] for res in out.values()], "scatter_wait_last")
        contribs = {
            "ffn1_w_in": lambda: [parts_of["ffn1", 0][0], parts_of["ffn1", 1][0]],
            "ffn1_w_out": lambda: [parts_of["ffn1", 0][1], parts_of["ffn1", 1][1]],
            "ffn2_w_in": lambda: [parts_of["ffn2", 0][0], parts_of["ffn2", 1][0]],
            "ffn2_w_out": lambda: [parts_of["ffn2", 0][1], parts_of["ffn2", 1][1]],
            "conv_w_in": lambda: [parts_of["conv"][0]], "conv_w_out": lambda: [parts_of["conv"][1]],
            "kv_w": lambda: [parts_of["kv"][0]], "attn_w_qg": lambda: [parts_of["attn"][0]],
            "attn_w_o": lambda: [parts_of["attn"][1]]}[nm]()
        if nm in ("ffn1_w_in", "ffn2_w_in", "kv_w"):
            rows, cols = w.shape[-2:]

            def view(a):
                return a.reshape(-1, rows, cols).transpose(0, 2, 1)

            res = adam_sharded(view(w), view(mm), view(vv), contribs, me, f"adam_{nm}")
            out[nm] = [r.transpose(0, 2, 1).reshape(w.shape) for r in res]
        else:
            shape3 = (len(contribs),) + contribs[0][0].shape[1:]
            res = adam_sharded(w.reshape(shape3), mm.reshape(shape3), vv.reshape(shape3), contribs, me, f"adam_{nm}")
            out[nm] = [r.reshape(w.shape) for r in res]

    small_names = ["ffn1_pre", "ffn1_post", "mix_pre", "mix_post", "ffn2_pre", "ffn2_post"]
    parts = [(grads_small[n, l], 2 * k + l) for k, n in enumerate(small_names) for l in range(2)]
    parts += [(dg_kv, 12), (dfb, 13), (d_ck, 14), (loss_part, 17)]
    total = all_reduce_small(parts, 24, out["ffn1_w_in"][0], "all_reduce_small")
    loss = total[17, 0]
    d_ck_mine = lax.dynamic_slice(total, (14, dev * LANES), (3, LANES))
    gains = [(ffn1_pre_g, m_ffn1_pre_g, v_ffn1_pre_g), (ffn1_post_g, m_ffn1_post_g, v_ffn1_post_g),
             (mix_pre_g, m_mix_pre_g, v_mix_pre_g), (mix_post_g, m_mix_post_g, v_mix_post_g),
             (ffn2_pre_g, m_ffn2_pre_g, v_ffn2_pre_g), (ffn2_post_g, m_ffn2_post_g, v_ffn2_post_g)]
    small_params = [(w, m, v, (2 * k, 2, D)) for k, (w, m, v) in enumerate(gains)]
    small_params += [(kv_g[None], m_kv_g[None], v_kv_g[None], (12, 1, D)),
                     (forget_b[None], m_forget_b[None], v_forget_b[None], (13, 1, N_HEADS)),
                     (conv_k[0], m_conv_k[0], v_conv_k[0], 0)]
    small_res = adam_small(small_params, total, [d_ck_mine], "adam_small")
    small_keys = [n + "_g" for n in small_names] + ["kv_g", "forget_b", "conv_k"]
    shapes = {"kv_g": kv_g.shape, "forget_b": forget_b.shape, "conv_k": conv_k.shape}
    small = [{key: res[kind].reshape(shapes.get(key, res[kind].shape)) for key, res in zip(small_keys, small_res)}
             for kind in range(4)]
    order = ["ffn1_pre_g", "ffn1_post_g", "ffn1_w_in", "ffn1_w_out", "mix_pre_g", "mix_post_g", "ffn2_pre_g", "ffn2_post_g",
             "ffn2_w_in", "ffn2_w_out", "conv_w_in", "conv_k", "conv_w_out", "kv_g", "kv_w", "forget_b", "attn_w_qg",
             "attn_w_o"]
    results = [loss, dx0.reshape(x.shape)]
    for kind in range(4):
        for nm in order:
            results.append(out[nm][kind] if nm in out else small[kind][nm])
    return tuple(results)
```

```python
import functools
import math

import jax
import jax.numpy as jnp
from jax import lax
from jax.experimental import pallas as pl
from jax.experimental.pallas import tpu as pltpu

F32, BF16 = jnp.float32, jnp.bfloat16
D = 1024
N_HEADS = 16
HEAD_DIM = 64
N_DEV = 8
RMS_EPS = 1e-6
ATT_SCALE = 1.0 / math.sqrt(HEAD_DIM)
LANES = 128
HALO = 8
TM = 512
TQ = 512
VMEM_LIMIT = 48 * 1024 * 1024
MESH = pl.DeviceIdType.MESH

ADAM_LR, ADAM_B1, ADAM_B2, ADAM_EPS, ADAM_WD, ADAM_STEP = 0.001, 0.9, 0.999, 1e-08, 0.01, 10

NT = (((1,), (1,)), ((), ()))
TN = (((0,), (0,)), ((), ()))


def _params(n_axes, vmem_limit=VMEM_LIMIT):
    return pltpu.CompilerParams(dimension_semantics=("arbitrary",) * n_axes, vmem_limit_bytes=vmem_limit)


def _tile(n, cap, mult):
    best = None
    for t in range(mult, min(n, cap) + 1, mult):
        if n % t == 0:
            best = t
    assert best is not None, (n, cap, mult)
    return best


def _rms_rstd(x):
    return lax.rsqrt(jnp.mean(x * x, axis=-1, keepdims=True) + RMS_EPS)


def _rms_fwd(x, g):
    return x * _rms_rstd(x) * g


def _rms_bwd(x, g, dy):
    xh = x * _rms_rstd(x)
    dyg = dy * g
    dx = _rms_rstd(x) * (dyg - xh * jnp.mean(dyg * xh, axis=-1, keepdims=True))
    return dx, jnp.sum(dy * xh, axis=0, keepdims=True)


def _accumulate(ref, first, value):
    @pl.when(first)
    def _():
        ref[...] = value

    @pl.when(jnp.logical_not(first))
    def _():
        ref[...] += value


def rms_proj(x, g, ws, out_dtypes, name, transposed=False):
    T = x.shape[0]
    tm = _tile(T, TM, 16)
    na = len(ws)

    def body(x_ref, g_ref, *refs):
        w_refs, xn_ref, o_refs = refs[:na], refs[na], refs[na + 1:]
        xn = _rms_fwd(x_ref[...], g_ref[...]).astype(BF16)
        xn_ref[...] = xn
        for a, (w, l) in enumerate(ws):
            for p in range(w.shape[0]):
                if transposed:
                    y = lax.dot_general(xn, w_refs[a][p, l], NT, preferred_element_type=F32)
                else:
                    y = jnp.dot(xn, w_refs[a][p, l], preferred_element_type=F32)
                o_refs[a][p] = y.astype(o_refs[a].dtype)

    in_specs = [pl.BlockSpec((tm, D), lambda i: (i, 0)), pl.BlockSpec((1, D), lambda i: (0, 0))]
    in_specs += [pl.BlockSpec(w.shape, lambda i: (0, 0, 0, 0), pipeline_mode=pl.Buffered(1)) for w, _ in ws]
    out_specs = [pl.BlockSpec((tm, D), lambda i: (i, 0))]
    out_shape = [jax.ShapeDtypeStruct((T, D), BF16)]
    for (w, _), dt in zip(ws, out_dtypes):
        nb, wb = w.shape[0], w.shape[2 if transposed else 3]
        out_specs.append(pl.BlockSpec((nb, tm, wb), lambda i: (0, i, 0)))
        out_shape.append(jax.ShapeDtypeStruct((nb, T, wb), dt))
    res = pl.pallas_call(
        body, name=name, grid=(T // tm,), in_specs=in_specs, out_specs=out_specs, out_shape=out_shape,
        compiler_params=_params(1),
    )(x, g, *[w for w, _ in ws])
    return res[0], res[1:]


def mix_out(pro, pro_inputs, pro_specs, w, res, g_post, alpha, nk, name, tm=None):
    w4, l = w
    T = res.shape[0]
    tm = _tile(T, TM, 16) if tm is None else tm
    dpb = N_DEV // nk
    rows = w4.shape[2]
    kb = dpb * rows
    npi = len(pro_inputs)

    def body(*refs):
        pro_refs = refs[:npi]
        w_ref, res_ref, g_ref, z_ref, m_ref, y_ref = refs[npi:]
        i = pl.program_id(0)
        m = None
        for k in range(nk):
            z = pro(i, k, *pro_refs).astype(BF16)
            z_ref[k] = z
            part = jnp.dot(z, w_ref[k * dpb:(k + 1) * dpb, l].reshape(kb, D), preferred_element_type=F32)
            m = part if m is None else m + part
        m_ref[...] = m
        y_ref[...] = res_ref[...] + alpha * _rms_fwd(m, g_ref[...])

    row = pl.BlockSpec((tm, D), lambda i: (i, 0))
    in_specs = list(pro_specs) + [
        pl.BlockSpec(w4.shape, lambda i: (0, 0, 0, 0), pipeline_mode=pl.Buffered(1)), row, pl.BlockSpec((1, D), lambda i: (0, 0))]
    z, m, y = pl.pallas_call(
        body, name=name, grid=(T // tm,), in_specs=in_specs,
        out_specs=[pl.BlockSpec((nk, tm, kb), lambda i: (0, i, 0)), row, row],
        out_shape=[jax.ShapeDtypeStruct((nk, T, kb), BF16), jax.ShapeDtypeStruct((T, D), F32),
                   jax.ShapeDtypeStruct((T, D), F32)],
        compiler_params=_params(1),
    )(*pro_inputs, w4, res, g_post)
    return z, m, y


def post_bwd(dy, m, g_post, alpha, w, nk, name, gu=None, tm=None):
    w4, l = w
    T = dy.shape[0]
    tm = _tile(T, TM, 16) if tm is None else tm
    dpb = N_DEV // nk
    rows = w4.shape[2]
    kb = dpb * rows
    swiglu = gu is not None

    def body(dy_ref, m_ref, g_ref, w_ref, *refs):
        if swiglu:
            gu_ref, dm_ref, dg_ref, dgu_ref = refs
        else:
            dm_ref, dg_ref, dz_ref = refs
        dm, dg = _rms_bwd(m_ref[...], g_ref[...], alpha * dy_ref[...])
        dm = dm.astype(BF16)
        dm_ref[...] = dm
        _accumulate(dg_ref, pl.program_id(0) == 0, dg)
        for k in range(nk):
            dz = lax.dot_general(dm, w_ref[k * dpb:(k + 1) * dpb, l].reshape(kb, D), NT, preferred_element_type=F32)
            if swiglu:
                gate, up = gu_ref[k].astype(F32), gu_ref[k + nk].astype(F32)
                sig = jax.nn.sigmoid(gate)
                dgu_ref[k] = (dz * up * sig * (1.0 + gate * (1.0 - sig))).astype(BF16)
                dgu_ref[k + nk] = (dz * gate * sig).astype(BF16)
            else:
                dz_ref[k] = dz.astype(BF16)

    row = pl.BlockSpec((tm, D), lambda i: (i, 0))
    vec = pl.BlockSpec((1, D), lambda i: (0, 0))
    blkk = pl.BlockSpec((nk, tm, kb), lambda i: (0, i, 0))
    in_specs = [row, row, vec, pl.BlockSpec(w4.shape, lambda i: (0, 0, 0, 0), pipeline_mode=pl.Buffered(1))]
    inputs = [dy, m, g_post, w4]
    n_dz = 2 * nk if swiglu else nk
    out_specs = [row, vec, pl.BlockSpec((n_dz, tm, kb), lambda i: (0, i, 0))]
    out_shape = [jax.ShapeDtypeStruct((T, D), BF16), jax.ShapeDtypeStruct((1, D), F32),
                 jax.ShapeDtypeStruct((n_dz, T, kb), BF16)]
    if swiglu:
        in_specs.append(pl.BlockSpec((2 * nk, tm, kb), lambda i: (0, i, 0)))
        inputs.append(gu)
    return pl.pallas_call(
        body, name=name, grid=(T // tm,), in_specs=in_specs, out_specs=out_specs, out_shape=out_shape,
        compiler_params=_params(1),
    )(*inputs)


def pre_bwd(pieces, x, g_pre, dres, name, transposed=False):
    T = x.shape[0]
    tm = _tile(T, TM, 16)
    na = len(pieces)
    weights = []
    for _, (w4, _), _ in pieces:
        if not any(w4 is w for w in weights):
            weights.append(w4)
    nw = len(weights)
    which = [[w4 is w for w in weights].index(True) for _, (w4, _), _ in pieces]

    def body(*refs):
        dz_refs, w_refs = refs[:na], refs[na:na + nw]
        x_ref, g_ref, dres_ref, dx_ref, dg_ref = refs[na + nw:]
        acc = None
        for a, (dz, (_, l), w_off) in enumerate(pieces):
            for p in range(dz.shape[0]):
                w = w_refs[which[a]][w_off + p, l]
                if transposed:
                    part = jnp.dot(dz_refs[a][p].astype(BF16), w, preferred_element_type=F32)
                else:
                    part = lax.dot_general(dz_refs[a][p].astype(BF16), w, NT, preferred_element_type=F32)
                acc = part if acc is None else acc + part
        dx, dg = _rms_bwd(x_ref[...], g_ref[...], acc)
        dx_ref[...] = dres_ref[...] + dx
        _accumulate(dg_ref, pl.program_id(0) == 0, dg)

    row = pl.BlockSpec((tm, D), lambda i: (i, 0))
    vec = pl.BlockSpec((1, D), lambda i: (0, 0))
    dz_specs = [pl.BlockSpec((dz.shape[0], tm, dz.shape[2]), lambda i: (0, i, 0)) for dz, _, _ in pieces]
    w_specs = [pl.BlockSpec(w.shape, lambda i: (0, 0, 0, 0), pipeline_mode=pl.Buffered(1)) for w in weights]
    return pl.pallas_call(
        body, name=name, grid=(T // tm,), in_specs=dz_specs + w_specs + [row, vec, row],
        out_specs=[row, vec], out_shape=[jax.ShapeDtypeStruct((T, D), F32), jax.ShapeDtypeStruct((1, D), F32)],
        compiler_params=_params(1),
    )(*[p[0] for p in pieces], *weights, x, g_pre, dres)


def wgrad(a_list, b_list, nout, out4_shape, name, a_cols=None, tm_cap=None, after=None):
    T = a_list[0].shape[1]
    tm = _tile(T, 4 * TM if tm_cap is None else tm_cap, 16)
    nt = T // tm
    dpb = out4_shape[0] // nout
    _, _, R, C = out4_shape
    na, nb = len(a_list), len(b_list)
    a_w = a_list[0].shape[2] if a_cols is None else a_cols
    a_per = a_list[0].shape[2] // a_w

    def spans(arrs, per):
        ns = [a.shape[0] * per for a in arrs]
        return ns, [sum(ns[:k]) for k in range(len(ns))], sum(ns)

    a_ns, a_offs, a_tot = spans(a_list, a_per)
    b_ns, b_offs, b_tot = spans(b_list, 1)
    assert a_tot in (1, nout) and b_tot in (1, nout)

    def body(*refs):
        a_refs, b_refs = refs[:na], refs[na:na + nb]
        out_ref, acc = refs[-2:]
        p, t = pl.program_id(0), pl.program_id(1)
        for ia in range(na):
            for ib in range(nb):
                conds = []
                if a_tot > 1:
                    conds += [p >= a_offs[ia], p < a_offs[ia] + a_ns[ia]]
                if b_tot > 1:
                    conds += [p >= b_offs[ib], p < b_offs[ib] + b_ns[ib]]

                def work(ia=ia, ib=ib):
                    part = lax.dot_general(a_refs[ia][0], b_refs[ib][0], TN, preferred_element_type=F32)
                    _accumulate(acc, t == 0, part)

                if conds:
                    pl.when(functools.reduce(jnp.logical_and, conds))(work)
                else:
                    work()

        @pl.when(t == nt - 1)
        def _():
            out_ref[...] = acc[...].astype(BF16).reshape(dpb, 1, R, C)

    def blk(off, n, tot):
        if tot == 1:
            return lambda p: 0
        return lambda p: jnp.clip(p - off, 0, n - 1)

    in_specs = []
    for arr, n, off in zip(a_list, a_ns, a_offs):
        in_specs.append(pl.BlockSpec((1, tm, a_w), lambda p, t, f=blk(off, n, a_tot): (f(p) // a_per, t, f(p) % a_per)))
    for arr, n, off in zip(b_list, b_ns, b_offs):
        in_specs.append(pl.BlockSpec((1, tm, arr.shape[2]), lambda p, t, f=blk(off, n, b_tot): (f(p), t, 0)))
    extra = [] if after is None else [after]
    return pl.pallas_call(
        body, name=name, grid=(nout, nt), in_specs=in_specs + [ANY] * len(extra),
        out_specs=pl.BlockSpec((dpb, 1, R, C), lambda p, t: (p, 0, 0, 0)),
        out_shape=jax.ShapeDtypeStruct(out4_shape, BF16),
        scratch_shapes=[pltpu.VMEM((dpb * R, C), F32)], compiler_params=_params(2),
    )(*a_list, *b_list, *extra)


def _swiglu_pro(i, k, gu_ref):
    gate, up = gu_ref[k].astype(F32), gu_ref[k + gu_ref.shape[0] // 2].astype(F32)
    return gate * jax.nn.sigmoid(gate) * up


def _attn_gate_pro(i, k, gate_ref, o_ref):
    return jax.nn.sigmoid(gate_ref[0].astype(F32)) * o_ref[...].astype(F32)


def _shift_rows(u, halo, d):
    rolled = pltpu.roll(u, d, 0)
    row = lax.broadcasted_iota(jnp.int32, u.shape, 0)
    for r in range(d):
        rolled = jnp.where(row == r, halo[HALO - d + r:HALO - d + r + 1, :], rolled)
    return rolled


def _advance_rows(u, halo, d):
    n = u.shape[0]
    rolled = pltpu.roll(u, n - d, 0)
    row = lax.broadcasted_iota(jnp.int32, u.shape, 0)
    for r in range(d):
        rolled = jnp.where(row == n - d + r, halo[r:r + 1, :], rolled)
    return rolled


def _make_conv_pro(tiles_per_seq):
    def pro(i, k, b_ref, c_ref, h_ref, ch_ref, hh_ref, ck_ref):
        u = c_ref[0].astype(F32) * h_ref[0].astype(F32)
        first = (i % tiles_per_seq) == 0
        halo = jnp.where(first, 0.0, ch_ref[0].astype(F32) * hh_ref[0].astype(F32))
        ck = ck_ref[...]
        conv = ck[2:3, :] * u + ck[1:2, :] * _shift_rows(u, halo, 1) + ck[0:1, :] * _shift_rows(u, halo, 2)
        return b_ref[0].astype(F32) * conv
    return pro


def conv_bwd_mix(dz, bch, conv_k, seq_len, name):
    T = dz.shape[1]
    tm = _tile(seq_len, TM, 16)
    tps = seq_len // tm
    nt = T // tm
    hb = tm // HALO

    def body(dz_ref, b_ref, c_ref, h_ref, cp_ref, hp_ref, dzn_ref, bn_ref, ck_ref, dbch_ref, dk_ref):
        i = pl.program_id(0)
        first = (i % tps) == 0
        last = (i % tps) == tps - 1
        b, c, h = b_ref[0].astype(F32), c_ref[0].astype(F32), h_ref[0].astype(F32)
        dzt = dz_ref[0].astype(F32)
        u = c * h
        prev = jnp.where(first, 0.0, cp_ref[0].astype(F32) * hp_ref[0].astype(F32))
        u1, u2 = _shift_rows(u, prev, 1), _shift_rows(u, prev, 2)
        ck = ck_ref[...]
        conv = ck[2:3, :] * u + ck[1:2, :] * u1 + ck[0:1, :] * u2
        dconv = dzt * b
        nxt = jnp.where(last, 0.0, dzn_ref[0].astype(F32) * bn_ref[0].astype(F32))
        du = ck[2:3, :] * dconv + ck[1:2, :] * _advance_rows(dconv, nxt, 1) + ck[0:1, :] * _advance_rows(dconv, nxt, 2)
        dbch_ref[0] = (dzt * conv).astype(BF16)
        dbch_ref[1] = (du * h).astype(BF16)
        dbch_ref[2] = (du * c).astype(BF16)
        tap = lax.broadcasted_iota(jnp.int32, (3, D), 0)
        dk = jnp.where(tap == 0, jnp.sum(dconv * u2, axis=0, keepdims=True),
                       jnp.where(tap == 1, jnp.sum(dconv * u1, axis=0, keepdims=True),
                                 jnp.sum(dconv * u, axis=0, keepdims=True)))
        _accumulate(dk_ref, i == 0, dk)

    def piece(p):
        return pl.BlockSpec((1, tm, D), lambda i, p=p: (p, i, 0))

    def prev(p):
        return pl.BlockSpec((1, HALO, D), lambda i, p=p: (p, jnp.maximum(i * hb - 1, 0), 0))

    def nxt(p):
        return pl.BlockSpec((1, HALO, D), lambda i, p=p: (p, jnp.minimum((i + 1) * hb, nt * hb - 1), 0))

    return pl.pallas_call(
        body, name=name, grid=(nt,),
        in_specs=[piece(0), piece(0), piece(1), piece(2), prev(1), prev(2), nxt(0), nxt(0),
                  pl.BlockSpec((3, D), lambda i: (0, 0))],
        out_specs=[pl.BlockSpec((3, tm, D), lambda i: (0, i, 0)), pl.BlockSpec((3, D), lambda i: (0, 0))],
        out_shape=[jax.ShapeDtypeStruct((3, T, D), BF16), jax.ShapeDtypeStruct((3, D), F32)],
        compiler_params=_params(1),
    )(dz, bch, bch, bch, bch, bch, dz, bch, conv_k)


def _log_sigmoid(x):
    return jnp.minimum(x, 0.0) - jnp.log(1.0 + jnp.exp(-jnp.abs(x)))


def forget_fwd(fl, fb, seq_len, name):
    T = fl.shape[1]

    def body(fl_ref, fb_ref, c_ref):
        c = _log_sigmoid(fl_ref[0] + fb_ref[...])
        row = lax.broadcasted_iota(jnp.int32, c.shape, 0)
        k = 1
        while k < seq_len:
            c = c + jnp.where(row >= k, pltpu.roll(c, k, 0), 0.0)
            k *= 2
        c_ref[...] = jnp.concatenate([jnp.broadcast_to(c[:, h:h + 1], (seq_len, HEAD_DIM)) for h in range(N_HEADS)], axis=1)

    return pl.pallas_call(
        body, name=name, grid=(T // seq_len,),
        in_specs=[pl.BlockSpec((1, seq_len, LANES), lambda b: (0, b, 0)), pl.BlockSpec((1, LANES), lambda b: (0, 0))],
        out_specs=pl.BlockSpec((seq_len, D), lambda b: (b, 0)),
        out_shape=jax.ShapeDtypeStruct((T, D), F32), compiler_params=_params(1),
    )(fl, fb)


def forget_bwd(dcb, fl, fb, seq_len, name):
    T = dcb.shape[0]
    pick = jnp.zeros((D, LANES), F32).at[HEAD_DIM * jnp.arange(N_HEADS), jnp.arange(N_HEADS)].set(1.0)

    def body(dc_ref, pick_ref, fl_ref, fb_ref, dfl_ref, dfb_ref):
        b = pl.program_id(0)
        r = jnp.dot(dc_ref[...], pick_ref[...], precision=lax.Precision.HIGHEST, preferred_element_type=F32)
        row = lax.broadcasted_iota(jnp.int32, r.shape, 0)
        k = 1
        while k < seq_len:
            r = r + jnp.where(row < seq_len - k, pltpu.roll(r, seq_len - k, 0), 0.0)
            k *= 2
        dfl = r * jax.nn.sigmoid(-(fl_ref[0] + fb_ref[...]))
        dfl_ref[0] = dfl
        _accumulate(dfb_ref, b == 0, jnp.sum(dfl, axis=0, keepdims=True))

    return pl.pallas_call(
        body, name=name, grid=(T // seq_len,),
        in_specs=[pl.BlockSpec((seq_len, D), lambda b: (b, 0)), pl.BlockSpec((D, LANES), lambda b: (0, 0)),
                  pl.BlockSpec((1, seq_len, LANES), lambda b: (0, b, 0)), pl.BlockSpec((1, LANES), lambda b: (0, 0))],
        out_specs=[pl.BlockSpec((1, seq_len, LANES), lambda b: (0, b, 0)), pl.BlockSpec((1, LANES), lambda b: (0, 0))],
        out_shape=[jax.ShapeDtypeStruct((1, T, LANES), F32), jax.ShapeDtypeStruct((1, LANES), F32)],
        compiler_params=_params(1),
    )(dcb, pick, fl, fb)


HEADS = (slice(0, HEAD_DIM), slice(HEAD_DIM, 2 * HEAD_DIM))


def _key_blocks(i, step, carry):
    carry = lax.fori_loop(0, i // 2, lambda jj, c: step(2 * jj, c, False, 2), carry)
    return lax.cond(i % 2 == 1, lambda c: step(i - 1, c, True, 2), lambda c: step(i, c, True, 1), carry)


def _causal(width, tq):
    keys = lax.broadcasted_iota(jnp.int32, (width * tq, tq), 0)
    return keys <= lax.broadcasted_iota(jnp.int32, (width * tq, tq), 1) + (width - 1) * tq


def attn_fwd(qg, kv, cb, n_seq, seq_len, name):
    T = n_seq * seq_len
    tq = _tile(seq_len, TQ, LANES)
    nq = seq_len // tq

    def body(q_ref, k_ref, v_ref, cb_ref, o_ref, lse_ref):
        i = pl.program_id(2)
        q8 = [q_ref[0, :, sl] * ATT_SCALE for sl in HEADS]

        def block(j, carry, diagonal, width):
            rows = pl.ds(pl.multiple_of(j * tq, tq), width * tq)
            out = []
            for hh, sl in enumerate(HEADS):
                m, l, acc = carry[3 * hh:3 * hh + 3]
                s = lax.dot_general(k_ref[0, rows, sl], q8[hh], NT, preferred_element_type=F32)
                s = s - cb_ref[rows, sl.start:sl.start + 1]
                if diagonal:
                    s = jnp.where(_causal(width, tq), s, -1e30)
                m_new = jnp.maximum(m, jnp.max(s, axis=0, keepdims=True))
                a = jnp.exp(m - m_new)
                p = jnp.exp(s - m_new)
                l = a * l + jnp.sum(p, axis=0, keepdims=True)
                acc = a * acc + lax.dot_general(v_ref[0, rows, sl], p.astype(BF16), TN, preferred_element_type=F32)
                out += [m_new, l, acc]
            return tuple(out)

        init = (jnp.full((1, tq), -1e30, F32), jnp.zeros((1, tq), F32), jnp.zeros((HEAD_DIM, tq), F32)) * 2
        carry = _key_blocks(i, block, init)
        o_ref[...] = jnp.concatenate([carry[2] / carry[1], carry[5] / carry[4]], axis=0).T
        for hh in range(2):
            lse_ref[0, 0, 0, hh:hh + 1, :] = carry[3 * hh] + jnp.log(carry[3 * hh + 1])

    seq2 = pl.BlockSpec((seq_len, LANES), lambda b, hp, i: (b, hp))
    return pl.pallas_call(
        body, name=name, grid=(n_seq, N_HEADS // 2, nq),
        in_specs=[pl.BlockSpec((1, tq, LANES), lambda b, hp, i: (0, b * nq + i, hp)),
                  pl.BlockSpec((1, seq_len, LANES), lambda b, hp, i: (0, b, hp)),
                  pl.BlockSpec((1, seq_len, LANES), lambda b, hp, i: (1, b, hp)), seq2],
        out_specs=[pl.BlockSpec((tq, LANES), lambda b, hp, i: (b * nq + i, hp)),
                   pl.BlockSpec((1, 1, 1, 2, tq), lambda b, hp, i: (b, hp, i, 0, 0))],
        out_shape=[jax.ShapeDtypeStruct((T, D), F32), jax.ShapeDtypeStruct((n_seq, N_HEADS // 2, nq, 2, tq), F32)],
        compiler_params=_params(3),
    )(qg, kv, kv, cb)


def attn_bwd(dz, qg, kv, o, lse, cb, n_seq, seq_len, name):
    T = n_seq * seq_len
    tq = _tile(seq_len, TQ, LANES)
    nq = seq_len // tq

    def body(dz_ref, q_ref, gate_ref, o_ref, lse_ref, cb_ref, k_ref, v_ref,
             dq_ref, dgate_ref, dk_ref, dv_ref, dc_ref, p_s, dp_s, dk_s, dv_s, dc_s):
        i = pl.program_id(2)

        @pl.when(i == 0)
        def _():
            dk_s[...] = jnp.zeros_like(dk_s)
            dv_s[...] = jnp.zeros_like(dv_s)
            dc_s[...] = jnp.zeros_like(dc_s)

        dzf = dz_ref[...].astype(F32)
        sig = jax.nn.sigmoid(gate_ref[0].astype(F32))
        dob = (dzf * sig).astype(BF16)
        dgate_ref[0] = (dzf * o_ref[...] * sig * (1.0 - sig)).astype(BF16)
        q8 = [q_ref[0, :, sl] * ATT_SCALE for sl in HEADS]
        do = [dob[:, sl] for sl in HEADS]
        lse_i = [lse_ref[0, 0, 0, hh:hh + 1, :] for hh in range(2)]

        def probs(j, dsums, diagonal, width):
            rows = pl.ds(pl.multiple_of(j * tq, tq), width * tq)
            out = []
            for hh, sl in enumerate(HEADS):
                s = lax.dot_general(k_ref[0, rows, sl], q8[hh], NT, preferred_element_type=F32)
                p = jnp.exp(s - cb_ref[rows, sl.start:sl.start + 1] - lse_i[hh])
                if diagonal:
                    p = jnp.where(_causal(width, tq), p, 0.0)
                dp = lax.dot_general(v_ref[0, rows, sl], do[hh], NT, preferred_element_type=F32)
                p_s[hh, rows, :] = p
                dp_s[hh, rows, :] = dp
                out.append(dsums[hh] + jnp.sum(p * dp, axis=0, keepdims=True))
            return tuple(out)

        dsums = _key_blocks(i, probs, (jnp.zeros((1, tq), F32),) * 2)

        def grads(j, dqs, diagonal, width):
            rows = pl.ds(pl.multiple_of(j * tq, tq), width * tq)
            out = []
            for hh, sl in enumerate(HEADS):
                p = p_s[hh, rows, :]
                ds = p * (dp_s[hh, rows, :] - dsums[hh])
                dc_s[hh, rows, :] -= jnp.sum(ds, axis=1, keepdims=True)
                dsb = ds.astype(BF16)
                dk_s[rows, sl] += jnp.dot(dsb, q8[hh], preferred_element_type=F32)
                dv_s[rows, sl] += jnp.dot(p.astype(BF16), do[hh], preferred_element_type=F32)
                out.append(dqs[hh] + lax.dot_general(k_ref[0, rows, sl], dsb, TN, preferred_element_type=F32))
            return tuple(out)

        dqs = _key_blocks(i, grads, (jnp.zeros((HEAD_DIM, tq), F32),) * 2)
        dq_ref[0] = (jnp.concatenate(dqs, axis=0).T * ATT_SCALE).astype(BF16)

        @pl.when(i == nq - 1)
        def _():
            dk_ref[0] = dk_s[...].astype(BF16)
            dv_ref[0] = dv_s[...].astype(BF16)
            dc_ref[...] = jnp.zeros_like(dc_ref)
            for hh, sl in enumerate(HEADS):
                dc_ref[:, sl.start:sl.start + 1] = dc_s[hh]

    qry2 = pl.BlockSpec((tq, LANES), lambda b, hp, i: (b * nq + i, hp))
    seq2 = pl.BlockSpec((seq_len, LANES), lambda b, hp, i: (b, hp))

    def qry3(p):
        return pl.BlockSpec((1, tq, LANES), lambda b, hp, i, p=p: (p, b * nq + i, hp))

    def seq3(p):
        return pl.BlockSpec((1, seq_len, LANES), lambda b, hp, i, p=p: (p, b, hp))

    act = jax.ShapeDtypeStruct((1, T, D), BF16)
    return pl.pallas_call(
        body, name=name, grid=(n_seq, N_HEADS // 2, nq),
        in_specs=[qry2, qry3(0), qry3(1), qry2, pl.BlockSpec((1, 1, 1, 2, tq), lambda b, hp, i: (b, hp, i, 0, 0)), seq2,
                  seq3(0), seq3(1)],
        out_specs=[qry3(0), qry3(0), seq3(0), seq3(0), seq2],
        out_shape=[act, act, act, act, jax.ShapeDtypeStruct((T, D), F32)],
        scratch_shapes=[pltpu.VMEM((2, seq_len, tq), F32), pltpu.VMEM((2, seq_len, tq), F32),
                        pltpu.VMEM((seq_len, LANES), F32), pltpu.VMEM((seq_len, LANES), F32),
                        pltpu.VMEM((2, seq_len, 1), F32)],
        compiler_params=_params(3),
    )(dz, qg, qg, o, lse, cb, kv, kv)


def loss_head(y, target, name):
    T = y.shape[0]
    tm = _tile(T, TM, 8)

    def body(y_ref, t_ref, dy_ref, loss_ref):
        err = y_ref[...] - t_ref[...]
        dy_ref[...] = err * (1.0 / D)
        part = 0.5 * jnp.sum(jnp.mean(err * err, axis=-1, keepdims=True), axis=0, keepdims=True)
        _accumulate(loss_ref, pl.program_id(0) == 0, jnp.broadcast_to(part, (1, LANES)))

    row = pl.BlockSpec((tm, D), lambda i: (i, 0))
    return pl.pallas_call(
        body, name=name, grid=(T // tm,), in_specs=[row, row],
        out_specs=[row, pl.BlockSpec((1, LANES), lambda i: (0, 0))],
        out_shape=[jax.ShapeDtypeStruct((T, D), F32), jax.ShapeDtypeStruct((1, LANES), F32)],
        compiler_params=_params(1),
    )(y, target)


def _adamw(w, g, m, v):
    m = ADAM_B1 * m + (1.0 - ADAM_B1) * g
    v = ADAM_B2 * v + (1.0 - ADAM_B2) * (g * g)
    m_hat = m / (1.0 - ADAM_B1 ** ADAM_STEP)
    v_hat = v / (1.0 - ADAM_B2 ** ADAM_STEP)
    delta = -ADAM_LR * (m_hat / (jnp.sqrt(v_hat) + ADAM_EPS) + ADAM_WD * w)
    return delta, m, v


def adam_sharded(w, m, v, contribs, me, name):
    L, R, C = w.shape
    tr = _tile(R, 256, 16) if R % 16 == 0 else R

    def body(me_ref, w_ref, m_ref, v_ref, *refs):
        c_refs, (g_ref, d_ref, nm_ref, nv_ref) = refs[:2 * L], refs[2 * L:]
        l = pl.program_id(0)
        for j in range(L):
            @pl.when(l == j)
            def _(j=j):
                own_ref, recv_ref = c_refs[2 * j], c_refs[2 * j + 1]
                g = own_ref[0].astype(F32)
                for k in range(N_DEV - 1):
                    g = g + recv_ref[k].astype(F32)
                delta, nm, nv = _adamw(w_ref[0], g, m_ref[0], v_ref[0])
                g_ref[0] = g
                d_ref[0] = delta
                nm_ref[0] = nm
                nv_ref[0] = nv

    blk = pl.BlockSpec((1, tr, C), lambda l, i, s: (l, i, 0))
    in_specs = [blk, blk, blk]
    inputs = [w, m, v]
    for j, (mine, recv) in enumerate(contribs):
        in_specs.append(pl.BlockSpec((1, tr, C), lambda l, i, s, j=j: (s[0], jnp.where(l == j, i, 0), 0)))
        in_specs.append(pl.BlockSpec((N_DEV - 1, tr, C), lambda l, i, s, j=j: (0, jnp.where(l == j, i, 0), 0)))
        inputs += [mine, recv]
    shp = jax.ShapeDtypeStruct((L, R, C), F32)
    grid_spec = pltpu.PrefetchScalarGridSpec(num_scalar_prefetch=1, grid=(L, R // tr), in_specs=in_specs, out_specs=[blk] * 4)
    return pl.pallas_call(body, name=name, grid_spec=grid_spec, out_shape=[shp] * 4, compiler_params=_params(2))(me, *inputs)


def cast_place(w, l, me, dtype, after, name):
    _, R, C = w.shape
    tr = _tile(R, 512, 16) if R % 16 == 0 else R

    def body(me_ref, w_ref, after_ref, o_ref):
        o_ref[0] = w_ref[0].astype(dtype)

    grid_spec = pltpu.PrefetchScalarGridSpec(
        num_scalar_prefetch=1, grid=(R // tr,), in_specs=[pl.BlockSpec((1, tr, C), lambda i, s: (l, i, 0)), ANY],
        out_specs=pl.BlockSpec((1, tr, C), lambda i, s: (s[0], i, 0)))
    return pl.pallas_call(body, name=name, grid_spec=grid_spec, out_shape=jax.ShapeDtypeStruct((N_DEV, R, C), dtype),
                          compiler_params=_params(1))(me, w, after)


def adam_small(params, total, extra_grads, name):
    n, ne = len(params), len(extra_grads)

    def body(*refs):
        total_ref, extra_refs = refs[0], refs[1:1 + ne]
        ins, outs = refs[1 + ne:1 + ne + 3 * n], refs[1 + ne + 3 * n:]
        for k, (_, _, _, where) in enumerate(params):
            if isinstance(where, int):
                g = extra_refs[where][...]
            else:
                row, rows, width = where
                g = total_ref[row:row + rows, 0:width]
            delta, nm, nv = _adamw(ins[3 * k][...], g, ins[3 * k + 1][...], ins[3 * k + 2][...])
            outs[4 * k][...] = g
            outs[4 * k + 1][...] = delta
            outs[4 * k + 2][...] = nm
            outs[4 * k + 3][...] = nv

    flat = [a for w, m, v, _ in params for a in (w, m, v)]
    out_shape = [jax.ShapeDtypeStruct(w.shape, F32) for w, _, _, _ in params for _ in range(4)]
    res = pl.pallas_call(body, name=name, out_shape=out_shape)(total, *extra_grads, *flat)
    return [res[4 * k:4 * k + 4] for k in range(n)]


def _place():
    return lax.axis_index("x"), lax.axis_index("y"), lax.axis_index("c")


def _peer(place, k):
    x, y, c = place
    return x ^ (k >> 2), y ^ ((k >> 1) & 1), c ^ (k & 1)


ANY = pl.BlockSpec(memory_space=pl.ANY)
HBM = pl.BlockSpec(memory_space=pltpu.HBM)
SEM = pl.BlockSpec(memory_space=pltpu.SEMAPHORE)
EFFECT = pltpu.SideEffectType.DATAFLOW_SIDE_EFFECTING


def _in_hbm(a):
    return pltpu.with_memory_space_constraint(a, pltpu.HBM)


def _number(place):
    return 4 * place[0] + 2 * place[1] + place[2]


SLOTS = {"gather_like": 4, "gather_pass": 3, "scatter": 7}


def _plan(mode, src_ref, land_ref, place):
    if mode == "gather_like":
        mine = land_ref.at[_number(place)]
        return [(mine, mine, _peer(place, k)) for k in (1, 2, 4, 6)]
    if mode == "gather_pass":
        slots = [land_ref.at[_number(_peer(place, k))] for k in (2, 4, 6)]
        return [(slot, slot, _peer(place, 1)) for slot in slots]
    return [(src_ref.at[_number(_peer(place, k))], land_ref.at[k - 1], _peer(place, k)) for k in range(1, N_DEV)]


def _exchange(name, groups, start, afters):
    sizes = [len(g[3]) for g in groups]
    na, ng = sum(sizes), len(groups)
    ns = sum(len(g[2]) for g in groups)
    waits = groups[0][0] is not None
    n_in_sems = 2 * ng if waits else 0
    n_out_sems = 2 * ng if start else 0

    def body(*refs):
        src_refs, land_refs = (refs[:ns] if ns else [None] * na), refs[ns:ns + na]
        in_sems = refs[ns + na:ns + na + n_in_sems]
        outs = refs[ns + na + n_in_sems + len(afters):]
        place = _place()
        a = 0
        for gi, n in enumerate(sizes):
            for idx in range(n):
                if waits:
                    zone = land_refs[a].at[pl.ds(0, groups[gi][4])]
                    copy = pltpu.make_async_remote_copy(
                        src_ref=zone, dst_ref=zone, send_sem=in_sems[2 * gi].at[idx], recv_sem=in_sems[2 * gi + 1].at[idx],
                        device_id=_peer(place, 1), device_id_type=MESH)
                    copy.wait_send()
                    copy.wait_recv()
                if start:
                    for src, dst, peer in _plan(start, src_refs[a], land_refs[a], place):
                        pltpu.make_async_remote_copy(
                            src_ref=src, dst_ref=dst, send_sem=outs[2 * gi].at[idx], recv_sem=outs[2 * gi + 1].at[idx],
                            device_id=peer, device_id_type=MESH).start()
                a += 1
        if start:
            outs[-1][...] = jnp.zeros_like(outs[-1])

    srcs = [_in_hbm(s) for g in groups for s in g[2]]
    lands = [_in_hbm(l) for g in groups for l in g[3]]
    sems = [s for g in groups for s in g[:2]] if waits else []
    out_shape = [pltpu.SemaphoreType.DMA((n,)) for n in sizes for _ in range(2)] if start else []
    out_shape += [pltpu.HBM(a.shape, a.dtype) for a in srcs + lands]
    out_specs = [SEM] * n_out_sems + [HBM] * (ns + na)
    if start:
        out_shape.append(jax.ShapeDtypeStruct((8, LANES), F32))
        out_specs.append(pl.BlockSpec(memory_space=pltpu.VMEM))
    res = pl.pallas_call(
        body, name=name, in_specs=[HBM] * (ns + na) + [SEM] * n_in_sems + [ANY] * len(afters),
        out_shape=out_shape, out_specs=out_specs,
        input_output_aliases={i: n_out_sems + i for i in range(ns + na)},
        compiler_params=pltpu.CompilerParams(has_side_effects=EFFECT),
    )(*srcs, *lands, *sems, *afters)
    new_sems, thru = res[:n_out_sems], res[n_out_sems:n_out_sems + ns + na]
    out, a = [], 0
    for gi, n in enumerate(sizes):
        pair = (new_sems[2 * gi], new_sems[2 * gi + 1]) if start else (None, None)
        out.append(pair + (thru[a:a + n] if ns else [], thru[ns + a:ns + a + n], SLOTS.get(start, 0)))
        a += n
    return out, (res[-1] if start else None)


def exchange_start(pair_groups, mode, name):
    groups = [(None, None, [s for s, _ in g if s is not None], [l for _, l in g], 0) for g in pair_groups]
    return _exchange(name, groups, mode, ())


def exchange_relay(groups, mode, afters, name):
    return _exchange(name, groups, mode, afters)


def exchange_wait(groups, afters, name):
    done, _ = _exchange(name, groups, None, afters)
    return [(g[2], g[3]) for g in done]


def all_reduce_small(parts, n_rows, after, name):
    R = n_rows
    n_parts = len(parts)

    def body(*refs):
        part_refs = refs[:n_parts]
        out_ref, buf, send_sems, recv_sems = refs[n_parts + 1:]
        x, y, c = _place()
        me = 4 * x + 2 * y + c
        own = buf.at[me]
        own[...] = jnp.zeros((R, D), F32)
        for ref, (arr, row) in zip(part_refs, parts):
            own[row:row + arr.shape[0], 0:arr.shape[1]] = ref[...]
        copies = []
        for k in range(1, N_DEV):
            peer = (x ^ (k >> 2), y ^ ((k >> 1) & 1), c ^ (k & 1))
            copies.append(pltpu.make_async_remote_copy(
                src_ref=own, dst_ref=own, send_sem=send_sems.at[k - 1], recv_sem=recv_sems.at[k - 1],
                device_id=peer, device_id_type=MESH))
        for cp in copies:
            cp.start()
        for cp in copies:
            cp.wait()
        total = buf[0]
        for d in range(1, N_DEV):
            total = total + buf[d]
        out_ref[...] = total

    vm = pl.BlockSpec(memory_space=pltpu.VMEM)
    return pl.pallas_call(
        body, name=name, in_specs=[vm] * n_parts + [ANY], out_specs=vm, out_shape=jax.ShapeDtypeStruct((R, D), F32),
        scratch_shapes=[pltpu.VMEM((N_DEV, R, D), F32), pltpu.SemaphoreType.DMA((N_DEV - 1,)),
                        pltpu.SemaphoreType.DMA((N_DEV - 1,))],
    )(*[arr for arr, _ in parts], after)


def _col_blocks(gathered, n_blocks):
    n, d, w = gathered.shape
    whole = gathered.transpose(1, 0, 2).reshape(d, n * w)
    return whole.reshape(d, n_blocks, n * w // n_blocks).transpose(1, 0, 2)[:, None]


def _col_shards(blocks):
    n, d, w = blocks.shape
    whole = blocks.transpose(1, 0, 2).reshape(d, n * w)
    return whole.reshape(d, N_DEV, n * w // N_DEV).transpose(1, 0, 2)


def kernel(x, ffn1_pre_g, ffn1_post_g, ffn1_w_in, ffn1_w_out, mix_pre_g, mix_post_g, ffn2_pre_g, ffn2_post_g, ffn2_w_in, ffn2_w_out, conv_w_in, conv_k, conv_w_out, kv_g, kv_w, forget_b, attn_w_qg, attn_w_o, loss_target, m_ffn1_pre_g, m_ffn1_post_g, m_ffn1_w_in, m_ffn1_w_out, m_mix_pre_g, m_mix_post_g, m_ffn2_pre_g, m_ffn2_post_g, m_ffn2_w_in, m_ffn2_w_out, m_conv_w_in, m_conv_k, m_conv_w_out, m_kv_g, m_kv_w, m_forget_b, m_attn_w_qg, m_attn_w_o, v_ffn1_pre_g, v_ffn1_post_g, v_ffn1_w_in, v_ffn1_w_out, v_mix_pre_g, v_mix_post_g, v_ffn2_pre_g, v_ffn2_post_g, v_ffn2_w_in, v_ffn2_w_out, v_conv_w_in, v_conv_k, v_conv_w_out, v_kv_g, v_kv_w, v_forget_b, v_attn_w_qg, v_attn_w_o):
    n_seq, seq_len, _ = x.shape
    T = n_seq * seq_len
    xi, yi, ci = _place()
    dev = 4 * xi + 2 * yi + ci
    x0 = x.reshape(T, D)
    target = loss_target.reshape(T, D)

    me = dev.reshape(1).astype(jnp.int32)
    w1_t, w2_t, kv_t = ffn1_w_in.transpose(0, 2, 1), ffn2_w_in.transpose(0, 2, 1), kv_w.T[None]

    def zones(specs, after):
        return [(None, cast_place(w, l, me, dt, after, f"place_{nm}")) for nm, w, l, dt in specs]

    gathers, token = exchange_start([zones([("w1_in0", w1_t, 0, BF16)], me)], "gather_like", "gather_start0")
    shard_groups = [
        [("w1_out0", ffn1_w_out, 0, BF16), ("cw_in", conv_w_in, 0, BF16), ("cw_out", conv_w_out, 0, BF16),
         ("ck", conv_k, 0, F32)],
        [("w2_in0", w2_t, 0, BF16), ("w2_out0", ffn2_w_out, 0, BF16)],
        [("kv", kv_t, 0, BF16), ("w1_in1", w1_t, 1, BF16), ("w1_out1", ffn1_w_out, 1, BF16),
         ("qg", attn_w_qg, 0, BF16), ("ow", attn_w_o, 0, BF16), ("w2_in1", w2_t, 1, BF16), ("w2_out1", ffn2_w_out, 1, BF16)]]
    later, token = exchange_start([zones(g, token) for g in shard_groups], "gather_like", "gather_start1")
    gathers = gathers + later

    def gathered(k, after):
        passed, _ = exchange_relay([gathers[k]], "gather_pass", [after], f"gather_pass{k}")
        return [z[:, None] for z in exchange_wait(passed, [after], f"gather_wait{k}")[0][1]]

    fb = jnp.pad(forget_b, (0, LANES - N_HEADS))[None]

    def vec(g, l):
        return g[l:l + 1]

    def behind(g, tok):
        return g + tok[:1, :1]

    grads_small = {}
    tm_ffn = _tile(T, TM, 16)

    def ffn_fwd(xin, g_pre, g_post, w_in, w_out, tag):
        w_gu = w_in.reshape(2, 1, -1, D)
        xn, (gu,) = rms_proj(xin, g_pre, [(w_gu, 0)], [BF16], f"{tag}_in", transposed=True)
        specs = [pl.BlockSpec((2, tm_ffn, gu.shape[2]), lambda i: (0, i, 0))]
        if callable(w_out):
            w_out = w_out(xn)
        a, h, y = mix_out(_swiglu_pro, [gu], specs, (w_out, 0), xin, g_post, 0.5, 1, f"{tag}_out", tm=tm_ffn)
        return y, (xin, xn, gu, a, h), w_out

    def ffn_bwd(dy, saved, g_pre, g_post, w_in, w_out, tag):
        xin, xn, gu, a, h = saved
        w_gu = w_in.reshape(2, 1, -1, D)
        half = gu.shape[2] // 2
        dh, dg_post, dgu = post_bwd(dy, h, g_post, 0.5, (w_out, 0), 1, f"{tag}_bwd_out", gu=gu, tm=tm_ffn)
        dw_out = wgrad([a], [dh[None]], 2, (2, 1, half, D), f"{tag}_dw_out", a_cols=half)
        started_out, tok = scatter_start([dw_out.reshape(8, -1, D)], f"{tag}_out_scatter_start")
        dw_in = wgrad([dgu], [xn[None]], 4, (4, 1, half, D), f"{tag}_dw_in", a_cols=half, after=tok)
        started_in, tok = scatter_start([dw_in.reshape(8, -1, D)], f"{tag}_in_scatter_start")
        dx, dg_pre = pre_bwd([(dgu, (w_gu, 0), 0)], xin, behind(g_pre, tok), dy, f"{tag}_bwd_in", transposed=True)
        return dx, dg_pre, dg_post, [started_in, started_out]

    def scatter_start(blocked, name):
        pairs = [(g, lax.empty((N_DEV - 1,) + g.shape[1:], g.dtype)) for g in blocked]
        started, tok = exchange_start([pairs], "scatter", name)
        return started[0], tok

    (w1_in,) = gathered(0, token)
    second = []

    def first_w_out(after):
        second.extend(gathered(1, after))
        return second[0]

    x1, s_f1a, w1_out = ffn_fwd(x0, vec(ffn1_pre_g, 0), vec(ffn1_post_g, 0), w1_in, first_w_out, "l0_ffn1")
    cw_in_g, cw_out, ck_g = second[1:]
    cw_in = _col_blocks(cw_in_g[:, 0], 3)
    ck = ck_g[:, 0].transpose(1, 0, 2).reshape(3, D)
    xn_c, (bch,) = rms_proj(x1, vec(mix_pre_g, 0), [(cw_in, 0)], [BF16], "conv_in")
    tmc = _tile(seq_len, TM, 16)
    hb = tmc // HALO

    def cpiece(p):
        return pl.BlockSpec((1, tmc, D), lambda i, p=p: (p, i, 0))

    def chalo(p):
        return pl.BlockSpec((1, HALO, D), lambda i, p=p: (p, jnp.maximum(i * hb - 1, 0), 0))

    conv_specs = [cpiece(0), cpiece(1), cpiece(2), chalo(1), chalo(2), pl.BlockSpec((3, D), lambda i: (0, 0))]
    z_c, m_c, x2 = mix_out(_make_conv_pro(seq_len // tmc), [bch, bch, bch, bch, bch, ck], conv_specs, (cw_out, 0),
                           x1, vec(mix_post_g, 0), 1.0, 1, "conv_out", tm=tmc)
    w2_in, w2_out = gathered(2, x2)
    x3, s_f2a, _ = ffn_fwd(x2, vec(ffn2_pre_g, 0), vec(ffn2_post_g, 0), w2_in, w2_out, "l0_ffn2")

    kvw_g, w1_in_b, w1_out_b, qgw_g, ow, w2_in_b, w2_out_b = gathered(3, x3)
    qg_w = _col_blocks(qgw_g[:, 0], 2)
    kv_whole = kvw_g.reshape(2 * D + N_HEADS, D).T
    kv_wb = kv_whole[:, :2 * D].reshape(D, 2, D).transpose(1, 0, 2)[:, None]
    f_w = jnp.pad(kv_whole[:, 2 * D:], ((0, 0), (0, LANES - N_HEADS)))[None, None]
    xn_kv, (kv, fl) = rms_proj(x3, kv_g[None], [(kv_wb, 0), (f_w, 0)], [BF16, F32], "kv_in")
    cb = forget_fwd(fl, fb, seq_len, "forget_fwd")

    x4, s_f1b, _ = ffn_fwd(x3, vec(ffn1_pre_g, 1), vec(ffn1_post_g, 1), w1_in_b, w1_out_b, "l1_ffn1")
    xn_a, (qg,) = rms_proj(x4, vec(mix_pre_g, 1), [(qg_w, 0)], [BF16], "attn_in")
    o, lse = attn_fwd(qg, kv, cb, n_seq, seq_len, "attn_fwd")
    tm = _tile(T, TM, 16)
    gate_specs = [pl.BlockSpec((1, tm, D), lambda i: (1, i, 0)), pl.BlockSpec((tm, D), lambda i: (i, 0))]
    z_a, m_a, x5 = mix_out(_attn_gate_pro, [qg, o], gate_specs, (ow, 0), x4, vec(mix_post_g, 1), 1.0, 1, "attn_out")
    x6, s_f2b, _ = ffn_fwd(x5, vec(ffn2_pre_g, 1), vec(ffn2_post_g, 1), w2_in_b, w2_out_b, "l1_ffn2")

    dy, loss_part = loss_head(x6, target, "loss_head")

    scatters = {}
    dx5, dg, dgp, scatters["ffn2", 1] = ffn_bwd(dy, s_f2b, vec(ffn2_pre_g, 1), vec(ffn2_post_g, 1), w2_in_b, w2_out_b, "l1_ffn2")
    grads_small["ffn2_pre", 1], grads_small["ffn2_post", 1] = dg, dgp
    dm_a, dgp, dz_a = post_bwd(dx5, m_a, vec(mix_post_g, 1), 1.0, (ow, 0), 1, "attn_bwd_out")
    grads_small["mix_post", 1] = dgp
    dq, dgate, dk, dv, dcb = attn_bwd(dz_a[0], qg, kv, o, lse, cb, n_seq, seq_len, "attn_bwd")
    dx4, dg = pre_bwd([(dq, (qg_w, 0), 0), (dgate, (qg_w, 0), 1)], x4, vec(mix_pre_g, 1), dx5, "attn_bwd_in")
    grads_small["mix_pre", 1] = dg
    d_ow = wgrad([z_a], [dm_a[None]], 1, ow.shape, "attn_dw_o")
    d_qgw = wgrad([xn_a[None]], [dq, dgate], 2, (2, 1, D, D), "attn_dw_qg")
    scatters["attn"], tok = scatter_start([_col_shards(d_qgw[:, 0]), d_ow.reshape(8, -1, D)], "attn_scatter_start")
    dx3, dg, dgp, scatters["ffn1", 1] = ffn_bwd(dx4, s_f1b, vec(ffn1_pre_g, 1), behind(vec(ffn1_post_g, 1), tok),
                                                w1_in_b, w1_out_b, "l1_ffn1")
    grads_small["ffn1_pre", 1], grads_small["ffn1_post", 1] = dg, dgp

    dfl, dfb = forget_bwd(dcb, fl, fb, seq_len, "forget_bwd")
    dx3, dg_kv = pre_bwd([(dk, (kv_wb, 0), 0), (dv, (kv_wb, 0), 1), (dfl, (f_w, 0), 0)], x3, kv_g[None], dx3, "kv_bwd_in")
    d_kvw = wgrad([xn_kv[None]], [dk, dv], 2, (2, 1, D, D), "kv_dw")
    d_fw = wgrad([xn_kv[None]], [dfl.astype(BF16)], 1, (1, 1, D, LANES), "forget_dw")
    d_kv_whole = jnp.concatenate([d_kvw[0, 0], d_kvw[1, 0], d_fw[0, 0, :, :N_HEADS]], axis=1)
    wshard = D * 2 + N_HEADS
    scatters["kv"], tok = scatter_start([d_kv_whole.T.reshape(N_DEV, wshard // N_DEV, D)], "kv_scatter_start")

    dx2, dg, dgp, scatters["ffn2", 0] = ffn_bwd(dx3, s_f2a, vec(ffn2_pre_g, 0), behind(vec(ffn2_post_g, 0), tok),
                                                w2_in, w2_out, "l0_ffn2")
    grads_small["ffn2_pre", 0], grads_small["ffn2_post", 0] = dg, dgp
    dm_c, dgp, dz_c = post_bwd(dx2, m_c, vec(mix_post_g, 0), 1.0, (cw_out, 0), 1, "conv_bwd_out")
    grads_small["mix_post", 0] = dgp
    dbch, d_ck = conv_bwd_mix(dz_c, bch, ck, seq_len, "conv_bwd_mix")
    dx1, dg = pre_bwd([(dbch, (cw_in, 0), 0)], x1, vec(mix_pre_g, 0), dx2, "conv_bwd_in")
    grads_small["mix_pre", 0] = dg
    d_cw_out = wgrad([z_c], [dm_c[None]], 1, cw_out.shape, "conv_dw_out")
    d_cw_in = wgrad([xn_c[None]], [dbch], 3, (3, 1, D, D), "conv_dw_in")
    scatters["conv"], tok = scatter_start([_col_shards(d_cw_in[:, 0]), d_cw_out.reshape(8, -1, D)], "conv_scatter_start")
    dx0, dg, dgp, scatters["ffn1", 0] = ffn_bwd(dx1, s_f1a, vec(ffn1_pre_g, 0), behind(vec(ffn1_post_g, 0), tok),
                                                w1_in, w1_out, "l0_ffn1")
    grads_small["ffn1_pre", 0], grads_small["ffn1_post", 0] = dg, dgp

    parts_of = {}

    def scatter_end(keys, afters, name):
        flat = [(k, g) for k in keys for g in (scatters[k] if isinstance(scatters[k], list) else [scatters[k]])]
        for (k, _), (sent, recv) in zip(flat, exchange_wait([g for _, g in flat], afters, name)):
            parts_of.setdefault(k, []).extend(zip(sent, recv))

    scatter_end([("ffn2", 1), "attn", ("ffn1", 1), "kv", ("ffn2", 0), "conv"], [dx0], "scatter_wait")
    sharded = {"ffn2_w_in": (ffn2_w_in, m_ffn2_w_in, v_ffn2_w_in), "ffn2_w_out": (ffn2_w_out, m_ffn2_w_out, v_ffn2_w_out),
               "conv_w_in": (conv_w_in, m_conv_w_in, v_conv_w_in), "conv_w_out": (conv_w_out, m_conv_w_out, v_conv_w_out),
               "kv_w": (kv_w, m_kv_w, v_kv_w), "attn_w_qg": (attn_w_qg, m_attn_w_qg, v_attn_w_qg),
               "attn_w_o": (attn_w_o, m_attn_w_o, v_attn_w_o),
               "ffn1_w_in": (ffn1_w_in, m_ffn1_w_in, v_ffn1_w_in), "ffn1_w_out": (ffn1_w_out, m_ffn1_w_out, v_ffn1_w_out)}
    out = {}
    for nm, (w, mm, vv) in sharded.items():
        if nm == "ffn1_w_in":
            scatter_end([("ffn1", 0)], [res[0] for res in out.values()], "scatter_wait_last")
        contribs = {
            "ffn1_w_in": lambda: [parts_of["ffn1", 0][0], parts_of["ffn1", 1][0]],
            "ffn1_w_out": lambda: [parts_of["ffn1", 0][1], parts_of["ffn1", 1][1]],
            "ffn2_w_in": lambda: [parts_of["ffn2", 0][0], parts_of["ffn2", 1][0]],
            "ffn2_w_out": lambda: [parts_of["ffn2", 0][1], parts_of["ffn2", 1][1]],
            "conv_w_in": lambda: [parts_of["conv"][0]], "conv_w_out": lambda: [parts_of["conv"][1]],
            "kv_w": lambda: [parts_of["kv"][0]], "attn_w_qg": lambda: [parts_of["attn"][0]],
            "attn_w_o": lambda: [parts_of["attn"][1]]}[nm]()
        if nm in ("ffn1_w_in", "ffn2_w_in", "kv_w"):
            rows, cols = w.shape[-2:]

            def view(a):
                return a.reshape(-1, rows, cols).transpose(0, 2, 1)

            res = adam_sharded(view(w), view(mm), view(vv), contribs, me, f"adam_{nm}")
            out[nm] = [r.transpose(0, 2, 1).reshape(w.shape) for r in res]
        else:
            shape3 = (len(contribs),) + contribs[0][0].shape[1:]
            res = adam_sharded(w.reshape(shape3), mm.reshape(shape3), vv.reshape(shape3), contribs, me, f"adam_{nm}")
            out[nm] = [r.reshape(w.shape) for r in res]

    small_names = ["ffn1_pre", "ffn1_post", "mix_pre", "mix_post", "ffn2_pre", "ffn2_post"]
    parts = [(grads_small[n, l], 2 * k + l) for k, n in enumerate(small_names) for l in range(2)]
    parts += [(dg_kv, 12), (dfb, 13), (d_ck, 14), (loss_part, 17)]
    total = all_reduce_small(parts, 24, out["ffn1_w_in"][0], "all_reduce_small")
    loss = total[17, 0]
    d_ck_mine = lax.dynamic_slice(total, (14, dev * LANES), (3, LANES))
    gains = [(ffn1_pre_g, m_ffn1_pre_g, v_ffn1_pre_g), (ffn1_post_g, m_ffn1_post_g, v_ffn1_post_g),
             (mix_pre_g, m_mix_pre_g, v_mix_pre_g), (mix_post_g, m_mix_post_g, v_mix_post_g),
             (ffn2_pre_g, m_ffn2_pre_g, v_ffn2_pre_g), (ffn2_post_g, m_ffn2_post_g, v_ffn2_post_g)]
    small_params = [(w, m, v, (2 * k, 2, D)) for k, (w, m, v) in enumerate(gains)]
    small_params += [(kv_g[None], m_kv_g[None], v_kv_g[None], (12, 1, D)),
                     (forget_b[None], m_forget_b[None], v_forget_b[None], (13, 1, N_HEADS)),
                     (conv_k[0], m_conv_k[0], v_conv_k[0], 0)]
    small_res = adam_small(small_params, total, [d_ck_mine], "adam_small")
    small_keys = [n + "_g" for n in small_names] + ["kv_g", "forget_b", "conv_k"]
    shapes = {"kv_g": kv_g.shape, "forget_b": forget_b.shape, "conv_k": conv_k.shape}
    small = [{key: res[kind].reshape(shapes.get(key, res[kind].shape)) for key, res in zip(small_keys, small_res)}
             for kind in range(4)]
    order = ["ffn1_pre_g", "ffn1_post_g", "ffn1_w_in", "ffn1_w_out", "mix_pre_g", "mix_post_g", "ffn2_pre_g", "ffn2_post_g",
             "ffn2_w_in", "ffn2_w_out", "conv_w_in", "conv_k", "conv_w_out", "kv_g", "kv_w", "forget_b", "attn_w_qg",
             "attn_w_o"]
    results = [loss, dx0.reshape(x.shape)]
    for kind in range(4):
        for nm in order:
            results.append(out[nm][kind] if nm in out else small[kind][nm])
    return tuple(results)
```

```python
import functools
import math

import jax
import jax.numpy as jnp
from jax import lax
from jax.experimental import pallas as pl
from jax.experimental.pallas import tpu as pltpu

F32, BF16 = jnp.float32, jnp.bfloat16
D = 1024
N_HEADS = 16
HEAD_DIM = 64
N_DEV = 8
RMS_EPS = 1e-6
ATT_SCALE = 1.0 / math.sqrt(HEAD_DIM)
LANES = 128
HALO = 8
TM = 512
TQ = 512
VMEM_LIMIT = 48 * 1024 * 1024
MESH = pl.DeviceIdType.MESH

ADAM_LR, ADAM_B1, ADAM_B2, ADAM_EPS, ADAM_WD, ADAM_STEP = 0.001, 0.9, 0.999, 1e-08, 0.01, 10

NT = (((1,), (1,)), ((), ()))
TN = (((0,), (0,)), ((), ()))


def _params(n_axes, vmem_limit=VMEM_LIMIT):
    return pltpu.CompilerParams(dimension_semantics=("arbitrary",) * n_axes, vmem_limit_bytes=vmem_limit)


def _tile(n, cap, mult):
    best = None
    for t in range(mult, min(n, cap) + 1, mult):
        if n % t == 0:
            best = t
    assert best is not None, (n, cap, mult)
    return best


def _rms_rstd(x):
    return lax.rsqrt(jnp.mean(x * x, axis=-1, keepdims=True) + RMS_EPS)


def _rms_fwd(x, g):
    return x * _rms_rstd(x) * g


def _rms_bwd(x, g, dy):
    xh = x * _rms_rstd(x)
    dyg = dy * g
    dx = _rms_rstd(x) * (dyg - xh * jnp.mean(dyg * xh, axis=-1, keepdims=True))
    return dx, jnp.sum(dy * xh, axis=0, keepdims=True)


def _accumulate(ref, first, value):
    @pl.when(first)
    def _():
        ref[...] = value

    @pl.when(jnp.logical_not(first))
    def _():
        ref[...] += value


def rms_proj(x, g, ws, out_dtypes, name, transposed=False):
    T = x.shape[0]
    tm = _tile(T, TM, 16)
    na = len(ws)

    def body(x_ref, g_ref, *refs):
        w_refs, xn_ref, o_refs = refs[:na], refs[na], refs[na + 1:]
        xn = _rms_fwd(x_ref[...], g_ref[...]).astype(BF16)
        xn_ref[...] = xn
        for a, (w, l) in enumerate(ws):
            for p in range(w.shape[0]):
                if transposed:
                    y = lax.dot_general(xn, w_refs[a][p, l], NT, preferred_element_type=F32)
                else:
                    y = jnp.dot(xn, w_refs[a][p, l], preferred_element_type=F32)
                o_refs[a][p] = y.astype(o_refs[a].dtype)

    in_specs = [pl.BlockSpec((tm, D), lambda i: (i, 0)), pl.BlockSpec((1, D), lambda i: (0, 0))]
    in_specs += [pl.BlockSpec(w.shape, lambda i: (0, 0, 0, 0), pipeline_mode=pl.Buffered(1)) for w, _ in ws]
    out_specs = [pl.BlockSpec((tm, D), lambda i: (i, 0))]
    out_shape = [jax.ShapeDtypeStruct((T, D), BF16)]
    for (w, _), dt in zip(ws, out_dtypes):
        nb, wb = w.shape[0], w.shape[2 if transposed else 3]
        out_specs.append(pl.BlockSpec((nb, tm, wb), lambda i: (0, i, 0)))
        out_shape.append(jax.ShapeDtypeStruct((nb, T, wb), dt))
    res = pl.pallas_call(
        body, name=name, grid=(T // tm,), in_specs=in_specs, out_specs=out_specs, out_shape=out_shape,
        compiler_params=_params(1),
    )(x, g, *[w for w, _ in ws])
    return res[0], res[1:]


def mix_out(pro, pro_inputs, pro_specs, w, res, g_post, alpha, nk, name, tm=None, loss_target=None):
    w4, l = w
    T = res.shape[0]
    tm = _tile(T, TM, 16) if tm is None else tm
    dpb = N_DEV // nk
    rows = w4.shape[2]
    kb = dpb * rows
    npi = len(pro_inputs)
    head = loss_target is not None

    def body(*refs):
        pro_refs = refs[:npi]
        w_ref, res_ref, g_ref = refs[npi:npi + 3]
        z_ref, m_ref, y_ref = refs[npi + 3 + head:npi + 6 + head]
        i = pl.program_id(0)
        m = None
        for k in range(nk):
            z = pro(i, k, *pro_refs).astype(BF16)
            z_ref[k] = z
            part = jnp.dot(z, w_ref[k * dpb:(k + 1) * dpb, l].reshape(kb, D), preferred_element_type=F32)
            m = part if m is None else m + part
        m_ref[...] = m
        y = res_ref[...] + alpha * _rms_fwd(m, g_ref[...])
        if head:
            err = y - refs[npi + 3][...]
            y_ref[...] = err * (1.0 / D)
            part = 0.5 * jnp.sum(jnp.mean(err * err, axis=-1, keepdims=True), axis=0, keepdims=True)
            _accumulate(refs[-1], i == 0, jnp.broadcast_to(part, (1, LANES)))
        else:
            y_ref[...] = y

    row = pl.BlockSpec((tm, D), lambda i: (i, 0))
    in_specs = list(pro_specs) + [
        pl.BlockSpec(w4.shape, lambda i: (0, 0, 0, 0), pipeline_mode=pl.Buffered(1)), row, pl.BlockSpec((1, D), lambda i: (0, 0))]
    out_specs = [pl.BlockSpec((nk, tm, kb), lambda i: (0, i, 0)), row, row]
    out_shape = [jax.ShapeDtypeStruct((nk, T, kb), BF16), jax.ShapeDtypeStruct((T, D), F32), jax.ShapeDtypeStruct((T, D), F32)]
    inputs = [*pro_inputs, w4, res, g_post]
    if head:
        in_specs.append(row)
        inputs.append(loss_target)
        out_specs.append(pl.BlockSpec((1, LANES), lambda i: (0, 0)))
        out_shape.append(jax.ShapeDtypeStruct((1, LANES), F32))
    return pl.pallas_call(
        body, name=name, grid=(T // tm,), in_specs=in_specs, out_specs=out_specs, out_shape=out_shape,
        compiler_params=_params(1),
    )(*inputs)


def post_bwd(dy, m, g_post, alpha, w, nk, name, gu=None, tm=None):
    w4, l = w
    T = dy.shape[0]
    tm = _tile(T, TM, 16) if tm is None else tm
    dpb = N_DEV // nk
    rows = w4.shape[2]
    kb = dpb * rows
    swiglu = gu is not None

    def body(dy_ref, m_ref, g_ref, w_ref, *refs):
        if swiglu:
            gu_ref, dm_ref, dg_ref, dgu_ref = refs
        else:
            dm_ref, dg_ref, dz_ref = refs
        dm, dg = _rms_bwd(m_ref[...], g_ref[...], alpha * dy_ref[...])
        dm = dm.astype(BF16)
        dm_ref[...] = dm
        _accumulate(dg_ref, pl.program_id(0) == 0, dg)
        for k in range(nk):
            dz = lax.dot_general(dm, w_ref[k * dpb:(k + 1) * dpb, l].reshape(kb, D), NT, preferred_element_type=F32)
            if swiglu:
                gate, up = gu_ref[k].astype(F32), gu_ref[k + nk].astype(F32)
                sig = jax.nn.sigmoid(gate)
                dgu_ref[k] = (dz * up * sig * (1.0 + gate * (1.0 - sig))).astype(BF16)
                dgu_ref[k + nk] = (dz * gate * sig).astype(BF16)
            else:
                dz_ref[k] = dz.astype(BF16)

    row = pl.BlockSpec((tm, D), lambda i: (i, 0))
    vec = pl.BlockSpec((1, D), lambda i: (0, 0))
    blkk = pl.BlockSpec((nk, tm, kb), lambda i: (0, i, 0))
    in_specs = [row, row, vec, pl.BlockSpec(w4.shape, lambda i: (0, 0, 0, 0), pipeline_mode=pl.Buffered(1))]
    inputs = [dy, m, g_post, w4]
    n_dz = 2 * nk if swiglu else nk
    out_specs = [row, vec, pl.BlockSpec((n_dz, tm, kb), lambda i: (0, i, 0))]
    out_shape = [jax.ShapeDtypeStruct((T, D), BF16), jax.ShapeDtypeStruct((1, D), F32),
                 jax.ShapeDtypeStruct((n_dz, T, kb), BF16)]
    if swiglu:
        in_specs.append(pl.BlockSpec((2 * nk, tm, kb), lambda i: (0, i, 0)))
        inputs.append(gu)
    return pl.pallas_call(
        body, name=name, grid=(T // tm,), in_specs=in_specs, out_specs=out_specs, out_shape=out_shape,
        compiler_params=_params(1),
    )(*inputs)


def pre_bwd(pieces, x, g_pre, dres, name, transposed=False):
    T = x.shape[0]
    tm = _tile(T, TM, 16)
    na = len(pieces)
    weights = []
    for _, (w4, _), _ in pieces:
        if not any(w4 is w for w in weights):
            weights.append(w4)
    nw = len(weights)
    which = [[w4 is w for w in weights].index(True) for _, (w4, _), _ in pieces]

    def body(*refs):
        dz_refs, w_refs = refs[:na], refs[na:na + nw]
        x_ref, g_ref, dres_ref, dx_ref, dg_ref = refs[na + nw:]
        acc = None
        for a, (dz, (_, l), w_off) in enumerate(pieces):
            for p in range(dz.shape[0]):
                w = w_refs[which[a]][w_off + p, l]
                if transposed:
                    part = jnp.dot(dz_refs[a][p].astype(BF16), w, preferred_element_type=F32)
                else:
                    part = lax.dot_general(dz_refs[a][p].astype(BF16), w, NT, preferred_element_type=F32)
                acc = part if acc is None else acc + part
        dx, dg = _rms_bwd(x_ref[...], g_ref[...], acc)
        dx_ref[...] = dres_ref[...] + dx
        _accumulate(dg_ref, pl.program_id(0) == 0, dg)

    row = pl.BlockSpec((tm, D), lambda i: (i, 0))
    vec = pl.BlockSpec((1, D), lambda i: (0, 0))
    dz_specs = [pl.BlockSpec((dz.shape[0], tm, dz.shape[2]), lambda i: (0, i, 0)) for dz, _, _ in pieces]
    w_specs = [pl.BlockSpec(w.shape, lambda i: (0, 0, 0, 0), pipeline_mode=pl.Buffered(1)) for w in weights]
    return pl.pallas_call(
        body, name=name, grid=(T // tm,), in_specs=dz_specs + w_specs + [row, vec, row],
        out_specs=[row, vec], out_shape=[jax.ShapeDtypeStruct((T, D), F32), jax.ShapeDtypeStruct((1, D), F32)],
        compiler_params=_params(1),
    )(*[p[0] for p in pieces], *weights, x, g_pre, dres)


def wgrad(a_list, b_list, nout, out4_shape, name, a_cols=None, tm_cap=None, after=None):
    T = a_list[0].shape[1]
    tm = _tile(T, 4 * TM if tm_cap is None else tm_cap, 16)
    nt = T // tm
    dpb = out4_shape[0] // nout
    _, _, R, C = out4_shape
    na, nb = len(a_list), len(b_list)
    a_w = a_list[0].shape[2] if a_cols is None else a_cols
    a_per = a_list[0].shape[2] // a_w

    def spans(arrs, per):
        ns = [a.shape[0] * per for a in arrs]
        return ns, [sum(ns[:k]) for k in range(len(ns))], sum(ns)

    a_ns, a_offs, a_tot = spans(a_list, a_per)
    b_ns, b_offs, b_tot = spans(b_list, 1)
    assert a_tot in (1, nout) and b_tot in (1, nout)

    def body(*refs):
        a_refs, b_refs = refs[:na], refs[na:na + nb]
        out_ref, acc = refs[-2:]
        p, t = pl.program_id(0), pl.program_id(1)
        for ia in range(na):
            for ib in range(nb):
                conds = []
                if a_tot > 1:
                    conds += [p >= a_offs[ia], p < a_offs[ia] + a_ns[ia]]
                if b_tot > 1:
                    conds += [p >= b_offs[ib], p < b_offs[ib] + b_ns[ib]]

                def work(ia=ia, ib=ib):
                    part = lax.dot_general(a_refs[ia][0], b_refs[ib][0], TN, preferred_element_type=F32)
                    _accumulate(acc, t == 0, part)

                if conds:
                    pl.when(functools.reduce(jnp.logical_and, conds))(work)
                else:
                    work()

        @pl.when(t == nt - 1)
        def _():
            out_ref[...] = acc[...].astype(BF16).reshape(dpb, 1, R, C)

    def blk(off, n, tot):
        if tot == 1:
            return lambda p: 0
        return lambda p: jnp.clip(p - off, 0, n - 1)

    in_specs = []
    for arr, n, off in zip(a_list, a_ns, a_offs):
        in_specs.append(pl.BlockSpec((1, tm, a_w), lambda p, t, f=blk(off, n, a_tot): (f(p) // a_per, t, f(p) % a_per)))
    for arr, n, off in zip(b_list, b_ns, b_offs):
        in_specs.append(pl.BlockSpec((1, tm, arr.shape[2]), lambda p, t, f=blk(off, n, b_tot): (f(p), t, 0)))
    extra = [] if after is None else [after]
    return pl.pallas_call(
        body, name=name, grid=(nout, nt), in_specs=in_specs + [ANY] * len(extra),
        out_specs=pl.BlockSpec((dpb, 1, R, C), lambda p, t: (p, 0, 0, 0)),
        out_shape=jax.ShapeDtypeStruct(out4_shape, BF16),
        scratch_shapes=[pltpu.VMEM((dpb * R, C), F32)], compiler_params=_params(2),
    )(*a_list, *b_list, *extra)


def _swiglu_pro(i, k, gu_ref):
    gate, up = gu_ref[k].astype(F32), gu_ref[k + gu_ref.shape[0] // 2].astype(F32)
    return gate * jax.nn.sigmoid(gate) * up


def _attn_gate_pro(i, k, gate_ref, o_ref):
    return jax.nn.sigmoid(gate_ref[0].astype(F32)) * o_ref[...].astype(F32)


def _shift_rows(u, halo, d):
    rolled = pltpu.roll(u, d, 0)
    row = lax.broadcasted_iota(jnp.int32, u.shape, 0)
    for r in range(d):
        rolled = jnp.where(row == r, halo[HALO - d + r:HALO - d + r + 1, :], rolled)
    return rolled


def _advance_rows(u, halo, d):
    n = u.shape[0]
    rolled = pltpu.roll(u, n - d, 0)
    row = lax.broadcasted_iota(jnp.int32, u.shape, 0)
    for r in range(d):
        rolled = jnp.where(row == n - d + r, halo[r:r + 1, :], rolled)
    return rolled


def _make_conv_pro(tiles_per_seq):
    def pro(i, k, b_ref, c_ref, h_ref, ch_ref, hh_ref, ck_ref):
        u = c_ref[0].astype(F32) * h_ref[0].astype(F32)
        first = (i % tiles_per_seq) == 0
        halo = jnp.where(first, 0.0, ch_ref[0].astype(F32) * hh_ref[0].astype(F32))
        ck = ck_ref[...]
        conv = ck[2:3, :] * u + ck[1:2, :] * _shift_rows(u, halo, 1) + ck[0:1, :] * _shift_rows(u, halo, 2)
        return b_ref[0].astype(F32) * conv
    return pro


def conv_bwd_mix(dz, bch, conv_k, seq_len, name):
    T = dz.shape[1]
    tm = _tile(seq_len, TM, 16)
    tps = seq_len // tm
    nt = T // tm
    hb = tm // HALO

    def body(dz_ref, b_ref, c_ref, h_ref, cp_ref, hp_ref, dzn_ref, bn_ref, ck_ref, dbch_ref, dk_ref):
        i = pl.program_id(0)
        first = (i % tps) == 0
        last = (i % tps) == tps - 1
        b, c, h = b_ref[0].astype(F32), c_ref[0].astype(F32), h_ref[0].astype(F32)
        dzt = dz_ref[0].astype(F32)
        u = c * h
        prev = jnp.where(first, 0.0, cp_ref[0].astype(F32) * hp_ref[0].astype(F32))
        u1, u2 = _shift_rows(u, prev, 1), _shift_rows(u, prev, 2)
        ck = ck_ref[...]
        conv = ck[2:3, :] * u + ck[1:2, :] * u1 + ck[0:1, :] * u2
        dconv = dzt * b
        nxt = jnp.where(last, 0.0, dzn_ref[0].astype(F32) * bn_ref[0].astype(F32))
        du = ck[2:3, :] * dconv + ck[1:2, :] * _advance_rows(dconv, nxt, 1) + ck[0:1, :] * _advance_rows(dconv, nxt, 2)
        dbch_ref[0] = (dzt * conv).astype(BF16)
        dbch_ref[1] = (du * h).astype(BF16)
        dbch_ref[2] = (du * c).astype(BF16)
        tap = lax.broadcasted_iota(jnp.int32, (3, D), 0)
        dk = jnp.where(tap == 0, jnp.sum(dconv * u2, axis=0, keepdims=True),
                       jnp.where(tap == 1, jnp.sum(dconv * u1, axis=0, keepdims=True),
                                 jnp.sum(dconv * u, axis=0, keepdims=True)))
        _accumulate(dk_ref, i == 0, dk)

    def piece(p):
        return pl.BlockSpec((1, tm, D), lambda i, p=p: (p, i, 0))

    def prev(p):
        return pl.BlockSpec((1, HALO, D), lambda i, p=p: (p, jnp.maximum(i * hb - 1, 0), 0))

    def nxt(p):
        return pl.BlockSpec((1, HALO, D), lambda i, p=p: (p, jnp.minimum((i + 1) * hb, nt * hb - 1), 0))

    return pl.pallas_call(
        body, name=name, grid=(nt,),
        in_specs=[piece(0), piece(0), piece(1), piece(2), prev(1), prev(2), nxt(0), nxt(0),
                  pl.BlockSpec((3, D), lambda i: (0, 0))],
        out_specs=[pl.BlockSpec((3, tm, D), lambda i: (0, i, 0)), pl.BlockSpec((3, D), lambda i: (0, 0))],
        out_shape=[jax.ShapeDtypeStruct((3, T, D), BF16), jax.ShapeDtypeStruct((3, D), F32)],
        compiler_params=_params(1),
    )(dz, bch, bch, bch, bch, bch, dz, bch, conv_k)


def _log_sigmoid(x):
    return jnp.minimum(x, 0.0) - jnp.log(1.0 + jnp.exp(-jnp.abs(x)))


def forget_fwd(fl, fb, seq_len, name):
    T = fl.shape[1]

    def body(fl_ref, fb_ref, c_ref):
        c = _log_sigmoid(fl_ref[0] + fb_ref[...])
        row = lax.broadcasted_iota(jnp.int32, c.shape, 0)
        k = 1
        while k < seq_len:
            c = c + jnp.where(row >= k, pltpu.roll(c, k, 0), 0.0)
            k *= 2
        c_ref[...] = jnp.concatenate([jnp.broadcast_to(c[:, h:h + 1], (seq_len, HEAD_DIM)) for h in range(N_HEADS)], axis=1)

    return pl.pallas_call(
        body, name=name, grid=(T // seq_len,),
        in_specs=[pl.BlockSpec((1, seq_len, LANES), lambda b: (0, b, 0)), pl.BlockSpec((1, LANES), lambda b: (0, 0))],
        out_specs=pl.BlockSpec((seq_len, D), lambda b: (b, 0)),
        out_shape=jax.ShapeDtypeStruct((T, D), F32), compiler_params=_params(1),
    )(fl, fb)


def forget_bwd(dcb, fl, fb, seq_len, name):
    T = dcb.shape[0]
    pick = jnp.zeros((D, LANES), F32).at[HEAD_DIM * jnp.arange(N_HEADS), jnp.arange(N_HEADS)].set(1.0)

    def body(dc_ref, pick_ref, fl_ref, fb_ref, dfl_ref, dfb_ref):
        b = pl.program_id(0)
        r = jnp.dot(dc_ref[...], pick_ref[...], precision=lax.Precision.HIGHEST, preferred_element_type=F32)
        row = lax.broadcasted_iota(jnp.int32, r.shape, 0)
        k = 1
        while k < seq_len:
            r = r + jnp.where(row < seq_len - k, pltpu.roll(r, seq_len - k, 0), 0.0)
            k *= 2
        dfl = r * jax.nn.sigmoid(-(fl_ref[0] + fb_ref[...]))
        dfl_ref[0] = dfl
        _accumulate(dfb_ref, b == 0, jnp.sum(dfl, axis=0, keepdims=True))

    return pl.pallas_call(
        body, name=name, grid=(T // seq_len,),
        in_specs=[pl.BlockSpec((seq_len, D), lambda b: (b, 0)), pl.BlockSpec((D, LANES), lambda b: (0, 0)),
                  pl.BlockSpec((1, seq_len, LANES), lambda b: (0, b, 0)), pl.BlockSpec((1, LANES), lambda b: (0, 0))],
        out_specs=[pl.BlockSpec((1, seq_len, LANES), lambda b: (0, b, 0)), pl.BlockSpec((1, LANES), lambda b: (0, 0))],
        out_shape=[jax.ShapeDtypeStruct((1, T, LANES), F32), jax.ShapeDtypeStruct((1, LANES), F32)],
        compiler_params=_params(1),
    )(dcb, pick, fl, fb)


HEADS = (slice(0, HEAD_DIM), slice(HEAD_DIM, 2 * HEAD_DIM))


def _key_blocks(i, step, carry):
    carry = lax.fori_loop(0, i // 2, lambda jj, c: step(2 * jj, c, False, 2), carry)
    return lax.cond(i % 2 == 1, lambda c: step(i - 1, c, True, 2), lambda c: step(i, c, True, 1), carry)


def _causal(width, tq):
    keys = lax.broadcasted_iota(jnp.int32, (width * tq, tq), 0)
    return keys <= lax.broadcasted_iota(jnp.int32, (width * tq, tq), 1) + (width - 1) * tq


def attn_fwd(qg, kv, cb, n_seq, seq_len, name):
    T = n_seq * seq_len
    tq = _tile(seq_len, TQ, LANES)
    nq = seq_len // tq

    def body(q_ref, k_ref, v_ref, cb_ref, o_ref, lse_ref):
        i = pl.program_id(2)
        q8 = [q_ref[0, :, sl] * ATT_SCALE for sl in HEADS]

        def block(j, carry, diagonal, width):
            rows = pl.ds(pl.multiple_of(j * tq, tq), width * tq)
            out = []
            for hh, sl in enumerate(HEADS):
                m, l, acc = carry[3 * hh:3 * hh + 3]
                s = lax.dot_general(k_ref[0, rows, sl], q8[hh], NT, preferred_element_type=F32)
                s = s - cb_ref[rows, sl.start:sl.start + 1]
                if diagonal:
                    s = jnp.where(_causal(width, tq), s, -1e30)
                m_new = jnp.maximum(m, jnp.max(s, axis=0, keepdims=True))
                a = jnp.exp(m - m_new)
                p = jnp.exp(s - m_new)
                l = a * l + jnp.sum(p, axis=0, keepdims=True)
                acc = a * acc + lax.dot_general(v_ref[0, rows, sl], p.astype(BF16), TN, preferred_element_type=F32)
                out += [m_new, l, acc]
            return tuple(out)

        init = (jnp.full((1, tq), -1e30, F32), jnp.zeros((1, tq), F32), jnp.zeros((HEAD_DIM, tq), F32)) * 2
        carry = _key_blocks(i, block, init)
        o_ref[...] = jnp.concatenate([carry[2] / carry[1], carry[5] / carry[4]], axis=0).T
        for hh in range(2):
            lse_ref[0, 0, 0, hh:hh + 1, :] = carry[3 * hh] + jnp.log(carry[3 * hh + 1])

    seq2 = pl.BlockSpec((seq_len, LANES), lambda b, hp, i: (b, hp))
    return pl.pallas_call(
        body, name=name, grid=(n_seq, N_HEADS // 2, nq),
        in_specs=[pl.BlockSpec((1, tq, LANES), lambda b, hp, i: (0, b * nq + i, hp)),
                  pl.BlockSpec((1, seq_len, LANES), lambda b, hp, i: (0, b, hp)),
                  pl.BlockSpec((1, seq_len, LANES), lambda b, hp, i: (1, b, hp)), seq2],
        out_specs=[pl.BlockSpec((tq, LANES), lambda b, hp, i: (b * nq + i, hp)),
                   pl.BlockSpec((1, 1, 1, 2, tq), lambda b, hp, i: (b, hp, i, 0, 0))],
        out_shape=[jax.ShapeDtypeStruct((T, D), F32), jax.ShapeDtypeStruct((n_seq, N_HEADS // 2, nq, 2, tq), F32)],
        compiler_params=_params(3),
    )(qg, kv, kv, cb)


def attn_bwd(dz, qg, kv, o, lse, cb, n_seq, seq_len, name):
    T = n_seq * seq_len
    tq = _tile(seq_len, TQ, LANES)
    nq = seq_len // tq

    def body(dz_ref, q_ref, gate_ref, o_ref, lse_ref, cb_ref, k_ref, v_ref,
             dq_ref, dgate_ref, dk_ref, dv_ref, dc_ref, p_s, dp_s, dk_s, dv_s, dc_s):
        i = pl.program_id(2)

        @pl.when(i == 0)
        def _():
            dk_s[...] = jnp.zeros_like(dk_s)
            dv_s[...] = jnp.zeros_like(dv_s)
            dc_s[...] = jnp.zeros_like(dc_s)

        dzf = dz_ref[...].astype(F32)
        sig = jax.nn.sigmoid(gate_ref[0].astype(F32))
        dob = (dzf * sig).astype(BF16)
        dgate_ref[0] = (dzf * o_ref[...] * sig * (1.0 - sig)).astype(BF16)
        q8 = [q_ref[0, :, sl] * ATT_SCALE for sl in HEADS]
        do = [dob[:, sl] for sl in HEADS]
        lse_i = [lse_ref[0, 0, 0, hh:hh + 1, :] for hh in range(2)]

        def probs(j, dsums, diagonal, width):
            rows = pl.ds(pl.multiple_of(j * tq, tq), width * tq)
            out = []
            for hh, sl in enumerate(HEADS):
                s = lax.dot_general(k_ref[0, rows, sl], q8[hh], NT, preferred_element_type=F32)
                p = jnp.exp(s - cb_ref[rows, sl.start:sl.start + 1] - lse_i[hh])
                if diagonal:
                    p = jnp.where(_causal(width, tq), p, 0.0)
                dp = lax.dot_general(v_ref[0, rows, sl], do[hh], NT, preferred_element_type=F32)
                p_s[hh, rows, :] = p
                dp_s[hh, rows, :] = dp
                out.append(dsums[hh] + jnp.sum(p * dp, axis=0, keepdims=True))
            return tuple(out)

        dsums = _key_blocks(i, probs, (jnp.zeros((1, tq), F32),) * 2)

        def grads(j, dqs, diagonal, width):
            rows = pl.ds(pl.multiple_of(j * tq, tq), width * tq)
            out = []
            for hh, sl in enumerate(HEADS):
                p = p_s[hh, rows, :]
                ds = p * (dp_s[hh, rows, :] - dsums[hh])
                dc_s[hh, rows, :] -= jnp.sum(ds, axis=1, keepdims=True)
                dsb = ds.astype(BF16)
                dk_s[rows, sl] += jnp.dot(dsb, q8[hh], preferred_element_type=F32)
                dv_s[rows, sl] += jnp.dot(p.astype(BF16), do[hh], preferred_element_type=F32)
                out.append(dqs[hh] + lax.dot_general(k_ref[0, rows, sl], dsb, TN, preferred_element_type=F32))
            return tuple(out)

        dqs = _key_blocks(i, grads, (jnp.zeros((HEAD_DIM, tq), F32),) * 2)
        dq_ref[0] = (jnp.concatenate(dqs, axis=0).T * ATT_SCALE).astype(BF16)

        @pl.when(i == nq - 1)
        def _():
            dk_ref[0] = dk_s[...].astype(BF16)
            dv_ref[0] = dv_s[...].astype(BF16)
            dc_ref[...] = jnp.zeros_like(dc_ref)
            for hh, sl in enumerate(HEADS):
                dc_ref[:, sl.start:sl.start + 1] = dc_s[hh]

    qry2 = pl.BlockSpec((tq, LANES), lambda b, hp, i: (b * nq + i, hp))
    seq2 = pl.BlockSpec((seq_len, LANES), lambda b, hp, i: (b, hp))

    def qry3(p):
        return pl.BlockSpec((1, tq, LANES), lambda b, hp, i, p=p: (p, b * nq + i, hp))

    def seq3(p):
        return pl.BlockSpec((1, seq_len, LANES), lambda b, hp, i, p=p: (p, b, hp))

    act = jax.ShapeDtypeStruct((1, T, D), BF16)
    return pl.pallas_call(
        body, name=name, grid=(n_seq, N_HEADS // 2, nq),
        in_specs=[qry2, qry3(0), qry3(1), qry2, pl.BlockSpec((1, 1, 1, 2, tq), lambda b, hp, i: (b, hp, i, 0, 0)), seq2,
                  seq3(0), seq3(1)],
        out_specs=[qry3(0), qry3(0), seq3(0), seq3(0), seq2],
        out_shape=[act, act, act, act, jax.ShapeDtypeStruct((T, D), F32)],
        scratch_shapes=[pltpu.VMEM((2, seq_len, tq), F32), pltpu.VMEM((2, seq_len, tq), F32),
                        pltpu.VMEM((seq_len, LANES), F32), pltpu.VMEM((seq_len, LANES), F32),
                        pltpu.VMEM((2, seq_len, 1), F32)],
        compiler_params=_params(3),
    )(dz, qg, qg, o, lse, cb, kv, kv)


def _adamw(w, g, m, v):
    m = ADAM_B1 * m + (1.0 - ADAM_B1) * g
    v = ADAM_B2 * v + (1.0 - ADAM_B2) * (g * g)
    m_hat = m / (1.0 - ADAM_B1 ** ADAM_STEP)
    v_hat = v / (1.0 - ADAM_B2 ** ADAM_STEP)
    delta = -ADAM_LR * (m_hat / (jnp.sqrt(v_hat) + ADAM_EPS) + ADAM_WD * w)
    return delta, m, v


def adam_sharded(w, m, v, contribs, me, name):
    L, R, C = w.shape
    tr = _tile(R, 256, 16) if R % 16 == 0 else R

    def body(me_ref, w_ref, m_ref, v_ref, *refs):
        c_refs, (g_ref, d_ref, nm_ref, nv_ref) = refs[:2 * L], refs[2 * L:]
        l = pl.program_id(0)
        for j in range(L):
            @pl.when(l == j)
            def _(j=j):
                own_ref, recv_ref = c_refs[2 * j], c_refs[2 * j + 1]
                g = own_ref[0].astype(F32)
                for k in range(N_DEV - 1):
                    g = g + recv_ref[k].astype(F32)
                delta, nm, nv = _adamw(w_ref[0], g, m_ref[0], v_ref[0])
                g_ref[0] = g
                d_ref[0] = delta
                nm_ref[0] = nm
                nv_ref[0] = nv

    blk = pl.BlockSpec((1, tr, C), lambda l, i, s: (l, i, 0))
    in_specs = [blk, blk, blk]
    inputs = [w, m, v]
    for j, (mine, recv) in enumerate(contribs):
        in_specs.append(pl.BlockSpec((1, tr, C), lambda l, i, s, j=j: (s[0], jnp.where(l == j, i, 0), 0)))
        in_specs.append(pl.BlockSpec((N_DEV - 1, tr, C), lambda l, i, s, j=j: (0, jnp.where(l == j, i, 0), 0)))
        inputs += [mine, recv]
    shp = jax.ShapeDtypeStruct((L, R, C), F32)
    grid_spec = pltpu.PrefetchScalarGridSpec(num_scalar_prefetch=1, grid=(L, R // tr), in_specs=in_specs, out_specs=[blk] * 4)
    return pl.pallas_call(body, name=name, grid_spec=grid_spec, out_shape=[shp] * 4, compiler_params=_params(2))(me, *inputs)


def cast_place(w, l, me, dtype, after, name):
    _, R, C = w.shape
    tr = _tile(R, 512, 16) if R % 16 == 0 else R

    def body(me_ref, w_ref, after_ref, o_ref):
        o_ref[0] = w_ref[0].astype(dtype)

    grid_spec = pltpu.PrefetchScalarGridSpec(
        num_scalar_prefetch=1, grid=(R // tr,), in_specs=[pl.BlockSpec((1, tr, C), lambda i, s: (l, i, 0)), ANY],
        out_specs=pl.BlockSpec((1, tr, C), lambda i, s: (s[0], i, 0)))
    return pl.pallas_call(body, name=name, grid_spec=grid_spec, out_shape=jax.ShapeDtypeStruct((N_DEV, R, C), dtype),
                          compiler_params=_params(1))(me, w, after)


def adam_small(params, total, extra_grads, name):
    n, ne = len(params), len(extra_grads)

    def body(*refs):
        total_ref, extra_refs = refs[0], refs[1:1 + ne]
        ins, outs = refs[1 + ne:1 + ne + 3 * n], refs[1 + ne + 3 * n:]
        for k, (_, _, _, where) in enumerate(params):
            if isinstance(where, int):
                g = extra_refs[where][...]
            else:
                row, rows, width = where
                g = total_ref[row:row + rows, 0:width]
            delta, nm, nv = _adamw(ins[3 * k][...], g, ins[3 * k + 1][...], ins[3 * k + 2][...])
            outs[4 * k][...] = g
            outs[4 * k + 1][...] = delta
            outs[4 * k + 2][...] = nm
            outs[4 * k + 3][...] = nv

    flat = [a for w, m, v, _ in params for a in (w, m, v)]
    out_shape = [jax.ShapeDtypeStruct(w.shape, F32) for w, _, _, _ in params for _ in range(4)]
    res = pl.pallas_call(body, name=name, out_shape=out_shape)(total, *extra_grads, *flat)
    return [res[4 * k:4 * k + 4] for k in range(n)]


def _place():
    return lax.axis_index("x"), lax.axis_index("y"), lax.axis_index("c")


def _peer(place, k):
    x, y, c = place
    return x ^ (k >> 2), y ^ ((k >> 1) & 1), c ^ (k & 1)


ANY = pl.BlockSpec(memory_space=pl.ANY)
HBM = pl.BlockSpec(memory_space=pltpu.HBM)
SEM = pl.BlockSpec(memory_space=pltpu.SEMAPHORE)
EFFECT = pltpu.SideEffectType.DATAFLOW_SIDE_EFFECTING


def _in_hbm(a):
    return pltpu.with_memory_space_constraint(a, pltpu.HBM)


def _number(place):
    return 4 * place[0] + 2 * place[1] + place[2]


SLOTS = {"gather_like": 4, "gather_pass": 3, "scatter": 7}


def _plan(mode, src_ref, land_ref, place):
    if mode == "gather_like":
        mine = land_ref.at[_number(place)]
        return [(mine, mine, _peer(place, k)) for k in (1, 2, 4, 6)]
    if mode == "gather_pass":
        slots = [land_ref.at[_number(_peer(place, k))] for k in (2, 4, 6)]
        return [(slot, slot, _peer(place, 1)) for slot in slots]
    return [(src_ref.at[_number(_peer(place, k))], land_ref.at[k - 1], _peer(place, k)) for k in range(1, N_DEV)]


def _exchange(name, groups, start, afters):
    sizes = [len(g[3]) for g in groups]
    na, ng = sum(sizes), len(groups)
    ns = sum(len(g[2]) for g in groups)
    waits = groups[0][0] is not None
    n_in_sems = 2 * ng if waits else 0
    n_out_sems = 2 * ng if start else 0

    def body(*refs):
        src_refs, land_refs = (refs[:ns] if ns else [None] * na), refs[ns:ns + na]
        in_sems = refs[ns + na:ns + na + n_in_sems]
        outs = refs[ns + na + n_in_sems + len(afters):]
        place = _place()
        a = 0
        for gi, n in enumerate(sizes):
            for idx in range(n):
                if waits:
                    zone = land_refs[a].at[pl.ds(0, groups[gi][4])]
                    copy = pltpu.make_async_remote_copy(
                        src_ref=zone, dst_ref=zone, send_sem=in_sems[2 * gi].at[idx], recv_sem=in_sems[2 * gi + 1].at[idx],
                        device_id=_peer(place, 1), device_id_type=MESH)
                    copy.wait_send()
                    copy.wait_recv()
                if start:
                    for src, dst, peer in _plan(start, src_refs[a], land_refs[a], place):
                        pltpu.make_async_remote_copy(
                            src_ref=src, dst_ref=dst, send_sem=outs[2 * gi].at[idx], recv_sem=outs[2 * gi + 1].at[idx],
                            device_id=peer, device_id_type=MESH).start()
                a += 1
        if start:
            outs[-1][...] = jnp.zeros_like(outs[-1])

    srcs = [_in_hbm(s) for g in groups for s in g[2]]
    lands = [_in_hbm(l) for g in groups for l in g[3]]
    sems = [s for g in groups for s in g[:2]] if waits else []
    out_shape = [pltpu.SemaphoreType.DMA((n,)) for n in sizes for _ in range(2)] if start else []
    out_shape += [pltpu.HBM(a.shape, a.dtype) for a in srcs + lands]
    out_specs = [SEM] * n_out_sems + [HBM] * (ns + na)
    if start:
        out_shape.append(jax.ShapeDtypeStruct((8, LANES), F32))
        out_specs.append(pl.BlockSpec(memory_space=pltpu.VMEM))
    res = pl.pallas_call(
        body, name=name, in_specs=[HBM] * (ns + na) + [SEM] * n_in_sems + [ANY] * len(afters),
        out_shape=out_shape, out_specs=out_specs,
        input_output_aliases={i: n_out_sems + i for i in range(ns + na)},
        compiler_params=pltpu.CompilerParams(has_side_effects=EFFECT),
    )(*srcs, *lands, *sems, *afters)
    new_sems, thru = res[:n_out_sems], res[n_out_sems:n_out_sems + ns + na]
    out, a = [], 0
    for gi, n in enumerate(sizes):
        pair = (new_sems[2 * gi], new_sems[2 * gi + 1]) if start else (None, None)
        out.append(pair + (thru[a:a + n] if ns else [], thru[ns + a:ns + a + n], SLOTS.get(start, 0)))
        a += n
    return out, (res[-1] if start else None)


def exchange_start(pair_groups, mode, name):
    groups = [(None, None, [s for s, _ in g if s is not None], [l for _, l in g], 0) for g in pair_groups]
    return _exchange(name, groups, mode, ())


def exchange_relay(groups, mode, afters, name):
    return _exchange(name, groups, mode, afters)


def exchange_wait(groups, afters, name):
    done, _ = _exchange(name, groups, None, afters)
    return [(g[2], g[3]) for g in done]


def all_reduce_small(parts, n_rows, after, name):
    R = n_rows
    n_parts = len(parts)

    def body(*refs):
        part_refs = refs[:n_parts]
        out_ref, buf, send_sems, recv_sems = refs[n_parts + 1:]
        x, y, c = _place()
        me = 4 * x + 2 * y + c
        own = buf.at[me]
        own[...] = jnp.zeros((R, D), F32)
        for ref, (arr, row) in zip(part_refs, parts):
            own[row:row + arr.shape[0], 0:arr.shape[1]] = ref[...]
        copies = []
        for k in range(1, N_DEV):
            peer = (x ^ (k >> 2), y ^ ((k >> 1) & 1), c ^ (k & 1))
            copies.append(pltpu.make_async_remote_copy(
                src_ref=own, dst_ref=own, send_sem=send_sems.at[k - 1], recv_sem=recv_sems.at[k - 1],
                device_id=peer, device_id_type=MESH))
        for cp in copies:
            cp.start()
        for cp in copies:
            cp.wait()
        total = buf[0]
        for d in range(1, N_DEV):
            total = total + buf[d]
        out_ref[...] = total

    vm = pl.BlockSpec(memory_space=pltpu.VMEM)
    return pl.pallas_call(
        body, name=name, in_specs=[vm] * n_parts + [ANY], out_specs=vm, out_shape=jax.ShapeDtypeStruct((R, D), F32),
        scratch_shapes=[pltpu.VMEM((N_DEV, R, D), F32), pltpu.SemaphoreType.DMA((N_DEV - 1,)),
                        pltpu.SemaphoreType.DMA((N_DEV - 1,))],
    )(*[arr for arr, _ in parts], after)


def _col_blocks(gathered, n_blocks):
    n, d, w = gathered.shape
    whole = gathered.transpose(1, 0, 2).reshape(d, n * w)
    return whole.reshape(d, n_blocks, n * w // n_blocks).transpose(1, 0, 2)[:, None]


def _col_shards(blocks):
    n, d, w = blocks.shape
    whole = blocks.transpose(1, 0, 2).reshape(d, n * w)
    return whole.reshape(d, N_DEV, n * w // N_DEV).transpose(1, 0, 2)


def kernel(x, ffn1_pre_g, ffn1_post_g, ffn1_w_in, ffn1_w_out, mix_pre_g, mix_post_g, ffn2_pre_g, ffn2_post_g, ffn2_w_in, ffn2_w_out, conv_w_in, conv_k, conv_w_out, kv_g, kv_w, forget_b, attn_w_qg, attn_w_o, loss_target, m_ffn1_pre_g, m_ffn1_post_g, m_ffn1_w_in, m_ffn1_w_out, m_mix_pre_g, m_mix_post_g, m_ffn2_pre_g, m_ffn2_post_g, m_ffn2_w_in, m_ffn2_w_out, m_conv_w_in, m_conv_k, m_conv_w_out, m_kv_g, m_kv_w, m_forget_b, m_attn_w_qg, m_attn_w_o, v_ffn1_pre_g, v_ffn1_post_g, v_ffn1_w_in, v_ffn1_w_out, v_mix_pre_g, v_mix_post_g, v_ffn2_pre_g, v_ffn2_post_g, v_ffn2_w_in, v_ffn2_w_out, v_conv_w_in, v_conv_k, v_conv_w_out, v_kv_g, v_kv_w, v_forget_b, v_attn_w_qg, v_attn_w_o):
    n_seq, seq_len, _ = x.shape
    T = n_seq * seq_len
    xi, yi, ci = _place()
    dev = 4 * xi + 2 * yi + ci
    x0 = x.reshape(T, D)
    target = loss_target.reshape(T, D)

    me = dev.reshape(1).astype(jnp.int32)
    w1_t, w2_t, kv_t = ffn1_w_in.transpose(0, 2, 1), ffn2_w_in.transpose(0, 2, 1), kv_w.T[None]

    def zones(specs, after):
        return [(None, cast_place(w, l, me, dt, after, f"place_{nm}")) for nm, w, l, dt in specs]

    gathers, token = exchange_start([zones([("w1_in0", w1_t, 0, BF16)], me)], "gather_like", "gather_start0")
    shard_groups = [
        [("w1_out0", ffn1_w_out, 0, BF16)],
        [("cw_in", conv_w_in, 0, BF16), ("cw_out", conv_w_out, 0, BF16), ("ck", conv_k, 0, F32)],
        [("w2_in0", w2_t, 0, BF16), ("w2_out0", ffn2_w_out, 0, BF16)],
        [("kv", kv_t, 0, BF16), ("w1_in1", w1_t, 1, BF16), ("w1_out1", ffn1_w_out, 1, BF16),
         ("qg", attn_w_qg, 0, BF16), ("ow", attn_w_o, 0, BF16)],
        [("w2_in1", w2_t, 1, BF16), ("w2_out1", ffn2_w_out, 1, BF16)]]
    later, token = exchange_start([zones(g, token) for g in shard_groups], "gather_like", "gather_start1")
    gathers = gathers + later

    def gathered(k, after):
        passed, _ = exchange_relay([gathers[k]], "gather_pass", [after], f"gather_pass{k}")
        return [z[:, None] for z in exchange_wait(passed, [after], f"gather_wait{k}")[0][1]]

    fb = jnp.pad(forget_b, (0, LANES - N_HEADS))[None]

    def vec(g, l):
        return g[l:l + 1]

    def behind(g, tok):
        return g + tok[:1, :1]

    grads_small = {}
    tm_ffn = _tile(T, TM, 16)

    def ffn_fwd(xin, g_pre, g_post, w_in, w_out, tag, loss_target=None):
        w_gu = w_in.reshape(2, 1, -1, D)
        xn, (gu,) = rms_proj(xin, g_pre, [(w_gu, 0)], [BF16], f"{tag}_in", transposed=True)
        specs = [pl.BlockSpec((2, tm_ffn, gu.shape[2]), lambda i: (0, i, 0))]
        if callable(w_out):
            w_out = w_out(xn)
        a, h, *y = mix_out(_swiglu_pro, [gu], specs, (w_out, 0), xin, g_post, 0.5, 1, f"{tag}_out", tm=tm_ffn,
                           loss_target=loss_target)
        return (y[0] if loss_target is None else tuple(y)), (xin, xn, gu, a, h), w_out

    def ffn_bwd(dy, saved, g_pre, g_post, w_in, w_out, tag):
        xin, xn, gu, a, h = saved
        w_gu = w_in.reshape(2, 1, -1, D)
        half = gu.shape[2] // 2
        dh, dg_post, dgu = post_bwd(dy, h, g_post, 0.5, (w_out, 0), 1, f"{tag}_bwd_out", gu=gu, tm=tm_ffn)
        dw_out = wgrad([a], [dh[None]], 2, (2, 1, half, D), f"{tag}_dw_out", a_cols=half)
        started_out, tok = scatter_start([dw_out.reshape(8, -1, D)], f"{tag}_out_scatter_start")
        dw_in = wgrad([dgu], [xn[None]], 4, (4, 1, half, D), f"{tag}_dw_in", a_cols=half, after=tok)
        started_in, tok = scatter_start([dw_in.reshape(8, -1, D)], f"{tag}_in_scatter_start")
        dx, dg_pre = pre_bwd([(dgu, (w_gu, 0), 0)], xin, behind(g_pre, tok), dy, f"{tag}_bwd_in", transposed=True)
        return dx, dg_pre, dg_post, [started_in, started_out]

    def scatter_start(blocked, name):
        pairs = [(g, lax.empty((N_DEV - 1,) + g.shape[1:], g.dtype)) for g in blocked]
        started, tok = exchange_start([pairs], "scatter", name)
        return started[0], tok

    (w1_in,) = gathered(0, token)
    x1, s_f1a, w1_out = ffn_fwd(x0, vec(ffn1_pre_g, 0), vec(ffn1_post_g, 0), w1_in, lambda after: gathered(1, after)[0],
                                "l0_ffn1")
    cw_in_g, cw_out, ck_g = gathered(2, x1)
    cw_in = _col_blocks(cw_in_g[:, 0], 3)
    ck = ck_g[:, 0].transpose(1, 0, 2).reshape(3, D)
    xn_c, (bch,) = rms_proj(x1, vec(mix_pre_g, 0), [(cw_in, 0)], [BF16], "conv_in")
    tmc = _tile(seq_len, TM, 16)
    hb = tmc // HALO

    def cpiece(p):
        return pl.BlockSpec((1, tmc, D), lambda i, p=p: (p, i, 0))

    def chalo(p):
        return pl.BlockSpec((1, HALO, D), lambda i, p=p: (p, jnp.maximum(i * hb - 1, 0), 0))

    conv_specs = [cpiece(0), cpiece(1), cpiece(2), chalo(1), chalo(2), pl.BlockSpec((3, D), lambda i: (0, 0))]
    z_c, m_c, x2 = mix_out(_make_conv_pro(seq_len // tmc), [bch, bch, bch, bch, bch, ck], conv_specs, (cw_out, 0),
                           x1, vec(mix_post_g, 0), 1.0, 1, "conv_out", tm=tmc)
    w2_in, w2_out = gathered(3, x2)
    x3, s_f2a, _ = ffn_fwd(x2, vec(ffn2_pre_g, 0), vec(ffn2_post_g, 0), w2_in, w2_out, "l0_ffn2")

    kvw_g, w1_in_b, w1_out_b, qgw_g, ow = gathered(4, x3)
    qg_w = _col_blocks(qgw_g[:, 0], 2)
    kv_whole = kvw_g.reshape(2 * D + N_HEADS, D).T
    kv_wb = kv_whole[:, :2 * D].reshape(D, 2, D).transpose(1, 0, 2)[:, None]
    f_w = jnp.pad(kv_whole[:, 2 * D:], ((0, 0), (0, LANES - N_HEADS)))[None, None]
    xn_kv, (kv, fl) = rms_proj(x3, kv_g[None], [(kv_wb, 0), (f_w, 0)], [BF16, F32], "kv_in")
    cb = forget_fwd(fl, fb, seq_len, "forget_fwd")

    x4, s_f1b, _ = ffn_fwd(x3, vec(ffn1_pre_g, 1), vec(ffn1_post_g, 1), w1_in_b, w1_out_b, "l1_ffn1")
    xn_a, (qg,) = rms_proj(x4, vec(mix_pre_g, 1), [(qg_w, 0)], [BF16], "attn_in")
    o, lse = attn_fwd(qg, kv, cb, n_seq, seq_len, "attn_fwd")
    tm = _tile(T, TM, 16)
    gate_specs = [pl.BlockSpec((1, tm, D), lambda i: (1, i, 0)), pl.BlockSpec((tm, D), lambda i: (i, 0))]
    z_a, m_a, x5 = mix_out(_attn_gate_pro, [qg, o], gate_specs, (ow, 0), x4, vec(mix_post_g, 1), 1.0, 1, "attn_out")
    w2_in_b, w2_out_b = gathered(5, x5)
    (dy, loss_part), s_f2b, _ = ffn_fwd(x5, vec(ffn2_pre_g, 1), vec(ffn2_post_g, 1), w2_in_b, w2_out_b, "l1_ffn2",
                                        loss_target=target)

    scatters = {}
    dx5, dg, dgp, scatters["ffn2", 1] = ffn_bwd(dy, s_f2b, vec(ffn2_pre_g, 1), vec(ffn2_post_g, 1), w2_in_b, w2_out_b, "l1_ffn2")
    grads_small["ffn2_pre", 1], grads_small["ffn2_post", 1] = dg, dgp
    dm_a, dgp, dz_a = post_bwd(dx5, m_a, vec(mix_post_g, 1), 1.0, (ow, 0), 1, "attn_bwd_out")
    grads_small["mix_post", 1] = dgp
    dq, dgate, dk, dv, dcb = attn_bwd(dz_a[0], qg, kv, o, lse, cb, n_seq, seq_len, "attn_bwd")
    dx4, dg = pre_bwd([(dq, (qg_w, 0), 0), (dgate, (qg_w, 0), 1)], x4, vec(mix_pre_g, 1), dx5, "attn_bwd_in")
    grads_small["mix_pre", 1] = dg
    d_ow = wgrad([z_a], [dm_a[None]], 1, ow.shape, "attn_dw_o")
    d_qgw = wgrad([xn_a[None]], [dq, dgate], 2, (2, 1, D, D), "attn_dw_qg")
    scatters["attn"], tok = scatter_start([_col_shards(d_qgw[:, 0]), d_ow.reshape(8, -1, D)], "attn_scatter_start")
    dx3, dg, dgp, scatters["ffn1", 1] = ffn_bwd(dx4, s_f1b, vec(ffn1_pre_g, 1), behind(vec(ffn1_post_g, 1), tok),
                                                w1_in_b, w1_out_b, "l1_ffn1")
    grads_small["ffn1_pre", 1], grads_small["ffn1_post", 1] = dg, dgp

    dfl, dfb = forget_bwd(dcb, fl, fb, seq_len, "forget_bwd")
    dx3, dg_kv = pre_bwd([(dk, (kv_wb, 0), 0), (dv, (kv_wb, 0), 1), (dfl, (f_w, 0), 0)], x3, kv_g[None], dx3, "kv_bwd_in")
    d_kvw = wgrad([xn_kv[None]], [dk, dv], 2, (2, 1, D, D), "kv_dw")
    d_fw = wgrad([xn_kv[None]], [dfl.astype(BF16)], 1, (1, 1, D, LANES), "forget_dw")
    d_kv_whole = jnp.concatenate([d_kvw[0, 0], d_kvw[1, 0], d_fw[0, 0, :, :N_HEADS]], axis=1)
    wshard = D * 2 + N_HEADS
    scatters["kv"], tok = scatter_start([d_kv_whole.T.reshape(N_DEV, wshard // N_DEV, D)], "kv_scatter_start")

    dx2, dg, dgp, scatters["ffn2", 0] = ffn_bwd(dx3, s_f2a, vec(ffn2_pre_g, 0), behind(vec(ffn2_post_g, 0), tok),
                                                w2_in, w2_out, "l0_ffn2")
    grads_small["ffn2_pre", 0], grads_small["ffn2_post", 0] = dg, dgp
    dm_c, dgp, dz_c = post_bwd(dx2, m_c, vec(mix_post_g, 0), 1.0, (cw_out, 0), 1, "conv_bwd_out")
    grads_small["mix_post", 0] = dgp
    dbch, d_ck = conv_bwd_mix(dz_c, bch, ck, seq_len, "conv_bwd_mix")
    dx1, dg = pre_bwd([(dbch, (cw_in, 0), 0)], x1, vec(mix_pre_g, 0), dx2, "conv_bwd_in")
    grads_small["mix_pre", 0] = dg
    d_cw_out = wgrad([z_c], [dm_c[None]], 1, cw_out.shape, "conv_dw_out")
    d_cw_in = wgrad([xn_c[None]], [dbch], 3, (3, 1, D, D), "conv_dw_in")
    scatters["conv"], tok = scatter_start([_col_shards(d_cw_in[:, 0]), d_cw_out.reshape(8, -1, D)], "conv_scatter_start")
    dx0, dg, dgp, scatters["ffn1", 0] = ffn_bwd(dx1, s_f1a, vec(ffn1_pre_g, 0), behind(vec(ffn1_post_g, 0), tok),
                                                w1_in, w1_out, "l0_ffn1")
    grads_small["ffn1_pre", 0], grads_small["ffn1_post", 0] = dg, dgp

    parts_of = {}

    def scatter_end(keys, afters, name):
        flat = [(k, g) for k in keys for g in (scatters[k] if isinstance(scatters[k], list) else [scatters[k]])]
        for (k, _), (sent, recv) in zip(flat, exchange_wait([g for _, g in flat], afters, name)):
            parts_of.setdefault(k, []).extend(zip(sent, recv))

    scatter_end([("ffn2", 1), "attn", ("ffn1", 1), "kv", ("ffn2", 0), "conv"], [dx0], "scatter_wait")
    sharded = {"ffn2_w_in": (ffn2_w_in, m_ffn2_w_in, v_ffn2_w_in), "ffn2_w_out": (ffn2_w_out, m_ffn2_w_out, v_ffn2_w_out),
               "conv_w_in": (conv_w_in, m_conv_w_in, v_conv_w_in), "conv_w_out": (conv_w_out, m_conv_w_out, v_conv_w_out),
               "kv_w": (kv_w, m_kv_w, v_kv_w), "attn_w_qg": (attn_w_qg, m_attn_w_qg, v_attn_w_qg),
               "attn_w_o": (attn_w_o, m_attn_w_o, v_attn_w_o),
               "ffn1_w_in": (ffn1_w_in, m_ffn1_w_in, v_ffn1_w_in), "ffn1_w_out": (ffn1_w_out, m_ffn1_w_out, v_ffn1_w_out)}
    out = {}
    for nm, (w, mm, vv) in sharded.items():
        if nm == "ffn1_w_in":
            scatter_end([("ffn1", 0)], [res[0] for res in out.values()], "scatter_wait_last")
        contribs = {
            "ffn1_w_in": lambda: [parts_of["ffn1", 0][0], parts_of["ffn1", 1][0]],
            "ffn1_w_out": lambda: [parts_of["ffn1", 0][1], parts_of["ffn1", 1][1]],
            "ffn2_w_in": lambda: [parts_of["ffn2", 0][0], parts_of["ffn2", 1][0]],
            "ffn2_w_out": lambda: [parts_of["ffn2", 0][1], parts_of["ffn2", 1][1]],
            "conv_w_in": lambda: [parts_of["conv"][0]], "conv_w_out": lambda: [parts_of["conv"][1]],
            "kv_w": lambda: [parts_of["kv"][0]], "attn_w_qg": lambda: [parts_of["attn"][0]],
            "attn_w_o": lambda: [parts_of["attn"][1]]}[nm]()
        if nm in ("ffn1_w_in", "ffn2_w_in", "kv_w"):
            rows, cols = w.shape[-2:]

            def view(a):
                return a.reshape(-1, rows, cols).transpose(0, 2, 1)

            res = adam_sharded(view(w), view(mm), view(vv), contribs, me, f"adam_{nm}")
            out[nm] = [r.transpose(0, 2, 1).reshape(w.shape) for r in res]
        else:
            shape3 = (len(contribs),) + contribs[0][0].shape[1:]
            res = adam_sharded(w.reshape(shape3), mm.reshape(shape3), vv.reshape(shape3), contribs, me, f"adam_{nm}")
            out[nm] = [r.reshape(w.shape) for r in res]

    small_names = ["ffn1_pre", "ffn1_post", "mix_pre", "mix_post", "ffn2_pre", "ffn2_post"]
    parts = [(grads_small[n, l], 2 * k + l) for k, n in enumerate(small_names) for l in range(2)]
    parts += [(dg_kv, 12), (dfb, 13), (d_ck, 14), (loss_part, 17)]
    total = all_reduce_small(parts, 24, out["ffn1_w_in"][0], "all_reduce_small")
    loss = total[17, 0]
    d_ck_mine = lax.dynamic_slice(total, (14, dev * LANES), (3, LANES))
    gains = [(ffn1_pre_g, m_ffn1_pre_g, v_ffn1_pre_g), (ffn1_post_g, m_ffn1_post_g, v_ffn1_post_g),
             (mix_pre_g, m_mix_pre_g, v_mix_pre_g), (mix_post_g, m_mix_post_g, v_mix_post_g),
             (ffn2_pre_g, m_ffn2_pre_g, v_ffn2_pre_g), (ffn2_post_g, m_ffn2_post_g, v_ffn2_post_g)]
    small_params = [(w, m, v, (2 * k, 2, D)) for k, (w, m, v) in enumerate(gains)]
    small_params += [(kv_g[None], m_kv_g[None], v_kv_g[None], (12, 1, D)),
                     (forget_b[None], m_forget_b[None], v_forget_b[None], (13, 1, N_HEADS)),
                     (conv_k[0], m_conv_k[0], v_conv_k[0], 0)]
    small_res = adam_small(small_params, total, [d_ck_mine], "adam_small")
    small_keys = [n + "_g" for n in small_names] + ["kv_g", "forget_b", "conv_k"]
    shapes = {"kv_g": kv_g.shape, "forget_b": forget_b.shape, "conv_k": conv_k.shape}
    small = [{key: res[kind].reshape(shapes.get(key, res[kind].shape)) for key, res in zip(small_keys, small_res)}
             for kind in range(4)]
    order = ["ffn1_pre_g", "ffn1_post_g", "ffn1_w_in", "ffn1_w_out", "mix_pre_g", "mix_post_g", "ffn2_pre_g", "ffn2_post_g",
             "ffn2_w_in", "ffn2_w_out", "conv_w_in", "conv_k", "conv_w_out", "kv_g", "kv_w", "forget_b", "attn_w_qg",
             "attn_w_o"]
    results = [loss, dx0.reshape(x.shape)]
    for kind in range(4):
        for nm in order:
            results.append(out[nm][kind] if nm in out else small[kind][nm])
    return tuple(results)
```

```python
import functools
import math

import jax
import jax.numpy as jnp
from jax import lax
from jax.experimental import pallas as pl
from jax.experimental.pallas import tpu as pltpu

F32, BF16 = jnp.float32, jnp.bfloat16
D = 1024
N_HEADS = 16
HEAD_DIM = 64
N_DEV = 8
RMS_EPS = 1e-6
ATT_SCALE = 1.0 / math.sqrt(HEAD_DIM)
LANES = 128
HALO = 8
TM = 512
TQ = 512
VMEM_LIMIT = 48 * 1024 * 1024
MESH = pl.DeviceIdType.MESH

ADAM_LR, ADAM_B1, ADAM_B2, ADAM_EPS, ADAM_WD, ADAM_STEP = 0.001, 0.9, 0.999, 1e-08, 0.01, 10

NT = (((1,), (1,)), ((), ()))
TN = (((0,), (0,)), ((), ()))


def _params(n_axes, vmem_limit=VMEM_LIMIT):
    return pltpu.CompilerParams(dimension_semantics=("arbitrary",) * n_axes, vmem_limit_bytes=vmem_limit)


def _tile(n, cap, mult):
    best = None
    for t in range(mult, min(n, cap) + 1, mult):
        if n % t == 0:
            best = t
    assert best is not None, (n, cap, mult)
    return best


def _rms_rstd(x):
    return lax.rsqrt(jnp.mean(x * x, axis=-1, keepdims=True) + RMS_EPS)


def _rms_fwd(x, g):
    return x * _rms_rstd(x) * g


def _rms_bwd(x, g, dy):
    xh = x * _rms_rstd(x)
    dyg = dy * g
    dx = _rms_rstd(x) * (dyg - xh * jnp.mean(dyg * xh, axis=-1, keepdims=True))
    return dx, jnp.sum(dy * xh, axis=0, keepdims=True)


def _accumulate(ref, first, value):
    @pl.when(first)
    def _():
        ref[...] = value

    @pl.when(jnp.logical_not(first))
    def _():
        ref[...] += value


def rms_proj(x, g, ws, out_dtypes, name, transposed=False):
    T = x.shape[0]
    tm = _tile(T, TM, 16)
    na = len(ws)

    def body(x_ref, g_ref, *refs):
        w_refs, xn_ref, o_refs = refs[:na], refs[na], refs[na + 1:]
        xn = _rms_fwd(x_ref[...], g_ref[...]).astype(BF16)
        xn_ref[...] = xn
        for a, (w, l) in enumerate(ws):
            for p in range(w.shape[0]):
                if transposed:
                    y = lax.dot_general(xn, w_refs[a][p, l], NT, preferred_element_type=F32)
                else:
                    y = jnp.dot(xn, w_refs[a][p, l], preferred_element_type=F32)
                o_refs[a][p] = y.astype(o_refs[a].dtype)

    in_specs = [pl.BlockSpec((tm, D), lambda i: (i, 0)), pl.BlockSpec((1, D), lambda i: (0, 0))]
    in_specs += [pl.BlockSpec(w.shape, lambda i: (0, 0, 0, 0), pipeline_mode=pl.Buffered(1)) for w, _ in ws]
    out_specs = [pl.BlockSpec((tm, D), lambda i: (i, 0))]
    out_shape = [jax.ShapeDtypeStruct((T, D), BF16)]
    for (w, _), dt in zip(ws, out_dtypes):
        nb, wb = w.shape[0], w.shape[2 if transposed else 3]
        out_specs.append(pl.BlockSpec((nb, tm, wb), lambda i: (0, i, 0)))
        out_shape.append(jax.ShapeDtypeStruct((nb, T, wb), dt))
    res = pl.pallas_call(
        body, name=name, grid=(T // tm,), in_specs=in_specs, out_specs=out_specs, out_shape=out_shape,
        compiler_params=_params(1),
    )(x, g, *[w for w, _ in ws])
    return res[0], res[1:]


def mix_out(pro, pro_inputs, pro_specs, w, res, g_post, alpha, nk, name, tm=None, loss_target=None):
    w4, l = w
    T = res.shape[0]
    tm = _tile(T, TM, 16) if tm is None else tm
    dpb = N_DEV // nk
    rows = w4.shape[2]
    kb = dpb * rows
    npi = len(pro_inputs)
    head = loss_target is not None

    def body(*refs):
        pro_refs = refs[:npi]
        w_ref, res_ref, g_ref = refs[npi:npi + 3]
        z_ref, m_ref, y_ref = refs[npi + 3 + head:npi + 6 + head]
        i = pl.program_id(0)
        m = None
        for k in range(nk):
            z = pro(i, k, *pro_refs).astype(BF16)
            z_ref[k] = z
            part = jnp.dot(z, w_ref[k * dpb:(k + 1) * dpb, l].reshape(kb, D), preferred_element_type=F32)
            m = part if m is None else m + part
        m_ref[...] = m
        y = res_ref[...] + alpha * _rms_fwd(m, g_ref[...])
        if head:
            err = y - refs[npi + 3][...]
            y_ref[...] = err * (1.0 / D)
            part = 0.5 * jnp.sum(jnp.mean(err * err, axis=-1, keepdims=True), axis=0, keepdims=True)
            _accumulate(refs[-1], i == 0, jnp.broadcast_to(part, (1, LANES)))
        else:
            y_ref[...] = y

    row = pl.BlockSpec((tm, D), lambda i: (i, 0))
    in_specs = list(pro_specs) + [
        pl.BlockSpec(w4.shape, lambda i: (0, 0, 0, 0), pipeline_mode=pl.Buffered(1)), row, pl.BlockSpec((1, D), lambda i: (0, 0))]
    out_specs = [pl.BlockSpec((nk, tm, kb), lambda i: (0, i, 0)), row, row]
    out_shape = [jax.ShapeDtypeStruct((nk, T, kb), BF16), jax.ShapeDtypeStruct((T, D), F32), jax.ShapeDtypeStruct((T, D), F32)]
    inputs = [*pro_inputs, w4, res, g_post]
    if head:
        in_specs.append(row)
        inputs.append(loss_target)
        out_specs.append(pl.BlockSpec((1, LANES), lambda i: (0, 0)))
        out_shape.append(jax.ShapeDtypeStruct((1, LANES), F32))
    return pl.pallas_call(
        body, name=name, grid=(T // tm,), in_specs=in_specs, out_specs=out_specs, out_shape=out_shape,
        compiler_params=_params(1),
    )(*inputs)


def post_bwd(dy, m, g_post, alpha, w, nk, name, gu=None, tm=None, after=None):
    w4, l = w
    T = dy.shape[0]
    tm = _tile(T, TM, 16) if tm is None else tm
    dpb = N_DEV // nk
    rows = w4.shape[2]
    kb = dpb * rows
    swiglu = gu is not None

    def body(dy_ref, m_ref, g_ref, w_ref, *refs):
        gu_ref = refs[0]
        dm_ref, dg_ref, dz_ref = refs[-3:]
        dgu_ref = dz_ref
        dm, dg = _rms_bwd(m_ref[...], g_ref[...], alpha * dy_ref[...])
        dm = dm.astype(BF16)
        dm_ref[...] = dm
        _accumulate(dg_ref, pl.program_id(0) == 0, dg)
        for k in range(nk):
            dz = lax.dot_general(dm, w_ref[k * dpb:(k + 1) * dpb, l].reshape(kb, D), NT, preferred_element_type=F32)
            if swiglu:
                gate, up = gu_ref[k].astype(F32), gu_ref[k + nk].astype(F32)
                sig = jax.nn.sigmoid(gate)
                dgu_ref[k] = (dz * up * sig * (1.0 + gate * (1.0 - sig))).astype(BF16)
                dgu_ref[k + nk] = (dz * gate * sig).astype(BF16)
            else:
                dz_ref[k] = dz.astype(BF16)

    row = pl.BlockSpec((tm, D), lambda i: (i, 0))
    vec = pl.BlockSpec((1, D), lambda i: (0, 0))
    blkk = pl.BlockSpec((nk, tm, kb), lambda i: (0, i, 0))
    in_specs = [row, row, vec, pl.BlockSpec(w4.shape, lambda i: (0, 0, 0, 0), pipeline_mode=pl.Buffered(1))]
    inputs = [dy, m, g_post, w4]
    n_dz = 2 * nk if swiglu else nk
    out_specs = [row, vec, pl.BlockSpec((n_dz, tm, kb), lambda i: (0, i, 0))]
    out_shape = [jax.ShapeDtypeStruct((T, D), BF16), jax.ShapeDtypeStruct((1, D), F32),
                 jax.ShapeDtypeStruct((n_dz, T, kb), BF16)]
    if swiglu:
        in_specs.append(pl.BlockSpec((2 * nk, tm, kb), lambda i: (0, i, 0)))
        inputs.append(gu)
    if after is not None:
        in_specs.append(ANY)
        inputs.append(after)
    return pl.pallas_call(
        body, name=name, grid=(T // tm,), in_specs=in_specs, out_specs=out_specs, out_shape=out_shape,
        compiler_params=_params(1),
    )(*inputs)


def pre_bwd(pieces, x, g_pre, dres, name, transposed=False, after=None):
    T = x.shape[0]
    tm = _tile(T, TM, 16)
    na = len(pieces)
    weights = []
    for _, (w4, _), _ in pieces:
        if not any(w4 is w for w in weights):
            weights.append(w4)
    nw = len(weights)
    which = [[w4 is w for w in weights].index(True) for _, (w4, _), _ in pieces]

    def body(*refs):
        dz_refs, w_refs = refs[:na], refs[na:na + nw]
        x_ref, g_ref, dres_ref = refs[na + nw:na + nw + 3]
        dx_ref, dg_ref = refs[-2:]
        acc = None
        for a, (dz, (_, l), w_off) in enumerate(pieces):
            for p in range(dz.shape[0]):
                w = w_refs[which[a]][w_off + p, l]
                if transposed:
                    part = jnp.dot(dz_refs[a][p].astype(BF16), w, preferred_element_type=F32)
                else:
                    part = lax.dot_general(dz_refs[a][p].astype(BF16), w, NT, preferred_element_type=F32)
                acc = part if acc is None else acc + part
        dx, dg = _rms_bwd(x_ref[...], g_ref[...], acc)
        dx_ref[...] = dres_ref[...] + dx
        _accumulate(dg_ref, pl.program_id(0) == 0, dg)

    row = pl.BlockSpec((tm, D), lambda i: (i, 0))
    vec = pl.BlockSpec((1, D), lambda i: (0, 0))
    dz_specs = [pl.BlockSpec((dz.shape[0], tm, dz.shape[2]), lambda i: (0, i, 0)) for dz, _, _ in pieces]
    w_specs = [pl.BlockSpec(w.shape, lambda i: (0, 0, 0, 0), pipeline_mode=pl.Buffered(1)) for w in weights]
    extra = [] if after is None else [after]
    return pl.pallas_call(
        body, name=name, grid=(T // tm,), in_specs=dz_specs + w_specs + [row, vec, row] + [ANY] * len(extra),
        out_specs=[row, vec], out_shape=[jax.ShapeDtypeStruct((T, D), F32), jax.ShapeDtypeStruct((1, D), F32)],
        compiler_params=_params(1),
    )(*[p[0] for p in pieces], *weights, x, g_pre, dres, *extra)


def wgrad(a_list, b_list, nout, out4_shape, name, a_cols=None, tm_cap=None, after=None):
    T = a_list[0].shape[1]
    tm = _tile(T, 4 * TM if tm_cap is None else tm_cap, 16)
    nt = T // tm
    dpb = out4_shape[0] // nout
    _, _, R, C = out4_shape
    na, nb = len(a_list), len(b_list)
    a_w = a_list[0].shape[2] if a_cols is None else a_cols
    a_per = a_list[0].shape[2] // a_w

    def spans(arrs, per):
        ns = [a.shape[0] * per for a in arrs]
        return ns, [sum(ns[:k]) for k in range(len(ns))], sum(ns)

    a_ns, a_offs, a_tot = spans(a_list, a_per)
    b_ns, b_offs, b_tot = spans(b_list, 1)
    assert a_tot in (1, nout) and b_tot in (1, nout)

    def body(*refs):
        a_refs, b_refs = refs[:na], refs[na:na + nb]
        out_ref, acc = refs[-2:]
        p, t = pl.program_id(0), pl.program_id(1)
        for ia in range(na):
            for ib in range(nb):
                conds = []
                if a_tot > 1:
                    conds += [p >= a_offs[ia], p < a_offs[ia] + a_ns[ia]]
                if b_tot > 1:
                    conds += [p >= b_offs[ib], p < b_offs[ib] + b_ns[ib]]

                def work(ia=ia, ib=ib):
                    part = lax.dot_general(a_refs[ia][0], b_refs[ib][0], TN, preferred_element_type=F32)
                    _accumulate(acc, t == 0, part)

                if conds:
                    pl.when(functools.reduce(jnp.logical_and, conds))(work)
                else:
                    work()

        @pl.when(t == nt - 1)
        def _():
            out_ref[...] = acc[...].astype(BF16).reshape(dpb, 1, R, C)

    def blk(off, n, tot):
        if tot == 1:
            return lambda p: 0
        return lambda p: jnp.clip(p - off, 0, n - 1)

    in_specs = []
    for arr, n, off in zip(a_list, a_ns, a_offs):
        in_specs.append(pl.BlockSpec((1, tm, a_w), lambda p, t, f=blk(off, n, a_tot): (f(p) // a_per, t, f(p) % a_per)))
    for arr, n, off in zip(b_list, b_ns, b_offs):
        in_specs.append(pl.BlockSpec((1, tm, arr.shape[2]), lambda p, t, f=blk(off, n, b_tot): (f(p), t, 0)))
    extra = [] if after is None else [after]
    return pl.pallas_call(
        body, name=name, grid=(nout, nt), in_specs=in_specs + [ANY] * len(extra),
        out_specs=pl.BlockSpec((dpb, 1, R, C), lambda p, t: (p, 0, 0, 0)),
        out_shape=jax.ShapeDtypeStruct(out4_shape, BF16),
        scratch_shapes=[pltpu.VMEM((dpb * R, C), F32)], compiler_params=_params(2),
    )(*a_list, *b_list, *extra)


def _swiglu_pro(i, k, gu_ref):
    gate, up = gu_ref[k].astype(F32), gu_ref[k + gu_ref.shape[0] // 2].astype(F32)
    return gate * jax.nn.sigmoid(gate) * up


def _attn_gate_pro(i, k, gate_ref, o_ref):
    return jax.nn.sigmoid(gate_ref[0].astype(F32)) * o_ref[...].astype(F32)


def _shift_rows(u, halo, d):
    rolled = pltpu.roll(u, d, 0)
    row = lax.broadcasted_iota(jnp.int32, u.shape, 0)
    for r in range(d):
        rolled = jnp.where(row == r, halo[HALO - d + r:HALO - d + r + 1, :], rolled)
    return rolled


def _advance_rows(u, halo, d):
    n = u.shape[0]
    rolled = pltpu.roll(u, n - d, 0)
    row = lax.broadcasted_iota(jnp.int32, u.shape, 0)
    for r in range(d):
        rolled = jnp.where(row == n - d + r, halo[r:r + 1, :], rolled)
    return rolled


def _make_conv_pro(tiles_per_seq):
    def pro(i, k, b_ref, c_ref, h_ref, ch_ref, hh_ref, ck_ref):
        u = c_ref[0].astype(F32) * h_ref[0].astype(F32)
        first = (i % tiles_per_seq) == 0
        halo = jnp.where(first, 0.0, ch_ref[0].astype(F32) * hh_ref[0].astype(F32))
        ck = ck_ref[...]
        conv = ck[2:3, :] * u + ck[1:2, :] * _shift_rows(u, halo, 1) + ck[0:1, :] * _shift_rows(u, halo, 2)
        return b_ref[0].astype(F32) * conv
    return pro


def conv_bwd_mix(dz, bch, conv_k, seq_len, name):
    T = dz.shape[1]
    tm = _tile(seq_len, TM, 16)
    tps = seq_len // tm
    nt = T // tm
    hb = tm // HALO

    def body(dz_ref, b_ref, c_ref, h_ref, cp_ref, hp_ref, dzn_ref, bn_ref, ck_ref, dbch_ref, dk_ref):
        i = pl.program_id(0)
        first = (i % tps) == 0
        last = (i % tps) == tps - 1
        b, c, h = b_ref[0].astype(F32), c_ref[0].astype(F32), h_ref[0].astype(F32)
        dzt = dz_ref[0].astype(F32)
        u = c * h
        prev = jnp.where(first, 0.0, cp_ref[0].astype(F32) * hp_ref[0].astype(F32))
        u1, u2 = _shift_rows(u, prev, 1), _shift_rows(u, prev, 2)
        ck = ck_ref[...]
        conv = ck[2:3, :] * u + ck[1:2, :] * u1 + ck[0:1, :] * u2
        dconv = dzt * b
        nxt = jnp.where(last, 0.0, dzn_ref[0].astype(F32) * bn_ref[0].astype(F32))
        du = ck[2:3, :] * dconv + ck[1:2, :] * _advance_rows(dconv, nxt, 1) + ck[0:1, :] * _advance_rows(dconv, nxt, 2)
        dbch_ref[0] = (dzt * conv).astype(BF16)
        dbch_ref[1] = (du * h).astype(BF16)
        dbch_ref[2] = (du * c).astype(BF16)
        tap = lax.broadcasted_iota(jnp.int32, (3, D), 0)
        dk = jnp.where(tap == 0, jnp.sum(dconv * u2, axis=0, keepdims=True),
                       jnp.where(tap == 1, jnp.sum(dconv * u1, axis=0, keepdims=True),
                                 jnp.sum(dconv * u, axis=0, keepdims=True)))
        _accumulate(dk_ref, i == 0, dk)

    def piece(p):
        return pl.BlockSpec((1, tm, D), lambda i, p=p: (p, i, 0))

    def prev(p):
        return pl.BlockSpec((1, HALO, D), lambda i, p=p: (p, jnp.maximum(i * hb - 1, 0), 0))

    def nxt(p):
        return pl.BlockSpec((1, HALO, D), lambda i, p=p: (p, jnp.minimum((i + 1) * hb, nt * hb - 1), 0))

    return pl.pallas_call(
        body, name=name, grid=(nt,),
        in_specs=[piece(0), piece(0), piece(1), piece(2), prev(1), prev(2), nxt(0), nxt(0),
                  pl.BlockSpec((3, D), lambda i: (0, 0))],
        out_specs=[pl.BlockSpec((3, tm, D), lambda i: (0, i, 0)), pl.BlockSpec((3, D), lambda i: (0, 0))],
        out_shape=[jax.ShapeDtypeStruct((3, T, D), BF16), jax.ShapeDtypeStruct((3, D), F32)],
        compiler_params=_params(1),
    )(dz, bch, bch, bch, bch, bch, dz, bch, conv_k)


def _log_sigmoid(x):
    return jnp.minimum(x, 0.0) - jnp.log(1.0 + jnp.exp(-jnp.abs(x)))


def forget_fwd(fl, fb, seq_len, name):
    T = fl.shape[1]

    def body(fl_ref, fb_ref, c_ref):
        c = _log_sigmoid(fl_ref[0] + fb_ref[...])
        row = lax.broadcasted_iota(jnp.int32, c.shape, 0)
        k = 1
        while k < seq_len:
            c = c + jnp.where(row >= k, pltpu.roll(c, k, 0), 0.0)
            k *= 2
        c_ref[...] = jnp.concatenate([jnp.broadcast_to(c[:, h:h + 1], (seq_len, HEAD_DIM)) for h in range(N_HEADS)], axis=1)

    return pl.pallas_call(
        body, name=name, grid=(T // seq_len,),
        in_specs=[pl.BlockSpec((1, seq_len, LANES), lambda b: (0, b, 0)), pl.BlockSpec((1, LANES), lambda b: (0, 0))],
        out_specs=pl.BlockSpec((seq_len, D), lambda b: (b, 0)),
        out_shape=jax.ShapeDtypeStruct((T, D), F32), compiler_params=_params(1),
    )(fl, fb)


def forget_bwd(dcb, fl, fb, seq_len, name):
    T = dcb.shape[0]
    pick = jnp.zeros((D, LANES), F32).at[HEAD_DIM * jnp.arange(N_HEADS), jnp.arange(N_HEADS)].set(1.0)

    def body(dc_ref, pick_ref, fl_ref, fb_ref, dfl_ref, dfb_ref):
        b = pl.program_id(0)
        r = jnp.dot(dc_ref[...], pick_ref[...], precision=lax.Precision.HIGHEST, preferred_element_type=F32)
        row = lax.broadcasted_iota(jnp.int32, r.shape, 0)
        k = 1
        while k < seq_len:
            r = r + jnp.where(row < seq_len - k, pltpu.roll(r, seq_len - k, 0), 0.0)
            k *= 2
        dfl = r * jax.nn.sigmoid(-(fl_ref[0] + fb_ref[...]))
        dfl_ref[0] = dfl
        _accumulate(dfb_ref, b == 0, jnp.sum(dfl, axis=0, keepdims=True))

    return pl.pallas_call(
        body, name=name, grid=(T // seq_len,),
        in_specs=[pl.BlockSpec((seq_len, D), lambda b: (b, 0)), pl.BlockSpec((D, LANES), lambda b: (0, 0)),
                  pl.BlockSpec((1, seq_len, LANES), lambda b: (0, b, 0)), pl.BlockSpec((1, LANES), lambda b: (0, 0))],
        out_specs=[pl.BlockSpec((1, seq_len, LANES), lambda b: (0, b, 0)), pl.BlockSpec((1, LANES), lambda b: (0, 0))],
        out_shape=[jax.ShapeDtypeStruct((1, T, LANES), F32), jax.ShapeDtypeStruct((1, LANES), F32)],
        compiler_params=_params(1),
    )(dcb, pick, fl, fb)


HEADS = (slice(0, HEAD_DIM), slice(HEAD_DIM, 2 * HEAD_DIM))


def _key_blocks(i, step, carry):
    carry = lax.fori_loop(0, i // 2, lambda jj, c: step(2 * jj, c, False, 2), carry)
    return lax.cond(i % 2 == 1, lambda c: step(i - 1, c, True, 2), lambda c: step(i, c, True, 1), carry)


def _causal(width, tq):
    keys = lax.broadcasted_iota(jnp.int32, (width * tq, tq), 0)
    return keys <= lax.broadcasted_iota(jnp.int32, (width * tq, tq), 1) + (width - 1) * tq


def attn_fwd(qg, kv, cb, n_seq, seq_len, name):
    T = n_seq * seq_len
    tq = _tile(seq_len, TQ, LANES)
    nq = seq_len // tq

    def body(q_ref, k_ref, v_ref, cb_ref, o_ref, lse_ref):
        i = pl.program_id(2)
        q8 = [q_ref[0, :, sl] * ATT_SCALE for sl in HEADS]

        def block(j, carry, diagonal, width):
            rows = pl.ds(pl.multiple_of(j * tq, tq), width * tq)
            out = []
            for hh, sl in enumerate(HEADS):
                m, l, acc = carry[3 * hh:3 * hh + 3]
                s = lax.dot_general(k_ref[0, rows, sl], q8[hh], NT, preferred_element_type=F32)
                s = s - cb_ref[rows, sl.start:sl.start + 1]
                if diagonal:
                    s = jnp.where(_causal(width, tq), s, -1e30)
                m_new = jnp.maximum(m, jnp.max(s, axis=0, keepdims=True))
                a = jnp.exp(m - m_new)
                p = jnp.exp(s - m_new)
                l = a * l + jnp.sum(p, axis=0, keepdims=True)
                acc = a * acc + lax.dot_general(v_ref[0, rows, sl], p.astype(BF16), TN, preferred_element_type=F32)
                out += [m_new, l, acc]
            return tuple(out)

        init = (jnp.full((1, tq), -1e30, F32), jnp.zeros((1, tq), F32), jnp.zeros((HEAD_DIM, tq), F32)) * 2
        carry = _key_blocks(i, block, init)
        o_ref[...] = jnp.concatenate([carry[2] / carry[1], carry[5] / carry[4]], axis=0).T
        for hh in range(2):
            lse_ref[0, 0, 0, hh:hh + 1, :] = carry[3 * hh] + jnp.log(carry[3 * hh + 1])

    seq2 = pl.BlockSpec((seq_len, LANES), lambda b, hp, i: (b, hp))
    return pl.pallas_call(
        body, name=name, grid=(n_seq, N_HEADS // 2, nq),
        in_specs=[pl.BlockSpec((1, tq, LANES), lambda b, hp, i: (0, b * nq + i, hp)),
                  pl.BlockSpec((1, seq_len, LANES), lambda b, hp, i: (0, b, hp)),
                  pl.BlockSpec((1, seq_len, LANES), lambda b, hp, i: (1, b, hp)), seq2],
        out_specs=[pl.BlockSpec((tq, LANES), lambda b, hp, i: (b * nq + i, hp)),
                   pl.BlockSpec((1, 1, 1, 2, tq), lambda b, hp, i: (b, hp, i, 0, 0))],
        out_shape=[jax.ShapeDtypeStruct((T, D), F32), jax.ShapeDtypeStruct((n_seq, N_HEADS // 2, nq, 2, tq), F32)],
        compiler_params=_params(3),
    )(qg, kv, kv, cb)


def attn_bwd(dz, qg, kv, o, lse, cb, n_seq, seq_len, name):
    T = n_seq * seq_len
    tq = _tile(seq_len, TQ, LANES)
    nq = seq_len // tq

    def body(dz_ref, q_ref, gate_ref, o_ref, lse_ref, cb_ref, k_ref, v_ref,
             dq_ref, dgate_ref, dk_ref, dv_ref, dc_ref, p_s, dp_s, dk_s, dv_s, dc_s):
        i = pl.program_id(2)

        @pl.when(i == 0)
        def _():
            dk_s[...] = jnp.zeros_like(dk_s)
            dv_s[...] = jnp.zeros_like(dv_s)
            dc_s[...] = jnp.zeros_like(dc_s)

        dzf = dz_ref[...].astype(F32)
        sig = jax.nn.sigmoid(gate_ref[0].astype(F32))
        dob = (dzf * sig).astype(BF16)
        dgate_ref[0] = (dzf * o_ref[...] * sig * (1.0 - sig)).astype(BF16)
        q8 = [q_ref[0, :, sl] * ATT_SCALE for sl in HEADS]
        do = [dob[:, sl] for sl in HEADS]
        lse_i = [lse_ref[0, 0, 0, hh:hh + 1, :] for hh in range(2)]

        def probs(j, dsums, diagonal, width):
            rows = pl.ds(pl.multiple_of(j * tq, tq), width * tq)
            out = []
            for hh, sl in enumerate(HEADS):
                s = lax.dot_general(k_ref[0, rows, sl], q8[hh], NT, preferred_element_type=F32)
                p = jnp.exp(s - cb_ref[rows, sl.start:sl.start + 1] - lse_i[hh])
                if diagonal:
                    p = jnp.where(_causal(width, tq), p, 0.0)
                dp = lax.dot_general(v_ref[0, rows, sl], do[hh], NT, preferred_element_type=F32)
                p_s[hh, rows, :] = p
                dp_s[hh, rows, :] = dp
                out.append(dsums[hh] + jnp.sum(p * dp, axis=0, keepdims=True))
            return tuple(out)

        dsums = _key_blocks(i, probs, (jnp.zeros((1, tq), F32),) * 2)

        def grads(j, dqs, diagonal, width):
            rows = pl.ds(pl.multiple_of(j * tq, tq), width * tq)
            out = []
            for hh, sl in enumerate(HEADS):
                p = p_s[hh, rows, :]
                ds = p * (dp_s[hh, rows, :] - dsums[hh])
                dc_s[hh, rows, :] -= jnp.sum(ds, axis=1, keepdims=True)
                dsb = ds.astype(BF16)
                dk_s[rows, sl] += jnp.dot(dsb, q8[hh], preferred_element_type=F32)
                dv_s[rows, sl] += jnp.dot(p.astype(BF16), do[hh], preferred_element_type=F32)
                out.append(dqs[hh] + lax.dot_general(k_ref[0, rows, sl], dsb, TN, preferred_element_type=F32))
            return tuple(out)

        dqs = _key_blocks(i, grads, (jnp.zeros((HEAD_DIM, tq), F32),) * 2)
        dq_ref[0] = (jnp.concatenate(dqs, axis=0).T * ATT_SCALE).astype(BF16)

        @pl.when(i == nq - 1)
        def _():
            dk_ref[0] = dk_s[...].astype(BF16)
            dv_ref[0] = dv_s[...].astype(BF16)
            dc_ref[...] = jnp.zeros_like(dc_ref)
            for hh, sl in enumerate(HEADS):
                dc_ref[:, sl.start:sl.start + 1] = dc_s[hh]

    qry2 = pl.BlockSpec((tq, LANES), lambda b, hp, i: (b * nq + i, hp))
    seq2 = pl.BlockSpec((seq_len, LANES), lambda b, hp, i: (b, hp))

    def qry3(p):
        return pl.BlockSpec((1, tq, LANES), lambda b, hp, i, p=p: (p, b * nq + i, hp))

    def seq3(p):
        return pl.BlockSpec((1, seq_len, LANES), lambda b, hp, i, p=p: (p, b, hp))

    act = jax.ShapeDtypeStruct((1, T, D), BF16)
    return pl.pallas_call(
        body, name=name, grid=(n_seq, N_HEADS // 2, nq),
        in_specs=[qry2, qry3(0), qry3(1), qry2, pl.BlockSpec((1, 1, 1, 2, tq), lambda b, hp, i: (b, hp, i, 0, 0)), seq2,
                  seq3(0), seq3(1)],
        out_specs=[qry3(0), qry3(0), seq3(0), seq3(0), seq2],
        out_shape=[act, act, act, act, jax.ShapeDtypeStruct((T, D), F32)],
        scratch_shapes=[pltpu.VMEM((2, seq_len, tq), F32), pltpu.VMEM((2, seq_len, tq), F32),
                        pltpu.VMEM((seq_len, LANES), F32), pltpu.VMEM((seq_len, LANES), F32),
                        pltpu.VMEM((2, seq_len, 1), F32)],
        compiler_params=_params(3),
    )(dz, qg, qg, o, lse, cb, kv, kv)


def _adamw(w, g, m, v):
    m = ADAM_B1 * m + (1.0 - ADAM_B1) * g
    v = ADAM_B2 * v + (1.0 - ADAM_B2) * (g * g)
    m_hat = m / (1.0 - ADAM_B1 ** ADAM_STEP)
    v_hat = v / (1.0 - ADAM_B2 ** ADAM_STEP)
    delta = -ADAM_LR * (m_hat / (jnp.sqrt(v_hat) + ADAM_EPS) + ADAM_WD * w)
    return delta, m, v


def adam_sharded(w, m, v, contribs, me, name):
    L, R, C = w.shape
    tr = _tile(R, 256, 16) if R % 16 == 0 else R

    def body(me_ref, w_ref, m_ref, v_ref, *refs):
        c_refs, (g_ref, d_ref, nm_ref, nv_ref) = refs[:2 * L], refs[2 * L:]
        l = pl.program_id(0)
        for j in range(L):
            @pl.when(l == j)
            def _(j=j):
                own_ref, recv_ref = c_refs[2 * j], c_refs[2 * j + 1]
                g = own_ref[0].astype(F32)
                for k in range(N_DEV - 1):
                    g = g + recv_ref[k].astype(F32)
                delta, nm, nv = _adamw(w_ref[0], g, m_ref[0], v_ref[0])
                g_ref[0] = g
                d_ref[0] = delta
                nm_ref[0] = nm
                nv_ref[0] = nv

    blk = pl.BlockSpec((1, tr, C), lambda l, i, s: (l, i, 0))
    in_specs = [blk, blk, blk]
    inputs = [w, m, v]
    for j, (mine, recv) in enumerate(contribs):
        in_specs.append(pl.BlockSpec((1, tr, C), lambda l, i, s, j=j: (s[0], jnp.where(l == j, i, 0), 0)))
        in_specs.append(pl.BlockSpec((N_DEV - 1, tr, C), lambda l, i, s, j=j: (0, jnp.where(l == j, i, 0), 0)))
        inputs += [mine, recv]
    shp = jax.ShapeDtypeStruct((L, R, C), F32)
    grid_spec = pltpu.PrefetchScalarGridSpec(num_scalar_prefetch=1, grid=(L, R // tr), in_specs=in_specs, out_specs=[blk] * 4)
    return pl.pallas_call(body, name=name, grid_spec=grid_spec, out_shape=[shp] * 4, compiler_params=_params(2))(me, *inputs)


def cast_place(w, l, me, dtype, after, name):
    _, R, C = w.shape
    tr = _tile(R, 512, 16) if R % 16 == 0 else R

    def body(me_ref, w_ref, after_ref, o_ref):
        o_ref[0] = w_ref[0].astype(dtype)

    grid_spec = pltpu.PrefetchScalarGridSpec(
        num_scalar_prefetch=1, grid=(R // tr,), in_specs=[pl.BlockSpec((1, tr, C), lambda i, s: (l, i, 0)), ANY],
        out_specs=pl.BlockSpec((1, tr, C), lambda i, s: (s[0], i, 0)))
    return pl.pallas_call(body, name=name, grid_spec=grid_spec, out_shape=jax.ShapeDtypeStruct((N_DEV, R, C), dtype),
                          compiler_params=_params(1))(me, w, after)


def adam_small(params, total, extra_grads, name):
    n, ne = len(params), len(extra_grads)

    def body(*refs):
        total_ref, extra_refs = refs[0], refs[1:1 + ne]
        ins, outs = refs[1 + ne:1 + ne + 3 * n], refs[1 + ne + 3 * n:]
        for k, (_, _, _, where) in enumerate(params):
            if isinstance(where, int):
                g = extra_refs[where][...]
            else:
                row, rows, width = where
                g = total_ref[row:row + rows, 0:width]
            delta, nm, nv = _adamw(ins[3 * k][...], g, ins[3 * k + 1][...], ins[3 * k + 2][...])
            outs[4 * k][...] = g
            outs[4 * k + 1][...] = delta
            outs[4 * k + 2][...] = nm
            outs[4 * k + 3][...] = nv

    flat = [a for w, m, v, _ in params for a in (w, m, v)]
    out_shape = [jax.ShapeDtypeStruct(w.shape, F32) for w, _, _, _ in params for _ in range(4)]
    res = pl.pallas_call(body, name=name, out_shape=out_shape)(total, *extra_grads, *flat)
    return [res[4 * k:4 * k + 4] for k in range(n)]


def _place():
    return lax.axis_index("x"), lax.axis_index("y"), lax.axis_index("c")


def _peer(place, k):
    x, y, c = place
    return x ^ (k >> 2), y ^ ((k >> 1) & 1), c ^ (k & 1)


ANY = pl.BlockSpec(memory_space=pl.ANY)
HBM = pl.BlockSpec(memory_space=pltpu.HBM)
SEM = pl.BlockSpec(memory_space=pltpu.SEMAPHORE)
EFFECT = pltpu.SideEffectType.DATAFLOW_SIDE_EFFECTING


def _in_hbm(a):
    return pltpu.with_memory_space_constraint(a, pltpu.HBM)


def _number(place):
    return 4 * place[0] + 2 * place[1] + place[2]


SLOTS = {"gather_like": 4, "gather_pass": 3, "scatter": 7}


def _plan(mode, src_ref, land_ref, place):
    if mode == "gather_like":
        mine = land_ref.at[_number(place)]
        return [(mine, mine, _peer(place, k)) for k in (1, 2, 4, 6)]
    if mode == "gather_pass":
        slots = [land_ref.at[_number(_peer(place, k))] for k in (2, 4, 6)]
        return [(slot, slot, _peer(place, 1)) for slot in slots]
    return [(src_ref.at[_number(_peer(place, k))], land_ref.at[k - 1], _peer(place, k)) for k in range(1, N_DEV)]


def _exchange(name, groups, start, afters):
    sizes = [len(g[3]) for g in groups]
    na, ng = sum(sizes), len(groups)
    ns = sum(len(g[2]) for g in groups)
    waits = groups[0][0] is not None
    n_in_sems = 2 * ng if waits else 0
    n_out_sems = 2 * ng if start else 0

    def body(*refs):
        src_refs, land_refs = (refs[:ns] if ns else [None] * na), refs[ns:ns + na]
        in_sems = refs[ns + na:ns + na + n_in_sems]
        outs = refs[ns + na + n_in_sems + len(afters):]
        place = _place()
        a = 0
        for gi, n in enumerate(sizes):
            for idx in range(n):
                if waits:
                    zone = land_refs[a].at[pl.ds(0, groups[gi][4])]
                    copy = pltpu.make_async_remote_copy(
                        src_ref=zone, dst_ref=zone, send_sem=in_sems[2 * gi].at[idx], recv_sem=in_sems[2 * gi + 1].at[idx],
                        device_id=_peer(place, 1), device_id_type=MESH)
                    copy.wait_send()
                    copy.wait_recv()
                if start:
                    for src, dst, peer in _plan(start, src_refs[a], land_refs[a], place):
                        pltpu.make_async_remote_copy(
                            src_ref=src, dst_ref=dst, send_sem=outs[2 * gi].at[idx], recv_sem=outs[2 * gi + 1].at[idx],
                            device_id=peer, device_id_type=MESH).start()
                a += 1
        if start:
            outs[-1][...] = jnp.zeros_like(outs[-1])

    srcs = [_in_hbm(s) for g in groups for s in g[2]]
    lands = [_in_hbm(l) for g in groups for l in g[3]]
    sems = [s for g in groups for s in g[:2]] if waits else []
    out_shape = [pltpu.SemaphoreType.DMA((n,)) for n in sizes for _ in range(2)] if start else []
    out_shape += [pltpu.HBM(a.shape, a.dtype) for a in srcs + lands]
    out_specs = [SEM] * n_out_sems + [HBM] * (ns + na)
    if start:
        out_shape.append(jax.ShapeDtypeStruct((8, LANES), F32))
        out_specs.append(pl.BlockSpec(memory_space=pltpu.VMEM))
    res = pl.pallas_call(
        body, name=name, in_specs=[HBM] * (ns + na) + [SEM] * n_in_sems + [ANY] * len(afters),
        out_shape=out_shape, out_specs=out_specs,
        input_output_aliases={i: n_out_sems + i for i in range(ns + na)},
        compiler_params=pltpu.CompilerParams(has_side_effects=EFFECT),
    )(*srcs, *lands, *sems, *afters)
    new_sems, thru = res[:n_out_sems], res[n_out_sems:n_out_sems + ns + na]
    out, a = [], 0
    for gi, n in enumerate(sizes):
        pair = (new_sems[2 * gi], new_sems[2 * gi + 1]) if start else (None, None)
        out.append(pair + (thru[a:a + n] if ns else [], thru[ns + a:ns + a + n], SLOTS.get(start, 0)))
        a += n
    return out, (res[-1] if start else None)


def exchange_start(pair_groups, mode, name):
    groups = [(None, None, [s for s, _ in g if s is not None], [l for _, l in g], 0) for g in pair_groups]
    return _exchange(name, groups, mode, ())


def exchange_relay(groups, mode, afters, name):
    return _exchange(name, groups, mode, afters)


def exchange_wait(groups, afters, name):
    done, _ = _exchange(name, groups, None, afters)
    return [(g[2], g[3]) for g in done]


def all_reduce_small(parts, n_rows, after, name):
    R = n_rows
    n_parts = len(parts)

    def body(*refs):
        part_refs = refs[:n_parts]
        out_ref, buf, send_sems, recv_sems = refs[n_parts + 1:]
        x, y, c = _place()
        me = 4 * x + 2 * y + c
        own = buf.at[me]
        own[...] = jnp.zeros((R, D), F32)
        for ref, (arr, row) in zip(part_refs, parts):
            own[row:row + arr.shape[0], 0:arr.shape[1]] = ref[...]
        copies = []
        for k in range(1, N_DEV):
            peer = (x ^ (k >> 2), y ^ ((k >> 1) & 1), c ^ (k & 1))
            copies.append(pltpu.make_async_remote_copy(
                src_ref=own, dst_ref=own, send_sem=send_sems.at[k - 1], recv_sem=recv_sems.at[k - 1],
                device_id=peer, device_id_type=MESH))
        for cp in copies:
            cp.start()
        for cp in copies:
            cp.wait()
        total = buf[0]
        for d in range(1, N_DEV):
            total = total + buf[d]
        out_ref[...] = total

    vm = pl.BlockSpec(memory_space=pltpu.VMEM)
    return pl.pallas_call(
        body, name=name, in_specs=[vm] * n_parts + [ANY], out_specs=vm, out_shape=jax.ShapeDtypeStruct((R, D), F32),
        scratch_shapes=[pltpu.VMEM((N_DEV, R, D), F32), pltpu.SemaphoreType.DMA((N_DEV - 1,)),
                        pltpu.SemaphoreType.DMA((N_DEV - 1,))],
    )(*[arr for arr, _ in parts], after)


def _col_blocks(gathered, n_blocks):
    n, d, w = gathered.shape
    whole = gathered.transpose(1, 0, 2).reshape(d, n * w)
    return whole.reshape(d, n_blocks, n * w // n_blocks).transpose(1, 0, 2)[:, None]


def _col_shards(blocks):
    n, d, w = blocks.shape
    whole = blocks.transpose(1, 0, 2).reshape(d, n * w)
    return whole.reshape(d, N_DEV, n * w // N_DEV).transpose(1, 0, 2)


def kernel(x, ffn1_pre_g, ffn1_post_g, ffn1_w_in, ffn1_w_out, mix_pre_g, mix_post_g, ffn2_pre_g, ffn2_post_g, ffn2_w_in, ffn2_w_out, conv_w_in, conv_k, conv_w_out, kv_g, kv_w, forget_b, attn_w_qg, attn_w_o, loss_target, m_ffn1_pre_g, m_ffn1_post_g, m_ffn1_w_in, m_ffn1_w_out, m_mix_pre_g, m_mix_post_g, m_ffn2_pre_g, m_ffn2_post_g, m_ffn2_w_in, m_ffn2_w_out, m_conv_w_in, m_conv_k, m_conv_w_out, m_kv_g, m_kv_w, m_forget_b, m_attn_w_qg, m_attn_w_o, v_ffn1_pre_g, v_ffn1_post_g, v_ffn1_w_in, v_ffn1_w_out, v_mix_pre_g, v_mix_post_g, v_ffn2_pre_g, v_ffn2_post_g, v_ffn2_w_in, v_ffn2_w_out, v_conv_w_in, v_conv_k, v_conv_w_out, v_kv_g, v_kv_w, v_forget_b, v_attn_w_qg, v_attn_w_o):
    n_seq, seq_len, _ = x.shape
    T = n_seq * seq_len
    xi, yi, ci = _place()
    dev = 4 * xi + 2 * yi + ci
    x0 = x.reshape(T, D)
    target = loss_target.reshape(T, D)

    me = dev.reshape(1).astype(jnp.int32)
    w1_t, w2_t, kv_t = ffn1_w_in.transpose(0, 2, 1), ffn2_w_in.transpose(0, 2, 1), kv_w.T[None]

    def zones(specs, after):
        return [(None, cast_place(w, l, me, dt, after, f"place_{nm}")) for nm, w, l, dt in specs]

    gathers, token = exchange_start([zones([("w1_in0", w1_t, 0, BF16)], me)], "gather_like", "gather_start0")
    shard_groups = [
        [("w1_out0", ffn1_w_out, 0, BF16)],
        [("cw_in", conv_w_in, 0, BF16), ("cw_out", conv_w_out, 0, BF16), ("ck", conv_k, 0, F32)],
        [("w2_in0", w2_t, 0, BF16), ("w2_out0", ffn2_w_out, 0, BF16)],
        [("kv", kv_t, 0, BF16), ("w1_in1", w1_t, 1, BF16), ("w1_out1", ffn1_w_out, 1, BF16),
         ("qg", attn_w_qg, 0, BF16), ("ow", attn_w_o, 0, BF16)],
        [("w2_in1", w2_t, 1, BF16), ("w2_out1", ffn2_w_out, 1, BF16)]]
    later, token = exchange_start([zones(g, token) for g in shard_groups], "gather_like", "gather_start1")
    gathers = gathers + later

    def gathered(k, after):
        passed, _ = exchange_relay([gathers[k]], "gather_pass", [after], f"gather_pass{k}")
        return [z[:, None] for z in exchange_wait(passed, [after], f"gather_wait{k}")[0][1]]

    fb = jnp.pad(forget_b, (0, LANES - N_HEADS))[None]

    def vec(g, l):
        return g[l:l + 1]

    grads_small = {}
    tm_ffn = _tile(T, TM, 16)

    def ffn_fwd(xin, g_pre, g_post, w_in, w_out, tag, loss_target=None):
        w_gu = w_in.reshape(2, 1, -1, D)
        xn, (gu,) = rms_proj(xin, g_pre, [(w_gu, 0)], [BF16], f"{tag}_in", transposed=True)
        specs = [pl.BlockSpec((2, tm_ffn, gu.shape[2]), lambda i: (0, i, 0))]
        if callable(w_out):
            w_out = w_out(xn)
        a, h, *y = mix_out(_swiglu_pro, [gu], specs, (w_out, 0), xin, g_post, 0.5, 1, f"{tag}_out", tm=tm_ffn,
                           loss_target=loss_target)
        return (y[0] if loss_target is None else tuple(y)), (xin, xn, gu, a, h), w_out

    def ffn_bwd(dy, saved, g_pre, g_post, w_in, w_out, tag, after=None):
        xin, xn, gu, a, h = saved
        w_gu = w_in.reshape(2, 1, -1, D)
        half = gu.shape[2] // 2
        dh, dg_post, dgu = post_bwd(dy, h, g_post, 0.5, (w_out, 0), 1, f"{tag}_bwd_out", gu=gu, tm=tm_ffn, after=after)
        dw_out = wgrad([a], [dh[None]], 2, (2, 1, half, D), f"{tag}_dw_out", a_cols=half)
        started_out, tok = scatter_start([dw_out.reshape(8, -1, D)], f"{tag}_out_scatter_start")
        dw_in = wgrad([dgu], [xn[None]], 4, (4, 1, half, D), f"{tag}_dw_in", a_cols=half, after=tok)
        started_in, tok = scatter_start([dw_in.reshape(8, -1, D)], f"{tag}_in_scatter_start")
        dx, dg_pre = pre_bwd([(dgu, (w_gu, 0), 0)], xin, g_pre, dy, f"{tag}_bwd_in", transposed=True, after=tok)
        return dx, dg_pre, dg_post, [started_in, started_out]

    def scatter_start(blocked, name):
        pairs = [(g, lax.empty((N_DEV - 1,) + g.shape[1:], g.dtype)) for g in blocked]
        started, tok = exchange_start([pairs], "scatter", name)
        return started[0], tok

    (w1_in,) = gathered(0, token)
    x1, s_f1a, w1_out = ffn_fwd(x0, vec(ffn1_pre_g, 0), vec(ffn1_post_g, 0), w1_in, lambda after: gathered(1, after)[0],
                                "l0_ffn1")
    cw_in_g, cw_out, ck_g = gathered(2, x1)
    cw_in = _col_blocks(cw_in_g[:, 0], 3)
    ck = ck_g[:, 0].transpose(1, 0, 2).reshape(3, D)
    xn_c, (bch,) = rms_proj(x1, vec(mix_pre_g, 0), [(cw_in, 0)], [BF16], "conv_in")
    tmc = _tile(seq_len, TM, 16)
    hb = tmc // HALO

    def cpiece(p):
        return pl.BlockSpec((1, tmc, D), lambda i, p=p: (p, i, 0))

    def chalo(p):
        return pl.BlockSpec((1, HALO, D), lambda i, p=p: (p, jnp.maximum(i * hb - 1, 0), 0))

    conv_specs = [cpiece(0), cpiece(1), cpiece(2), chalo(1), chalo(2), pl.BlockSpec((3, D), lambda i: (0, 0))]
    z_c, m_c, x2 = mix_out(_make_conv_pro(seq_len // tmc), [bch, bch, bch, bch, bch, ck], conv_specs, (cw_out, 0),
                           x1, vec(mix_post_g, 0), 1.0, 1, "conv_out", tm=tmc)
    w2_in, w2_out = gathered(3, x2)
    x3, s_f2a, _ = ffn_fwd(x2, vec(ffn2_pre_g, 0), vec(ffn2_post_g, 0), w2_in, w2_out, "l0_ffn2")

    kvw_g, w1_in_b, w1_out_b, qgw_g, ow = gathered(4, x3)
    qg_w = _col_blocks(qgw_g[:, 0], 2)
    kv_whole = kvw_g.reshape(2 * D + N_HEADS, D).T
    kv_wb = kv_whole[:, :2 * D].reshape(D, 2, D).transpose(1, 0, 2)[:, None]
    f_w = jnp.pad(kv_whole[:, 2 * D:], ((0, 0), (0, LANES - N_HEADS)))[None, None]
    xn_kv, (kv, fl) = rms_proj(x3, kv_g[None], [(kv_wb, 0), (f_w, 0)], [BF16, F32], "kv_in")
    cb = forget_fwd(fl, fb, seq_len, "forget_fwd")

    x4, s_f1b, _ = ffn_fwd(x3, vec(ffn1_pre_g, 1), vec(ffn1_post_g, 1), w1_in_b, w1_out_b, "l1_ffn1")
    xn_a, (qg,) = rms_proj(x4, vec(mix_pre_g, 1), [(qg_w, 0)], [BF16], "attn_in")
    o, lse = attn_fwd(qg, kv, cb, n_seq, seq_len, "attn_fwd")
    tm = _tile(T, TM, 16)
    gate_specs = [pl.BlockSpec((1, tm, D), lambda i: (1, i, 0)), pl.BlockSpec((tm, D), lambda i: (i, 0))]
    z_a, m_a, x5 = mix_out(_attn_gate_pro, [qg, o], gate_specs, (ow, 0), x4, vec(mix_post_g, 1), 1.0, 1, "attn_out")
    w2_in_b, w2_out_b = gathered(5, x5)
    (dy, loss_part), s_f2b, _ = ffn_fwd(x5, vec(ffn2_pre_g, 1), vec(ffn2_post_g, 1), w2_in_b, w2_out_b, "l1_ffn2",
                                        loss_target=target)

    scatters = {}
    dx5, dg, dgp, scatters["ffn2", 1] = ffn_bwd(dy, s_f2b, vec(ffn2_pre_g, 1), vec(ffn2_post_g, 1), w2_in_b, w2_out_b, "l1_ffn2")
    grads_small["ffn2_pre", 1], grads_small["ffn2_post", 1] = dg, dgp
    dm_a, dgp, dz_a = post_bwd(dx5, m_a, vec(mix_post_g, 1), 1.0, (ow, 0), 1, "attn_bwd_out")
    grads_small["mix_post", 1] = dgp
    dq, dgate, dk, dv, dcb = attn_bwd(dz_a[0], qg, kv, o, lse, cb, n_seq, seq_len, "attn_bwd")
    dx4, dg = pre_bwd([(dq, (qg_w, 0), 0), (dgate, (qg_w, 0), 1)], x4, vec(mix_pre_g, 1), dx5, "attn_bwd_in")
    grads_small["mix_pre", 1] = dg
    d_ow = wgrad([z_a], [dm_a[None]], 1, ow.shape, "attn_dw_o")
    d_qgw = wgrad([xn_a[None]], [dq, dgate], 2, (2, 1, D, D), "attn_dw_qg")
    scatters["attn"], tok = scatter_start([_col_shards(d_qgw[:, 0]), d_ow.reshape(8, -1, D)], "attn_scatter_start")
    dx3, dg, dgp, scatters["ffn1", 1] = ffn_bwd(dx4, s_f1b, vec(ffn1_pre_g, 1), vec(ffn1_post_g, 1),
                                                w1_in_b, w1_out_b, "l1_ffn1", after=tok)
    grads_small["ffn1_pre", 1], grads_small["ffn1_post", 1] = dg, dgp

    dfl, dfb = forget_bwd(dcb, fl, fb, seq_len, "forget_bwd")
    dx3, dg_kv = pre_bwd([(dk, (kv_wb, 0), 0), (dv, (kv_wb, 0), 1), (dfl, (f_w, 0), 0)], x3, kv_g[None], dx3, "kv_bwd_in")
    d_kvw = wgrad([xn_kv[None]], [dk, dv], 2, (2, 1, D, D), "kv_dw")
    d_fw = wgrad([xn_kv[None]], [dfl.astype(BF16)], 1, (1, 1, D, LANES), "forget_dw")
    d_kv_whole = jnp.concatenate([d_kvw[0, 0], d_kvw[1, 0], d_fw[0, 0, :, :N_HEADS]], axis=1)
    wshard = D * 2 + N_HEADS
    scatters["kv"], tok = scatter_start([d_kv_whole.T.reshape(N_DEV, wshard // N_DEV, D)], "kv_scatter_start")

    dx2, dg, dgp, scatters["ffn2", 0] = ffn_bwd(dx3, s_f2a, vec(ffn2_pre_g, 0), vec(ffn2_post_g, 0),
                                                w2_in, w2_out, "l0_ffn2", after=tok)
    grads_small["ffn2_pre", 0], grads_small["ffn2_post", 0] = dg, dgp
    dm_c, dgp, dz_c = post_bwd(dx2, m_c, vec(mix_post_g, 0), 1.0, (cw_out, 0), 1, "conv_bwd_out")
    grads_small["mix_post", 0] = dgp
    dbch, d_ck = conv_bwd_mix(dz_c, bch, ck, seq_len, "conv_bwd_mix")
    dx1, dg = pre_bwd([(dbch, (cw_in, 0), 0)], x1, vec(mix_pre_g, 0), dx2, "conv_bwd_in")
    grads_small["mix_pre", 0] = dg
    d_cw_out = wgrad([z_c], [dm_c[None]], 1, cw_out.shape, "conv_dw_out")
    d_cw_in = wgrad([xn_c[None]], [dbch], 3, (3, 1, D, D), "conv_dw_in")
    scatters["conv"], tok = scatter_start([_col_shards(d_cw_in[:, 0]), d_cw_out.reshape(8, -1, D)], "conv_scatter_start")
    dx0, dg, dgp, scatters["ffn1", 0] = ffn_bwd(dx1, s_f1a, vec(ffn1_pre_g, 0), vec(ffn1_post_g, 0),
                                                w1_in, w1_out, "l0_ffn1", after=tok)
    grads_small["ffn1_pre", 0], grads_small["ffn1_post", 0] = dg, dgp

    parts_of = {}

    def scatter_end(keys, afters, name):
        flat = [(k, g) for k in keys for g in (scatters[k] if isinstance(scatters[k], list) else [scatters[k]])]
        for (k, _), (sent, recv) in zip(flat, exchange_wait([g for _, g in flat], afters, name)):
            parts_of.setdefault(k, []).extend(zip(sent, recv))

    scatter_end([("ffn2", 1), "attn", ("ffn1", 1), "kv", ("ffn2", 0), "conv"], [dx0], "scatter_wait")
    sharded = {"ffn2_w_in": (ffn2_w_in, m_ffn2_w_in, v_ffn2_w_in), "ffn2_w_out": (ffn2_w_out, m_ffn2_w_out, v_ffn2_w_out),
               "conv_w_in": (conv_w_in, m_conv_w_in, v_conv_w_in), "conv_w_out": (conv_w_out, m_conv_w_out, v_conv_w_out),
               "kv_w": (kv_w, m_kv_w, v_kv_w), "attn_w_qg": (attn_w_qg, m_attn_w_qg, v_attn_w_qg),
               "attn_w_o": (attn_w_o, m_attn_w_o, v_attn_w_o),
               "ffn1_w_in": (ffn1_w_in, m_ffn1_w_in, v_ffn1_w_in), "ffn1_w_out": (ffn1_w_out, m_ffn1_w_out, v_ffn1_w_out)}
    out = {}
    for nm, (w, mm, vv) in sharded.items():
        if nm == "ffn1_w_in":
            scatter_end([("ffn1", 0)], [res[0] for res in out.values()], "scatter_wait_last")
        contribs = {
            "ffn1_w_in": lambda: [parts_of["ffn1", 0][0], parts_of["ffn1", 1][0]],
            "ffn1_w_out": lambda: [parts_of["ffn1", 0][1], parts_of["ffn1", 1][1]],
            "ffn2_w_in": lambda: [parts_of["ffn2", 0][0], parts_of["ffn2", 1][0]],
            "ffn2_w_out": lambda: [parts_of["ffn2", 0][1], parts_of["ffn2", 1][1]],
            "conv_w_in": lambda: [parts_of["conv"][0]], "conv_w_out": lambda: [parts_of["conv"][1]],
            "kv_w": lambda: [parts_of["kv"][0]], "attn_w_qg": lambda: [parts_of["attn"][0]],
            "attn_w_o": lambda: [parts_of["attn"][1]]}[nm]()
        if nm in ("ffn1_w_in", "ffn2_w_in", "kv_w"):
            rows, cols = w.shape[-2:]

            def view(a):
                return a.reshape(-1, rows, cols).transpose(0, 2, 1)

            res = adam_sharded(view(w), view(mm), view(vv), contribs, me, f"adam_{nm}")
            out[nm] = [r.transpose(0, 2, 1).reshape(w.shape) for r in res]
        else:
            shape3 = (len(contribs),) + contribs[0][0].shape[1:]
            res = adam_sharded(w.reshape(shape3), mm.reshape(shape3), vv.reshape(shape3), contribs, me, f"adam_{nm}")
            out[nm] = [r.reshape(w.shape) for r in res]

    small_names = ["ffn1_pre", "ffn1_post", "mix_pre", "mix_post", "ffn2_pre", "ffn2_post"]
    parts = [(grads_small[n, l], 2 * k + l) for k, n in enumerate(small_names) for l in range(2)]
    parts += [(dg_kv, 12), (dfb, 13), (d_ck, 14), (loss_part, 17)]
    total = all_reduce_small(parts, 24, out["ffn1_w_in"][0], "all_reduce_small")
    loss = total[17, 0]
    d_ck_mine = lax.dynamic_slice(total, (14, dev * LANES), (3, LANES))
    gains = [(ffn1_pre_g, m_ffn1_pre_g, v_ffn1_pre_g), (ffn1_post_g, m_ffn1_post_g, v_ffn1_post_g),
             (mix_pre_g, m_mix_pre_g, v_mix_pre_g), (mix_post_g, m_mix_post_g, v_mix_post_g),
             (ffn2_pre_g, m_ffn2_pre_g, v_ffn2_pre_g), (ffn2_post_g, m_ffn2_post_g, v_ffn2_post_g)]
    small_params = [(w, m, v, (2 * k, 2, D)) for k, (w, m, v) in enumerate(gains)]
    small_params += [(kv_g[None], m_kv_g[None], v_kv_g[None], (12, 1, D)),
                     (forget_b[None], m_forget_b[None], v_forget_b[None], (13, 1, N_HEADS)),
                     (conv_k[0], m_conv_k[0], v_conv_k[0], 0)]
    small_res = adam_small(small_params, total, [d_ck_mine], "adam_small")
    small_keys = [n + "_g" for n in small_names] + ["kv_g", "forget_b", "conv_k"]
    shapes = {"kv_g": kv_g.shape, "forget_b": forget_b.shape, "conv_k": conv_k.shape}
    small = [{key: res[kind].reshape(shapes.get(key, res[kind].shape)) for key, res in zip(small_keys, small_res)}
             for kind in range(4)]
    order = ["ffn1_pre_g", "ffn1_post_g", "ffn1_w_in", "ffn1_w_out", "mix_pre_g", "mix_post_g", "ffn2_pre_g", "ffn2_post_g",
             "ffn2_w_in", "ffn2_w_out", "conv_w_in", "conv_k", "conv_w_out", "kv_g", "kv_w", "forget_b", "attn_w_qg",
             "attn_w_o"]
    results = [loss, dx0.reshape(x.shape)]
    for kind in range(4):
        for nm in order:
            results.append(out[nm][kind] if nm in out else small[kind][nm])
    return tuple(results)
```
